```python
import jax, jax.numpy as jnp
from jax import lax
import numpy as np

D_MODEL = 1024
BATCH = 32
SEQ = 2048
DEPTH = 1

D_PLE = 256
D_MIX = 2 * D_MODEL
GM_WIDTH = D_MIX // 2
GM_HEADS = 8
GM_HEAD_DIM = GM_WIDTH // GM_HEADS
GM_CHUNK = 128
SSD_WIDTH = D_MIX - GM_WIDTH
SSD_HEAD_DIM = 64
SSD_HEADS = SSD_WIDTH // SSD_HEAD_DIM
SSD_GROUPS = 2
SSD_HEADS_PER_GROUP = SSD_HEADS // SSD_GROUPS
SSD_STATE = 128
SSD_CONV = 4
SSD_CHUNK = 128
D_FF = 4 * D_MODEL
EPS = 1e-6
GM_COLS = 2 * GM_WIDTH
SSD_CONV_CH = SSD_WIDTH + 2 * SSD_GROUPS * SSD_STATE
SSD_COLS = SSD_WIDTH + SSD_CONV_CH + SSD_HEADS
D_IN_PROJ = GM_COLS + SSD_COLS

kernel_name = "hybrid_gmlp_ssd_parallel_heads"


def rmsnorm(x, g):
    xf = x.astype(jnp.float32)
    y = xf * lax.rsqrt(jnp.mean(xf * xf, axis=-1, keepdims=True) + EPS)
    return (y * g.astype(jnp.float32)).astype(x.dtype)


def gmlp_chunk_mixer(uv, v_norm_g, ws, bs, out_norm_g):
    b, s, _ = uv.shape
    nc = s // GM_CHUNK
    uv = jax.nn.gelu(uv, approximate=False)
    u = uv[..., :GM_WIDTH].reshape(b, nc, GM_CHUNK, GM_HEADS, GM_HEAD_DIM)
    v = uv[..., GM_WIDTH:].reshape(b, s, GM_HEADS, GM_HEAD_DIM)
    v = rmsnorm(v, v_norm_g.reshape(GM_HEADS, GM_HEAD_DIM))
    v = v.reshape(b, nc, GM_CHUNK, GM_HEADS, GM_HEAD_DIM)
    causal = jnp.tril(jnp.ones((GM_CHUNK, GM_CHUNK), dtype=bool))
    w = jnp.where(causal[None], ws, jnp.zeros_like(ws))
    mixed = jnp.einsum('hts,bcshp->bcthp', w, v) + bs.T[None, None, :, :, None]
    y = (u * mixed).reshape(b, s, GM_WIDTH)
    return rmsnorm(y, out_norm_g)


def causal_depthwise_conv(x, w, bias):
    out = lax.conv_general_dilated(
        x, w[:, None, :], window_strides=(1,), padding=[(SSD_CONV - 1, 0)],
        dimension_numbers=('NWC', 'WIO', 'NWC'), feature_group_count=x.shape[-1])
    return out + bias


def ssd_chunked_scan(x, dt, A, B, C):
    b, s, G, R, P = x.shape
    N = B.shape[-1]
    nc, L = s // SSD_CHUNK, SSD_CHUNK
    x = x.reshape(b, nc, L, G, R, P)
    dt = dt.reshape(b, nc, L, G, R)
    B = B.reshape(b, nc, L, G, N)
    C = C.reshape(b, nc, L, G, N)
    cs = jnp.cumsum(dt * A, axis=2)
    x_dt = x * dt[..., None].astype(x.dtype)
    seg = cs[:, :, :, None] - cs[:, :, None, :]
    causal = jnp.tril(jnp.ones((L, L), dtype=bool))[:, :, None, None]
    lmat = jnp.exp(jnp.where(causal, seg, -jnp.inf)).astype(x.dtype)
    cb = jnp.einsum('bclgn,bcsgn->bclsg', C, B)
    y_diag = jnp.einsum('bclsgr,bcsgrp->bclgrp', cb[..., None] * lmat, x_dt)
    decay_states = jnp.exp(cs[:, :, -1:] - cs).astype(x.dtype)
    states = jnp.einsum('bclgn,bclgr,bclgrp->bcgrpn', B, decay_states, x_dt)
    chunk_decay = jnp.exp(cs[:, :, -1]).astype(x.dtype)

    def step(carry, inp):
        st, dec = inp
        return carry * dec[..., None, None] + st, carry

    init = jnp.zeros((b, G, R, P, N), dtype=states.dtype)
    _, prev = lax.scan(step, init, (jnp.moveaxis(states, 1, 0), jnp.moveaxis(chunk_decay, 1, 0)))
    prev = jnp.moveaxis(prev, 0, 1)
    y_off = jnp.einsum('bclgn,bcgrpn,bclgr->bclgrp', C, prev, jnp.exp(cs).astype(x.dtype))
    return (y_diag + y_off).reshape(b, s, G, R, P)


def ssd_mixer(zxbcdt, conv_w, conv_b, dt_bias, a_log, d_skip, norm_g):
    b, s, _ = zxbcdt.shape
    z = zxbcdt[..., :SSD_WIDTH]
    xbc = zxbcdt[..., SSD_WIDTH:SSD_WIDTH + SSD_CONV_CH]
    dt_raw = zxbcdt[..., SSD_WIDTH + SSD_CONV_CH:]
    xbc = jax.nn.silu(causal_depthwise_conv(xbc, conv_w, conv_b))
    xs = xbc[..., :SSD_WIDTH].reshape(b, s, SSD_GROUPS, SSD_HEADS_PER_GROUP, SSD_HEAD_DIM)
    Bm = xbc[..., SSD_WIDTH:SSD_WIDTH + SSD_GROUPS * SSD_STATE].reshape(b, s, SSD_GROUPS, SSD_STATE)
    Cm = xbc[..., SSD_WIDTH + SSD_GROUPS * SSD_STATE:].reshape(b, s, SSD_GROUPS, SSD_STATE)
    dt = jax.nn.softplus(dt_raw.astype(jnp.float32) + dt_bias.astype(jnp.float32))
    dt = dt.reshape(b, s, SSD_GROUPS, SSD_HEADS_PER_GROUP)
    A = -jnp.exp(a_log.astype(jnp.float32)).reshape(SSD_GROUPS, SSD_HEADS_PER_GROUP)
    y = ssd_chunked_scan(xs, dt, A, Bm, Cm)
    y = y + d_skip.reshape(SSD_GROUPS, SSD_HEADS_PER_GROUP)[:, :, None] * xs
    y = y.reshape(b, s, SSD_WIDTH) * jax.nn.silu(z)
    gw = SSD_WIDTH // SSD_GROUPS
    y = rmsnorm(y.reshape(b, s, SSD_GROUPS, gw), norm_g.reshape(SSD_GROUPS, gw))
    return y.reshape(b, s, SSD_WIDTH)


def _fwd_setup_inputs(seed: int = 0) -> dict:
    key = jax.random.key(seed)
    ks = jax.random.split(key, 24)
    f32 = jnp.float32
    nrm = lambda k, shape, scale: jax.random.normal(k, shape, f32) * scale
    gain = lambda k, shape: 1.0 + 0.05 * jax.random.normal(k, shape, f32)
    dt_init = jnp.exp(jax.random.uniform(ks[14], (DEPTH, SSD_HEADS), f32,
                                         np.log(1e-3).astype(np.float32), np.log(1e-1).astype(np.float32)))
    return {
        "x": jax.random.normal(ks[0], (BATCH, SEQ, D_MODEL), f32),
        "p": jax.random.normal(ks[1], (DEPTH, BATCH, SEQ, D_PLE), f32),
        "norm_mix_g": gain(ks[2], (DEPTH, D_MODEL)),
        "w_in": nrm(ks[3], (DEPTH, D_MODEL, D_IN_PROJ), D_MODEL ** -0.5),
        "gm_v_norm_g": gain(ks[4], (DEPTH, GM_WIDTH)),
        "gm_ws": nrm(ks[5], (DEPTH, GM_HEADS, GM_CHUNK, GM_CHUNK), GM_CHUNK ** -0.5),
        "gm_bs": gain(ks[6], (DEPTH, GM_HEADS, GM_CHUNK)),
        "gm_out_norm_g": gain(ks[7], (DEPTH, GM_WIDTH)),
        "ssd_conv_w": nrm(ks[8], (DEPTH, SSD_CONV, SSD_CONV_CH), SSD_CONV ** -0.5),
        "ssd_conv_b": nrm(ks[9], (DEPTH, SSD_CONV_CH), 0.02),
        "ssd_dt_bias": dt_init + jnp.log(-jnp.expm1(-dt_init)),
        "ssd_a_log": jnp.log(jax.random.uniform(ks[10], (DEPTH, SSD_HEADS), f32, 1.0, 16.0)),
        "ssd_d": gain(ks[11], (DEPTH, SSD_HEADS)),
        "ssd_norm_g": gain(ks[12], (DEPTH, SSD_WIDTH)),
        "w_out": nrm(ks[13], (DEPTH, D_MIX, D_MODEL), D_MIX ** -0.5),
        "norm_mlp_g": gain(ks[15], (DEPTH, D_MODEL)),
        "w_ff1": nrm(ks[16], (DEPTH, D_MODEL, D_FF), D_MODEL ** -0.5),
        "w_ff2": nrm(ks[17], (DEPTH, D_FF, D_MODEL), D_FF ** -0.5),
        "ple_norm_g": gain(ks[18], (DEPTH, D_MODEL)),
        "w_ple_gate": nrm(ks[19], (DEPTH, D_MODEL, D_MODEL), D_MODEL ** -0.5),
        "w_ple_proj": nrm(ks[20], (DEPTH, D_PLE, D_MODEL), D_PLE ** -0.5),
        "final_norm_g": gain(ks[21], (D_MODEL,)),
    }


def _fwd_reference(x, p, norm_mix_g, w_in, gm_v_norm_g, gm_ws, gm_bs, gm_out_norm_g,
              ssd_conv_w, ssd_conv_b, ssd_dt_bias, ssd_a_log, ssd_d, ssd_norm_g,
              w_out, norm_mlp_g, w_ff1, w_ff2, ple_norm_g, w_ple_gate, w_ple_proj,
              final_norm_g):
    h = x
    for i in range(DEPTH):
        proj = rmsnorm(h, norm_mix_g[i]) @ w_in[i]
        y_a = gmlp_chunk_mixer(proj[..., :GM_COLS], gm_v_norm_g[i], gm_ws[i], gm_bs[i],
                               gm_out_norm_g[i])
        y_b = ssd_mixer(proj[..., GM_COLS:], ssd_conv_w[i], ssd_conv_b[i], ssd_dt_bias[i],
                        ssd_a_log[i], ssd_d[i], ssd_norm_g[i])
        h = h + jnp.concatenate([y_a, y_b], axis=-1) @ w_out[i]
        hid = jax.nn.relu(rmsnorm(h, norm_mlp_g[i]) @ w_ff1[i])
        h = h + (hid * hid) @ w_ff2[i]
        gate = jax.nn.sigmoid(rmsnorm(h, ple_norm_g[i]) @ w_ple_gate[i])
        h = h + gate * (p[i] @ w_ple_proj[i])
    return rmsnorm(h, final_norm_g)


import jax as _jax
import jax.numpy as _jnp

TWIN_FORMAT = 'train_step'
FWD_PARAMS = ['x', 'p', 'norm_mix_g', 'w_in', 'gm_v_norm_g', 'gm_ws', 'gm_bs', 'gm_out_norm_g', 'ssd_conv_w', 'ssd_conv_b', 'ssd_dt_bias', 'ssd_a_log', 'ssd_d', 'ssd_norm_g', 'w_out', 'norm_mlp_g', 'w_ff1', 'w_ff2', 'ple_norm_g', 'w_ple_gate', 'w_ple_proj', 'final_norm_g']
TWIN_WEIGHTS = ['norm_mix_g', 'w_in', 'gm_v_norm_g', 'gm_ws', 'gm_bs', 'gm_out_norm_g', 'ssd_conv_w', 'ssd_conv_b', 'ssd_dt_bias', 'ssd_a_log', 'ssd_d', 'ssd_norm_g', 'w_out', 'norm_mlp_g', 'w_ff1', 'w_ff2', 'ple_norm_g', 'w_ple_gate', 'w_ple_proj', 'final_norm_g']
TWIN_DIFF_INPUT = 'x'
TWIN_INPUTS = ['x', 'p', 'norm_mix_g', 'w_in', 'gm_v_norm_g', 'gm_ws', 'gm_bs', 'gm_out_norm_g', 'ssd_conv_w', 'ssd_conv_b', 'ssd_dt_bias', 'ssd_a_log', 'ssd_d', 'ssd_norm_g', 'w_out', 'norm_mlp_g', 'w_ff1', 'w_ff2', 'ple_norm_g', 'w_ple_gate', 'w_ple_proj', 'final_norm_g', 'loss_target', 'm_norm_mix_g', 'm_w_in', 'm_gm_v_norm_g', 'm_gm_ws', 'm_gm_bs', 'm_gm_out_norm_g', 'm_ssd_conv_w', 'm_ssd_conv_b', 'm_ssd_dt_bias', 'm_ssd_a_log', 'm_ssd_d', 'm_ssd_norm_g', 'm_w_out', 'm_norm_mlp_g', 'm_w_ff1', 'm_w_ff2', 'm_ple_norm_g', 'm_w_ple_gate', 'm_w_ple_proj', 'm_final_norm_g', 'v_norm_mix_g', 'v_w_in', 'v_gm_v_norm_g', 'v_gm_ws', 'v_gm_bs', 'v_gm_out_norm_g', 'v_ssd_conv_w', 'v_ssd_conv_b', 'v_ssd_dt_bias', 'v_ssd_a_log', 'v_ssd_d', 'v_ssd_norm_g', 'v_w_out', 'v_norm_mlp_g', 'v_w_ff1', 'v_w_ff2', 'v_ple_norm_g', 'v_w_ple_gate', 'v_w_ple_proj', 'v_final_norm_g']
TWIN_OUTPUTS = ['loss', 'grad_x', 'grad_norm_mix_g', 'grad_w_in', 'grad_gm_v_norm_g', 'grad_gm_ws', 'grad_gm_bs', 'grad_gm_out_norm_g', 'grad_ssd_conv_w', 'grad_ssd_conv_b', 'grad_ssd_dt_bias', 'grad_ssd_a_log', 'grad_ssd_d', 'grad_ssd_norm_g', 'grad_w_out', 'grad_norm_mlp_g', 'grad_w_ff1', 'grad_w_ff2', 'grad_ple_norm_g', 'grad_w_ple_gate', 'grad_w_ple_proj', 'grad_final_norm_g', 'delta_norm_mix_g', 'delta_w_in', 'delta_gm_v_norm_g', 'delta_gm_ws', 'delta_gm_bs', 'delta_gm_out_norm_g', 'delta_ssd_conv_w', 'delta_ssd_conv_b', 'delta_ssd_dt_bias', 'delta_ssd_a_log', 'delta_ssd_d', 'delta_ssd_norm_g', 'delta_w_out', 'delta_norm_mlp_g', 'delta_w_ff1', 'delta_w_ff2', 'delta_ple_norm_g', 'delta_w_ple_gate', 'delta_w_ple_proj', 'delta_final_norm_g', 'new_m_norm_mix_g', 'new_m_w_in', 'new_m_gm_v_norm_g', 'new_m_gm_ws', 'new_m_gm_bs', 'new_m_gm_out_norm_g', 'new_m_ssd_conv_w', 'new_m_ssd_conv_b', 'new_m_ssd_dt_bias', 'new_m_ssd_a_log', 'new_m_ssd_d', 'new_m_ssd_norm_g', 'new_m_w_out', 'new_m_norm_mlp_g', 'new_m_w_ff1', 'new_m_w_ff2', 'new_m_ple_norm_g', 'new_m_w_ple_gate', 'new_m_w_ple_proj', 'new_m_final_norm_g', 'new_v_norm_mix_g', 'new_v_w_in', 'new_v_gm_v_norm_g', 'new_v_gm_ws', 'new_v_gm_bs', 'new_v_gm_out_norm_g', 'new_v_ssd_conv_w', 'new_v_ssd_conv_b', 'new_v_ssd_dt_bias', 'new_v_ssd_a_log', 'new_v_ssd_d', 'new_v_ssd_norm_g', 'new_v_w_out', 'new_v_norm_mlp_g', 'new_v_w_ff1', 'new_v_w_ff2', 'new_v_ple_norm_g', 'new_v_w_ple_gate', 'new_v_w_ple_proj', 'new_v_final_norm_g']
TWIN_LEAF_KINDS = {'loss': 'loss', 'grad_x': 'grad_x', 'grad_norm_mix_g': 'grad_w', 'grad_w_in': 'grad_w', 'grad_gm_v_norm_g': 'grad_w', 'grad_gm_ws': 'grad_w', 'grad_gm_bs': 'grad_w', 'grad_gm_out_norm_g': 'grad_w', 'grad_ssd_conv_w': 'grad_w', 'grad_ssd_conv_b': 'grad_w', 'grad_ssd_dt_bias': 'grad_w', 'grad_ssd_a_log': 'grad_w', 'grad_ssd_d': 'grad_w', 'grad_ssd_norm_g': 'grad_w', 'grad_w_out': 'grad_w', 'grad_norm_mlp_g': 'grad_w', 'grad_w_ff1': 'grad_w', 'grad_w_ff2': 'grad_w', 'grad_ple_norm_g': 'grad_w', 'grad_w_ple_gate': 'grad_w', 'grad_w_ple_proj': 'grad_w', 'grad_final_norm_g': 'grad_w', 'delta_norm_mix_g': 'delta_w', 'delta_w_in': 'delta_w', 'delta_gm_v_norm_g': 'delta_w', 'delta_gm_ws': 'delta_w', 'delta_gm_bs': 'delta_w', 'delta_gm_out_norm_g': 'delta_w', 'delta_ssd_conv_w': 'delta_w', 'delta_ssd_conv_b': 'delta_w', 'delta_ssd_dt_bias': 'delta_w', 'delta_ssd_a_log': 'delta_w', 'delta_ssd_d': 'delta_w', 'delta_ssd_norm_g': 'delta_w', 'delta_w_out': 'delta_w', 'delta_norm_mlp_g': 'delta_w', 'delta_w_ff1': 'delta_w', 'delta_w_ff2': 'delta_w', 'delta_ple_norm_g': 'delta_w', 'delta_w_ple_gate': 'delta_w', 'delta_w_ple_proj': 'delta_w', 'delta_final_norm_g': 'delta_w', 'new_m_norm_mix_g': 'new_m', 'new_m_w_in': 'new_m', 'new_m_gm_v_norm_g': 'new_m', 'new_m_gm_ws': 'new_m', 'new_m_gm_bs': 'new_m', 'new_m_gm_out_norm_g': 'new_m', 'new_m_ssd_conv_w': 'new_m', 'new_m_ssd_conv_b': 'new_m', 'new_m_ssd_dt_bias': 'new_m', 'new_m_ssd_a_log': 'new_m', 'new_m_ssd_d': 'new_m', 'new_m_ssd_norm_g': 'new_m', 'new_m_w_out': 'new_m', 'new_m_norm_mlp_g': 'new_m', 'new_m_w_ff1': 'new_m', 'new_m_w_ff2': 'new_m', 'new_m_ple_norm_g': 'new_m', 'new_m_w_ple_gate': 'new_m', 'new_m_w_ple_proj': 'new_m', 'new_m_final_norm_g': 'new_m', 'new_v_norm_mix_g': 'new_v', 'new_v_w_in': 'new_v', 'new_v_gm_v_norm_g': 'new_v', 'new_v_gm_ws': 'new_v', 'new_v_gm_bs': 'new_v', 'new_v_gm_out_norm_g': 'new_v', 'new_v_ssd_conv_w': 'new_v', 'new_v_ssd_conv_b': 'new_v', 'new_v_ssd_dt_bias': 'new_v', 'new_v_ssd_a_log': 'new_v', 'new_v_ssd_d': 'new_v', 'new_v_ssd_norm_g': 'new_v', 'new_v_w_out': 'new_v', 'new_v_norm_mlp_g': 'new_v', 'new_v_w_ff1': 'new_v', 'new_v_w_ff2': 'new_v', 'new_v_ple_norm_g': 'new_v', 'new_v_w_ple_gate': 'new_v', 'new_v_w_ple_proj': 'new_v', 'new_v_final_norm_g': 'new_v'}


def _forward(args):
    return _fwd_reference(*[args[k] for k in FWD_PARAMS])


def _output_shape():
    out = _jax.eval_shape(lambda: _forward(_fwd_setup_inputs(0)))
    return out.shape, out.dtype

N_MICROBATCH = 1
ADAM_LR = 0.001
ADAM_B1 = 0.9
ADAM_B2 = 0.999
ADAM_EPS = 1e-08
ADAM_WD = 0.01
ADAM_STEP = 10
PER_EXAMPLE_BATCH_AXIS = {'x': 0, 'p': 1, 'loss_target': 0}
SHARED_INPUTS = []
_WEIGHT_DTYPES = {'norm_mix_g': _jnp.float32, 'w_in': _jnp.float32, 'gm_v_norm_g': _jnp.float32, 'gm_ws': _jnp.float32, 'gm_bs': _jnp.float32, 'gm_out_norm_g': _jnp.float32, 'ssd_conv_w': _jnp.float32, 'ssd_conv_b': _jnp.float32, 'ssd_dt_bias': _jnp.float32, 'ssd_a_log': _jnp.float32, 'ssd_d': _jnp.float32, 'ssd_norm_g': _jnp.float32, 'w_out': _jnp.float32, 'norm_mlp_g': _jnp.float32, 'w_ff1': _jnp.float32, 'w_ff2': _jnp.float32, 'ple_norm_g': _jnp.float32, 'w_ple_gate': _jnp.float32, 'w_ple_proj': _jnp.float32, 'final_norm_g': _jnp.float32}
MOMENT_SCALE = {'norm_mix_g': 2.595193e-01, 'w_in': 1.177806e-01, 'gm_v_norm_g': 7.371659e-02, 'gm_ws': 7.314801e-02, 'gm_bs': 1.117701e-01, 'gm_out_norm_g': 1.838137e-01, 'ssd_conv_w': 1.219578e-01, 'ssd_conv_b': 2.187494e-01, 'ssd_dt_bias': 4.337086e-01, 'ssd_a_log': 9.021708e-01, 'ssd_d': 1.114834e+00, 'ssd_norm_g': 1.683366e-01, 'w_out': 2.352041e-01, 'norm_mlp_g': 1.943661e-01, 'w_ff1': 1.004227e-01, 'w_ff2': 3.114785e-01, 'ple_norm_g': 3.427369e-02, 'w_ple_gate': 3.693527e-02, 'w_ple_proj': 7.098804e-02, 'final_norm_g': 6.465822e+01}


def _to_microbatches(a, axis):
    t = _jnp.moveaxis(a, axis, 0)
    t = t.reshape((N_MICROBATCH, t.shape[0] // N_MICROBATCH) + t.shape[1:])
    return _jnp.moveaxis(t, 1, axis + 1)


def setup_inputs(seed: int = 0) -> dict:
    inp = _fwd_setup_inputs(seed)
    key = _jax.random.fold_in(_jax.random.key(seed), 7919)
    shape, _ = _output_shape()
    out = dict(inp)
    out["loss_target"] = _jax.random.normal(_jax.random.fold_in(key, 0), shape, _jnp.float32)
    for i, name in enumerate(TWIN_WEIGHTS):
        w = inp[name].astype(_jnp.float32)
        if MOMENT_SCALE is None:
            s = _jnp.sqrt(_jnp.mean(_jnp.square(w)) + 1e-30)
        else:
            s = MOMENT_SCALE[name]
        km, kv = _jax.random.split(_jax.random.fold_in(key, i + 1))
        out[name] = w
        out["m_" + name] = s * _jax.random.normal(km, w.shape, _jnp.float32)
        out["v_" + name] = (s * s) * _jax.random.uniform(kv, w.shape, _jnp.float32, 0.5, 1.5)
    if N_MICROBATCH > 1:
        for name, axis in PER_EXAMPLE_BATCH_AXIS.items():
            out[name] = _to_microbatches(out[name], axis)
    return {'x': out['x'], 'p': out['p'], 'norm_mix_g': out['norm_mix_g'], 'w_in': out['w_in'], 'gm_v_norm_g': out['gm_v_norm_g'], 'gm_ws': out['gm_ws'], 'gm_bs': out['gm_bs'], 'gm_out_norm_g': out['gm_out_norm_g'], 'ssd_conv_w': out['ssd_conv_w'], 'ssd_conv_b': out['ssd_conv_b'], 'ssd_dt_bias': out['ssd_dt_bias'], 'ssd_a_log': out['ssd_a_log'], 'ssd_d': out['ssd_d'], 'ssd_norm_g': out['ssd_norm_g'], 'w_out': out['w_out'], 'norm_mlp_g': out['norm_mlp_g'], 'w_ff1': out['w_ff1'], 'w_ff2': out['w_ff2'], 'ple_norm_g': out['ple_norm_g'], 'w_ple_gate': out['w_ple_gate'], 'w_ple_proj': out['w_ple_proj'], 'final_norm_g': out['final_norm_g'], 'loss_target': out['loss_target'], 'm_norm_mix_g': out['m_norm_mix_g'], 'm_w_in': out['m_w_in'], 'm_gm_v_norm_g': out['m_gm_v_norm_g'], 'm_gm_ws': out['m_gm_ws'], 'm_gm_bs': out['m_gm_bs'], 'm_gm_out_norm_g': out['m_gm_out_norm_g'], 'm_ssd_conv_w': out['m_ssd_conv_w'], 'm_ssd_conv_b': out['m_ssd_conv_b'], 'm_ssd_dt_bias': out['m_ssd_dt_bias'], 'm_ssd_a_log': out['m_ssd_a_log'], 'm_ssd_d': out['m_ssd_d'], 'm_ssd_norm_g': out['m_ssd_norm_g'], 'm_w_out': out['m_w_out'], 'm_norm_mlp_g': out['m_norm_mlp_g'], 'm_w_ff1': out['m_w_ff1'], 'm_w_ff2': out['m_w_ff2'], 'm_ple_norm_g': out['m_ple_norm_g'], 'm_w_ple_gate': out['m_w_ple_gate'], 'm_w_ple_proj': out['m_w_ple_proj'], 'm_final_norm_g': out['m_final_norm_g'], 'v_norm_mix_g': out['v_norm_mix_g'], 'v_w_in': out['v_w_in'], 'v_gm_v_norm_g': out['v_gm_v_norm_g'], 'v_gm_ws': out['v_gm_ws'], 'v_gm_bs': out['v_gm_bs'], 'v_gm_out_norm_g': out['v_gm_out_norm_g'], 'v_ssd_conv_w': out['v_ssd_conv_w'], 'v_ssd_conv_b': out['v_ssd_conv_b'], 'v_ssd_dt_bias': out['v_ssd_dt_bias'], 'v_ssd_a_log': out['v_ssd_a_log'], 'v_ssd_d': out['v_ssd_d'], 'v_ssd_norm_g': out['v_ssd_norm_g'], 'v_w_out': out['v_w_out'], 'v_norm_mlp_g': out['v_norm_mlp_g'], 'v_w_ff1': out['v_w_ff1'], 'v_w_ff2': out['v_w_ff2'], 'v_ple_norm_g': out['v_ple_norm_g'], 'v_w_ple_gate': out['v_w_ple_gate'], 'v_w_ple_proj': out['v_w_ple_proj'], 'v_final_norm_g': out['v_final_norm_g']}


def _loss(weights, diff, rest, loss_target):
    with _jax.named_scope("forward"):
        args = {**rest, TWIN_DIFF_INPUT: diff, **{k: w.astype(_WEIGHT_DTYPES[k]) for k, w in weights.items()}}
        y = _forward(args)
    with _jax.named_scope("loss_head"):
        err = _jnp.square(y.astype(_jnp.float32) - loss_target)
        return 0.5 * _jnp.sum(_jnp.mean(err, axis=-1)) if err.ndim else 0.5 * err


def _adamw(w, g, m, v):
    m = ADAM_B1 * m + (1.0 - ADAM_B1) * g
    v = ADAM_B2 * v + (1.0 - ADAM_B2) * _jnp.square(g)
    m_hat = m / (1.0 - ADAM_B1 ** ADAM_STEP)
    v_hat = v / (1.0 - ADAM_B2 ** ADAM_STEP)
    delta = -ADAM_LR * (m_hat / (_jnp.sqrt(v_hat) + ADAM_EPS) + ADAM_WD * w)
    return delta, m, v


def reference(x, p, norm_mix_g, w_in, gm_v_norm_g, gm_ws, gm_bs, gm_out_norm_g, ssd_conv_w, ssd_conv_b, ssd_dt_bias, ssd_a_log, ssd_d, ssd_norm_g, w_out, norm_mlp_g, w_ff1, w_ff2, ple_norm_g, w_ple_gate, w_ple_proj, final_norm_g, loss_target, m_norm_mix_g, m_w_in, m_gm_v_norm_g, m_gm_ws, m_gm_bs, m_gm_out_norm_g, m_ssd_conv_w, m_ssd_conv_b, m_ssd_dt_bias, m_ssd_a_log, m_ssd_d, m_ssd_norm_g, m_w_out, m_norm_mlp_g, m_w_ff1, m_w_ff2, m_ple_norm_g, m_w_ple_gate, m_w_ple_proj, m_final_norm_g, v_norm_mix_g, v_w_in, v_gm_v_norm_g, v_gm_ws, v_gm_bs, v_gm_out_norm_g, v_ssd_conv_w, v_ssd_conv_b, v_ssd_dt_bias, v_ssd_a_log, v_ssd_d, v_ssd_norm_g, v_w_out, v_norm_mlp_g, v_w_ff1, v_w_ff2, v_ple_norm_g, v_w_ple_gate, v_w_ple_proj, v_final_norm_g):
    given = dict(x=x, p=p, norm_mix_g=norm_mix_g, w_in=w_in, gm_v_norm_g=gm_v_norm_g, gm_ws=gm_ws, gm_bs=gm_bs, gm_out_norm_g=gm_out_norm_g, ssd_conv_w=ssd_conv_w, ssd_conv_b=ssd_conv_b, ssd_dt_bias=ssd_dt_bias, ssd_a_log=ssd_a_log, ssd_d=ssd_d, ssd_norm_g=ssd_norm_g, w_out=w_out, norm_mlp_g=norm_mlp_g, w_ff1=w_ff1, w_ff2=w_ff2, ple_norm_g=ple_norm_g, w_ple_gate=w_ple_gate, w_ple_proj=w_ple_proj, final_norm_g=final_norm_g, loss_target=loss_target, m_norm_mix_g=m_norm_mix_g, m_w_in=m_w_in, m_gm_v_norm_g=m_gm_v_norm_g, m_gm_ws=m_gm_ws, m_gm_bs=m_gm_bs, m_gm_out_norm_g=m_gm_out_norm_g, m_ssd_conv_w=m_ssd_conv_w, m_ssd_conv_b=m_ssd_conv_b, m_ssd_dt_bias=m_ssd_dt_bias, m_ssd_a_log=m_ssd_a_log, m_ssd_d=m_ssd_d, m_ssd_norm_g=m_ssd_norm_g, m_w_out=m_w_out, m_norm_mlp_g=m_norm_mlp_g, m_w_ff1=m_w_ff1, m_w_ff2=m_w_ff2, m_ple_norm_g=m_ple_norm_g, m_w_ple_gate=m_w_ple_gate, m_w_ple_proj=m_w_ple_proj, m_final_norm_g=m_final_norm_g, v_norm_mix_g=v_norm_mix_g, v_w_in=v_w_in, v_gm_v_norm_g=v_gm_v_norm_g, v_gm_ws=v_gm_ws, v_gm_bs=v_gm_bs, v_gm_out_norm_g=v_gm_out_norm_g, v_ssd_conv_w=v_ssd_conv_w, v_ssd_conv_b=v_ssd_conv_b, v_ssd_dt_bias=v_ssd_dt_bias, v_ssd_a_log=v_ssd_a_log, v_ssd_d=v_ssd_d, v_ssd_norm_g=v_ssd_norm_g, v_w_out=v_w_out, v_norm_mlp_g=v_norm_mlp_g, v_w_ff1=v_w_ff1, v_w_ff2=v_w_ff2, v_ple_norm_g=v_ple_norm_g, v_w_ple_gate=v_w_ple_gate, v_w_ple_proj=v_w_ple_proj, v_final_norm_g=v_final_norm_g)
    weights = {n: given[n] for n in TWIN_WEIGHTS}
    shared = {n: given[n] for n in SHARED_INPUTS}
    per_example = {n: given[n] for n in ['x', 'p']}
    grad_fn = _jax.value_and_grad(_loss, argnums=(0, 1))

    def one_microbatch(ex, loss_target):
        ex = dict(ex)
        diff = ex.pop(TWIN_DIFF_INPUT)
        return grad_fn(weights, diff, {**shared, **ex}, loss_target)

    if N_MICROBATCH == 1:
        loss, (grad_w, grad_x) = one_microbatch(per_example, given["loss_target"])
    else:
        def body(carry, xs):
            loss_sum, grad_sum = carry
            l_k, (gw_k, gx_k) = one_microbatch(xs[0], xs[1])
            with _jax.named_scope("update"):
                return (loss_sum + l_k, _jax.tree.map(_jnp.add, grad_sum, gw_k)), gx_k

        init = (_jnp.zeros((), _jnp.float32), _jax.tree.map(_jnp.zeros_like, weights))
        (loss, grad_w), grad_x = _jax.lax.scan(body, init, (per_example, given["loss_target"]))
    with _jax.named_scope("update"):
        delta_w, new_m, new_v = {}, {}, {}
        for n in TWIN_WEIGHTS:
            delta_w[n], new_m[n], new_v[n] = _adamw(weights[n], grad_w[n], given["m_" + n], given["v_" + n])
    return (loss, grad_x, *[grad_w[n] for n in TWIN_WEIGHTS], *[delta_w[n] for n in TWIN_WEIGHTS],
            *[new_m[n] for n in TWIN_WEIGHTS], *[new_v[n] for n in TWIN_WEIGHTS])
```

```python
import functools
import math

import jax
import jax.numpy as jnp
import numpy as np
from jax import lax
from jax.experimental import pallas as pl
from jax.experimental.pallas import tpu as pltpu

_F32 = jnp.float32
_BF16 = jnp.bfloat16
_MXU = jnp.bfloat16

D = 1024
D_PLE = 256
GM_W = 1024
GM_H = 8
CH = 128
SSD_W = 1024
SSD_H = 16
SSD_P = 64
SSD_G = 2
SSD_N = 128
CONV_K = 4
CONV_CH = SSD_W + 2 * SSD_G * SSD_N
D_FF = 4096
D_IN = 2 * GM_W + SSD_W + CONV_CH + SSD_H
D_IN_PAD = 4736
DT_BLK = (D_IN_PAD - 128) // 128
EPS = 1e-6
N_DEV = 8
SHARD_IN = D_IN // N_DEV

LR, B1, B2, ADAM_EPS, WD, STEP = 0.001, 0.9, 0.999, 1e-08, 0.01, 10

_V7X_VMEM_BYTES = 64 * 1024 * 1024
_VMEM_CAP = _V7X_VMEM_BYTES - 8 * 1024 * 1024
_MESH = pl.DeviceIdType.MESH


def _vmem_limit(nbytes):
    return int(min(_VMEM_CAP, max(32 * 1024 * 1024, nbytes * 5 // 4 + (4 << 20))))


def _nbytes(shape, dtype):
    return int(np.prod(shape)) * jnp.dtype(dtype).itemsize


def _mx(v):
    return v.astype(_MXU)


def _dot(a, b):
    return jnp.dot(a, b, preferred_element_type=_F32)


def _dot_nt(a, b):
    return lax.dot_general(a, b, (((1,), (1,)), ((), ())), preferred_element_type=_F32)


def _dot_tn(a, b):
    return lax.dot_general(a, b, (((0,), (0,)), ((), ())), preferred_element_type=_F32)


def _split3(a):
    hi = a.astype(_BF16)
    r = a - hi.astype(_F32)
    mid = r.astype(_BF16)
    lo = (r - mid.astype(_F32)).astype(_BF16)
    return hi, mid, lo


def _xdot(dotfn, a, b01):
    b = b01.astype(_BF16)
    hi, mid, lo = _split3(a)
    return (dotfn(hi, b) + dotfn(mid, b)) + dotfn(lo, b)


def _xdot_left(dotfn, a01, b):
    a = a01.astype(_BF16)
    hi, mid, lo = _split3(b)
    return (dotfn(a, hi) + dotfn(a, mid)) + dotfn(a, lo)


def _sum8(v):
    r, n = v.shape
    return v.reshape(r // 8, 8, n).sum(axis=0)


def _sigmoid(v):
    return 1.0 / (1.0 + jnp.exp(-v))


def _gelu(v):
    return 0.5 * v * (1.0 + lax.erf(v * 0.7071067811865476))


def _gelu_grad(v):
    return 0.5 * (1.0 + lax.erf(v * 0.7071067811865476)) + v * jnp.exp(-0.5 * v * v) * 0.3989422804014327


def _rms(xv, g):
    ms = jnp.mean(xv * xv, axis=-1, keepdims=True)
    return xv * lax.rsqrt(ms + EPS) * g


def _rms_bwd(xv, g, dn):
    n = xv.shape[-1]
    r = lax.rsqrt(jnp.mean(xv * xv, axis=-1, keepdims=True) + EPS)
    gy = dn * g
    dot = jnp.sum(gy * xv, axis=-1, keepdims=True)
    dx = r * gy - xv * (r * r * r) * (dot * (1.0 / n))
    return dx, _sum8(dn * xv * r)


def _iota2(shape, axis):
    return lax.broadcasted_iota(jnp.int32, shape, axis)


def _norm_cast(name, x, g, tm=512):
    t, n = x.shape
    tm = min(tm, t)

    def body(x_ref, g_ref, o_ref):
        o_ref[...] = _rms(x_ref[...], g_ref[...]).astype(o_ref.dtype)

    return pl.pallas_call(
        body, name=name, grid=(t // tm,),
        in_specs=[pl.BlockSpec((tm, n), lambda i: (i, 0)), pl.BlockSpec((1, n), lambda i: (0, 0))],
        out_specs=pl.BlockSpec((tm, n), lambda i: (i, 0)),
        out_shape=jax.ShapeDtypeStruct((t, n), _MXU),
        compiler_params=pltpu.CompilerParams(dimension_semantics=("parallel",)),
    )(x, g)


def _matmul(name, a, b, mode, tm, tn, tk, epilogue, outs, extras=()):
    m, k = a.shape[::-1] if mode == "tn" else a.shape
    n = b.shape[0] if mode == "nt" else b.shape[1]
    tm, tn, tk = min(tm, m), min(tn, n), min(tk, k)
    assert m % tm == 0 and n % tn == 0 and k % tk == 0, (name, m, n, k, tm, tn, tk)
    if mode == "nn":
        a_spec = pl.BlockSpec((tm, tk), lambda i, j, kk: (i, kk))
        b_spec = pl.BlockSpec((tk, tn), lambda i, j, kk: (kk, j))
        dotfn = _dot
    elif mode == "nt":
        a_spec = pl.BlockSpec((tm, tk), lambda i, j, kk: (i, kk))
        b_spec = pl.BlockSpec((tn, tk), lambda i, j, kk: (j, kk))
        dotfn = _dot_nt
    else:
        a_spec = pl.BlockSpec((tk, tm), lambda i, j, kk: (kk, i))
        b_spec = pl.BlockSpec((tk, tn), lambda i, j, kk: (kk, j))
        dotfn = _dot_tn
    nk = k // tk
    n_ex, n_out = len(extras), len(outs)

    in_specs, vmem = [a_spec, b_spec], 2 * (tm * tk * a.dtype.itemsize + tk * tn * b.dtype.itemsize)
    for arr, kind in extras:
        if kind == "tile":
            in_specs.append(pl.BlockSpec((tm, tn), lambda i, j, kk: (i, j)))
            vmem += 2 * _nbytes((tm, tn), arr.dtype)
        else:
            in_specs.append(pl.BlockSpec((1, tn), lambda i, j, kk: (0, j)))
    out_specs, out_shape = [], []
    for kind, dt in outs:
        if kind == "tile":
            out_specs.append(pl.BlockSpec((tm, tn), lambda i, j, kk: (i, j)))
            out_shape.append(jax.ShapeDtypeStruct((m, n), dt))
            vmem += 2 * _nbytes((tm, tn), dt)
        else:
            out_specs.append(pl.BlockSpec((8, tn), lambda i, j, kk: (i, j)))
            out_shape.append(jax.ShapeDtypeStruct((8 * (m // tm), n), dt))
    scratch = [pltpu.VMEM((tm, tn), _F32)] if nk > 1 else []
    vmem += _nbytes((tm, tn), _F32) * (2 if nk > 1 else 1)

    def body(*refs):
        a_ref, b_ref = refs[0], refs[1]
        ex_refs = refs[2:2 + n_ex]
        out_refs = refs[2 + n_ex:2 + n_ex + n_out]
        part = dotfn(_mx(a_ref[...]), _mx(b_ref[...]))

        def finish(acc):
            vals = epilogue(acc, *[r[...] for r in ex_refs])
            for r, v in zip(out_refs, vals):
                r[...] = v.astype(r.dtype)

        if nk == 1:
            finish(part)
        else:
            acc_ref = refs[-1]
            kk = pl.program_id(2)

            @pl.when(kk == 0)
            def _():
                acc_ref[...] = part

            @pl.when(kk > 0)
            def _():
                acc_ref[...] += part

            @pl.when(kk == nk - 1)
            def _():
                finish(acc_ref[...])

    res = pl.pallas_call(
        body, name=name, grid=(m // tm, n // tn, nk),
        in_specs=in_specs, out_specs=out_specs, out_shape=out_shape, scratch_shapes=scratch,
        compiler_params=pltpu.CompilerParams(
            dimension_semantics=("parallel", "parallel", "arbitrary"), vmem_limit_bytes=_vmem_limit(vmem)),
    )(a, b, *[arr for arr, _ in extras])
    return res


def _shift_down(v, halo8, j):
    if j == 0:
        return v
    r = pltpu.roll(v, j, axis=0)
    hr = pltpu.roll(halo8, j, axis=0)
    top = jnp.where(_iota2(hr.shape, 0) < j, hr, r[:8])
    return jnp.concatenate([top, r[8:]], axis=0)


def _shift_up(v, next8, j):
    if j == 0:
        return v
    rows = v.shape[0]
    r = pltpu.roll(v, rows - j, axis=0)
    nr = pltpu.roll(next8, 8 - j, axis=0)
    bot = jnp.where(_iota2(nr.shape, 0) >= 8 - j, nr, r[rows - 8:])
    return jnp.concatenate([r[:rows - 8], bot], axis=0)


def _silu_grad(c):
    s = _sigmoid(c)
    return s * (1.0 + c * (1.0 - s))


def _gmlp_fwd_vals(pu, pv, gv, ws_ref, bsb_ref, want_bwd):
    tril = _iota2((CH, CH), 0) >= _iota2((CH, CH), 1)
    u = _gelu(pu)
    v = _gelu(pv)
    ys, keep = [], []
    for h in range(GM_H):
        sl = slice(h * 128, (h + 1) * 128)
        vh = v[:, sl]
        r = lax.rsqrt(jnp.mean(vh * vh, axis=-1, keepdims=True) + EPS)
        vn = vh * r * gv[:, sl]
        wm = _mx(jnp.where(tril, ws_ref[h], 0.0))
        mixed = _dot(wm, _mx(vn)) + bsb_ref[h]
        ys.append(u[:, sl] * mixed)
        if want_bwd:
            keep.append((vh, r, vn, wm, mixed))
    return jnp.concatenate(ys, axis=1), u, keep


def _ssd_common(xbc, halo8, dtraw, cw_ref, cb, dtb, alog, e_ref, ltri):
    xs = [_shift_down(xbc, halo8, j) for j in range(CONV_K)]
    cpre = cb + sum(cw_ref[k:k + 1, :] * xs[CONV_K - 1 - k] for k in range(CONV_K))
    act = cpre * _sigmoid(cpre)
    dtin = dtraw + dtb
    dt = jnp.maximum(dtin, 0.0) + jnp.log(1.0 + jnp.exp(-jnp.abs(dtin)))
    a_neg = -jnp.exp(alog)
    cs = _xdot_left(_dot, ltri, dt * a_neg)
    cs_last = cs[CH - 1:CH, :]
    ecs = jnp.exp(cs)
    dec = jnp.exp(cs_last - cs)
    cdec = jnp.exp(cs_last)
    e = e_ref[...]
    dt_x = _xdot(_dot, dt, e)
    ecs_x = _xdot(_dot, ecs, e)
    dec_x = _xdot(_dot, dec, e)
    cdec_x = _xdot(_dot, jnp.broadcast_to(cdec, (8, 128)), e)[0:1, :]
    return dict(xs=xs, cpre=cpre, act=act, dtin=dtin, dt=dt, a_neg=a_neg, cs=cs, ecs=ecs, dec=dec, cdec=cdec,
                dt_x=dt_x, ecs_x=ecs_x, dec_x=dec_x, cdec_x=cdec_x)


def _head_lm(csb_ref, cst_ref, h, tril):
    seg = csb_ref[:, h * 128:(h + 1) * 128] - cst_ref[h:h + 1, :]
    return jnp.exp(jnp.where(tril, seg, -jnp.inf))


def _mixer_fwd(proj, gv, ws, bsb, gout, cw8, cb, dtb, alog, d_x, ng, e_mat, e2_mat, ltri_mat, seq_chunks):
    t = proj.shape[0]
    n_chunks = t // CH

    def body(pu_ref, pv_ref, z_ref, xbc_ref, dt_ref, halo_ref, gv_ref, ws_ref, bsb_ref, gout_ref, cw_ref, cb_ref,
             dtb_ref, alog_ref, dx_ref, ng_ref, e_ref, e2_ref, ltri_ref,
             cat_ref, y_ref, st_ref, s_ref, csb_ref, cst_ref):
        c = pl.program_id(0)
        first = (c % seq_chunks) == 0
        tril = _iota2((CH, CH), 0) >= _iota2((CH, CH), 1)
        lane = _iota2((CH, 128), 1)

        y_a, _, _ = _gmlp_fwd_vals(pu_ref[...], pv_ref[...], gv_ref[...], ws_ref, bsb_ref, False)
        cat_ref[:, 0:GM_W] = _rms(y_a, gout_ref[...]).astype(cat_ref.dtype)

        @pl.when(first)
        def _():
            s_ref[...] = jnp.zeros_like(s_ref)

        halo8 = jnp.where(first, 0.0, halo_ref[...])
        q = _ssd_common(xbc_ref[...], halo8, dt_ref[...], cw_ref, cb_ref[...], dtb_ref[...], alog_ref[...], e_ref,
                        ltri_ref[...])
        act = q["act"]
        xv = act[:, 0:SSD_W]
        xdt = xv * q["dt_x"]
        xdt_m = _mx(xdt)
        csb_ref[...] = _xdot(_dot, q["cs"], e2_ref[...])
        cst_ref[...] = q["cs"].T
        s_prev = s_ref[...]
        st_ref[...] = s_prev
        ys = []
        for g in range(SSD_G):
            bg = _mx(act[:, SSD_W + g * SSD_N:SSD_W + (g + 1) * SSD_N])
            cg = _mx(act[:, SSD_W + SSD_G * SSD_N + g * SSD_N:SSD_W + SSD_G * SSD_N + (g + 1) * SSD_N])
            cbm = _dot_nt(cg, bg)
            gs = slice(g * 512, (g + 1) * 512)
            for pr in range(4):
                ps = slice(g * 512 + pr * 128, g * 512 + (pr + 1) * 128)
                o = []
                for hh in range(2):
                    h = g * 8 + pr * 2 + hh
                    m_h = _mx(cbm * _head_lm(csb_ref, cst_ref, h, tril))
                    o.append(_dot(m_h, xdt_m[:, ps]))
                ys.append(jnp.where(lane < SSD_P, o[0], o[1]))
            sg = s_prev[:, gs]
            yoff = _dot(cg, _mx(sg)) * q["ecs_x"][:, gs]
            ys[-4:] = [ys[-4 + i] + yoff[:, i * 128:(i + 1) * 128] for i in range(4)]
            st_new = _dot_tn(bg, _mx(q["dec_x"][:, gs] * xdt[:, gs]))
            s_ref[:, gs] = sg * q["cdec_x"][:, gs] + st_new
        y = jnp.concatenate(ys, axis=1) + dx_ref[...] * xv
        y_ref[...] = y
        zv = z_ref[...]
        yg = y * (zv * _sigmoid(zv))
        for g in range(SSD_G):
            gs = slice(g * 512, (g + 1) * 512)
            cat_ref[:, GM_W + g * 512:GM_W + (g + 1) * 512] = _rms(yg[:, gs], ng_ref[:, gs]).astype(cat_ref.dtype)

    blk = lambda w, j: pl.BlockSpec((CH, w), lambda c: (c, j))
    full = lambda arr: pl.BlockSpec(arr.shape, lambda c: (0,) * arr.ndim)
    consts = [gv, ws, bsb, gout, cw8, cb, dtb, alog, d_x, ng, e_mat, e2_mat, ltri_mat]
    return pl.pallas_call(
        body, name="mixer_fwd", grid=(n_chunks,),
        in_specs=[blk(GM_W, 0), blk(GM_W, 1), blk(SSD_W, 2), blk(CONV_CH, 2), blk(128, DT_BLK),
                  pl.BlockSpec((8, CONV_CH), lambda c: (jnp.maximum(c * (CH // 8) - 1, 0), 2))]
        + [full(a) for a in consts],
        out_specs=[pl.BlockSpec((CH, 2 * D), lambda c: (c, 0)), pl.BlockSpec((CH, SSD_W), lambda c: (c, 0)),
                   pl.BlockSpec((CH, SSD_W), lambda c: (c, 0))],
        out_shape=[jax.ShapeDtypeStruct((t, 2 * D), _MXU), jax.ShapeDtypeStruct((t, SSD_W), _F32),
                   jax.ShapeDtypeStruct((t, SSD_W), _F32)],
        scratch_shapes=[pltpu.VMEM((SSD_N, SSD_W), _F32), pltpu.VMEM((CH, SSD_H * 128), _F32),
                        pltpu.VMEM((128, CH), _F32)],
        compiler_params=pltpu.CompilerParams(dimension_semantics=("arbitrary",), vmem_limit_bytes=48 << 20),
    )(proj, proj, proj, proj, proj, proj, *consts)


def _mixer_bwd(proj, dcat, yss, states, gv, ws, bsb, gout, cw8, cb, dtb, alog, d_x, ng, e_mat, et_mat, e2_mat,
               ltri_mat, seq_chunks):
    t = proj.shape[0]
    n_chunks = t // CH

    def body(pu_ref, pv_ref, z_ref, xbc_ref, dt_ref, halo_ref, dcat_ref, y_ref, st_ref,
             gv_ref, ws_ref, bsb_ref, gout_ref, cw_ref, cb_ref, dtb_ref, alog_ref, dx_ref, ng_ref,
             e_ref, et_ref, e2_ref, ltri_ref,
             dproj_ref, dws_ref, dbs_ref, dgv_ref, dgout_ref, dng_ref, dcw_ref, dcb_ref, ddtb_ref, dalog_ref, dd_ref,
             ds_ref, dnext_ref, csb_ref, cst_ref, dbacc_ref, ddacc_ref):
        i = pl.program_id(0)
        c = n_chunks - 1 - i
        first = (c % seq_chunks) == 0
        last_in_seq = (c % seq_chunks) == seq_chunks - 1
        tril = _iota2((CH, CH), 0) >= _iota2((CH, CH), 1)
        lane = _iota2((CH, 128), 1)
        row = _iota2((CH, 128), 0)
        ones_bf = jnp.ones((CH, 128), _BF16)

        @pl.when(i == 0)
        def _():
            for r in (dws_ref, dbs_ref, dgv_ref, dgout_ref, dng_ref, dcw_ref, dcb_ref, ddtb_ref, dalog_ref, dd_ref,
                      dbacc_ref, ddacc_ref):
                r[...] = jnp.zeros_like(r)

        @pl.when(last_in_seq)
        def _():
            ds_ref[...] = jnp.zeros_like(ds_ref)
            dnext_ref[...] = jnp.zeros_like(dnext_ref)

        dcat_v = dcat_ref[...].astype(_F32)

        pu, pv = pu_ref[...], pv_ref[...]
        gv_v = gv_ref[...]
        y_a, u, keep = _gmlp_fwd_vals(pu, pv, gv_v, ws_ref, bsb_ref, True)
        dy, dgout8 = _rms_bwd(y_a, gout_ref[...], dcat_v[:, 0:GM_W])
        dgout_ref[...] += dgout8
        dus, dvs, dgvs = [], [], []
        for h in range(GM_H):
            sl = slice(h * 128, (h + 1) * 128)
            vh, r, vn, wm, mixed = keep[h]
            dyh = dy[:, sl]
            dus.append(dyh * mixed)
            dmix = dyh * u[:, sl]
            dmix_m = _mx(dmix)
            dws_ref[h] += jnp.where(tril, _dot_nt(dmix_m, _mx(vn)), 0.0)
            dbacc_ref[h] += dmix
            dvn = _dot_tn(wm, dmix_m)
            gy = dvn * gv_v[:, sl]
            dot = jnp.sum(gy * vh, axis=-1, keepdims=True)
            dvs.append(r * gy - vh * (r * r * r) * (dot * (1.0 / 128)))
            dgvs.append(_sum8(dvn * vh * r))
        dgv_ref[...] += jnp.concatenate(dgvs, axis=1)
        dproj_ref[:, 0:GM_W] = (jnp.concatenate(dus, axis=1) * _gelu_grad(pu)).astype(dproj_ref.dtype)
        dproj_ref[:, GM_W:2 * GM_W] = (jnp.concatenate(dvs, axis=1) * _gelu_grad(pv)).astype(dproj_ref.dtype)

        halo8 = jnp.where(first, 0.0, halo_ref[...])
        q = _ssd_common(xbc_ref[...], halo8, dt_ref[...], cw_ref, cb_ref[...], dtb_ref[...], alog_ref[...], e_ref,
                        ltri_ref[...])
        act = q["act"]
        xv = act[:, 0:SSD_W]
        dt_x, ecs_x, dec_x, cdec_x = q["dt_x"], q["ecs_x"], q["dec_x"], q["cdec_x"]
        xdt = xv * dt_x
        xdt_m = _mx(xdt)
        csb_ref[...] = _xdot(_dot, q["cs"], e2_ref[...])
        cst_ref[...] = q["cs"].T
        s_prev = st_ref[...]
        ds = ds_ref[...]
        yv = y_ref[...]
        zv = z_ref[...]
        sz = zv * _sigmoid(zv)
        yg = yv * sz
        dygs, dng8 = [], []
        for g in range(SSD_G):
            gs = slice(g * 512, (g + 1) * 512)
            a_, b_ = _rms_bwd(yg[:, gs], ng_ref[:, gs], dcat_v[:, GM_W + g * 512:GM_W + (g + 1) * 512])
            dygs.append(a_)
            dng8.append(b_)
        dyg = jnp.concatenate(dygs, axis=1)
        dng_ref[...] += jnp.concatenate(dng8, axis=1)
        dyv = dyg * sz
        dproj_ref[:, 2 * GM_W:2 * GM_W + SSD_W] = (dyg * yv * _silu_grad(zv)).astype(dproj_ref.dtype)
        ddacc_ref[...] += _sum8(dyv * xv)
        dyv_m = _mx(dyv)

        dxdt_parts, db_parts, dc_parts = [], [], []
        dcs = jnp.zeros((CH, 128), _F32)
        dcs_x_parts, ddec_x_parts, dcl_x_parts = [], [], []
        for g in range(SSD_G):
            gs = slice(g * 512, (g + 1) * 512)
            bg = _mx(act[:, SSD_W + g * SSD_N:SSD_W + (g + 1) * SSD_N])
            cg = _mx(act[:, SSD_W + SSD_G * SSD_N + g * SSD_N:SSD_W + SSD_G * SSD_N + (g + 1) * SSD_N])
            cbm = _dot_nt(cg, bg)
            sg = s_prev[:, gs]
            sg_m = _mx(sg)
            dsg = ds[:, gs]
            dsg_m = _mx(dsg)
            zoff = _dot(cg, sg_m)
            dz_off = dyv[:, gs] * ecs_x[:, gs]
            dz_off_m = _mx(dz_off)
            dcs_x_parts.append(dyv[:, gs] * zoff * ecs_x[:, gs])
            dcg = _dot_nt(dz_off_m, sg_m)
            dsprev = _dot_tn(cg, dz_off_m)
            w_st = dec_x[:, gs] * xdt[:, gs]
            dw_st = _dot(bg, dsg_m)
            dbg = _dot_nt(_mx(w_st), dsg_m)
            dxdt_g = dec_x[:, gs] * dw_st
            ddec_x_parts.append(dw_st * xdt[:, gs])
            dsprev = dsprev + cdec_x[:, gs] * dsg
            dcl_x_parts.append(jnp.sum(dsg * sg, axis=0, keepdims=True) * cdec_x[:, gs])
            ds_ref[:, gs] = dsprev
            dcb = jnp.zeros((CH, CH), _F32)
            dxdt_pairs = []
            for pr in range(4):
                ps = slice(g * 512 + pr * 128, g * 512 + (pr + 1) * 128)
                acc_pair = None
                for hh in range(2):
                    h = g * 8 + pr * 2 + hh
                    in_head = (lane < SSD_P) if hh == 0 else (lane >= SSD_P)
                    lm = _head_lm(csb_ref, cst_ref, h, tril)
                    m_h = cbm * lm
                    m_hm = _mx(m_h)
                    dyh_m = _mx(jnp.where(in_head, dyv[:, ps], 0.0))
                    dm = _dot_nt(dyh_m, xdt_m[:, ps])
                    dcb = dcb + dm * lm
                    qm = dm * m_h
                    rc = _xdot(_dot, qm, ones_bf) - _xdot(_dot_tn, qm, ones_bf)
                    dcs = dcs + jnp.where(lane == h, rc, 0.0)
                    contrib = jnp.where(in_head, _dot_tn(m_hm, dyv_m[:, ps]), 0.0)
                    acc_pair = contrib if acc_pair is None else acc_pair + contrib
                dxdt_pairs.append(acc_pair)
            dxdt_parts.append(dxdt_g + jnp.concatenate(dxdt_pairs, axis=1))
            dcb_m = _mx(dcb)
            dc_parts.append(dcg + _dot(dcb_m, bg))
            db_parts.append(dbg + _dot_tn(dcb_m, cg))
        dxdt = jnp.concatenate(dxdt_parts, axis=1)
        dxv = dx_ref[...] * dyv + dxdt * dt_x
        et = et_ref[...]
        ddt = _xdot(_dot, dxdt * xv, et)
        dcs = dcs + _xdot(_dot, jnp.concatenate(dcs_x_parts, axis=1), et)
        ddec = _xdot(_dot, jnp.concatenate(ddec_x_parts, axis=1), et) * q["dec"]
        dcs = dcs - ddec
        dcl = jnp.sum(ddec, axis=0, keepdims=True) + _xdot(
            _dot, jnp.broadcast_to(jnp.concatenate(dcl_x_parts, axis=1), (8, SSD_W)), et)[0:1, :]
        dcs = jnp.where(row == CH - 1, dcs + dcl, dcs)
        da = _xdot_left(_dot_tn, ltri_ref[...], dcs)
        ddt = ddt + da * q["a_neg"]
        dalog_ref[...] += _sum8(da * q["dt"] * q["a_neg"])
        ddtraw = jnp.where(lane < SSD_H, ddt * _sigmoid(q["dtin"]), 0.0)
        ddtb_ref[...] += _sum8(ddtraw)
        dproj_ref[:, D_IN_PAD - 128:D_IN_PAD] = ddtraw.astype(dproj_ref.dtype)
        dcpre = jnp.concatenate([dxv] + db_parts + dc_parts, axis=1) * _silu_grad(q["cpre"])
        dcb_ref[...] += _sum8(dcpre)
        for k in range(CONV_K):
            dcw_ref[k:k + 1, :] += jnp.sum(dcpre * q["xs"][CONV_K - 1 - k], axis=0, keepdims=True)
        next8 = dnext_ref[...]
        dxbc = sum(cw_ref[k:k + 1, :] * _shift_up(dcpre, next8, CONV_K - 1 - k) for k in range(CONV_K))
        dproj_ref[:, 2 * GM_W + SSD_W:2 * GM_W + SSD_W + CONV_CH] = dxbc.astype(dproj_ref.dtype)
        dnext_ref[...] = dcpre[0:8, :]

        @pl.when(i == n_chunks - 1)
        def _():
            for h in range(GM_H):
                dbs_ref[h:h + 1, :] = _xdot_left(_dot_nt, jnp.ones((8, 128), _BF16), dbacc_ref[h])[0:1, :]
            dd_ref[...] = _xdot(_dot, ddacc_ref[...], et)

    rblk = lambda w, j: pl.BlockSpec((CH, w), lambda i: (n_chunks - 1 - i, j))
    full = lambda arr: pl.BlockSpec(arr.shape, lambda i: (0,) * arr.ndim)
    acc = lambda shape: pl.BlockSpec(shape, lambda i: (0,) * len(shape))
    consts = [gv, ws, bsb, gout, cw8, cb, dtb, alog, d_x, ng, e_mat, et_mat, e2_mat, ltri_mat]
    acc_shapes = [(GM_H, CH, CH), (8, 128), (8, GM_W), (8, GM_W), (8, SSD_W), (8, CONV_CH), (8, CONV_CH), (8, 128),
                  (8, 128), (8, 128)]
    return pl.pallas_call(
        body, name="mixer_bwd", grid=(n_chunks,),
        in_specs=[rblk(GM_W, 0), rblk(GM_W, 1), rblk(SSD_W, 2), rblk(CONV_CH, 2), rblk(128, DT_BLK),
                  pl.BlockSpec((8, CONV_CH), lambda i: (jnp.maximum((n_chunks - 1 - i) * (CH // 8) - 1, 0), 2)),
                  rblk(2 * D, 0), rblk(SSD_W, 0), rblk(SSD_W, 0)] + [full(a) for a in consts],
        out_specs=[rblk(D_IN_PAD, 0)] + [acc(s) for s in acc_shapes],
        out_shape=[jax.ShapeDtypeStruct((t, D_IN_PAD), _MXU)] + [jax.ShapeDtypeStruct(s, _F32) for s in acc_shapes],
        scratch_shapes=[pltpu.VMEM((SSD_N, SSD_W), _F32), pltpu.VMEM((8, CONV_CH), _F32),
                        pltpu.VMEM((CH, SSD_H * 128), _F32), pltpu.VMEM((128, CH), _F32),
                        pltpu.VMEM((GM_H, CH, 128), _F32), pltpu.VMEM((8, SSD_W), _F32)],
        compiler_params=pltpu.CompilerParams(dimension_semantics=("arbitrary",), vmem_limit_bytes=48 << 20),
    )(proj, proj, proj, proj, proj, proj, dcat, yss, states, *consts)


def _peers():
    x, y, c = lax.axis_index("x"), lax.axis_index("y"), lax.axis_index("c")
    out = []
    for k in range(1, N_DEV):
        fx, fy, fc = (k >> 2) & 1, (k >> 1) & 1, k & 1
        px, py, pc = (x + fx) % 2, (y + fy) % 2, (c + fc) % 2
        out.append((k - 1, (px, py, pc), 4 * px + 2 * py + pc))
    return out, 4 * x + 2 * y + c


def _all_gather(shard):
    r, w = shard.shape

    def body(src_ref, out_ref, send_sems, recv_sems, local_sem):
        peers, me = _peers()
        mine = pltpu.make_async_copy(src_ref, out_ref.at[me], local_sem)
        mine.start()
        copies = [pltpu.make_async_remote_copy(src_ref=src_ref, dst_ref=out_ref.at[me], send_sem=send_sems.at[k],
                                               recv_sem=recv_sems.at[k], device_id=pid, device_id_type=_MESH)
                  for k, pid, _ in peers]
        for cp in copies:
            cp.start()
        for cp in copies:
            cp.wait()
        mine.wait()

    return pl.pallas_call(
        body, name="all_gather_weights",
        in_specs=[pl.BlockSpec(memory_space=pl.ANY)], out_specs=pl.BlockSpec(memory_space=pl.ANY),
        out_shape=jax.ShapeDtypeStruct((N_DEV, r, w), shard.dtype),
        scratch_shapes=[pltpu.SemaphoreType.DMA((N_DEV - 1,)), pltpu.SemaphoreType.DMA((N_DEV - 1,)),
                        pltpu.SemaphoreType.DMA],
    )(shard)


def _exchange_blocks(blocks):
    _, r, w = blocks.shape

    def body(src_ref, out_ref, send_sems, recv_sems, local_sem):
        peers, me = _peers()
        mine = pltpu.make_async_copy(src_ref.at[me], out_ref.at[me], local_sem)
        mine.start()
        copies = [pltpu.make_async_remote_copy(src_ref=src_ref.at[pidx], dst_ref=out_ref.at[me],
                                               send_sem=send_sems.at[k], recv_sem=recv_sems.at[k], device_id=pid,
                                               device_id_type=_MESH)
                  for k, pid, pidx in peers]
        for cp in copies:
            cp.start()
        for cp in copies:
            cp.wait()
        mine.wait()

    return pl.pallas_call(
        body, name="exchange_weight_grads",
        in_specs=[pl.BlockSpec(memory_space=pl.ANY)], out_specs=pl.BlockSpec(memory_space=pl.ANY),
        out_shape=jax.ShapeDtypeStruct(blocks.shape, blocks.dtype),
        scratch_shapes=[pltpu.SemaphoreType.DMA((N_DEV - 1,)), pltpu.SemaphoreType.DMA((N_DEV - 1,)),
                        pltpu.SemaphoreType.DMA],
    )(blocks)


def _adam_vals(w, g, m, v):
    m = B1 * m + (1.0 - B1) * g
    v = B2 * v + (1.0 - B2) * (g * g)
    m_hat = m / (1.0 - B1 ** STEP)
    v_hat = v / (1.0 - B2 ** STEP)
    delta = -LR * (m_hat / (jnp.sqrt(v_hat) + ADAM_EPS) + WD * w)
    return delta, m, v


def _sum_adam(recv, w, m, v, tr=256):
    _, r, wd = recv.shape

    def body(recv_ref, w_ref, m_ref, v_ref, g_out, d_out, m_out, v_out):
        g = recv_ref[0]
        for s in range(1, N_DEV):
            g = g + recv_ref[s]
        d_, m_, v_ = _adam_vals(w_ref[...], g, m_ref[...], v_ref[...])
        g_out[...] = g
        d_out[...] = d_
        m_out[...] = m_
        v_out[...] = v_

    spec = pl.BlockSpec((tr, wd), lambda i: (i, 0))
    return pl.pallas_call(
        body, name="sum_adam", grid=(r // tr,),
        in_specs=[pl.BlockSpec((N_DEV, tr, wd), lambda i: (0, i, 0)), spec, spec, spec],
        out_specs=[spec] * 4, out_shape=[jax.ShapeDtypeStruct((r, wd), _F32)] * 4,
        compiler_params=pltpu.CompilerParams(dimension_semantics=("parallel",), vmem_limit_bytes=48 << 20),
    )(recv, w, m, v)


def _small_reduce_adam(parts, segments, n_rows, loss_row, w, m, v):
    def body(parts_ref, w_ref, m_ref, v_ref, g_out, d_out, m_out, v_out, loc_ref, recv_ref, send_sems, recv_sems):
        peers, me = _peers()
        loc_ref[...] = jnp.zeros_like(loc_ref)
        for out_row, n_out, in_row, n_in, kind in segments:
            if kind == "copy":
                loc_ref[out_row:out_row + n_out, :] = parts_ref[in_row:in_row + n_in, :]
            else:
                s = jnp.sum(parts_ref[in_row:in_row + n_in, :], axis=0, keepdims=True)
                if kind == "loss":
                    s = jnp.broadcast_to(jnp.sum(s, axis=1, keepdims=True) * (0.5 / D), (1, D))
                loc_ref[out_row:out_row + 1, :] = s
        recv_ref[me] = loc_ref[...]
        copies = [pltpu.make_async_remote_copy(src_ref=loc_ref, dst_ref=recv_ref.at[me], send_sem=send_sems.at[k],
                                               recv_sem=recv_sems.at[k], device_id=pid, device_id_type=_MESH)
                  for k, pid, _ in peers]
        for cp in copies:
            cp.start()
        for cp in copies:
            cp.wait()
        g = recv_ref[0]
        for s in range(1, N_DEV):
            g = g + recv_ref[s]
        d_, m_, v_ = _adam_vals(w_ref[...], g, m_ref[...], v_ref[...])
        g_out[...] = g
        d_out[...] = d_
        m_out[...] = m_
        v_out[...] = v_

    vm = pl.BlockSpec(memory_space=pltpu.VMEM)
    return pl.pallas_call(
        body, name="small_reduce_adam", in_specs=[vm] * 4, out_specs=[vm] * 4,
        out_shape=[jax.ShapeDtypeStruct((n_rows, D), _F32)] * 4,
        scratch_shapes=[pltpu.VMEM((n_rows, D), _F32), pltpu.VMEM((N_DEV, n_rows, D), _F32),
                        pltpu.SemaphoreType.DMA((N_DEV - 1,)), pltpu.SemaphoreType.DMA((N_DEV - 1,))],
        compiler_params=pltpu.CompilerParams(vmem_limit_bytes=48 << 20),
    )(parts, w, m, v)


_BIG = (("w_in", SHARD_IN, True), ("w_out", 256, False), ("w_ff1", 512, True), ("w_ff2", 512, False),
        ("w_ple_gate", 128, False), ("w_ple_proj", 32, True))
_BIG_ROWS = 2048


_BIG_USED = sum(n for _, n, _ in _BIG)


def _pack_shards(shards, extra=None):
    rows = [shards[name].reshape(n, D) for name, n, _ in _BIG]
    used = _BIG_USED
    if extra is not None:
        rows.append(extra)
        used += extra.shape[0]
    rows.append(jnp.zeros((_BIG_ROWS - used, D), rows[0].dtype))
    return jnp.concatenate(rows, axis=0)


def _unpack_shards(packed, shapes):
    out, r0 = {}, 0
    for name, n, _ in _BIG:
        out[name] = packed[r0:r0 + n].reshape(shapes[name])
        r0 += n
    return out


def _full_from_gathered(gathered):
    out, r0 = {}, 0
    for name, n, by_cols in _BIG:
        blk = gathered[:, r0:r0 + n]
        r0 += n
        if by_cols:
            rows_full = {"w_in": D, "w_ff1": D, "w_ple_proj": D_PLE}[name]
            cols = n * D // rows_full
            out[name] = blk.reshape(N_DEV, rows_full, cols).transpose(1, 0, 2).reshape(rows_full, N_DEV * cols)
        else:
            out[name] = blk.reshape(N_DEV * n, D)
    return out


def _blocks_from_full(full):
    parts = []
    for name, n, by_cols in _BIG:
        g = full[name]
        if by_cols:
            rows_full = g.shape[0]
            cols = g.shape[1] // N_DEV
            g = g.reshape(rows_full, N_DEV, cols).transpose(1, 0, 2)
        parts.append(g.reshape(N_DEV, n, D))
    used = sum(n for _, n, _ in _BIG)
    parts.append(jnp.zeros((N_DEV, _BIG_ROWS - used, D), _F32))
    return jnp.concatenate(parts, axis=1)


_G_VECS = ("norm_mix_g", "gm_v_norm_g", "gm_out_norm_g", "ssd_norm_g", "norm_mlp_g", "ple_norm_g", "final_norm_g")


def _const_mats():
    h = np.arange(128)[:, None]
    ch = np.arange(SSD_W)[None, :]
    e = (ch // SSD_P == h).astype(np.float32)
    j = np.arange(SSD_H * 128)[None, :]
    e2 = (j // 128 == h).astype(np.float32)
    ltri = (np.arange(CH)[:, None] >= np.arange(CH)[None, :]).astype(np.float32)
    return jnp.asarray(e, _BF16), jnp.asarray(e.T, _BF16), jnp.asarray(e2, _BF16), jnp.asarray(ltri, _BF16)


def _pad_lanes(v, n=128):
    return jnp.pad(v, ((0, 0), (0, n - v.shape[1])))


def _local_step(x, p, tgt, wts, small, seq_len):
    t = x.shape[0]
    seq_chunks = seq_len // CH
    e_mat, et_mat, e2_mat, ltri_mat = _const_mats()
    g_mix, g_mlp, g_ple = small["norm_mix_g"], small["norm_mlp_g"], small["ple_norm_g"]
    g_fin = small["final_norm_g"].reshape(1, D)
    w_in = jnp.pad(wts["w_in"], ((0, 0), (0, D_IN_PAD - D_IN)))
    w_out, w1, w2, wg, wp = wts["w_out"], wts["w_ff1"], wts["w_ff2"], wts["w_ple_gate"], wts["w_ple_proj"]
    gv, gout, ng = small["gm_v_norm_g"], small["gm_out_norm_g"], small["ssd_norm_g"]
    ws = small["gm_ws"][0]
    bsb = jnp.broadcast_to(small["gm_bs"][0][:, :, None], (GM_H, CH, 128))
    cw8 = jnp.pad(small["ssd_conv_w_full"], ((0, 8 - CONV_K), (0, 0)))
    cb = small["ssd_conv_b"]
    dtb, alog = _pad_lanes(small["ssd_dt_bias"]), _pad_lanes(small["ssd_a_log"])
    d_x = jnp.repeat(small["ssd_d"], SSD_P, axis=1)
    mix_consts = (gv, ws, bsb, gout, cw8, cb, dtb, alog, d_x, ng)

    first = lambda acc: (acc,)
    n1 = _norm_cast("norm_mix", x, g_mix)
    (proj,) = _matmul("proj_in", n1, w_in, "nn", 256, D_IN_PAD, D, first, [("tile", _F32)])
    cat, yss, states = _mixer_fwd(proj, *mix_consts, e_mat, e2_mat, ltri_mat, seq_chunks)

    def epi_res_norm(acc, res, g):
        hv = acc + res
        return hv, _rms(hv, g)

    h1, n2 = _matmul("proj_out", cat, w_out, "nn", 512, D, 2 * D, epi_res_norm, [("tile", _F32), ("tile", _MXU)],
                     extras=[(x, "tile"), (g_mlp, "row")])

    def epi_relu2(acc):
        hid = jnp.maximum(acc, 0.0)
        return hid, hid * hid

    hid, hid2 = _matmul("ff1", n2, w1, "nn", 512, 1024, D, epi_relu2, [("tile", _MXU), ("tile", _MXU)])
    h2, n3 = _matmul("ff2", hid2, w2, "nn", 512, D, 1024, epi_res_norm, [("tile", _F32), ("tile", _MXU)],
                     extras=[(h1, "tile"), (g_ple, "row")])
    (pp,) = _matmul("ple_proj", p, wp, "nn", 512, D, D_PLE, first, [("tile", _F32)])

    def epi_head(acc, ppv, h2v, tg, gf):
        gate = _sigmoid(acc)
        h3 = h2v + gate * ppv
        r = lax.rsqrt(jnp.mean(h3 * h3, axis=-1, keepdims=True) + EPS)
        yv = h3 * r * gf
        err = yv - tg
        dy = err * (1.0 / D)
        dh3, dgf8 = _rms_bwd(h3, gf, dy)
        da3 = dh3 * ppv * gate * (1.0 - gate)
        return dh3, da3, dh3 * gate, _sum8(err * err), dgf8

    dh3, da3, dpp, lossp, dgfin = _matmul(
        "ple_gate_loss", n3, wg, "nn", 512, D, D, epi_head,
        [("tile", _F32), ("tile", _MXU), ("tile", _MXU), ("part8", _F32), ("part8", _F32)],
        extras=[(pp, "tile"), (h2, "tile"), (tgt, "tile"), (g_fin, "row")])

    (dwp,) = _matmul("d_w_ple_proj", p, dpp, "tn", D_PLE, D, 512, first, [("tile", _F32)])
    (dwg,) = _matmul("d_w_ple_gate", n3, da3, "tn", D, D, 512, first, [("tile", _F32)])

    def epi_norm_bwd(acc, up, hv, g):
        dx, dg8 = _rms_bwd(hv, g, acc)
        dh = up + dx
        return dh, dh, dg8

    dh2, dh2b, dgple = _matmul("d_h2", da3, wg, "nt", 512, D, D, epi_norm_bwd,
                               [("tile", _F32), ("tile", _MXU), ("part8", _F32)],
                               extras=[(dh3, "tile"), (h2, "tile"), (g_ple, "row")])
    (dw2,) = _matmul("d_w_ff2", hid2, dh2b, "tn", 1024, D, 512, first, [("tile", _F32)])
    (da1,) = _matmul("d_ff_hidden", dh2b, w2, "nt", 512, 1024, D, lambda acc, hv: (acc * 2.0 * hv.astype(_F32),),
                     [("tile", _MXU)], extras=[(hid, "tile")])
    (dw1,) = _matmul("d_w_ff1", n2, da1, "tn", 1024, 1024, 512, first, [("tile", _F32)])
    dh1, dh1b, dgmlp = _matmul("d_h1", da1, w1, "nt", 512, D, 1024, epi_norm_bwd,
                               [("tile", _F32), ("tile", _MXU), ("part8", _F32)],
                               extras=[(dh2, "tile"), (h1, "tile"), (g_mlp, "row")])
    (dwout,) = _matmul("d_w_out", cat, dh1b, "tn", 1024, D, 512, first, [("tile", _F32)])
    (dcat,) = _matmul("d_cat", dh1b, w_out, "nt", 512, 1024, D, first, [("tile", _F32)])
    (dproj, dws, dbs, dgv, dgout, dng, dcw, dcb, ddtb, dalog, dd) = _mixer_bwd(
        proj, dcat, yss, states, *mix_consts, e_mat, et_mat, e2_mat, ltri_mat, seq_chunks)
    (dwin,) = _matmul("d_w_in", n1, dproj, "tn", 512, D_IN_PAD, 256, first, [("tile", _F32)])
    gx, _, dgmix = _matmul("d_x", dproj, w_in, "nt", 256, D, D_IN_PAD, epi_norm_bwd,
                           [("tile", _F32), ("tile", _MXU), ("part8", _F32)],
                           extras=[(dh1, "tile"), (x, "tile"), (g_mix, "row")])

    big = dict(w_in=dwin[:, :D_IN], w_out=dwout, w_ff1=dw1, w_ff2=dw2, w_ple_gate=dwg, w_ple_proj=dwp)
    pieces = dict(norm_mix_g=dgmix, gm_v_norm_g=dgv, gm_out_norm_g=dgout, ssd_norm_g=dng, norm_mlp_g=dgmlp,
                  ple_norm_g=dgple, final_norm_g=dgfin, gm_ws=dws, gm_bs=dbs, ssd_conv_w=dcw, ssd_conv_b=dcb,
                  ssd_dt_bias=ddtb, ssd_a_log=dalog, ssd_d=dd, loss=lossp)
    return gx, big, pieces


def _small_layout(pieces):
    rows, segments = [], []
    in_row, out_row = 0, 0

    def add(arr, kind, n_out):
        nonlocal in_row, out_row
        rows.append(arr)
        segments.append((out_row, n_out, in_row, arr.shape[0], kind))
        start = out_row
        in_row += arr.shape[0]
        out_row += n_out
        return start

    where = {}
    for name in _G_VECS:
        where[name] = add(pieces[name], "sum", 1)
    where["gm_ws"] = add(pieces["gm_ws"].reshape(GM_H * CH * CH // D, D), "copy", GM_H * CH * CH // D)
    where["gm_bs"] = add(pieces["gm_bs"].reshape(1, D), "copy", 1)
    cb = jnp.pad(pieces["ssd_conv_b"], ((0, 0), (0, 2 * D - CONV_CH)))
    where["ssd_conv_b"] = add(cb[:, :D], "sum", 1)
    add(cb[:, D:], "sum", 1)
    cw = jnp.pad(pieces["ssd_conv_w"][:CONV_K], ((0, 0), (0, 2 * D - CONV_CH)))
    where["ssd_conv_w"] = add(cw.reshape(2 * CONV_K, D), "copy", 2 * CONV_K)
    misc = jnp.concatenate([pieces["ssd_dt_bias"], pieces["ssd_a_log"], pieces["ssd_d"],
                            jnp.zeros((8, D - 3 * 128), _F32)], axis=1)
    where["misc"] = add(misc, "sum", 1)
    where["loss"] = add(pieces["loss"], "loss", 1)
    n_rows = -(-out_row // 8) * 8
    return jnp.concatenate(rows, axis=0), tuple(segments), n_rows, where


def _pack_small_params(vals, where, n_rows, my_block):
    buf = jnp.zeros((n_rows, D), _F32)

    def put(b, row, arr):
        return lax.dynamic_update_slice(b, arr, (row, 0))

    for name in _G_VECS:
        buf = put(buf, where[name], vals[name].reshape(1, D))
    buf = put(buf, where["gm_ws"], vals["gm_ws"].reshape(GM_H * CH * CH // D, D))
    buf = put(buf, where["gm_bs"], vals["gm_bs"].reshape(1, D))
    cb = jnp.pad(vals["ssd_conv_b"].reshape(1, CONV_CH), ((0, 0), (0, 2 * D - CONV_CH)))
    buf = put(buf, where["ssd_conv_b"], cb.reshape(2, D))
    cw = lax.dynamic_update_slice(jnp.zeros((CONV_K, 2 * D), _F32), vals["ssd_conv_w"].reshape(CONV_K, -1),
                                  (0, my_block * (CONV_CH // N_DEV)))
    buf = put(buf, where["ssd_conv_w"], cw.reshape(2 * CONV_K, D))
    misc = jnp.concatenate([_pad_lanes(vals["ssd_dt_bias"].reshape(1, SSD_H)),
                            _pad_lanes(vals["ssd_a_log"].reshape(1, SSD_H)),
                            _pad_lanes(vals["ssd_d"].reshape(1, SSD_H)), jnp.zeros((1, D - 3 * 128), _F32)], axis=1)
    buf = put(buf, where["misc"], misc)
    return buf


def _unpack_small(buf, where, my_block, shapes):
    out = {}
    for name in _G_VECS:
        out[name] = buf[where[name]].reshape(shapes[name])
    n_ws = GM_H * CH * CH // D
    out["gm_ws"] = buf[where["gm_ws"]:where["gm_ws"] + n_ws].reshape(shapes["gm_ws"])
    out["gm_bs"] = buf[where["gm_bs"]].reshape(shapes["gm_bs"])
    r = where["ssd_conv_b"]
    out["ssd_conv_b"] = buf[r:r + 2].reshape(1, 2 * D)[:, :CONV_CH].reshape(shapes["ssd_conv_b"])
    r = where["ssd_conv_w"]
    cw = buf[r:r + 2 * CONV_K].reshape(CONV_K, 2 * D)
    out["ssd_conv_w"] = lax.dynamic_slice(cw, (0, my_block * (CONV_CH // N_DEV)),
                                          (CONV_K, CONV_CH // N_DEV)).reshape(shapes["ssd_conv_w"])
    misc = buf[where["misc"]]
    for i, name in enumerate(("ssd_dt_bias", "ssd_a_log", "ssd_d")):
        out[name] = misc[i * 128:i * 128 + SSD_H].reshape(shapes[name])
    return out


_WEIGHTS = ("norm_mix_g", "w_in", "gm_v_norm_g", "gm_ws", "gm_bs", "gm_out_norm_g", "ssd_conv_w", "ssd_conv_b",
            "ssd_dt_bias", "ssd_a_log", "ssd_d", "ssd_norm_g", "w_out", "norm_mlp_g", "w_ff1", "w_ff2", "ple_norm_g",
            "w_ple_gate", "w_ple_proj", "final_norm_g")
_BIG_NAMES = tuple(n for n, _, _ in _BIG)


def kernel(x, p, norm_mix_g, w_in, gm_v_norm_g, gm_ws, gm_bs, gm_out_norm_g, ssd_conv_w, ssd_conv_b, ssd_dt_bias, ssd_a_log, ssd_d, ssd_norm_g, w_out, norm_mlp_g, w_ff1, w_ff2, ple_norm_g, w_ple_gate, w_ple_proj, final_norm_g, loss_target, m_norm_mix_g, m_w_in, m_gm_v_norm_g, m_gm_ws, m_gm_bs, m_gm_out_norm_g, m_ssd_conv_w, m_ssd_conv_b, m_ssd_dt_bias, m_ssd_a_log, m_ssd_d, m_ssd_norm_g, m_w_out, m_norm_mlp_g, m_w_ff1, m_w_ff2, m_ple_norm_g, m_w_ple_gate, m_w_ple_proj, m_final_norm_g, v_norm_mix_g, v_w_in, v_gm_v_norm_g, v_gm_ws, v_gm_bs, v_gm_out_norm_g, v_ssd_conv_w, v_ssd_conv_b, v_ssd_dt_bias, v_ssd_a_log, v_ssd_d, v_ssd_norm_g, v_w_out, v_norm_mlp_g, v_w_ff1, v_w_ff2, v_ple_norm_g, v_w_ple_gate, v_w_ple_proj, v_final_norm_g):
    args = dict(locals())
    w = {n: args[n] for n in _WEIGHTS}
    m = {n: args["m_" + n] for n in _WEIGHTS}
    v = {n: args["v_" + n] for n in _WEIGHTS}
    shapes = {n: w[n].shape for n in _WEIGHTS}
    my_block = 4 * lax.axis_index("x") + 2 * lax.axis_index("y") + lax.axis_index("c")
    nb, seq_len, _ = x.shape

    big_shard = {n: w[n][0] for n in _BIG_NAMES}
    cw_cols = CONV_CH // N_DEV
    cw_bits = lax.bitcast_convert_type(ssd_conv_w[0], _BF16).reshape(1, 2 * CONV_K * cw_cols)
    cw_rows = jnp.pad(cw_bits, ((0, 0), (0, 2 * D - 2 * CONV_K * cw_cols))).reshape(2, D)
    gathered = _all_gather(_pack_shards({n: a.astype(_BF16) for n, a in big_shard.items()}, extra=cw_rows))
    wts = {n: a.astype(_MXU) for n, a in _full_from_gathered(gathered).items()}
    cw_all = gathered[:, _BIG_USED:_BIG_USED + 2].reshape(N_DEV, 2 * D)[:, :2 * CONV_K * cw_cols]
    cw_all = lax.bitcast_convert_type(cw_all.reshape(N_DEV, CONV_K, cw_cols, 2), _F32)
    conv_w_full = cw_all.transpose(1, 0, 2).reshape(CONV_K, CONV_CH)

    small = {n: w[n] for n in _WEIGHTS if n not in _BIG_NAMES}
    small["ssd_conv_w_full"] = conv_w_full
    gx, big_grads, pieces = _local_step(x.reshape(nb * seq_len, D), p.reshape(nb * seq_len, D_PLE),
                                        loss_target.reshape(nb * seq_len, D), wts, small, seq_len)

    recv = _exchange_blocks(_blocks_from_full(big_grads))
    pack32 = lambda d: _pack_shards({n: d[n][0] for n in _BIG_NAMES})
    big_out = [_unpack_shards(a, shapes) for a in _sum_adam(recv, pack32(w), pack32(m), pack32(v))]

    parts, segments, n_rows, where = _small_layout(pieces)
    packs = [_pack_small_params(d, where, n_rows, my_block) for d in (w, m, v)]
    small_res = _small_reduce_adam(parts, segments, n_rows, where["loss"], *packs)
    loss = small_res[0][where["loss"], 0]
    small_out = [_unpack_small(a, where, my_block, shapes) for a in small_res]

    outs = [loss, gx.reshape(x.shape)]
    for k in range(4):
        outs += [big_out[k][n] if n in _BIG_NAMES else small_out[k][n] for n in _WEIGHTS]
    return tuple(outs)
```

```python
import functools
import math

import jax
import jax.numpy as jnp
import numpy as np
from jax import lax
from jax.experimental import pallas as pl
from jax.experimental.pallas import tpu as pltpu

_F32 = jnp.float32
_BF16 = jnp.bfloat16
_MXU = jnp.bfloat16

D = 1024
D_PLE = 256
GM_W = 1024
GM_H = 8
CH = 128
SSD_W = 1024
SSD_H = 16
SSD_P = 64
SSD_G = 2
SSD_N = 128
CONV_K = 4
CONV_CH = SSD_W + 2 * SSD_G * SSD_N
D_FF = 4096
D_IN = 2 * GM_W + SSD_W + CONV_CH + SSD_H
D_IN_PAD = 4736
DT_BLK = (D_IN_PAD - 128) // 128
EPS = 1e-6
N_DEV = 8
SHARD_IN = D_IN // N_DEV

LR, B1, B2, ADAM_EPS, WD, STEP = 0.001, 0.9, 0.999, 1e-08, 0.01, 10

_V7X_VMEM_BYTES = 64 * 1024 * 1024
_VMEM_CAP = _V7X_VMEM_BYTES - 8 * 1024 * 1024
_MESH = pl.DeviceIdType.MESH


def _vmem_limit(nbytes):
    return int(min(_VMEM_CAP, max(32 * 1024 * 1024, nbytes * 5 // 4 + (4 << 20))))


def _nbytes(shape, dtype):
    return int(np.prod(shape)) * jnp.dtype(dtype).itemsize


def _mx(v):
    return v.astype(_MXU)


def _dot(a, b):
    return jnp.dot(a, b, preferred_element_type=_F32)


def _dot_nt(a, b):
    return lax.dot_general(a, b, (((1,), (1,)), ((), ())), preferred_element_type=_F32)


def _dot_tn(a, b):
    return lax.dot_general(a, b, (((0,), (0,)), ((), ())), preferred_element_type=_F32)


def _split3(a):
    hi = a.astype(_BF16)
    r = a - hi.astype(_F32)
    mid = r.astype(_BF16)
    lo = (r - mid.astype(_F32)).astype(_BF16)
    return hi, mid, lo


def _xdot(dotfn, a, b01):
    b = b01.astype(_BF16)
    hi, mid, lo = _split3(a)
    return (dotfn(hi, b) + dotfn(mid, b)) + dotfn(lo, b)


def _xdot_left(dotfn, a01, b):
    a = a01.astype(_BF16)
    hi, mid, lo = _split3(b)
    return (dotfn(a, hi) + dotfn(a, mid)) + dotfn(a, lo)


def _sum8(v):
    r, n = v.shape
    return v.reshape(r // 8, 8, n).sum(axis=0)


def _sigmoid(v):
    return 1.0 / (1.0 + jnp.exp(-v))


def _gelu(v):
    return 0.5 * v * (1.0 + lax.erf(v * 0.7071067811865476))


def _gelu_grad(v):
    return 0.5 * (1.0 + lax.erf(v * 0.7071067811865476)) + v * jnp.exp(-0.5 * v * v) * 0.3989422804014327


def _rms(xv, g):
    ms = jnp.mean(xv * xv, axis=-1, keepdims=True)
    return xv * lax.rsqrt(ms + EPS) * g


def _rms_bwd(xv, g, dn):
    n = xv.shape[-1]
    r = lax.rsqrt(jnp.mean(xv * xv, axis=-1, keepdims=True) + EPS)
    gy = dn * g
    dot = jnp.sum(gy * xv, axis=-1, keepdims=True)
    dx = r * gy - xv * (r * r * r) * (dot * (1.0 / n))
    return dx, _sum8(dn * xv * r)


def _iota2(shape, axis):
    return lax.broadcasted_iota(jnp.int32, shape, axis)


def _norm_cast(name, x, g, tm=512, comm=()):
    t, n = x.shape
    tm = min(tm, t)
    steps = t // tm
    kinds = [kind for kind, _ in comm]
    c_in, c_in_specs, c_out_specs, c_out_shape, c_scratch = _comm_io(comm)

    def body(*refs):
        x_ref, g_ref = refs[0], refs[1]
        o_ref = refs[2 + len(comm)]
        copies = lambda: _comm_copies(kinds, refs[2:2 + len(comm)], refs[3 + len(comm):3 + 2 * len(comm)],
                                      *refs[3 + 2 * len(comm):])
        if comm:
            @pl.when(pl.program_id(0) == 0)
            def _():
                for cp in copies():
                    cp.start()

        o_ref[...] = _rms(x_ref[...], g_ref[...]).astype(o_ref.dtype)
        if comm:
            @pl.when(pl.program_id(0) == steps - 1)
            def _():
                for cp in copies():
                    cp.wait()

    res = pl.pallas_call(
        body, name=name, grid=(steps,),
        in_specs=[pl.BlockSpec((tm, n), lambda i: (i, 0)), pl.BlockSpec((1, n), lambda i: (0, 0))] + c_in_specs,
        out_specs=[pl.BlockSpec((tm, n), lambda i: (i, 0))] + c_out_specs,
        out_shape=[jax.ShapeDtypeStruct((t, n), _MXU)] + c_out_shape, scratch_shapes=c_scratch,
        compiler_params=pltpu.CompilerParams(dimension_semantics=("arbitrary",)),
    )(x, g, *c_in)
    return res[0], res[1:]


def _matmul(name, a, b, mode, tm, tn, tk, epilogue, outs, extras=(), b_blocked=False, comm=()):
    m, k = a.shape[::-1] if mode == "tn" else a.shape
    if b_blocked:
        if mode == "nn":
            tn, n = b.shape[2], b.shape[0] * b.shape[2]
        else:
            tk, n = b.shape[2], b.shape[1]
            assert mode == "nt" and k == b.shape[0] * tk
    else:
        n = b.shape[0] if mode == "nt" else b.shape[1]
    tm, tn, tk = min(tm, m), min(tn, n), min(tk, k)
    assert m % tm == 0 and n % tn == 0 and k % tk == 0, (name, m, n, k, tm, tn, tk)
    if mode == "nn":
        a_spec = pl.BlockSpec((tm, tk), lambda i, j, kk: (i, kk))
        b_spec = (pl.BlockSpec((None, tk, tn), lambda i, j, kk: (j, kk, 0)) if b_blocked
                  else pl.BlockSpec((tk, tn), lambda i, j, kk: (kk, j)))
        dotfn = _dot
    elif mode == "nt":
        a_spec = pl.BlockSpec((tm, tk), lambda i, j, kk: (i, kk))
        b_spec = (pl.BlockSpec((None, tn, tk), lambda i, j, kk: (kk, j, 0)) if b_blocked
                  else pl.BlockSpec((tn, tk), lambda i, j, kk: (j, kk)))
        dotfn = _dot_nt
    else:
        a_spec = pl.BlockSpec((tk, tm), lambda i, j, kk: (kk, i))
        b_spec = pl.BlockSpec((tk, tn), lambda i, j, kk: (kk, j))
        dotfn = _dot_tn
    ni, nj, nk = m // tm, n // tn, k // tk
    n_ex, n_out, n_comm = len(extras), len(outs), len(comm)
    kinds = [kind for kind, _ in comm]
    c_in, c_in_specs, c_out_specs, c_out_shape, c_scratch = _comm_io(comm)

    in_specs, vmem = [a_spec, b_spec], 2 * (tm * tk * a.dtype.itemsize + tk * tn * b.dtype.itemsize)
    for arr, kind in extras:
        if kind == "tile":
            in_specs.append(pl.BlockSpec((tm, tn), lambda i, j, kk: (i, j)))
            vmem += 2 * _nbytes((tm, tn), arr.dtype)
        else:
            in_specs.append(pl.BlockSpec((1, tn), lambda i, j, kk: (0, j)))
    out_specs, out_shape = [], []
    for kind, dt in outs:
        if kind == "tile":
            out_specs.append(pl.BlockSpec((tm, tn), lambda i, j, kk: (i, j)))
            out_shape.append(jax.ShapeDtypeStruct((m, n), dt))
            vmem += 2 * _nbytes((tm, tn), dt)
        elif kind == "blk":
            out_specs.append(pl.BlockSpec((None, tm, tn), lambda i, j, kk: (j, i, 0)))
            out_shape.append(jax.ShapeDtypeStruct((nj, m, tn), dt))
            vmem += 2 * _nbytes((tm, tn), dt)
        else:
            out_specs.append(pl.BlockSpec((8, tn), lambda i, j, kk: (i, j)))
            out_shape.append(jax.ShapeDtypeStruct((8 * ni, n), dt))
    scratch = [pltpu.VMEM((tm, tn), _F32)] if nk > 1 else []
    vmem += _nbytes((tm, tn), _F32) * (2 if nk > 1 else 1)

    def body(*refs):
        a_ref, b_ref = refs[0], refs[1]
        ex_refs = refs[2:2 + n_ex]
        n_in = 2 + n_ex + n_comm
        out_refs = refs[n_in:n_in + n_out]
        i, j, kk = pl.program_id(0), pl.program_id(1), pl.program_id(2)
        copies = lambda: _comm_copies(kinds, refs[2 + n_ex:n_in], refs[n_in + n_out:n_in + n_out + n_comm],
                                      *refs[len(refs) - 3:])
        if n_comm:
            @pl.when((i == 0) & (j == 0) & (kk == 0))
            def _():
                for cp in copies():
                    cp.start()

        part = dotfn(_mx(a_ref[...]), _mx(b_ref[...]))

        def finish(acc):
            vals = epilogue(acc, *[r[...] for r in ex_refs])
            for r, v in zip(out_refs, vals):
                r[...] = v.astype(r.dtype)

        if nk == 1:
            finish(part)
        else:
            acc_ref = refs[n_in + n_out + n_comm]

            @pl.when(kk == 0)
            def _():
                acc_ref[...] = part

            @pl.when(kk > 0)
            def _():
                acc_ref[...] += part

            @pl.when(kk == nk - 1)
            def _():
                finish(acc_ref[...])

        if n_comm:
            @pl.when((i == ni - 1) & (j == nj - 1) & (kk == nk - 1))
            def _():
                for cp in copies():
                    cp.wait()

    sem = ("arbitrary",) * 3 if n_comm else ("parallel", "parallel", "arbitrary")
    res = pl.pallas_call(
        body, name=name, grid=(ni, nj, nk),
        in_specs=in_specs + c_in_specs, out_specs=out_specs + c_out_specs, out_shape=out_shape + c_out_shape,
        scratch_shapes=scratch + c_scratch,
        compiler_params=pltpu.CompilerParams(dimension_semantics=sem, vmem_limit_bytes=_vmem_limit(vmem)),
    )(a, b, *[arr for arr, _ in extras], *c_in)
    return res[:n_out], res[n_out:]


def _shift_down(v, halo8, j):
    if j == 0:
        return v
    r = pltpu.roll(v, j, axis=0)
    hr = pltpu.roll(halo8, j, axis=0)
    top = jnp.where(_iota2(hr.shape, 0) < j, hr, r[:8])
    return jnp.concatenate([top, r[8:]], axis=0)


def _shift_up(v, next8, j):
    if j == 0:
        return v
    rows = v.shape[0]
    r = pltpu.roll(v, rows - j, axis=0)
    nr = pltpu.roll(next8, 8 - j, axis=0)
    bot = jnp.where(_iota2(nr.shape, 0) >= 8 - j, nr, r[rows - 8:])
    return jnp.concatenate([r[:rows - 8], bot], axis=0)


def _silu_grad(c):
    s = _sigmoid(c)
    return s * (1.0 + c * (1.0 - s))


def _gmlp_fwd_vals(pu, pv, gv, ws_ref, bsb_ref, want_bwd):
    tril = _iota2((CH, CH), 0) >= _iota2((CH, CH), 1)
    u = _gelu(pu)
    v = _gelu(pv)
    ys, keep = [], []
    for h in range(GM_H):
        sl = slice(h * 128, (h + 1) * 128)
        vh = v[:, sl]
        r = lax.rsqrt(jnp.mean(vh * vh, axis=-1, keepdims=True) + EPS)
        vn = vh * r * gv[:, sl]
        wm = _mx(jnp.where(tril, ws_ref[h], 0.0))
        mixed = _dot(wm, _mx(vn)) + bsb_ref[h]
        ys.append(u[:, sl] * mixed)
        if want_bwd:
            keep.append((vh, r, vn, wm, mixed))
    return jnp.concatenate(ys, axis=1), u, keep


def _ssd_common(xbc, halo8, dtraw, cw_ref, cb, dtb, alog, e_ref, ltri):
    xs = [_shift_down(xbc, halo8, j) for j in range(CONV_K)]
    cpre = cb + sum(cw_ref[k:k + 1, :] * xs[CONV_K - 1 - k] for k in range(CONV_K))
    act = cpre * _sigmoid(cpre)
    dtin = dtraw + dtb
    dt = jnp.maximum(dtin, 0.0) + jnp.log(1.0 + jnp.exp(-jnp.abs(dtin)))
    a_neg = -jnp.exp(alog)
    cs = _xdot_left(_dot, ltri, dt * a_neg)
    cs_last = cs[CH - 1:CH, :]
    ecs = jnp.exp(cs)
    dec = jnp.exp(cs_last - cs)
    cdec = jnp.exp(cs_last)
    e = e_ref[...]
    dt_x = _xdot(_dot, dt, e)
    ecs_x = _xdot(_dot, ecs, e)
    dec_x = _xdot(_dot, dec, e)
    cdec_x = _xdot(_dot, jnp.broadcast_to(cdec, (8, 128)), e)[0:1, :]
    return dict(xs=xs, cpre=cpre, act=act, dtin=dtin, dt=dt, a_neg=a_neg, cs=cs, ecs=ecs, dec=dec, cdec=cdec,
                dt_x=dt_x, ecs_x=ecs_x, dec_x=dec_x, cdec_x=cdec_x)


def _head_lm(csb_ref, cst_ref, h, tril):
    seg = csb_ref[:, h * 128:(h + 1) * 128] - cst_ref[h:h + 1, :]
    return jnp.exp(jnp.where(tril, seg, -jnp.inf))


def _mixer_fwd(proj, gv, ws, bsb, gout, cw8, cb, dtb, alog, d_x, ng, e_mat, e2_mat, ltri_mat, seq_chunks, comm=()):
    t = proj.shape[0]
    n_chunks = t // CH
    n_comm = len(comm)
    kinds = [kind for kind, _ in comm]
    c_in, c_in_specs, c_out_specs, c_out_shape, c_scratch = _comm_io(comm)

    def body(*refs):
        (pu_ref, pv_ref, z_ref, xbc_ref, dt_ref, halo_ref, gv_ref, ws_ref, bsb_ref, gout_ref, cw_ref, cb_ref,
         dtb_ref, alog_ref, dx_ref, ng_ref, e_ref, e2_ref, ltri_ref) = refs[:19]
        cat_ref, y_ref, st_ref = refs[19 + n_comm:22 + n_comm]
        s_ref, csb_ref, cst_ref = refs[22 + 2 * n_comm:25 + 2 * n_comm]
        copies = lambda: _comm_copies(kinds, refs[19:19 + n_comm], refs[22 + n_comm:22 + 2 * n_comm],
                                      *refs[25 + 2 * n_comm:])
        c = pl.program_id(0)
        if n_comm:
            @pl.when(c == 0)
            def _():
                for cp in copies():
                    cp.start()

            @pl.when(c == n_chunks - 1)
            def _():
                for cp in copies():
                    cp.wait()

        first = (c % seq_chunks) == 0
        tril = _iota2((CH, CH), 0) >= _iota2((CH, CH), 1)
        lane = _iota2((CH, 128), 1)

        y_a, _, _ = _gmlp_fwd_vals(pu_ref[...], pv_ref[...], gv_ref[...], ws_ref, bsb_ref, False)
        cat_ref[:, 0:GM_W] = _rms(y_a, gout_ref[...]).astype(cat_ref.dtype)

        @pl.when(first)
        def _():
            s_ref[...] = jnp.zeros_like(s_ref)

        halo8 = jnp.where(first, 0.0, halo_ref[...])
        q = _ssd_common(xbc_ref[...], halo8, dt_ref[...], cw_ref, cb_ref[...], dtb_ref[...], alog_ref[...], e_ref,
                        ltri_ref[...])
        act = q["act"]
        xv = act[:, 0:SSD_W]
        xdt = xv * q["dt_x"]
        xdt_m = _mx(xdt)
        csb_ref[...] = _xdot(_dot, q["cs"], e2_ref[...])
        cst_ref[...] = q["cs"].T
        s_prev = s_ref[...]
        st_ref[...] = s_prev
        ys = []
        for g in range(SSD_G):
            bg = _mx(act[:, SSD_W + g * SSD_N:SSD_W + (g + 1) * SSD_N])
            cg = _mx(act[:, SSD_W + SSD_G * SSD_N + g * SSD_N:SSD_W + SSD_G * SSD_N + (g + 1) * SSD_N])
            cbm = _dot_nt(cg, bg)
            gs = slice(g * 512, (g + 1) * 512)
            for pr in range(4):
                ps = slice(g * 512 + pr * 128, g * 512 + (pr + 1) * 128)
                o = []
                for hh in range(2):
                    h = g * 8 + pr * 2 + hh
                    m_h = _mx(cbm * _head_lm(csb_ref, cst_ref, h, tril))
                    o.append(_dot(m_h, xdt_m[:, ps]))
                ys.append(jnp.where(lane < SSD_P, o[0], o[1]))
            sg = s_prev[:, gs]
            yoff = _dot(cg, _mx(sg)) * q["ecs_x"][:, gs]
            ys[-4:] = [ys[-4 + i] + yoff[:, i * 128:(i + 1) * 128] for i in range(4)]
            st_new = _dot_tn(bg, _mx(q["dec_x"][:, gs] * xdt[:, gs]))
            s_ref[:, gs] = sg * q["cdec_x"][:, gs] + st_new
        y = jnp.concatenate(ys, axis=1) + dx_ref[...] * xv
        y_ref[...] = y
        zv = z_ref[...]
        yg = y * (zv * _sigmoid(zv))
        for g in range(SSD_G):
            gs = slice(g * 512, (g + 1) * 512)
            cat_ref[:, GM_W + g * 512:GM_W + (g + 1) * 512] = _rms(yg[:, gs], ng_ref[:, gs]).astype(cat_ref.dtype)

    blk = lambda w, j: pl.BlockSpec((CH, w), lambda c: (c, j))
    full = lambda arr: pl.BlockSpec(arr.shape, lambda c: (0,) * arr.ndim)
    consts = [gv, ws, bsb, gout, cw8, cb, dtb, alog, d_x, ng, e_mat, e2_mat, ltri_mat]
    res = pl.pallas_call(
        body, name="mixer_fwd", grid=(n_chunks,),
        in_specs=[blk(GM_W, 0), blk(GM_W, 1), blk(SSD_W, 2), blk(CONV_CH, 2), blk(128, DT_BLK),
                  pl.BlockSpec((8, CONV_CH), lambda c: (jnp.maximum(c * (CH // 8) - 1, 0), 2))]
        + [full(a) for a in consts] + c_in_specs,
        out_specs=[pl.BlockSpec((CH, 2 * D), lambda c: (c, 0)), pl.BlockSpec((CH, SSD_W), lambda c: (c, 0)),
                   pl.BlockSpec((CH, SSD_W), lambda c: (c, 0))] + c_out_specs,
        out_shape=[jax.ShapeDtypeStruct((t, 2 * D), _MXU), jax.ShapeDtypeStruct((t, SSD_W), _F32),
                   jax.ShapeDtypeStruct((t, SSD_W), _F32)] + c_out_shape,
        scratch_shapes=[pltpu.VMEM((SSD_N, SSD_W), _F32), pltpu.VMEM((CH, SSD_H * 128), _F32),
                        pltpu.VMEM((128, CH), _F32)] + c_scratch,
        compiler_params=pltpu.CompilerParams(dimension_semantics=("arbitrary",), vmem_limit_bytes=48 << 20),
    )(proj, proj, proj, proj, proj, proj, *consts, *c_in)
    return res[:3], res[3:]


def _mixer_bwd(proj, dcat, yss, states, gv, ws, bsb, gout, cw8, cb, dtb, alog, d_x, ng, e_mat, et_mat, e2_mat,
               ltri_mat, seq_chunks):
    t = proj.shape[0]
    n_chunks = t // CH

    def body(pu_ref, pv_ref, z_ref, xbc_ref, dt_ref, halo_ref, dcat_ref, y_ref, st_ref,
             gv_ref, ws_ref, bsb_ref, gout_ref, cw_ref, cb_ref, dtb_ref, alog_ref, dx_ref, ng_ref,
             e_ref, et_ref, e2_ref, ltri_ref,
             dproj_ref, dws_ref, dbs_ref, dgv_ref, dgout_ref, dng_ref, dcw_ref, dcb_ref, ddtb_ref, dalog_ref, dd_ref,
             ds_ref, dnext_ref, csb_ref, cst_ref, dbacc_ref, ddacc_ref):
        i = pl.program_id(0)
        c = n_chunks - 1 - i
        first = (c % seq_chunks) == 0
        last_in_seq = (c % seq_chunks) == seq_chunks - 1
        tril = _iota2((CH, CH), 0) >= _iota2((CH, CH), 1)
        lane = _iota2((CH, 128), 1)
        row = _iota2((CH, 128), 0)
        ones_bf = jnp.ones((CH, 128), _BF16)

        @pl.when(i == 0)
        def _():
            for r in (dws_ref, dbs_ref, dgv_ref, dgout_ref, dng_ref, dcw_ref, dcb_ref, ddtb_ref, dalog_ref, dd_ref,
                      dbacc_ref, ddacc_ref):
                r[...] = jnp.zeros_like(r)

        @pl.when(last_in_seq)
        def _():
            ds_ref[...] = jnp.zeros_like(ds_ref)
            dnext_ref[...] = jnp.zeros_like(dnext_ref)

        dcat_v = dcat_ref[...].astype(_F32)

        pu, pv = pu_ref[...], pv_ref[...]
        gv_v = gv_ref[...]
        y_a, u, keep = _gmlp_fwd_vals(pu, pv, gv_v, ws_ref, bsb_ref, True)
        dy, dgout8 = _rms_bwd(y_a, gout_ref[...], dcat_v[:, 0:GM_W])
        dgout_ref[...] += dgout8
        dus, dvs, dgvs = [], [], []
        for h in range(GM_H):
            sl = slice(h * 128, (h + 1) * 128)
            vh, r, vn, wm, mixed = keep[h]
            dyh = dy[:, sl]
            dus.append(dyh * mixed)
            dmix = dyh * u[:, sl]
            dmix_m = _mx(dmix)
            dws_ref[h] += jnp.where(tril, _dot_nt(dmix_m, _mx(vn)), 0.0)
            dbacc_ref[h] += dmix
            dvn = _dot_tn(wm, dmix_m)
            gy = dvn * gv_v[:, sl]
            dot = jnp.sum(gy * vh, axis=-1, keepdims=True)
            dvs.append(r * gy - vh * (r * r * r) * (dot * (1.0 / 128)))
            dgvs.append(_sum8(dvn * vh * r))
        dgv_ref[...] += jnp.concatenate(dgvs, axis=1)
        dproj_ref[:, 0:GM_W] = (jnp.concatenate(dus, axis=1) * _gelu_grad(pu)).astype(dproj_ref.dtype)
        dproj_ref[:, GM_W:2 * GM_W] = (jnp.concatenate(dvs, axis=1) * _gelu_grad(pv)).astype(dproj_ref.dtype)

        halo8 = jnp.where(first, 0.0, halo_ref[...])
        q = _ssd_common(xbc_ref[...], halo8, dt_ref[...], cw_ref, cb_ref[...], dtb_ref[...], alog_ref[...], e_ref,
                        ltri_ref[...])
        act = q["act"]
        xv = act[:, 0:SSD_W]
        dt_x, ecs_x, dec_x, cdec_x = q["dt_x"], q["ecs_x"], q["dec_x"], q["cdec_x"]
        xdt = xv * dt_x
        xdt_m = _mx(xdt)
        csb_ref[...] = _xdot(_dot, q["cs"], e2_ref[...])
        cst_ref[...] = q["cs"].T
        s_prev = st_ref[...]
        ds = ds_ref[...]
        yv = y_ref[...]
        zv = z_ref[...]
        sz = zv * _sigmoid(zv)
        yg = yv * sz
        dygs, dng8 = [], []
        for g in range(SSD_G):
            gs = slice(g * 512, (g + 1) * 512)
            a_, b_ = _rms_bwd(yg[:, gs], ng_ref[:, gs], dcat_v[:, GM_W + g * 512:GM_W + (g + 1) * 512])
            dygs.append(a_)
            dng8.append(b_)
        dyg = jnp.concatenate(dygs, axis=1)
        dng_ref[...] += jnp.concatenate(dng8, axis=1)
        dyv = dyg * sz
        dproj_ref[:, 2 * GM_W:2 * GM_W + SSD_W] = (dyg * yv * _silu_grad(zv)).astype(dproj_ref.dtype)
        ddacc_ref[...] += _sum8(dyv * xv)
        dyv_m = _mx(dyv)

        dxdt_parts, db_parts, dc_parts = [], [], []
        dcs = jnp.zeros((CH, 128), _F32)
        dcs_x_parts, ddec_x_parts, dcl_x_parts = [], [], []
        for g in range(SSD_G):
            gs = slice(g * 512, (g + 1) * 512)
            bg = _mx(act[:, SSD_W + g * SSD_N:SSD_W + (g + 1) * SSD_N])
            cg = _mx(act[:, SSD_W + SSD_G * SSD_N + g * SSD_N:SSD_W + SSD_G * SSD_N + (g + 1) * SSD_N])
            cbm = _dot_nt(cg, bg)
            sg = s_prev[:, gs]
            sg_m = _mx(sg)
            dsg = ds[:, gs]
            dsg_m = _mx(dsg)
            zoff = _dot(cg, sg_m)
            dz_off = dyv[:, gs] * ecs_x[:, gs]
            dz_off_m = _mx(dz_off)
            dcs_x_parts.append(dyv[:, gs] * zoff * ecs_x[:, gs])
            dcg = _dot_nt(dz_off_m, sg_m)
            dsprev = _dot_tn(cg, dz_off_m)
            w_st = dec_x[:, gs] * xdt[:, gs]
            dw_st = _dot(bg, dsg_m)
            dbg = _dot_nt(_mx(w_st), dsg_m)
            dxdt_g = dec_x[:, gs] * dw_st
            ddec_x_parts.append(dw_st * xdt[:, gs])
            dsprev = dsprev + cdec_x[:, gs] * dsg
            dcl_x_parts.append(jnp.sum(dsg * sg, axis=0, keepdims=True) * cdec_x[:, gs])
            ds_ref[:, gs] = dsprev
            dcb = jnp.zeros((CH, CH), _F32)
            dxdt_pairs = []
            for pr in range(4):
                ps = slice(g * 512 + pr * 128, g * 512 + (pr + 1) * 128)
                acc_pair = None
                for hh in range(2):
                    h = g * 8 + pr * 2 + hh
                    in_head = (lane < SSD_P) if hh == 0 else (lane >= SSD_P)
                    lm = _head_lm(csb_ref, cst_ref, h, tril)
                    m_h = cbm * lm
                    m_hm = _mx(m_h)
                    dyh_m = _mx(jnp.where(in_head, dyv[:, ps], 0.0))
                    dm = _dot_nt(dyh_m, xdt_m[:, ps])
                    dcb = dcb + dm * lm
                    qm = dm * m_h
                    rc = _xdot(_dot, qm, ones_bf) - _xdot(_dot_tn, qm, ones_bf)
                    dcs = dcs + jnp.where(lane == h, rc, 0.0)
                    contrib = jnp.where(in_head, _dot_tn(m_hm, dyv_m[:, ps]), 0.0)
                    acc_pair = contrib if acc_pair is None else acc_pair + contrib
                dxdt_pairs.append(acc_pair)
            dxdt_parts.append(dxdt_g + jnp.concatenate(dxdt_pairs, axis=1))
            dcb_m = _mx(dcb)
            dc_parts.append(dcg + _dot(dcb_m, bg))
            db_parts.append(dbg + _dot_tn(dcb_m, cg))
        dxdt = jnp.concatenate(dxdt_parts, axis=1)
        dxv = dx_ref[...] * dyv + dxdt * dt_x
        et = et_ref[...]
        ddt = _xdot(_dot, dxdt * xv, et)
        dcs = dcs + _xdot(_dot, jnp.concatenate(dcs_x_parts, axis=1), et)
        ddec = _xdot(_dot, jnp.concatenate(ddec_x_parts, axis=1), et) * q["dec"]
        dcs = dcs - ddec
        dcl = jnp.sum(ddec, axis=0, keepdims=True) + _xdot(
            _dot, jnp.broadcast_to(jnp.concatenate(dcl_x_parts, axis=1), (8, SSD_W)), et)[0:1, :]
        dcs = jnp.where(row == CH - 1, dcs + dcl, dcs)
        da = _xdot_left(_dot_tn, ltri_ref[...], dcs)
        ddt = ddt + da * q["a_neg"]
        dalog_ref[...] += _sum8(da * q["dt"] * q["a_neg"])
        ddtraw = jnp.where(lane < SSD_H, ddt * _sigmoid(q["dtin"]), 0.0)
        ddtb_ref[...] += _sum8(ddtraw)
        dproj_ref[:, D_IN_PAD - 128:D_IN_PAD] = ddtraw.astype(dproj_ref.dtype)
        dcpre = jnp.concatenate([dxv] + db_parts + dc_parts, axis=1) * _silu_grad(q["cpre"])
        dcb_ref[...] += _sum8(dcpre)
        for k in range(CONV_K):
            dcw_ref[k:k + 1, :] += jnp.sum(dcpre * q["xs"][CONV_K - 1 - k], axis=0, keepdims=True)
        next8 = dnext_ref[...]
        dxbc = sum(cw_ref[k:k + 1, :] * _shift_up(dcpre, next8, CONV_K - 1 - k) for k in range(CONV_K))
        dproj_ref[:, 2 * GM_W + SSD_W:2 * GM_W + SSD_W + CONV_CH] = dxbc.astype(dproj_ref.dtype)
        dnext_ref[...] = dcpre[0:8, :]

        @pl.when(i == n_chunks - 1)
        def _():
            for h in range(GM_H):
                dbs_ref[h:h + 1, :] = _xdot_left(_dot_nt, jnp.ones((8, 128), _BF16), dbacc_ref[h])[0:1, :]
            dd_ref[...] = _xdot(_dot, ddacc_ref[...], et)

    rblk = lambda w, j: pl.BlockSpec((CH, w), lambda i: (n_chunks - 1 - i, j))
    full = lambda arr: pl.BlockSpec(arr.shape, lambda i: (0,) * arr.ndim)
    acc = lambda shape: pl.BlockSpec(shape, lambda i: (0,) * len(shape))
    consts = [gv, ws, bsb, gout, cw8, cb, dtb, alog, d_x, ng, e_mat, et_mat, e2_mat, ltri_mat]
    acc_shapes = [(GM_H, CH, CH), (8, 128), (8, GM_W), (8, GM_W), (8, SSD_W), (8, CONV_CH), (8, CONV_CH), (8, 128),
                  (8, 128), (8, 128)]
    return pl.pallas_call(
        body, name="mixer_bwd", grid=(n_chunks,),
        in_specs=[rblk(GM_W, 0), rblk(GM_W, 1), rblk(SSD_W, 2), rblk(CONV_CH, 2), rblk(128, DT_BLK),
                  pl.BlockSpec((8, CONV_CH), lambda i: (jnp.maximum((n_chunks - 1 - i) * (CH // 8) - 1, 0), 2)),
                  rblk(2 * D, 0), rblk(SSD_W, 0), rblk(SSD_W, 0)] + [full(a) for a in consts],
        out_specs=[rblk(D_IN_PAD, 0)] + [acc(s) for s in acc_shapes],
        out_shape=[jax.ShapeDtypeStruct((t, D_IN_PAD), _MXU)] + [jax.ShapeDtypeStruct(s, _F32) for s in acc_shapes],
        scratch_shapes=[pltpu.VMEM((SSD_N, SSD_W), _F32), pltpu.VMEM((8, CONV_CH), _F32),
                        pltpu.VMEM((CH, SSD_H * 128), _F32), pltpu.VMEM((128, CH), _F32),
                        pltpu.VMEM((GM_H, CH, 128), _F32), pltpu.VMEM((8, SSD_W), _F32)],
        compiler_params=pltpu.CompilerParams(dimension_semantics=("arbitrary",), vmem_limit_bytes=48 << 20),
    )(proj, proj, proj, proj, proj, proj, dcat, yss, states, *consts)


def _peers():
    x, y, c = lax.axis_index("x"), lax.axis_index("y"), lax.axis_index("c")
    out = []
    for k in range(1, N_DEV):
        fx, fy, fc = (k >> 2) & 1, (k >> 1) & 1, k & 1
        px, py, pc = (x + fx) % 2, (y + fy) % 2, (c + fc) % 2
        out.append((k - 1, (px, py, pc), 4 * px + 2 * py + pc))
    return out, 4 * x + 2 * y + c


def _comm_io(comm):
    any_spec = pl.BlockSpec(memory_space=pl.ANY)
    n = len(comm)
    out_shape = [jax.ShapeDtypeStruct((N_DEV,) + (src.shape if kind == "gather" else src.shape[1:]), src.dtype)
                 for kind, src in comm]
    scratch = [pltpu.SemaphoreType.DMA((n * (N_DEV - 1),)), pltpu.SemaphoreType.DMA((n * (N_DEV - 1),)),
               pltpu.SemaphoreType.DMA((n,))] if n else []
    return [src for _, src in comm], [any_spec] * n, [any_spec] * n, out_shape, scratch


def _comm_copies(kinds, src_refs, dst_refs, send_sems, recv_sems, local_sems):
    peers, me = _peers()
    cps = []
    for s, (kind, src, dst) in enumerate(zip(kinds, src_refs, dst_refs)):
        gather = kind == "gather"
        cps.append(pltpu.make_async_copy(src if gather else src.at[me], dst.at[me], local_sems.at[s]))
        for k, pid, pidx in peers:
            cps.append(pltpu.make_async_remote_copy(
                src_ref=src if gather else src.at[pidx], dst_ref=dst.at[me],
                send_sem=send_sems.at[s * (N_DEV - 1) + k], recv_sem=recv_sems.at[s * (N_DEV - 1) + k],
                device_id=pid, device_id_type=_MESH))
    return cps


def _adam_vals(w, g, m, v):
    m = B1 * m + (1.0 - B1) * g
    v = B2 * v + (1.0 - B2) * (g * g)
    m_hat = m / (1.0 - B1 ** STEP)
    v_hat = v / (1.0 - B2 ** STEP)
    delta = -LR * (m_hat / (jnp.sqrt(v_hat) + ADAM_EPS) + WD * w)
    return delta, m, v


def _sum_adam(name, recv, w, m, v, tr=256):
    _, r, wd = recv.shape
    tr = min(tr, r)

    def body(recv_ref, w_ref, m_ref, v_ref, g_out, d_out, m_out, v_out):
        g = recv_ref[0].astype(_F32)
        for s in range(1, N_DEV):
            g = g + recv_ref[s].astype(_F32)
        d_, m_, v_ = _adam_vals(w_ref[...], g, m_ref[...], v_ref[...])
        g_out[...] = g
        d_out[...] = d_
        m_out[...] = m_
        v_out[...] = v_

    spec = pl.BlockSpec((tr, wd), lambda i: (i, 0))
    return pl.pallas_call(
        body, name=name, grid=(r // tr,),
        in_specs=[pl.BlockSpec((N_DEV, tr, wd), lambda i: (0, i, 0)), spec, spec, spec],
        out_specs=[spec] * 4, out_shape=[jax.ShapeDtypeStruct((r, wd), _F32)] * 4,
        compiler_params=pltpu.CompilerParams(dimension_semantics=("parallel",), vmem_limit_bytes=48 << 20),
    )(recv, w, m, v)


def _small_reduce_adam(parts, segments, n_rows, loss_row, w, m, v):
    def body(parts_ref, w_ref, m_ref, v_ref, g_out, d_out, m_out, v_out, loc_ref, recv_ref, send_sems, recv_sems):
        peers, me = _peers()
        loc_ref[...] = jnp.zeros_like(loc_ref)
        for out_row, n_out, in_row, n_in, kind in segments:
            if kind == "copy":
                loc_ref[out_row:out_row + n_out, :] = parts_ref[in_row:in_row + n_in, :]
            else:
                s = jnp.sum(parts_ref[in_row:in_row + n_in, :], axis=0, keepdims=True)
                if kind == "loss":
                    s = jnp.broadcast_to(jnp.sum(s, axis=1, keepdims=True) * (0.5 / D), (1, D))
                loc_ref[out_row:out_row + 1, :] = s
        recv_ref[me] = loc_ref[...]
        copies = [pltpu.make_async_remote_copy(src_ref=loc_ref, dst_ref=recv_ref.at[me], send_sem=send_sems.at[k],
                                               recv_sem=recv_sems.at[k], device_id=pid, device_id_type=_MESH)
                  for k, pid, _ in peers]
        for cp in copies:
            cp.start()
        for cp in copies:
            cp.wait()
        g = recv_ref[0]
        for s in range(1, N_DEV):
            g = g + recv_ref[s]
        d_, m_, v_ = _adam_vals(w_ref[...], g, m_ref[...], v_ref[...])
        g_out[...] = g
        d_out[...] = d_
        m_out[...] = m_
        v_out[...] = v_

    vm = pl.BlockSpec(memory_space=pltpu.VMEM)
    return pl.pallas_call(
        body, name="small_reduce_adam", in_specs=[vm] * 4, out_specs=[vm] * 4,
        out_shape=[jax.ShapeDtypeStruct((n_rows, D), _F32)] * 4,
        scratch_shapes=[pltpu.VMEM((n_rows, D), _F32), pltpu.VMEM((N_DEV, n_rows, D), _F32),
                        pltpu.SemaphoreType.DMA((N_DEV - 1,)), pltpu.SemaphoreType.DMA((N_DEV - 1,))],
        compiler_params=pltpu.CompilerParams(vmem_limit_bytes=48 << 20),
    )(parts, w, m, v)


_BIG_NAMES = ("w_in", "w_out", "w_ff1", "w_ff2", "w_ple_gate", "w_ple_proj")

_G_VECS = ("norm_mix_g", "gm_v_norm_g", "gm_out_norm_g", "ssd_norm_g", "norm_mlp_g", "ple_norm_g", "final_norm_g")


def _const_mats():
    h = np.arange(128)[:, None]
    ch = np.arange(SSD_W)[None, :]
    e = (ch // SSD_P == h).astype(np.float32)
    j = np.arange(SSD_H * 128)[None, :]
    e2 = (j // 128 == h).astype(np.float32)
    ltri = (np.arange(CH)[:, None] >= np.arange(CH)[None, :]).astype(np.float32)
    return jnp.asarray(e, _BF16), jnp.asarray(e.T, _BF16), jnp.asarray(e2, _BF16), jnp.asarray(ltri, _BF16)


def _pad_lanes(v, n=128):
    return jnp.pad(v, ((0, 0), (0, n - v.shape[1])))


def _local_step(x, p, tgt, shard, conv_w_shard, small, seq_len):
    seq_chunks = seq_len // CH
    e_mat, et_mat, e2_mat, ltri_mat = _const_mats()
    g_mix, g_mlp, g_ple = small["norm_mix_g"], small["norm_mlp_g"], small["ple_norm_g"]
    g_fin = small["final_norm_g"].reshape(1, D)
    gv, gout, ng = small["gm_v_norm_g"], small["gm_out_norm_g"], small["ssd_norm_g"]
    ws = small["gm_ws"][0]
    bsb = jnp.broadcast_to(small["gm_bs"][0][:, :, None], (GM_H, CH, 128))
    cb = small["ssd_conv_b"]
    dtb, alog = _pad_lanes(small["ssd_dt_bias"]), _pad_lanes(small["ssd_a_log"])
    d_x = jnp.repeat(small["ssd_d"], SSD_P, axis=1)

    first = lambda acc: (acc,)
    n1, (g_win, g_cw) = _norm_cast("norm_mix", x, g_mix,
                                   comm=[("gather", shard["w_in"]), ("gather", conv_w_shard)])
    w_in = jnp.pad(g_win.transpose(1, 0, 2).reshape(D, D_IN), ((0, 0), (0, D_IN_PAD - D_IN)))
    cw8 = jnp.pad(g_cw.transpose(1, 0, 2).reshape(CONV_K, CONV_CH), ((0, 8 - CONV_K), (0, 0)))
    mix_consts = (gv, ws, bsb, gout, cw8, cb, dtb, alog, d_x, ng)
    (proj,), (g_wout, w1) = _matmul("proj_in", n1, w_in, "nn", 256, D_IN_PAD, D, first, [("tile", _F32)],
                                    comm=[("gather", shard["w_out"]), ("gather", shard["w_ff1"])])
    w_out = g_wout.reshape(2 * D, D)
    (cat, yss, states), (g_w2, g_wg, g_wp) = _mixer_fwd(
        proj, *mix_consts, e_mat, e2_mat, ltri_mat, seq_chunks,
        comm=[("gather", shard["w_ff2"]), ("gather", shard["w_ple_gate"]), ("gather", shard["w_ple_proj"])])
    w2, wg = g_w2.reshape(D_FF, D), g_wg.reshape(D, D)
    wp = g_wp.transpose(1, 0, 2).reshape(D_PLE, D)

    def epi_res_norm(acc, res, g):
        hv = acc + res
        return hv, _rms(hv, g)

    (h1, n2), _ = _matmul("proj_out", cat, w_out, "nn", 512, D, 2 * D, epi_res_norm,
                          [("tile", _F32), ("tile", _MXU)], extras=[(x, "tile"), (g_mlp, "row")])

    def epi_relu2(acc):
        hid = jnp.maximum(acc, 0.0)
        return hid, hid * hid

    (hid, hid2), _ = _matmul("ff1", n2, w1, "nn", 512, 512, D, epi_relu2, [("tile", _MXU), ("tile", _MXU)],
                             b_blocked=True)
    (h2, n3), _ = _matmul("ff2", hid2, w2, "nn", 512, D, 1024, epi_res_norm, [("tile", _F32), ("tile", _MXU)],
                          extras=[(h1, "tile"), (g_ple, "row")])
    (pp,), _ = _matmul("ple_proj", p, wp, "nn", 512, D, D_PLE, first, [("tile", _F32)])

    def epi_head(acc, ppv, h2v, tg, gf):
        gate = _sigmoid(acc)
        h3 = h2v + gate * ppv
        r = lax.rsqrt(jnp.mean(h3 * h3, axis=-1, keepdims=True) + EPS)
        yv = h3 * r * gf
        err = yv - tg
        dy = err * (1.0 / D)
        dh3, dgf8 = _rms_bwd(h3, gf, dy)
        da3 = dh3 * ppv * gate * (1.0 - gate)
        return dh3, da3, dh3 * gate, _sum8(err * err), dgf8

    (dh3, da3, dpp, lossp, dgfin), _ = _matmul(
        "ple_gate_loss", n3, wg, "nn", 512, D, D, epi_head,
        [("tile", _F32), ("tile", _MXU), ("tile", _MXU), ("part8", _F32), ("part8", _F32)],
        extras=[(pp, "tile"), (h2, "tile"), (tgt, "tile"), (g_fin, "row")])

    (dwp,), _ = _matmul("d_w_ple_proj", p, dpp, "tn", D_PLE, D // N_DEV, 512, first, [("blk", _BF16)])
    (dwg,), _ = _matmul("d_w_ple_gate", n3, da3, "tn", D, D, 512, first, [("tile", _BF16)])

    def epi_norm_bwd(acc, up, hv, g):
        dx, dg8 = _rms_bwd(hv, g, acc)
        dh = up + dx
        return dh, dh, dg8

    (dh2, dh2b, dgple), (r_wp, r_wg) = _matmul(
        "d_h2", da3, wg, "nt", 512, D, D, epi_norm_bwd, [("tile", _F32), ("tile", _MXU), ("part8", _F32)],
        extras=[(dh3, "tile"), (h2, "tile"), (g_ple, "row")],
        comm=[("scatter", dwp), ("scatter", dwg.reshape(N_DEV, D // N_DEV, D))])
    (dw2,), _ = _matmul("d_w_ff2", hid2, dh2b, "tn", 1024, D, 512, first, [("tile", _BF16)])
    (da1,), (r_w2,) = _matmul("d_ff_hidden", dh2b, w2, "nt", 512, 1024, D,
                              lambda acc, hv: (acc * 2.0 * hv.astype(_F32),), [("tile", _MXU)],
                              extras=[(hid, "tile")], comm=[("scatter", dw2.reshape(N_DEV, D_FF // N_DEV, D))])
    (dw1,), _ = _matmul("d_w_ff1", n2, da1, "tn", 1024, D_FF // N_DEV, 512, first, [("blk", _BF16)])
    (dh1, dh1b, dgmlp), (r_w1,) = _matmul(
        "d_h1", da1, w1, "nt", 512, D, 512, epi_norm_bwd, [("tile", _F32), ("tile", _MXU), ("part8", _F32)],
        extras=[(dh2, "tile"), (h1, "tile"), (g_mlp, "row")], b_blocked=True, comm=[("scatter", dw1)])
    (dwout,), _ = _matmul("d_w_out", cat, dh1b, "tn", 1024, D, 512, first, [("tile", _BF16)])
    (dcat,), (r_wout,) = _matmul("d_cat", dh1b, w_out, "nt", 512, 1024, D, first, [("tile", _F32)],
                                 comm=[("scatter", dwout.reshape(N_DEV, 2 * D // N_DEV, D))])
    (dproj, dws, dbs, dgv, dgout, dng, dcw, dcb, ddtb, dalog, dd) = _mixer_bwd(
        proj, dcat, yss, states, *mix_consts, e_mat, et_mat, e2_mat, ltri_mat, seq_chunks)
    (dwin,), _ = _matmul("d_w_in", n1, dproj, "tn", 512, D_IN_PAD, 256, first, [("tile", _BF16)])
    dwin_blocks = dwin[:, :D_IN].reshape(D, N_DEV, SHARD_IN).transpose(1, 0, 2)
    (gx, dgmix), (r_win,) = _matmul(
        "d_x", dproj, w_in, "nt", 256, D, D_IN_PAD, lambda *a: epi_norm_bwd(*a)[1:], [("tile", _F32), ("part8", _F32)],
        extras=[(dh1, "tile"), (x, "tile"), (g_mix, "row")], comm=[("scatter", dwin_blocks)])

    big = dict(w_in=r_win, w_out=r_wout, w_ff1=r_w1, w_ff2=r_w2, w_ple_gate=r_wg, w_ple_proj=r_wp)
    pieces = dict(norm_mix_g=dgmix, gm_v_norm_g=dgv, gm_out_norm_g=dgout, ssd_norm_g=dng, norm_mlp_g=dgmlp,
                  ple_norm_g=dgple, final_norm_g=dgfin, gm_ws=dws, gm_bs=dbs, ssd_conv_w=dcw, ssd_conv_b=dcb,
                  ssd_dt_bias=ddtb, ssd_a_log=dalog, ssd_d=dd, loss=lossp)
    return gx, big, pieces


def _small_layout(pieces):
    rows, segments = [], []
    in_row, out_row = 0, 0

    def add(arr, kind, n_out):
        nonlocal in_row, out_row
        rows.append(arr)
        segments.append((out_row, n_out, in_row, arr.shape[0], kind))
        start = out_row
        in_row += arr.shape[0]
        out_row += n_out
        return start

    where = {}
    for name in _G_VECS:
        where[name] = add(pieces[name], "sum", 1)
    where["gm_ws"] = add(pieces["gm_ws"].reshape(GM_H * CH * CH // D, D), "copy", GM_H * CH * CH // D)
    where["gm_bs"] = add(pieces["gm_bs"].reshape(1, D), "copy", 1)
    cb = jnp.pad(pieces["ssd_conv_b"], ((0, 0), (0, 2 * D - CONV_CH)))
    where["ssd_conv_b"] = add(cb[:, :D], "sum", 1)
    add(cb[:, D:], "sum", 1)
    cw = jnp.pad(pieces["ssd_conv_w"][:CONV_K], ((0, 0), (0, 2 * D - CONV_CH)))
    where["ssd_conv_w"] = add(cw.reshape(2 * CONV_K, D), "copy", 2 * CONV_K)
    misc = jnp.concatenate([pieces["ssd_dt_bias"], pieces["ssd_a_log"], pieces["ssd_d"],
                            jnp.zeros((8, D - 3 * 128), _F32)], axis=1)
    where["misc"] = add(misc, "sum", 1)
    where["loss"] = add(pieces["loss"], "loss", 1)
    n_rows = -(-out_row // 8) * 8
    return jnp.concatenate(rows, axis=0), tuple(segments), n_rows, where


def _pack_small_params(vals, where, n_rows, my_block):
    buf = jnp.zeros((n_rows, D), _F32)

    def put(b, row, arr):
        return lax.dynamic_update_slice(b, arr, (row, 0))

    for name in _G_VECS:
        buf = put(buf, where[name], vals[name].reshape(1, D))
    buf = put(buf, where["gm_ws"], vals["gm_ws"].reshape(GM_H * CH * CH // D, D))
    buf = put(buf, where["gm_bs"], vals["gm_bs"].reshape(1, D))
    cb = jnp.pad(vals["ssd_conv_b"].reshape(1, CONV_CH), ((0, 0), (0, 2 * D - CONV_CH)))
    buf = put(buf, where["ssd_conv_b"], cb.reshape(2, D))
    cw = lax.dynamic_update_slice(jnp.zeros((CONV_K, 2 * D), _F32), vals["ssd_conv_w"].reshape(CONV_K, -1),
                                  (0, my_block * (CONV_CH // N_DEV)))
    buf = put(buf, where["ssd_conv_w"], cw.reshape(2 * CONV_K, D))
    misc = jnp.concatenate([_pad_lanes(vals["ssd_dt_bias"].reshape(1, SSD_H)),
                            _pad_lanes(vals["ssd_a_log"].reshape(1, SSD_H)),
                            _pad_lanes(vals["ssd_d"].reshape(1, SSD_H)), jnp.zeros((1, D - 3 * 128), _F32)], axis=1)
    buf = put(buf, where["misc"], misc)
    return buf


def _unpack_small(buf, where, my_block, shapes):
    out = {}
    for name in _G_VECS:
        out[name] = buf[where[name]].reshape(shapes[name])
    n_ws = GM_H * CH * CH // D
    out["gm_ws"] = buf[where["gm_ws"]:where["gm_ws"] + n_ws].reshape(shapes["gm_ws"])
    out["gm_bs"] = buf[where["gm_bs"]].reshape(shapes["gm_bs"])
    r = where["ssd_conv_b"]
    out["ssd_conv_b"] = buf[r:r + 2].reshape(1, 2 * D)[:, :CONV_CH].reshape(shapes["ssd_conv_b"])
    r = where["ssd_conv_w"]
    cw = buf[r:r + 2 * CONV_K].reshape(CONV_K, 2 * D)
    out["ssd_conv_w"] = lax.dynamic_slice(cw, (0, my_block * (CONV_CH // N_DEV)),
                                          (CONV_K, CONV_CH // N_DEV)).reshape(shapes["ssd_conv_w"])
    misc = buf[where["misc"]]
    for i, name in enumerate(("ssd_dt_bias", "ssd_a_log", "ssd_d")):
        out[name] = misc[i * 128:i * 128 + SSD_H].reshape(shapes[name])
    return out


_WEIGHTS = ("norm_mix_g", "w_in", "gm_v_norm_g", "gm_ws", "gm_bs", "gm_out_norm_g", "ssd_conv_w", "ssd_conv_b",
            "ssd_dt_bias", "ssd_a_log", "ssd_d", "ssd_norm_g", "w_out", "norm_mlp_g", "w_ff1", "w_ff2", "ple_norm_g",
            "w_ple_gate", "w_ple_proj", "final_norm_g")


def kernel(x, p, norm_mix_g, w_in, gm_v_norm_g, gm_ws, gm_bs, gm_out_norm_g, ssd_conv_w, ssd_conv_b, ssd_dt_bias, ssd_a_log, ssd_d, ssd_norm_g, w_out, norm_mlp_g, w_ff1, w_ff2, ple_norm_g, w_ple_gate, w_ple_proj, final_norm_g, loss_target, m_norm_mix_g, m_w_in, m_gm_v_norm_g, m_gm_ws, m_gm_bs, m_gm_out_norm_g, m_ssd_conv_w, m_ssd_conv_b, m_ssd_dt_bias, m_ssd_a_log, m_ssd_d, m_ssd_norm_g, m_w_out, m_norm_mlp_g, m_w_ff1, m_w_ff2, m_ple_norm_g, m_w_ple_gate, m_w_ple_proj, m_final_norm_g, v_norm_mix_g, v_w_in, v_gm_v_norm_g, v_gm_ws, v_gm_bs, v_gm_out_norm_g, v_ssd_conv_w, v_ssd_conv_b, v_ssd_dt_bias, v_ssd_a_log, v_ssd_d, v_ssd_norm_g, v_w_out, v_norm_mlp_g, v_w_ff1, v_w_ff2, v_ple_norm_g, v_w_ple_gate, v_w_ple_proj, v_final_norm_g):
    args = dict(locals())
    w = {n: args[n] for n in _WEIGHTS}
    m = {n: args["m_" + n] for n in _WEIGHTS}
    v = {n: args["v_" + n] for n in _WEIGHTS}
    shapes = {n: w[n].shape for n in _WEIGHTS}
    my_block = 4 * lax.axis_index("x") + 2 * lax.axis_index("y") + lax.axis_index("c")
    nb, seq_len, _ = x.shape

    shard = {n: w[n][0].astype(_MXU) for n in _BIG_NAMES}
    small = {n: w[n] for n in _WEIGHTS if n not in _BIG_NAMES}
    gx, recv, pieces = _local_step(x.reshape(nb * seq_len, D), p.reshape(nb * seq_len, D_PLE),
                                   loss_target.reshape(nb * seq_len, D), shard, ssd_conv_w[0], small, seq_len)

    big_out = [{}, {}, {}, {}]
    for n in _BIG_NAMES:
        res = _sum_adam("sum_adam_" + n, recv[n], w[n][0], m[n][0], v[n][0])
        for k in range(4):
            big_out[k][n] = res[k].reshape(shapes[n])

    parts, segments, n_rows, where = _small_layout(pieces)
    packs = [_pack_small_params(d, where, n_rows, my_block) for d in (w, m, v)]
    small_res = _small_reduce_adam(parts, segments, n_rows, where["loss"], *packs)
    loss = small_res[0][where["loss"], 0]
    small_out = [_unpack_small(a, where, my_block, shapes) for a in small_res]

    outs = [loss, gx.reshape(x.shape)]
    for k in range(4):
        outs += [big_out[k][n] if n in _BIG_NAMES else small_out[k][n] for n in _WEIGHTS]
    return tuple(outs)
```

```python
import functools
import math

import jax
import jax.numpy as jnp
import numpy as np
from jax import lax
from jax.experimental import pallas as pl
from jax.experimental.pallas import tpu as pltpu

_F32 = jnp.float32
_BF16 = jnp.bfloat16
_MXU = jnp.bfloat16

D = 1024
D_PLE = 256
GM_W = 1024
GM_H = 8
CH = 128
SSD_W = 1024
SSD_H = 16
SSD_P = 64
SSD_G = 2
SSD_N = 128
CONV_K = 4
CONV_CH = SSD_W + 2 * SSD_G * SSD_N
D_FF = 4096
D_IN = 2 * GM_W + SSD_W + CONV_CH + SSD_H
D_IN_PAD = 4736
DT_BLK = (D_IN_PAD - 128) // 128
EPS = 1e-6
N_DEV = 8
SHARD_IN = D_IN // N_DEV

LR, B1, B2, ADAM_EPS, WD, STEP = 0.001, 0.9, 0.999, 1e-08, 0.01, 10

_V7X_VMEM_BYTES = 64 * 1024 * 1024
_VMEM_CAP = _V7X_VMEM_BYTES - 8 * 1024 * 1024
_MESH = pl.DeviceIdType.MESH


def _vmem_limit(nbytes):
    return int(min(_VMEM_CAP, max(32 * 1024 * 1024, nbytes * 5 // 4 + (4 << 20))))


def _nbytes(shape, dtype):
    return int(np.prod(shape)) * jnp.dtype(dtype).itemsize


def _mx(v):
    return v.astype(_MXU)


def _dot(a, b):
    return jnp.dot(a, b, preferred_element_type=_F32)


def _dot_nt(a, b):
    return lax.dot_general(a, b, (((1,), (1,)), ((), ())), preferred_element_type=_F32)


def _dot_tn(a, b):
    return lax.dot_general(a, b, (((0,), (0,)), ((), ())), preferred_element_type=_F32)


def _split3(a):
    hi = a.astype(_BF16)
    r = a - hi.astype(_F32)
    mid = r.astype(_BF16)
    lo = (r - mid.astype(_F32)).astype(_BF16)
    return hi, mid, lo


def _xdot(dotfn, a, b01):
    b = b01.astype(_BF16)
    hi, mid, lo = _split3(a)
    return (dotfn(hi, b) + dotfn(mid, b)) + dotfn(lo, b)


def _xdot_left(dotfn, a01, b):
    a = a01.astype(_BF16)
    hi, mid, lo = _split3(b)
    return (dotfn(a, hi) + dotfn(a, mid)) + dotfn(a, lo)


def _sum8(v):
    r, n = v.shape
    return v.reshape(r // 8, 8, n).sum(axis=0)


def _sigmoid(v):
    return 1.0 / (1.0 + jnp.exp(-v))


def _gelu(v):
    return 0.5 * v * (1.0 + lax.erf(v * 0.7071067811865476))


def _gelu_grad(v):
    return 0.5 * (1.0 + lax.erf(v * 0.7071067811865476)) + v * jnp.exp(-0.5 * v * v) * 0.3989422804014327


def _rms(xv, g):
    ms = jnp.mean(xv * xv, axis=-1, keepdims=True)
    return xv * lax.rsqrt(ms + EPS) * g


def _rms_bwd(xv, g, dn):
    n = xv.shape[-1]
    r = lax.rsqrt(jnp.mean(xv * xv, axis=-1, keepdims=True) + EPS)
    gy = dn * g
    dot = jnp.sum(gy * xv, axis=-1, keepdims=True)
    dx = r * gy - xv * (r * r * r) * (dot * (1.0 / n))
    return dx, _sum8(dn * xv * r)


def _iota2(shape, axis):
    return lax.broadcasted_iota(jnp.int32, shape, axis)


def _norm_cast(name, x, g, tm=512, comm=()):
    t, n = x.shape
    tm = min(tm, t)
    steps = t // tm
    kinds = [kind for kind, _ in comm]
    c_in, c_in_specs, c_out_specs, c_out_shape, c_scratch = _comm_io(comm)

    def body(*refs):
        x_ref, g_ref = refs[0], refs[1]
        o_ref = refs[2 + len(comm)]
        copies = lambda: _comm_copies(kinds, refs[2:2 + len(comm)], refs[3 + len(comm):3 + 2 * len(comm)],
                                      *refs[3 + 2 * len(comm):])
        if comm:
            @pl.when(pl.program_id(0) == 0)
            def _():
                for cp in copies():
                    cp.start()

        o_ref[...] = _rms(x_ref[...], g_ref[...]).astype(o_ref.dtype)
        if comm:
            @pl.when(pl.program_id(0) == steps - 1)
            def _():
                for cp in copies():
                    cp.wait()

    res = pl.pallas_call(
        body, name=name, grid=(steps,),
        in_specs=[pl.BlockSpec((tm, n), lambda i: (i, 0)), pl.BlockSpec((1, n), lambda i: (0, 0))] + c_in_specs,
        out_specs=[pl.BlockSpec((tm, n), lambda i: (i, 0))] + c_out_specs,
        out_shape=[jax.ShapeDtypeStruct((t, n), _MXU)] + c_out_shape, scratch_shapes=c_scratch,
        compiler_params=pltpu.CompilerParams(dimension_semantics=("arbitrary",)),
    )(x, g, *c_in)
    return res[0], res[1:]


def _matmul(name, a, b, mode, tm, tn, tk, epilogue, outs, extras=(), comm=()):
    m, k = a.shape[::-1] if mode == "tn" else a.shape
    n = b.shape[0] if mode == "nt" else b.shape[1]
    tm, tn, tk = min(tm, m), min(tn, n), min(tk, k)
    assert m % tm == 0 and n % tn == 0 and k % tk == 0, (name, m, n, k, tm, tn, tk)
    if mode == "nn":
        a_spec = pl.BlockSpec((tm, tk), lambda i, j, kk: (i, kk))
        b_spec = pl.BlockSpec((tk, tn), lambda i, j, kk: (kk, j))
        dotfn = _dot
    elif mode == "nt":
        a_spec = pl.BlockSpec((tm, tk), lambda i, j, kk: (i, kk))
        b_spec = pl.BlockSpec((tn, tk), lambda i, j, kk: (j, kk))
        dotfn = _dot_nt
    else:
        a_spec = pl.BlockSpec((tk, tm), lambda i, j, kk: (kk, i))
        b_spec = pl.BlockSpec((tk, tn), lambda i, j, kk: (kk, j))
        dotfn = _dot_tn
    ni, nj, nk = m // tm, n // tn, k // tk
    n_ex, n_out, n_comm = len(extras), len(outs), len(comm)
    kinds = [kind for kind, _ in comm]
    c_in, c_in_specs, c_out_specs, c_out_shape, c_scratch = _comm_io(comm)

    in_specs, vmem = [a_spec, b_spec], 2 * (tm * tk * a.dtype.itemsize + tk * tn * b.dtype.itemsize)
    for arr, kind in extras:
        if kind == "tile":
            in_specs.append(pl.BlockSpec((tm, tn), lambda i, j, kk: (i, j)))
            vmem += 2 * _nbytes((tm, tn), arr.dtype)
        else:
            in_specs.append(pl.BlockSpec((1, tn), lambda i, j, kk: (0, j)))
    out_specs, out_shape = [], []
    for kind, dt in outs:
        if kind == "tile":
            out_specs.append(pl.BlockSpec((tm, tn), lambda i, j, kk: (i, j)))
            out_shape.append(jax.ShapeDtypeStruct((m, n), dt))
            vmem += 2 * _nbytes((tm, tn), dt)
        else:
            out_specs.append(pl.BlockSpec((8, tn), lambda i, j, kk: (i, j)))
            out_shape.append(jax.ShapeDtypeStruct((8 * ni, n), dt))
    scratch = [pltpu.VMEM((tm, tn), _F32)] if nk > 1 else []
    vmem += _nbytes((tm, tn), _F32) * (2 if nk > 1 else 1)

    def body(*refs):
        a_ref, b_ref = refs[0], refs[1]
        ex_refs = refs[2:2 + n_ex]
        n_in = 2 + n_ex + n_comm
        out_refs = refs[n_in:n_in + n_out]
        i, j, kk = pl.program_id(0), pl.program_id(1), pl.program_id(2)
        copies = lambda: _comm_copies(kinds, refs[2 + n_ex:n_in], refs[n_in + n_out:n_in + n_out + n_comm],
                                      *refs[len(refs) - 3:])
        if n_comm:
            @pl.when((i == 0) & (j == 0) & (kk == 0))
            def _():
                for cp in copies():
                    cp.start()

        part = dotfn(_mx(a_ref[...]), _mx(b_ref[...]))

        def finish(acc):
            vals = epilogue(acc, *[r[...] for r in ex_refs])
            for r, v in zip(out_refs, vals):
                r[...] = v.astype(r.dtype)

        if nk == 1:
            finish(part)
        else:
            acc_ref = refs[n_in + n_out + n_comm]

            @pl.when(kk == 0)
            def _():
                acc_ref[...] = part

            @pl.when(kk > 0)
            def _():
                acc_ref[...] += part

            @pl.when(kk == nk - 1)
            def _():
                finish(acc_ref[...])

        if n_comm:
            @pl.when((i == ni - 1) & (j == nj - 1) & (kk == nk - 1))
            def _():
                for cp in copies():
                    cp.wait()

    sem = ("arbitrary",) * 3 if n_comm else ("parallel", "parallel", "arbitrary")
    res = pl.pallas_call(
        body, name=name, grid=(ni, nj, nk),
        in_specs=in_specs + c_in_specs, out_specs=out_specs + c_out_specs, out_shape=out_shape + c_out_shape,
        scratch_shapes=scratch + c_scratch,
        compiler_params=pltpu.CompilerParams(dimension_semantics=sem, vmem_limit_bytes=_vmem_limit(vmem)),
    )(a, b, *[arr for arr, _ in extras], *c_in)
    return res[:n_out], res[n_out:]


def _shift_down(v, halo8, j):
    if j == 0:
        return v
    r = pltpu.roll(v, j, axis=0)
    hr = pltpu.roll(halo8, j, axis=0)
    top = jnp.where(_iota2(hr.shape, 0) < j, hr, r[:8])
    return jnp.concatenate([top, r[8:]], axis=0)


def _shift_up(v, next8, j):
    if j == 0:
        return v
    rows = v.shape[0]
    r = pltpu.roll(v, rows - j, axis=0)
    nr = pltpu.roll(next8, 8 - j, axis=0)
    bot = jnp.where(_iota2(nr.shape, 0) >= 8 - j, nr, r[rows - 8:])
    return jnp.concatenate([r[:rows - 8], bot], axis=0)


def _silu_grad(c):
    s = _sigmoid(c)
    return s * (1.0 + c * (1.0 - s))


def _gmlp_fwd_vals(pu, pv, gv, ws_ref, bsb_ref, want_bwd):
    tril = _iota2((CH, CH), 0) >= _iota2((CH, CH), 1)
    u = _gelu(pu)
    v = _gelu(pv)
    ys, keep = [], []
    for h in range(GM_H):
        sl = slice(h * 128, (h + 1) * 128)
        vh = v[:, sl]
        r = lax.rsqrt(jnp.mean(vh * vh, axis=-1, keepdims=True) + EPS)
        vn = vh * r * gv[:, sl]
        wm = _mx(jnp.where(tril, ws_ref[h], 0.0))
        mixed = _dot(wm, _mx(vn)) + bsb_ref[h]
        ys.append(u[:, sl] * mixed)
        if want_bwd:
            keep.append((vh, r, vn, wm, mixed))
    return jnp.concatenate(ys, axis=1), u, keep


def _ssd_common(xbc, halo8, dtraw, cw_ref, cb, dtb, alog, e_ref, ltri):
    xs = [_shift_down(xbc, halo8, j) for j in range(CONV_K)]
    cpre = cb + sum(cw_ref[k:k + 1, :] * xs[CONV_K - 1 - k] for k in range(CONV_K))
    act = cpre * _sigmoid(cpre)
    dtin = dtraw + dtb
    dt = jnp.maximum(dtin, 0.0) + jnp.log(1.0 + jnp.exp(-jnp.abs(dtin)))
    a_neg = -jnp.exp(alog)
    cs = _xdot_left(_dot, ltri, dt * a_neg)
    cs_last = cs[CH - 1:CH, :]
    ecs = jnp.exp(cs)
    dec = jnp.exp(cs_last - cs)
    cdec = jnp.exp(cs_last)
    e = e_ref[...]
    dt_x = _xdot(_dot, dt, e)
    ecs_x = _xdot(_dot, ecs, e)
    dec_x = _xdot(_dot, dec, e)
    cdec_x = _xdot(_dot, jnp.broadcast_to(cdec, (8, 128)), e)[0:1, :]
    return dict(xs=xs, cpre=cpre, act=act, dtin=dtin, dt=dt, a_neg=a_neg, cs=cs, ecs=ecs, dec=dec, cdec=cdec,
                dt_x=dt_x, ecs_x=ecs_x, dec_x=dec_x, cdec_x=cdec_x)


def _head_lm(csb_ref, cst_ref, h, tril):
    seg = csb_ref[:, h * 128:(h + 1) * 128] - cst_ref[h:h + 1, :]
    return jnp.exp(jnp.where(tril, seg, -jnp.inf))


def _mixer_fwd(proj, gv, ws, bsb, gout, cw8, cb, dtb, alog, d_x, ng, e_mat, e2_mat, ltri_mat, seq_chunks, comm=()):
    t = proj.shape[0]
    n_chunks = t // CH
    n_comm = len(comm)
    kinds = [kind for kind, _ in comm]
    c_in, c_in_specs, c_out_specs, c_out_shape, c_scratch = _comm_io(comm)

    def body(*refs):
        (pu_ref, pv_ref, z_ref, xbc_ref, dt_ref, halo_ref, gv_ref, ws_ref, bsb_ref, gout_ref, cw_ref, cb_ref,
         dtb_ref, alog_ref, dx_ref, ng_ref, e_ref, e2_ref, ltri_ref) = refs[:19]
        cat_ref, y_ref, st_ref = refs[19 + n_comm:22 + n_comm]
        s_ref, csb_ref, cst_ref = refs[22 + 2 * n_comm:25 + 2 * n_comm]
        copies = lambda: _comm_copies(kinds, refs[19:19 + n_comm], refs[22 + n_comm:22 + 2 * n_comm],
                                      *refs[25 + 2 * n_comm:])
        c = pl.program_id(0)
        if n_comm:
            @pl.when(c == 0)
            def _():
                for cp in copies():
                    cp.start()

            @pl.when(c == n_chunks - 1)
            def _():
                for cp in copies():
                    cp.wait()

        first = (c % seq_chunks) == 0
        tril = _iota2((CH, CH), 0) >= _iota2((CH, CH), 1)
        lane = _iota2((CH, 128), 1)

        y_a, _, _ = _gmlp_fwd_vals(pu_ref[...], pv_ref[...], gv_ref[...], ws_ref, bsb_ref, False)
        cat_ref[:, 0:GM_W] = _rms(y_a, gout_ref[...]).astype(cat_ref.dtype)

        @pl.when(first)
        def _():
            s_ref[...] = jnp.zeros_like(s_ref)

        halo8 = jnp.where(first, 0.0, halo_ref[...])
        q = _ssd_common(xbc_ref[...], halo8, dt_ref[...], cw_ref, cb_ref[...], dtb_ref[...], alog_ref[...], e_ref,
                        ltri_ref[...])
        act = q["act"]
        xv = act[:, 0:SSD_W]
        xdt = xv * q["dt_x"]
        xdt_m = _mx(xdt)
        csb_ref[...] = _xdot(_dot, q["cs"], e2_ref[...])
        cst_ref[...] = q["cs"].T
        s_prev = s_ref[...]
        st_ref[...] = s_prev
        ys = []
        for g in range(SSD_G):
            bg = _mx(act[:, SSD_W + g * SSD_N:SSD_W + (g + 1) * SSD_N])
            cg = _mx(act[:, SSD_W + SSD_G * SSD_N + g * SSD_N:SSD_W + SSD_G * SSD_N + (g + 1) * SSD_N])
            cbm = _dot_nt(cg, bg)
            gs = slice(g * 512, (g + 1) * 512)
            for pr in range(4):
                ps = slice(g * 512 + pr * 128, g * 512 + (pr + 1) * 128)
                o = []
                for hh in range(2):
                    h = g * 8 + pr * 2 + hh
                    m_h = _mx(cbm * _head_lm(csb_ref, cst_ref, h, tril))
                    o.append(_dot(m_h, xdt_m[:, ps]))
                ys.append(jnp.where(lane < SSD_P, o[0], o[1]))
            sg = s_prev[:, gs]
            yoff = _dot(cg, _mx(sg)) * q["ecs_x"][:, gs]
            ys[-4:] = [ys[-4 + i] + yoff[:, i * 128:(i + 1) * 128] for i in range(4)]
            st_new = _dot_tn(bg, _mx(q["dec_x"][:, gs] * xdt[:, gs]))
            s_ref[:, gs] = sg * q["cdec_x"][:, gs] + st_new
        y = jnp.concatenate(ys, axis=1) + dx_ref[...] * xv
        y_ref[...] = y
        zv = z_ref[...]
        yg = y * (zv * _sigmoid(zv))
        for g in range(SSD_G):
            gs = slice(g * 512, (g + 1) * 512)
            cat_ref[:, GM_W + g * 512:GM_W + (g + 1) * 512] = _rms(yg[:, gs], ng_ref[:, gs]).astype(cat_ref.dtype)

    blk = lambda w, j: pl.BlockSpec((CH, w), lambda c: (c, j))
    full = lambda arr: pl.BlockSpec(arr.shape, lambda c: (0,) * arr.ndim)
    consts = [gv, ws, bsb, gout, cw8, cb, dtb, alog, d_x, ng, e_mat, e2_mat, ltri_mat]
    res = pl.pallas_call(
        body, name="mixer_fwd", grid=(n_chunks,),
        in_specs=[blk(GM_W, 0), blk(GM_W, 1), blk(SSD_W, 2), blk(CONV_CH, 2), blk(128, DT_BLK),
                  pl.BlockSpec((8, CONV_CH), lambda c: (jnp.maximum(c * (CH // 8) - 1, 0), 2))]
        + [full(a) for a in consts] + c_in_specs,
        out_specs=[pl.BlockSpec((CH, 2 * D), lambda c: (c, 0)), pl.BlockSpec((CH, SSD_W), lambda c: (c, 0)),
                   pl.BlockSpec((CH, SSD_W), lambda c: (c, 0))] + c_out_specs,
        out_shape=[jax.ShapeDtypeStruct((t, 2 * D), _MXU), jax.ShapeDtypeStruct((t, SSD_W), _F32),
                   jax.ShapeDtypeStruct((t, SSD_W), _F32)] + c_out_shape,
        scratch_shapes=[pltpu.VMEM((SSD_N, SSD_W), _F32), pltpu.VMEM((CH, SSD_H * 128), _F32),
                        pltpu.VMEM((128, CH), _F32)] + c_scratch,
        compiler_params=pltpu.CompilerParams(dimension_semantics=("arbitrary",), vmem_limit_bytes=48 << 20),
    )(proj, proj, proj, proj, proj, proj, *consts, *c_in)
    return res[:3], res[3:]


def _mixer_bwd(proj, dcat, yss, states, gv, ws, bsb, gout, cw8, cb, dtb, alog, d_x, ng, e_mat, et_mat, e2_mat,
               ltri_mat, seq_chunks):
    t = proj.shape[0]
    n_chunks = t // CH

    def body(pu_ref, pv_ref, z_ref, xbc_ref, dt_ref, halo_ref, dcat_ref, y_ref, st_ref,
             gv_ref, ws_ref, bsb_ref, gout_ref, cw_ref, cb_ref, dtb_ref, alog_ref, dx_ref, ng_ref,
             e_ref, et_ref, e2_ref, ltri_ref,
             dproj_ref, dws_ref, dbs_ref, dgv_ref, dgout_ref, dng_ref, dcw_ref, dcb_ref, ddtb_ref, dalog_ref, dd_ref,
             ds_ref, dnext_ref, csb_ref, cst_ref, dbacc_ref, ddacc_ref):
        i = pl.program_id(0)
        c = n_chunks - 1 - i
        first = (c % seq_chunks) == 0
        last_in_seq = (c % seq_chunks) == seq_chunks - 1
        tril = _iota2((CH, CH), 0) >= _iota2((CH, CH), 1)
        lane = _iota2((CH, 128), 1)
        row = _iota2((CH, 128), 0)
        ones_bf = jnp.ones((CH, 128), _BF16)

        @pl.when(i == 0)
        def _():
            for r in (dws_ref, dbs_ref, dgv_ref, dgout_ref, dng_ref, dcw_ref, dcb_ref, ddtb_ref, dalog_ref, dd_ref,
                      dbacc_ref, ddacc_ref):
                r[...] = jnp.zeros_like(r)

        @pl.when(last_in_seq)
        def _():
            ds_ref[...] = jnp.zeros_like(ds_ref)
            dnext_ref[...] = jnp.zeros_like(dnext_ref)

        dcat_v = dcat_ref[...].astype(_F32)

        pu, pv = pu_ref[...], pv_ref[...]
        gv_v = gv_ref[...]
        y_a, u, keep = _gmlp_fwd_vals(pu, pv, gv_v, ws_ref, bsb_ref, True)
        dy, dgout8 = _rms_bwd(y_a, gout_ref[...], dcat_v[:, 0:GM_W])
        dgout_ref[...] += dgout8
        dus, dvs, dgvs = [], [], []
        for h in range(GM_H):
            sl = slice(h * 128, (h + 1) * 128)
            vh, r, vn, wm, mixed = keep[h]
            dyh = dy[:, sl]
            dus.append(dyh * mixed)
            dmix = dyh * u[:, sl]
            dmix_m = _mx(dmix)
            dws_ref[h] += jnp.where(tril, _dot_nt(dmix_m, _mx(vn)), 0.0)
            dbacc_ref[h] += dmix
            dvn = _dot_tn(wm, dmix_m)
            gy = dvn * gv_v[:, sl]
            dot = jnp.sum(gy * vh, axis=-1, keepdims=True)
            dvs.append(r * gy - vh * (r * r * r) * (dot * (1.0 / 128)))
            dgvs.append(_sum8(dvn * vh * r))
        dgv_ref[...] += jnp.concatenate(dgvs, axis=1)
        dproj_ref[:, 0:GM_W] = (jnp.concatenate(dus, axis=1) * _gelu_grad(pu)).astype(dproj_ref.dtype)
        dproj_ref[:, GM_W:2 * GM_W] = (jnp.concatenate(dvs, axis=1) * _gelu_grad(pv)).astype(dproj_ref.dtype)

        halo8 = jnp.where(first, 0.0, halo_ref[...])
        q = _ssd_common(xbc_ref[...], halo8, dt_ref[...], cw_ref, cb_ref[...], dtb_ref[...], alog_ref[...], e_ref,
                        ltri_ref[...])
        act = q["act"]
        xv = act[:, 0:SSD_W]
        dt_x, ecs_x, dec_x, cdec_x = q["dt_x"], q["ecs_x"], q["dec_x"], q["cdec_x"]
        xdt = xv * dt_x
        xdt_m = _mx(xdt)
        csb_ref[...] = _xdot(_dot, q["cs"], e2_ref[...])
        cst_ref[...] = q["cs"].T
        s_prev = st_ref[...]
        ds = ds_ref[...]
        yv = y_ref[...]
        zv = z_ref[...]
        sz = zv * _sigmoid(zv)
        yg = yv * sz
        dygs, dng8 = [], []
        for g in range(SSD_G):
            gs = slice(g * 512, (g + 1) * 512)
            a_, b_ = _rms_bwd(yg[:, gs], ng_ref[:, gs], dcat_v[:, GM_W + g * 512:GM_W + (g + 1) * 512])
            dygs.append(a_)
            dng8.append(b_)
        dyg = jnp.concatenate(dygs, axis=1)
        dng_ref[...] += jnp.concatenate(dng8, axis=1)
        dyv = dyg * sz
        dproj_ref[:, 2 * GM_W:2 * GM_W + SSD_W] = (dyg * yv * _silu_grad(zv)).astype(dproj_ref.dtype)
        ddacc_ref[...] += _sum8(dyv * xv)
        dyv_m = _mx(dyv)

        dxdt_parts, db_parts, dc_parts = [], [], []
        dcs = jnp.zeros((CH, 128), _F32)
        dcs_x_parts, ddec_x_parts, dcl_x_parts = [], [], []
        for g in range(SSD_G):
            gs = slice(g * 512, (g + 1) * 512)
            bg = _mx(act[:, SSD_W + g * SSD_N:SSD_W + (g + 1) * SSD_N])
            cg = _mx(act[:, SSD_W + SSD_G * SSD_N + g * SSD_N:SSD_W + SSD_G * SSD_N + (g + 1) * SSD_N])
            cbm = _dot_nt(cg, bg)
            sg = s_prev[:, gs]
            sg_m = _mx(sg)
            dsg = ds[:, gs]
            dsg_m = _mx(dsg)
            zoff = _dot(cg, sg_m)
            dz_off = dyv[:, gs] * ecs_x[:, gs]
            dz_off_m = _mx(dz_off)
            dcs_x_parts.append(dyv[:, gs] * zoff * ecs_x[:, gs])
            dcg = _dot_nt(dz_off_m, sg_m)
            dsprev = _dot_tn(cg, dz_off_m)
            w_st = dec_x[:, gs] * xdt[:, gs]
            dw_st = _dot(bg, dsg_m)
            dbg = _dot_nt(_mx(w_st), dsg_m)
            dxdt_g = dec_x[:, gs] * dw_st
            ddec_x_parts.append(dw_st * xdt[:, gs])
            dsprev = dsprev + cdec_x[:, gs] * dsg
            dcl_x_parts.append(jnp.sum(dsg * sg, axis=0, keepdims=True) * cdec_x[:, gs])
            ds_ref[:, gs] = dsprev
            dcb = jnp.zeros((CH, CH), _F32)
            dxdt_pairs = []
            for pr in range(4):
                ps = slice(g * 512 + pr * 128, g * 512 + (pr + 1) * 128)
                acc_pair = None
                for hh in range(2):
                    h = g * 8 + pr * 2 + hh
                    in_head = (lane < SSD_P) if hh == 0 else (lane >= SSD_P)
                    lm = _head_lm(csb_ref, cst_ref, h, tril)
                    m_h = cbm * lm
                    m_hm = _mx(m_h)
                    dyh_m = _mx(jnp.where(in_head, dyv[:, ps], 0.0))
                    dm = _dot_nt(dyh_m, xdt_m[:, ps])
                    dcb = dcb + dm * lm
                    qm = dm * m_h
                    rc = _xdot(_dot, qm, ones_bf) - _xdot(_dot_tn, qm, ones_bf)
                    dcs = dcs + jnp.where(lane == h, rc, 0.0)
                    contrib = jnp.where(in_head, _dot_tn(m_hm, dyv_m[:, ps]), 0.0)
                    acc_pair = contrib if acc_pair is None else acc_pair + contrib
                dxdt_pairs.append(acc_pair)
            dxdt_parts.append(dxdt_g + jnp.concatenate(dxdt_pairs, axis=1))
            dcb_m = _mx(dcb)
            dc_parts.append(dcg + _dot(dcb_m, bg))
            db_parts.append(dbg + _dot_tn(dcb_m, cg))
        dxdt = jnp.concatenate(dxdt_parts, axis=1)
        dxv = dx_ref[...] * dyv + dxdt * dt_x
        et = et_ref[...]
        ddt = _xdot(_dot, dxdt * xv, et)
        dcs = dcs + _xdot(_dot, jnp.concatenate(dcs_x_parts, axis=1), et)
        ddec = _xdot(_dot, jnp.concatenate(ddec_x_parts, axis=1), et) * q["dec"]
        dcs = dcs - ddec
        dcl = jnp.sum(ddec, axis=0, keepdims=True) + _xdot(
            _dot, jnp.broadcast_to(jnp.concatenate(dcl_x_parts, axis=1), (8, SSD_W)), et)[0:1, :]
        dcs = jnp.where(row == CH - 1, dcs + dcl, dcs)
        da = _xdot_left(_dot_tn, ltri_ref[...], dcs)
        ddt = ddt + da * q["a_neg"]
        dalog_ref[...] += _sum8(da * q["dt"] * q["a_neg"])
        ddtraw = jnp.where(lane < SSD_H, ddt * _sigmoid(q["dtin"]), 0.0)
        ddtb_ref[...] += _sum8(ddtraw)
        dproj_ref[:, D_IN_PAD - 128:D_IN_PAD] = ddtraw.astype(dproj_ref.dtype)
        dcpre = jnp.concatenate([dxv] + db_parts + dc_parts, axis=1) * _silu_grad(q["cpre"])
        dcb_ref[...] += _sum8(dcpre)
        for k in range(CONV_K):
            dcw_ref[k:k + 1, :] += jnp.sum(dcpre * q["xs"][CONV_K - 1 - k], axis=0, keepdims=True)
        next8 = dnext_ref[...]
        dxbc = sum(cw_ref[k:k + 1, :] * _shift_up(dcpre, next8, CONV_K - 1 - k) for k in range(CONV_K))
        dproj_ref[:, 2 * GM_W + SSD_W:2 * GM_W + SSD_W + CONV_CH] = dxbc.astype(dproj_ref.dtype)
        dnext_ref[...] = dcpre[0:8, :]

        @pl.when(i == n_chunks - 1)
        def _():
            for h in range(GM_H):
                dbs_ref[h:h + 1, :] = _xdot_left(_dot_nt, jnp.ones((8, 128), _BF16), dbacc_ref[h])[0:1, :]
            dd_ref[...] = _xdot(_dot, ddacc_ref[...], et)

    rblk = lambda w, j: pl.BlockSpec((CH, w), lambda i: (n_chunks - 1 - i, j))
    full = lambda arr: pl.BlockSpec(arr.shape, lambda i: (0,) * arr.ndim)
    acc = lambda shape: pl.BlockSpec(shape, lambda i: (0,) * len(shape))
    consts = [gv, ws, bsb, gout, cw8, cb, dtb, alog, d_x, ng, e_mat, et_mat, e2_mat, ltri_mat]
    acc_shapes = [(GM_H, CH, CH), (8, 128), (8, GM_W), (8, GM_W), (8, SSD_W), (8, CONV_CH), (8, CONV_CH), (8, 128),
                  (8, 128), (8, 128)]
    return pl.pallas_call(
        body, name="mixer_bwd", grid=(n_chunks,),
        in_specs=[rblk(GM_W, 0), rblk(GM_W, 1), rblk(SSD_W, 2), rblk(CONV_CH, 2), rblk(128, DT_BLK),
                  pl.BlockSpec((8, CONV_CH), lambda i: (jnp.maximum((n_chunks - 1 - i) * (CH // 8) - 1, 0), 2)),
                  rblk(2 * D, 0), rblk(SSD_W, 0), rblk(SSD_W, 0)] + [full(a) for a in consts],
        out_specs=[rblk(D_IN_PAD, 0)] + [acc(s) for s in acc_shapes],
        out_shape=[jax.ShapeDtypeStruct((t, D_IN_PAD), _MXU)] + [jax.ShapeDtypeStruct(s, _F32) for s in acc_shapes],
        scratch_shapes=[pltpu.VMEM((SSD_N, SSD_W), _F32), pltpu.VMEM((8, CONV_CH), _F32),
                        pltpu.VMEM((CH, SSD_H * 128), _F32), pltpu.VMEM((128, CH), _F32),
                        pltpu.VMEM((GM_H, CH, 128), _F32), pltpu.VMEM((8, SSD_W), _F32)],
        compiler_params=pltpu.CompilerParams(dimension_semantics=("arbitrary",), vmem_limit_bytes=48 << 20),
    )(proj, proj, proj, proj, proj, proj, dcat, yss, states, *consts)


def _peers():
    x, y, c = lax.axis_index("x"), lax.axis_index("y"), lax.axis_index("c")
    out = []
    for k in range(1, N_DEV):
        fx, fy, fc = (k >> 2) & 1, (k >> 1) & 1, k & 1
        px, py, pc = (x + fx) % 2, (y + fy) % 2, (c + fc) % 2
        out.append((k - 1, (px, py, pc), 4 * px + 2 * py + pc))
    return out, 4 * x + 2 * y + c


def _comm_io(comm):
    any_spec = pl.BlockSpec(memory_space=pl.ANY)
    n = len(comm)
    out_shape = []
    for (kind, axis), src in comm:
        shp = list(src.shape)
        if kind == "gather":
            shp[axis] *= N_DEV
        else:
            shp[axis] //= N_DEV
            shp = [N_DEV] + shp
        out_shape.append(jax.ShapeDtypeStruct(tuple(shp), src.dtype))
    scratch = [pltpu.SemaphoreType.DMA((n * (N_DEV - 1),)), pltpu.SemaphoreType.DMA((n * (N_DEV - 1),)),
               pltpu.SemaphoreType.DMA((n,))] if n else []
    return [src for _, src in comm], [any_spec] * n, [any_spec] * n, out_shape, scratch


def _window(ref, axis, idx, size):
    start = pl.multiple_of(idx * size, size)
    return ref.at[tuple(pl.ds(start, size) if a == axis else slice(None) for a in range(len(ref.shape)))]


def _comm_copies(kinds, src_refs, dst_refs, send_sems, recv_sems, local_sems):
    peers, me = _peers()
    cps = []
    for s, ((kind, axis), src, dst) in enumerate(zip(kinds, src_refs, dst_refs)):
        if kind == "gather":
            size = src.shape[axis]
            src_for = lambda pidx: src
            dst_mine = _window(dst, axis, me, size)
        else:
            size = src.shape[axis] // N_DEV
            src_for = lambda pidx: _window(src, axis, pidx, size)
            dst_mine = dst.at[me]
        cps.append(pltpu.make_async_copy(src_for(me), dst_mine, local_sems.at[s]))
        for k, pid, pidx in peers:
            cps.append(pltpu.make_async_remote_copy(
                src_ref=src_for(pidx), dst_ref=dst_mine,
                send_sem=send_sems.at[s * (N_DEV - 1) + k], recv_sem=recv_sems.at[s * (N_DEV - 1) + k],
                device_id=pid, device_id_type=_MESH))
    return cps


def _adam_vals(w, g, m, v):
    m = B1 * m + (1.0 - B1) * g
    v = B2 * v + (1.0 - B2) * (g * g)
    m_hat = m / (1.0 - B1 ** STEP)
    v_hat = v / (1.0 - B2 ** STEP)
    delta = -LR * (m_hat / (jnp.sqrt(v_hat) + ADAM_EPS) + WD * w)
    return delta, m, v


def _sum_adam(name, recv, w, m, v, tr=256):
    _, r, wd = recv.shape
    tr = min(tr, r)

    def body(recv_ref, w_ref, m_ref, v_ref, g_out, d_out, m_out, v_out):
        g = recv_ref[0].astype(_F32)
        for s in range(1, N_DEV):
            g = g + recv_ref[s].astype(_F32)
        d_, m_, v_ = _adam_vals(w_ref[...], g, m_ref[...], v_ref[...])
        g_out[...] = g
        d_out[...] = d_
        m_out[...] = m_
        v_out[...] = v_

    spec = pl.BlockSpec((tr, wd), lambda i: (i, 0))
    return pl.pallas_call(
        body, name=name, grid=(r // tr,),
        in_specs=[pl.BlockSpec((N_DEV, tr, wd), lambda i: (0, i, 0)), spec, spec, spec],
        out_specs=[spec] * 4, out_shape=[jax.ShapeDtypeStruct((r, wd), _F32)] * 4,
        compiler_params=pltpu.CompilerParams(dimension_semantics=("parallel",), vmem_limit_bytes=48 << 20),
    )(recv, w, m, v)


def _small_reduce_adam(parts, segments, n_rows, loss_row, w, m, v):
    def body(parts_ref, w_ref, m_ref, v_ref, g_out, d_out, m_out, v_out, loc_ref, recv_ref, send_sems, recv_sems):
        peers, me = _peers()
        loc_ref[...] = jnp.zeros_like(loc_ref)
        for out_row, n_out, in_row, n_in, kind in segments:
            if kind == "copy":
                loc_ref[out_row:out_row + n_out, :] = parts_ref[in_row:in_row + n_in, :]
            else:
                s = jnp.sum(parts_ref[in_row:in_row + n_in, :], axis=0, keepdims=True)
                if kind == "loss":
                    s = jnp.broadcast_to(jnp.sum(s, axis=1, keepdims=True) * (0.5 / D), (1, D))
                loc_ref[out_row:out_row + 1, :] = s
        recv_ref[me] = loc_ref[...]
        copies = [pltpu.make_async_remote_copy(src_ref=loc_ref, dst_ref=recv_ref.at[me], send_sem=send_sems.at[k],
                                               recv_sem=recv_sems.at[k], device_id=pid, device_id_type=_MESH)
                  for k, pid, _ in peers]
        for cp in copies:
            cp.start()
        for cp in copies:
            cp.wait()
        g = recv_ref[0]
        for s in range(1, N_DEV):
            g = g + recv_ref[s]
        d_, m_, v_ = _adam_vals(w_ref[...], g, m_ref[...], v_ref[...])
        g_out[...] = g
        d_out[...] = d_
        m_out[...] = m_
        v_out[...] = v_

    vm = pl.BlockSpec(memory_space=pltpu.VMEM)
    return pl.pallas_call(
        body, name="small_reduce_adam", in_specs=[vm] * 4, out_specs=[vm] * 4,
        out_shape=[jax.ShapeDtypeStruct((n_rows, D), _F32)] * 4,
        scratch_shapes=[pltpu.VMEM((n_rows, D), _F32), pltpu.VMEM((N_DEV, n_rows, D), _F32),
                        pltpu.SemaphoreType.DMA((N_DEV - 1,)), pltpu.SemaphoreType.DMA((N_DEV - 1,))],
        compiler_params=pltpu.CompilerParams(vmem_limit_bytes=48 << 20),
    )(parts, w, m, v)


_BIG_NAMES = ("w_in", "w_out", "w_ff1", "w_ff2", "w_ple_gate", "w_ple_proj")

_G_VECS = ("norm_mix_g", "gm_v_norm_g", "gm_out_norm_g", "ssd_norm_g", "norm_mlp_g", "ple_norm_g", "final_norm_g")


def _const_mats():
    h = np.arange(128)[:, None]
    ch = np.arange(SSD_W)[None, :]
    e = (ch // SSD_P == h).astype(np.float32)
    j = np.arange(SSD_H * 128)[None, :]
    e2 = (j // 128 == h).astype(np.float32)
    ltri = (np.arange(CH)[:, None] >= np.arange(CH)[None, :]).astype(np.float32)
    return jnp.asarray(e, _BF16), jnp.asarray(e.T, _BF16), jnp.asarray(e2, _BF16), jnp.asarray(ltri, _BF16)


def _pad_lanes(v, n=128):
    return jnp.pad(v, ((0, 0), (0, n - v.shape[1])))


def _local_step(x, p, tgt, shard, conv_w_shard, small, seq_len):
    seq_chunks = seq_len // CH
    e_mat, et_mat, e2_mat, ltri_mat = _const_mats()
    g_mix, g_mlp, g_ple = small["norm_mix_g"], small["norm_mlp_g"], small["ple_norm_g"]
    g_fin = small["final_norm_g"].reshape(1, D)
    gv, gout, ng = small["gm_v_norm_g"], small["gm_out_norm_g"], small["ssd_norm_g"]
    ws = small["gm_ws"][0]
    bsb = jnp.broadcast_to(small["gm_bs"][0][:, :, None], (GM_H, CH, 128))
    cb = small["ssd_conv_b"]
    dtb, alog = _pad_lanes(small["ssd_dt_bias"]), _pad_lanes(small["ssd_a_log"])
    d_x = jnp.repeat(small["ssd_d"], SSD_P, axis=1)

    first = lambda acc: (acc,)
    rows, cols = ("gather", 0), ("gather", 1)
    n1, (g_win, g_cw) = _norm_cast("norm_mix", x, g_mix,
                                   comm=[(rows, shard["w_in"][None]), (rows, conv_w_shard[None])])
    w_in = jnp.pad(g_win.transpose(1, 0, 2).reshape(D, D_IN), ((0, 0), (0, D_IN_PAD - D_IN)))
    cw8 = jnp.pad(g_cw.transpose(1, 0, 2).reshape(CONV_K, CONV_CH), ((0, 8 - CONV_K), (0, 0)))
    mix_consts = (gv, ws, bsb, gout, cw8, cb, dtb, alog, d_x, ng)
    (proj,), (w_out, w1) = _matmul("proj_in", n1, w_in, "nn", 256, D_IN_PAD, D, first, [("tile", _F32)],
                                   comm=[(rows, shard["w_out"]), (cols, shard["w_ff1"])])
    (cat, yss, states), (w2, wg, wp) = _mixer_fwd(
        proj, *mix_consts, e_mat, e2_mat, ltri_mat, seq_chunks,
        comm=[(rows, shard["w_ff2"]), (rows, shard["w_ple_gate"]), (cols, shard["w_ple_proj"])])

    def epi_res_norm(acc, res, g):
        hv = acc + res
        return hv, _rms(hv, g)

    (h1, n2), _ = _matmul("proj_out", cat, w_out, "nn", 512, D, 2 * D, epi_res_norm,
                          [("tile", _F32), ("tile", _MXU)], extras=[(x, "tile"), (g_mlp, "row")])

    def epi_relu2(acc):
        hid = jnp.maximum(acc, 0.0)
        return hid, hid * hid

    (hid, hid2), _ = _matmul("ff1", n2, w1, "nn", 512, 1024, D, epi_relu2, [("tile", _MXU), ("tile", _MXU)])
    (h2, n3), _ = _matmul("ff2", hid2, w2, "nn", 512, D, D_FF, epi_res_norm, [("tile", _F32), ("tile", _MXU)],
                          extras=[(h1, "tile"), (g_ple, "row")])
    (pp,), _ = _matmul("ple_proj", p, wp, "nn", 512, D, D_PLE, first, [("tile", _F32)])

    def epi_head(acc, ppv, h2v, tg, gf):
        gate = _sigmoid(acc)
        h3 = h2v + gate * ppv
        r = lax.rsqrt(jnp.mean(h3 * h3, axis=-1, keepdims=True) + EPS)
        yv = h3 * r * gf
        err = yv - tg
        dy = err * (1.0 / D)
        dh3, dgf8 = _rms_bwd(h3, gf, dy)
        da3 = dh3 * ppv * gate * (1.0 - gate)
        return dh3, da3, dh3 * gate, _sum8(err * err), dgf8

    (dh3, da3, dpp, lossp, dgfin), _ = _matmul(
        "ple_gate_loss", n3, wg, "nn", 512, D, D, epi_head,
        [("tile", _F32), ("tile", _MXU), ("tile", _MXU), ("part8", _F32), ("part8", _F32)],
        extras=[(pp, "tile"), (h2, "tile"), (tgt, "tile"), (g_fin, "row")])

    s_rows, s_cols = ("scatter", 0), ("scatter", 1)
    (dwp,), _ = _matmul("d_w_ple_proj", p, dpp, "tn", D_PLE, D, 2048, first, [("tile", _BF16)])
    (dwg,), _ = _matmul("d_w_ple_gate", n3, da3, "tn", D, D, 2048, first, [("tile", _BF16)])

    def epi_norm_bwd(acc, up, hv, g):
        dx, dg8 = _rms_bwd(hv, g, acc)
        dh = up + dx
        return dh, dh, dg8

    (dh2, dh2b, dgple), (r_wp, r_wg) = _matmul(
        "d_h2", da3, wg, "nt", 512, D, D, epi_norm_bwd, [("tile", _F32), ("tile", _MXU), ("part8", _F32)],
        extras=[(dh3, "tile"), (h2, "tile"), (g_ple, "row")],
        comm=[(s_cols, dwp), (s_rows, dwg)])
    (dw2,), _ = _matmul("d_w_ff2", hid2, dh2b, "tn", 1024, D, 2048, first, [("tile", _BF16)])
    (da1,), (r_w2,) = _matmul("d_ff_hidden", dh2b, w2, "nt", 512, 1024, D,
                              lambda acc, hv: (acc * 2.0 * hv.astype(_F32),), [("tile", _MXU)],
                              extras=[(hid, "tile")], comm=[(s_rows, dw2)])
    (dw1,), _ = _matmul("d_w_ff1", n2, da1, "tn", 1024, 1024, 2048, first, [("tile", _BF16)])
    (dh1, dh1b, dgmlp), (r_w1,) = _matmul(
        "d_h1", da1, w1, "nt", 256, D, D_FF, epi_norm_bwd, [("tile", _F32), ("tile", _MXU), ("part8", _F32)],
        extras=[(dh2, "tile"), (h1, "tile"), (g_mlp, "row")], comm=[(s_cols, dw1)])
    (dwout,), _ = _matmul("d_w_out", cat, dh1b, "tn", 1024, D, 2048, first, [("tile", _BF16)])
    (dcat,), (r_wout,) = _matmul("d_cat", dh1b, w_out, "nt", 512, 1024, D, first, [("tile", _F32)],
                                 comm=[(s_rows, dwout)])
    (dproj, dws, dbs, dgv, dgout, dng, dcw, dcb, ddtb, dalog, dd) = _mixer_bwd(
        proj, dcat, yss, states, *mix_consts, e_mat, et_mat, e2_mat, ltri_mat, seq_chunks)
    (dwin,), _ = _matmul("d_w_in", n1, dproj, "tn", 512, D_IN_PAD, 512, first, [("tile", _BF16)])
    dwin_blocks = dwin[:, :D_IN].reshape(D, N_DEV, SHARD_IN).transpose(1, 0, 2)
    (gx, dgmix), (r_win,) = _matmul(
        "d_x", dproj, w_in, "nt", 256, D, D_IN_PAD, lambda *a: epi_norm_bwd(*a)[1:], [("tile", _F32), ("part8", _F32)],
        extras=[(dh1, "tile"), (x, "tile"), (g_mix, "row")], comm=[(s_rows, dwin_blocks)])
    r_win = r_win.reshape(N_DEV, D, SHARD_IN)

    big = dict(w_in=r_win, w_out=r_wout, w_ff1=r_w1, w_ff2=r_w2, w_ple_gate=r_wg, w_ple_proj=r_wp)
    pieces = dict(norm_mix_g=dgmix, gm_v_norm_g=dgv, gm_out_norm_g=dgout, ssd_norm_g=dng, norm_mlp_g=dgmlp,
                  ple_norm_g=dgple, final_norm_g=dgfin, gm_ws=dws, gm_bs=dbs, ssd_conv_w=dcw, ssd_conv_b=dcb,
                  ssd_dt_bias=ddtb, ssd_a_log=dalog, ssd_d=dd, loss=lossp)
    return gx, big, pieces


def _small_layout(pieces):
    rows, segments = [], []
    in_row, out_row = 0, 0

    def add(arr, kind, n_out):
        nonlocal in_row, out_row
        rows.append(arr)
        segments.append((out_row, n_out, in_row, arr.shape[0], kind))
        start = out_row
        in_row += arr.shape[0]
        out_row += n_out
        return start

    where = {}
    for name in _G_VECS:
        where[name] = add(pieces[name], "sum", 1)
    where["gm_ws"] = add(pieces["gm_ws"].reshape(GM_H * CH * CH // D, D), "copy", GM_H * CH * CH // D)
    where["gm_bs"] = add(pieces["gm_bs"].reshape(1, D), "copy", 1)
    cb = jnp.pad(pieces["ssd_conv_b"], ((0, 0), (0, 2 * D - CONV_CH)))
    where["ssd_conv_b"] = add(cb[:, :D], "sum", 1)
    add(cb[:, D:], "sum", 1)
    cw = jnp.pad(pieces["ssd_conv_w"][:CONV_K], ((0, 0), (0, 2 * D - CONV_CH)))
    where["ssd_conv_w"] = add(cw.reshape(2 * CONV_K, D), "copy", 2 * CONV_K)
    misc = jnp.concatenate([pieces["ssd_dt_bias"], pieces["ssd_a_log"], pieces["ssd_d"],
                            jnp.zeros((8, D - 3 * 128), _F32)], axis=1)
    where["misc"] = add(misc, "sum", 1)
    where["loss"] = add(pieces["loss"], "loss", 1)
    n_rows = -(-out_row // 8) * 8
    return jnp.concatenate(rows, axis=0), tuple(segments), n_rows, where


def _pack_small_params(vals, where, n_rows, my_block):
    buf = jnp.zeros((n_rows, D), _F32)

    def put(b, row, arr):
        return lax.dynamic_update_slice(b, arr, (row, 0))

    for name in _G_VECS:
        buf = put(buf, where[name], vals[name].reshape(1, D))
    buf = put(buf, where["gm_ws"], vals["gm_ws"].reshape(GM_H * CH * CH // D, D))
    buf = put(buf, where["gm_bs"], vals["gm_bs"].reshape(1, D))
    cb = jnp.pad(vals["ssd_conv_b"].reshape(1, CONV_CH), ((0, 0), (0, 2 * D - CONV_CH)))
    buf = put(buf, where["ssd_conv_b"], cb.reshape(2, D))
    cw = lax.dynamic_update_slice(jnp.zeros((CONV_K, 2 * D), _F32), vals["ssd_conv_w"].reshape(CONV_K, -1),
                                  (0, my_block * (CONV_CH // N_DEV)))
    buf = put(buf, where["ssd_conv_w"], cw.reshape(2 * CONV_K, D))
    misc = jnp.concatenate([_pad_lanes(vals["ssd_dt_bias"].reshape(1, SSD_H)),
                            _pad_lanes(vals["ssd_a_log"].reshape(1, SSD_H)),
                            _pad_lanes(vals["ssd_d"].reshape(1, SSD_H)), jnp.zeros((1, D - 3 * 128), _F32)], axis=1)
    buf = put(buf, where["misc"], misc)
    return buf


def _unpack_small(buf, where, my_block, shapes):
    out = {}
    for name in _G_VECS:
        out[name] = buf[where[name]].reshape(shapes[name])
    n_ws = GM_H * CH * CH // D
    out["gm_ws"] = buf[where["gm_ws"]:where["gm_ws"] + n_ws].reshape(shapes["gm_ws"])
    out["gm_bs"] = buf[where["gm_bs"]].reshape(shapes["gm_bs"])
    r = where["ssd_conv_b"]
    out["ssd_conv_b"] = buf[r:r + 2].reshape(1, 2 * D)[:, :CONV_CH].reshape(shapes["ssd_conv_b"])
    r = where["ssd_conv_w"]
    cw = buf[r:r + 2 * CONV_K].reshape(CONV_K, 2 * D)
    out["ssd_conv_w"] = lax.dynamic_slice(cw, (0, my_block * (CONV_CH // N_DEV)),
                                          (CONV_K, CONV_CH // N_DEV)).reshape(shapes["ssd_conv_w"])
    misc = buf[where["misc"]]
    for i, name in enumerate(("ssd_dt_bias", "ssd_a_log", "ssd_d")):
        out[name] = misc[i * 128:i * 128 + SSD_H].reshape(shapes[name])
    return out


_WEIGHTS = ("norm_mix_g", "w_in", "gm_v_norm_g", "gm_ws", "gm_bs", "gm_out_norm_g", "ssd_conv_w", "ssd_conv_b",
            "ssd_dt_bias", "ssd_a_log", "ssd_d", "ssd_norm_g", "w_out", "norm_mlp_g", "w_ff1", "w_ff2", "ple_norm_g",
            "w_ple_gate", "w_ple_proj", "final_norm_g")


def kernel(x, p, norm_mix_g, w_in, gm_v_norm_g, gm_ws, gm_bs, gm_out_norm_g, ssd_conv_w, ssd_conv_b, ssd_dt_bias, ssd_a_log, ssd_d, ssd_norm_g, w_out, norm_mlp_g, w_ff1, w_ff2, ple_norm_g, w_ple_gate, w_ple_proj, final_norm_g, loss_target, m_norm_mix_g, m_w_in, m_gm_v_norm_g, m_gm_ws, m_gm_bs, m_gm_out_norm_g, m_ssd_conv_w, m_ssd_conv_b, m_ssd_dt_bias, m_ssd_a_log, m_ssd_d, m_ssd_norm_g, m_w_out, m_norm_mlp_g, m_w_ff1, m_w_ff2, m_ple_norm_g, m_w_ple_gate, m_w_ple_proj, m_final_norm_g, v_norm_mix_g, v_w_in, v_gm_v_norm_g, v_gm_ws, v_gm_bs, v_gm_out_norm_g, v_ssd_conv_w, v_ssd_conv_b, v_ssd_dt_bias, v_ssd_a_log, v_ssd_d, v_ssd_norm_g, v_w_out, v_norm_mlp_g, v_w_ff1, v_w_ff2, v_ple_norm_g, v_w_ple_gate, v_w_ple_proj, v_final_norm_g):
    args = dict(locals())
    w = {n: args[n] for n in _WEIGHTS}
    m = {n: args["m_" + n] for n in _WEIGHTS}
    v = {n: args["v_" + n] for n in _WEIGHTS}
    shapes = {n: w[n].shape for n in _WEIGHTS}
    my_block = 4 * lax.axis_index("x") + 2 * lax.axis_index("y") + lax.axis_index("c")
    nb, seq_len, _ = x.shape

    shard = {n: w[n][0].astype(_MXU) for n in _BIG_NAMES}
    small = {n: w[n] for n in _WEIGHTS if n not in _BIG_NAMES}
    gx, recv, pieces = _local_step(x.reshape(nb * seq_len, D), p.reshape(nb * seq_len, D_PLE),
                                   loss_target.reshape(nb * seq_len, D), shard, ssd_conv_w[0], small, seq_len)

    big_out = [{}, {}, {}, {}]
    for n in _BIG_NAMES:
        res = _sum_adam("sum_adam_" + n, recv[n], w[n][0], m[n][0], v[n][0])
        for k in range(4):
            big_out[k][n] = res[k].reshape(shapes[n])

    parts, segments, n_rows, where = _small_layout(pieces)
    packs = [_pack_small_params(d, where, n_rows, my_block) for d in (w, m, v)]
    small_res = _small_reduce_adam(parts, segments, n_rows, where["loss"], *packs)
    loss = small_res[0][where["loss"], 0]
    small_out = [_unpack_small(a, where, my_block, shapes) for a in small_res]

    outs = [loss, gx.reshape(x.shape)]
    for k in range(4):
        outs += [big_out[k][n] if n in _BIG_NAMES else small_out[k][n] for n in _WEIGHTS]
    return tuple(outs)
```

```python
import functools
import math

import jax
import jax.numpy as jnp
import numpy as np
from jax import lax
from jax.experimental import pallas as pl
from jax.experimental.pallas import tpu as pltpu

_F32 = jnp.float32
_BF16 = jnp.bfloat16
_MXU = jnp.bfloat16

D = 1024
D_PLE = 256
GM_W = 1024
GM_H = 8
CH = 128
SSD_W = 1024
SSD_H = 16
SSD_P = 64
SSD_G = 2
SSD_N = 128
CONV_K = 4
CONV_CH = SSD_W + 2 * SSD_G * SSD_N
D_FF = 4096
D_IN = 2 * GM_W + SSD_W + CONV_CH + SSD_H
D_IN_PAD = 4736
DT_BLK = (D_IN_PAD - 128) // 128
EPS = 1e-6
N_DEV = 8
SHARD_IN = D_IN // N_DEV

LR, B1, B2, ADAM_EPS, WD, STEP = 0.001, 0.9, 0.999, 1e-08, 0.01, 10

_V7X_VMEM_BYTES = 64 * 1024 * 1024
_VMEM_CAP = _V7X_VMEM_BYTES - 8 * 1024 * 1024
_MESH = pl.DeviceIdType.MESH


def _vmem_limit(nbytes):
    return int(min(_VMEM_CAP, max(32 * 1024 * 1024, nbytes * 5 // 4 + (4 << 20))))


def _nbytes(shape, dtype):
    return int(np.prod(shape)) * jnp.dtype(dtype).itemsize


def _mx(v):
    return v.astype(_MXU)


def _dot(a, b):
    return jnp.dot(a, b, preferred_element_type=_F32)


def _dot_nt(a, b):
    return lax.dot_general(a, b, (((1,), (1,)), ((), ())), preferred_element_type=_F32)


def _dot_tn(a, b):
    return lax.dot_general(a, b, (((0,), (0,)), ((), ())), preferred_element_type=_F32)


def _split3(a):
    hi = a.astype(_BF16)
    r = a - hi.astype(_F32)
    mid = r.astype(_BF16)
    lo = (r - mid.astype(_F32)).astype(_BF16)
    return hi, mid, lo


def _xdot(dotfn, a, b01):
    b = b01.astype(_BF16)
    hi, mid, lo = _split3(a)
    return (dotfn(hi, b) + dotfn(mid, b)) + dotfn(lo, b)


def _xdot_left(dotfn, a01, b):
    a = a01.astype(_BF16)
    hi, mid, lo = _split3(b)
    return (dotfn(a, hi) + dotfn(a, mid)) + dotfn(a, lo)


def _sum8(v):
    r, n = v.shape
    return v.reshape(r // 8, 8, n).sum(axis=0)


def _sigmoid(v):
    return 1.0 / (1.0 + jnp.exp(-v))


def _rms(xv, g):
    ms = jnp.mean(xv * xv, axis=-1, keepdims=True)
    return xv * lax.rsqrt(ms + EPS) * g


def _rms_bwd(xv, g, dn):
    n = xv.shape[-1]
    r = lax.rsqrt(jnp.mean(xv * xv, axis=-1, keepdims=True) + EPS)
    gy = dn * g
    dot = jnp.sum(gy * xv, axis=-1, keepdims=True)
    dx = r * gy - xv * (r * r * r) * (dot * (1.0 / n))
    return dx, _sum8(dn * xv * r)


def _iota2(shape, axis):
    return lax.broadcasted_iota(jnp.int32, shape, axis)


def _norm_cast(name, x, g, tm=512, comm=()):
    t, n = x.shape
    tm = min(tm, t)
    steps = t // tm
    kinds = [kind for kind, _ in comm]
    c_in, c_in_specs, c_out_specs, c_out_shape, c_scratch = _comm_io(comm)

    def body(*refs):
        x_ref, g_ref = refs[0], refs[1]
        o_ref = refs[2 + len(comm)]
        copies = lambda: _comm_copies(kinds, refs[2:2 + len(comm)], refs[3 + len(comm):3 + 2 * len(comm)],
                                      *refs[3 + 2 * len(comm):])
        if comm:
            @pl.when(pl.program_id(0) == 0)
            def _():
                for cp in copies():
                    cp.start()

        o_ref[...] = _rms(x_ref[...], g_ref[...]).astype(o_ref.dtype)
        if comm:
            @pl.when(pl.program_id(0) == steps - 1)
            def _():
                for cp in copies():
                    cp.wait()

    res = pl.pallas_call(
        body, name=name, grid=(steps,),
        in_specs=[pl.BlockSpec((tm, n), lambda i: (i, 0)), pl.BlockSpec((1, n), lambda i: (0, 0))] + c_in_specs,
        out_specs=[pl.BlockSpec((tm, n), lambda i: (i, 0))] + c_out_specs,
        out_shape=[jax.ShapeDtypeStruct((t, n), _MXU)] + c_out_shape, scratch_shapes=c_scratch,
        compiler_params=pltpu.CompilerParams(dimension_semantics=("arbitrary",)),
    )(x, g, *c_in)
    return res[0], res[1:]


def _matmul(name, a, b, mode, tm, tn, tk, epilogue, outs, extras=(), comm=()):
    m, k = a.shape[::-1] if mode == "tn" else a.shape
    n = b.shape[0] if mode == "nt" else b.shape[1]
    tm, tn, tk = min(tm, m), min(tn, n), min(tk, k)
    assert m % tm == 0 and n % tn == 0 and k % tk == 0, (name, m, n, k, tm, tn, tk)
    if mode == "nn":
        a_spec = pl.BlockSpec((tm, tk), lambda i, j, kk: (i, kk))
        b_spec = pl.BlockSpec((tk, tn), lambda i, j, kk: (kk, j))
        dotfn = _dot
    elif mode == "nt":
        a_spec = pl.BlockSpec((tm, tk), lambda i, j, kk: (i, kk))
        b_spec = pl.BlockSpec((tn, tk), lambda i, j, kk: (j, kk))
        dotfn = _dot_nt
    else:
        a_spec = pl.BlockSpec((tk, tm), lambda i, j, kk: (kk, i))
        b_spec = pl.BlockSpec((tk, tn), lambda i, j, kk: (kk, j))
        dotfn = _dot_tn
    ni, nj, nk = m // tm, n // tn, k // tk
    n_ex, n_out, n_comm = len(extras), len(outs), len(comm)
    kinds = [kind for kind, _ in comm]
    c_in, c_in_specs, c_out_specs, c_out_shape, c_scratch = _comm_io(comm)

    in_specs, vmem = [a_spec, b_spec], 2 * (tm * tk * a.dtype.itemsize + tk * tn * b.dtype.itemsize)
    for arr, kind in extras:
        if kind == "tile":
            in_specs.append(pl.BlockSpec((tm, tn), lambda i, j, kk: (i, j)))
            vmem += 2 * _nbytes((tm, tn), arr.dtype)
        else:
            in_specs.append(pl.BlockSpec((1, tn), lambda i, j, kk: (0, j)))
    out_specs, out_shape = [], []
    for kind, dt in outs:
        if kind == "tile":
            out_specs.append(pl.BlockSpec((tm, tn), lambda i, j, kk: (i, j)))
            out_shape.append(jax.ShapeDtypeStruct((m, n), dt))
            vmem += 2 * _nbytes((tm, tn), dt)
        else:
            out_specs.append(pl.BlockSpec((8, tn), lambda i, j, kk: (i, j)))
            out_shape.append(jax.ShapeDtypeStruct((8 * ni, n), dt))
    scratch = [pltpu.VMEM((tm, tn), _F32)] if nk > 1 else []
    vmem += _nbytes((tm, tn), _F32) * (2 if nk > 1 else 1)

    def body(*refs):
        a_ref, b_ref = refs[0], refs[1]
        ex_refs = refs[2:2 + n_ex]
        n_in = 2 + n_ex + n_comm
        out_refs = refs[n_in:n_in + n_out]
        i, j, kk = pl.program_id(0), pl.program_id(1), pl.program_id(2)
        copies = lambda: _comm_copies(kinds, refs[2 + n_ex:n_in], refs[n_in + n_out:n_in + n_out + n_comm],
                                      *refs[len(refs) - 3:])
        if n_comm:
            @pl.when((i == 0) & (j == 0) & (kk == 0))
            def _():
                for cp in copies():
                    cp.start()

        part = dotfn(_mx(a_ref[...]), _mx(b_ref[...]))

        def finish(acc):
            vals = epilogue(acc, *[r[...] for r in ex_refs])
            for r, v in zip(out_refs, vals):
                r[...] = v.astype(r.dtype)

        if nk == 1:
            finish(part)
        else:
            acc_ref = refs[n_in + n_out + n_comm]

            @pl.when(kk == 0)
            def _():
                acc_ref[...] = part

            @pl.when(kk > 0)
            def _():
                acc_ref[...] += part

            @pl.when(kk == nk - 1)
            def _():
                finish(acc_ref[...])

        if n_comm:
            @pl.when((i == ni - 1) & (j == nj - 1) & (kk == nk - 1))
            def _():
                for cp in copies():
                    cp.wait()

    sem = ("arbitrary",) * 3 if n_comm else ("parallel", "parallel", "arbitrary")
    res = pl.pallas_call(
        body, name=name, grid=(ni, nj, nk),
        in_specs=in_specs + c_in_specs, out_specs=out_specs + c_out_specs, out_shape=out_shape + c_out_shape,
        scratch_shapes=scratch + c_scratch,
        compiler_params=pltpu.CompilerParams(dimension_semantics=sem, vmem_limit_bytes=_vmem_limit(vmem)),
    )(a, b, *[arr for arr, _ in extras], *c_in)
    return res[:n_out], res[n_out:]


def _shift_down(v, halo8, j):
    if j == 0:
        return v
    r = pltpu.roll(v, j, axis=0)
    hr = pltpu.roll(halo8, j, axis=0)
    top = jnp.where(_iota2(hr.shape, 0) < j, hr, r[:8])
    return jnp.concatenate([top, r[8:]], axis=0)


def _shift_up(v, next8, j):
    if j == 0:
        return v
    rows = v.shape[0]
    r = pltpu.roll(v, rows - j, axis=0)
    nr = pltpu.roll(next8, 8 - j, axis=0)
    bot = jnp.where(_iota2(nr.shape, 0) >= 8 - j, nr, r[rows - 8:])
    return jnp.concatenate([r[:rows - 8], bot], axis=0)


def _silu_grad(c):
    s = _sigmoid(c)
    return s * (1.0 + c * (1.0 - s))


def _gmlp_fwd_vals(pu, pv, gv, ws_ref, bsb_ref, want_bwd):
    tril = _iota2((CH, CH), 0) >= _iota2((CH, CH), 1)
    cdf_u = 0.5 * (1.0 + lax.erf(pu * 0.7071067811865476))
    cdf_v = 0.5 * (1.0 + lax.erf(pv * 0.7071067811865476))
    u = pu * cdf_u
    v = pv * cdf_v
    ys, keep = [], [(cdf_u, cdf_v)] if want_bwd else []
    for h in range(GM_H):
        sl = slice(h * 128, (h + 1) * 128)
        vh = v[:, sl]
        r = lax.rsqrt(jnp.mean(vh * vh, axis=-1, keepdims=True) + EPS)
        vn = vh * r * gv[:, sl]
        wm = _mx(jnp.where(tril, ws_ref[h], 0.0))
        mixed = _dot(wm, _mx(vn)) + bsb_ref[h]
        ys.append(u[:, sl] * mixed)
        if want_bwd:
            keep.append((vh, r, vn, wm, mixed))
    return jnp.concatenate(ys, axis=1), u, keep


def _ssd_common(xbc, halo8, dtraw, cw_ref, cb, dtb, alog, e_ref, ltri):
    xs = [_shift_down(xbc, halo8, j) for j in range(CONV_K)]
    cpre = cb + sum(cw_ref[k:k + 1, :] * xs[CONV_K - 1 - k] for k in range(CONV_K))
    act = cpre * _sigmoid(cpre)
    dtin = dtraw + dtb
    dt = jnp.maximum(dtin, 0.0) + jnp.log(1.0 + jnp.exp(-jnp.abs(dtin)))
    a_neg = -jnp.exp(alog)
    cs = _xdot_left(_dot, ltri, dt * a_neg)
    cs_last = cs[CH - 1:CH, :]
    ecs = jnp.exp(cs)
    dec = jnp.exp(cs_last - cs)
    cdec = jnp.exp(cs_last)
    e = e_ref[...]
    dt_x = _dot(dt.astype(_BF16), e)
    ecs_x = _dot(ecs.astype(_BF16), e)
    dec_x = _dot(dec.astype(_BF16), e)
    cdec_x = _xdot(_dot, jnp.broadcast_to(cdec, (8, 128)), e)[0:1, :]
    return dict(xs=xs, cpre=cpre, act=act, dtin=dtin, dt=dt, a_neg=a_neg, cs=cs, ecs=ecs, dec=dec, cdec=cdec,
                dt_x=dt_x, ecs_x=ecs_x, dec_x=dec_x, cdec_x=cdec_x)


def _head_lm(cs, cst_ref, h, tril):
    seg = jnp.broadcast_to(cs[:, h:h + 1], (CH, CH)) - cst_ref[h:h + 1, :]
    return jnp.exp(jnp.where(tril, seg, -jnp.inf))


def _mixer_fwd(proj, gv, ws, bsb, gout, cw8, cb, dtb, alog, d_x, ng, e_mat, ltri_mat, seq_chunks, comm=()):
    t = proj.shape[0]
    n_chunks = t // CH
    n_comm = len(comm)
    kinds = [kind for kind, _ in comm]
    c_in, c_in_specs, c_out_specs, c_out_shape, c_scratch = _comm_io(comm)

    def body(*refs):
        (pu_ref, pv_ref, z_ref, xbc_ref, dt_ref, halo_ref, gv_ref, ws_ref, bsb_ref, gout_ref, cw_ref, cb_ref,
         dtb_ref, alog_ref, dx_ref, ng_ref, e_ref, ltri_ref) = refs[:18]
        cat_ref, y_ref, st_ref = refs[18 + n_comm:21 + n_comm]
        s_ref, cst_ref = refs[21 + 2 * n_comm:23 + 2 * n_comm]
        copies = lambda: _comm_copies(kinds, refs[18:18 + n_comm], refs[21 + n_comm:21 + 2 * n_comm],
                                      *refs[23 + 2 * n_comm:])
        c = pl.program_id(0)
        if n_comm:
            @pl.when(c == 0)
            def _():
                for cp in copies():
                    cp.start()

            @pl.when(c == n_chunks - 1)
            def _():
                for cp in copies():
                    cp.wait()

        first = (c % seq_chunks) == 0
        tril = _iota2((CH, CH), 0) >= _iota2((CH, CH), 1)
        lane = _iota2((CH, 128), 1)

        y_a, _, _ = _gmlp_fwd_vals(pu_ref[...], pv_ref[...], gv_ref[...], ws_ref, bsb_ref, False)
        cat_ref[:, 0:GM_W] = _rms(y_a, gout_ref[...]).astype(cat_ref.dtype)

        @pl.when(first)
        def _():
            s_ref[...] = jnp.zeros_like(s_ref)

        halo8 = jnp.where(first, 0.0, halo_ref[...])
        q = _ssd_common(xbc_ref[...], halo8, dt_ref[...], cw_ref, cb_ref[...], dtb_ref[...], alog_ref[...], e_ref,
                        ltri_ref[...])
        act = q["act"]
        xv = act[:, 0:SSD_W]
        xdt = xv * q["dt_x"]
        xdt_m = _mx(xdt)
        cs = q["cs"]
        cst_ref[...] = cs.T
        s_prev = s_ref[...]
        st_ref[...] = s_prev
        ys = []
        for g in range(SSD_G):
            bg = _mx(act[:, SSD_W + g * SSD_N:SSD_W + (g + 1) * SSD_N])
            cg = _mx(act[:, SSD_W + SSD_G * SSD_N + g * SSD_N:SSD_W + SSD_G * SSD_N + (g + 1) * SSD_N])
            cbm = _dot_nt(cg, bg)
            gs = slice(g * 512, (g + 1) * 512)
            for pr in range(4):
                ps = slice(g * 512 + pr * 128, g * 512 + (pr + 1) * 128)
                o = []
                for hh in range(2):
                    h = g * 8 + pr * 2 + hh
                    m_h = _mx(cbm * _head_lm(cs, cst_ref, h, tril))
                    o.append(_dot(m_h, xdt_m[:, ps]))
                ys.append(jnp.where(lane < SSD_P, o[0], o[1]))
            sg = s_prev[:, gs]
            yoff = _dot(cg, _mx(sg)) * q["ecs_x"][:, gs]
            ys[-4:] = [ys[-4 + i] + yoff[:, i * 128:(i + 1) * 128] for i in range(4)]
            st_new = _dot_tn(bg, _mx(q["dec_x"][:, gs] * xdt[:, gs]))
            s_ref[:, gs] = sg * q["cdec_x"][:, gs] + st_new
        y = jnp.concatenate(ys, axis=1) + dx_ref[...] * xv
        y_ref[...] = y
        zv = z_ref[...]
        yg = y * (zv * _sigmoid(zv))
        for g in range(SSD_G):
            gs = slice(g * 512, (g + 1) * 512)
            cat_ref[:, GM_W + g * 512:GM_W + (g + 1) * 512] = _rms(yg[:, gs], ng_ref[:, gs]).astype(cat_ref.dtype)

    blk = lambda w, j: pl.BlockSpec((CH, w), lambda c: (c, j))
    full = lambda arr: pl.BlockSpec(arr.shape, lambda c: (0,) * arr.ndim)
    consts = [gv, ws, bsb, gout, cw8, cb, dtb, alog, d_x, ng, e_mat, ltri_mat]
    res = pl.pallas_call(
        body, name="mixer_fwd", grid=(n_chunks,),
        in_specs=[blk(GM_W, 0), blk(GM_W, 1), blk(SSD_W, 2), blk(CONV_CH, 2), blk(128, DT_BLK),
                  pl.BlockSpec((8, CONV_CH), lambda c: (jnp.maximum(c * (CH // 8) - 1, 0), 2))]
        + [full(a) for a in consts] + c_in_specs,
        out_specs=[pl.BlockSpec((CH, 2 * D), lambda c: (c, 0)), pl.BlockSpec((CH, SSD_W), lambda c: (c, 0)),
                   pl.BlockSpec((CH, SSD_W), lambda c: (c, 0))] + c_out_specs,
        out_shape=[jax.ShapeDtypeStruct((t, 2 * D), _MXU), jax.ShapeDtypeStruct((t, SSD_W), _F32),
                   jax.ShapeDtypeStruct((t, SSD_W), _F32)] + c_out_shape,
        scratch_shapes=[pltpu.VMEM((SSD_N, SSD_W), _F32), pltpu.VMEM((128, CH), _F32)] + c_scratch,
        compiler_params=pltpu.CompilerParams(dimension_semantics=("arbitrary",), vmem_limit_bytes=48 << 20),
    )(proj, proj, proj, proj, proj, proj, *consts, *c_in)
    return res[:3], res[3:]


def _mixer_bwd(proj, dcat, yss, states, gv, ws, bsb, gout, cw8, cb, dtb, alog, d_x, ng, e_mat, et_mat, ltri_mat,
               seq_chunks):
    t = proj.shape[0]
    n_chunks = t // CH

    def body(pu_ref, pv_ref, z_ref, xbc_ref, dt_ref, halo_ref, dcat_ref, y_ref, st_ref,
             gv_ref, ws_ref, bsb_ref, gout_ref, cw_ref, cb_ref, dtb_ref, alog_ref, dx_ref, ng_ref,
             e_ref, et_ref, ltri_ref,
             dproj_ref, dws_ref, dbs_ref, dgv_ref, dgout_ref, dng_ref, dcw_ref, dcb_ref, ddtb_ref, dalog_ref, dd_ref,
             ds_ref, dnext_ref, dcst_ref, cst_ref, dbacc_ref, ddacc_ref):
        i = pl.program_id(0)
        c = n_chunks - 1 - i
        first = (c % seq_chunks) == 0
        last_in_seq = (c % seq_chunks) == seq_chunks - 1
        tril = _iota2((CH, CH), 0) >= _iota2((CH, CH), 1)
        lane = _iota2((CH, 128), 1)
        row = _iota2((CH, 128), 0)

        @pl.when(i == 0)
        def _():
            for r in (dws_ref, dbs_ref, dgv_ref, dgout_ref, dng_ref, dcw_ref, dcb_ref, ddtb_ref, dalog_ref, dd_ref,
                      dbacc_ref, ddacc_ref, dcst_ref):
                r[...] = jnp.zeros_like(r)

        @pl.when(last_in_seq)
        def _():
            ds_ref[...] = jnp.zeros_like(ds_ref)
            dnext_ref[...] = jnp.zeros_like(dnext_ref)

        dcat_v = dcat_ref[...].astype(_F32)

        pu, pv = pu_ref[...], pv_ref[...]
        gv_v = gv_ref[...]
        y_a, u, keep = _gmlp_fwd_vals(pu, pv, gv_v, ws_ref, bsb_ref, True)
        dy, dgout8 = _rms_bwd(y_a, gout_ref[...], dcat_v[:, 0:GM_W])
        dgout_ref[...] += dgout8
        dus, dvs, dgvs = [], [], []
        for h in range(GM_H):
            sl = slice(h * 128, (h + 1) * 128)
            vh, r, vn, wm, mixed = keep[h + 1]
            dyh = dy[:, sl]
            dus.append(dyh * mixed)
            dmix = dyh * u[:, sl]
            dmix_m = _mx(dmix)
            dws_ref[h] += jnp.where(tril, _dot_nt(dmix_m, _mx(vn)), 0.0)
            dbacc_ref[h] += dmix
            dvn = _dot_tn(wm, dmix_m)
            gy = dvn * gv_v[:, sl]
            dot = jnp.sum(gy * vh, axis=-1, keepdims=True)
            dvs.append(r * gy - vh * (r * r * r) * (dot * (1.0 / 128)))
            dgvs.append(_sum8(dvn * vh * r))
        dgv_ref[...] += jnp.concatenate(dgvs, axis=1)
        cdf_u, cdf_v = keep[0]
        gelu_grad = lambda pre, cdf: cdf + pre * jnp.exp(-0.5 * pre * pre) * 0.3989422804014327
        dproj_ref[:, 0:GM_W] = (jnp.concatenate(dus, axis=1) * gelu_grad(pu, cdf_u)).astype(dproj_ref.dtype)
        dproj_ref[:, GM_W:2 * GM_W] = (jnp.concatenate(dvs, axis=1) * gelu_grad(pv, cdf_v)).astype(dproj_ref.dtype)

        halo8 = jnp.where(first, 0.0, halo_ref[...])
        q = _ssd_common(xbc_ref[...], halo8, dt_ref[...], cw_ref, cb_ref[...], dtb_ref[...], alog_ref[...], e_ref,
                        ltri_ref[...])
        act = q["act"]
        xv = act[:, 0:SSD_W]
        dt_x, ecs_x, dec_x, cdec_x = q["dt_x"], q["ecs_x"], q["dec_x"], q["cdec_x"]
        xdt = xv * dt_x
        xdt_m = _mx(xdt)
        cs = q["cs"]
        cst_ref[...] = cs.T
        s_prev = st_ref[...]
        ds = ds_ref[...]
        yv = y_ref[...]
        zv = z_ref[...]
        sz = zv * _sigmoid(zv)
        yg = yv * sz
        dygs, dng8 = [], []
        for g in range(SSD_G):
            gs = slice(g * 512, (g + 1) * 512)
            a_, b_ = _rms_bwd(yg[:, gs], ng_ref[:, gs], dcat_v[:, GM_W + g * 512:GM_W + (g + 1) * 512])
            dygs.append(a_)
            dng8.append(b_)
        dyg = jnp.concatenate(dygs, axis=1)
        dng_ref[...] += jnp.concatenate(dng8, axis=1)
        dyv = dyg * sz
        dproj_ref[:, 2 * GM_W:2 * GM_W + SSD_W] = (dyg * yv * _silu_grad(zv)).astype(dproj_ref.dtype)
        ddacc_ref[...] += _sum8(dyv * xv)
        dyv_m = _mx(dyv)

        dxdt_parts, db_parts, dc_parts = [], [], []
        dcs = jnp.zeros((CH, 128), _F32)
        dcs_x_parts, ddec_x_parts, dcl_x_parts = [], [], []
        for g in range(SSD_G):
            gs = slice(g * 512, (g + 1) * 512)
            bg = _mx(act[:, SSD_W + g * SSD_N:SSD_W + (g + 1) * SSD_N])
            cg = _mx(act[:, SSD_W + SSD_G * SSD_N + g * SSD_N:SSD_W + SSD_G * SSD_N + (g + 1) * SSD_N])
            cbm = _dot_nt(cg, bg)
            sg = s_prev[:, gs]
            sg_m = _mx(sg)
            dsg = ds[:, gs]
            dsg_m = _mx(dsg)
            zoff = _dot(cg, sg_m)
            dz_off = dyv[:, gs] * ecs_x[:, gs]
            dz_off_m = _mx(dz_off)
            dcs_x_parts.append(dyv[:, gs] * zoff * ecs_x[:, gs])
            dcg = _dot_nt(dz_off_m, sg_m)
            dsprev = _dot_tn(cg, dz_off_m)
            w_st = dec_x[:, gs] * xdt[:, gs]
            dw_st = _dot(bg, dsg_m)
            dbg = _dot_nt(_mx(w_st), dsg_m)
            dxdt_g = dec_x[:, gs] * dw_st
            ddec_x_parts.append(dw_st * xdt[:, gs])
            dsprev = dsprev + cdec_x[:, gs] * dsg
            dcl_x_parts.append(jnp.sum(dsg * sg, axis=0, keepdims=True) * cdec_x[:, gs])
            ds_ref[:, gs] = dsprev
            dcb = jnp.zeros((CH, CH), _F32)
            dxdt_pairs = []
            for pr in range(4):
                ps = slice(g * 512 + pr * 128, g * 512 + (pr + 1) * 128)
                acc_pair = None
                for hh in range(2):
                    h = g * 8 + pr * 2 + hh
                    in_head = (lane < SSD_P) if hh == 0 else (lane >= SSD_P)
                    lm = _head_lm(cs, cst_ref, h, tril)
                    m_h = cbm * lm
                    m_hm = _mx(m_h)
                    dyh_m = _mx(jnp.where(in_head, dyv[:, ps], 0.0))
                    dm = _dot_nt(dyh_m, xdt_m[:, ps])
                    dcb = dcb + dm * lm
                    qm = dm * m_h
                    dcs = dcs + jnp.where(lane == h, jnp.sum(qm, axis=1, keepdims=True), 0.0)
                    dcst_ref[h:h + 1, :] = jnp.sum(qm, axis=0, keepdims=True)
                    contrib = jnp.where(in_head, _dot_tn(m_hm, dyv_m[:, ps]), 0.0)
                    acc_pair = contrib if acc_pair is None else acc_pair + contrib
                dxdt_pairs.append(acc_pair)
            dxdt_parts.append(dxdt_g + jnp.concatenate(dxdt_pairs, axis=1))
            dcb_m = _mx(dcb)
            dc_parts.append(dcg + _dot(dcb_m, bg))
            db_parts.append(dbg + _dot_tn(dcb_m, cg))
        dxdt = jnp.concatenate(dxdt_parts, axis=1)
        dxv = dx_ref[...] * dyv + dxdt * dt_x
        et = et_ref[...]
        head_sum = lambda v: _dot(v.astype(_BF16), et)
        ddt = head_sum(dxdt * xv)
        dcs = dcs - dcst_ref[...].T + head_sum(jnp.concatenate(dcs_x_parts, axis=1))
        ddec = head_sum(jnp.concatenate(ddec_x_parts, axis=1)) * q["dec"]
        dcs = dcs - ddec
        dcl = jnp.sum(ddec, axis=0, keepdims=True) + _xdot(
            _dot, jnp.broadcast_to(jnp.concatenate(dcl_x_parts, axis=1), (8, SSD_W)), et)[0:1, :]
        dcs = jnp.where(row == CH - 1, dcs + dcl, dcs)
        da = _xdot_left(_dot_tn, ltri_ref[...], dcs)
        ddt = ddt + da * q["a_neg"]
        dalog_ref[...] += _sum8(da * q["dt"] * q["a_neg"])
        ddtraw = jnp.where(lane < SSD_H, ddt * _sigmoid(q["dtin"]), 0.0)
        ddtb_ref[...] += _sum8(ddtraw)
        dproj_ref[:, D_IN_PAD - 128:D_IN_PAD] = ddtraw.astype(dproj_ref.dtype)
        dcpre = jnp.concatenate([dxv] + db_parts + dc_parts, axis=1) * _silu_grad(q["cpre"])
        dcb_ref[...] += _sum8(dcpre)
        for k in range(CONV_K):
            dcw_ref[k:k + 1, :] += jnp.sum(dcpre * q["xs"][CONV_K - 1 - k], axis=0, keepdims=True)
        next8 = dnext_ref[...]
        dxbc = sum(cw_ref[k:k + 1, :] * _shift_up(dcpre, next8, CONV_K - 1 - k) for k in range(CONV_K))
        dproj_ref[:, 2 * GM_W + SSD_W:2 * GM_W + SSD_W + CONV_CH] = dxbc.astype(dproj_ref.dtype)
        dnext_ref[...] = dcpre[0:8, :]

        @pl.when(i == n_chunks - 1)
        def _():
            for h in range(GM_H):
                dbs_ref[h:h + 1, :] = _xdot_left(_dot_nt, jnp.ones((8, 128), _BF16), dbacc_ref[h])[0:1, :]
            dd_ref[...] = _xdot(_dot, ddacc_ref[...], et)

    rblk = lambda w, j: pl.BlockSpec((CH, w), lambda i: (n_chunks - 1 - i, j))
    full = lambda arr: pl.BlockSpec(arr.shape, lambda i: (0,) * arr.ndim)
    acc = lambda shape: pl.BlockSpec(shape, lambda i: (0,) * len(shape))
    consts = [gv, ws, bsb, gout, cw8, cb, dtb, alog, d_x, ng, e_mat, et_mat, ltri_mat]
    acc_shapes = [(GM_H, CH, CH), (8, 128), (8, GM_W), (8, GM_W), (8, SSD_W), (8, CONV_CH), (8, CONV_CH), (8, 128),
                  (8, 128), (8, 128)]
    return pl.pallas_call(
        body, name="mixer_bwd", grid=(n_chunks,),
        in_specs=[rblk(GM_W, 0), rblk(GM_W, 1), rblk(SSD_W, 2), rblk(CONV_CH, 2), rblk(128, DT_BLK),
                  pl.BlockSpec((8, CONV_CH), lambda i: (jnp.maximum((n_chunks - 1 - i) * (CH // 8) - 1, 0), 2)),
                  rblk(2 * D, 0), rblk(SSD_W, 0), rblk(SSD_W, 0)] + [full(a) for a in consts],
        out_specs=[rblk(D_IN_PAD, 0)] + [acc(s) for s in acc_shapes],
        out_shape=[jax.ShapeDtypeStruct((t, D_IN_PAD), _MXU)] + [jax.ShapeDtypeStruct(s, _F32) for s in acc_shapes],
        scratch_shapes=[pltpu.VMEM((SSD_N, SSD_W), _F32), pltpu.VMEM((8, CONV_CH), _F32),
                        pltpu.VMEM((128, CH), _F32), pltpu.VMEM((128, CH), _F32),
                        pltpu.VMEM((GM_H, CH, 128), _F32), pltpu.VMEM((8, SSD_W), _F32)],
        compiler_params=pltpu.CompilerParams(dimension_semantics=("arbitrary",), vmem_limit_bytes=48 << 20),
    )(proj, proj, proj, proj, proj, proj, dcat, yss, states, *consts)


def _peers():
    x, y, c = lax.axis_index("x"), lax.axis_index("y"), lax.axis_index("c")
    out = []
    for k in range(1, N_DEV):
        fx, fy, fc = (k >> 2) & 1, (k >> 1) & 1, k & 1
        px, py, pc = (x + fx) % 2, (y + fy) % 2, (c + fc) % 2
        out.append((k - 1, (px, py, pc), 4 * px + 2 * py + pc))
    return out, 4 * x + 2 * y + c


def _comm_io(comm):
    any_spec = pl.BlockSpec(memory_space=pl.ANY)
    n = len(comm)
    out_shape = []
    for (kind, axis), src in comm:
        shp = list(src.shape)
        if kind == "gather":
            shp[axis] *= N_DEV
        else:
            shp[axis] //= N_DEV
            shp = [N_DEV] + shp
        out_shape.append(jax.ShapeDtypeStruct(tuple(shp), src.dtype))
    scratch = [pltpu.SemaphoreType.DMA((n * (N_DEV - 1),)), pltpu.SemaphoreType.DMA((n * (N_DEV - 1),)),
               pltpu.SemaphoreType.DMA((n,))] if n else []
    return [src for _, src in comm], [any_spec] * n, [any_spec] * n, out_shape, scratch


def _window(ref, axis, idx, size):
    start = pl.multiple_of(idx * size, size)
    return ref.at[tuple(pl.ds(start, size) if a == axis else slice(None) for a in range(len(ref.shape)))]


def _comm_copies(kinds, src_refs, dst_refs, send_sems, recv_sems, local_sems):
    peers, me = _peers()
    cps = []
    for s, ((kind, axis), src, dst) in enumerate(zip(kinds, src_refs, dst_refs)):
        if kind == "gather":
            size = src.shape[axis]
            src_for = lambda pidx: src
            dst_mine = _window(dst, axis, me, size)
        else:
            size = src.shape[axis] // N_DEV
            src_for = lambda pidx: _window(src, axis, pidx, size)
            dst_mine = dst.at[me]
        cps.append(pltpu.make_async_copy(src_for(me), dst_mine, local_sems.at[s]))
        for k, pid, pidx in peers:
            cps.append(pltpu.make_async_remote_copy(
                src_ref=src_for(pidx), dst_ref=dst_mine,
                send_sem=send_sems.at[s * (N_DEV - 1) + k], recv_sem=recv_sems.at[s * (N_DEV - 1) + k],
                device_id=pid, device_id_type=_MESH))
    return cps


def _adam_vals(w, g, m, v):
    m = B1 * m + (1.0 - B1) * g
    v = B2 * v + (1.0 - B2) * (g * g)
    m_hat = m / (1.0 - B1 ** STEP)
    v_hat = v / (1.0 - B2 ** STEP)
    delta = -LR * (m_hat / (jnp.sqrt(v_hat) + ADAM_EPS) + WD * w)
    return delta, m, v


def _sum_adam(name, recv, w, m, v, tr=256):
    _, r, wd = recv.shape
    tr = min(tr, r)

    def body(recv_ref, w_ref, m_ref, v_ref, g_out, d_out, m_out, v_out):
        g = recv_ref[0].astype(_F32)
        for s in range(1, N_DEV):
            g = g + recv_ref[s].astype(_F32)
        d_, m_, v_ = _adam_vals(w_ref[...], g, m_ref[...], v_ref[...])
        g_out[...] = g
        d_out[...] = d_
        m_out[...] = m_
        v_out[...] = v_

    spec = pl.BlockSpec((tr, wd), lambda i: (i, 0))
    return pl.pallas_call(
        body, name=name, grid=(r // tr,),
        in_specs=[pl.BlockSpec((N_DEV, tr, wd), lambda i: (0, i, 0)), spec, spec, spec],
        out_specs=[spec] * 4, out_shape=[jax.ShapeDtypeStruct((r, wd), _F32)] * 4,
        compiler_params=pltpu.CompilerParams(dimension_semantics=("parallel",), vmem_limit_bytes=48 << 20),
    )(recv, w, m, v)


def _small_reduce_adam(parts, segments, n_rows, loss_row, w, m, v):
    def body(parts_ref, w_ref, m_ref, v_ref, g_out, d_out, m_out, v_out, loc_ref, recv_ref, send_sems, recv_sems):
        peers, me = _peers()
        loc_ref[...] = jnp.zeros_like(loc_ref)
        for out_row, n_out, in_row, n_in, kind in segments:
            if kind == "copy":
                loc_ref[out_row:out_row + n_out, :] = parts_ref[in_row:in_row + n_in, :]
            else:
                s = jnp.sum(parts_ref[in_row:in_row + n_in, :], axis=0, keepdims=True)
                if kind == "loss":
                    s = jnp.broadcast_to(jnp.sum(s, axis=1, keepdims=True) * (0.5 / D), (1, D))
                loc_ref[out_row:out_row + 1, :] = s
        recv_ref[me] = loc_ref[...]
        copies = [pltpu.make_async_remote_copy(src_ref=loc_ref, dst_ref=recv_ref.at[me], send_sem=send_sems.at[k],
                                               recv_sem=recv_sems.at[k], device_id=pid, device_id_type=_MESH)
                  for k, pid, _ in peers]
        for cp in copies:
            cp.start()
        for cp in copies:
            cp.wait()
        g = recv_ref[0]
        for s in range(1, N_DEV):
            g = g + recv_ref[s]
        d_, m_, v_ = _adam_vals(w_ref[...], g, m_ref[...], v_ref[...])
        g_out[...] = g
        d_out[...] = d_
        m_out[...] = m_
        v_out[...] = v_

    vm = pl.BlockSpec(memory_space=pltpu.VMEM)
    return pl.pallas_call(
        body, name="small_reduce_adam", in_specs=[vm] * 4, out_specs=[vm] * 4,
        out_shape=[jax.ShapeDtypeStruct((n_rows, D), _F32)] * 4,
        scratch_shapes=[pltpu.VMEM((n_rows, D), _F32), pltpu.VMEM((N_DEV, n_rows, D), _F32),
                        pltpu.SemaphoreType.DMA((N_DEV - 1,)), pltpu.SemaphoreType.DMA((N_DEV - 1,))],
        compiler_params=pltpu.CompilerParams(vmem_limit_bytes=48 << 20),
    )(parts, w, m, v)


_BIG_NAMES = ("w_in", "w_out", "w_ff1", "w_ff2", "w_ple_gate", "w_ple_proj")

_G_VECS = ("norm_mix_g", "gm_v_norm_g", "gm_out_norm_g", "ssd_norm_g", "norm_mlp_g", "ple_norm_g", "final_norm_g")


def _const_mats():
    h = np.arange(128)[:, None]
    ch = np.arange(SSD_W)[None, :]
    e = (ch // SSD_P == h).astype(np.float32)
    ltri = (np.arange(CH)[:, None] >= np.arange(CH)[None, :]).astype(np.float32)
    return jnp.asarray(e, _BF16), jnp.asarray(e.T, _BF16), jnp.asarray(ltri, _BF16)


def _pad_lanes(v, n=128):
    return jnp.pad(v, ((0, 0), (0, n - v.shape[1])))


def _local_step(x, p, tgt, shard, conv_w_shard, small, seq_len):
    seq_chunks = seq_len // CH
    e_mat, et_mat, ltri_mat = _const_mats()
    g_mix, g_mlp, g_ple = small["norm_mix_g"], small["norm_mlp_g"], small["ple_norm_g"]
    g_fin = small["final_norm_g"].reshape(1, D)
    gv, gout, ng = small["gm_v_norm_g"], small["gm_out_norm_g"], small["ssd_norm_g"]
    ws = small["gm_ws"][0]
    bsb = jnp.broadcast_to(small["gm_bs"][0][:, :, None], (GM_H, CH, 128))
    cb = small["ssd_conv_b"]
    dtb, alog = _pad_lanes(small["ssd_dt_bias"]), _pad_lanes(small["ssd_a_log"])
    d_x = jnp.repeat(small["ssd_d"], SSD_P, axis=1)

    first = lambda acc: (acc,)
    rows, cols = ("gather", 0), ("gather", 1)
    n1, (g_win, g_cw) = _norm_cast("norm_mix", x, g_mix,
                                   comm=[(rows, shard["w_in"][None]), (rows, conv_w_shard[None])])
    w_in = jnp.pad(g_win.transpose(1, 0, 2).reshape(D, D_IN), ((0, 0), (0, D_IN_PAD - D_IN)))
    cw8 = jnp.pad(g_cw.transpose(1, 0, 2).reshape(CONV_K, CONV_CH), ((0, 8 - CONV_K), (0, 0)))
    mix_consts = (gv, ws, bsb, gout, cw8, cb, dtb, alog, d_x, ng)
    (proj,), (w_out, w1) = _matmul("proj_in", n1, w_in, "nn", 256, D_IN_PAD, D, first, [("tile", _F32)],
                                   comm=[(rows, shard["w_out"]), (cols, shard["w_ff1"])])
    (cat, yss, states), (w2, wg, wp) = _mixer_fwd(
        proj, *mix_consts, e_mat, ltri_mat, seq_chunks,
        comm=[(rows, shard["w_ff2"]), (rows, shard["w_ple_gate"]), (cols, shard["w_ple_proj"])])

    def epi_res_norm(acc, res, g):
        hv = acc + res
        return hv, _rms(hv, g)

    (h1, n2), _ = _matmul("proj_out", cat, w_out, "nn", 512, D, 2 * D, epi_res_norm,
                          [("tile", _F32), ("tile", _MXU)], extras=[(x, "tile"), (g_mlp, "row")])

    def epi_relu2(acc):
        hid = jnp.maximum(acc, 0.0)
        return hid, hid * hid

    (hid, hid2), _ = _matmul("ff1", n2, w1, "nn", 512, 1024, D, epi_relu2, [("tile", _MXU), ("tile", _MXU)])
    (h2, n3), _ = _matmul("ff2", hid2, w2, "nn", 512, D, D_FF, epi_res_norm, [("tile", _F32), ("tile", _MXU)],
                          extras=[(h1, "tile"), (g_ple, "row")])
    (pp,), _ = _matmul("ple_proj", p, wp, "nn", 512, D, D_PLE, first, [("tile", _F32)])

    def epi_head(acc, ppv, h2v, tg, gf):
        gate = _sigmoid(acc)
        h3 = h2v + gate * ppv
        r = lax.rsqrt(jnp.mean(h3 * h3, axis=-1, keepdims=True) + EPS)
        yv = h3 * r * gf
        err = yv - tg
        dy = err * (1.0 / D)
        dh3, dgf8 = _rms_bwd(h3, gf, dy)
        da3 = dh3 * ppv * gate * (1.0 - gate)
        return dh3, da3, dh3 * gate, _sum8(err * err), dgf8

    (dh3, da3, dpp, lossp, dgfin), _ = _matmul(
        "ple_gate_loss", n3, wg, "nn", 512, D, D, epi_head,
        [("tile", _F32), ("tile", _MXU), ("tile", _MXU), ("part8", _F32), ("part8", _F32)],
        extras=[(pp, "tile"), (h2, "tile"), (tgt, "tile"), (g_fin, "row")])

    s_rows, s_cols = ("scatter", 0), ("scatter", 1)
    (dwp,), _ = _matmul("d_w_ple_proj", p, dpp, "tn", D_PLE, D, 2048, first, [("tile", _BF16)])
    (dwg,), _ = _matmul("d_w_ple_gate", n3, da3, "tn", D, D, 2048, first, [("tile", _BF16)])

    def epi_norm_bwd(acc, up, hv, g):
        dx, dg8 = _rms_bwd(hv, g, acc)
        dh = up + dx
        return dh, dh, dg8

    (dh2, dh2b, dgple), (r_wp, r_wg) = _matmul(
        "d_h2", da3, wg, "nt", 512, D, D, epi_norm_bwd, [("tile", _F32), ("tile", _MXU), ("part8", _F32)],
        extras=[(dh3, "tile"), (h2, "tile"), (g_ple, "row")],
        comm=[(s_cols, dwp), (s_rows, dwg)])
    (dw2,), _ = _matmul("d_w_ff2", hid2, dh2b, "tn", 1024, D, 2048, first, [("tile", _BF16)])
    (da1,), (r_w2,) = _matmul("d_ff_hidden", dh2b, w2, "nt", 512, 1024, D,
                              lambda acc, hv: (acc * 2.0 * hv.astype(_F32),), [("tile", _MXU)],
                              extras=[(hid, "tile")], comm=[(s_rows, dw2)])
    (dw1,), _ = _matmul("d_w_ff1", n2, da1, "tn", 1024, 1024, 2048, first, [("tile", _BF16)])
    (dh1, dh1b, dgmlp), (r_w1,) = _matmul(
        "d_h1", da1, w1, "nt", 256, D, D_FF, epi_norm_bwd, [("tile", _F32), ("tile", _MXU), ("part8", _F32)],
        extras=[(dh2, "tile"), (h1, "tile"), (g_mlp, "row")], comm=[(s_cols, dw1)])
    (dwout,), _ = _matmul("d_w_out", cat, dh1b, "tn", 1024, D, 2048, first, [("tile", _BF16)])
    (dcat,), (r_wout,) = _matmul("d_cat", dh1b, w_out, "nt", 512, 1024, D, first, [("tile", _F32)],
                                 comm=[(s_rows, dwout)])
    (dproj, dws, dbs, dgv, dgout, dng, dcw, dcb, ddtb, dalog, dd) = _mixer_bwd(
        proj, dcat, yss, states, *mix_consts, e_mat, et_mat, ltri_mat, seq_chunks)
    (dwin,), _ = _matmul("d_w_in", n1, dproj, "tn", 512, D_IN_PAD, 1024, first, [("tile", _BF16)])
    dwin_blocks = dwin[:, :D_IN].reshape(D, N_DEV, SHARD_IN).transpose(1, 0, 2)
    (gx, dgmix), (r_win,) = _matmul(
        "d_x", dproj, w_in, "nt", 256, D, D_IN_PAD, lambda *a: epi_norm_bwd(*a)[1:], [("tile", _F32), ("part8", _F32)],
        extras=[(dh1, "tile"), (x, "tile"), (g_mix, "row")], comm=[(s_rows, dwin_blocks)])
    r_win = r_win.reshape(N_DEV, D, SHARD_IN)

    big = dict(w_in=r_win, w_out=r_wout, w_ff1=r_w1, w_ff2=r_w2, w_ple_gate=r_wg, w_ple_proj=r_wp)
    pieces = dict(norm_mix_g=dgmix, gm_v_norm_g=dgv, gm_out_norm_g=dgout, ssd_norm_g=dng, norm_mlp_g=dgmlp,
                  ple_norm_g=dgple, final_norm_g=dgfin, gm_ws=dws, gm_bs=dbs, ssd_conv_w=dcw, ssd_conv_b=dcb,
                  ssd_dt_bias=ddtb, ssd_a_log=dalog, ssd_d=dd, loss=lossp)
    return gx, big, pieces


def _small_layout(pieces):
    rows, segments = [], []
    in_row, out_row = 0, 0

    def add(arr, kind, n_out):
        nonlocal in_row, out_row
        rows.append(arr)
        segments.append((out_row, n_out, in_row, arr.shape[0], kind))
        start = out_row
        in_row += arr.shape[0]
        out_row += n_out
        return start

    where = {}
    for name in _G_VECS:
        where[name] = add(pieces[name], "sum", 1)
    where["gm_ws"] = add(pieces["gm_ws"].reshape(GM_H * CH * CH // D, D), "copy", GM_H * CH * CH // D)
    where["gm_bs"] = add(pieces["gm_bs"].reshape(1, D), "copy", 1)
    cb = jnp.pad(pieces["ssd_conv_b"], ((0, 0), (0, 2 * D - CONV_CH)))
    where["ssd_conv_b"] = add(cb[:, :D], "sum", 1)
    add(cb[:, D:], "sum", 1)
    cw = jnp.pad(pieces["ssd_conv_w"][:CONV_K], ((0, 0), (0, 2 * D - CONV_CH)))
    where["ssd_conv_w"] = add(cw.reshape(2 * CONV_K, D), "copy", 2 * CONV_K)
    misc = jnp.concatenate([pieces["ssd_dt_bias"], pieces["ssd_a_log"], pieces["ssd_d"],
                            jnp.zeros((8, D - 3 * 128), _F32)], axis=1)
    where["misc"] = add(misc, "sum", 1)
    where["loss"] = add(pieces["loss"], "loss", 1)
    n_rows = -(-out_row // 8) * 8
    return jnp.concatenate(rows, axis=0), tuple(segments), n_rows, where


def _pack_small_params(vals, where, n_rows, my_block):
    buf = jnp.zeros((n_rows, D), _F32)

    def put(b, row, arr):
        return lax.dynamic_update_slice(b, arr, (row, 0))

    for name in _G_VECS:
        buf = put(buf, where[name], vals[name].reshape(1, D))
    buf = put(buf, where["gm_ws"], vals["gm_ws"].reshape(GM_H * CH * CH // D, D))
    buf = put(buf, where["gm_bs"], vals["gm_bs"].reshape(1, D))
    cb = jnp.pad(vals["ssd_conv_b"].reshape(1, CONV_CH), ((0, 0), (0, 2 * D - CONV_CH)))
    buf = put(buf, where["ssd_conv_b"], cb.reshape(2, D))
    cw = lax.dynamic_update_slice(jnp.zeros((CONV_K, 2 * D), _F32), vals["ssd_conv_w"].reshape(CONV_K, -1),
                                  (0, my_block * (CONV_CH // N_DEV)))
    buf = put(buf, where["ssd_conv_w"], cw.reshape(2 * CONV_K, D))
    misc = jnp.concatenate([_pad_lanes(vals["ssd_dt_bias"].reshape(1, SSD_H)),
                            _pad_lanes(vals["ssd_a_log"].reshape(1, SSD_H)),
                            _pad_lanes(vals["ssd_d"].reshape(1, SSD_H)), jnp.zeros((1, D - 3 * 128), _F32)], axis=1)
    buf = put(buf, where["misc"], misc)
    return buf


def _unpack_small(buf, where, my_block, shapes):
    out = {}
    for name in _G_VECS:
        out[name] = buf[where[name]].reshape(shapes[name])
    n_ws = GM_H * CH * CH // D
    out["gm_ws"] = buf[where["gm_ws"]:where["gm_ws"] + n_ws].reshape(shapes["gm_ws"])
    out["gm_bs"] = buf[where["gm_bs"]].reshape(shapes["gm_bs"])
    r = where["ssd_conv_b"]
    out["ssd_conv_b"] = buf[r:r + 2].reshape(1, 2 * D)[:, :CONV_CH].reshape(shapes["ssd_conv_b"])
    r = where["ssd_conv_w"]
    cw = buf[r:r + 2 * CONV_K].reshape(CONV_K, 2 * D)
    out["ssd_conv_w"] = lax.dynamic_slice(cw, (0, my_block * (CONV_CH // N_DEV)),
                                          (CONV_K, CONV_CH // N_DEV)).reshape(shapes["ssd_conv_w"])
    misc = buf[where["misc"]]
    for i, name in enumerate(("ssd_dt_bias", "ssd_a_log", "ssd_d")):
        out[name] = misc[i * 128:i * 128 + SSD_H].reshape(shapes[name])
    return out


_WEIGHTS = ("norm_mix_g", "w_in", "gm_v_norm_g", "gm_ws", "gm_bs", "gm_out_norm_g", "ssd_conv_w", "ssd_conv_b",
            "ssd_dt_bias", "ssd_a_log", "ssd_d", "ssd_norm_g", "w_out", "norm_mlp_g", "w_ff1", "w_ff2", "ple_norm_g",
            "w_ple_gate", "w_ple_proj", "final_norm_g")


def kernel(x, p, norm_mix_g, w_in, gm_v_norm_g, gm_ws, gm_bs, gm_out_norm_g, ssd_conv_w, ssd_conv_b, ssd_dt_bias, ssd_a_log, ssd_d, ssd_norm_g, w_out, norm_mlp_g, w_ff1, w_ff2, ple_norm_g, w_ple_gate, w_ple_proj, final_norm_g, loss_target, m_norm_mix_g, m_w_in, m_gm_v_norm_g, m_gm_ws, m_gm_bs, m_gm_out_norm_g, m_ssd_conv_w, m_ssd_conv_b, m_ssd_dt_bias, m_ssd_a_log, m_ssd_d, m_ssd_norm_g, m_w_out, m_norm_mlp_g, m_w_ff1, m_w_ff2, m_ple_norm_g, m_w_ple_gate, m_w_ple_proj, m_final_norm_g, v_norm_mix_g, v_w_in, v_gm_v_norm_g, v_gm_ws, v_gm_bs, v_gm_out_norm_g, v_ssd_conv_w, v_ssd_conv_b, v_ssd_dt_bias, v_ssd_a_log, v_ssd_d, v_ssd_norm_g, v_w_out, v_norm_mlp_g, v_w_ff1, v_w_ff2, v_ple_norm_g, v_w_ple_gate, v_w_ple_proj, v_final_norm_g):
    args = dict(locals())
    w = {n: args[n] for n in _WEIGHTS}
    m = {n: args["m_" + n] for n in _WEIGHTS}
    v = {n: args["v_" + n] for n in _WEIGHTS}
    shapes = {n: w[n].shape for n in _WEIGHTS}
    my_block = 4 * lax.axis_index("x") + 2 * lax.axis_index("y") + lax.axis_index("c")
    nb, seq_len, _ = x.shape

    shard = {n: w[n][0].astype(_MXU) for n in _BIG_NAMES}
    small = {n: w[n] for n in _WEIGHTS if n not in _BIG_NAMES}
    gx, recv, pieces = _local_step(x.reshape(nb * seq_len, D), p.reshape(nb * seq_len, D_PLE),
                                   loss_target.reshape(nb * seq_len, D), shard, ssd_conv_w[0], small, seq_len)

    big_out = [{}, {}, {}, {}]
    for n in _BIG_NAMES:
        res = _sum_adam("sum_adam_" + n, recv[n], w[n][0], m[n][0], v[n][0])
        for k in range(4):
            big_out[k][n] = res[k].reshape(shapes[n])

    parts, segments, n_rows, where = _small_layout(pieces)
    packs = [_pack_small_params(d, where, n_rows, my_block) for d in (w, m, v)]
    small_res = _small_reduce_adam(parts, segments, n_rows, where["loss"], *packs)
    loss = small_res[0][where["loss"], 0]
    small_out = [_unpack_small(a, where, my_block, shapes) for a in small_res]

    outs = [loss, gx.reshape(x.shape)]
    for k in range(4):
        outs += [big_out[k][n] if n in _BIG_NAMES else small_out[k][n] for n in _WEIGHTS]
    return tuple(outs)
```

```python
import functools
import math

import jax
import jax.numpy as jnp
import numpy as np
from jax import lax
from jax.experimental import pallas as pl
from jax.experimental.pallas import tpu as pltpu

_F32 = jnp.float32
_BF16 = jnp.bfloat16
_MXU = jnp.bfloat16

D = 1024
D_PLE = 256
GM_W = 1024
GM_H = 8
CH = 128
SSD_W = 1024
SSD_H = 16
SSD_P = 64
SSD_G = 2
SSD_N = 128
CONV_K = 4
CONV_CH = SSD_W + 2 * SSD_G * SSD_N
D_FF = 4096
D_IN = 2 * GM_W + SSD_W + CONV_CH + SSD_H
D_IN_PAD = 4736
DT_BLK = (D_IN_PAD - 128) // 128
EPS = 1e-6
N_DEV = 8
SHARD_IN = D_IN // N_DEV

LR, B1, B2, ADAM_EPS, WD, STEP = 0.001, 0.9, 0.999, 1e-08, 0.01, 10

_V7X_VMEM_BYTES = 64 * 1024 * 1024
_VMEM_CAP = _V7X_VMEM_BYTES - 8 * 1024 * 1024
_MESH = pl.DeviceIdType.MESH


def _vmem_limit(nbytes):
    return int(min(_VMEM_CAP, max(32 * 1024 * 1024, nbytes * 5 // 4 + (4 << 20))))


def _nbytes(shape, dtype):
    return int(np.prod(shape)) * jnp.dtype(dtype).itemsize


def _mx(v):
    return v.astype(_MXU)


def _dot(a, b):
    return jnp.dot(a, b, preferred_element_type=_F32)


def _dot_nt(a, b):
    return lax.dot_general(a, b, (((1,), (1,)), ((), ())), preferred_element_type=_F32)


def _dot_tn(a, b):
    return lax.dot_general(a, b, (((0,), (0,)), ((), ())), preferred_element_type=_F32)


def _split3(a):
    hi = a.astype(_BF16)
    r = a - hi.astype(_F32)
    mid = r.astype(_BF16)
    lo = (r - mid.astype(_F32)).astype(_BF16)
    return hi, mid, lo


def _xdot(dotfn, a, b01):
    b = b01.astype(_BF16)
    hi, mid, lo = _split3(a)
    return (dotfn(hi, b) + dotfn(mid, b)) + dotfn(lo, b)


def _xdot_left(dotfn, a01, b):
    a = a01.astype(_BF16)
    hi, mid, lo = _split3(b)
    return (dotfn(a, hi) + dotfn(a, mid)) + dotfn(a, lo)


def _sum8(v):
    r, n = v.shape
    return v.reshape(r // 8, 8, n).sum(axis=0)


def _sigmoid(v):
    return 1.0 / (1.0 + jnp.exp(-v))


def _rms(xv, g):
    ms = jnp.mean(xv * xv, axis=-1, keepdims=True)
    return xv * lax.rsqrt(ms + EPS) * g


def _rms_bwd(xv, g, dn):
    n = xv.shape[-1]
    r = lax.rsqrt(jnp.mean(xv * xv, axis=-1, keepdims=True) + EPS)
    gy = dn * g
    dot = jnp.sum(gy * xv, axis=-1, keepdims=True)
    dx = r * gy - xv * (r * r * r) * (dot * (1.0 / n))
    return dx, _sum8(dn * xv * r)


def _iota2(shape, axis):
    return lax.broadcasted_iota(jnp.int32, shape, axis)


def _norm_cast(name, x, g, tm=512, comm=()):
    t, n = x.shape
    tm = min(tm, t)
    steps = t // tm
    kinds = [kind for kind, _ in comm]
    c_in, c_in_specs, c_out_specs, c_out_shape, c_scratch = _comm_io(comm)

    def body(*refs):
        x_ref, g_ref = refs[0], refs[1]
        o_ref = refs[2 + len(comm)]
        comm_refs = (kinds, refs[2:2 + len(comm)], refs[3 + len(comm):3 + 2 * len(comm)], *refs[3 + 2 * len(comm):])
        if comm:
            pl.when(pl.program_id(0) == 0)(lambda: _comm_start(*comm_refs))

        o_ref[...] = _rms(x_ref[...], g_ref[...]).astype(o_ref.dtype)
        if comm:
            pl.when(pl.program_id(0) == steps - 1)(lambda: _comm_finish(*comm_refs))

    res = pl.pallas_call(
        body, name=name, grid=(steps,),
        in_specs=[pl.BlockSpec((tm, n), lambda i: (i, 0)), pl.BlockSpec((1, n), lambda i: (0, 0))] + c_in_specs,
        out_specs=[pl.BlockSpec((tm, n), lambda i: (i, 0))] + c_out_specs,
        out_shape=[jax.ShapeDtypeStruct((t, n), _MXU)] + c_out_shape, scratch_shapes=c_scratch,
        compiler_params=pltpu.CompilerParams(dimension_semantics=("arbitrary",)),
    )(x, g, *c_in)
    return res[0], res[1:]


def _matmul(name, a, b, mode, tm, tn, tk, epilogue, outs, extras=(), comm=()):
    m, k = a.shape[::-1] if mode == "tn" else a.shape
    n = b.shape[0] if mode == "nt" else b.shape[1]
    tm, tn, tk = min(tm, m), min(tn, n), min(tk, k)
    assert m % tm == 0 and n % tn == 0 and k % tk == 0, (name, m, n, k, tm, tn, tk)
    if mode == "nn":
        a_spec = pl.BlockSpec((tm, tk), lambda i, j, kk: (i, kk))
        b_spec = pl.BlockSpec((tk, tn), lambda i, j, kk: (kk, j))
        dotfn = _dot
    elif mode == "nt":
        a_spec = pl.BlockSpec((tm, tk), lambda i, j, kk: (i, kk))
        b_spec = pl.BlockSpec((tn, tk), lambda i, j, kk: (j, kk))
        dotfn = _dot_nt
    else:
        a_spec = pl.BlockSpec((tk, tm), lambda i, j, kk: (kk, i))
        b_spec = pl.BlockSpec((tk, tn), lambda i, j, kk: (kk, j))
        dotfn = _dot_tn
    ni, nj, nk = m // tm, n // tn, k // tk
    n_ex, n_out, n_comm = len(extras), len(outs), len(comm)
    kinds = [kind for kind, _ in comm]
    c_in, c_in_specs, c_out_specs, c_out_shape, c_scratch = _comm_io(comm)

    in_specs, vmem = [a_spec, b_spec], 2 * (tm * tk * a.dtype.itemsize + tk * tn * b.dtype.itemsize)
    for arr, kind in extras:
        if kind == "tile":
            in_specs.append(pl.BlockSpec((tm, tn), lambda i, j, kk: (i, j)))
            vmem += 2 * _nbytes((tm, tn), arr.dtype)
        else:
            in_specs.append(pl.BlockSpec((1, tn), lambda i, j, kk: (0, j)))
    out_specs, out_shape = [], []
    for kind, dt in outs:
        if kind == "tile":
            out_specs.append(pl.BlockSpec((tm, tn), lambda i, j, kk: (i, j)))
            out_shape.append(jax.ShapeDtypeStruct((m, n), dt))
            vmem += 2 * _nbytes((tm, tn), dt)
        else:
            assert nj == 1, "the partial-sum rows are accumulated over consecutive row tiles"
            out_specs.append(pl.BlockSpec((8, tn), lambda i, j, kk: (0, 0)))
            out_shape.append(jax.ShapeDtypeStruct((8, n), dt))
    scratch = [pltpu.VMEM((tm, tn), _F32)] if nk > 1 else []
    vmem += _nbytes((tm, tn), _F32) * (2 if nk > 1 else 1)

    def body(*refs):
        a_ref, b_ref = refs[0], refs[1]
        ex_refs = refs[2:2 + n_ex]
        n_in = 2 + n_ex + n_comm
        out_refs = refs[n_in:n_in + n_out]
        i, j, kk = pl.program_id(0), pl.program_id(1), pl.program_id(2)
        comm_refs = (kinds, refs[2 + n_ex:n_in], refs[n_in + n_out:n_in + n_out + n_comm], *refs[len(refs) - 3:])
        if n_comm:
            pl.when((i == 0) & (j == 0) & (kk == 0))(lambda: _comm_start(*comm_refs))

        part = dotfn(_mx(a_ref[...]), _mx(b_ref[...]))

        def finish(acc):
            vals = epilogue(acc, *[r[...] for r in ex_refs])
            for r, v, (kind, _) in zip(out_refs, vals, outs):
                if kind == "part8":
                    @pl.when(i == 0)
                    def _():
                        r[...] = v

                    @pl.when(i > 0)
                    def _():
                        r[...] += v
                else:
                    r[...] = v.astype(r.dtype)

        if nk == 1:
            finish(part)
        else:
            acc_ref = refs[n_in + n_out + n_comm]

            @pl.when(kk == 0)
            def _():
                acc_ref[...] = part

            @pl.when(kk > 0)
            def _():
                acc_ref[...] += part

            @pl.when(kk == nk - 1)
            def _():
                finish(acc_ref[...])

        if n_comm:
            pl.when((i == ni - 1) & (j == nj - 1) & (kk == nk - 1))(lambda: _comm_finish(*comm_refs))

    carried =n_comm or any(kind == "part8" for kind, _ in outs)
    sem = ("arbitrary",) * 3 if carried else ("parallel", "parallel", "arbitrary")
    res = pl.pallas_call(
        body, name=name, grid=(ni, nj, nk),
        in_specs=in_specs + c_in_specs, out_specs=out_specs + c_out_specs, out_shape=out_shape + c_out_shape,
        scratch_shapes=scratch + c_scratch,
        compiler_params=pltpu.CompilerParams(dimension_semantics=sem, vmem_limit_bytes=_vmem_limit(vmem)),
    )(a, b, *[arr for arr, _ in extras], *c_in)
    return res[:n_out], res[n_out:]


def _shift_down(v, halo8, j):
    if j == 0:
        return v
    r = pltpu.roll(v, j, axis=0)
    hr = pltpu.roll(halo8, j, axis=0)
    top = jnp.where(_iota2(hr.shape, 0) < j, hr, r[:8])
    return jnp.concatenate([top, r[8:]], axis=0)


def _shift_up(v, next8, j):
    if j == 0:
        return v
    rows = v.shape[0]
    r = pltpu.roll(v, rows - j, axis=0)
    nr = pltpu.roll(next8, 8 - j, axis=0)
    bot = jnp.where(_iota2(nr.shape, 0) >= 8 - j, nr, r[rows - 8:])
    return jnp.concatenate([r[:rows - 8], bot], axis=0)


def _silu_grad(c):
    s = _sigmoid(c)
    return s * (1.0 + c * (1.0 - s))


def _gmlp_fwd_vals(pu, pv, gv, ws_ref, bsb_ref, want_bwd):
    tril = _iota2((CH, CH), 0) >= _iota2((CH, CH), 1)
    cdf_u = 0.5 * (1.0 + lax.erf(pu * 0.7071067811865476))
    cdf_v = 0.5 * (1.0 + lax.erf(pv * 0.7071067811865476))
    u = pu * cdf_u
    v = pv * cdf_v
    ys, keep = [], [(cdf_u, cdf_v)] if want_bwd else []
    for h in range(GM_H):
        sl = slice(h * 128, (h + 1) * 128)
        vh = v[:, sl]
        r = lax.rsqrt(jnp.mean(vh * vh, axis=-1, keepdims=True) + EPS)
        vn = vh * r * gv[:, sl]
        wm = _mx(jnp.where(tril, ws_ref[h], 0.0))
        mixed = _dot(wm, _mx(vn)) + bsb_ref[h]
        ys.append(u[:, sl] * mixed)
        if want_bwd:
            keep.append((vh, r, vn, wm, mixed))
    return jnp.concatenate(ys, axis=1), u, keep


def _ssd_common(xbc, halo8, dtraw, cw_ref, cb, dtb, alog, e_ref, ltri):
    xs = [_shift_down(xbc, halo8, j) for j in range(CONV_K)]
    cpre = cb + sum(cw_ref[k:k + 1, :] * xs[CONV_K - 1 - k] for k in range(CONV_K))
    act = cpre * _sigmoid(cpre)
    dtin = dtraw + dtb
    dt = jnp.maximum(dtin, 0.0) + jnp.log(1.0 + jnp.exp(-jnp.abs(dtin)))
    a_neg = -jnp.exp(alog)
    cs = _xdot_left(_dot, ltri, dt * a_neg)
    cs_last = cs[CH - 1:CH, :]
    ecs = jnp.exp(cs)
    dec = jnp.exp(cs_last - cs)
    cdec = jnp.exp(cs_last)
    e = e_ref[...]
    dt_x = _dot(dt.astype(_BF16), e)
    ecs_x = _dot(ecs.astype(_BF16), e)
    dec_x = _dot(dec.astype(_BF16), e)
    cdec_x = _xdot(_dot, jnp.broadcast_to(cdec, (8, 128)), e)[0:1, :]
    return dict(xs=xs, cpre=cpre, act=act, dtin=dtin, dt=dt, a_neg=a_neg, cs=cs, ecs=ecs, dec=dec, cdec=cdec,
                dt_x=dt_x, ecs_x=ecs_x, dec_x=dec_x, cdec_x=cdec_x)


def _head_lm(cs, cst_ref, h, tril):
    seg = jnp.broadcast_to(cs[:, h:h + 1], (CH, CH)) - cst_ref[h:h + 1, :]
    return jnp.exp(jnp.where(tril, seg, -jnp.inf))


def _mixer_fwd(proj, gv, ws, bsb, gout, cw8, cb, dtb, alog, d_x, ng, e_mat, ltri_mat, seq_chunks, comm=()):
    t = proj.shape[0]
    n_chunks = t // CH
    n_comm = len(comm)
    kinds = [kind for kind, _ in comm]
    c_in, c_in_specs, c_out_specs, c_out_shape, c_scratch = _comm_io(comm)

    def body(*refs):
        (pu_ref, pv_ref, z_ref, xbc_ref, dt_ref, halo_ref, gv_ref, ws_ref, bsb_ref, gout_ref, cw_ref, cb_ref,
         dtb_ref, alog_ref, dx_ref, ng_ref, e_ref, ltri_ref) = refs[:18]
        cat_ref, y_ref, st_ref = refs[18 + n_comm:21 + n_comm]
        s_ref, cst_ref = refs[21 + 2 * n_comm:23 + 2 * n_comm]
        comm_refs = (kinds, refs[18:18 + n_comm], refs[21 + n_comm:21 + 2 * n_comm], *refs[23 + 2 * n_comm:])
        c = pl.program_id(0)
        if n_comm:
            pl.when(c == 0)(lambda: _comm_start(*comm_refs))
            pl.when(c == n_chunks - 1)(lambda: _comm_finish(*comm_refs))

        first = (c % seq_chunks) == 0
        tril = _iota2((CH, CH), 0) >= _iota2((CH, CH), 1)
        lane = _iota2((CH, 128), 1)

        y_a, _, _ = _gmlp_fwd_vals(pu_ref[...], pv_ref[...], gv_ref[...], ws_ref, bsb_ref, False)
        cat_ref[:, 0:GM_W] = _rms(y_a, gout_ref[...]).astype(cat_ref.dtype)

        @pl.when(first)
        def _():
            s_ref[...] = jnp.zeros_like(s_ref)

        halo8 = jnp.where(first, 0.0, halo_ref[...])
        q = _ssd_common(xbc_ref[...], halo8, dt_ref[...], cw_ref, cb_ref[...], dtb_ref[...], alog_ref[...], e_ref,
                        ltri_ref[...])
        act = q["act"]
        xv = act[:, 0:SSD_W]
        xdt = xv * q["dt_x"]
        xdt_m = _mx(xdt)
        cs = q["cs"]
        cst_ref[...] = cs.T
        s_prev = s_ref[...]
        st_ref[...] = s_prev
        ys = []
        for g in range(SSD_G):
            bg = _mx(act[:, SSD_W + g * SSD_N:SSD_W + (g + 1) * SSD_N])
            cg = _mx(act[:, SSD_W + SSD_G * SSD_N + g * SSD_N:SSD_W + SSD_G * SSD_N + (g + 1) * SSD_N])
            cbm = _dot_nt(cg, bg)
            gs = slice(g * 512, (g + 1) * 512)
            for pr in range(4):
                ps = slice(g * 512 + pr * 128, g * 512 + (pr + 1) * 128)
                o = []
                for hh in range(2):
                    h = g * 8 + pr * 2 + hh
                    m_h = _mx(cbm * _head_lm(cs, cst_ref, h, tril))
                    o.append(_dot(m_h, xdt_m[:, ps]))
                ys.append(jnp.where(lane < SSD_P, o[0], o[1]))
            sg = s_prev[:, gs]
            yoff = _dot(cg, _mx(sg)) * q["ecs_x"][:, gs]
            ys[-4:] = [ys[-4 + i] + yoff[:, i * 128:(i + 1) * 128] for i in range(4)]
            st_new = _dot_tn(bg, _mx(q["dec_x"][:, gs] * xdt[:, gs]))
            s_ref[:, gs] = sg * q["cdec_x"][:, gs] + st_new
        y = jnp.concatenate(ys, axis=1) + dx_ref[...] * xv
        y_ref[...] = y
        zv = z_ref[...]
        yg = y * (zv * _sigmoid(zv))
        for g in range(SSD_G):
            gs = slice(g * 512, (g + 1) * 512)
            cat_ref[:, GM_W + g * 512:GM_W + (g + 1) * 512] = _rms(yg[:, gs], ng_ref[:, gs]).astype(cat_ref.dtype)

    blk = lambda w, j: pl.BlockSpec((CH, w), lambda c: (c, j))
    full = lambda arr: pl.BlockSpec(arr.shape, lambda c: (0,) * arr.ndim)
    consts = [gv, ws, bsb, gout, cw8, cb, dtb, alog, d_x, ng, e_mat, ltri_mat]
    res = pl.pallas_call(
        body, name="mixer_fwd", grid=(n_chunks,),
        in_specs=[blk(GM_W, 0), blk(GM_W, 1), blk(SSD_W, 2), blk(CONV_CH, 2), blk(128, DT_BLK),
                  pl.BlockSpec((8, CONV_CH), lambda c: (jnp.maximum(c * (CH // 8) - 1, 0), 2))]
        + [full(a) for a in consts] + c_in_specs,
        out_specs=[pl.BlockSpec((CH, 2 * D), lambda c: (c, 0)), pl.BlockSpec((CH, SSD_W), lambda c: (c, 0)),
                   pl.BlockSpec((CH, SSD_W), lambda c: (c, 0))] + c_out_specs,
        out_shape=[jax.ShapeDtypeStruct((t, 2 * D), _MXU), jax.ShapeDtypeStruct((t, SSD_W), _F32),
                   jax.ShapeDtypeStruct((t, SSD_W), _F32)] + c_out_shape,
        scratch_shapes=[pltpu.VMEM((SSD_N, SSD_W), _F32), pltpu.VMEM((128, CH), _F32)] + c_scratch,
        compiler_params=pltpu.CompilerParams(dimension_semantics=("arbitrary",), vmem_limit_bytes=48 << 20),
    )(proj, proj, proj, proj, proj, proj, *consts, *c_in)
    return res[:3], res[3:]


def _mixer_bwd(proj, dcat, yss, states, gv, ws, bsb, gout, cw8, cb, dtb, alog, d_x, ng, e_mat, et_mat, ltri_mat,
               seq_chunks):
    t = proj.shape[0]
    n_chunks = t // CH

    def body(pu_ref, pv_ref, z_ref, xbc_ref, dt_ref, halo_ref, dcat_ref, y_ref, st_ref,
             gv_ref, ws_ref, bsb_ref, gout_ref, cw_ref, cb_ref, dtb_ref, alog_ref, dx_ref, ng_ref,
             e_ref, et_ref, ltri_ref,
             dproj_ref, dws_ref, dbs_ref, dgv_ref, dgout_ref, dng_ref, dcw_ref, dcb_ref, ddtb_ref, dalog_ref, dd_ref,
             ds_ref, dnext_ref, dcst_ref, cst_ref, dbacc_ref, ddacc_ref):
        i = pl.program_id(0)
        c = n_chunks - 1 - i
        first = (c % seq_chunks) == 0
        last_in_seq = (c % seq_chunks) == seq_chunks - 1
        tril = _iota2((CH, CH), 0) >= _iota2((CH, CH), 1)
        lane = _iota2((CH, 128), 1)
        row = _iota2((CH, 128), 0)

        @pl.when(i == 0)
        def _():
            for r in (dws_ref, dbs_ref, dgv_ref, dgout_ref, dng_ref, dcw_ref, dcb_ref, ddtb_ref, dalog_ref, dd_ref,
                      dbacc_ref, ddacc_ref, dcst_ref):
                r[...] = jnp.zeros_like(r)

        @pl.when(last_in_seq)
        def _():
            ds_ref[...] = jnp.zeros_like(ds_ref)
            dnext_ref[...] = jnp.zeros_like(dnext_ref)

        dcat_v = dcat_ref[...].astype(_F32)

        pu, pv = pu_ref[...], pv_ref[...]
        gv_v = gv_ref[...]
        y_a, u, keep = _gmlp_fwd_vals(pu, pv, gv_v, ws_ref, bsb_ref, True)
        dy, dgout8 = _rms_bwd(y_a, gout_ref[...], dcat_v[:, 0:GM_W])
        dgout_ref[...] += dgout8
        dus, dvs, dgvs = [], [], []
        for h in range(GM_H):
            sl = slice(h * 128, (h + 1) * 128)
            vh, r, vn, wm, mixed = keep[h + 1]
            dyh = dy[:, sl]
            dus.append(dyh * mixed)
            dmix = dyh * u[:, sl]
            dmix_m = _mx(dmix)
            dws_ref[h] += jnp.where(tril, _dot_nt(dmix_m, _mx(vn)), 0.0)
            dbacc_ref[h] += dmix
            dvn = _dot_tn(wm, dmix_m)
            gy = dvn * gv_v[:, sl]
            dot = jnp.sum(gy * vh, axis=-1, keepdims=True)
            dvs.append(r * gy - vh * (r * r * r) * (dot * (1.0 / 128)))
            dgvs.append(_sum8(dvn * vh * r))
        dgv_ref[...] += jnp.concatenate(dgvs, axis=1)
        cdf_u, cdf_v = keep[0]
        gelu_grad = lambda pre, cdf: cdf + pre * jnp.exp(-0.5 * pre * pre) * 0.3989422804014327
        dproj_ref[:, 0:GM_W] = (jnp.concatenate(dus, axis=1) * gelu_grad(pu, cdf_u)).astype(dproj_ref.dtype)
        dproj_ref[:, GM_W:2 * GM_W] = (jnp.concatenate(dvs, axis=1) * gelu_grad(pv, cdf_v)).astype(dproj_ref.dtype)

        halo8 = jnp.where(first, 0.0, halo_ref[...])
        q = _ssd_common(xbc_ref[...], halo8, dt_ref[...], cw_ref, cb_ref[...], dtb_ref[...], alog_ref[...], e_ref,
                        ltri_ref[...])
        act = q["act"]
        xv = act[:, 0:SSD_W]
        dt_x, ecs_x, dec_x, cdec_x = q["dt_x"], q["ecs_x"], q["dec_x"], q["cdec_x"]
        xdt = xv * dt_x
        xdt_m = _mx(xdt)
        cs = q["cs"]
        cst_ref[...] = cs.T
        s_prev = st_ref[...]
        ds = ds_ref[...]
        yv = y_ref[...]
        zv = z_ref[...]
        sz = zv * _sigmoid(zv)
        yg = yv * sz
        dygs, dng8 = [], []
        for g in range(SSD_G):
            gs = slice(g * 512, (g + 1) * 512)
            a_, b_ = _rms_bwd(yg[:, gs], ng_ref[:, gs], dcat_v[:, GM_W + g * 512:GM_W + (g + 1) * 512])
            dygs.append(a_)
            dng8.append(b_)
        dyg = jnp.concatenate(dygs, axis=1)
        dng_ref[...] += jnp.concatenate(dng8, axis=1)
        dyv = dyg * sz
        dproj_ref[:, 2 * GM_W:2 * GM_W + SSD_W] = (dyg * yv * _silu_grad(zv)).astype(dproj_ref.dtype)
        ddacc_ref[...] += _sum8(dyv * xv)
        dyv_m = _mx(dyv)

        dxdt_parts, db_parts, dc_parts = [], [], []
        dcs = jnp.zeros((CH, 128), _F32)
        dcs_x_parts, ddec_x_parts, dcl_x_parts = [], [], []
        for g in range(SSD_G):
            gs = slice(g * 512, (g + 1) * 512)
            bg = _mx(act[:, SSD_W + g * SSD_N:SSD_W + (g + 1) * SSD_N])
            cg = _mx(act[:, SSD_W + SSD_G * SSD_N + g * SSD_N:SSD_W + SSD_G * SSD_N + (g + 1) * SSD_N])
            cbm = _dot_nt(cg, bg)
            sg = s_prev[:, gs]
            sg_m = _mx(sg)
            dsg = ds[:, gs]
            dsg_m = _mx(dsg)
            zoff = _dot(cg, sg_m)
            dz_off = dyv[:, gs] * ecs_x[:, gs]
            dz_off_m = _mx(dz_off)
            dcs_x_parts.append(dyv[:, gs] * zoff * ecs_x[:, gs])
            dcg = _dot_nt(dz_off_m, sg_m)
            dsprev = _dot_tn(cg, dz_off_m)
            w_st = dec_x[:, gs] * xdt[:, gs]
            dw_st = _dot(bg, dsg_m)
            dbg = _dot_nt(_mx(w_st), dsg_m)
            dxdt_g = dec_x[:, gs] * dw_st
            ddec_x_parts.append(dw_st * xdt[:, gs])
            dsprev = dsprev + cdec_x[:, gs] * dsg
            dcl_x_parts.append(jnp.sum(dsg * sg, axis=0, keepdims=True) * cdec_x[:, gs])
            ds_ref[:, gs] = dsprev
            dcb = jnp.zeros((CH, CH), _F32)
            dxdt_pairs = []
            for pr in range(4):
                ps = slice(g * 512 + pr * 128, g * 512 + (pr + 1) * 128)
                acc_pair = None
                for hh in range(2):
                    h = g * 8 + pr * 2 + hh
                    in_head = (lane < SSD_P) if hh == 0 else (lane >= SSD_P)
                    lm = _head_lm(cs, cst_ref, h, tril)
                    m_h = cbm * lm
                    m_hm = _mx(m_h)
                    dyh_m = _mx(jnp.where(in_head, dyv[:, ps], 0.0))
                    dm = _dot_nt(dyh_m, xdt_m[:, ps])
                    dcb = dcb + dm * lm
                    qm = dm * m_h
                    dcs = dcs + jnp.where(lane == h, jnp.sum(qm, axis=1, keepdims=True), 0.0)
                    dcst_ref[h:h + 1, :] = jnp.sum(qm, axis=0, keepdims=True)
                    contrib = jnp.where(in_head, _dot_tn(m_hm, dyv_m[:, ps]), 0.0)
                    acc_pair = contrib if acc_pair is None else acc_pair + contrib
                dxdt_pairs.append(acc_pair)
            dxdt_parts.append(dxdt_g + jnp.concatenate(dxdt_pairs, axis=1))
            dcb_m = _mx(dcb)
            dc_parts.append(dcg + _dot(dcb_m, bg))
            db_parts.append(dbg + _dot_tn(dcb_m, cg))
        dxdt = jnp.concatenate(dxdt_parts, axis=1)
        dxv = dx_ref[...] * dyv + dxdt * dt_x
        et = et_ref[...]
        head_sum = lambda v: _dot(v.astype(_BF16), et)
        ddt = head_sum(dxdt * xv)
        dcs = dcs - dcst_ref[...].T + head_sum(jnp.concatenate(dcs_x_parts, axis=1))
        ddec = head_sum(jnp.concatenate(ddec_x_parts, axis=1)) * q["dec"]
        dcs = dcs - ddec
        dcl = jnp.sum(ddec, axis=0, keepdims=True) + _xdot(
            _dot, jnp.broadcast_to(jnp.concatenate(dcl_x_parts, axis=1), (8, SSD_W)), et)[0:1, :]
        dcs = jnp.where(row == CH - 1, dcs + dcl, dcs)
        da = _xdot_left(_dot_tn, ltri_ref[...], dcs)
        ddt = ddt + da * q["a_neg"]
        dalog_ref[...] += _sum8(da * q["dt"] * q["a_neg"])
        ddtraw = jnp.where(lane < SSD_H, ddt * _sigmoid(q["dtin"]), 0.0)
        ddtb_ref[...] += _sum8(ddtraw)
        dproj_ref[:, D_IN_PAD - 128:D_IN_PAD] = ddtraw.astype(dproj_ref.dtype)
        dcpre = jnp.concatenate([dxv] + db_parts + dc_parts, axis=1) * _silu_grad(q["cpre"])
        dcb_ref[...] += _sum8(dcpre)
        for k in range(CONV_K):
            dcw_ref[k:k + 1, :] += jnp.sum(dcpre * q["xs"][CONV_K - 1 - k], axis=0, keepdims=True)
        next8 = dnext_ref[...]
        dxbc = sum(cw_ref[k:k + 1, :] * _shift_up(dcpre, next8, CONV_K - 1 - k) for k in range(CONV_K))
        dproj_ref[:, 2 * GM_W + SSD_W:2 * GM_W + SSD_W + CONV_CH] = dxbc.astype(dproj_ref.dtype)
        dnext_ref[...] = dcpre[0:8, :]

        @pl.when(i == n_chunks - 1)
        def _():
            for h in range(GM_H):
                dbs_ref[h:h + 1, :] = _xdot_left(_dot_nt, jnp.ones((8, 128), _BF16), dbacc_ref[h])[0:1, :]
            dd_ref[...] = _xdot(_dot, ddacc_ref[...], et)

    rblk = lambda w, j: pl.BlockSpec((CH, w), lambda i: (n_chunks - 1 - i, j))
    full = lambda arr: pl.BlockSpec(arr.shape, lambda i: (0,) * arr.ndim)
    acc = lambda shape: pl.BlockSpec(shape, lambda i: (0,) * len(shape))
    consts = [gv, ws, bsb, gout, cw8, cb, dtb, alog, d_x, ng, e_mat, et_mat, ltri_mat]
    acc_shapes = [(GM_H, CH, CH), (8, 128), (8, GM_W), (8, GM_W), (8, SSD_W), (8, CONV_CH), (8, CONV_CH), (8, 128),
                  (8, 128), (8, 128)]
    return pl.pallas_call(
        body, name="mixer_bwd", grid=(n_chunks,),
        in_specs=[rblk(GM_W, 0), rblk(GM_W, 1), rblk(SSD_W, 2), rblk(CONV_CH, 2), rblk(128, DT_BLK),
                  pl.BlockSpec((8, CONV_CH), lambda i: (jnp.maximum((n_chunks - 1 - i) * (CH // 8) - 1, 0), 2)),
                  rblk(2 * D, 0), rblk(SSD_W, 0), rblk(SSD_W, 0)] + [full(a) for a in consts],
        out_specs=[rblk(D_IN_PAD, 0)] + [acc(s) for s in acc_shapes],
        out_shape=[jax.ShapeDtypeStruct((t, D_IN_PAD), _MXU)] + [jax.ShapeDtypeStruct(s, _F32) for s in acc_shapes],
        scratch_shapes=[pltpu.VMEM((SSD_N, SSD_W), _F32), pltpu.VMEM((8, CONV_CH), _F32),
                        pltpu.VMEM((128, CH), _F32), pltpu.VMEM((128, CH), _F32),
                        pltpu.VMEM((GM_H, CH, 128), _F32), pltpu.VMEM((8, SSD_W), _F32)],
        compiler_params=pltpu.CompilerParams(dimension_semantics=("arbitrary",), vmem_limit_bytes=48 << 20),
    )(proj, proj, proj, proj, proj, proj, dcat, yss, states, *consts)


def _peers():
    x, y, c = lax.axis_index("x"), lax.axis_index("y"), lax.axis_index("c")
    out = []
    for k in range(1, N_DEV):
        fx, fy, fc = (k >> 2) & 1, (k >> 1) & 1, k & 1
        px, py, pc = (x + fx) % 2, (y + fy) % 2, (c + fc) % 2
        out.append((k - 1, (px, py, pc), 4 * px + 2 * py + pc))
    return out, 4 * x + 2 * y + c


def _comm_io(comm):
    any_spec = pl.BlockSpec(memory_space=pl.ANY)
    n = len(comm)
    out_shape = []
    for (kind, axis), src in comm:
        shp = list(src.shape)
        if kind in ("gather", "gather2"):
            shp[axis] *= N_DEV
        else:
            shp[axis] //= N_DEV
            shp = [N_DEV] + shp
        out_shape.append(jax.ShapeDtypeStruct(tuple(shp), src.dtype))
    scratch = [pltpu.SemaphoreType.DMA((n * (N_DEV - 1),)), pltpu.SemaphoreType.DMA((n * (N_DEV - 1),)),
               pltpu.SemaphoreType.DMA((n,))] if n else []
    return [src for _, src in comm], [any_spec] * n, [any_spec] * n, out_shape, scratch


def _window(ref, axis, idx, size):
    start = pl.multiple_of(idx * size, size)
    return ref.at[tuple(pl.ds(start, size) if a == axis else slice(None) for a in range(len(ref.shape)))]


def _comm_plans(kinds, src_refs, dst_refs, send_sems, recv_sems, local_sems):
    x, y, c = lax.axis_index("x"), lax.axis_index("y"), lax.axis_index("c")
    peers, me = _peers()
    plans = []
    for s, ((kind, axis), src, dst) in enumerate(zip(kinds, src_refs, dst_refs)):
        sems = lambda k: dict(send_sem=send_sems.at[s * (N_DEV - 1) + k], recv_sem=recv_sems.at[s * (N_DEV - 1) + k])
        remote = lambda src_ref, dst_ref, k, pid: pltpu.make_async_remote_copy(
            src_ref=src_ref, dst_ref=dst_ref, device_id=pid, device_id_type=_MESH, **sems(k))
        if kind == "gather2":
            size = src.shape[axis]
            win = lambda idx: _window(dst, axis, idx, size)
            sib, sib_idx = (x, y, 1 - c), 4 * x + 2 * y + (1 - c)
            local = pltpu.make_async_copy(src, win(me), local_sems.at[s])
            to_sib = remote(src, win(me), 0, sib)
            starts, forwards = [local, to_sib], []
            waits = [(local, "local"), (to_sib, "send"), (remote(src, win(sib_idx), 0, sib), "recv")]
            for j, (fx, fy) in enumerate(((1, 0), (0, 1), (1, 1))):
                px, py = (x + fx) % 2, (y + fy) % 2
                same, other = 4 * px + 2 * py + c, 4 * px + 2 * py + (1 - c)
                out = remote(src, win(me), 1 + j, (px, py, c))
                starts.append(out)
                passed = remote(win(same), win(same), 4 + j, sib)
                forwards.append((remote(src, win(same), 1 + j, (px, py, c)), passed))
                waits += [(out, "send"), (passed, "send"), (remote(win(other), win(other), 4 + j, sib), "recv")]
            plans.append((starts, forwards, waits))
            continue
        if kind == "gather":
            size = src.shape[axis]
            src_for = lambda pidx: src
            dst_mine = _window(dst, axis, me, size)
        else:
            size = src.shape[axis] // N_DEV
            src_for = lambda pidx: _window(src, axis, pidx, size)
            dst_mine = dst.at[me]
        local = pltpu.make_async_copy(src_for(me), dst_mine, local_sems.at[s])
        remotes = [remote(src_for(pidx), dst_mine, k, pid) for k, pid, pidx in peers]
        plans.append(([local] + remotes, [], [(local, "local")] + [(cp, "both") for cp in remotes]))
    return plans


def _comm_start(*refs):
    for starts, _, _ in _comm_plans(*refs):
        for cp in starts:
            cp.start()


def _comm_finish(*refs):
    for _, forwards, waits in _comm_plans(*refs):
        for arrival, cp in forwards:
            arrival.wait_recv()
            cp.start()
        for cp, what in waits:
            if what == "send":
                cp.wait_send()
            elif what == "recv":
                cp.wait_recv()
            else:
                cp.wait()


def _adam_vals(w, g, m, v):
    m = B1 * m + (1.0 - B1) * g
    v = B2 * v + (1.0 - B2) * (g * g)
    m_hat = m / (1.0 - B1 ** STEP)
    v_hat = v / (1.0 - B2 ** STEP)
    delta = -LR * (m_hat / (jnp.sqrt(v_hat) + ADAM_EPS) + WD * w)
    return delta, m, v


def _sum_adam(name, recv, w, m, v, tr=256):
    _, r, wd = recv.shape
    tr = min(tr, r)

    def body(recv_ref, w_ref, m_ref, v_ref, g_out, d_out, m_out, v_out):
        g = recv_ref[0].astype(_F32)
        for s in range(1, N_DEV):
            g = g + recv_ref[s].astype(_F32)
        d_, m_, v_ = _adam_vals(w_ref[...], g, m_ref[...], v_ref[...])
        g_out[...] = g
        d_out[...] = d_
        m_out[...] = m_
        v_out[...] = v_

    spec = pl.BlockSpec((tr, wd), lambda i: (i, 0))
    return pl.pallas_call(
        body, name=name, grid=(r // tr,),
        in_specs=[pl.BlockSpec((N_DEV, tr, wd), lambda i: (0, i, 0)), spec, spec, spec],
        out_specs=[spec] * 4, out_shape=[jax.ShapeDtypeStruct((r, wd), _F32)] * 4,
        compiler_params=pltpu.CompilerParams(dimension_semantics=("parallel",), vmem_limit_bytes=48 << 20),
    )(recv, w, m, v)


def _small_local(parts, segments, n_rows):
    def body(parts_ref, loc_ref):
        loc_ref[...] = jnp.zeros_like(loc_ref)
        for out_row, n_out, in_row, n_in, kind in segments:
            if kind == "copy":
                loc_ref[out_row:out_row + n_out, :] = parts_ref[in_row:in_row + n_in, :]
            else:
                s = jnp.sum(parts_ref[in_row:in_row + n_in, :], axis=0, keepdims=True)
                if kind == "loss":
                    s = jnp.broadcast_to(jnp.sum(s, axis=1, keepdims=True) * (0.5 / D), (1, D))
                loc_ref[out_row:out_row + 1, :] = s

    vm = pl.BlockSpec(memory_space=pltpu.VMEM)
    return pl.pallas_call(body, name="small_local", in_specs=[vm], out_specs=vm,
                          out_shape=jax.ShapeDtypeStruct((n_rows, D), _F32))(parts)


def _small_final(blocks, late8, late_row, w, m, v):
    n_rows = w.shape[0]

    def body(blocks_ref, late_ref, w_ref, m_ref, v_ref, g_out, d_out, m_out, v_out, loc_ref, recv_ref, send_sems,
             recv_sems):
        peers, me = _peers()
        loc_ref[...] = jnp.broadcast_to(jnp.sum(late_ref[...], axis=0, keepdims=True), (8, D))
        recv_ref[me] = loc_ref[...]
        copies = [pltpu.make_async_remote_copy(src_ref=loc_ref, dst_ref=recv_ref.at[me], send_sem=send_sems.at[k],
                                               recv_sem=recv_sems.at[k], device_id=pid, device_id_type=_MESH)
                  for k, pid, _ in peers]
        for cp in copies:
            cp.start()
        g = blocks_ref[0:n_rows, :]
        for s in range(1, N_DEV):
            g = g + blocks_ref[s * n_rows:(s + 1) * n_rows, :]
        for cp in copies:
            cp.wait()
        late = recv_ref[0]
        for s in range(1, N_DEV):
            late = late + recv_ref[s]
        g = jnp.where(_iota2((n_rows, D), 0) == late_row, jnp.broadcast_to(late[0:1, :], (n_rows, D)), g)
        d_, m_, v_ = _adam_vals(w_ref[...], g, m_ref[...], v_ref[...])
        g_out[...] = g
        d_out[...] = d_
        m_out[...] = m_
        v_out[...] = v_

    vm = pl.BlockSpec(memory_space=pltpu.VMEM)
    return pl.pallas_call(
        body, name="small_final", in_specs=[vm] * 5, out_specs=[vm] * 4,
        out_shape=[jax.ShapeDtypeStruct((n_rows, D), _F32)] * 4,
        scratch_shapes=[pltpu.VMEM((8, D), _F32), pltpu.VMEM((N_DEV, 8, D), _F32),
                        pltpu.SemaphoreType.DMA((N_DEV - 1,)), pltpu.SemaphoreType.DMA((N_DEV - 1,))],
        compiler_params=pltpu.CompilerParams(vmem_limit_bytes=48 << 20),
    )(blocks, late8, w, m, v)


_BIG_NAMES =("w_in", "w_out", "w_ff1", "w_ff2", "w_ple_gate", "w_ple_proj")

_G_VECS = ("norm_mix_g", "gm_v_norm_g", "gm_out_norm_g", "ssd_norm_g", "norm_mlp_g", "ple_norm_g", "final_norm_g")
_LATE = _G_VECS[0]


def _const_mats():
    h = np.arange(128)[:, None]
    ch = np.arange(SSD_W)[None, :]
    e = (ch // SSD_P == h).astype(np.float32)
    ltri = (np.arange(CH)[:, None] >= np.arange(CH)[None, :]).astype(np.float32)
    return jnp.asarray(e, _BF16), jnp.asarray(e.T, _BF16), jnp.asarray(ltri, _BF16)


def _pad_lanes(v, n=128):
    return jnp.pad(v, ((0, 0), (0, n - v.shape[1])))


def _local_step(x, p, tgt, shard, conv_w_shard, small, seq_len):
    seq_chunks = seq_len // CH
    e_mat, et_mat, ltri_mat = _const_mats()
    g_mix, g_mlp, g_ple = small["norm_mix_g"], small["norm_mlp_g"], small["ple_norm_g"]
    g_fin = small["final_norm_g"].reshape(1, D)
    gv, gout, ng = small["gm_v_norm_g"], small["gm_out_norm_g"], small["ssd_norm_g"]
    ws = small["gm_ws"][0]
    bsb = jnp.broadcast_to(small["gm_bs"][0][:, :, None], (GM_H, CH, 128))
    cb = small["ssd_conv_b"]
    dtb, alog = _pad_lanes(small["ssd_dt_bias"]), _pad_lanes(small["ssd_a_log"])
    d_x = jnp.repeat(small["ssd_d"], SSD_P, axis=1)

    first = lambda acc: (acc,)
    rows, cols = ("gather2", 0), ("gather2", 1)
    n1, (g_win, g_cw) = _norm_cast("norm_mix", x, g_mix,
                                   comm=[(rows, shard["w_in"][None]), (("gather", 0), conv_w_shard[None])])
    w_in = jnp.pad(g_win.transpose(1, 0, 2).reshape(D, D_IN), ((0, 0), (0, D_IN_PAD - D_IN)))
    cw8 = jnp.pad(g_cw.transpose(1, 0, 2).reshape(CONV_K, CONV_CH), ((0, 8 - CONV_K), (0, 0)))
    mix_consts = (gv, ws, bsb, gout, cw8, cb, dtb, alog, d_x, ng)
    (proj,), (w_out, w1) = _matmul("proj_in", n1, w_in, "nn", 256, D_IN_PAD, D, first, [("tile", _F32)],
                                   comm=[(rows, shard["w_out"]), (cols, shard["w_ff1"])])
    (cat, yss, states), (w2, wg, wp) = _mixer_fwd(
        proj, *mix_consts, e_mat, ltri_mat, seq_chunks,
        comm=[(rows, shard["w_ff2"]), (rows, shard["w_ple_gate"]), (cols, shard["w_ple_proj"])])

    def epi_res_norm(acc, res, g):
        hv = acc + res
        return hv, _rms(hv, g)

    (h1, n2), _ = _matmul("proj_out", cat, w_out, "nn", 512, D, 2 * D, epi_res_norm,
                          [("tile", _F32), ("tile", _MXU)], extras=[(x, "tile"), (g_mlp, "row")])

    def epi_relu2(acc):
        hid = jnp.maximum(acc, 0.0)
        return hid, hid * hid

    (hid, hid2), _ = _matmul("ff1", n2, w1, "nn", 512, 1024, D, epi_relu2, [("tile", _MXU), ("tile", _MXU)])
    (h2, n3), _ = _matmul("ff2", hid2, w2, "nn", 512, D, D_FF, epi_res_norm, [("tile", _F32), ("tile", _MXU)],
                          extras=[(h1, "tile"), (g_ple, "row")])
    (pp,), _ = _matmul("ple_proj", p, wp, "nn", 512, D, D_PLE, first, [("tile", _F32)])

    def epi_head(acc, ppv, h2v, tg, gf):
        gate = _sigmoid(acc)
        h3 = h2v + gate * ppv
        r = lax.rsqrt(jnp.mean(h3 * h3, axis=-1, keepdims=True) + EPS)
        yv = h3 * r * gf
        err = yv - tg
        dy = err * (1.0 / D)
        dh3, dgf8 = _rms_bwd(h3, gf, dy)
        da3 = dh3 * ppv * gate * (1.0 - gate)
        return dh3, da3, dh3 * gate, _sum8(err * err), dgf8

    (dh3, da3, dpp, lossp, dgfin), _ = _matmul(
        "ple_gate_loss", n3, wg, "nn", 512, D, D, epi_head,
        [("tile", _F32), ("tile", _MXU), ("tile", _MXU), ("part8", _F32), ("part8", _F32)],
        extras=[(pp, "tile"), (h2, "tile"), (tgt, "tile"), (g_fin, "row")])

    s_rows, s_cols = ("scatter", 0), ("scatter", 1)
    (dwp,), _ = _matmul("d_w_ple_proj", p, dpp, "tn", D_PLE, D, 2048, first, [("tile", _BF16)])
    (dwg,), _ = _matmul("d_w_ple_gate", n3, da3, "tn", D, D, 2048, first, [("tile", _BF16)])

    def epi_norm_bwd(acc, up, hv, g):
        dx, dg8 = _rms_bwd(hv, g, acc)
        dh = up + dx
        return dh, dh, dg8

    (dh2, dh2b, dgple), (r_wp, r_wg) = _matmul(
        "d_h2", da3, wg, "nt", 512, D, D, epi_norm_bwd, [("tile", _F32), ("tile", _MXU), ("part8", _F32)],
        extras=[(dh3, "tile"), (h2, "tile"), (g_ple, "row")],
        comm=[(s_cols, dwp), (s_rows, dwg)])
    (dw2,), _ = _matmul("d_w_ff2", hid2, dh2b, "tn", 1024, D, 2048, first, [("tile", _BF16)])
    (da1,), (r_w2,) = _matmul("d_ff_hidden", dh2b, w2, "nt", 512, 1024, D,
                              lambda acc, hv: (acc * 2.0 * hv.astype(_F32),), [("tile", _MXU)],
                              extras=[(hid, "tile")], comm=[(s_rows, dw2)])
    (dw1,), _ = _matmul("d_w_ff1", n2, da1, "tn", 1024, 1024, 2048, first, [("tile", _BF16)])
    (dh1, dh1b, dgmlp), (r_w1,) = _matmul(
        "d_h1", da1, w1, "nt", 256, D, D_FF, epi_norm_bwd, [("tile", _F32), ("tile", _MXU), ("part8", _F32)],
        extras=[(dh2, "tile"), (h1, "tile"), (g_mlp, "row")], comm=[(s_cols, dw1)])
    (dwout,), _ = _matmul("d_w_out", cat, dh1b, "tn", 1024, D, 2048, first, [("tile", _BF16)])
    (dcat,), (r_wout,) = _matmul("d_cat", dh1b, w_out, "nt", 512, 1024, D, first, [("tile", _F32)],
                                 comm=[(s_rows, dwout)])
    (dproj, dws, dbs, dgv, dgout, dng, dcw, dcb, ddtb, dalog, dd) = _mixer_bwd(
        proj, dcat, yss, states, *mix_consts, e_mat, et_mat, ltri_mat, seq_chunks)
    pieces = dict(gm_v_norm_g=dgv, gm_out_norm_g=dgout, ssd_norm_g=dng, norm_mlp_g=dgmlp, ple_norm_g=dgple,
                  final_norm_g=dgfin, gm_ws=dws, gm_bs=dbs, ssd_conv_w=dcw, ssd_conv_b=dcb, ssd_dt_bias=ddtb,
                  ssd_a_log=dalog, ssd_d=dd, loss=lossp)
    parts, segments, n_rows, where = _small_layout(pieces)
    small_block = _small_local(parts, segments, n_rows)
    (dwin,), (small_blocks,) = _matmul("d_w_in", n1, dproj, "tn", 512, D_IN_PAD, 1024, first, [("tile", _BF16)],
                                       comm=[(("gather", 0), small_block)])
    dwin_blocks = dwin[:, :D_IN].reshape(D, N_DEV, SHARD_IN).transpose(1, 0, 2)
    (gx, dgmix), (r_win,) = _matmul(
        "d_x", dproj, w_in, "nt", 256, D, D_IN_PAD, lambda *a: epi_norm_bwd(*a)[1:], [("tile", _F32), ("part8", _F32)],
        extras=[(dh1, "tile"), (x, "tile"), (g_mix, "row")], comm=[(s_rows, dwin_blocks)])
    r_win = r_win.reshape(N_DEV, D, SHARD_IN)

    big = dict(w_in=r_win, w_out=r_wout, w_ff1=r_w1, w_ff2=r_w2, w_ple_gate=r_wg, w_ple_proj=r_wp)
    return gx, big, small_blocks, dgmix, n_rows, where


def _small_layout(pieces):
    rows, segments = [], []
    in_row, out_row = 0, 0

    def add(arr, kind, n_out):
        nonlocal in_row, out_row
        rows.append(arr)
        segments.append((out_row, n_out, in_row, arr.shape[0], kind))
        start = out_row
        in_row += arr.shape[0]
        out_row += n_out
        return start

    where = {_LATE: 0}
    out_row = 1
    for name in _G_VECS[1:]:
        where[name] = add(pieces[name], "sum", 1)
    where["gm_ws"] = add(pieces["gm_ws"].reshape(GM_H * CH * CH // D, D), "copy", GM_H * CH * CH // D)
    where["gm_bs"] = add(pieces["gm_bs"].reshape(1, D), "copy", 1)
    cb = jnp.pad(pieces["ssd_conv_b"], ((0, 0), (0, 2 * D - CONV_CH)))
    where["ssd_conv_b"] = add(cb[:, :D], "sum", 1)
    add(cb[:, D:], "sum", 1)
    cw = jnp.pad(pieces["ssd_conv_w"][:CONV_K], ((0, 0), (0, 2 * D - CONV_CH)))
    where["ssd_conv_w"] = add(cw.reshape(2 * CONV_K, D), "copy", 2 * CONV_K)
    misc = jnp.concatenate([pieces["ssd_dt_bias"], pieces["ssd_a_log"], pieces["ssd_d"],
                            jnp.zeros((8, D - 3 * 128), _F32)], axis=1)
    where["misc"] = add(misc, "sum", 1)
    where["loss"] = add(pieces["loss"], "loss", 1)
    n_rows = -(-out_row // 8) * 8
    return jnp.concatenate(rows, axis=0), tuple(segments), n_rows, where


def _pack_small_params(vals, where, n_rows, my_block):
    rows, at = [], {}

    def add(name, arr):
        at[name] = sum(r.shape[0] for r in rows)
        rows.append(arr)

    for name in _G_VECS:
        add(name, vals[name].reshape(1, D))
    add("gm_ws", vals["gm_ws"].reshape(GM_H * CH * CH // D, D))
    add("gm_bs", vals["gm_bs"].reshape(1, D))
    cb = jnp.pad(vals["ssd_conv_b"].reshape(1, CONV_CH), ((0, 0), (0, 2 * D - CONV_CH)))
    add("ssd_conv_b", cb.reshape(2, D))
    cw = lax.dynamic_update_slice(jnp.zeros((CONV_K, 2 * D), _F32), vals["ssd_conv_w"].reshape(CONV_K, -1),
                                  (0, my_block * (CONV_CH // N_DEV)))
    add("ssd_conv_w", cw.reshape(2 * CONV_K, D))
    misc = jnp.concatenate([_pad_lanes(vals["ssd_dt_bias"].reshape(1, SSD_H)),
                            _pad_lanes(vals["ssd_a_log"].reshape(1, SSD_H)),
                            _pad_lanes(vals["ssd_d"].reshape(1, SSD_H)), jnp.zeros((1, D - 3 * 128), _F32)], axis=1)
    add("misc", misc)
    assert all(where[k] == r for k, r in at.items()), (where, at)
    rows.append(jnp.zeros((n_rows - sum(r.shape[0] for r in rows), D), _F32))
    return jnp.concatenate(rows, axis=0)


def _unpack_small(buf, where, my_block, shapes):
    out = {}
    for name in _G_VECS:
        out[name] = buf[where[name]].reshape(shapes[name])
    n_ws = GM_H * CH * CH // D
    out["gm_ws"] = buf[where["gm_ws"]:where["gm_ws"] + n_ws].reshape(shapes["gm_ws"])
    out["gm_bs"] = buf[where["gm_bs"]].reshape(shapes["gm_bs"])
    r = where["ssd_conv_b"]
    out["ssd_conv_b"] = buf[r:r + 2].reshape(1, 2 * D)[:, :CONV_CH].reshape(shapes["ssd_conv_b"])
    r = where["ssd_conv_w"]
    cw = buf[r:r + 2 * CONV_K].reshape(CONV_K, 2 * D)
    out["ssd_conv_w"] = lax.dynamic_slice(cw, (0, my_block * (CONV_CH // N_DEV)),
                                          (CONV_K, CONV_CH // N_DEV)).reshape(shapes["ssd_conv_w"])
    misc = buf[where["misc"]]
    for i, name in enumerate(("ssd_dt_bias", "ssd_a_log", "ssd_d")):
        out[name] = misc[i * 128:i * 128 + SSD_H].reshape(shapes[name])
    return out


_WEIGHTS = ("norm_mix_g", "w_in", "gm_v_norm_g", "gm_ws", "gm_bs", "gm_out_norm_g", "ssd_conv_w", "ssd_conv_b",
            "ssd_dt_bias", "ssd_a_log", "ssd_d", "ssd_norm_g", "w_out", "norm_mlp_g", "w_ff1", "w_ff2", "ple_norm_g",
            "w_ple_gate", "w_ple_proj", "final_norm_g")


def kernel(x, p, norm_mix_g, w_in, gm_v_norm_g, gm_ws, gm_bs, gm_out_norm_g, ssd_conv_w, ssd_conv_b, ssd_dt_bias, ssd_a_log, ssd_d, ssd_norm_g, w_out, norm_mlp_g, w_ff1, w_ff2, ple_norm_g, w_ple_gate, w_ple_proj, final_norm_g, loss_target, m_norm_mix_g, m_w_in, m_gm_v_norm_g, m_gm_ws, m_gm_bs, m_gm_out_norm_g, m_ssd_conv_w, m_ssd_conv_b, m_ssd_dt_bias, m_ssd_a_log, m_ssd_d, m_ssd_norm_g, m_w_out, m_norm_mlp_g, m_w_ff1, m_w_ff2, m_ple_norm_g, m_w_ple_gate, m_w_ple_proj, m_final_norm_g, v_norm_mix_g, v_w_in, v_gm_v_norm_g, v_gm_ws, v_gm_bs, v_gm_out_norm_g, v_ssd_conv_w, v_ssd_conv_b, v_ssd_dt_bias, v_ssd_a_log, v_ssd_d, v_ssd_norm_g, v_w_out, v_norm_mlp_g, v_w_ff1, v_w_ff2, v_ple_norm_g, v_w_ple_gate, v_w_ple_proj, v_final_norm_g):
    args = dict(locals())
    w = {n: args[n] for n in _WEIGHTS}
    m = {n: args["m_" + n] for n in _WEIGHTS}
    v = {n: args["v_" + n] for n in _WEIGHTS}
    shapes = {n: w[n].shape for n in _WEIGHTS}
    my_block = 4 * lax.axis_index("x") + 2 * lax.axis_index("y") + lax.axis_index("c")
    nb, seq_len, _ = x.shape

    shard = {n: w[n][0].astype(_MXU) for n in _BIG_NAMES}
    small = {n: w[n] for n in _WEIGHTS if n not in _BIG_NAMES}
    gx, recv, small_blocks, late8, n_rows, where = _local_step(
        x.reshape(nb * seq_len, D), p.reshape(nb * seq_len, D_PLE), loss_target.reshape(nb * seq_len, D), shard,
        ssd_conv_w[0], small, seq_len)

    big_out = [{}, {}, {}, {}]
    for n in _BIG_NAMES:
        res = _sum_adam("sum_adam_" + n, recv[n], w[n][0], m[n][0], v[n][0])
        for k in range(4):
            big_out[k][n] = res[k].reshape(shapes[n])

    packs = [_pack_small_params(d, where, n_rows, my_block) for d in (w, m, v)]
    small_res = _small_final(small_blocks, late8, where[_LATE], *packs)
    loss = small_res[0][where["loss"], 0]
    small_out = [_unpack_small(a, where, my_block, shapes) for a in small_res]

    outs = [loss, gx.reshape(x.shape)]
    for k in range(4):
        outs += [big_out[k][n] if n in _BIG_NAMES else small_out[k][n] for n in _WEIGHTS]
    return tuple(outs)
```

```python
import functools
import math

import jax
import jax.numpy as jnp
import numpy as np
from jax import lax
from jax.experimental import pallas as pl
from jax.experimental.pallas import tpu as pltpu

_F32 = jnp.float32
_BF16 = jnp.bfloat16
_MXU = jnp.bfloat16

D = 1024
D_PLE = 256
GM_W = 1024
GM_H = 8
CH = 128
SSD_W = 1024
SSD_H = 16
SSD_P = 64
SSD_G = 2
SSD_N = 128
CONV_K = 4
CONV_CH = SSD_W + 2 * SSD_G * SSD_N
D_FF = 4096
D_IN = 2 * GM_W + SSD_W + CONV_CH + SSD_H
D_IN_PAD = 4736
DT_BLK = (D_IN_PAD - 128) // 128
EPS = 1e-6
N_DEV = 8
SHARD_IN = D_IN // N_DEV

LR, B1, B2, ADAM_EPS, WD, STEP = 0.001, 0.9, 0.999, 1e-08, 0.01, 10
_LOG2E = math.log2(math.e)
_LOG2_INV_SQRT_2PI = -0.5 * math.log2(2.0 * math.pi)

_V7X_VMEM_BYTES = 64 * 1024 * 1024
_VMEM_CAP = _V7X_VMEM_BYTES - 8 * 1024 * 1024
_MESH = pl.DeviceIdType.MESH


def _vmem_limit(nbytes):
    return int(min(_VMEM_CAP, max(32 * 1024 * 1024, nbytes * 5 // 4 + (4 << 20))))


def _nbytes(shape, dtype):
    return int(np.prod(shape)) * jnp.dtype(dtype).itemsize


def _mx(v):
    return v.astype(_MXU)


def _dot(a, b):
    return jnp.dot(a, b, preferred_element_type=_F32)


def _dot_nt(a, b):
    return lax.dot_general(a, b, (((1,), (1,)), ((), ())), preferred_element_type=_F32)


def _dot_tn(a, b):
    return lax.dot_general(a, b, (((0,), (0,)), ((), ())), preferred_element_type=_F32)


def _split3(a):
    hi = a.astype(_BF16)
    r = a - hi.astype(_F32)
    mid = r.astype(_BF16)
    lo = (r - mid.astype(_F32)).astype(_BF16)
    return hi, mid, lo


def _xdot(dotfn, a, b01):
    b = b01.astype(_BF16)
    hi, mid, lo = _split3(a)
    return (dotfn(hi, b) + dotfn(mid, b)) + dotfn(lo, b)


def _xdot_left(dotfn, a01, b):
    a = a01.astype(_BF16)
    hi, mid, lo = _split3(b)
    return (dotfn(a, hi) + dotfn(a, mid)) + dotfn(a, lo)


def _sum8(v):
    r, n = v.shape
    return v.reshape(r // 8, 8, n).sum(axis=0)


def _sigmoid(v):
    return 1.0 / (1.0 + jnp.exp(-v))


def _rms(xv, g):
    ms = jnp.mean(xv * xv, axis=-1, keepdims=True)
    return xv * lax.rsqrt(ms + EPS) * g


def _rms_bwd(xv, g, dn):
    r = lax.rsqrt(jnp.mean(xv * xv, axis=-1, keepdims=True) + EPS)
    nh = xv * r
    gy = dn * g
    dx = (gy - nh * jnp.mean(gy * nh, axis=-1, keepdims=True)) * r
    return dx, _sum8(dn * nh)


def _iota2(shape, axis):
    return lax.broadcasted_iota(jnp.int32, shape, axis)


def _norm_cast(name, x, g, tm=512, comm=()):
    t, n = x.shape
    tm = min(tm, t)
    steps = t // tm
    kinds = [kind for kind, _ in comm]
    c_in, c_in_specs, c_out_specs, c_out_shape, c_scratch = _comm_io(comm)

    def body(*refs):
        x_ref, g_ref = refs[0], refs[1]
        o_ref = refs[2 + len(comm)]
        comm_refs = (kinds, refs[2:2 + len(comm)], refs[3 + len(comm):3 + 2 * len(comm)], *refs[3 + 2 * len(comm):])
        if comm:
            pl.when(pl.program_id(0) == 0)(lambda: _comm_start(*comm_refs))

        o_ref[...] = _rms(x_ref[...], g_ref[...]).astype(o_ref.dtype)
        if comm:
            pl.when(pl.program_id(0) == steps - 1)(lambda: _comm_finish(*comm_refs))

    res = pl.pallas_call(
        body, name=name, grid=(steps,),
        in_specs=[pl.BlockSpec((tm, n), lambda i: (i, 0)), pl.BlockSpec((1, n), lambda i: (0, 0))] + c_in_specs,
        out_specs=[pl.BlockSpec((tm, n), lambda i: (i, 0))] + c_out_specs,
        out_shape=[jax.ShapeDtypeStruct((t, n), _MXU)] + c_out_shape, scratch_shapes=c_scratch,
        compiler_params=pltpu.CompilerParams(dimension_semantics=("arbitrary",)),
    )(x, g, *c_in)
    return res[0], res[1:]


def _matmul(name, a, b, mode, tm, tn, tk, epilogue, outs, extras=(), comm=()):
    m, k = a.shape[::-1] if mode == "tn" else a.shape
    n = b.shape[0] if mode == "nt" else b.shape[1]
    tm, tn, tk = min(tm, m), min(tn, n), min(tk, k)
    assert m % tm == 0 and n % tn == 0 and k % tk == 0, (name, m, n, k, tm, tn, tk)
    if mode == "nn":
        a_spec = pl.BlockSpec((tm, tk), lambda i, j, kk: (i, kk))
        b_spec = pl.BlockSpec((tk, tn), lambda i, j, kk: (kk, j))
        dotfn = _dot
    elif mode == "nt":
        a_spec = pl.BlockSpec((tm, tk), lambda i, j, kk: (i, kk))
        b_spec = pl.BlockSpec((tn, tk), lambda i, j, kk: (j, kk))
        dotfn = _dot_nt
    else:
        a_spec = pl.BlockSpec((tk, tm), lambda i, j, kk: (kk, i))
        b_spec = pl.BlockSpec((tk, tn), lambda i, j, kk: (kk, j))
        dotfn = _dot_tn
    ni, nj, nk = m // tm, n // tn, k // tk
    n_ex, n_out, n_comm = len(extras), len(outs), len(comm)
    kinds = [kind for kind, _ in comm]
    c_in, c_in_specs, c_out_specs, c_out_shape, c_scratch = _comm_io(comm)

    in_specs, vmem = [a_spec, b_spec], 2 * (tm * tk * a.dtype.itemsize + tk * tn * b.dtype.itemsize)
    for arr, kind in extras:
        if kind == "tile":
            in_specs.append(pl.BlockSpec((tm, tn), lambda i, j, kk: (i, j)))
            vmem += 2 * _nbytes((tm, tn), arr.dtype)
        else:
            in_specs.append(pl.BlockSpec((1, tn), lambda i, j, kk: (0, j)))
    out_specs, out_shape = [], []
    for kind, dt in outs:
        if kind == "tile":
            out_specs.append(pl.BlockSpec((tm, tn), lambda i, j, kk: (i, j)))
            out_shape.append(jax.ShapeDtypeStruct((m, n), dt))
            vmem += 2 * _nbytes((tm, tn), dt)
        else:
            assert nj == 1, "the partial-sum rows are accumulated over consecutive row tiles"
            out_specs.append(pl.BlockSpec((8, tn), lambda i, j, kk: (0, 0)))
            out_shape.append(jax.ShapeDtypeStruct((8, n), dt))
    scratch = [pltpu.VMEM((tm, tn), _F32)] if nk > 1 else []
    vmem += _nbytes((tm, tn), _F32) * (2 if nk > 1 else 1)

    def body(*refs):
        a_ref, b_ref = refs[0], refs[1]
        ex_refs = refs[2:2 + n_ex]
        n_in = 2 + n_ex + n_comm
        out_refs = refs[n_in:n_in + n_out]
        i, j, kk = pl.program_id(0), pl.program_id(1), pl.program_id(2)
        comm_refs = (kinds, refs[2 + n_ex:n_in], refs[n_in + n_out:n_in + n_out + n_comm], *refs[len(refs) - 3:])
        if n_comm:
            pl.when((i == 0) & (j == 0) & (kk == 0))(lambda: _comm_start(*comm_refs))

        part = dotfn(_mx(a_ref[...]), _mx(b_ref[...]))

        def finish(acc):
            vals = epilogue(acc, *[r[...] for r in ex_refs])
            for r, v, (kind, _) in zip(out_refs, vals, outs):
                if kind == "part8":
                    @pl.when(i == 0)
                    def _():
                        r[...] = v

                    @pl.when(i > 0)
                    def _():
                        r[...] += v
                else:
                    r[...] = v.astype(r.dtype)

        if nk == 1:
            finish(part)
        else:
            acc_ref = refs[n_in + n_out + n_comm]

            @pl.when(kk == 0)
            def _():
                acc_ref[...] = part

            @pl.when(kk > 0)
            def _():
                acc_ref[...] += part

            @pl.when(kk == nk - 1)
            def _():
                finish(acc_ref[...])

        if n_comm:
            pl.when((i == ni - 1) & (j == nj - 1) & (kk == nk - 1))(lambda: _comm_finish(*comm_refs))

    carried =n_comm or any(kind == "part8" for kind, _ in outs)
    sem = ("arbitrary",) * 3 if carried else ("parallel", "parallel", "arbitrary")
    res = pl.pallas_call(
        body, name=name, grid=(ni, nj, nk),
        in_specs=in_specs + c_in_specs, out_specs=out_specs + c_out_specs, out_shape=out_shape + c_out_shape,
        scratch_shapes=scratch + c_scratch,
        compiler_params=pltpu.CompilerParams(dimension_semantics=sem, vmem_limit_bytes=_vmem_limit(vmem)),
    )(a, b, *[arr for arr, _ in extras], *c_in)
    return res[:n_out], res[n_out:]


def _shift_down(v, halo8, j):
    if j == 0:
        return v
    r = pltpu.roll(v, j, axis=0)
    hr = pltpu.roll(halo8, j, axis=0)
    top = jnp.where(_iota2(hr.shape, 0) < j, hr, r[:8])
    return jnp.concatenate([top, r[8:]], axis=0)


def _shift_up(v, next8, j):
    if j == 0:
        return v
    rows = v.shape[0]
    r = pltpu.roll(v, rows - j, axis=0)
    nr = pltpu.roll(next8, 8 - j, axis=0)
    bot = jnp.where(_iota2(nr.shape, 0) >= 8 - j, nr, r[rows - 8:])
    return jnp.concatenate([r[:rows - 8], bot], axis=0)


def _silu_grad(sig, silu):
    return sig + silu * (1.0 - sig)


def _gmlp_fwd_vals(pu, pv, gv, ws_ref, bsb_ref, want_bwd):
    tril = _iota2((CH, CH), 0) >= _iota2((CH, CH), 1)
    cdf_u = 0.5 * (1.0 + lax.erf(pu * 0.7071067811865476))
    cdf_v = 0.5 * (1.0 + lax.erf(pv * 0.7071067811865476))
    u = pu * cdf_u
    v = pv * cdf_v
    ys, keep = [], [(cdf_u, cdf_v)] if want_bwd else []
    for h in range(GM_H):
        sl = slice(h * 128, (h + 1) * 128)
        vh = v[:, sl]
        r = lax.rsqrt(jnp.mean(vh * vh, axis=-1, keepdims=True) + EPS)
        vn = vh * r * gv[:, sl]
        wm = _mx(jnp.where(tril, ws_ref[h], 0.0))
        mixed = _dot(wm, _mx(vn)) + bsb_ref[h]
        ys.append(u[:, sl] * mixed)
        if want_bwd:
            keep.append((vh, r, vn, wm, mixed))
    return jnp.concatenate(ys, axis=1), u, keep


def _ssd_common(xbc, halo8, dtraw, cw_ref, cb, dtb, alog, e_ref, ltri):
    xs = [_shift_down(xbc, halo8, j) for j in range(CONV_K)]
    cpre = cb + sum(cw_ref[k:k + 1, :] * xs[CONV_K - 1 - k] for k in range(CONV_K))
    sig = _sigmoid(cpre)
    act = cpre * sig
    dtin = dtraw + dtb
    dt = jnp.maximum(dtin, 0.0) + jnp.log(1.0 + jnp.exp(-jnp.abs(dtin)))
    a_neg = -jnp.exp(alog)
    cs = _xdot_left(_dot, ltri, dt * a_neg)
    cs_last = cs[CH - 1:CH, :]
    ecs = jnp.exp(cs)
    dec = jnp.exp(cs_last - cs)
    cdec = jnp.exp(cs_last)
    e = e_ref[...]
    dt_x = _dot(dt.astype(_BF16), e)
    ecs_x = _dot(ecs.astype(_BF16), e)
    dec_x = _dot(dec.astype(_BF16), e)
    cdec_x = _xdot(_dot, jnp.broadcast_to(cdec, (8, 128)), e)[0:1, :]
    return dict(xs=xs, sig=sig, act=act, dtin=dtin, dt=dt, a_neg=a_neg, cs=cs, ecs=ecs, dec=dec, cdec=cdec,
                dt_x=dt_x, ecs_x=ecs_x, dec_x=dec_x, cdec_x=cdec_x)


def _head_lm(cs, cst_ref, h, tril):
    seg = jnp.broadcast_to(cs[:, h:h + 1], (CH, CH)) - cst_ref[h:h + 1, :]
    return jnp.exp(jnp.where(tril, seg, -jnp.inf))


def _mixer_fwd(proj, gv, ws, bsb, gout, cw8, cb, dtb, alog, d_x, ng, e_mat, ltri_mat, seq_chunks, comm=()):
    t = proj.shape[0]
    n_chunks = t // CH
    n_comm = len(comm)
    kinds = [kind for kind, _ in comm]
    c_in, c_in_specs, c_out_specs, c_out_shape, c_scratch = _comm_io(comm)

    def body(*refs):
        (pu_ref, pv_ref, z_ref, xbc_ref, dt_ref, halo_ref, gv_ref, ws_ref, bsb_ref, gout_ref, cw_ref, cb_ref,
         dtb_ref, alog_ref, dx_ref, ng_ref, e_ref, ltri_ref) = refs[:18]
        cat_ref, y_ref, st_ref = refs[18 + n_comm:21 + n_comm]
        s_ref, cst_ref = refs[21 + 2 * n_comm:23 + 2 * n_comm]
        comm_refs = (kinds, refs[18:18 + n_comm], refs[21 + n_comm:21 + 2 * n_comm], *refs[23 + 2 * n_comm:])
        c = pl.program_id(0)
        if n_comm:
            pl.when(c == 0)(lambda: _comm_start(*comm_refs))
            pl.when(c == n_chunks - 1)(lambda: _comm_finish(*comm_refs))

        first = (c % seq_chunks) == 0
        tril = _iota2((CH, CH), 0) >= _iota2((CH, CH), 1)
        lane = _iota2((CH, 128), 1)

        y_a, _, _ = _gmlp_fwd_vals(pu_ref[...], pv_ref[...], gv_ref[...], ws_ref, bsb_ref, False)
        cat_ref[:, 0:GM_W] = _rms(y_a, gout_ref[...]).astype(cat_ref.dtype)

        @pl.when(first)
        def _():
            s_ref[...] = jnp.zeros_like(s_ref)

        halo8 = jnp.where(first, 0.0, halo_ref[...])
        q = _ssd_common(xbc_ref[...], halo8, dt_ref[...], cw_ref, cb_ref[...], dtb_ref[...], alog_ref[...], e_ref,
                        ltri_ref[...])
        act = q["act"]
        xv = act[:, 0:SSD_W]
        xdt = xv * q["dt_x"]
        xdt_m = _mx(xdt)
        cs = q["cs"]
        cst_ref[...] = cs.T
        s_prev = s_ref[...]
        st_ref[...] = s_prev
        ys = []
        for g in range(SSD_G):
            bg = _mx(act[:, SSD_W + g * SSD_N:SSD_W + (g + 1) * SSD_N])
            cg = _mx(act[:, SSD_W + SSD_G * SSD_N + g * SSD_N:SSD_W + SSD_G * SSD_N + (g + 1) * SSD_N])
            cbm = _dot_nt(cg, bg)
            gs = slice(g * 512, (g + 1) * 512)
            for pr in range(4):
                ps = slice(g * 512 + pr * 128, g * 512 + (pr + 1) * 128)
                o = []
                for hh in range(2):
                    h = g * 8 + pr * 2 + hh
                    m_h = _mx(cbm * _head_lm(cs, cst_ref, h, tril))
                    o.append(_dot(m_h, xdt_m[:, ps]))
                ys.append(jnp.where(lane < SSD_P, o[0], o[1]))
            sg = s_prev[:, gs]
            yoff = _dot(cg, _mx(sg)) * q["ecs_x"][:, gs]
            ys[-4:] = [ys[-4 + i] + yoff[:, i * 128:(i + 1) * 128] for i in range(4)]
            st_new = _dot_tn(bg, _mx(q["dec_x"][:, gs] * xdt[:, gs]))
            s_ref[:, gs] = sg * q["cdec_x"][:, gs] + st_new
        y = jnp.concatenate(ys, axis=1) + dx_ref[...] * xv
        y_ref[...] = y
        zv = z_ref[...]
        yg = y * (zv * _sigmoid(zv))
        for g in range(SSD_G):
            gs = slice(g * 512, (g + 1) * 512)
            cat_ref[:, GM_W + g * 512:GM_W + (g + 1) * 512] = _rms(yg[:, gs], ng_ref[:, gs]).astype(cat_ref.dtype)

    blk = lambda w, j: pl.BlockSpec((CH, w), lambda c: (c, j))
    full = lambda arr: pl.BlockSpec(arr.shape, lambda c: (0,) * arr.ndim)
    consts = [gv, ws, bsb, gout, cw8, cb, dtb, alog, d_x, ng, e_mat, ltri_mat]
    res = pl.pallas_call(
        body, name="mixer_fwd", grid=(n_chunks,),
        in_specs=[blk(GM_W, 0), blk(GM_W, 1), blk(SSD_W, 2), blk(CONV_CH, 2), blk(128, DT_BLK),
                  pl.BlockSpec((8, CONV_CH), lambda c: (jnp.maximum(c * (CH // 8) - 1, 0), 2))]
        + [full(a) for a in consts] + c_in_specs,
        out_specs=[pl.BlockSpec((CH, 2 * D), lambda c: (c, 0)), pl.BlockSpec((CH, SSD_W), lambda c: (c, 0)),
                   pl.BlockSpec((CH, SSD_W), lambda c: (c, 0))] + c_out_specs,
        out_shape=[jax.ShapeDtypeStruct((t, 2 * D), _MXU), jax.ShapeDtypeStruct((t, SSD_W), _F32),
                   jax.ShapeDtypeStruct((t, SSD_W), _F32)] + c_out_shape,
        scratch_shapes=[pltpu.VMEM((SSD_N, SSD_W), _F32), pltpu.VMEM((128, CH), _F32)] + c_scratch,
        compiler_params=pltpu.CompilerParams(dimension_semantics=("arbitrary",), vmem_limit_bytes=48 << 20),
    )(proj, proj, proj, proj, proj, proj, *consts, *c_in)
    return res[:3], res[3:]


def _mixer_bwd(proj, dcat, yss, states, gv, ws, bsb, gout, cw8, cb, dtb, alog, d_x, ng, e_mat, et_mat, ltri_mat,
               seq_chunks):
    t = proj.shape[0]
    n_chunks = t // CH

    def body(pu_ref, pv_ref, z_ref, xbc_ref, dt_ref, halo_ref, dcat_ref, y_ref, st_ref,
             gv_ref, ws_ref, bsb_ref, gout_ref, cw_ref, cb_ref, dtb_ref, alog_ref, dx_ref, ng_ref,
             e_ref, et_ref, ltri_ref,
             dproj_ref, dws_ref, dbs_ref, dgv_ref, dgout_ref, dng_ref, dcw_ref, dcb_ref, ddtb_ref, dalog_ref, dd_ref,
             ds_ref, dnext_ref, dcst_ref, cst_ref, dbacc_ref, ddacc_ref):
        i = pl.program_id(0)
        c = n_chunks - 1 - i
        first = (c % seq_chunks) == 0
        last_in_seq = (c % seq_chunks) == seq_chunks - 1
        tril = _iota2((CH, CH), 0) >= _iota2((CH, CH), 1)
        lane = _iota2((CH, 128), 1)
        row = _iota2((CH, 128), 0)

        @pl.when(i == 0)
        def _():
            for r in (dws_ref, dbs_ref, dgv_ref, dgout_ref, dng_ref, dcw_ref, dcb_ref, ddtb_ref, dalog_ref, dd_ref,
                      dbacc_ref, ddacc_ref, dcst_ref):
                r[...] = jnp.zeros_like(r)

        @pl.when(last_in_seq)
        def _():
            ds_ref[...] = jnp.zeros_like(ds_ref)
            dnext_ref[...] = jnp.zeros_like(dnext_ref)

        dcat_v = dcat_ref[...].astype(_F32)

        pu, pv = pu_ref[...], pv_ref[...]
        gv_v = gv_ref[...]
        y_a, u, keep = _gmlp_fwd_vals(pu, pv, gv_v, ws_ref, bsb_ref, True)
        dy, dgout8 = _rms_bwd(y_a, gout_ref[...], dcat_v[:, 0:GM_W])
        dgout_ref[...] += dgout8
        dus, dvs, dgvs = [], [], []
        for h in range(GM_H):
            sl = slice(h * 128, (h + 1) * 128)
            vh, r, vn, wm, mixed = keep[h + 1]
            dyh = dy[:, sl]
            dus.append(dyh * mixed)
            dmix = dyh * u[:, sl]
            dmix_m = _mx(dmix)
            dws_ref[h] += jnp.where(tril, _dot_nt(dmix_m, _mx(vn)), 0.0)
            dbacc_ref[h] += dmix
            dvn = _dot_tn(wm, dmix_m)
            gy = dvn * gv_v[:, sl]
            nh = vh * r
            dvs.append((gy - nh * jnp.mean(gy * nh, axis=-1, keepdims=True)) * r)
            dgvs.append(_sum8(dvn * nh))
        dgv_ref[...] += jnp.concatenate(dgvs, axis=1)
        cdf_u, cdf_v = keep[0]
        gelu_grad = lambda pre, cdf: cdf + pre * jnp.exp2(pre * pre * (-0.5 * _LOG2E) + _LOG2_INV_SQRT_2PI)
        dproj_ref[:, 0:GM_W] = (jnp.concatenate(dus, axis=1) * gelu_grad(pu, cdf_u)).astype(dproj_ref.dtype)
        dproj_ref[:, GM_W:2 * GM_W] = (jnp.concatenate(dvs, axis=1) * gelu_grad(pv, cdf_v)).astype(dproj_ref.dtype)

        halo8 = jnp.where(first, 0.0, halo_ref[...])
        q = _ssd_common(xbc_ref[...], halo8, dt_ref[...], cw_ref, cb_ref[...], dtb_ref[...], alog_ref[...], e_ref,
                        ltri_ref[...])
        act = q["act"]
        xv = act[:, 0:SSD_W]
        dt_x, ecs_x, dec_x, cdec_x = q["dt_x"], q["ecs_x"], q["dec_x"], q["cdec_x"]
        xdt = xv * dt_x
        xdt_m = _mx(xdt)
        cs = q["cs"]
        cst_ref[...] = cs.T
        s_prev = st_ref[...]
        ds = ds_ref[...]
        yv = y_ref[...]
        zv = z_ref[...]
        sig_z = _sigmoid(zv)
        sz = zv * sig_z
        yg = yv * sz
        dygs, dng8 = [], []
        for g in range(SSD_G):
            gs = slice(g * 512, (g + 1) * 512)
            a_, b_ = _rms_bwd(yg[:, gs], ng_ref[:, gs], dcat_v[:, GM_W + g * 512:GM_W + (g + 1) * 512])
            dygs.append(a_)
            dng8.append(b_)
        dyg = jnp.concatenate(dygs, axis=1)
        dng_ref[...] += jnp.concatenate(dng8, axis=1)
        dyv = dyg * sz
        dproj_ref[:, 2 * GM_W:2 * GM_W + SSD_W] = (dyg * yv * _silu_grad(sig_z, sz)).astype(dproj_ref.dtype)
        ddacc_ref[...] += _sum8(dyv * xv)
        dyv_m = _mx(dyv)

        dxdt_parts, db_parts, dc_parts = [], [], []
        dcs = jnp.zeros((CH, 128), _F32)
        dcs_x_parts, ddec_x_parts, dcl_x_parts = [], [], []
        for g in range(SSD_G):
            gs = slice(g * 512, (g + 1) * 512)
            bg = _mx(act[:, SSD_W + g * SSD_N:SSD_W + (g + 1) * SSD_N])
            cg = _mx(act[:, SSD_W + SSD_G * SSD_N + g * SSD_N:SSD_W + SSD_G * SSD_N + (g + 1) * SSD_N])
            cbm = _dot_nt(cg, bg)
            sg = s_prev[:, gs]
            sg_m = _mx(sg)
            dsg = ds[:, gs]
            dsg_m = _mx(dsg)
            zoff = _dot(cg, sg_m)
            dz_off = dyv[:, gs] * ecs_x[:, gs]
            dz_off_m = _mx(dz_off)
            dcs_x_parts.append(dyv[:, gs] * zoff * ecs_x[:, gs])
            dcg = _dot_nt(dz_off_m, sg_m)
            dsprev = _dot_tn(cg, dz_off_m)
            w_st = dec_x[:, gs] * xdt[:, gs]
            dw_st = _dot(bg, dsg_m)
            dbg = _dot_nt(_mx(w_st), dsg_m)
            dxdt_g = dec_x[:, gs] * dw_st
            ddec_x_parts.append(dw_st * xdt[:, gs])
            dsprev = dsprev + cdec_x[:, gs] * dsg
            dcl_x_parts.append(jnp.sum(dsg * sg, axis=0, keepdims=True) * cdec_x[:, gs])
            ds_ref[:, gs] = dsprev
            dcb = jnp.zeros((CH, CH), _F32)
            dxdt_pairs = []
            for pr in range(4):
                ps = slice(g * 512 + pr * 128, g * 512 + (pr + 1) * 128)
                acc_pair = None
                for hh in range(2):
                    h = g * 8 + pr * 2 + hh
                    in_head = (lane < SSD_P) if hh == 0 else (lane >= SSD_P)
                    lm = _head_lm(cs, cst_ref, h, tril)
                    m_h = cbm * lm
                    m_hm = _mx(m_h)
                    dyh_m = _mx(jnp.where(in_head, dyv[:, ps], 0.0))
                    dm = _dot_nt(dyh_m, xdt_m[:, ps])
                    dcb = dcb + dm * lm
                    qm = dm * m_h
                    dcs = dcs + jnp.where(lane == h, jnp.sum(qm, axis=1, keepdims=True), 0.0)
                    dcst_ref[h:h + 1, :] = jnp.sum(qm, axis=0, keepdims=True)
                    contrib = jnp.where(in_head, _dot_tn(m_hm, dyv_m[:, ps]), 0.0)
                    acc_pair = contrib if acc_pair is None else acc_pair + contrib
                dxdt_pairs.append(acc_pair)
            dxdt_parts.append(dxdt_g + jnp.concatenate(dxdt_pairs, axis=1))
            dcb_m = _mx(dcb)
            dc_parts.append(dcg + _dot(dcb_m, bg))
            db_parts.append(dbg + _dot_tn(dcb_m, cg))
        dxdt = jnp.concatenate(dxdt_parts, axis=1)
        dxv = dx_ref[...] * dyv + dxdt * dt_x
        et = et_ref[...]
        head_sum = lambda v: _dot(v.astype(_BF16), et)
        ddt = head_sum(dxdt * xv)
        dcs = dcs - dcst_ref[...].T + head_sum(jnp.concatenate(dcs_x_parts, axis=1))
        ddec = head_sum(jnp.concatenate(ddec_x_parts, axis=1)) * q["dec"]
        dcs = dcs - ddec
        dcl = jnp.sum(ddec, axis=0, keepdims=True) + _xdot(
            _dot, jnp.broadcast_to(jnp.concatenate(dcl_x_parts, axis=1), (8, SSD_W)), et)[0:1, :]
        dcs = jnp.where(row == CH - 1, dcs + dcl, dcs)
        da = _xdot_left(_dot_tn, ltri_ref[...], dcs)
        ddt = ddt + da * q["a_neg"]
        dalog_ref[...] += _sum8(da * q["dt"] * q["a_neg"])
        ddtraw = jnp.where(lane < SSD_H, ddt * _sigmoid(q["dtin"]), 0.0)
        ddtb_ref[...] += _sum8(ddtraw)
        dproj_ref[:, D_IN_PAD - 128:D_IN_PAD] = ddtraw.astype(dproj_ref.dtype)
        dcpre = jnp.concatenate([dxv] + db_parts + dc_parts, axis=1) * _silu_grad(q["sig"], act)
        dcb_ref[...] += _sum8(dcpre)
        for k in range(CONV_K):
            dcw_ref[k:k + 1, :] += jnp.sum(dcpre * q["xs"][CONV_K - 1 - k], axis=0, keepdims=True)
        next8 = dnext_ref[...]
        dxbc = sum(cw_ref[k:k + 1, :] * _shift_up(dcpre, next8, CONV_K - 1 - k) for k in range(CONV_K))
        dproj_ref[:, 2 * GM_W + SSD_W:2 * GM_W + SSD_W + CONV_CH] = dxbc.astype(dproj_ref.dtype)
        dnext_ref[...] = dcpre[0:8, :]

        @pl.when(i == n_chunks - 1)
        def _():
            for h in range(GM_H):
                dbs_ref[h:h + 1, :] = _xdot_left(_dot_nt, jnp.ones((8, 128), _BF16), dbacc_ref[h])[0:1, :]
            dd_ref[...] = _xdot(_dot, ddacc_ref[...], et)

    rblk = lambda w, j: pl.BlockSpec((CH, w), lambda i: (n_chunks - 1 - i, j))
    full = lambda arr: pl.BlockSpec(arr.shape, lambda i: (0,) * arr.ndim)
    acc = lambda shape: pl.BlockSpec(shape, lambda i: (0,) * len(shape))
    consts = [gv, ws, bsb, gout, cw8, cb, dtb, alog, d_x, ng, e_mat, et_mat, ltri_mat]
    acc_shapes = [(GM_H, CH, CH), (8, 128), (8, GM_W), (8, GM_W), (8, SSD_W), (8, CONV_CH), (8, CONV_CH), (8, 128),
                  (8, 128), (8, 128)]
    return pl.pallas_call(
        body, name="mixer_bwd", grid=(n_chunks,),
        in_specs=[rblk(GM_W, 0), rblk(GM_W, 1), rblk(SSD_W, 2), rblk(CONV_CH, 2), rblk(128, DT_BLK),
                  pl.BlockSpec((8, CONV_CH), lambda i: (jnp.maximum((n_chunks - 1 - i) * (CH // 8) - 1, 0), 2)),
                  rblk(2 * D, 0), rblk(SSD_W, 0), rblk(SSD_W, 0)] + [full(a) for a in consts],
        out_specs=[rblk(D_IN_PAD, 0)] + [acc(s) for s in acc_shapes],
        out_shape=[jax.ShapeDtypeStruct((t, D_IN_PAD), _MXU)] + [jax.ShapeDtypeStruct(s, _F32) for s in acc_shapes],
        scratch_shapes=[pltpu.VMEM((SSD_N, SSD_W), _F32), pltpu.VMEM((8, CONV_CH), _F32),
                        pltpu.VMEM((128, CH), _F32), pltpu.VMEM((128, CH), _F32),
                        pltpu.VMEM((GM_H, CH, 128), _F32), pltpu.VMEM((8, SSD_W), _F32)],
        compiler_params=pltpu.CompilerParams(dimension_semantics=("arbitrary",), vmem_limit_bytes=48 << 20),
    )(proj, proj, proj, proj, proj, proj, dcat, yss, states, *consts)


def _peers():
    x, y, c = lax.axis_index("x"), lax.axis_index("y"), lax.axis_index("c")
    out = []
    for k in range(1, N_DEV):
        fx, fy, fc = (k >> 2) & 1, (k >> 1) & 1, k & 1
        px, py, pc = (x + fx) % 2, (y + fy) % 2, (c + fc) % 2
        out.append((k - 1, (px, py, pc), 4 * px + 2 * py + pc))
    return out, 4 * x + 2 * y + c


def _comm_io(comm):
    any_spec = pl.BlockSpec(memory_space=pl.ANY)
    n = len(comm)
    out_shape = []
    for (kind, axis), src in comm:
        shp = list(src.shape)
        if kind in ("gather", "gather2"):
            shp[axis] *= N_DEV
        else:
            shp[axis] //= N_DEV
            shp = [N_DEV] + shp
        out_shape.append(jax.ShapeDtypeStruct(tuple(shp), src.dtype))
    scratch = [pltpu.SemaphoreType.DMA((n * (N_DEV - 1),)), pltpu.SemaphoreType.DMA((n * (N_DEV - 1),)),
               pltpu.SemaphoreType.DMA((n,))] if n else []
    return [src for _, src in comm], [any_spec] * n, [any_spec] * n, out_shape, scratch


def _window(ref, axis, idx, size):
    start = pl.multiple_of(idx * size, size)
    return ref.at[tuple(pl.ds(start, size) if a == axis else slice(None) for a in range(len(ref.shape)))]


def _comm_plans(kinds, src_refs, dst_refs, send_sems, recv_sems, local_sems):
    x, y, c = lax.axis_index("x"), lax.axis_index("y"), lax.axis_index("c")
    peers, me = _peers()
    plans = []
    for s, ((kind, axis), src, dst) in enumerate(zip(kinds, src_refs, dst_refs)):
        sems = lambda k: dict(send_sem=send_sems.at[s * (N_DEV - 1) + k], recv_sem=recv_sems.at[s * (N_DEV - 1) + k])
        remote = lambda src_ref, dst_ref, k, pid: pltpu.make_async_remote_copy(
            src_ref=src_ref, dst_ref=dst_ref, device_id=pid, device_id_type=_MESH, **sems(k))
        if kind == "gather2":
            size = src.shape[axis]
            win = lambda idx: _window(dst, axis, idx, size)
            sib, sib_idx = (x, y, 1 - c), 4 * x + 2 * y + (1 - c)
            local = pltpu.make_async_copy(src, win(me), local_sems.at[s])
            to_sib = remote(src, win(me), 0, sib)
            starts, forwards = [local, to_sib], []
            waits = [(local, "local"), (to_sib, "send"), (remote(src, win(sib_idx), 0, sib), "recv")]
            for j, (fx, fy) in enumerate(((1, 0), (0, 1), (1, 1))):
                px, py = (x + fx) % 2, (y + fy) % 2
                same, other = 4 * px + 2 * py + c, 4 * px + 2 * py + (1 - c)
                out = remote(src, win(me), 1 + j, (px, py, c))
                starts.append(out)
                passed = remote(win(same), win(same), 4 + j, sib)
                forwards.append((remote(src, win(same), 1 + j, (px, py, c)), passed))
                waits += [(out, "send"), (passed, "send"), (remote(win(other), win(other), 4 + j, sib), "recv")]
            plans.append((starts, forwards, waits))
            continue
        if kind == "gather":
            size = src.shape[axis]
            src_for = lambda pidx: src
            dst_mine = _window(dst, axis, me, size)
        else:
            size = src.shape[axis] // N_DEV
            src_for = lambda pidx: _window(src, axis, pidx, size)
            dst_mine = dst.at[me]
        local = pltpu.make_async_copy(src_for(me), dst_mine, local_sems.at[s])
        remotes = [remote(src_for(pidx), dst_mine, k, pid) for k, pid, pidx in peers]
        plans.append(([local] + remotes, [], [(local, "local")] + [(cp, "both") for cp in remotes]))
    return plans


def _comm_start(*refs):
    for starts, _, _ in _comm_plans(*refs):
        for cp in starts:
            cp.start()


def _comm_finish(*refs):
    for _, forwards, waits in _comm_plans(*refs):
        for arrival, cp in forwards:
            arrival.wait_recv()
            cp.start()
        for cp, what in waits:
            if what == "send":
                cp.wait_send()
            elif what == "recv":
                cp.wait_recv()
            else:
                cp.wait()


def _adam_vals(w, g, m, v):
    m = B1 * m + (1.0 - B1) * g
    v = B2 * v + (1.0 - B2) * (g * g)
    m_hat = m / (1.0 - B1 ** STEP)
    v_hat = v / (1.0 - B2 ** STEP)
    delta = -LR * (m_hat / (jnp.sqrt(v_hat) + ADAM_EPS) + WD * w)
    return delta, m, v


def _sum_adam(name, recv, w, m, v, tr=256):
    _, r, wd = recv.shape
    tr = min(tr, r)

    def body(recv_ref, w_ref, m_ref, v_ref, g_out, d_out, m_out, v_out):
        g = recv_ref[0].astype(_F32)
        for s in range(1, N_DEV):
            g = g + recv_ref[s].astype(_F32)
        d_, m_, v_ = _adam_vals(w_ref[...], g, m_ref[...], v_ref[...])
        g_out[...] = g
        d_out[...] = d_
        m_out[...] = m_
        v_out[...] = v_

    spec = pl.BlockSpec((tr, wd), lambda i: (i, 0))
    return pl.pallas_call(
        body, name=name, grid=(r // tr,),
        in_specs=[pl.BlockSpec((N_DEV, tr, wd), lambda i: (0, i, 0)), spec, spec, spec],
        out_specs=[spec] * 4, out_shape=[jax.ShapeDtypeStruct((r, wd), _F32)] * 4,
        compiler_params=pltpu.CompilerParams(dimension_semantics=("parallel",), vmem_limit_bytes=48 << 20),
    )(recv, w, m, v)


def _small_local(parts, segments, n_rows):
    def body(parts_ref, loc_ref):
        loc_ref[...] = jnp.zeros_like(loc_ref)
        for out_row, n_out, in_row, n_in, kind in segments:
            if kind == "copy":
                loc_ref[out_row:out_row + n_out, :] = parts_ref[in_row:in_row + n_in, :]
            else:
                s = jnp.sum(parts_ref[in_row:in_row + n_in, :], axis=0, keepdims=True)
                if kind == "loss":
                    s = jnp.broadcast_to(jnp.sum(s, axis=1, keepdims=True) * (0.5 / D), (1, D))
                loc_ref[out_row:out_row + 1, :] = s

    vm = pl.BlockSpec(memory_space=pltpu.VMEM)
    return pl.pallas_call(body, name="small_local", in_specs=[vm], out_specs=vm,
                          out_shape=jax.ShapeDtypeStruct((n_rows, D), _F32))(parts)


def _small_final(blocks, late8, late_row, w, m, v):
    n_rows = w.shape[0]

    def body(blocks_ref, late_ref, w_ref, m_ref, v_ref, g_out, d_out, m_out, v_out, loc_ref, recv_ref, send_sems,
             recv_sems):
        peers, me = _peers()
        loc_ref[...] = jnp.broadcast_to(jnp.sum(late_ref[...], axis=0, keepdims=True), (8, D))
        recv_ref[me] = loc_ref[...]
        copies = [pltpu.make_async_remote_copy(src_ref=loc_ref, dst_ref=recv_ref.at[me], send_sem=send_sems.at[k],
                                               recv_sem=recv_sems.at[k], device_id=pid, device_id_type=_MESH)
                  for k, pid, _ in peers]
        for cp in copies:
            cp.start()
        g = blocks_ref[0:n_rows, :]
        for s in range(1, N_DEV):
            g = g + blocks_ref[s * n_rows:(s + 1) * n_rows, :]
        for cp in copies:
            cp.wait()
        late = recv_ref[0]
        for s in range(1, N_DEV):
            late = late + recv_ref[s]
        g = jnp.where(_iota2((n_rows, D), 0) == late_row, jnp.broadcast_to(late[0:1, :], (n_rows, D)), g)
        d_, m_, v_ = _adam_vals(w_ref[...], g, m_ref[...], v_ref[...])
        g_out[...] = g
        d_out[...] = d_
        m_out[...] = m_
        v_out[...] = v_

    vm = pl.BlockSpec(memory_space=pltpu.VMEM)
    return pl.pallas_call(
        body, name="small_final", in_specs=[vm] * 5, out_specs=[vm] * 4,
        out_shape=[jax.ShapeDtypeStruct((n_rows, D), _F32)] * 4,
        scratch_shapes=[pltpu.VMEM((8, D), _F32), pltpu.VMEM((N_DEV, 8, D), _F32),
                        pltpu.SemaphoreType.DMA((N_DEV - 1,)), pltpu.SemaphoreType.DMA((N_DEV - 1,))],
        compiler_params=pltpu.CompilerParams(vmem_limit_bytes=48 << 20),
    )(blocks, late8, w, m, v)


_BIG_NAMES =("w_in", "w_out", "w_ff1", "w_ff2", "w_ple_gate", "w_ple_proj")

_G_VECS = ("norm_mix_g", "gm_v_norm_g", "gm_out_norm_g", "ssd_norm_g", "norm_mlp_g", "ple_norm_g", "final_norm_g")
_LATE = _G_VECS[0]


def _const_mats():
    h = np.arange(128)[:, None]
    ch = np.arange(SSD_W)[None, :]
    e = (ch // SSD_P == h).astype(np.float32)
    ltri = (np.arange(CH)[:, None] >= np.arange(CH)[None, :]).astype(np.float32)
    return jnp.asarray(e, _BF16), jnp.asarray(e.T, _BF16), jnp.asarray(ltri, _BF16)


def _pad_lanes(v, n=128):
    return jnp.pad(v, ((0, 0), (0, n - v.shape[1])))


def _local_step(x, p, tgt, shard, conv_w_shard, small, seq_len):
    seq_chunks = seq_len // CH
    e_mat, et_mat, ltri_mat = _const_mats()
    g_mix, g_mlp, g_ple = small["norm_mix_g"], small["norm_mlp_g"], small["ple_norm_g"]
    g_fin = small["final_norm_g"].reshape(1, D)
    gv, gout, ng = small["gm_v_norm_g"], small["gm_out_norm_g"], small["ssd_norm_g"]
    ws = small["gm_ws"][0]
    bsb = jnp.broadcast_to(small["gm_bs"][0][:, :, None], (GM_H, CH, 128))
    cb = small["ssd_conv_b"]
    dtb, alog = _pad_lanes(small["ssd_dt_bias"]), _pad_lanes(small["ssd_a_log"])
    d_x = jnp.repeat(small["ssd_d"], SSD_P, axis=1)

    first = lambda acc: (acc,)
    rows, cols = ("gather2", 0), ("gather2", 1)
    n1, (g_win, g_cw) = _norm_cast("norm_mix", x, g_mix,
                                   comm=[(rows, shard["w_in"][None]), (("gather", 0), conv_w_shard[None])])
    w_in = jnp.pad(g_win.transpose(1, 0, 2).reshape(D, D_IN), ((0, 0), (0, D_IN_PAD - D_IN)))
    cw8 = jnp.pad(g_cw.transpose(1, 0, 2).reshape(CONV_K, CONV_CH), ((0, 8 - CONV_K), (0, 0)))
    mix_consts = (gv, ws, bsb, gout, cw8, cb, dtb, alog, d_x, ng)
    (proj,), (w_out, w1) = _matmul("proj_in", n1, w_in, "nn", 256, D_IN_PAD, D, first, [("tile", _F32)],
                                   comm=[(rows, shard["w_out"]), (cols, shard["w_ff1"])])
    (cat, yss, states), (w2, wg, wp) = _mixer_fwd(
        proj, *mix_consts, e_mat, ltri_mat, seq_chunks,
        comm=[(rows, shard["w_ff2"]), (rows, shard["w_ple_gate"]), (cols, shard["w_ple_proj"])])

    def epi_res_norm(acc, res, g):
        hv = acc + res
        return hv, _rms(hv, g)

    (h1, n2), _ = _matmul("proj_out", cat, w_out, "nn", 1024, D, 2 * D, epi_res_norm,
                          [("tile", _F32), ("tile", _MXU)], extras=[(x, "tile"), (g_mlp, "row")])

    def epi_relu2(acc):
        hid = jnp.maximum(acc, 0.0)
        return hid, hid * hid

    (hid, hid2), _ = _matmul("ff1", n2, w1, "nn", 1024, 1024, D, epi_relu2, [("tile", _MXU), ("tile", _MXU)])
    (h2, n3), _ = _matmul("ff2", hid2, w2, "nn", 512, D, D_FF, epi_res_norm, [("tile", _F32), ("tile", _MXU)],
                          extras=[(h1, "tile"), (g_ple, "row")])
    (pp,), _ = _matmul("ple_proj", p, wp, "nn", 512, D, D_PLE, first, [("tile", _F32)])

    def epi_head(acc, ppv, h2v, tg, gf):
        gate = _sigmoid(acc)
        gp = gate * ppv
        h3 = h2v + gp
        r = lax.rsqrt(jnp.mean(h3 * h3, axis=-1, keepdims=True) + EPS)
        nh = h3 * r
        err = nh * gf - tg
        gy = err * (gf * (1.0 / D))
        dh3 = (gy - nh * jnp.mean(gy * nh, axis=-1, keepdims=True)) * r
        dpp = dh3 * gate
        da3 = dpp * (ppv - gp)
        return dh3, da3, dpp, _sum8(err * err), _sum8(err * nh) * (1.0 / D)

    (dh3, da3, dpp, lossp, dgfin), _ = _matmul(
        "ple_gate_loss", n3, wg, "nn", 512, D, D, epi_head,
        [("tile", _F32), ("tile", _MXU), ("tile", _MXU), ("part8", _F32), ("part8", _F32)],
        extras=[(pp, "tile"), (h2, "tile"), (tgt, "tile"), (g_fin, "row")])

    s_rows, s_cols = ("scatter", 0), ("scatter", 1)
    (dwp,), _ = _matmul("d_w_ple_proj", p, dpp, "tn", D_PLE, D, 2048, first, [("tile", _BF16)])
    (dwg,), _ = _matmul("d_w_ple_gate", n3, da3, "tn", D, D, 2048, first, [("tile", _BF16)])

    def epi_norm_bwd(acc, up, hv, g):
        dx, dg8 = _rms_bwd(hv, g, acc)
        dh = up + dx
        return dh, dh, dg8

    (dh2, dh2b, dgple), (r_wp, r_wg) = _matmul(
        "d_h2", da3, wg, "nt", 1024, D, D, epi_norm_bwd, [("tile", _F32), ("tile", _MXU), ("part8", _F32)],
        extras=[(dh3, "tile"), (h2, "tile"), (g_ple, "row")],
        comm=[(s_cols, dwp), (s_rows, dwg)])
    (dw2,), _ = _matmul("d_w_ff2", hid2, dh2b, "tn", 1024, D, 2048, first, [("tile", _BF16)])
    (da1,), (r_w2,) = _matmul("d_ff_hidden", dh2b, w2, "nt", 512, 2048, D,
                              lambda acc, hv: (acc * 2.0 * hv.astype(_F32),), [("tile", _MXU)],
                              extras=[(hid, "tile")], comm=[(s_rows, dw2)])
    (dw1,), _ = _matmul("d_w_ff1", n2, da1, "tn", 1024, 1024, 2048, first, [("tile", _BF16)])
    (dh1, dh1b, dgmlp), (r_w1,) = _matmul(
        "d_h1", da1, w1, "nt", 256, D, D_FF, epi_norm_bwd, [("tile", _F32), ("tile", _MXU), ("part8", _F32)],
        extras=[(dh2, "tile"), (h1, "tile"), (g_mlp, "row")], comm=[(s_cols, dw1)])
    (dwout,), _ = _matmul("d_w_out", cat, dh1b, "tn", 1024, D, 2048, first, [("tile", _BF16)])
    (dcat,), (r_wout,) = _matmul("d_cat", dh1b, w_out, "nt", 1024, 1024, D, first, [("tile", _F32)],
                                 comm=[(s_rows, dwout)])
    (dproj, dws, dbs, dgv, dgout, dng, dcw, dcb, ddtb, dalog, dd) = _mixer_bwd(
        proj, dcat, yss, states, *mix_consts, e_mat, et_mat, ltri_mat, seq_chunks)
    pieces = dict(gm_v_norm_g=dgv, gm_out_norm_g=dgout, ssd_norm_g=dng, norm_mlp_g=dgmlp, ple_norm_g=dgple,
                  final_norm_g=dgfin, gm_ws=dws, gm_bs=dbs, ssd_conv_w=dcw, ssd_conv_b=dcb, ssd_dt_bias=ddtb,
                  ssd_a_log=dalog, ssd_d=dd, loss=lossp)
    parts, segments, n_rows, where = _small_layout(pieces)
    small_block = _small_local(parts, segments, n_rows)
    (dwin,), (small_blocks,) = _matmul("d_w_in", n1, dproj, "tn", 512, D_IN_PAD, 1024, first, [("tile", _BF16)],
                                       comm=[(("gather", 0), small_block)])
    dwin_blocks = dwin[:, :D_IN].reshape(D, N_DEV, SHARD_IN).transpose(1, 0, 2)
    (gx, dgmix), (r_win,) = _matmul(
        "d_x", dproj, w_in, "nt", 256, D, D_IN_PAD, lambda *a: epi_norm_bwd(*a)[1:], [("tile", _F32), ("part8", _F32)],
        extras=[(dh1, "tile"), (x, "tile"), (g_mix, "row")], comm=[(s_rows, dwin_blocks)])
    r_win = r_win.reshape(N_DEV, D, SHARD_IN)

    big = dict(w_in=r_win, w_out=r_wout, w_ff1=r_w1, w_ff2=r_w2, w_ple_gate=r_wg, w_ple_proj=r_wp)
    return gx, big, small_blocks, dgmix, n_rows, where


def _small_layout(pieces):
    rows, segments = [], []
    in_row, out_row = 0, 0

    def add(arr, kind, n_out):
        nonlocal in_row, out_row
        rows.append(arr)
        segments.append((out_row, n_out, in_row, arr.shape[0], kind))
        start = out_row
        in_row += arr.shape[0]
        out_row += n_out
        return start

    where = {_LATE: 0}
    out_row = 1
    for name in _G_VECS[1:]:
        where[name] = add(pieces[name], "sum", 1)
    where["gm_ws"] = add(pieces["gm_ws"].reshape(GM_H * CH * CH // D, D), "copy", GM_H * CH * CH // D)
    where["gm_bs"] = add(pieces["gm_bs"].reshape(1, D), "copy", 1)
    cb = jnp.pad(pieces["ssd_conv_b"], ((0, 0), (0, 2 * D - CONV_CH)))
    where["ssd_conv_b"] = add(cb[:, :D], "sum", 1)
    add(cb[:, D:], "sum", 1)
    cw = jnp.pad(pieces["ssd_conv_w"][:CONV_K], ((0, 0), (0, 2 * D - CONV_CH)))
    where["ssd_conv_w"] = add(cw.reshape(2 * CONV_K, D), "copy", 2 * CONV_K)
    misc = jnp.concatenate([pieces["ssd_dt_bias"], pieces["ssd_a_log"], pieces["ssd_d"],
                            jnp.zeros((8, D - 3 * 128), _F32)], axis=1)
    where["misc"] = add(misc, "sum", 1)
    where["loss"] = add(pieces["loss"], "loss", 1)
    n_rows = -(-out_row // 8) * 8
    return jnp.concatenate(rows, axis=0), tuple(segments), n_rows, where


def _pack_small_params(vals, where, n_rows, my_block):
    rows, at = [], {}

    def add(name, arr):
        at[name] = sum(r.shape[0] for r in rows)
        rows.append(arr)

    for name in _G_VECS:
        add(name, vals[name].reshape(1, D))
    add("gm_ws", vals["gm_ws"].reshape(GM_H * CH * CH // D, D))
    add("gm_bs", vals["gm_bs"].reshape(1, D))
    cb = jnp.pad(vals["ssd_conv_b"].reshape(1, CONV_CH), ((0, 0), (0, 2 * D - CONV_CH)))
    add("ssd_conv_b", cb.reshape(2, D))
    cw = lax.dynamic_update_slice(jnp.zeros((CONV_K, 2 * D), _F32), vals["ssd_conv_w"].reshape(CONV_K, -1),
                                  (0, my_block * (CONV_CH // N_DEV)))
    add("ssd_conv_w", cw.reshape(2 * CONV_K, D))
    misc = jnp.concatenate([_pad_lanes(vals["ssd_dt_bias"].reshape(1, SSD_H)),
                            _pad_lanes(vals["ssd_a_log"].reshape(1, SSD_H)),
                            _pad_lanes(vals["ssd_d"].reshape(1, SSD_H)), jnp.zeros((1, D - 3 * 128), _F32)], axis=1)
    add("misc", misc)
    assert all(where[k] == r for k, r in at.items()), (where, at)
    rows.append(jnp.zeros((n_rows - sum(r.shape[0] for r in rows), D), _F32))
    return jnp.concatenate(rows, axis=0)


def _unpack_small(buf, where, my_block, shapes):
    out = {}
    for name in _G_VECS:
        out[name] = buf[where[name]].reshape(shapes[name])
    n_ws = GM_H * CH * CH // D
    out["gm_ws"] = buf[where["gm_ws"]:where["gm_ws"] + n_ws].reshape(shapes["gm_ws"])
    out["gm_bs"] = buf[where["gm_bs"]].reshape(shapes["gm_bs"])
    r = where["ssd_conv_b"]
    out["ssd_conv_b"] = buf[r:r + 2].reshape(1, 2 * D)[:, :CONV_CH].reshape(shapes["ssd_conv_b"])
    r = where["ssd_conv_w"]
    cw = buf[r:r + 2 * CONV_K].reshape(CONV_K, 2 * D)
    out["ssd_conv_w"] = lax.dynamic_slice(cw, (0, my_block * (CONV_CH // N_DEV)),
                                          (CONV_K, CONV_CH // N_DEV)).reshape(shapes["ssd_conv_w"])
    misc = buf[where["misc"]]
    for i, name in enumerate(("ssd_dt_bias", "ssd_a_log", "ssd_d")):
        out[name] = misc[i * 128:i * 128 + SSD_H].reshape(shapes[name])
    return out


_WEIGHTS = ("norm_mix_g", "w_in", "gm_v_norm_g", "gm_ws", "gm_bs", "gm_out_norm_g", "ssd_conv_w", "ssd_conv_b",
            "ssd_dt_bias", "ssd_a_log", "ssd_d", "ssd_norm_g", "w_out", "norm_mlp_g", "w_ff1", "w_ff2", "ple_norm_g",
            "w_ple_gate", "w_ple_proj", "final_norm_g")


def kernel(x, p, norm_mix_g, w_in, gm_v_norm_g, gm_ws, gm_bs, gm_out_norm_g, ssd_conv_w, ssd_conv_b, ssd_dt_bias, ssd_a_log, ssd_d, ssd_norm_g, w_out, norm_mlp_g, w_ff1, w_ff2, ple_norm_g, w_ple_gate, w_ple_proj, final_norm_g, loss_target, m_norm_mix_g, m_w_in, m_gm_v_norm_g, m_gm_ws, m_gm_bs, m_gm_out_norm_g, m_ssd_conv_w, m_ssd_conv_b, m_ssd_dt_bias, m_ssd_a_log, m_ssd_d, m_ssd_norm_g, m_w_out, m_norm_mlp_g, m_w_ff1, m_w_ff2, m_ple_norm_g, m_w_ple_gate, m_w_ple_proj, m_final_norm_g, v_norm_mix_g, v_w_in, v_gm_v_norm_g, v_gm_ws, v_gm_bs, v_gm_out_norm_g, v_ssd_conv_w, v_ssd_conv_b, v_ssd_dt_bias, v_ssd_a_log, v_ssd_d, v_ssd_norm_g, v_w_out, v_norm_mlp_g, v_w_ff1, v_w_ff2, v_ple_norm_g, v_w_ple_gate, v_w_ple_proj, v_final_norm_g):
    args = dict(locals())
    w = {n: args[n] for n in _WEIGHTS}
    m = {n: args["m_" + n] for n in _WEIGHTS}
    v = {n: args["v_" + n] for n in _WEIGHTS}
    shapes = {n: w[n].shape for n in _WEIGHTS}
    my_block = 4 * lax.axis_index("x") + 2 * lax.axis_index("y") + lax.axis_index("c")
    nb, seq_len, _ = x.shape

    shard = {n: w[n][0].astype(_MXU) for n in _BIG_NAMES}
    small = {n: w[n] for n in _WEIGHTS if n not in _BIG_NAMES}
    gx, recv, small_blocks, late8, n_rows, where = _local_step(
        x.reshape(nb * seq_len, D), p.reshape(nb * seq_len, D_PLE), loss_target.reshape(nb * seq_len, D), shard,
        ssd_conv_w[0], small, seq_len)

    big_out = [{}, {}, {}, {}]
    for n in _BIG_NAMES:
        res = _sum_adam("sum_adam_" + n, recv[n], w[n][0], m[n][0], v[n][0])
        for k in range(4):
            big_out[k][n] = res[k].reshape(shapes[n])

    packs = [_pack_small_params(d, where, n_rows, my_block) for d in (w, m, v)]
    small_res = _small_final(small_blocks, late8, where[_LATE], *packs)
    loss = small_res[0][where["loss"], 0]
    small_out = [_unpack_small(a, where, my_block, shapes) for a in small_res]

    outs = [loss, gx.reshape(x.shape)]
    for k in range(4):
        outs += [big_out[k][n] if n in _BIG_NAMES else small_out[k][n] for n in _WEIGHTS]
    return tuple(outs)
```

```python
import functools
import math

import jax
import jax.numpy as jnp
import numpy as np
from jax import lax
from jax.experimental import pallas as pl
from jax.experimental.pallas import tpu as pltpu

_F32 = jnp.float32
_BF16 = jnp.bfloat16
_MXU = jnp.bfloat16

D = 1024
D_PLE = 256
GM_W = 1024
GM_H = 8
CH = 128
SSD_W = 1024
SSD_H = 16
SSD_P = 64
SSD_G = 2
SSD_N = 128
CONV_K = 4
CONV_CH = SSD_W + 2 * SSD_G * SSD_N
D_FF = 4096
D_IN = 2 * GM_W + SSD_W + CONV_CH + SSD_H
D_IN_PAD = 4736
DT_BLK = (D_IN_PAD - 128) // 128
EPS = 1e-6
N_DEV = 8
SHARD_IN = D_IN // N_DEV

LR, B1, B2, ADAM_EPS, WD, STEP = 0.001, 0.9, 0.999, 1e-08, 0.01, 10
_LOG2E = math.log2(math.e)
_LOG2_INV_SQRT_2PI = -0.5 * math.log2(2.0 * math.pi)

_V7X_VMEM_BYTES = 64 * 1024 * 1024
_VMEM_CAP = _V7X_VMEM_BYTES - 8 * 1024 * 1024
_MESH = pl.DeviceIdType.MESH


def _vmem_limit(nbytes):
    return int(min(_VMEM_CAP, max(32 * 1024 * 1024, nbytes * 5 // 4 + (4 << 20))))


def _nbytes(shape, dtype):
    return int(np.prod(shape)) * jnp.dtype(dtype).itemsize


def _mx(v):
    return v.astype(_MXU)


def _dot(a, b):
    return jnp.dot(a, b, preferred_element_type=_F32)


def _dot_nt(a, b):
    return lax.dot_general(a, b, (((1,), (1,)), ((), ())), preferred_element_type=_F32)


def _dot_tn(a, b):
    return lax.dot_general(a, b, (((0,), (0,)), ((), ())), preferred_element_type=_F32)


def _split3(a):
    hi = a.astype(_BF16)
    r = a - hi.astype(_F32)
    mid = r.astype(_BF16)
    lo = (r - mid.astype(_F32)).astype(_BF16)
    return hi, mid, lo


def _xdot(dotfn, a, b01):
    b = b01.astype(_BF16)
    hi, mid, lo = _split3(a)
    return (dotfn(hi, b) + dotfn(mid, b)) + dotfn(lo, b)


def _xdot_left(dotfn, a01, b):
    a = a01.astype(_BF16)
    hi, mid, lo = _split3(b)
    return (dotfn(a, hi) + dotfn(a, mid)) + dotfn(a, lo)


def _sum8(v):
    r, n = v.shape
    return v.reshape(r // 8, 8, n).sum(axis=0)


def _sigmoid(v):
    return 1.0 / (1.0 + jnp.exp(-v))


def _rms(xv, g):
    ms = jnp.mean(xv * xv, axis=-1, keepdims=True)
    return xv * lax.rsqrt(ms + EPS) * g


def _rms_bwd(xv, g, dn):
    r = lax.rsqrt(jnp.mean(xv * xv, axis=-1, keepdims=True) + EPS)
    nh = xv * r
    gy = dn * g
    dx = (gy - nh * jnp.mean(gy * nh, axis=-1, keepdims=True)) * r
    return dx, _sum8(dn * nh)


def _iota2(shape, axis):
    return lax.broadcasted_iota(jnp.int32, shape, axis)


def _norm_cast(name, x, g, tm=512, comm=()):
    t, n = x.shape
    tm = min(tm, t)
    steps = t // tm
    kinds = [kind for kind, _ in comm]
    c_in, c_in_specs, c_out_specs, c_out_shape, c_scratch = _comm_io(comm)

    def body(*refs):
        x_ref, g_ref = refs[0], refs[1]
        o_ref = refs[2 + len(comm)]
        comm_refs = (kinds, refs[2:2 + len(comm)], refs[3 + len(comm):3 + 2 * len(comm)], *refs[3 + 2 * len(comm):])
        if comm:
            pl.when(pl.program_id(0) == 0)(lambda: _comm_start(*comm_refs))

        o_ref[...] = _rms(x_ref[...], g_ref[...]).astype(o_ref.dtype)
        if comm:
            pl.when(pl.program_id(0) == steps - 1)(lambda: _comm_finish(*comm_refs))

    res = pl.pallas_call(
        body, name=name, grid=(steps,),
        in_specs=[pl.BlockSpec((tm, n), lambda i: (i, 0)), pl.BlockSpec((1, n), lambda i: (0, 0))] + c_in_specs,
        out_specs=[pl.BlockSpec((tm, n), lambda i: (i, 0))] + c_out_specs,
        out_shape=[jax.ShapeDtypeStruct((t, n), _MXU)] + c_out_shape, scratch_shapes=c_scratch,
        compiler_params=pltpu.CompilerParams(dimension_semantics=("arbitrary",)),
    )(x, g, *c_in)
    return res[0], res[1:]


def _matmul(name, a, b, mode, tm, tn, tk, epilogue, outs, extras=(), comm=()):
    m, k = a.shape[::-1] if mode == "tn" else a.shape
    n = b.shape[0] if mode == "nt" else b.shape[1]
    tm, tn, tk = min(tm, m), min(tn, n), min(tk, k)
    assert m % tm == 0 and n % tn == 0 and k % tk == 0, (name, m, n, k, tm, tn, tk)
    if mode == "nn":
        a_spec = pl.BlockSpec((tm, tk), lambda i, j, kk: (i, kk))
        b_spec = pl.BlockSpec((tk, tn), lambda i, j, kk: (kk, j))
        dotfn = _dot
    elif mode == "nt":
        a_spec = pl.BlockSpec((tm, tk), lambda i, j, kk: (i, kk))
        b_spec = pl.BlockSpec((tn, tk), lambda i, j, kk: (j, kk))
        dotfn = _dot_nt
    else:
        a_spec = pl.BlockSpec((tk, tm), lambda i, j, kk: (kk, i))
        b_spec = pl.BlockSpec((tk, tn), lambda i, j, kk: (kk, j))
        dotfn = _dot_tn
    ni, nj, nk = m // tm, n // tn, k // tk
    n_ex, n_out, n_comm = len(extras), len(outs), len(comm)
    kinds = [kind for kind, _ in comm]
    c_in, c_in_specs, c_out_specs, c_out_shape, c_scratch = _comm_io(comm)

    in_specs, vmem = [a_spec, b_spec], 2 * (tm * tk * a.dtype.itemsize + tk * tn * b.dtype.itemsize)
    for arr, kind in extras:
        if kind == "tile":
            in_specs.append(pl.BlockSpec((tm, tn), lambda i, j, kk: (i, j)))
            vmem += 2 * _nbytes((tm, tn), arr.dtype)
        else:
            in_specs.append(pl.BlockSpec((1, tn), lambda i, j, kk: (0, j)))
    out_specs, out_shape = [], []
    for kind, dt in outs:
        if kind == "tile":
            out_specs.append(pl.BlockSpec((tm, tn), lambda i, j, kk: (i, j)))
            out_shape.append(jax.ShapeDtypeStruct((m, n), dt))
            vmem += 2 * _nbytes((tm, tn), dt)
        else:
            assert nj == 1, "the partial-sum rows are accumulated over consecutive row tiles"
            out_specs.append(pl.BlockSpec((8, tn), lambda i, j, kk: (0, 0)))
            out_shape.append(jax.ShapeDtypeStruct((8, n), dt))
    scratch = [pltpu.VMEM((tm, tn), _F32)] if nk > 1 else []
    vmem += _nbytes((tm, tn), _F32) * 2

    def body(*refs):
        a_ref, b_ref = refs[0], refs[1]
        ex_refs = refs[2:2 + n_ex]
        n_in = 2 + n_ex + n_comm
        out_refs = refs[n_in:n_in + n_out]
        i, j, kk = pl.program_id(0), pl.program_id(1), pl.program_id(2)
        comm_refs = (kinds, refs[2 + n_ex:n_in], refs[n_in + n_out:n_in + n_out + n_comm], *refs[len(refs) - 3:])
        if n_comm:
            pl.when((i == 0) & (j == 0) & (kk == 0))(lambda: _comm_start(*comm_refs))

        part = dotfn(_mx(a_ref[...]), _mx(b_ref[...]))

        def finish(acc):
            vals = epilogue(acc, *[r[...] for r in ex_refs])
            for r, v, (kind, _) in zip(out_refs, vals, outs):
                if kind == "part8":
                    @pl.when(i == 0)
                    def _():
                        r[...] = v

                    @pl.when(i > 0)
                    def _():
                        r[...] += v
                else:
                    r[...] = v.astype(r.dtype)

        if nk == 1:
            finish(part)
        else:
            acc_ref = refs[n_in + n_out + n_comm]

            @pl.when(kk == 0)
            def _():
                acc_ref[...] = part

            @pl.when(kk > 0)
            def _():
                acc_ref[...] += part

            @pl.when(kk == nk - 1)
            def _():
                finish(acc_ref[...])

        if n_comm:
            pl.when((i == ni - 1) & (j == nj - 1) & (kk == nk - 1))(lambda: _comm_finish(*comm_refs))

    carried =n_comm or any(kind == "part8" for kind, _ in outs)
    sem = ("arbitrary",) * 3 if carried else ("parallel", "parallel", "arbitrary")
    res = pl.pallas_call(
        body, name=name, grid=(ni, nj, nk),
        in_specs=in_specs + c_in_specs, out_specs=out_specs + c_out_specs, out_shape=out_shape + c_out_shape,
        scratch_shapes=scratch + c_scratch,
        compiler_params=pltpu.CompilerParams(dimension_semantics=sem, vmem_limit_bytes=_vmem_limit(vmem)),
    )(a, b, *[arr for arr, _ in extras], *c_in)
    return res[:n_out], res[n_out:]


def _shift_down(v, halo8, j):
    if j == 0:
        return v
    r = pltpu.roll(v, j, axis=0)
    hr = pltpu.roll(halo8, j, axis=0)
    top = jnp.where(_iota2(hr.shape, 0) < j, hr, r[:8])
    return jnp.concatenate([top, r[8:]], axis=0)


def _shift_up(v, next8, j):
    if j == 0:
        return v
    rows = v.shape[0]
    r = pltpu.roll(v, rows - j, axis=0)
    nr = pltpu.roll(next8, 8 - j, axis=0)
    bot = jnp.where(_iota2(nr.shape, 0) >= 8 - j, nr, r[rows - 8:])
    return jnp.concatenate([r[:rows - 8], bot], axis=0)


def _silu_grad(sig, silu):
    return sig + silu * (1.0 - sig)


def _gmlp_fwd_vals(pu, pv, gv, ws_ref, bsb_ref, want_bwd):
    tril = _iota2((CH, CH), 0) >= _iota2((CH, CH), 1)
    cdf_u = 0.5 * (1.0 + lax.erf(pu * 0.7071067811865476))
    cdf_v = 0.5 * (1.0 + lax.erf(pv * 0.7071067811865476))
    u = pu * cdf_u
    v = pv * cdf_v
    ys, keep = [], [(cdf_u, cdf_v)] if want_bwd else []
    for h in range(GM_H):
        sl = slice(h * 128, (h + 1) * 128)
        vh = v[:, sl]
        r = lax.rsqrt(jnp.mean(vh * vh, axis=-1, keepdims=True) + EPS)
        vn = vh * r * gv[:, sl]
        wm = _mx(jnp.where(tril, ws_ref[h], 0.0))
        mixed = _dot(wm, _mx(vn)) + bsb_ref[h]
        ys.append(u[:, sl] * mixed)
        if want_bwd:
            keep.append((vh, r, vn, wm, mixed))
    return jnp.concatenate(ys, axis=1), u, keep


def _ssd_common(xbc, halo8, dtraw, cw_ref, cb, dtb, alog, e_ref, ltri):
    xs = [_shift_down(xbc, halo8, j) for j in range(CONV_K)]
    cpre = cb + sum(cw_ref[k:k + 1, :] * xs[CONV_K - 1 - k] for k in range(CONV_K))
    sig = _sigmoid(cpre)
    act = cpre * sig
    dtin = dtraw + dtb
    dt = jnp.maximum(dtin, 0.0) + jnp.log(1.0 + jnp.exp(-jnp.abs(dtin)))
    a_neg = -jnp.exp(alog)
    cs = _xdot_left(_dot, ltri, dt * a_neg)
    cs_last = cs[CH - 1:CH, :]
    ecs = jnp.exp(cs)
    dec = jnp.exp(cs_last - cs)
    cdec = jnp.exp(cs_last)
    e = e_ref[...]
    dt_x = _dot(dt.astype(_BF16), e)
    ecs_x = _dot(ecs.astype(_BF16), e)
    dec_x = _dot(dec.astype(_BF16), e)
    cdec_x = _xdot(_dot, jnp.broadcast_to(cdec, (8, 128)), e)[0:1, :]
    return dict(xs=xs, sig=sig, act=act, dtin=dtin, dt=dt, a_neg=a_neg, cs=cs, ecs=ecs, dec=dec, cdec=cdec,
                dt_x=dt_x, ecs_x=ecs_x, dec_x=dec_x, cdec_x=cdec_x)


def _head_lm(cs, cst_ref, h, tril):
    seg = jnp.broadcast_to(cs[:, h:h + 1], (CH, CH)) - cst_ref[h:h + 1, :]
    return jnp.exp(jnp.where(tril, seg, -jnp.inf))


def _mixer_fwd(proj, gv, ws, bsb, gout, cw8, cb, dtb, alog, d_x, ng, e_mat, ltri_mat, seq_chunks, comm=()):
    t = proj.shape[0]
    n_chunks = t // CH
    n_comm = len(comm)
    kinds = [kind for kind, _ in comm]
    c_in, c_in_specs, c_out_specs, c_out_shape, c_scratch = _comm_io(comm)

    def body(*refs):
        (pu_ref, pv_ref, z_ref, xbc_ref, dt_ref, halo_ref, gv_ref, ws_ref, bsb_ref, gout_ref, cw_ref, cb_ref,
         dtb_ref, alog_ref, dx_ref, ng_ref, e_ref, ltri_ref) = refs[:18]
        cat_ref, y_ref, st_ref = refs[18 + n_comm:21 + n_comm]
        s_ref, cst_ref = refs[21 + 2 * n_comm:23 + 2 * n_comm]
        comm_refs = (kinds, refs[18:18 + n_comm], refs[21 + n_comm:21 + 2 * n_comm], *refs[23 + 2 * n_comm:])
        c = pl.program_id(0)
        if n_comm:
            pl.when(c == 0)(lambda: _comm_start(*comm_refs))
            pl.when(c == n_chunks - 1)(lambda: _comm_finish(*comm_refs))

        first = (c % seq_chunks) == 0
        tril = _iota2((CH, CH), 0) >= _iota2((CH, CH), 1)
        lane = _iota2((CH, 128), 1)

        y_a, _, _ = _gmlp_fwd_vals(pu_ref[...], pv_ref[...], gv_ref[...], ws_ref, bsb_ref, False)
        cat_ref[:, 0:GM_W] = _rms(y_a, gout_ref[...]).astype(cat_ref.dtype)

        @pl.when(first)
        def _():
            s_ref[...] = jnp.zeros_like(s_ref)

        halo8 = jnp.where(first, 0.0, halo_ref[...])
        q = _ssd_common(xbc_ref[...], halo8, dt_ref[...], cw_ref, cb_ref[...], dtb_ref[...], alog_ref[...], e_ref,
                        ltri_ref[...])
        act = q["act"]
        xv = act[:, 0:SSD_W]
        xdt = xv * q["dt_x"]
        xdt_m = _mx(xdt)
        cs = q["cs"]
        cst_ref[...] = cs.T
        s_prev = s_ref[...]
        st_ref[...] = s_prev
        ys = []
        for g in range(SSD_G):
            bg = _mx(act[:, SSD_W + g * SSD_N:SSD_W + (g + 1) * SSD_N])
            cg = _mx(act[:, SSD_W + SSD_G * SSD_N + g * SSD_N:SSD_W + SSD_G * SSD_N + (g + 1) * SSD_N])
            cbm = _dot_nt(cg, bg)
            gs = slice(g * 512, (g + 1) * 512)
            for pr in range(4):
                ps = slice(g * 512 + pr * 128, g * 512 + (pr + 1) * 128)
                o = []
                for hh in range(2):
                    h = g * 8 + pr * 2 + hh
                    m_h = _mx(cbm * _head_lm(cs, cst_ref, h, tril))
                    o.append(_dot(m_h, xdt_m[:, ps]))
                ys.append(jnp.where(lane < SSD_P, o[0], o[1]))
            sg = s_prev[:, gs]
            yoff = _dot(cg, _mx(sg)) * q["ecs_x"][:, gs]
            ys[-4:] = [ys[-4 + i] + yoff[:, i * 128:(i + 1) * 128] for i in range(4)]
            st_new = _dot_tn(bg, _mx(q["dec_x"][:, gs] * xdt[:, gs]))
            s_ref[:, gs] = sg * q["cdec_x"][:, gs] + st_new
        y = jnp.concatenate(ys, axis=1) + dx_ref[...] * xv
        y_ref[...] = y
        zv = z_ref[...]
        yg = y * (zv * _sigmoid(zv))
        for g in range(SSD_G):
            gs = slice(g * 512, (g + 1) * 512)
            cat_ref[:, GM_W + g * 512:GM_W + (g + 1) * 512] = _rms(yg[:, gs], ng_ref[:, gs]).astype(cat_ref.dtype)

    blk = lambda w, j: pl.BlockSpec((CH, w), lambda c: (c, j))
    full = lambda arr: pl.BlockSpec(arr.shape, lambda c: (0,) * arr.ndim)
    consts = [gv, ws, bsb, gout, cw8, cb, dtb, alog, d_x, ng, e_mat, ltri_mat]
    res = pl.pallas_call(
        body, name="mixer_fwd", grid=(n_chunks,),
        in_specs=[blk(GM_W, 0), blk(GM_W, 1), blk(SSD_W, 2), blk(CONV_CH, 2), blk(128, DT_BLK),
                  pl.BlockSpec((8, CONV_CH), lambda c: (jnp.maximum(c * (CH // 8) - 1, 0), 2))]
        + [full(a) for a in consts] + c_in_specs,
        out_specs=[pl.BlockSpec((CH, 2 * D), lambda c: (c, 0)), pl.BlockSpec((CH, SSD_W), lambda c: (c, 0)),
                   pl.BlockSpec((CH, SSD_W), lambda c: (c, 0))] + c_out_specs,
        out_shape=[jax.ShapeDtypeStruct((t, 2 * D), _MXU), jax.ShapeDtypeStruct((t, SSD_W), _F32),
                   jax.ShapeDtypeStruct((t, SSD_W), _F32)] + c_out_shape,
        scratch_shapes=[pltpu.VMEM((SSD_N, SSD_W), _F32), pltpu.VMEM((128, CH), _F32)] + c_scratch,
        compiler_params=pltpu.CompilerParams(dimension_semantics=("arbitrary",), vmem_limit_bytes=48 << 20),
    )(proj, proj, proj, proj, proj, proj, *consts, *c_in)
    return res[:3], res[3:]


def _mixer_bwd(proj, dcat, yss, states, gv, ws, bsb, gout, cw8, cb, dtb, alog, d_x, ng, e_mat, et_mat, ltri_mat,
               seq_chunks):
    t = proj.shape[0]
    n_chunks = t // CH

    def body(pu_ref, pv_ref, z_ref, xbc_ref, dt_ref, halo_ref, dcat_ref, y_ref, st_ref,
             gv_ref, ws_ref, bsb_ref, gout_ref, cw_ref, cb_ref, dtb_ref, alog_ref, dx_ref, ng_ref,
             e_ref, et_ref, ltri_ref,
             dproj_ref, dws_ref, dbs_ref, dgv_ref, dgout_ref, dng_ref, dcw_ref, dcb_ref, ddtb_ref, dalog_ref, dd_ref,
             ds_ref, dnext_ref, dcst_ref, cst_ref, dbacc_ref, ddacc_ref):
        i = pl.program_id(0)
        c = n_chunks - 1 - i
        first = (c % seq_chunks) == 0
        last_in_seq = (c % seq_chunks) == seq_chunks - 1
        tril = _iota2((CH, CH), 0) >= _iota2((CH, CH), 1)
        lane = _iota2((CH, 128), 1)
        row = _iota2((CH, 128), 0)

        @pl.when(i == 0)
        def _():
            for r in (dws_ref, dbs_ref, dgv_ref, dgout_ref, dng_ref, dcw_ref, dcb_ref, ddtb_ref, dalog_ref, dd_ref,
                      dbacc_ref, ddacc_ref, dcst_ref):
                r[...] = jnp.zeros_like(r)

        @pl.when(last_in_seq)
        def _():
            ds_ref[...] = jnp.zeros_like(ds_ref)
            dnext_ref[...] = jnp.zeros_like(dnext_ref)

        dcat_v = dcat_ref[...].astype(_F32)

        pu, pv = pu_ref[...], pv_ref[...]
        gv_v = gv_ref[...]
        y_a, u, keep = _gmlp_fwd_vals(pu, pv, gv_v, ws_ref, bsb_ref, True)
        dy, dgout8 = _rms_bwd(y_a, gout_ref[...], dcat_v[:, 0:GM_W])
        dgout_ref[...] += dgout8
        dus, dvs, dgvs = [], [], []
        for h in range(GM_H):
            sl = slice(h * 128, (h + 1) * 128)
            vh, r, vn, wm, mixed = keep[h + 1]
            dyh = dy[:, sl]
            dus.append(dyh * mixed)
            dmix = dyh * u[:, sl]
            dmix_m = _mx(dmix)
            dws_ref[h] += jnp.where(tril, _dot_nt(dmix_m, _mx(vn)), 0.0)
            dbacc_ref[h] += dmix
            dvn = _dot_tn(wm, dmix_m)
            gy = dvn * gv_v[:, sl]
            nh = vh * r
            dvs.append((gy - nh * jnp.mean(gy * nh, axis=-1, keepdims=True)) * r)
            dgvs.append(_sum8(dvn * nh))
        dgv_ref[...] += jnp.concatenate(dgvs, axis=1)
        cdf_u, cdf_v = keep[0]
        gelu_grad = lambda pre, cdf: cdf + pre * jnp.exp2(pre * pre * (-0.5 * _LOG2E) + _LOG2_INV_SQRT_2PI)
        dproj_ref[:, 0:GM_W] = (jnp.concatenate(dus, axis=1) * gelu_grad(pu, cdf_u)).astype(dproj_ref.dtype)
        dproj_ref[:, GM_W:2 * GM_W] = (jnp.concatenate(dvs, axis=1) * gelu_grad(pv, cdf_v)).astype(dproj_ref.dtype)

        halo8 = jnp.where(first, 0.0, halo_ref[...])
        q = _ssd_common(xbc_ref[...], halo8, dt_ref[...], cw_ref, cb_ref[...], dtb_ref[...], alog_ref[...], e_ref,
                        ltri_ref[...])
        act = q["act"]
        xv = act[:, 0:SSD_W]
        dt_x, ecs_x, dec_x, cdec_x = q["dt_x"], q["ecs_x"], q["dec_x"], q["cdec_x"]
        xdt = xv * dt_x
        xdt_m = _mx(xdt)
        cs = q["cs"]
        cst_ref[...] = cs.T
        s_prev = st_ref[...]
        ds = ds_ref[...]
        yv = y_ref[...]
        zv = z_ref[...]
        sig_z = _sigmoid(zv)
        sz = zv * sig_z
        yg = yv * sz
        dygs, dng8 = [], []
        for g in range(SSD_G):
            gs = slice(g * 512, (g + 1) * 512)
            a_, b_ = _rms_bwd(yg[:, gs], ng_ref[:, gs], dcat_v[:, GM_W + g * 512:GM_W + (g + 1) * 512])
            dygs.append(a_)
            dng8.append(b_)
        dyg = jnp.concatenate(dygs, axis=1)
        dng_ref[...] += jnp.concatenate(dng8, axis=1)
        dyv = dyg * sz
        dproj_ref[:, 2 * GM_W:2 * GM_W + SSD_W] = (dyg * yv * _silu_grad(sig_z, sz)).astype(dproj_ref.dtype)
        ddacc_ref[...] += _sum8(dyv * xv)
        dyv_m = _mx(dyv)

        dxdt_parts, db_parts, dc_parts = [], [], []
        dcs = jnp.zeros((CH, 128), _F32)
        dcs_x_parts, ddec_x_parts, dcl_x_parts = [], [], []
        for g in range(SSD_G):
            gs = slice(g * 512, (g + 1) * 512)
            bg = _mx(act[:, SSD_W + g * SSD_N:SSD_W + (g + 1) * SSD_N])
            cg = _mx(act[:, SSD_W + SSD_G * SSD_N + g * SSD_N:SSD_W + SSD_G * SSD_N + (g + 1) * SSD_N])
            cbm = _dot_nt(cg, bg)
            sg = s_prev[:, gs]
            sg_m = _mx(sg)
            dsg = ds[:, gs]
            dsg_m = _mx(dsg)
            zoff = _dot(cg, sg_m)
            dz_off = dyv[:, gs] * ecs_x[:, gs]
            dz_off_m = _mx(dz_off)
            dcs_x_parts.append(dyv[:, gs] * zoff * ecs_x[:, gs])
            dcg = _dot_nt(dz_off_m, sg_m)
            dsprev = _dot_tn(cg, dz_off_m)
            w_st = dec_x[:, gs] * xdt[:, gs]
            dw_st = _dot(bg, dsg_m)
            dbg = _dot_nt(_mx(w_st), dsg_m)
            dxdt_g = dec_x[:, gs] * dw_st
            ddec_x_parts.append(dw_st * xdt[:, gs])
            dsprev = dsprev + cdec_x[:, gs] * dsg
            dcl_x_parts.append(jnp.sum(dsg * sg, axis=0, keepdims=True) * cdec_x[:, gs])
            ds_ref[:, gs] = dsprev
            dcb = jnp.zeros((CH, CH), _F32)
            dxdt_pairs = []
            for pr in range(4):
                ps = slice(g * 512 + pr * 128, g * 512 + (pr + 1) * 128)
                acc_pair = None
                for hh in range(2):
                    h = g * 8 + pr * 2 + hh
                    in_head = (lane < SSD_P) if hh == 0 else (lane >= SSD_P)
                    lm = _head_lm(cs, cst_ref, h, tril)
                    m_h = cbm * lm
                    m_hm = _mx(m_h)
                    dyh_m = _mx(jnp.where(in_head, dyv[:, ps], 0.0))
                    dm = _dot_nt(dyh_m, xdt_m[:, ps])
                    dcb = dcb + dm * lm
                    qm = dm * m_h
                    dcs = dcs + jnp.where(lane == h, jnp.sum(qm, axis=1, keepdims=True), 0.0)
                    dcst_ref[h:h + 1, :] = jnp.sum(qm, axis=0, keepdims=True)
                    contrib = jnp.where(in_head, _dot_tn(m_hm, dyv_m[:, ps]), 0.0)
                    acc_pair = contrib if acc_pair is None else acc_pair + contrib
                dxdt_pairs.append(acc_pair)
            dxdt_parts.append(dxdt_g + jnp.concatenate(dxdt_pairs, axis=1))
            dcb_m = _mx(dcb)
            dc_parts.append(dcg + _dot(dcb_m, bg))
            db_parts.append(dbg + _dot_tn(dcb_m, cg))
        dxdt = jnp.concatenate(dxdt_parts, axis=1)
        dxv = dx_ref[...] * dyv + dxdt * dt_x
        et = et_ref[...]
        head_sum = lambda v: _dot(v.astype(_BF16), et)
        ddt = head_sum(dxdt * xv)
        dcs = dcs - dcst_ref[...].T + head_sum(jnp.concatenate(dcs_x_parts, axis=1))
        ddec = head_sum(jnp.concatenate(ddec_x_parts, axis=1)) * q["dec"]
        dcs = dcs - ddec
        dcl = jnp.sum(ddec, axis=0, keepdims=True) + _xdot(
            _dot, jnp.broadcast_to(jnp.concatenate(dcl_x_parts, axis=1), (8, SSD_W)), et)[0:1, :]
        dcs = jnp.where(row == CH - 1, dcs + dcl, dcs)
        da = _xdot_left(_dot_tn, ltri_ref[...], dcs)
        ddt = ddt + da * q["a_neg"]
        dalog_ref[...] += _sum8(da * q["dt"] * q["a_neg"])
        ddtraw = jnp.where(lane < SSD_H, ddt * _sigmoid(q["dtin"]), 0.0)
        ddtb_ref[...] += _sum8(ddtraw)
        dproj_ref[:, D_IN_PAD - 128:D_IN_PAD] = ddtraw.astype(dproj_ref.dtype)
        dcpre = jnp.concatenate([dxv] + db_parts + dc_parts, axis=1) * _silu_grad(q["sig"], act)
        dcb_ref[...] += _sum8(dcpre)
        for k in range(CONV_K):
            dcw_ref[k:k + 1, :] += jnp.sum(dcpre * q["xs"][CONV_K - 1 - k], axis=0, keepdims=True)
        next8 = dnext_ref[...]
        dxbc = sum(cw_ref[k:k + 1, :] * _shift_up(dcpre, next8, CONV_K - 1 - k) for k in range(CONV_K))
        dproj_ref[:, 2 * GM_W + SSD_W:2 * GM_W + SSD_W + CONV_CH] = dxbc.astype(dproj_ref.dtype)
        dnext_ref[...] = dcpre[0:8, :]

        @pl.when(i == n_chunks - 1)
        def _():
            for h in range(GM_H):
                dbs_ref[h:h + 1, :] = _xdot_left(_dot_nt, jnp.ones((8, 128), _BF16), dbacc_ref[h])[0:1, :]
            dd_ref[...] = _xdot(_dot, ddacc_ref[...], et)

    rblk = lambda w, j: pl.BlockSpec((CH, w), lambda i: (n_chunks - 1 - i, j))
    full = lambda arr: pl.BlockSpec(arr.shape, lambda i: (0,) * arr.ndim)
    acc = lambda shape: pl.BlockSpec(shape, lambda i: (0,) * len(shape))
    consts = [gv, ws, bsb, gout, cw8, cb, dtb, alog, d_x, ng, e_mat, et_mat, ltri_mat]
    acc_shapes = [(GM_H, CH, CH), (8, 128), (8, GM_W), (8, GM_W), (8, SSD_W), (8, CONV_CH), (8, CONV_CH), (8, 128),
                  (8, 128), (8, 128)]
    return pl.pallas_call(
        body, name="mixer_bwd", grid=(n_chunks,),
        in_specs=[rblk(GM_W, 0), rblk(GM_W, 1), rblk(SSD_W, 2), rblk(CONV_CH, 2), rblk(128, DT_BLK),
                  pl.BlockSpec((8, CONV_CH), lambda i: (jnp.maximum((n_chunks - 1 - i) * (CH // 8) - 1, 0), 2)),
                  rblk(2 * D, 0), rblk(SSD_W, 0), rblk(SSD_W, 0)] + [full(a) for a in consts],
        out_specs=[rblk(D_IN_PAD, 0)] + [acc(s) for s in acc_shapes],
        out_shape=[jax.ShapeDtypeStruct((t, D_IN_PAD), _MXU)] + [jax.ShapeDtypeStruct(s, _F32) for s in acc_shapes],
        scratch_shapes=[pltpu.VMEM((SSD_N, SSD_W), _F32), pltpu.VMEM((8, CONV_CH), _F32),
                        pltpu.VMEM((128, CH), _F32), pltpu.VMEM((128, CH), _F32),
                        pltpu.VMEM((GM_H, CH, 128), _F32), pltpu.VMEM((8, SSD_W), _F32)],
        compiler_params=pltpu.CompilerParams(dimension_semantics=("arbitrary",), vmem_limit_bytes=48 << 20),
    )(proj, proj, proj, proj, proj, proj, dcat, yss, states, *consts)


def _peers():
    x, y, c = lax.axis_index("x"), lax.axis_index("y"), lax.axis_index("c")
    out = []
    for k in range(1, N_DEV):
        fx, fy, fc = (k >> 2) & 1, (k >> 1) & 1, k & 1
        px, py, pc = (x + fx) % 2, (y + fy) % 2, (c + fc) % 2
        out.append((k - 1, (px, py, pc), 4 * px + 2 * py + pc))
    return out, 4 * x + 2 * y + c


def _comm_io(comm):
    any_spec = pl.BlockSpec(memory_space=pl.ANY)
    n = len(comm)
    out_shape = []
    for (kind, axis), src in comm:
        shp = list(src.shape)
        if kind in ("gather", "gather2"):
            shp[axis] *= N_DEV
        else:
            shp[axis] //= N_DEV
            shp = [N_DEV] + shp
        out_shape.append(jax.ShapeDtypeStruct(tuple(shp), src.dtype))
    scratch = [pltpu.SemaphoreType.DMA((n * (N_DEV - 1),)), pltpu.SemaphoreType.DMA((n * (N_DEV - 1),)),
               pltpu.SemaphoreType.DMA((n,))] if n else []
    return [src for _, src in comm], [any_spec] * n, [any_spec] * n, out_shape, scratch


def _window(ref, axis, idx, size):
    start = pl.multiple_of(idx * size, size)
    return ref.at[tuple(pl.ds(start, size) if a == axis else slice(None) for a in range(len(ref.shape)))]


def _comm_plans(kinds, src_refs, dst_refs, send_sems, recv_sems, local_sems):
    x, y, c = lax.axis_index("x"), lax.axis_index("y"), lax.axis_index("c")
    peers, me = _peers()
    plans = []
    for s, ((kind, axis), src, dst) in enumerate(zip(kinds, src_refs, dst_refs)):
        sems = lambda k: dict(send_sem=send_sems.at[s * (N_DEV - 1) + k], recv_sem=recv_sems.at[s * (N_DEV - 1) + k])
        remote = lambda src_ref, dst_ref, k, pid: pltpu.make_async_remote_copy(
            src_ref=src_ref, dst_ref=dst_ref, device_id=pid, device_id_type=_MESH, **sems(k))
        if kind == "gather2":
            size = src.shape[axis]
            win = lambda idx: _window(dst, axis, idx, size)
            sib, sib_idx = (x, y, 1 - c), 4 * x + 2 * y + (1 - c)
            local = pltpu.make_async_copy(src, win(me), local_sems.at[s])
            to_sib = remote(src, win(me), 0, sib)
            starts, forwards = [local, to_sib], []
            waits = [(local, "local"), (to_sib, "send"), (remote(src, win(sib_idx), 0, sib), "recv")]
            for j, (fx, fy) in enumerate(((1, 0), (0, 1), (1, 1))):
                px, py = (x + fx) % 2, (y + fy) % 2
                same, other = 4 * px + 2 * py + c, 4 * px + 2 * py + (1 - c)
                out = remote(src, win(me), 1 + j, (px, py, c))
                starts.append(out)
                passed = remote(win(same), win(same), 4 + j, sib)
                forwards.append((remote(src, win(same), 1 + j, (px, py, c)), passed))
                waits += [(out, "send"), (passed, "send"), (remote(win(other), win(other), 4 + j, sib), "recv")]
            plans.append((starts, forwards, waits))
            continue
        if kind == "gather":
            size = src.shape[axis]
            src_for = lambda pidx: src
            dst_mine = _window(dst, axis, me, size)
        else:
            size = src.shape[axis] // N_DEV
            src_for = lambda pidx: _window(src, axis, pidx, size)
            dst_mine = dst.at[me]
        local = pltpu.make_async_copy(src_for(me), dst_mine, local_sems.at[s])
        remotes = [remote(src_for(pidx), dst_mine, k, pid) for k, pid, pidx in peers]
        plans.append(([local] + remotes, [], [(local, "local")] + [(cp, "both") for cp in remotes]))
    return plans


def _comm_start(*refs):
    for starts, _, _ in _comm_plans(*refs):
        for cp in starts:
            cp.start()


def _comm_finish(*refs):
    for _, forwards, waits in _comm_plans(*refs):
        for arrival, cp in forwards:
            arrival.wait_recv()
            cp.start()
        for cp, what in waits:
            if what == "send":
                cp.wait_send()
            elif what == "recv":
                cp.wait_recv()
            else:
                cp.wait()


def _adam_vals(w, g, m, v):
    m = B1 * m + (1.0 - B1) * g
    v = B2 * v + (1.0 - B2) * (g * g)
    m_hat = m / (1.0 - B1 ** STEP)
    v_hat = v / (1.0 - B2 ** STEP)
    delta = -LR * (m_hat / (jnp.sqrt(v_hat) + ADAM_EPS) + WD * w)
    return delta, m, v


def _sum_adam(name, recv, w, m, v, tile=256):
    _, r, wd = recv.shape
    if r % min(tile, r) == 0:
        tr, tc = min(tile, r), wd
    else:
        tr, tc = r, tile
        assert wd % tc == 0, (name, r, wd)

    def body(recv_ref, w_ref, m_ref, v_ref, g_out, d_out, m_out, v_out):
        g = recv_ref[0].astype(_F32)
        for s in range(1, N_DEV):
            g = g + recv_ref[s].astype(_F32)
        d_, m_, v_ = _adam_vals(w_ref[...], g, m_ref[...], v_ref[...])
        g_out[...] = g
        d_out[...] = d_
        m_out[...] = m_
        v_out[...] = v_

    spec = pl.BlockSpec((tr, tc), lambda i, j: (i, j))
    return pl.pallas_call(
        body, name=name, grid=(r // tr, wd // tc),
        in_specs=[pl.BlockSpec((N_DEV, tr, tc), lambda i, j: (0, i, j)), spec, spec, spec],
        out_specs=[spec] * 4, out_shape=[jax.ShapeDtypeStruct((r, wd), _F32)] * 4,
        compiler_params=pltpu.CompilerParams(dimension_semantics=("parallel", "parallel"),
                                             vmem_limit_bytes=48 << 20),
    )(recv, w, m, v)


def _small_local(parts, segments, n_rows):
    def body(parts_ref, loc_ref):
        loc_ref[...] = jnp.zeros_like(loc_ref)
        for out_row, n_out, in_row, n_in, kind in segments:
            if kind == "copy":
                loc_ref[out_row:out_row + n_out, :] = parts_ref[in_row:in_row + n_in, :]
            else:
                s = jnp.sum(parts_ref[in_row:in_row + n_in, :], axis=0, keepdims=True)
                if kind == "loss":
                    s = jnp.broadcast_to(jnp.sum(s, axis=1, keepdims=True) * (0.5 / D), (1, D))
                loc_ref[out_row:out_row + 1, :] = s

    vm = pl.BlockSpec(memory_space=pltpu.VMEM)
    return pl.pallas_call(body, name="small_local", in_specs=[vm], out_specs=vm,
                          out_shape=jax.ShapeDtypeStruct((n_rows, D), _F32))(parts)


def _small_final(blocks, late8, late_row, w, m, v):
    n_rows = w.shape[0]

    def body(blocks_ref, late_ref, w_ref, m_ref, v_ref, g_out, d_out, m_out, v_out, loc_ref, recv_ref, send_sems,
             recv_sems):
        peers, me = _peers()
        loc_ref[...] = jnp.broadcast_to(jnp.sum(late_ref[...], axis=0, keepdims=True), (8, D))
        recv_ref[me] = loc_ref[...]
        copies = [pltpu.make_async_remote_copy(src_ref=loc_ref, dst_ref=recv_ref.at[me], send_sem=send_sems.at[k],
                                               recv_sem=recv_sems.at[k], device_id=pid, device_id_type=_MESH)
                  for k, pid, _ in peers]
        for cp in copies:
            cp.start()
        g = blocks_ref[0:n_rows, :]
        for s in range(1, N_DEV):
            g = g + blocks_ref[s * n_rows:(s + 1) * n_rows, :]
        for cp in copies:
            cp.wait()
        late = recv_ref[0]
        for s in range(1, N_DEV):
            late = late + recv_ref[s]
        g = jnp.where(_iota2((n_rows, D), 0) == late_row, jnp.broadcast_to(late[0:1, :], (n_rows, D)), g)
        d_, m_, v_ = _adam_vals(w_ref[...], g, m_ref[...], v_ref[...])
        g_out[...] = g
        d_out[...] = d_
        m_out[...] = m_
        v_out[...] = v_

    vm = pl.BlockSpec(memory_space=pltpu.VMEM)
    return pl.pallas_call(
        body, name="small_final", in_specs=[vm] * 5, out_specs=[vm] * 4,
        out_shape=[jax.ShapeDtypeStruct((n_rows, D), _F32)] * 4,
        scratch_shapes=[pltpu.VMEM((8, D), _F32), pltpu.VMEM((N_DEV, 8, D), _F32),
                        pltpu.SemaphoreType.DMA((N_DEV - 1,)), pltpu.SemaphoreType.DMA((N_DEV - 1,))],
        compiler_params=pltpu.CompilerParams(vmem_limit_bytes=48 << 20),
    )(blocks, late8, w, m, v)


_BIG_NAMES =("w_in", "w_out", "w_ff1", "w_ff2", "w_ple_gate", "w_ple_proj")

_G_VECS = ("norm_mix_g", "gm_v_norm_g", "gm_out_norm_g", "ssd_norm_g", "norm_mlp_g", "ple_norm_g", "final_norm_g")
_LATE = _G_VECS[0]


def _const_mats():
    h = np.arange(128)[:, None]
    ch = np.arange(SSD_W)[None, :]
    e = (ch // SSD_P == h).astype(np.float32)
    ltri = (np.arange(CH)[:, None] >= np.arange(CH)[None, :]).astype(np.float32)
    return jnp.asarray(e, _BF16), jnp.asarray(e.T, _BF16), jnp.asarray(ltri, _BF16)


def _pad_lanes(v, n=128):
    return jnp.pad(v, ((0, 0), (0, n - v.shape[1])))


def _local_step(x, p, tgt, shard, conv_w_shard, small, seq_len):
    seq_chunks = seq_len // CH
    e_mat, et_mat, ltri_mat = _const_mats()
    g_mix, g_mlp, g_ple = small["norm_mix_g"], small["norm_mlp_g"], small["ple_norm_g"]
    g_fin = small["final_norm_g"].reshape(1, D)
    gv, gout, ng = small["gm_v_norm_g"], small["gm_out_norm_g"], small["ssd_norm_g"]
    ws = small["gm_ws"][0]
    bsb = jnp.broadcast_to(small["gm_bs"][0][:, :, None], (GM_H, CH, 128))
    cb = small["ssd_conv_b"]
    dtb, alog = _pad_lanes(small["ssd_dt_bias"]), _pad_lanes(small["ssd_a_log"])
    d_x = jnp.repeat(small["ssd_d"], SSD_P, axis=1)

    first = lambda acc: (acc,)
    rows, cols = ("gather2", 0), ("gather2", 1)
    n1, (g_win, g_cw) = _norm_cast("norm_mix", x, g_mix,
                                   comm=[(rows, shard["w_in"][None]), (("gather", 0), conv_w_shard[None])])
    w_in = jnp.pad(g_win.transpose(2, 0, 1).reshape(D, D_IN), ((0, 0), (0, D_IN_PAD - D_IN)))
    cw8 = jnp.pad(g_cw.transpose(1, 0, 2).reshape(CONV_K, CONV_CH), ((0, 8 - CONV_K), (0, 0)))
    mix_consts = (gv, ws, bsb, gout, cw8, cb, dtb, alog, d_x, ng)
    (proj,), (w_out, w1) = _matmul("proj_in", n1, w_in, "nn", 256, D_IN_PAD, D, first, [("tile", _F32)],
                                   comm=[(rows, shard["w_out"]), (cols, shard["w_ff1"])])
    (cat, yss, states), (w2, wg, wp) = _mixer_fwd(
        proj, *mix_consts, e_mat, ltri_mat, seq_chunks,
        comm=[(rows, shard["w_ff2"]), (rows, shard["w_ple_gate"]), (cols, shard["w_ple_proj"])])

    def epi_res_norm(acc, res, g):
        hv = acc + res
        return hv, _rms(hv, g)

    (h1, n2), _ = _matmul("proj_out", cat, w_out, "nn", 1024, D, 2 * D, epi_res_norm,
                          [("tile", _F32), ("tile", _MXU)], extras=[(x, "tile"), (g_mlp, "row")])

    def epi_relu2(acc):
        hid = jnp.maximum(acc, 0.0)
        return hid, hid * hid

    (hid, hid2), _ = _matmul("ff1", n2, w1, "nn", 1024, 1024, D, epi_relu2, [("tile", _MXU), ("tile", _MXU)])
    (h2, n3), _ = _matmul("ff2", hid2, w2, "nn", 512, D, D_FF, epi_res_norm, [("tile", _F32), ("tile", _MXU)],
                          extras=[(h1, "tile"), (g_ple, "row")])
    (pp,), _ = _matmul("ple_proj", p, wp, "nn", 512, D, D_PLE, first, [("tile", _F32)])

    def epi_head(acc, ppv, h2v, tg, gf):
        gate = _sigmoid(acc)
        gp = gate * ppv
        h3 = h2v + gp
        r = lax.rsqrt(jnp.mean(h3 * h3, axis=-1, keepdims=True) + EPS)
        nh = h3 * r
        err = nh * gf - tg
        gy = err * (gf * (1.0 / D))
        dh3 = (gy - nh * jnp.mean(gy * nh, axis=-1, keepdims=True)) * r
        dpp = dh3 * gate
        da3 = dpp * (ppv - gp)
        return dh3, da3, dpp, _sum8(err * err), _sum8(err * nh) * (1.0 / D)

    (dh3, da3, dpp, lossp, dgfin), _ = _matmul(
        "ple_gate_loss", n3, wg, "nn", 512, D, D, epi_head,
        [("tile", _F32), ("tile", _MXU), ("tile", _MXU), ("part8", _F32), ("part8", _F32)],
        extras=[(pp, "tile"), (h2, "tile"), (tgt, "tile"), (g_fin, "row")])

    s_rows, s_cols = ("scatter", 0), ("scatter", 1)
    (dwp,), _ = _matmul("d_w_ple_proj", p, dpp, "tn", D_PLE, D, 2048, first, [("tile", _BF16)])
    (dwg,), _ = _matmul("d_w_ple_gate", n3, da3, "tn", D, D, 2048, first, [("tile", _BF16)])

    def epi_norm_bwd(acc, up, hv, g):
        dx, dg8 = _rms_bwd(hv, g, acc)
        dh = up + dx
        return dh, dh, dg8

    (dh2, dh2b, dgple), (r_wp, r_wg) = _matmul(
        "d_h2", da3, wg, "nt", 1024, D, D, epi_norm_bwd, [("tile", _F32), ("tile", _MXU), ("part8", _F32)],
        extras=[(dh3, "tile"), (h2, "tile"), (g_ple, "row")],
        comm=[(s_cols, dwp), (s_rows, dwg)])
    (dw2,), _ = _matmul("d_w_ff2", hid2, dh2b, "tn", 1024, D, 2048, first, [("tile", _BF16)])
    (da1,), (r_w2,) = _matmul("d_ff_hidden", dh2b, w2, "nt", 512, 2048, D,
                              lambda acc, hv: (acc * 2.0 * hv.astype(_F32),), [("tile", _MXU)],
                              extras=[(hid, "tile")], comm=[(s_rows, dw2)])
    (dw1,), _ = _matmul("d_w_ff1", n2, da1, "tn", 1024, 1024, 2048, first, [("tile", _BF16)])
    (dh1, dh1b, dgmlp), (r_w1,) = _matmul(
        "d_h1", da1, w1, "nt", 256, D, D_FF, epi_norm_bwd, [("tile", _F32), ("tile", _MXU), ("part8", _F32)],
        extras=[(dh2, "tile"), (h1, "tile"), (g_mlp, "row")], comm=[(s_cols, dw1)])
    (dwout,), _ = _matmul("d_w_out", cat, dh1b, "tn", 1024, D, 2048, first, [("tile", _BF16)])
    (dcat,), (r_wout,) = _matmul("d_cat", dh1b, w_out, "nt", 1024, 1024, D, first, [("tile", _F32)],
                                 comm=[(s_rows, dwout)])
    (dproj, dws, dbs, dgv, dgout, dng, dcw, dcb, ddtb, dalog, dd) = _mixer_bwd(
        proj, dcat, yss, states, *mix_consts, e_mat, et_mat, ltri_mat, seq_chunks)
    pieces = dict(gm_v_norm_g=dgv, gm_out_norm_g=dgout, ssd_norm_g=dng, norm_mlp_g=dgmlp, ple_norm_g=dgple,
                  final_norm_g=dgfin, gm_ws=dws, gm_bs=dbs, ssd_conv_w=dcw, ssd_conv_b=dcb, ssd_dt_bias=ddtb,
                  ssd_a_log=dalog, ssd_d=dd, loss=lossp)
    parts, segments, n_rows, where = _small_layout(pieces)
    small_block = _small_local(parts, segments, n_rows)
    (dwin,), (small_blocks,) = _matmul("d_w_in", n1, dproj, "tn", 512, D_IN_PAD, 1024, first, [("tile", _BF16)],
                                       comm=[(("gather", 0), small_block)])
    dwin_blocks = dwin[:, :D_IN].reshape(D, N_DEV, SHARD_IN).transpose(1, 2, 0)
    (gx, dgmix), (r_win,) = _matmul(
        "d_x", dproj, w_in, "nt", 256, D, D_IN_PAD, lambda *a: epi_norm_bwd(*a)[1:], [("tile", _F32), ("part8", _F32)],
        extras=[(dh1, "tile"), (x, "tile"), (g_mix, "row")], comm=[(s_rows, dwin_blocks)])
    r_win = r_win.reshape(N_DEV, SHARD_IN, D)

    big = dict(w_in=r_win, w_out=r_wout, w_ff1=r_w1, w_ff2=r_w2, w_ple_gate=r_wg, w_ple_proj=r_wp)
    return gx, big, small_blocks, dgmix, n_rows, where


def _small_layout(pieces):
    rows, segments = [], []
    in_row, out_row = 0, 0

    def add(arr, kind, n_out):
        nonlocal in_row, out_row
        rows.append(arr)
        segments.append((out_row, n_out, in_row, arr.shape[0], kind))
        start = out_row
        in_row += arr.shape[0]
        out_row += n_out
        return start

    where = {_LATE: 0}
    out_row = 1
    for name in _G_VECS[1:]:
        where[name] = add(pieces[name], "sum", 1)
    where["gm_ws"] = add(pieces["gm_ws"].reshape(GM_H * CH * CH // D, D), "copy", GM_H * CH * CH // D)
    where["gm_bs"] = add(pieces["gm_bs"].reshape(1, D), "copy", 1)
    cb = jnp.pad(pieces["ssd_conv_b"], ((0, 0), (0, 2 * D - CONV_CH)))
    where["ssd_conv_b"] = add(cb[:, :D], "sum", 1)
    add(cb[:, D:], "sum", 1)
    cw = jnp.pad(pieces["ssd_conv_w"][:CONV_K], ((0, 0), (0, 2 * D - CONV_CH)))
    where["ssd_conv_w"] = add(cw.reshape(2 * CONV_K, D), "copy", 2 * CONV_K)
    misc = jnp.concatenate([pieces["ssd_dt_bias"], pieces["ssd_a_log"], pieces["ssd_d"],
                            jnp.zeros((8, D - 3 * 128), _F32)], axis=1)
    where["misc"] = add(misc, "sum", 1)
    where["loss"] = add(pieces["loss"], "loss", 1)
    n_rows = -(-out_row // 8) * 8
    return jnp.concatenate(rows, axis=0), tuple(segments), n_rows, where


def _pack_small_params(vals, where, n_rows, my_block):
    rows, at = [], {}

    def add(name, arr):
        at[name] = sum(r.shape[0] for r in rows)
        rows.append(arr)

    for name in _G_VECS:
        add(name, vals[name].reshape(1, D))
    add("gm_ws", vals["gm_ws"].reshape(GM_H * CH * CH // D, D))
    add("gm_bs", vals["gm_bs"].reshape(1, D))
    cb = jnp.pad(vals["ssd_conv_b"].reshape(1, CONV_CH), ((0, 0), (0, 2 * D - CONV_CH)))
    add("ssd_conv_b", cb.reshape(2, D))
    cw = lax.dynamic_update_slice(jnp.zeros((CONV_K, 2 * D), _F32), vals["ssd_conv_w"].reshape(CONV_K, -1),
                                  (0, my_block * (CONV_CH // N_DEV)))
    add("ssd_conv_w", cw.reshape(2 * CONV_K, D))
    misc = jnp.concatenate([_pad_lanes(vals["ssd_dt_bias"].reshape(1, SSD_H)),
                            _pad_lanes(vals["ssd_a_log"].reshape(1, SSD_H)),
                            _pad_lanes(vals["ssd_d"].reshape(1, SSD_H)), jnp.zeros((1, D - 3 * 128), _F32)], axis=1)
    add("misc", misc)
    assert all(where[k] == r for k, r in at.items()), (where, at)
    rows.append(jnp.zeros((n_rows - sum(r.shape[0] for r in rows), D), _F32))
    return jnp.concatenate(rows, axis=0)


def _unpack_small(buf, where, my_block, shapes):
    out = {}
    for name in _G_VECS:
        out[name] = buf[where[name]].reshape(shapes[name])
    n_ws = GM_H * CH * CH // D
    out["gm_ws"] = buf[where["gm_ws"]:where["gm_ws"] + n_ws].reshape(shapes["gm_ws"])
    out["gm_bs"] = buf[where["gm_bs"]].reshape(shapes["gm_bs"])
    r = where["ssd_conv_b"]
    out["ssd_conv_b"] = buf[r:r + 2].reshape(1, 2 * D)[:, :CONV_CH].reshape(shapes["ssd_conv_b"])
    r = where["ssd_conv_w"]
    cw = buf[r:r + 2 * CONV_K].reshape(CONV_K, 2 * D)
    out["ssd_conv_w"] = lax.dynamic_slice(cw, (0, my_block * (CONV_CH // N_DEV)),
                                          (CONV_K, CONV_CH // N_DEV)).reshape(shapes["ssd_conv_w"])
    misc = buf[where["misc"]]
    for i, name in enumerate(("ssd_dt_bias", "ssd_a_log", "ssd_d")):
        out[name] = misc[i * 128:i * 128 + SSD_H].reshape(shapes[name])
    return out


_WEIGHTS = ("norm_mix_g", "w_in", "gm_v_norm_g", "gm_ws", "gm_bs", "gm_out_norm_g", "ssd_conv_w", "ssd_conv_b",
            "ssd_dt_bias", "ssd_a_log", "ssd_d", "ssd_norm_g", "w_out", "norm_mlp_g", "w_ff1", "w_ff2", "ple_norm_g",
            "w_ple_gate", "w_ple_proj", "final_norm_g")


def kernel(x, p, norm_mix_g, w_in, gm_v_norm_g, gm_ws, gm_bs, gm_out_norm_g, ssd_conv_w, ssd_conv_b, ssd_dt_bias, ssd_a_log, ssd_d, ssd_norm_g, w_out, norm_mlp_g, w_ff1, w_ff2, ple_norm_g, w_ple_gate, w_ple_proj, final_norm_g, loss_target, m_norm_mix_g, m_w_in, m_gm_v_norm_g, m_gm_ws, m_gm_bs, m_gm_out_norm_g, m_ssd_conv_w, m_ssd_conv_b, m_ssd_dt_bias, m_ssd_a_log, m_ssd_d, m_ssd_norm_g, m_w_out, m_norm_mlp_g, m_w_ff1, m_w_ff2, m_ple_norm_g, m_w_ple_gate, m_w_ple_proj, m_final_norm_g, v_norm_mix_g, v_w_in, v_gm_v_norm_g, v_gm_ws, v_gm_bs, v_gm_out_norm_g, v_ssd_conv_w, v_ssd_conv_b, v_ssd_dt_bias, v_ssd_a_log, v_ssd_d, v_ssd_norm_g, v_w_out, v_norm_mlp_g, v_w_ff1, v_w_ff2, v_ple_norm_g, v_w_ple_gate, v_w_ple_proj, v_final_norm_g):
    args = dict(locals())
    w = {n: args[n] for n in _WEIGHTS}
    m = {n: args["m_" + n] for n in _WEIGHTS}
    v = {n: args["v_" + n] for n in _WEIGHTS}
    shapes = {n: w[n].shape for n in _WEIGHTS}
    my_block = 4 * lax.axis_index("x") + 2 * lax.axis_index("y") + lax.axis_index("c")
    nb, seq_len, _ = x.shape

    local = lambda d, n: d[n][0].T if n == "w_in" else d[n][0]
    shard = {n: local(w, n).astype(_MXU) for n in _BIG_NAMES}
    small = {n: w[n] for n in _WEIGHTS if n not in _BIG_NAMES}
    gx, recv, small_blocks, late8, n_rows, where = _local_step(
        x.reshape(nb * seq_len, D), p.reshape(nb * seq_len, D_PLE), loss_target.reshape(nb * seq_len, D), shard,
        ssd_conv_w[0], small, seq_len)

    big_out = [{}, {}, {}, {}]
    for n in _BIG_NAMES:
        res = _sum_adam("sum_adam_" + n, recv[n], local(w, n), local(m, n), local(v, n))
        for k in range(4):
            big_out[k][n] = (res[k].T if n == "w_in" else res[k]).reshape(shapes[n])

    packs = [_pack_small_params(d, where, n_rows, my_block) for d in (w, m, v)]
    small_res = _small_final(small_blocks, late8, where[_LATE], *packs)
    loss = small_res[0][where["loss"], 0]
    small_out = [_unpack_small(a, where, my_block, shapes) for a in small_res]

    outs = [loss, gx.reshape(x.shape)]
    for k in range(4):
        outs += [big_out[k][n] if n in _BIG_NAMES else small_out[k][n] for n in _WEIGHTS]
    return tuple(outs)
```

```python
import functools
import math

import jax
import jax.numpy as jnp
import numpy as np
from jax import lax
from jax.experimental import pallas as pl
from jax.experimental.pallas import tpu as pltpu

_F32 = jnp.float32
_BF16 = jnp.bfloat16
_MXU = jnp.bfloat16

D = 1024
D_PLE = 256
GM_W = 1024
GM_H = 8
CH = 128
SSD_W = 1024
SSD_H = 16
SSD_P = 64
SSD_G = 2
SSD_N = 128
CONV_K = 4
CONV_CH = SSD_W + 2 * SSD_G * SSD_N
D_FF = 4096
D_IN = 2 * GM_W + SSD_W + CONV_CH + SSD_H
D_IN_PAD = 4736
DT_BLK = (D_IN_PAD - 128) // 128
EPS = 1e-6
N_DEV = 8
SHARD_IN = D_IN // N_DEV

LR, B1, B2, ADAM_EPS, WD, STEP = 0.001, 0.9, 0.999, 1e-08, 0.01, 10
_LOG2E = math.log2(math.e)
_LOG2_INV_SQRT_2PI = -0.5 * math.log2(2.0 * math.pi)

_V7X_VMEM_BYTES = 64 * 1024 * 1024
_VMEM_CAP = _V7X_VMEM_BYTES - 8 * 1024 * 1024
_MESH = pl.DeviceIdType.MESH


def _vmem_limit(nbytes):
    return int(min(_VMEM_CAP, max(32 * 1024 * 1024, nbytes * 5 // 4 + (4 << 20))))


def _nbytes(shape, dtype):
    return int(np.prod(shape)) * jnp.dtype(dtype).itemsize


def _mx(v):
    return v.astype(_MXU)


def _dot(a, b):
    return jnp.dot(a, b, preferred_element_type=_F32)


def _dot_nt(a, b):
    return lax.dot_general(a, b, (((1,), (1,)), ((), ())), preferred_element_type=_F32)


def _dot_tn(a, b):
    return lax.dot_general(a, b, (((0,), (0,)), ((), ())), preferred_element_type=_F32)


def _split3(a):
    hi = a.astype(_BF16)
    r = a - hi.astype(_F32)
    mid = r.astype(_BF16)
    lo = (r - mid.astype(_F32)).astype(_BF16)
    return hi, mid, lo


def _xdot(dotfn, a, b01):
    b = b01.astype(_BF16)
    hi, mid, lo = _split3(a)
    return (dotfn(hi, b) + dotfn(mid, b)) + dotfn(lo, b)


def _xdot_left(dotfn, a01, b):
    a = a01.astype(_BF16)
    hi, mid, lo = _split3(b)
    return (dotfn(a, hi) + dotfn(a, mid)) + dotfn(a, lo)


def _sum8(v):
    r, n = v.shape
    return v.reshape(r // 8, 8, n).sum(axis=0)


def _sigmoid(v):
    return 1.0 / (1.0 + jnp.exp(-v))


def _rms(xv, g):
    ms = jnp.mean(xv * xv, axis=-1, keepdims=True)
    return xv * lax.rsqrt(ms + EPS) * g


def _rms_bwd(xv, g, dn):
    r = lax.rsqrt(jnp.mean(xv * xv, axis=-1, keepdims=True) + EPS)
    nh = xv * r
    gy = dn * g
    dx = (gy - nh * jnp.mean(gy * nh, axis=-1, keepdims=True)) * r
    return dx, _sum8(dn * nh)


def _iota2(shape, axis):
    return lax.broadcasted_iota(jnp.int32, shape, axis)


def _norm_cast(name, x, g, tm=512, comm=()):
    t, n = x.shape
    tm = min(tm, t)
    steps = t // tm
    kinds = [kind for kind, _ in comm]
    c_in, c_in_specs, c_out_specs, c_out_shape, c_scratch = _comm_io(comm)

    def body(*refs):
        x_ref, g_ref = refs[0], refs[1]
        o_ref = refs[2 + len(comm)]
        comm_refs = (kinds, refs[2:2 + len(comm)], refs[3 + len(comm):3 + 2 * len(comm)], *refs[3 + 2 * len(comm):])
        if comm:
            pl.when(pl.program_id(0) == 0)(lambda: _comm_start(*comm_refs))

        o_ref[...] = _rms(x_ref[...], g_ref[...]).astype(o_ref.dtype)
        if comm:
            pl.when(pl.program_id(0) == steps - 1)(lambda: _comm_finish(*comm_refs))

    res = pl.pallas_call(
        body, name=name, grid=(steps,),
        in_specs=[pl.BlockSpec((tm, n), lambda i: (i, 0)), pl.BlockSpec((1, n), lambda i: (0, 0))] + c_in_specs,
        out_specs=[pl.BlockSpec((tm, n), lambda i: (i, 0))] + c_out_specs,
        out_shape=[jax.ShapeDtypeStruct((t, n), _MXU)] + c_out_shape, scratch_shapes=c_scratch,
        compiler_params=pltpu.CompilerParams(dimension_semantics=("arbitrary",)),
    )(x, g, *c_in)
    return res[0], res[1:]


def _matmul(name, a, b, mode, tm, tn, tk, epilogue, outs, extras=(), comm=()):
    m, k = a.shape[::-1] if mode == "tn" else a.shape
    n = b.shape[0] if mode == "nt" else b.shape[1]
    tm, tn, tk = min(tm, m), min(tn, n), min(tk, k)
    assert m % tm == 0 and n % tn == 0 and k % tk == 0, (name, m, n, k, tm, tn, tk)
    if mode == "nn":
        a_spec = pl.BlockSpec((tm, tk), lambda i, j, kk: (i, kk))
        b_spec = pl.BlockSpec((tk, tn), lambda i, j, kk: (kk, j))
        dotfn = _dot
    elif mode == "nt":
        a_spec = pl.BlockSpec((tm, tk), lambda i, j, kk: (i, kk))
        b_spec = pl.BlockSpec((tn, tk), lambda i, j, kk: (j, kk))
        dotfn = _dot_nt
    else:
        a_spec = pl.BlockSpec((tk, tm), lambda i, j, kk: (kk, i))
        b_spec = pl.BlockSpec((tk, tn), lambda i, j, kk: (kk, j))
        dotfn = _dot_tn
    ni, nj, nk = m // tm, n // tn, k // tk
    n_ex, n_out, n_comm = len(extras), len(outs), len(comm)
    kinds = [kind for kind, _ in comm]
    c_in, c_in_specs, c_out_specs, c_out_shape, c_scratch = _comm_io(comm)

    in_specs, vmem = [a_spec, b_spec], 2 * (tm * tk * a.dtype.itemsize + tk * tn * b.dtype.itemsize)
    for arr, kind in extras:
        if kind == "tile":
            in_specs.append(pl.BlockSpec((tm, tn), lambda i, j, kk: (i, j)))
            vmem += 2 * _nbytes((tm, tn), arr.dtype)
        else:
            in_specs.append(pl.BlockSpec((1, tn), lambda i, j, kk: (0, j)))
    out_specs, out_shape = [], []
    for kind, dt in outs:
        if kind == "tile":
            out_specs.append(pl.BlockSpec((tm, tn), lambda i, j, kk: (i, j)))
            out_shape.append(jax.ShapeDtypeStruct((m, n), dt))
            vmem += 2 * _nbytes((tm, tn), dt)
        elif kind == "tile_t":
            out_specs.append(pl.BlockSpec((tn, tm), lambda i, j, kk: (j, i)))
            out_shape.append(jax.ShapeDtypeStruct((n, m), dt))
            vmem += 2 * _nbytes((tm, tn), dt)
        else:
            assert nj == 1, "the partial-sum rows are accumulated over consecutive row tiles"
            out_specs.append(pl.BlockSpec((8, tn), lambda i, j, kk: (0, 0)))
            out_shape.append(jax.ShapeDtypeStruct((8, n), dt))
    scratch = [pltpu.VMEM((tm, tn), _F32)] if nk > 1 else []
    vmem += _nbytes((tm, tn), _F32) * 2

    def body(*refs):
        a_ref, b_ref = refs[0], refs[1]
        ex_refs = refs[2:2 + n_ex]
        n_in = 2 + n_ex + n_comm
        out_refs = refs[n_in:n_in + n_out]
        i, j, kk = pl.program_id(0), pl.program_id(1), pl.program_id(2)
        comm_refs = (kinds, refs[2 + n_ex:n_in], refs[n_in + n_out:n_in + n_out + n_comm], *refs[len(refs) - 3:])
        if n_comm:
            pl.when((i == 0) & (j == 0) & (kk == 0))(lambda: _comm_start(*comm_refs))

        part = dotfn(_mx(a_ref[...]), _mx(b_ref[...]))

        def finish(acc):
            vals = epilogue(acc, *[r[...] for r in ex_refs])
            for r, v, (kind, _) in zip(out_refs, vals, outs):
                if kind == "part8":
                    @pl.when(i == 0)
                    def _():
                        r[...] = v

                    @pl.when(i > 0)
                    def _():
                        r[...] += v
                else:
                    r[...] = v.astype(r.dtype)

        if nk == 1:
            finish(part)
        else:
            acc_ref = refs[n_in + n_out + n_comm]

            @pl.when(kk == 0)
            def _():
                acc_ref[...] = part

            @pl.when(kk > 0)
            def _():
                acc_ref[...] += part

            @pl.when(kk == nk - 1)
            def _():
                finish(acc_ref[...])

        if n_comm:
            pl.when((i == ni - 1) & (j == nj - 1) & (kk == nk - 1))(lambda: _comm_finish(*comm_refs))

    carried =n_comm or any(kind == "part8" for kind, _ in outs)
    sem = ("arbitrary",) * 3 if carried else ("parallel", "parallel", "arbitrary")
    res = pl.pallas_call(
        body, name=name, grid=(ni, nj, nk),
        in_specs=in_specs + c_in_specs, out_specs=out_specs + c_out_specs, out_shape=out_shape + c_out_shape,
        scratch_shapes=scratch + c_scratch,
        compiler_params=pltpu.CompilerParams(dimension_semantics=sem, vmem_limit_bytes=_vmem_limit(vmem)),
    )(a, b, *[arr for arr, _ in extras], *c_in)
    return res[:n_out], res[n_out:]


def _shift_down(v, halo8, j):
    if j == 0:
        return v
    r = pltpu.roll(v, j, axis=0)
    hr = pltpu.roll(halo8, j, axis=0)
    top = jnp.where(_iota2(hr.shape, 0) < j, hr, r[:8])
    return jnp.concatenate([top, r[8:]], axis=0)


def _shift_up(v, next8, j):
    if j == 0:
        return v
    rows = v.shape[0]
    r = pltpu.roll(v, rows - j, axis=0)
    nr = pltpu.roll(next8, 8 - j, axis=0)
    bot = jnp.where(_iota2(nr.shape, 0) >= 8 - j, nr, r[rows - 8:])
    return jnp.concatenate([r[:rows - 8], bot], axis=0)


def _silu_grad(sig, silu):
    return sig + silu * (1.0 - sig)


def _gmlp_fwd_vals(pu, pv, gv, ws_ref, bsb_ref, want_bwd):
    tril = _iota2((CH, CH), 0) >= _iota2((CH, CH), 1)
    cdf_u = 0.5 * (1.0 + lax.erf(pu * 0.7071067811865476))
    cdf_v = 0.5 * (1.0 + lax.erf(pv * 0.7071067811865476))
    u = pu * cdf_u
    v = pv * cdf_v
    ys, keep = [], [(cdf_u, cdf_v)] if want_bwd else []
    for h in range(GM_H):
        sl = slice(h * 128, (h + 1) * 128)
        vh = v[:, sl]
        r = lax.rsqrt(jnp.mean(vh * vh, axis=-1, keepdims=True) + EPS)
        vn = vh * r * gv[:, sl]
        wm = _mx(jnp.where(tril, ws_ref[h], 0.0))
        mixed = _dot(wm, _mx(vn)) + bsb_ref[h]
        ys.append(u[:, sl] * mixed)
        if want_bwd:
            keep.append((vh, r, vn, wm, mixed))
    return jnp.concatenate(ys, axis=1), u, keep


def _ssd_common(xbc, halo8, dtraw, cw_ref, cb, dtb, alog, e_ref, ltri):
    xs = [_shift_down(xbc, halo8, j) for j in range(CONV_K)]
    cpre = cb + sum(cw_ref[k:k + 1, :] * xs[CONV_K - 1 - k] for k in range(CONV_K))
    sig = _sigmoid(cpre)
    act = cpre * sig
    dtin = dtraw + dtb
    dt = jnp.maximum(dtin, 0.0) + jnp.log(1.0 + jnp.exp(-jnp.abs(dtin)))
    a_neg = -jnp.exp(alog)
    cs = _xdot_left(_dot, ltri, dt * a_neg)
    cs_last = cs[CH - 1:CH, :]
    ecs = jnp.exp(cs)
    dec = jnp.exp(cs_last - cs)
    cdec = jnp.exp(cs_last)
    e = e_ref[...]
    dt_x = _dot(dt.astype(_BF16), e)
    ecs_x = _dot(ecs.astype(_BF16), e)
    dec_x = _dot(dec.astype(_BF16), e)
    cdec_x = _xdot(_dot, jnp.broadcast_to(cdec, (8, 128)), e)[0:1, :]
    return dict(xs=xs, sig=sig, act=act, dtin=dtin, dt=dt, a_neg=a_neg, cs=cs, ecs=ecs, dec=dec, cdec=cdec,
                dt_x=dt_x, ecs_x=ecs_x, dec_x=dec_x, cdec_x=cdec_x)


def _head_lm(cs, cst_ref, h, tril):
    seg = jnp.broadcast_to(cs[:, h:h + 1], (CH, CH)) - cst_ref[h:h + 1, :]
    return jnp.exp(jnp.where(tril, seg, -jnp.inf))


def _mixer_fwd(proj, gv, ws, bsb, gout, cw8, cb, dtb, alog, d_x, ng, e_mat, ltri_mat, seq_chunks, comm=()):
    t = proj.shape[0]
    n_chunks = t // CH
    n_comm = len(comm)
    kinds = [kind for kind, _ in comm]
    c_in, c_in_specs, c_out_specs, c_out_shape, c_scratch = _comm_io(comm)

    def body(*refs):
        (pu_ref, pv_ref, z_ref, xbc_ref, dt_ref, halo_ref, gv_ref, ws_ref, bsb_ref, gout_ref, cw_ref, cb_ref,
         dtb_ref, alog_ref, dx_ref, ng_ref, e_ref, ltri_ref) = refs[:18]
        cat_ref, y_ref, st_ref = refs[18 + n_comm:21 + n_comm]
        s_ref, cst_ref = refs[21 + 2 * n_comm:23 + 2 * n_comm]
        comm_refs = (kinds, refs[18:18 + n_comm], refs[21 + n_comm:21 + 2 * n_comm], *refs[23 + 2 * n_comm:])
        c = pl.program_id(0)
        if n_comm:
            pl.when(c == 0)(lambda: _comm_start(*comm_refs))
            pl.when(c == n_chunks - 1)(lambda: _comm_finish(*comm_refs))

        first = (c % seq_chunks) == 0
        tril = _iota2((CH, CH), 0) >= _iota2((CH, CH), 1)
        lane = _iota2((CH, 128), 1)

        y_a, _, _ = _gmlp_fwd_vals(pu_ref[...], pv_ref[...], gv_ref[...], ws_ref, bsb_ref, False)
        cat_ref[:, 0:GM_W] = _rms(y_a, gout_ref[...]).astype(cat_ref.dtype)

        @pl.when(first)
        def _():
            s_ref[...] = jnp.zeros_like(s_ref)

        halo8 = jnp.where(first, 0.0, halo_ref[...])
        q = _ssd_common(xbc_ref[...], halo8, dt_ref[...], cw_ref, cb_ref[...], dtb_ref[...], alog_ref[...], e_ref,
                        ltri_ref[...])
        act = q["act"]
        xv = act[:, 0:SSD_W]
        xdt = xv * q["dt_x"]
        xdt_m = _mx(xdt)
        cs = q["cs"]
        cst_ref[...] = cs.T
        s_prev = s_ref[...]
        st_ref[...] = s_prev
        ys = []
        for g in range(SSD_G):
            bg = _mx(act[:, SSD_W + g * SSD_N:SSD_W + (g + 1) * SSD_N])
            cg = _mx(act[:, SSD_W + SSD_G * SSD_N + g * SSD_N:SSD_W + SSD_G * SSD_N + (g + 1) * SSD_N])
            cbm = _dot_nt(cg, bg)
            gs = slice(g * 512, (g + 1) * 512)
            for pr in range(4):
                ps = slice(g * 512 + pr * 128, g * 512 + (pr + 1) * 128)
                o = []
                for hh in range(2):
                    h = g * 8 + pr * 2 + hh
                    m_h = _mx(cbm * _head_lm(cs, cst_ref, h, tril))
                    o.append(_dot(m_h, xdt_m[:, ps]))
                ys.append(jnp.where(lane < SSD_P, o[0], o[1]))
            sg = s_prev[:, gs]
            yoff = _dot(cg, _mx(sg)) * q["ecs_x"][:, gs]
            ys[-4:] = [ys[-4 + i] + yoff[:, i * 128:(i + 1) * 128] for i in range(4)]
            st_new = _dot_tn(bg, _mx(q["dec_x"][:, gs] * xdt[:, gs]))
            s_ref[:, gs] = sg * q["cdec_x"][:, gs] + st_new
        y = jnp.concatenate(ys, axis=1) + dx_ref[...] * xv
        y_ref[...] = y
        zv = z_ref[...]
        yg = y * (zv * _sigmoid(zv))
        for g in range(SSD_G):
            gs = slice(g * 512, (g + 1) * 512)
            cat_ref[:, GM_W + g * 512:GM_W + (g + 1) * 512] = _rms(yg[:, gs], ng_ref[:, gs]).astype(cat_ref.dtype)

    blk = lambda w, j: pl.BlockSpec((CH, w), lambda c: (c, j))
    full = lambda arr: pl.BlockSpec(arr.shape, lambda c: (0,) * arr.ndim)
    consts = [gv, ws, bsb, gout, cw8, cb, dtb, alog, d_x, ng, e_mat, ltri_mat]
    res = pl.pallas_call(
        body, name="mixer_fwd", grid=(n_chunks,),
        in_specs=[blk(GM_W, 0), blk(GM_W, 1), blk(SSD_W, 2), blk(CONV_CH, 2), blk(128, DT_BLK),
                  pl.BlockSpec((8, CONV_CH), lambda c: (jnp.maximum(c * (CH // 8) - 1, 0), 2))]
        + [full(a) for a in consts] + c_in_specs,
        out_specs=[pl.BlockSpec((CH, 2 * D), lambda c: (c, 0)), pl.BlockSpec((CH, SSD_W), lambda c: (c, 0)),
                   pl.BlockSpec((CH, SSD_W), lambda c: (c, 0))] + c_out_specs,
        out_shape=[jax.ShapeDtypeStruct((t, 2 * D), _MXU), jax.ShapeDtypeStruct((t, SSD_W), _F32),
                   jax.ShapeDtypeStruct((t, SSD_W), _F32)] + c_out_shape,
        scratch_shapes=[pltpu.VMEM((SSD_N, SSD_W), _F32), pltpu.VMEM((128, CH), _F32)] + c_scratch,
        compiler_params=pltpu.CompilerParams(dimension_semantics=("arbitrary",), vmem_limit_bytes=48 << 20),
    )(proj, proj, proj, proj, proj, proj, *consts, *c_in)
    return res[:3], res[3:]


def _mixer_bwd(proj, dcat, yss, states, gv, ws, bsb, gout, cw8, cb, dtb, alog, d_x, ng, e_mat, et_mat, ltri_mat,
               seq_chunks):
    t = proj.shape[0]
    n_chunks = t // CH

    def body(pu_ref, pv_ref, z_ref, xbc_ref, dt_ref, halo_ref, dcat_ref, y_ref, st_ref,
             gv_ref, ws_ref, bsb_ref, gout_ref, cw_ref, cb_ref, dtb_ref, alog_ref, dx_ref, ng_ref,
             e_ref, et_ref, ltri_ref,
             dproj_ref, dws_ref, dbs_ref, dgv_ref, dgout_ref, dng_ref, dcw_ref, dcb_ref, ddtb_ref, dalog_ref, dd_ref,
             ds_ref, dnext_ref, dcst_ref, cst_ref, dbacc_ref, ddacc_ref):
        i = pl.program_id(0)
        c = n_chunks - 1 - i
        first = (c % seq_chunks) == 0
        last_in_seq = (c % seq_chunks) == seq_chunks - 1
        tril = _iota2((CH, CH), 0) >= _iota2((CH, CH), 1)
        lane = _iota2((CH, 128), 1)
        row = _iota2((CH, 128), 0)

        @pl.when(i == 0)
        def _():
            for r in (dws_ref, dbs_ref, dgv_ref, dgout_ref, dng_ref, dcw_ref, dcb_ref, ddtb_ref, dalog_ref, dd_ref,
                      dbacc_ref, ddacc_ref, dcst_ref):
                r[...] = jnp.zeros_like(r)

        @pl.when(last_in_seq)
        def _():
            ds_ref[...] = jnp.zeros_like(ds_ref)
            dnext_ref[...] = jnp.zeros_like(dnext_ref)

        dcat_v = dcat_ref[...].astype(_F32)

        pu, pv = pu_ref[...], pv_ref[...]
        gv_v = gv_ref[...]
        y_a, u, keep = _gmlp_fwd_vals(pu, pv, gv_v, ws_ref, bsb_ref, True)
        dy, dgout8 = _rms_bwd(y_a, gout_ref[...], dcat_v[:, 0:GM_W])
        dgout_ref[...] += dgout8
        dus, dvs, dgvs = [], [], []
        for h in range(GM_H):
            sl = slice(h * 128, (h + 1) * 128)
            vh, r, vn, wm, mixed = keep[h + 1]
            dyh = dy[:, sl]
            dus.append(dyh * mixed)
            dmix = dyh * u[:, sl]
            dmix_m = _mx(dmix)
            dws_ref[h] += jnp.where(tril, _dot_nt(dmix_m, _mx(vn)), 0.0)
            dbacc_ref[h] += dmix
            dvn = _dot_tn(wm, dmix_m)
            gy = dvn * gv_v[:, sl]
            nh = vh * r
            dvs.append((gy - nh * jnp.mean(gy * nh, axis=-1, keepdims=True)) * r)
            dgvs.append(_sum8(dvn * nh))
        dgv_ref[...] += jnp.concatenate(dgvs, axis=1)
        cdf_u, cdf_v = keep[0]
        gelu_grad = lambda pre, cdf: cdf + pre * jnp.exp2(pre * pre * (-0.5 * _LOG2E) + _LOG2_INV_SQRT_2PI)
        dproj_ref[:, 0:GM_W] = (jnp.concatenate(dus, axis=1) * gelu_grad(pu, cdf_u)).astype(dproj_ref.dtype)
        dproj_ref[:, GM_W:2 * GM_W] = (jnp.concatenate(dvs, axis=1) * gelu_grad(pv, cdf_v)).astype(dproj_ref.dtype)

        halo8 = jnp.where(first, 0.0, halo_ref[...])
        q = _ssd_common(xbc_ref[...], halo8, dt_ref[...], cw_ref, cb_ref[...], dtb_ref[...], alog_ref[...], e_ref,
                        ltri_ref[...])
        act = q["act"]
        xv = act[:, 0:SSD_W]
        dt_x, ecs_x, dec_x, cdec_x = q["dt_x"], q["ecs_x"], q["dec_x"], q["cdec_x"]
        xdt = xv * dt_x
        xdt_m = _mx(xdt)
        cs = q["cs"]
        cst_ref[...] = cs.T
        s_prev = st_ref[...]
        ds = ds_ref[...]
        yv = y_ref[...]
        zv = z_ref[...]
        sig_z = _sigmoid(zv)
        sz = zv * sig_z
        yg = yv * sz
        dygs, dng8 = [], []
        for g in range(SSD_G):
            gs = slice(g * 512, (g + 1) * 512)
            a_, b_ = _rms_bwd(yg[:, gs], ng_ref[:, gs], dcat_v[:, GM_W + g * 512:GM_W + (g + 1) * 512])
            dygs.append(a_)
            dng8.append(b_)
        dyg = jnp.concatenate(dygs, axis=1)
        dng_ref[...] += jnp.concatenate(dng8, axis=1)
        dyv = dyg * sz
        dproj_ref[:, 2 * GM_W:2 * GM_W + SSD_W] = (dyg * yv * _silu_grad(sig_z, sz)).astype(dproj_ref.dtype)
        ddacc_ref[...] += _sum8(dyv * xv)
        dyv_m = _mx(dyv)

        dxdt_parts, db_parts, dc_parts = [], [], []
        dcs = jnp.zeros((CH, 128), _F32)
        dcs_x_parts, ddec_x_parts, dcl_x_parts = [], [], []
        for g in range(SSD_G):
            gs = slice(g * 512, (g + 1) * 512)
            bg = _mx(act[:, SSD_W + g * SSD_N:SSD_W + (g + 1) * SSD_N])
            cg = _mx(act[:, SSD_W + SSD_G * SSD_N + g * SSD_N:SSD_W + SSD_G * SSD_N + (g + 1) * SSD_N])
            cbm = _dot_nt(cg, bg)
            sg = s_prev[:, gs]
            sg_m = _mx(sg)
            dsg = ds[:, gs]
            dsg_m = _mx(dsg)
            zoff = _dot(cg, sg_m)
            dz_off = dyv[:, gs] * ecs_x[:, gs]
            dz_off_m = _mx(dz_off)
            dcs_x_parts.append(dyv[:, gs] * zoff * ecs_x[:, gs])
            dcg = _dot_nt(dz_off_m, sg_m)
            dsprev = _dot_tn(cg, dz_off_m)
            w_st = dec_x[:, gs] * xdt[:, gs]
            dw_st = _dot(bg, dsg_m)
            dbg = _dot_nt(_mx(w_st), dsg_m)
            dxdt_g = dec_x[:, gs] * dw_st
            ddec_x_parts.append(dw_st * xdt[:, gs])
            dsprev = dsprev + cdec_x[:, gs] * dsg
            dcl_x_parts.append(jnp.sum(dsg * sg, axis=0, keepdims=True) * cdec_x[:, gs])
            ds_ref[:, gs] = dsprev
            dcb = jnp.zeros((CH, CH), _F32)
            dxdt_pairs = []
            for pr in range(4):
                ps = slice(g * 512 + pr * 128, g * 512 + (pr + 1) * 128)
                acc_pair = None
                for hh in range(2):
                    h = g * 8 + pr * 2 + hh
                    in_head = (lane < SSD_P) if hh == 0 else (lane >= SSD_P)
                    lm = _head_lm(cs, cst_ref, h, tril)
                    m_h = cbm * lm
                    m_hm = _mx(m_h)
                    dyh_m = _mx(jnp.where(in_head, dyv[:, ps], 0.0))
                    dm = _dot_nt(dyh_m, xdt_m[:, ps])
                    dcb = dcb + dm * lm
                    qm = dm * m_h
                    dcs = dcs + jnp.where(lane == h, jnp.sum(qm, axis=1, keepdims=True), 0.0)
                    dcst_ref[h:h + 1, :] = jnp.sum(qm, axis=0, keepdims=True)
                    contrib = jnp.where(in_head, _dot_tn(m_hm, dyv_m[:, ps]), 0.0)
                    acc_pair = contrib if acc_pair is None else acc_pair + contrib
                dxdt_pairs.append(acc_pair)
            dxdt_parts.append(dxdt_g + jnp.concatenate(dxdt_pairs, axis=1))
            dcb_m = _mx(dcb)
            dc_parts.append(dcg + _dot(dcb_m, bg))
            db_parts.append(dbg + _dot_tn(dcb_m, cg))
        dxdt = jnp.concatenate(dxdt_parts, axis=1)
        dxv = dx_ref[...] * dyv + dxdt * dt_x
        et = et_ref[...]
        head_sum = lambda v: _dot(v.astype(_BF16), et)
        ddt = head_sum(dxdt * xv)
        dcs = dcs - dcst_ref[...].T + head_sum(jnp.concatenate(dcs_x_parts, axis=1))
        ddec = head_sum(jnp.concatenate(ddec_x_parts, axis=1)) * q["dec"]
        dcs = dcs - ddec
        dcl = jnp.sum(ddec, axis=0, keepdims=True) + _xdot(
            _dot, jnp.broadcast_to(jnp.concatenate(dcl_x_parts, axis=1), (8, SSD_W)), et)[0:1, :]
        dcs = jnp.where(row == CH - 1, dcs + dcl, dcs)
        da = _xdot_left(_dot_tn, ltri_ref[...], dcs)
        ddt = ddt + da * q["a_neg"]
        dalog_ref[...] += _sum8(da * q["dt"] * q["a_neg"])
        ddtraw = jnp.where(lane < SSD_H, ddt * _sigmoid(q["dtin"]), 0.0)
        ddtb_ref[...] += _sum8(ddtraw)
        dproj_ref[:, D_IN_PAD - 128:D_IN_PAD] = ddtraw.astype(dproj_ref.dtype)
        dcpre = jnp.concatenate([dxv] + db_parts + dc_parts, axis=1) * _silu_grad(q["sig"], act)
        dcb_ref[...] += _sum8(dcpre)
        for k in range(CONV_K):
            dcw_ref[k:k + 1, :] += jnp.sum(dcpre * q["xs"][CONV_K - 1 - k], axis=0, keepdims=True)
        next8 = dnext_ref[...]
        dxbc = sum(cw_ref[k:k + 1, :] * _shift_up(dcpre, next8, CONV_K - 1 - k) for k in range(CONV_K))
        dproj_ref[:, 2 * GM_W + SSD_W:2 * GM_W + SSD_W + CONV_CH] = dxbc.astype(dproj_ref.dtype)
        dnext_ref[...] = dcpre[0:8, :]

        @pl.when(i == n_chunks - 1)
        def _():
            for h in range(GM_H):
                dbs_ref[h:h + 1, :] = _xdot_left(_dot_nt, jnp.ones((8, 128), _BF16), dbacc_ref[h])[0:1, :]
            dd_ref[...] = _xdot(_dot, ddacc_ref[...], et)

    rblk = lambda w, j: pl.BlockSpec((CH, w), lambda i: (n_chunks - 1 - i, j))
    full = lambda arr: pl.BlockSpec(arr.shape, lambda i: (0,) * arr.ndim)
    acc = lambda shape: pl.BlockSpec(shape, lambda i: (0,) * len(shape))
    consts = [gv, ws, bsb, gout, cw8, cb, dtb, alog, d_x, ng, e_mat, et_mat, ltri_mat]
    acc_shapes = [(GM_H, CH, CH), (8, 128), (8, GM_W), (8, GM_W), (8, SSD_W), (8, CONV_CH), (8, CONV_CH), (8, 128),
                  (8, 128), (8, 128)]
    return pl.pallas_call(
        body, name="mixer_bwd", grid=(n_chunks,),
        in_specs=[rblk(GM_W, 0), rblk(GM_W, 1), rblk(SSD_W, 2), rblk(CONV_CH, 2), rblk(128, DT_BLK),
                  pl.BlockSpec((8, CONV_CH), lambda i: (jnp.maximum((n_chunks - 1 - i) * (CH // 8) - 1, 0), 2)),
                  rblk(2 * D, 0), rblk(SSD_W, 0), rblk(SSD_W, 0)] + [full(a) for a in consts],
        out_specs=[rblk(D_IN_PAD, 0)] + [acc(s) for s in acc_shapes],
        out_shape=[jax.ShapeDtypeStruct((t, D_IN_PAD), _MXU)] + [jax.ShapeDtypeStruct(s, _F32) for s in acc_shapes],
        scratch_shapes=[pltpu.VMEM((SSD_N, SSD_W), _F32), pltpu.VMEM((8, CONV_CH), _F32),
                        pltpu.VMEM((128, CH), _F32), pltpu.VMEM((128, CH), _F32),
                        pltpu.VMEM((GM_H, CH, 128), _F32), pltpu.VMEM((8, SSD_W), _F32)],
        compiler_params=pltpu.CompilerParams(dimension_semantics=("arbitrary",), vmem_limit_bytes=48 << 20),
    )(proj, proj, proj, proj, proj, proj, dcat, yss, states, *consts)


def _peers():
    x, y, c = lax.axis_index("x"), lax.axis_index("y"), lax.axis_index("c")
    out = []
    for k in range(1, N_DEV):
        fx, fy, fc = (k >> 2) & 1, (k >> 1) & 1, k & 1
        px, py, pc = (x + fx) % 2, (y + fy) % 2, (c + fc) % 2
        out.append((k - 1, (px, py, pc), 4 * px + 2 * py + pc))
    return out, 4 * x + 2 * y + c


def _comm_io(comm):
    any_spec = pl.BlockSpec(memory_space=pl.ANY)
    n = len(comm)
    out_shape = []
    for (kind, axis), src in comm:
        shp = list(src.shape)
        if kind in ("gather", "gather2"):
            shp[axis] *= N_DEV
        else:
            shp[axis] //= N_DEV
            shp = [N_DEV] + shp
        out_shape.append(jax.ShapeDtypeStruct(tuple(shp), src.dtype))
    scratch = [pltpu.SemaphoreType.DMA((n * (N_DEV - 1),)), pltpu.SemaphoreType.DMA((n * (N_DEV - 1),)),
               pltpu.SemaphoreType.DMA((n,))] if n else []
    return [src for _, src in comm], [any_spec] * n, [any_spec] * n, out_shape, scratch


def _window(ref, axis, idx, size):
    start = pl.multiple_of(idx * size, size)
    return ref.at[tuple(pl.ds(start, size) if a == axis else slice(None) for a in range(len(ref.shape)))]


def _comm_plans(kinds, src_refs, dst_refs, send_sems, recv_sems, local_sems):
    x, y, c = lax.axis_index("x"), lax.axis_index("y"), lax.axis_index("c")
    peers, me = _peers()
    plans = []
    for s, ((kind, axis), src, dst) in enumerate(zip(kinds, src_refs, dst_refs)):
        sems = lambda k: dict(send_sem=send_sems.at[s * (N_DEV - 1) + k], recv_sem=recv_sems.at[s * (N_DEV - 1) + k])
        remote = lambda src_ref, dst_ref, k, pid: pltpu.make_async_remote_copy(
            src_ref=src_ref, dst_ref=dst_ref, device_id=pid, device_id_type=_MESH, **sems(k))
        if kind == "gather2":
            size = src.shape[axis]
            win = lambda idx: _window(dst, axis, idx, size)
            sib, sib_idx = (x, y, 1 - c), 4 * x + 2 * y + (1 - c)
            local = pltpu.make_async_copy(src, win(me), local_sems.at[s])
            to_sib = remote(src, win(me), 0, sib)
            starts, forwards = [local, to_sib], []
            waits = [(local, "local"), (to_sib, "send"), (remote(src, win(sib_idx), 0, sib), "recv")]
            for j, (fx, fy) in enumerate(((1, 0), (0, 1), (1, 1))):
                px, py = (x + fx) % 2, (y + fy) % 2
                same, other = 4 * px + 2 * py + c, 4 * px + 2 * py + (1 - c)
                out = remote(src, win(me), 1 + j, (px, py, c))
                starts.append(out)
                passed = remote(win(same), win(same), 4 + j, sib)
                forwards.append((remote(src, win(same), 1 + j, (px, py, c)), passed))
                waits += [(out, "send"), (passed, "send"), (remote(win(other), win(other), 4 + j, sib), "recv")]
            plans.append((starts, forwards, waits))
            continue
        if kind == "gather":
            size = src.shape[axis]
            src_for = lambda pidx: src
            dst_mine = _window(dst, axis, me, size)
        else:
            size = src.shape[axis] // N_DEV
            src_for = lambda pidx: _window(src, axis, pidx, size)
            dst_mine = dst.at[me]
        local = pltpu.make_async_copy(src_for(me), dst_mine, local_sems.at[s])
        remotes = [remote(src_for(pidx), dst_mine, k, pid) for k, pid, pidx in peers]
        plans.append(([local] + remotes, [], [(local, "local")] + [(cp, "both") for cp in remotes]))
    return plans


def _comm_start(*refs):
    for starts, _, _ in _comm_plans(*refs):
        for cp in starts:
            cp.start()


def _comm_finish(*refs):
    for _, forwards, waits in _comm_plans(*refs):
        for arrival, cp in forwards:
            arrival.wait_recv()
            cp.start()
        for cp, what in waits:
            if what == "send":
                cp.wait_send()
            elif what == "recv":
                cp.wait_recv()
            else:
                cp.wait()


def _adam_vals(w, g, m, v):
    m = B1 * m + (1.0 - B1) * g
    v = B2 * v + (1.0 - B2) * (g * g)
    m_hat = m / (1.0 - B1 ** STEP)
    v_hat = v / (1.0 - B2 ** STEP)
    delta = -LR * (m_hat / (jnp.sqrt(v_hat) + ADAM_EPS) + WD * w)
    return delta, m, v


def _sum_adam(name, recv, w, m, v, tile=256):
    _, r, wd = recv.shape
    if r % min(tile, r) == 0:
        tr, tc = min(tile, r), wd
    else:
        tr, tc = r, tile
        assert wd % tc == 0, (name, r, wd)

    def body(recv_ref, w_ref, m_ref, v_ref, g_out, d_out, m_out, v_out):
        g = recv_ref[0].astype(_F32)
        for s in range(1, N_DEV):
            g = g + recv_ref[s].astype(_F32)
        d_, m_, v_ = _adam_vals(w_ref[...], g, m_ref[...], v_ref[...])
        g_out[...] = g
        d_out[...] = d_
        m_out[...] = m_
        v_out[...] = v_

    spec = pl.BlockSpec((tr, tc), lambda i, j: (i, j))
    return pl.pallas_call(
        body, name=name, grid=(r // tr, wd // tc),
        in_specs=[pl.BlockSpec((N_DEV, tr, tc), lambda i, j: (0, i, j)), spec, spec, spec],
        out_specs=[spec] * 4, out_shape=[jax.ShapeDtypeStruct((r, wd), _F32)] * 4,
        compiler_params=pltpu.CompilerParams(dimension_semantics=("parallel", "parallel"),
                                             vmem_limit_bytes=48 << 20),
    )(recv, w, m, v)


def _small_local(parts, segments, n_rows):
    def body(parts_ref, loc_ref):
        loc_ref[...] = jnp.zeros_like(loc_ref)
        for out_row, n_out, in_row, n_in, kind in segments:
            if kind == "copy":
                loc_ref[out_row:out_row + n_out, :] = parts_ref[in_row:in_row + n_in, :]
            else:
                s = jnp.sum(parts_ref[in_row:in_row + n_in, :], axis=0, keepdims=True)
                if kind == "loss":
                    s = jnp.broadcast_to(jnp.sum(s, axis=1, keepdims=True) * (0.5 / D), (1, D))
                loc_ref[out_row:out_row + 1, :] = s

    vm = pl.BlockSpec(memory_space=pltpu.VMEM)
    return pl.pallas_call(body, name="small_local", in_specs=[vm], out_specs=vm,
                          out_shape=jax.ShapeDtypeStruct((n_rows, D), _F32))(parts)


def _small_final(blocks, late8, late_row, w, m, v):
    n_rows = w.shape[0]

    def body(blocks_ref, late_ref, w_ref, m_ref, v_ref, g_out, d_out, m_out, v_out, loc_ref, recv_ref, send_sems,
             recv_sems):
        peers, me = _peers()
        loc_ref[...] = jnp.broadcast_to(jnp.sum(late_ref[...], axis=0, keepdims=True), (8, D))
        recv_ref[me] = loc_ref[...]
        copies = [pltpu.make_async_remote_copy(src_ref=loc_ref, dst_ref=recv_ref.at[me], send_sem=send_sems.at[k],
                                               recv_sem=recv_sems.at[k], device_id=pid, device_id_type=_MESH)
                  for k, pid, _ in peers]
        for cp in copies:
            cp.start()
        g = blocks_ref[0:n_rows, :]
        for s in range(1, N_DEV):
            g = g + blocks_ref[s * n_rows:(s + 1) * n_rows, :]
        for cp in copies:
            cp.wait()
        late = recv_ref[0]
        for s in range(1, N_DEV):
            late = late + recv_ref[s]
        g = jnp.where(_iota2((n_rows, D), 0) == late_row, jnp.broadcast_to(late[0:1, :], (n_rows, D)), g)
        d_, m_, v_ = _adam_vals(w_ref[...], g, m_ref[...], v_ref[...])
        g_out[...] = g
        d_out[...] = d_
        m_out[...] = m_
        v_out[...] = v_

    vm = pl.BlockSpec(memory_space=pltpu.VMEM)
    return pl.pallas_call(
        body, name="small_final", in_specs=[vm] * 5, out_specs=[vm] * 4,
        out_shape=[jax.ShapeDtypeStruct((n_rows, D), _F32)] * 4,
        scratch_shapes=[pltpu.VMEM((8, D), _F32), pltpu.VMEM((N_DEV, 8, D), _F32),
                        pltpu.SemaphoreType.DMA((N_DEV - 1,)), pltpu.SemaphoreType.DMA((N_DEV - 1,))],
        compiler_params=pltpu.CompilerParams(vmem_limit_bytes=48 << 20),
    )(blocks, late8, w, m, v)


_BIG_NAMES =("w_in", "w_out", "w_ff1", "w_ff2", "w_ple_gate", "w_ple_proj")

_G_VECS = ("norm_mix_g", "gm_v_norm_g", "gm_out_norm_g", "ssd_norm_g", "norm_mlp_g", "ple_norm_g", "final_norm_g")
_LATE = _G_VECS[0]


def _const_mats():
    h = np.arange(128)[:, None]
    ch = np.arange(SSD_W)[None, :]
    e = (ch // SSD_P == h).astype(np.float32)
    ltri = (np.arange(CH)[:, None] >= np.arange(CH)[None, :]).astype(np.float32)
    return jnp.asarray(e, _BF16), jnp.asarray(e.T, _BF16), jnp.asarray(ltri, _BF16)


def _pad_lanes(v, n=128):
    return jnp.pad(v, ((0, 0), (0, n - v.shape[1])))


def _local_step(x, p, tgt, shard, conv_w_shard, small, seq_len):
    seq_chunks = seq_len // CH
    e_mat, et_mat, ltri_mat = _const_mats()
    g_mix, g_mlp, g_ple = small["norm_mix_g"], small["norm_mlp_g"], small["ple_norm_g"]
    g_fin = small["final_norm_g"].reshape(1, D)
    gv, gout, ng = small["gm_v_norm_g"], small["gm_out_norm_g"], small["ssd_norm_g"]
    ws = small["gm_ws"][0]
    bsb = jnp.broadcast_to(small["gm_bs"][0][:, :, None], (GM_H, CH, 128))
    cb = small["ssd_conv_b"]
    dtb, alog = _pad_lanes(small["ssd_dt_bias"]), _pad_lanes(small["ssd_a_log"])
    d_x = jnp.repeat(small["ssd_d"], SSD_P, axis=1)

    first = lambda acc: (acc,)
    rows, cols = ("gather2", 0), ("gather2", 1)
    n1, (g_win, g_cw) = _norm_cast("norm_mix", x, g_mix,
                                   comm=[(rows, shard["w_in"][None]), (("gather", 0), conv_w_shard[None])])
    w_in_t = jnp.pad(g_win.reshape(D_IN, D), ((0, D_IN_PAD - D_IN), (0, 0)))
    cw8 = jnp.pad(g_cw.transpose(1, 0, 2).reshape(CONV_K, CONV_CH), ((0, 8 - CONV_K), (0, 0)))
    mix_consts = (gv, ws, bsb, gout, cw8, cb, dtb, alog, d_x, ng)
    (proj,), (w_out, w1) = _matmul("proj_in", n1, w_in_t, "nt", 256, D_IN_PAD, D, first, [("tile", _F32)],
                                   comm=[(rows, shard["w_out"]), (cols, shard["w_ff1"])])
    (cat, yss, states), (w2, wg, wp) = _mixer_fwd(
        proj, *mix_consts, e_mat, ltri_mat, seq_chunks,
        comm=[(rows, shard["w_ff2"]), (rows, shard["w_ple_gate"]), (cols, shard["w_ple_proj"])])

    def epi_res_norm(acc, res, g):
        hv = acc + res
        return hv, _rms(hv, g)

    (h1, n2), _ = _matmul("proj_out", cat, w_out, "nn", 1024, D, 2 * D, epi_res_norm,
                          [("tile", _F32), ("tile", _MXU)], extras=[(x, "tile"), (g_mlp, "row")])

    def epi_relu2(acc):
        hid = jnp.maximum(acc, 0.0)
        return hid, hid * hid

    (hid, hid2), _ = _matmul("ff1", n2, w1, "nn", 1024, 1024, D, epi_relu2, [("tile", _MXU), ("tile", _MXU)])
    (h2, n3), _ = _matmul("ff2", hid2, w2, "nn", 512, D, D_FF, epi_res_norm, [("tile", _F32), ("tile", _MXU)],
                          extras=[(h1, "tile"), (g_ple, "row")])
    (pp,), _ = _matmul("ple_proj", p, wp, "nn", 512, D, D_PLE, first, [("tile", _F32)])

    def epi_head(acc, ppv, h2v, tg, gf):
        gate = _sigmoid(acc)
        gp = gate * ppv
        h3 = h2v + gp
        r = lax.rsqrt(jnp.mean(h3 * h3, axis=-1, keepdims=True) + EPS)
        nh = h3 * r
        err = nh * gf - tg
        gy = err * (gf * (1.0 / D))
        dh3 = (gy - nh * jnp.mean(gy * nh, axis=-1, keepdims=True)) * r
        dpp = dh3 * gate
        da3 = dpp * (ppv - gp)
        return dh3, da3, dpp, _sum8(err * err), _sum8(err * nh) * (1.0 / D)

    (dh3, da3, dpp, lossp, dgfin), _ = _matmul(
        "ple_gate_loss", n3, wg, "nn", 512, D, D, epi_head,
        [("tile", _F32), ("tile", _MXU), ("tile", _MXU), ("part8", _F32), ("part8", _F32)],
        extras=[(pp, "tile"), (h2, "tile"), (tgt, "tile"), (g_fin, "row")])

    s_rows, s_cols = ("scatter", 0), ("scatter", 1)
    (dwp,), _ = _matmul("d_w_ple_proj", p, dpp, "tn", D_PLE, D, 2048, first, [("tile", _BF16)])
    (dwg,), _ = _matmul("d_w_ple_gate", n3, da3, "tn", D, D, 2048, first, [("tile", _BF16)])

    def epi_norm_bwd(acc, up, hv, g):
        dx, dg8 = _rms_bwd(hv, g, acc)
        dh = up + dx
        return dh, dh, dg8

    (dh2, dh2b, dgple), (r_wp, r_wg) = _matmul(
        "d_h2", da3, wg, "nt", 1024, D, D, epi_norm_bwd, [("tile", _F32), ("tile", _MXU), ("part8", _F32)],
        extras=[(dh3, "tile"), (h2, "tile"), (g_ple, "row")],
        comm=[(s_cols, dwp), (s_rows, dwg)])
    (dw2,), _ = _matmul("d_w_ff2", hid2, dh2b, "tn", 1024, D, 2048, first, [("tile", _BF16)])
    (da1,), (r_w2,) = _matmul("d_ff_hidden", dh2b, w2, "nt", 512, 2048, D,
                              lambda acc, hv: (acc * 2.0 * hv.astype(_F32),), [("tile", _MXU)],
                              extras=[(hid, "tile")], comm=[(s_rows, dw2)])
    (dw1,), _ = _matmul("d_w_ff1", n2, da1, "tn", 1024, 1024, 2048, first, [("tile", _BF16)])
    (dh1, dh1b, dgmlp), (r_w1,) = _matmul(
        "d_h1", da1, w1, "nt", 256, D, D_FF, epi_norm_bwd, [("tile", _F32), ("tile", _MXU), ("part8", _F32)],
        extras=[(dh2, "tile"), (h1, "tile"), (g_mlp, "row")], comm=[(s_cols, dw1)])
    (dwout,), _ = _matmul("d_w_out", cat, dh1b, "tn", 1024, D, 2048, first, [("tile", _BF16)])
    (dcat,), (r_wout,) = _matmul("d_cat", dh1b, w_out, "nt", 1024, 1024, D, first, [("tile", _F32)],
                                 comm=[(s_rows, dwout)])
    (dproj, dws, dbs, dgv, dgout, dng, dcw, dcb, ddtb, dalog, dd) = _mixer_bwd(
        proj, dcat, yss, states, *mix_consts, e_mat, et_mat, ltri_mat, seq_chunks)
    pieces = dict(gm_v_norm_g=dgv, gm_out_norm_g=dgout, ssd_norm_g=dng, norm_mlp_g=dgmlp, ple_norm_g=dgple,
                  final_norm_g=dgfin, gm_ws=dws, gm_bs=dbs, ssd_conv_w=dcw, ssd_conv_b=dcb, ssd_dt_bias=ddtb,
                  ssd_a_log=dalog, ssd_d=dd, loss=lossp)
    parts, segments, n_rows, where = _small_layout(pieces)
    small_block = _small_local(parts, segments, n_rows)
    (dwin_t,), (small_blocks,) = _matmul("d_w_in", n1, dproj, "tn", 512, D_IN_PAD, 1024, lambda acc: (acc.T,),
                                         [("tile_t", _BF16)], comm=[(("gather", 0), small_block)])
    dwin_blocks = dwin_t[:D_IN].reshape(N_DEV, SHARD_IN, D)
    (gx, dgmix), (r_win,) = _matmul(
        "d_x", dproj, w_in_t, "nn", 256, D, D_IN_PAD, lambda *a: epi_norm_bwd(*a)[1:],
        [("tile", _F32), ("part8", _F32)], extras=[(dh1, "tile"), (x, "tile"), (g_mix, "row")],
        comm=[(s_rows, dwin_blocks)])
    r_win = r_win.reshape(N_DEV, SHARD_IN, D)

    big = dict(w_in=r_win, w_out=r_wout, w_ff1=r_w1, w_ff2=r_w2, w_ple_gate=r_wg, w_ple_proj=r_wp)
    return gx, big, small_blocks, dgmix, n_rows, where


def _small_layout(pieces):
    rows, segments = [], []
    in_row, out_row = 0, 0

    def add(arr, kind, n_out):
        nonlocal in_row, out_row
        rows.append(arr)
        segments.append((out_row, n_out, in_row, arr.shape[0], kind))
        start = out_row
        in_row += arr.shape[0]
        out_row += n_out
        return start

    where = {_LATE: 0}
    out_row = 1
    for name in _G_VECS[1:]:
        where[name] = add(pieces[name], "sum", 1)
    where["gm_ws"] = add(pieces["gm_ws"].reshape(GM_H * CH * CH // D, D), "copy", GM_H * CH * CH // D)
    where["gm_bs"] = add(pieces["gm_bs"].reshape(1, D), "copy", 1)
    cb = jnp.pad(pieces["ssd_conv_b"], ((0, 0), (0, 2 * D - CONV_CH)))
    where["ssd_conv_b"] = add(cb[:, :D], "sum", 1)
    add(cb[:, D:], "sum", 1)
    cw = jnp.pad(pieces["ssd_conv_w"][:CONV_K], ((0, 0), (0, 2 * D - CONV_CH)))
    where["ssd_conv_w"] = add(cw.reshape(2 * CONV_K, D), "copy", 2 * CONV_K)
    misc = jnp.concatenate([pieces["ssd_dt_bias"], pieces["ssd_a_log"], pieces["ssd_d"],
                            jnp.zeros((8, D - 3 * 128), _F32)], axis=1)
    where["misc"] = add(misc, "sum", 1)
    where["loss"] = add(pieces["loss"], "loss", 1)
    n_rows = -(-out_row // 8) * 8
    return jnp.concatenate(rows, axis=0), tuple(segments), n_rows, where


def _pack_small_params(vals, where, n_rows, my_block):
    rows, at = [], {}

    def add(name, arr):
        at[name] = sum(r.shape[0] for r in rows)
        rows.append(arr)

    for name in _G_VECS:
        add(name, vals[name].reshape(1, D))
    add("gm_ws", vals["gm_ws"].reshape(GM_H * CH * CH // D, D))
    add("gm_bs", vals["gm_bs"].reshape(1, D))
    cb = jnp.pad(vals["ssd_conv_b"].reshape(1, CONV_CH), ((0, 0), (0, 2 * D - CONV_CH)))
    add("ssd_conv_b", cb.reshape(2, D))
    cw = lax.dynamic_update_slice(jnp.zeros((CONV_K, 2 * D), _F32), vals["ssd_conv_w"].reshape(CONV_K, -1),
                                  (0, my_block * (CONV_CH // N_DEV)))
    add("ssd_conv_w", cw.reshape(2 * CONV_K, D))
    misc = jnp.concatenate([_pad_lanes(vals["ssd_dt_bias"].reshape(1, SSD_H)),
                            _pad_lanes(vals["ssd_a_log"].reshape(1, SSD_H)),
                            _pad_lanes(vals["ssd_d"].reshape(1, SSD_H)), jnp.zeros((1, D - 3 * 128), _F32)], axis=1)
    add("misc", misc)
    assert all(where[k] == r for k, r in at.items()), (where, at)
    rows.append(jnp.zeros((n_rows - sum(r.shape[0] for r in rows), D), _F32))
    return jnp.concatenate(rows, axis=0)


def _unpack_small(buf, where, my_block, shapes):
    out = {}
    for name in _G_VECS:
        out[name] = buf[where[name]].reshape(shapes[name])
    n_ws = GM_H * CH * CH // D
    out["gm_ws"] = buf[where["gm_ws"]:where["gm_ws"] + n_ws].reshape(shapes["gm_ws"])
    out["gm_bs"] = buf[where["gm_bs"]].reshape(shapes["gm_bs"])
    r = where["ssd_conv_b"]
    out["ssd_conv_b"] = buf[r:r + 2].reshape(1, 2 * D)[:, :CONV_CH].reshape(shapes["ssd_conv_b"])
    r = where["ssd_conv_w"]
    cw = buf[r:r + 2 * CONV_K].reshape(CONV_K, 2 * D)
    out["ssd_conv_w"] = lax.dynamic_slice(cw, (0, my_block * (CONV_CH // N_DEV)),
                                          (CONV_K, CONV_CH // N_DEV)).reshape(shapes["ssd_conv_w"])
    misc = buf[where["misc"]]
    for i, name in enumerate(("ssd_dt_bias", "ssd_a_log", "ssd_d")):
        out[name] = misc[i * 128:i * 128 + SSD_H].reshape(shapes[name])
    return out


_WEIGHTS = ("norm_mix_g", "w_in", "gm_v_norm_g", "gm_ws", "gm_bs", "gm_out_norm_g", "ssd_conv_w", "ssd_conv_b",
            "ssd_dt_bias", "ssd_a_log", "ssd_d", "ssd_norm_g", "w_out", "norm_mlp_g", "w_ff1", "w_ff2", "ple_norm_g",
            "w_ple_gate", "w_ple_proj", "final_norm_g")


def kernel(x, p, norm_mix_g, w_in, gm_v_norm_g, gm_ws, gm_bs, gm_out_norm_g, ssd_conv_w, ssd_conv_b, ssd_dt_bias, ssd_a_log, ssd_d, ssd_norm_g, w_out, norm_mlp_g, w_ff1, w_ff2, ple_norm_g, w_ple_gate, w_ple_proj, final_norm_g, loss_target, m_norm_mix_g, m_w_in, m_gm_v_norm_g, m_gm_ws, m_gm_bs, m_gm_out_norm_g, m_ssd_conv_w, m_ssd_conv_b, m_ssd_dt_bias, m_ssd_a_log, m_ssd_d, m_ssd_norm_g, m_w_out, m_norm_mlp_g, m_w_ff1, m_w_ff2, m_ple_norm_g, m_w_ple_gate, m_w_ple_proj, m_final_norm_g, v_norm_mix_g, v_w_in, v_gm_v_norm_g, v_gm_ws, v_gm_bs, v_gm_out_norm_g, v_ssd_conv_w, v_ssd_conv_b, v_ssd_dt_bias, v_ssd_a_log, v_ssd_d, v_ssd_norm_g, v_w_out, v_norm_mlp_g, v_w_ff1, v_w_ff2, v_ple_norm_g, v_w_ple_gate, v_w_ple_proj, v_final_norm_g):
    args = dict(locals())
    w = {n: args[n] for n in _WEIGHTS}
    m = {n: args["m_" + n] for n in _WEIGHTS}
    v = {n: args["v_" + n] for n in _WEIGHTS}
    shapes = {n: w[n].shape for n in _WEIGHTS}
    my_block = 4 * lax.axis_index("x") + 2 * lax.axis_index("y") + lax.axis_index("c")
    nb, seq_len, _ = x.shape

    local = lambda d, n: d[n][0].T if n == "w_in" else d[n][0]
    shard = {n: local(w, n).astype(_MXU) for n in _BIG_NAMES}
    small = {n: w[n] for n in _WEIGHTS if n not in _BIG_NAMES}
    gx, recv, small_blocks, late8, n_rows, where = _local_step(
        x.reshape(nb * seq_len, D), p.reshape(nb * seq_len, D_PLE), loss_target.reshape(nb * seq_len, D), shard,
        ssd_conv_w[0], small, seq_len)

    big_out = [{}, {}, {}, {}]
    for n in _BIG_NAMES:
        res = _sum_adam("sum_adam_" + n, recv[n], local(w, n), local(m, n), local(v, n))
        for k in range(4):
            big_out[k][n] = (res[k].T if n == "w_in" else res[k]).reshape(shapes[n])

    packs = [_pack_small_params(d, where, n_rows, my_block) for d in (w, m, v)]
    small_res = _small_final(small_blocks, late8, where[_LATE], *packs)
    loss = small_res[0][where["loss"], 0]
    small_out = [_unpack_small(a, where, my_block, shapes) for a in small_res]

    outs = [loss, gx.reshape(x.shape)]
    for k in range(4):
        outs += [big_out[k][n] if n in _BIG_NAMES else small_out[k][n] for n in _WEIGHTS]
    return tuple(outs)
```

```python
import functools
import math

import jax
import jax.numpy as jnp
import numpy as np
from jax import lax
from jax.experimental import pallas as pl
from jax.experimental.pallas import tpu as pltpu

_F32 = jnp.float32
_BF16 = jnp.bfloat16
_MXU = jnp.bfloat16

D = 1024
D_PLE = 256
GM_W = 1024
GM_H = 8
CH = 128
SSD_W = 1024
SSD_H = 16
SSD_P = 64
SSD_G = 2
SSD_N = 128
CONV_K = 4
CONV_CH = SSD_W + 2 * SSD_G * SSD_N
D_FF = 4096
D_IN = 2 * GM_W + SSD_W + CONV_CH + SSD_H
D_IN_PAD = 4736
DT_BLK = (D_IN_PAD - 128) // 128
EPS = 1e-6
N_DEV = 8
SHARD_IN = D_IN // N_DEV

LR, B1, B2, ADAM_EPS, WD, STEP = 0.001, 0.9, 0.999, 1e-08, 0.01, 10
_LOG2E = math.log2(math.e)
_LOG2_INV_SQRT_2PI = -0.5 * math.log2(2.0 * math.pi)

_V7X_VMEM_BYTES = 64 * 1024 * 1024
_VMEM_CAP = _V7X_VMEM_BYTES - 8 * 1024 * 1024
_MESH = pl.DeviceIdType.MESH


def _vmem_limit(nbytes):
    return int(min(_VMEM_CAP, max(32 * 1024 * 1024, nbytes * 5 // 4 + (4 << 20))))


def _nbytes(shape, dtype):
    return int(np.prod(shape)) * jnp.dtype(dtype).itemsize


def _mx(v):
    return v.astype(_MXU)


def _dot(a, b):
    return jnp.dot(a, b, preferred_element_type=_F32)


def _dot_nt(a, b):
    return lax.dot_general(a, b, (((1,), (1,)), ((), ())), preferred_element_type=_F32)


def _dot_tn(a, b):
    return lax.dot_general(a, b, (((0,), (0,)), ((), ())), preferred_element_type=_F32)


def _split3(a):
    hi = a.astype(_BF16)
    r = a - hi.astype(_F32)
    mid = r.astype(_BF16)
    lo = (r - mid.astype(_F32)).astype(_BF16)
    return hi, mid, lo


def _xdot(dotfn, a, b01):
    b = b01.astype(_BF16)
    hi, mid, lo = _split3(a)
    return (dotfn(hi, b) + dotfn(mid, b)) + dotfn(lo, b)


def _xdot_left(dotfn, a01, b):
    a = a01.astype(_BF16)
    hi, mid, lo = _split3(b)
    return (dotfn(a, hi) + dotfn(a, mid)) + dotfn(a, lo)


def _sum8(v):
    r, n = v.shape
    return v.reshape(r // 8, 8, n).sum(axis=0)


def _sigmoid(v):
    return 1.0 / (1.0 + jnp.exp(-v))


def _rms(xv, g):
    ms = jnp.mean(xv * xv, axis=-1, keepdims=True)
    return xv * lax.rsqrt(ms + EPS) * g


def _rms_bwd(xv, g, dn):
    r = lax.rsqrt(jnp.mean(xv * xv, axis=-1, keepdims=True) + EPS)
    nh = xv * r
    gy = dn * g
    dx = (gy - nh * jnp.mean(gy * nh, axis=-1, keepdims=True)) * r
    return dx, _sum8(dn * nh)


def _iota2(shape, axis):
    return lax.broadcasted_iota(jnp.int32, shape, axis)


def _norm_cast(name, x, g, tm=512, comm=()):
    t, n = x.shape
    tm = min(tm, t)
    steps = t // tm
    kinds = [kind for kind, _ in comm]
    c_in, c_in_specs, c_out_specs, c_out_shape, c_scratch = _comm_io(comm)

    def body(*refs):
        x_ref, g_ref = refs[0], refs[1]
        o_ref = refs[2 + len(comm)]
        comm_refs = (kinds, refs[2:2 + len(comm)], refs[3 + len(comm):3 + 2 * len(comm)], *refs[3 + 2 * len(comm):])
        if comm:
            pl.when(pl.program_id(0) == 0)(lambda: _comm_start(*comm_refs))

        o_ref[...] = _rms(x_ref[...], g_ref[...]).astype(o_ref.dtype)
        if comm:
            pl.when(pl.program_id(0) == steps - 1)(lambda: _comm_finish(*comm_refs))

    res = pl.pallas_call(
        body, name=name, grid=(steps,),
        in_specs=[pl.BlockSpec((tm, n), lambda i: (i, 0)), pl.BlockSpec((1, n), lambda i: (0, 0))] + c_in_specs,
        out_specs=[pl.BlockSpec((tm, n), lambda i: (i, 0))] + c_out_specs,
        out_shape=[jax.ShapeDtypeStruct((t, n), _MXU)] + c_out_shape, scratch_shapes=c_scratch,
        compiler_params=pltpu.CompilerParams(dimension_semantics=("arbitrary",)),
    )(x, g, *c_in)
    return res[0], res[1:]


def _matmul(name, a, b, mode, tm, tn, tk, epilogue, outs, extras=(), comm=(), b_to_epilogue=False):
    m, k = a.shape[::-1] if mode == "tn" else a.shape
    n = b.shape[0] if mode == "nt" else b.shape[1]
    tm, tn, tk = min(tm, m), min(tn, n), min(tk, k)
    assert m % tm == 0 and n % tn == 0 and k % tk == 0, (name, m, n, k, tm, tn, tk)
    if mode == "nn":
        a_spec = pl.BlockSpec((tm, tk), lambda i, j, kk: (i, kk))
        b_spec = pl.BlockSpec((tk, tn), lambda i, j, kk: (kk, j))
        dotfn = _dot
    elif mode == "nt":
        a_spec = pl.BlockSpec((tm, tk), lambda i, j, kk: (i, kk))
        b_spec = pl.BlockSpec((tn, tk), lambda i, j, kk: (j, kk))
        dotfn = _dot_nt
    else:
        a_spec = pl.BlockSpec((tk, tm), lambda i, j, kk: (kk, i))
        b_spec = pl.BlockSpec((tk, tn), lambda i, j, kk: (kk, j))
        dotfn = _dot_tn
    ni, nj, nk = m // tm, n // tn, k // tk
    n_ex, n_out, n_comm = len(extras), len(outs), len(comm)
    kinds = [kind for kind, _ in comm]
    c_in, c_in_specs, c_out_specs, c_out_shape, c_scratch = _comm_io(comm)

    in_specs, vmem = [a_spec, b_spec], 2 * (tm * tk * a.dtype.itemsize + tk * tn * b.dtype.itemsize)
    for arr, kind in extras:
        if kind == "tile":
            in_specs.append(pl.BlockSpec((tm, tn), lambda i, j, kk: (i, j)))
            vmem += 2 * _nbytes((tm, tn), arr.dtype)
        else:
            in_specs.append(pl.BlockSpec((1, tn), lambda i, j, kk: (0, j)))
    out_specs, out_shape = [], []
    for kind, dt in outs:
        if kind == "tile":
            out_specs.append(pl.BlockSpec((tm, tn), lambda i, j, kk: (i, j)))
            out_shape.append(jax.ShapeDtypeStruct((m, n), dt))
            vmem += 2 * _nbytes((tm, tn), dt)
        elif kind == "tile_t":
            out_specs.append(pl.BlockSpec((tn, tm), lambda i, j, kk: (j, i)))
            out_shape.append(jax.ShapeDtypeStruct((n, m), dt))
            vmem += 2 * _nbytes((tm, tn), dt)
        else:
            assert nj == 1, "the partial-sum rows are accumulated over consecutive row tiles"
            out_specs.append(pl.BlockSpec((8, tn), lambda i, j, kk: (0, 0)))
            out_shape.append(jax.ShapeDtypeStruct((8, n), dt))
    scratch = [pltpu.VMEM((tm, tn), _F32)] if nk > 1 else []
    vmem += _nbytes((tm, tn), _F32) * 2

    def body(*refs):
        a_ref, b_ref = refs[0], refs[1]
        ex_refs = refs[2:2 + n_ex]
        n_in = 2 + n_ex + n_comm
        out_refs = refs[n_in:n_in + n_out]
        i, j, kk = pl.program_id(0), pl.program_id(1), pl.program_id(2)
        comm_refs = (kinds, refs[2 + n_ex:n_in], refs[n_in + n_out:n_in + n_out + n_comm], *refs[len(refs) - 3:])
        if n_comm:
            pl.when((i == 0) & (j == 0) & (kk == 0))(lambda: _comm_start(*comm_refs))

        b_val = _mx(b_ref[...])
        part = dotfn(_mx(a_ref[...]), b_val)

        def finish(acc):
            vals = epilogue(acc, *([b_val] if b_to_epilogue else []), *[r[...] for r in ex_refs])
            for r, v, (kind, _) in zip(out_refs, vals, outs):
                if kind == "part8":
                    @pl.when(i == 0)
                    def _():
                        r[...] = v

                    @pl.when(i > 0)
                    def _():
                        r[...] += v
                else:
                    r[...] = v.astype(r.dtype)

        if nk == 1:
            finish(part)
        else:
            acc_ref = refs[n_in + n_out + n_comm]

            @pl.when(kk == 0)
            def _():
                acc_ref[...] = part

            @pl.when(kk > 0)
            def _():
                acc_ref[...] += part

            @pl.when(kk == nk - 1)
            def _():
                finish(acc_ref[...])

        if n_comm:
            pl.when((i == ni - 1) & (j == nj - 1) & (kk == nk - 1))(lambda: _comm_finish(*comm_refs))

    carried =n_comm or any(kind == "part8" for kind, _ in outs)
    sem = ("arbitrary",) * 3 if carried else ("parallel", "parallel", "arbitrary")
    res = pl.pallas_call(
        body, name=name, grid=(ni, nj, nk),
        in_specs=in_specs + c_in_specs, out_specs=out_specs + c_out_specs, out_shape=out_shape + c_out_shape,
        scratch_shapes=scratch + c_scratch,
        compiler_params=pltpu.CompilerParams(dimension_semantics=sem, vmem_limit_bytes=_vmem_limit(vmem)),
    )(a, b, *[arr for arr, _ in extras], *c_in)
    return res[:n_out], res[n_out:]


def _shift_down(v, halo8, j):
    if j == 0:
        return v
    r = pltpu.roll(v, j, axis=0)
    hr = pltpu.roll(halo8, j, axis=0)
    top = jnp.where(_iota2(hr.shape, 0) < j, hr, r[:8])
    return jnp.concatenate([top, r[8:]], axis=0)


def _shift_up(v, next8, j):
    if j == 0:
        return v
    rows = v.shape[0]
    r = pltpu.roll(v, rows - j, axis=0)
    nr = pltpu.roll(next8, 8 - j, axis=0)
    bot = jnp.where(_iota2(nr.shape, 0) >= 8 - j, nr, r[rows - 8:])
    return jnp.concatenate([r[:rows - 8], bot], axis=0)


def _silu_grad(sig, silu):
    return sig + silu * (1.0 - sig)


def _gmlp_fwd_vals(pu, pv, gv, ws_ref, bsb_ref, want_bwd):
    tril = _iota2((CH, CH), 0) >= _iota2((CH, CH), 1)
    cdf_u = 0.5 * (1.0 + lax.erf(pu * 0.7071067811865476))
    cdf_v = 0.5 * (1.0 + lax.erf(pv * 0.7071067811865476))
    u = pu * cdf_u
    v = pv * cdf_v
    ys, keep = [], [(cdf_u, cdf_v)] if want_bwd else []
    for h in range(GM_H):
        sl = slice(h * 128, (h + 1) * 128)
        vh = v[:, sl]
        r = lax.rsqrt(jnp.mean(vh * vh, axis=-1, keepdims=True) + EPS)
        vn = vh * r * gv[:, sl]
        wm = _mx(jnp.where(tril, ws_ref[h], 0.0))
        mixed = _dot(wm, _mx(vn)) + bsb_ref[h]
        ys.append(u[:, sl] * mixed)
        if want_bwd:
            keep.append((vh, r, vn, wm, mixed))
    return jnp.concatenate(ys, axis=1), u, keep


def _ssd_common(xbc, halo8, dtraw, cw_ref, cb, dtb, alog, e_ref, ltri):
    xs = [_shift_down(xbc, halo8, j) for j in range(CONV_K)]
    cpre = cb + sum(cw_ref[k:k + 1, :] * xs[CONV_K - 1 - k] for k in range(CONV_K))
    sig = _sigmoid(cpre)
    act = cpre * sig
    dtin = dtraw + dtb
    dt = jnp.maximum(dtin, 0.0) + jnp.log(1.0 + jnp.exp(-jnp.abs(dtin)))
    a_neg = -jnp.exp(alog)
    cs = _xdot_left(_dot, ltri, dt * a_neg)
    cs_last = cs[CH - 1:CH, :]
    ecs = jnp.exp(cs)
    dec = jnp.exp(cs_last - cs)
    cdec = jnp.exp(cs_last)
    e = e_ref[...]
    dt_x = _dot(dt.astype(_BF16), e)
    ecs_x = _dot(ecs.astype(_BF16), e)
    dec_x = _dot(dec.astype(_BF16), e)
    cdec_x = _xdot(_dot, jnp.broadcast_to(cdec, (8, 128)), e)[0:1, :]
    return dict(xs=xs, sig=sig, act=act, dtin=dtin, dt=dt, a_neg=a_neg, cs=cs, ecs=ecs, dec=dec, cdec=cdec,
                dt_x=dt_x, ecs_x=ecs_x, dec_x=dec_x, cdec_x=cdec_x)


def _head_lm(cs, cst_ref, h, tril):
    seg = jnp.broadcast_to(cs[:, h:h + 1], (CH, CH)) - cst_ref[h:h + 1, :]
    return jnp.exp(jnp.where(tril, seg, -jnp.inf))


def _mixer_fwd(proj, gv, ws, bsb, gout, cw8, cb, dtb, alog, d_x, ng, e_mat, ltri_mat, seq_chunks, comm=()):
    t = proj.shape[0]
    n_chunks = t // CH
    n_comm = len(comm)
    kinds = [kind for kind, _ in comm]
    c_in, c_in_specs, c_out_specs, c_out_shape, c_scratch = _comm_io(comm)

    def body(*refs):
        (pu_ref, pv_ref, z_ref, xbc_ref, dt_ref, halo_ref, gv_ref, ws_ref, bsb_ref, gout_ref, cw_ref, cb_ref,
         dtb_ref, alog_ref, dx_ref, ng_ref, e_ref, ltri_ref) = refs[:18]
        cat_ref, y_ref, st_ref = refs[18 + n_comm:21 + n_comm]
        s_ref, cst_ref = refs[21 + 2 * n_comm:23 + 2 * n_comm]
        comm_refs = (kinds, refs[18:18 + n_comm], refs[21 + n_comm:21 + 2 * n_comm], *refs[23 + 2 * n_comm:])
        c = pl.program_id(0)
        if n_comm:
            pl.when(c == 0)(lambda: _comm_start(*comm_refs))
            pl.when(c == n_chunks - 1)(lambda: _comm_finish(*comm_refs))

        first = (c % seq_chunks) == 0
        tril = _iota2((CH, CH), 0) >= _iota2((CH, CH), 1)
        lane = _iota2((CH, 128), 1)

        y_a, _, _ = _gmlp_fwd_vals(pu_ref[...], pv_ref[...], gv_ref[...], ws_ref, bsb_ref, False)
        cat_ref[:, 0:GM_W] = _rms(y_a, gout_ref[...]).astype(cat_ref.dtype)

        @pl.when(first)
        def _():
            s_ref[...] = jnp.zeros_like(s_ref)

        halo8 = jnp.where(first, 0.0, halo_ref[...])
        q = _ssd_common(xbc_ref[...], halo8, dt_ref[...], cw_ref, cb_ref[...], dtb_ref[...], alog_ref[...], e_ref,
                        ltri_ref[...])
        act = q["act"]
        xv = act[:, 0:SSD_W]
        xdt = xv * q["dt_x"]
        xdt_m = _mx(xdt)
        cs = q["cs"]
        cst_ref[...] = cs.T
        s_prev = s_ref[...]
        st_ref[...] = s_prev
        ys = []
        for g in range(SSD_G):
            bg = _mx(act[:, SSD_W + g * SSD_N:SSD_W + (g + 1) * SSD_N])
            cg = _mx(act[:, SSD_W + SSD_G * SSD_N + g * SSD_N:SSD_W + SSD_G * SSD_N + (g + 1) * SSD_N])
            cbm = _dot_nt(cg, bg)
            gs = slice(g * 512, (g + 1) * 512)
            for pr in range(4):
                ps = slice(g * 512 + pr * 128, g * 512 + (pr + 1) * 128)
                o = []
                for hh in range(2):
                    h = g * 8 + pr * 2 + hh
                    m_h = _mx(cbm * _head_lm(cs, cst_ref, h, tril))
                    o.append(_dot(m_h, xdt_m[:, ps]))
                ys.append(jnp.where(lane < SSD_P, o[0], o[1]))
            sg = s_prev[:, gs]
            yoff = _dot(cg, _mx(sg)) * q["ecs_x"][:, gs]
            ys[-4:] = [ys[-4 + i] + yoff[:, i * 128:(i + 1) * 128] for i in range(4)]
            st_new = _dot_tn(bg, _mx(q["dec_x"][:, gs] * xdt[:, gs]))
            s_ref[:, gs] = sg * q["cdec_x"][:, gs] + st_new
        y = jnp.concatenate(ys, axis=1) + dx_ref[...] * xv
        y_ref[...] = y
        zv = z_ref[...]
        yg = y * (zv * _sigmoid(zv))
        for g in range(SSD_G):
            gs = slice(g * 512, (g + 1) * 512)
            cat_ref[:, GM_W + g * 512:GM_W + (g + 1) * 512] = _rms(yg[:, gs], ng_ref[:, gs]).astype(cat_ref.dtype)

    blk = lambda w, j: pl.BlockSpec((CH, w), lambda c: (c, j))
    full = lambda arr: pl.BlockSpec(arr.shape, lambda c: (0,) * arr.ndim)
    consts = [gv, ws, bsb, gout, cw8, cb, dtb, alog, d_x, ng, e_mat, ltri_mat]
    res = pl.pallas_call(
        body, name="mixer_fwd", grid=(n_chunks,),
        in_specs=[blk(GM_W, 0), blk(GM_W, 1), blk(SSD_W, 2), blk(CONV_CH, 2), blk(128, DT_BLK),
                  pl.BlockSpec((8, CONV_CH), lambda c: (jnp.maximum(c * (CH // 8) - 1, 0), 2))]
        + [full(a) for a in consts] + c_in_specs,
        out_specs=[pl.BlockSpec((CH, 2 * D), lambda c: (c, 0)), pl.BlockSpec((CH, SSD_W), lambda c: (c, 0)),
                   pl.BlockSpec((CH, SSD_W), lambda c: (c, 0))] + c_out_specs,
        out_shape=[jax.ShapeDtypeStruct((t, 2 * D), _MXU), jax.ShapeDtypeStruct((t, SSD_W), _F32),
                   jax.ShapeDtypeStruct((t, SSD_W), _F32)] + c_out_shape,
        scratch_shapes=[pltpu.VMEM((SSD_N, SSD_W), _F32), pltpu.VMEM((128, CH), _F32)] + c_scratch,
        compiler_params=pltpu.CompilerParams(dimension_semantics=("arbitrary",), vmem_limit_bytes=48 << 20),
    )(proj, proj, proj, proj, proj, proj, *consts, *c_in)
    return res[:3], res[3:]


def _mixer_bwd(proj, dcat, yss, states, gv, ws, bsb, gout, cw8, cb, dtb, alog, d_x, ng, e_mat, et_mat, ltri_mat,
               seq_chunks):
    t = proj.shape[0]
    n_chunks = t // CH

    def body(pu_ref, pv_ref, z_ref, xbc_ref, dt_ref, halo_ref, dcat_ref, y_ref, st_ref,
             gv_ref, ws_ref, bsb_ref, gout_ref, cw_ref, cb_ref, dtb_ref, alog_ref, dx_ref, ng_ref,
             e_ref, et_ref, ltri_ref,
             dproj_ref, dws_ref, dbs_ref, dgv_ref, dgout_ref, dng_ref, dcw_ref, dcb_ref, ddtb_ref, dalog_ref, dd_ref,
             ds_ref, dnext_ref, dcst_ref, cst_ref, dbacc_ref, ddacc_ref):
        i = pl.program_id(0)
        c = n_chunks - 1 - i
        first = (c % seq_chunks) == 0
        last_in_seq = (c % seq_chunks) == seq_chunks - 1
        tril = _iota2((CH, CH), 0) >= _iota2((CH, CH), 1)
        lane = _iota2((CH, 128), 1)
        row = _iota2((CH, 128), 0)

        @pl.when(i == 0)
        def _():
            for r in (dws_ref, dbs_ref, dgv_ref, dgout_ref, dng_ref, dcw_ref, dcb_ref, ddtb_ref, dalog_ref, dd_ref,
                      dbacc_ref, ddacc_ref, dcst_ref):
                r[...] = jnp.zeros_like(r)

        @pl.when(last_in_seq)
        def _():
            ds_ref[...] = jnp.zeros_like(ds_ref)
            dnext_ref[...] = jnp.zeros_like(dnext_ref)

        dcat_v = dcat_ref[...].astype(_F32)

        pu, pv = pu_ref[...], pv_ref[...]
        gv_v = gv_ref[...]
        y_a, u, keep = _gmlp_fwd_vals(pu, pv, gv_v, ws_ref, bsb_ref, True)
        dy, dgout8 = _rms_bwd(y_a, gout_ref[...], dcat_v[:, 0:GM_W])
        dgout_ref[...] += dgout8
        dus, dvs, dgvs = [], [], []
        for h in range(GM_H):
            sl = slice(h * 128, (h + 1) * 128)
            vh, r, vn, wm, mixed = keep[h + 1]
            dyh = dy[:, sl]
            dus.append(dyh * mixed)
            dmix = dyh * u[:, sl]
            dmix_m = _mx(dmix)
            dws_ref[h] += jnp.where(tril, _dot_nt(dmix_m, _mx(vn)), 0.0)
            dbacc_ref[h] += dmix
            dvn = _dot_tn(wm, dmix_m)
            gy = dvn * gv_v[:, sl]
            nh = vh * r
            dvs.append((gy - nh * jnp.mean(gy * nh, axis=-1, keepdims=True)) * r)
            dgvs.append(_sum8(dvn * nh))
        dgv_ref[...] += jnp.concatenate(dgvs, axis=1)
        cdf_u, cdf_v = keep[0]
        gelu_grad = lambda pre, cdf: cdf + pre * jnp.exp2(pre * pre * (-0.5 * _LOG2E) + _LOG2_INV_SQRT_2PI)
        dproj_ref[:, 0:GM_W] = (jnp.concatenate(dus, axis=1) * gelu_grad(pu, cdf_u)).astype(dproj_ref.dtype)
        dproj_ref[:, GM_W:2 * GM_W] = (jnp.concatenate(dvs, axis=1) * gelu_grad(pv, cdf_v)).astype(dproj_ref.dtype)

        halo8 = jnp.where(first, 0.0, halo_ref[...])
        q = _ssd_common(xbc_ref[...], halo8, dt_ref[...], cw_ref, cb_ref[...], dtb_ref[...], alog_ref[...], e_ref,
                        ltri_ref[...])
        act = q["act"]
        xv = act[:, 0:SSD_W]
        dt_x, ecs_x, dec_x, cdec_x = q["dt_x"], q["ecs_x"], q["dec_x"], q["cdec_x"]
        xdt = xv * dt_x
        xdt_m = _mx(xdt)
        cs = q["cs"]
        cst_ref[...] = cs.T
        s_prev = st_ref[...]
        ds = ds_ref[...]
        yv = y_ref[...]
        zv = z_ref[...]
        sig_z = _sigmoid(zv)
        sz = zv * sig_z
        yg = yv * sz
        dygs, dng8 = [], []
        for g in range(SSD_G):
            gs = slice(g * 512, (g + 1) * 512)
            a_, b_ = _rms_bwd(yg[:, gs], ng_ref[:, gs], dcat_v[:, GM_W + g * 512:GM_W + (g + 1) * 512])
            dygs.append(a_)
            dng8.append(b_)
        dyg = jnp.concatenate(dygs, axis=1)
        dng_ref[...] += jnp.concatenate(dng8, axis=1)
        dyv = dyg * sz
        dproj_ref[:, 2 * GM_W:2 * GM_W + SSD_W] = (dyg * yv * _silu_grad(sig_z, sz)).astype(dproj_ref.dtype)
        ddacc_ref[...] += _sum8(dyv * xv)
        dyv_m = _mx(dyv)

        dxdt_parts, db_parts, dc_parts = [], [], []
        dcs = jnp.zeros((CH, 128), _F32)
        dcs_x_parts, ddec_x_parts, dcl_x_parts = [], [], []
        for g in range(SSD_G):
            gs = slice(g * 512, (g + 1) * 512)
            bg = _mx(act[:, SSD_W + g * SSD_N:SSD_W + (g + 1) * SSD_N])
            cg = _mx(act[:, SSD_W + SSD_G * SSD_N + g * SSD_N:SSD_W + SSD_G * SSD_N + (g + 1) * SSD_N])
            cbm = _dot_nt(cg, bg)
            sg = s_prev[:, gs]
            sg_m = _mx(sg)
            dsg = ds[:, gs]
            dsg_m = _mx(dsg)
            zoff = _dot(cg, sg_m)
            dz_off = dyv[:, gs] * ecs_x[:, gs]
            dz_off_m = _mx(dz_off)
            dcs_x_parts.append(dyv[:, gs] * zoff * ecs_x[:, gs])
            dcg = _dot_nt(dz_off_m, sg_m)
            dsprev = _dot_tn(cg, dz_off_m)
            w_st = dec_x[:, gs] * xdt[:, gs]
            dw_st = _dot(bg, dsg_m)
            dbg = _dot_nt(_mx(w_st), dsg_m)
            dxdt_g = dec_x[:, gs] * dw_st
            ddec_x_parts.append(dw_st * xdt[:, gs])
            dsprev = dsprev + cdec_x[:, gs] * dsg
            dcl_x_parts.append(jnp.sum(dsg * sg, axis=0, keepdims=True) * cdec_x[:, gs])
            ds_ref[:, gs] = dsprev
            dcb = jnp.zeros((CH, CH), _F32)
            dxdt_pairs = []
            for pr in range(4):
                ps = slice(g * 512 + pr * 128, g * 512 + (pr + 1) * 128)
                acc_pair = None
                for hh in range(2):
                    h = g * 8 + pr * 2 + hh
                    in_head = (lane < SSD_P) if hh == 0 else (lane >= SSD_P)
                    lm = _head_lm(cs, cst_ref, h, tril)
                    m_h = cbm * lm
                    m_hm = _mx(m_h)
                    dyh_m = _mx(jnp.where(in_head, dyv[:, ps], 0.0))
                    dm = _dot_nt(dyh_m, xdt_m[:, ps])
                    dcb = dcb + dm * lm
                    qm = dm * m_h
                    dcs = dcs + jnp.where(lane == h, jnp.sum(qm, axis=1, keepdims=True), 0.0)
                    dcst_ref[h:h + 1, :] = jnp.sum(qm, axis=0, keepdims=True)
                    contrib = jnp.where(in_head, _dot_tn(m_hm, dyv_m[:, ps]), 0.0)
                    acc_pair = contrib if acc_pair is None else acc_pair + contrib
                dxdt_pairs.append(acc_pair)
            dxdt_parts.append(dxdt_g + jnp.concatenate(dxdt_pairs, axis=1))
            dcb_m = _mx(dcb)
            dc_parts.append(dcg + _dot(dcb_m, bg))
            db_parts.append(dbg + _dot_tn(dcb_m, cg))
        dxdt = jnp.concatenate(dxdt_parts, axis=1)
        dxv = dx_ref[...] * dyv + dxdt * dt_x
        et = et_ref[...]
        head_sum = lambda v: _dot(v.astype(_BF16), et)
        ddt = head_sum(dxdt * xv)
        dcs = dcs - dcst_ref[...].T + head_sum(jnp.concatenate(dcs_x_parts, axis=1))
        ddec = head_sum(jnp.concatenate(ddec_x_parts, axis=1)) * q["dec"]
        dcs = dcs - ddec
        dcl = jnp.sum(ddec, axis=0, keepdims=True) + _xdot(
            _dot, jnp.broadcast_to(jnp.concatenate(dcl_x_parts, axis=1), (8, SSD_W)), et)[0:1, :]
        dcs = jnp.where(row == CH - 1, dcs + dcl, dcs)
        da = _xdot_left(_dot_tn, ltri_ref[...], dcs)
        ddt = ddt + da * q["a_neg"]
        dalog_ref[...] += _sum8(da * q["dt"] * q["a_neg"])
        ddtraw = jnp.where(lane < SSD_H, ddt * _sigmoid(q["dtin"]), 0.0)
        ddtb_ref[...] += _sum8(ddtraw)
        dproj_ref[:, D_IN_PAD - 128:D_IN_PAD] = ddtraw.astype(dproj_ref.dtype)
        dcpre = jnp.concatenate([dxv] + db_parts + dc_parts, axis=1) * _silu_grad(q["sig"], act)
        dcb_ref[...] += _sum8(dcpre)
        for k in range(CONV_K):
            dcw_ref[k:k + 1, :] += jnp.sum(dcpre * q["xs"][CONV_K - 1 - k], axis=0, keepdims=True)
        next8 = dnext_ref[...]
        dxbc = sum(cw_ref[k:k + 1, :] * _shift_up(dcpre, next8, CONV_K - 1 - k) for k in range(CONV_K))
        dproj_ref[:, 2 * GM_W + SSD_W:2 * GM_W + SSD_W + CONV_CH] = dxbc.astype(dproj_ref.dtype)
        dnext_ref[...] = dcpre[0:8, :]

        @pl.when(i == n_chunks - 1)
        def _():
            for h in range(GM_H):
                dbs_ref[h:h + 1, :] = _xdot_left(_dot_nt, jnp.ones((8, 128), _BF16), dbacc_ref[h])[0:1, :]
            dd_ref[...] = _xdot(_dot, ddacc_ref[...], et)

    rblk = lambda w, j: pl.BlockSpec((CH, w), lambda i: (n_chunks - 1 - i, j))
    full = lambda arr: pl.BlockSpec(arr.shape, lambda i: (0,) * arr.ndim)
    acc = lambda shape: pl.BlockSpec(shape, lambda i: (0,) * len(shape))
    consts = [gv, ws, bsb, gout, cw8, cb, dtb, alog, d_x, ng, e_mat, et_mat, ltri_mat]
    acc_shapes = [(GM_H, CH, CH), (8, 128), (8, GM_W), (8, GM_W), (8, SSD_W), (8, CONV_CH), (8, CONV_CH), (8, 128),
                  (8, 128), (8, 128)]
    return pl.pallas_call(
        body, name="mixer_bwd", grid=(n_chunks,),
        in_specs=[rblk(GM_W, 0), rblk(GM_W, 1), rblk(SSD_W, 2), rblk(CONV_CH, 2), rblk(128, DT_BLK),
                  pl.BlockSpec((8, CONV_CH), lambda i: (jnp.maximum((n_chunks - 1 - i) * (CH // 8) - 1, 0), 2)),
                  rblk(2 * D, 0), rblk(SSD_W, 0), rblk(SSD_W, 0)] + [full(a) for a in consts],
        out_specs=[rblk(D_IN_PAD, 0)] + [acc(s) for s in acc_shapes],
        out_shape=[jax.ShapeDtypeStruct((t, D_IN_PAD), _MXU)] + [jax.ShapeDtypeStruct(s, _F32) for s in acc_shapes],
        scratch_shapes=[pltpu.VMEM((SSD_N, SSD_W), _F32), pltpu.VMEM((8, CONV_CH), _F32),
                        pltpu.VMEM((128, CH), _F32), pltpu.VMEM((128, CH), _F32),
                        pltpu.VMEM((GM_H, CH, 128), _F32), pltpu.VMEM((8, SSD_W), _F32)],
        compiler_params=pltpu.CompilerParams(dimension_semantics=("arbitrary",), vmem_limit_bytes=48 << 20),
    )(proj, proj, proj, proj, proj, proj, dcat, yss, states, *consts)


def _peers():
    x, y, c = lax.axis_index("x"), lax.axis_index("y"), lax.axis_index("c")
    out = []
    for k in range(1, N_DEV):
        fx, fy, fc = (k >> 2) & 1, (k >> 1) & 1, k & 1
        px, py, pc = (x + fx) % 2, (y + fy) % 2, (c + fc) % 2
        out.append((k - 1, (px, py, pc), 4 * px + 2 * py + pc))
    return out, 4 * x + 2 * y + c


def _comm_io(comm):
    any_spec = pl.BlockSpec(memory_space=pl.ANY)
    n = len(comm)
    out_shape = []
    for (kind, axis), src in comm:
        shp = list(src.shape)
        if kind in ("gather", "gather2"):
            shp[axis] *= N_DEV
        else:
            shp[axis] //= N_DEV
            shp = [N_DEV] + shp
        out_shape.append(jax.ShapeDtypeStruct(tuple(shp), src.dtype))
    scratch = [pltpu.SemaphoreType.DMA((n * (N_DEV - 1),)), pltpu.SemaphoreType.DMA((n * (N_DEV - 1),)),
               pltpu.SemaphoreType.DMA((n,))] if n else []
    return [src for _, src in comm], [any_spec] * n, [any_spec] * n, out_shape, scratch


def _window(ref, axis, idx, size):
    start = pl.multiple_of(idx * size, size)
    return ref.at[tuple(pl.ds(start, size) if a == axis else slice(None) for a in range(len(ref.shape)))]


def _comm_plans(kinds, src_refs, dst_refs, send_sems, recv_sems, local_sems):
    x, y, c = lax.axis_index("x"), lax.axis_index("y"), lax.axis_index("c")
    peers, me = _peers()
    plans = []
    for s, ((kind, axis), src, dst) in enumerate(zip(kinds, src_refs, dst_refs)):
        sems = lambda k: dict(send_sem=send_sems.at[s * (N_DEV - 1) + k], recv_sem=recv_sems.at[s * (N_DEV - 1) + k])
        remote = lambda src_ref, dst_ref, k, pid: pltpu.make_async_remote_copy(
            src_ref=src_ref, dst_ref=dst_ref, device_id=pid, device_id_type=_MESH, **sems(k))
        if kind == "gather2":
            size = src.shape[axis]
            win = lambda idx: _window(dst, axis, idx, size)
            sib, sib_idx = (x, y, 1 - c), 4 * x + 2 * y + (1 - c)
            local = pltpu.make_async_copy(src, win(me), local_sems.at[s])
            to_sib = remote(src, win(me), 0, sib)
            starts, forwards = [local, to_sib], []
            waits = [(local, "local"), (to_sib, "send"), (remote(src, win(sib_idx), 0, sib), "recv")]
            for j, (fx, fy) in enumerate(((1, 0), (0, 1), (1, 1))):
                px, py = (x + fx) % 2, (y + fy) % 2
                same, other = 4 * px + 2 * py + c, 4 * px + 2 * py + (1 - c)
                out = remote(src, win(me), 1 + j, (px, py, c))
                starts.append(out)
                passed = remote(win(same), win(same), 4 + j, sib)
                forwards.append((remote(src, win(same), 1 + j, (px, py, c)), passed))
                waits += [(out, "send"), (passed, "send"), (remote(win(other), win(other), 4 + j, sib), "recv")]
            plans.append((starts, forwards, waits))
            continue
        if kind == "gather":
            size = src.shape[axis]
            src_for = lambda pidx: src
            dst_mine = _window(dst, axis, me, size)
        else:
            size = src.shape[axis] // N_DEV
            src_for = lambda pidx: _window(src, axis, pidx, size)
            dst_mine = dst.at[me]
        local = pltpu.make_async_copy(src_for(me), dst_mine, local_sems.at[s])
        remotes = [remote(src_for(pidx), dst_mine, k, pid) for k, pid, pidx in peers]
        plans.append(([local] + remotes, [], [(local, "local")] + [(cp, "both") for cp in remotes]))
    return plans


def _comm_start(*refs):
    for starts, _, _ in _comm_plans(*refs):
        for cp in starts:
            cp.start()


def _comm_finish(*refs):
    for _, forwards, waits in _comm_plans(*refs):
        for arrival, cp in forwards:
            arrival.wait_recv()
            cp.start()
        for cp, what in waits:
            if what == "send":
                cp.wait_send()
            elif what == "recv":
                cp.wait_recv()
            else:
                cp.wait()


def _adam_vals(w, g, m, v):
    m = B1 * m + (1.0 - B1) * g
    v = B2 * v + (1.0 - B2) * (g * g)
    m_hat = m / (1.0 - B1 ** STEP)
    v_hat = v / (1.0 - B2 ** STEP)
    delta = -LR * (m_hat / (jnp.sqrt(v_hat) + ADAM_EPS) + WD * w)
    return delta, m, v


def _sum_adam(name, recv, w, m, v, tile=256):
    _, r, wd = recv.shape
    if r % min(tile, r) == 0:
        tr, tc = min(tile, r), wd
    else:
        tr, tc = r, tile
        assert wd % tc == 0, (name, r, wd)

    def body(recv_ref, w_ref, m_ref, v_ref, g_out, d_out, m_out, v_out):
        g = recv_ref[0].astype(_F32)
        for s in range(1, N_DEV):
            g = g + recv_ref[s].astype(_F32)
        d_, m_, v_ = _adam_vals(w_ref[...], g, m_ref[...], v_ref[...])
        g_out[...] = g
        d_out[...] = d_
        m_out[...] = m_
        v_out[...] = v_

    spec = pl.BlockSpec((tr, tc), lambda i, j: (i, j))
    return pl.pallas_call(
        body, name=name, grid=(r // tr, wd // tc),
        in_specs=[pl.BlockSpec((N_DEV, tr, tc), lambda i, j: (0, i, j)), spec, spec, spec],
        out_specs=[spec] * 4, out_shape=[jax.ShapeDtypeStruct((r, wd), _F32)] * 4,
        compiler_params=pltpu.CompilerParams(dimension_semantics=("parallel", "parallel"),
                                             vmem_limit_bytes=48 << 20),
    )(recv, w, m, v)


def _small_local(parts, segments, n_rows):
    def body(parts_ref, loc_ref):
        loc_ref[...] = jnp.zeros_like(loc_ref)
        for out_row, n_out, in_row, n_in, kind in segments:
            if kind == "copy":
                loc_ref[out_row:out_row + n_out, :] = parts_ref[in_row:in_row + n_in, :]
            else:
                s = jnp.sum(parts_ref[in_row:in_row + n_in, :], axis=0, keepdims=True)
                if kind == "loss":
                    s = jnp.broadcast_to(jnp.sum(s, axis=1, keepdims=True) * (0.5 / D), (1, D))
                loc_ref[out_row:out_row + 1, :] = s

    vm = pl.BlockSpec(memory_space=pltpu.VMEM)
    return pl.pallas_call(body, name="small_local", in_specs=[vm], out_specs=vm,
                          out_shape=jax.ShapeDtypeStruct((n_rows, D), _F32))(parts)


def _small_final(blocks, late8, late_row, w, m, v):
    n_rows = w.shape[0]

    def body(blocks_ref, late_ref, w_ref, m_ref, v_ref, g_out, d_out, m_out, v_out, loc_ref, recv_ref, send_sems,
             recv_sems):
        peers, me = _peers()
        loc_ref[...] = jnp.broadcast_to(jnp.sum(late_ref[...], axis=0, keepdims=True), (8, D))
        recv_ref[me] = loc_ref[...]
        copies = [pltpu.make_async_remote_copy(src_ref=loc_ref, dst_ref=recv_ref.at[me], send_sem=send_sems.at[k],
                                               recv_sem=recv_sems.at[k], device_id=pid, device_id_type=_MESH)
                  for k, pid, _ in peers]
        for cp in copies:
            cp.start()
        g = blocks_ref[0:n_rows, :]
        for s in range(1, N_DEV):
            g = g + blocks_ref[s * n_rows:(s + 1) * n_rows, :]
        for cp in copies:
            cp.wait()
        late = recv_ref[0]
        for s in range(1, N_DEV):
            late = late + recv_ref[s]
        g = jnp.where(_iota2((n_rows, D), 0) == late_row, jnp.broadcast_to(late[0:1, :], (n_rows, D)), g)
        d_, m_, v_ = _adam_vals(w_ref[...], g, m_ref[...], v_ref[...])
        g_out[...] = g
        d_out[...] = d_
        m_out[...] = m_
        v_out[...] = v_

    vm = pl.BlockSpec(memory_space=pltpu.VMEM)
    return pl.pallas_call(
        body, name="small_final", in_specs=[vm] * 5, out_specs=[vm] * 4,
        out_shape=[jax.ShapeDtypeStruct((n_rows, D), _F32)] * 4,
        scratch_shapes=[pltpu.VMEM((8, D), _F32), pltpu.VMEM((N_DEV, 8, D), _F32),
                        pltpu.SemaphoreType.DMA((N_DEV - 1,)), pltpu.SemaphoreType.DMA((N_DEV - 1,))],
        compiler_params=pltpu.CompilerParams(vmem_limit_bytes=48 << 20),
    )(blocks, late8, w, m, v)


_BIG_NAMES =("w_in", "w_out", "w_ff1", "w_ff2", "w_ple_gate", "w_ple_proj")

_G_VECS = ("norm_mix_g", "gm_v_norm_g", "gm_out_norm_g", "ssd_norm_g", "norm_mlp_g", "ple_norm_g", "final_norm_g")
_LATE = _G_VECS[0]


def _const_mats():
    h = np.arange(128)[:, None]
    ch = np.arange(SSD_W)[None, :]
    e = (ch // SSD_P == h).astype(np.float32)
    ltri = (np.arange(CH)[:, None] >= np.arange(CH)[None, :]).astype(np.float32)
    return jnp.asarray(e, _BF16), jnp.asarray(e.T, _BF16), jnp.asarray(ltri, _BF16)


def _pad_lanes(v, n=128):
    return jnp.pad(v, ((0, 0), (0, n - v.shape[1])))


def _local_step(x, p, tgt, shard, conv_w_shard, small, seq_len):
    seq_chunks = seq_len // CH
    e_mat, et_mat, ltri_mat = _const_mats()
    g_mix, g_mlp, g_ple = small["norm_mix_g"], small["norm_mlp_g"], small["ple_norm_g"]
    g_fin = small["final_norm_g"].reshape(1, D)
    gv, gout, ng = small["gm_v_norm_g"], small["gm_out_norm_g"], small["ssd_norm_g"]
    ws = small["gm_ws"][0]
    bsb = jnp.broadcast_to(small["gm_bs"][0][:, :, None], (GM_H, CH, 128))
    cb = small["ssd_conv_b"]
    dtb, alog = _pad_lanes(small["ssd_dt_bias"]), _pad_lanes(small["ssd_a_log"])
    d_x = jnp.repeat(small["ssd_d"], SSD_P, axis=1)

    first = lambda acc: (acc,)
    rows, cols = ("gather2", 0), ("gather2", 1)
    n1, (g_win, g_cw) = _norm_cast("norm_mix", x, g_mix,
                                   comm=[(rows, shard["w_in"][None]), (("gather", 0), conv_w_shard[None])])
    w_in_t = jnp.pad(g_win.reshape(D_IN, D), ((0, D_IN_PAD - D_IN), (0, 0)))
    cw8 = jnp.pad(g_cw.transpose(1, 0, 2).reshape(CONV_K, CONV_CH), ((0, 8 - CONV_K), (0, 0)))
    mix_consts = (gv, ws, bsb, gout, cw8, cb, dtb, alog, d_x, ng)
    (proj,), (w_out, w1) = _matmul("proj_in", n1, w_in_t, "nt", 256, D_IN_PAD, D, first, [("tile", _F32)],
                                   comm=[(rows, shard["w_out"]), (cols, shard["w_ff1"])])
    (cat, yss, states), (w2, wg, wp) = _mixer_fwd(
        proj, *mix_consts, e_mat, ltri_mat, seq_chunks,
        comm=[(rows, shard["w_ff2"]), (rows, shard["w_ple_gate"]), (cols, shard["w_ple_proj"])])

    def epi_res_norm(acc, res, g):
        hv = acc + res
        return hv, _rms(hv, g)

    (h1, n2), _ = _matmul("proj_out", cat, w_out, "nn", 1024, D, 2 * D, epi_res_norm,
                          [("tile", _F32), ("tile", _MXU)], extras=[(x, "tile"), (g_mlp, "row")])

    def epi_relu2(acc):
        hid = jnp.maximum(acc, 0.0)
        return hid, hid * hid

    (hid, hid2), _ = _matmul("ff1", n2, w1, "nn", 1024, 1024, D, epi_relu2, [("tile", _MXU), ("tile", _MXU)])
    (h2, n3), _ = _matmul("ff2", hid2, w2, "nn", 512, D, D_FF, epi_res_norm, [("tile", _F32), ("tile", _MXU)],
                          extras=[(h1, "tile"), (g_ple, "row")])
    (pp,), _ = _matmul("ple_proj", p, wp, "nn", 512, D, D_PLE, first, [("tile", _F32)])

    def epi_norm_bwd(acc, up, hv, g):
        dx, dg8 = _rms_bwd(hv, g, acc)
        dh = up + dx
        return dh, dh, dg8

    def epi_head(acc, wg_v, ppv, h2v, tg, gf, gp_):
        gate = _sigmoid(acc)
        gp = gate * ppv
        h3 = h2v + gp
        r = lax.rsqrt(jnp.mean(h3 * h3, axis=-1, keepdims=True) + EPS)
        nh = h3 * r
        err = nh * gf - tg
        gy = err * (gf * (1.0 / D))
        dh3 = (gy - nh * jnp.mean(gy * nh, axis=-1, keepdims=True)) * r
        dpp = dh3 * gate
        da3 = dpp * (ppv - gp)
        dh2, dh2_again, dgple8 = epi_norm_bwd(_dot_nt(_mx(da3), wg_v), dh3, h2v, gp_)
        return da3, dpp, dh2, dh2_again, _sum8(err * err), _sum8(err * nh) * (1.0 / D), dgple8

    (da3, dpp, dh2, dh2b, lossp, dgfin, dgple), _ = _matmul(
        "ple_gate_loss_bwd", n3, wg, "nn", 512, D, D, epi_head,
        [("tile", _MXU), ("tile", _MXU), ("tile", _F32), ("tile", _MXU), ("part8", _F32), ("part8", _F32),
         ("part8", _F32)],
        extras=[(pp, "tile"), (h2, "tile"), (tgt, "tile"), (g_fin, "row"), (g_ple, "row")], b_to_epilogue=True)

    s_rows, s_cols = ("scatter", 0), ("scatter", 1)
    (dwp,), _ = _matmul("d_w_ple_proj", p, dpp, "tn", D_PLE, D, 2048, first, [("tile", _BF16)])
    (dwg,), _ = _matmul("d_w_ple_gate", n3, da3, "tn", D, D, 2048, first, [("tile", _BF16)])
    (dw2,), (r_wp, r_wg) = _matmul("d_w_ff2", hid2, dh2b, "tn", 1024, D, 2048, first, [("tile", _BF16)],
                                   comm=[(s_cols, dwp), (s_rows, dwg)])
    (da1,), (r_w2,) = _matmul("d_ff_hidden", dh2b, w2, "nt", 512, 2048, D,
                              lambda acc, hv: (acc * 2.0 * hv.astype(_F32),), [("tile", _MXU)],
                              extras=[(hid, "tile")], comm=[(s_rows, dw2)])
    (dw1,), _ = _matmul("d_w_ff1", n2, da1, "tn", 1024, 1024, 2048, first, [("tile", _BF16)])
    (dh1, dh1b, dgmlp), (r_w1,) = _matmul(
        "d_h1", da1, w1, "nt", 256, D, D_FF, epi_norm_bwd, [("tile", _F32), ("tile", _MXU), ("part8", _F32)],
        extras=[(dh2, "tile"), (h1, "tile"), (g_mlp, "row")], comm=[(s_cols, dw1)])
    (dwout,), _ = _matmul("d_w_out", cat, dh1b, "tn", 1024, D, 2048, first, [("tile", _BF16)])
    (dcat,), (r_wout,) = _matmul("d_cat", dh1b, w_out, "nt", 1024, 1024, D, first, [("tile", _F32)],
                                 comm=[(s_rows, dwout)])
    (dproj, dws, dbs, dgv, dgout, dng, dcw, dcb, ddtb, dalog, dd) = _mixer_bwd(
        proj, dcat, yss, states, *mix_consts, e_mat, et_mat, ltri_mat, seq_chunks)
    pieces = dict(gm_v_norm_g=dgv, gm_out_norm_g=dgout, ssd_norm_g=dng, norm_mlp_g=dgmlp, ple_norm_g=dgple,
                  final_norm_g=dgfin, gm_ws=dws, gm_bs=dbs, ssd_conv_w=dcw, ssd_conv_b=dcb, ssd_dt_bias=ddtb,
                  ssd_a_log=dalog, ssd_d=dd, loss=lossp)
    parts, segments, n_rows, where = _small_layout(pieces)
    small_block = _small_local(parts, segments, n_rows)
    (dwin_t,), (small_blocks,) = _matmul("d_w_in", n1, dproj, "tn", 512, D_IN_PAD, 1024, lambda acc: (acc.T,),
                                         [("tile_t", _BF16)], comm=[(("gather", 0), small_block)])
    dwin_blocks = dwin_t[:D_IN].reshape(N_DEV, SHARD_IN, D)
    (gx, dgmix), (r_win,) = _matmul(
        "d_x", dproj, w_in_t, "nn", 256, D, D_IN_PAD, lambda *a: epi_norm_bwd(*a)[1:],
        [("tile", _F32), ("part8", _F32)], extras=[(dh1, "tile"), (x, "tile"), (g_mix, "row")],
        comm=[(s_rows, dwin_blocks)])
    r_win = r_win.reshape(N_DEV, SHARD_IN, D)

    big = dict(w_in=r_win, w_out=r_wout, w_ff1=r_w1, w_ff2=r_w2, w_ple_gate=r_wg, w_ple_proj=r_wp)
    return gx, big, small_blocks, dgmix, n_rows, where


def _small_layout(pieces):
    rows, segments = [], []
    in_row, out_row = 0, 0

    def add(arr, kind, n_out):
        nonlocal in_row, out_row
        rows.append(arr)
        segments.append((out_row, n_out, in_row, arr.shape[0], kind))
        start = out_row
        in_row += arr.shape[0]
        out_row += n_out
        return start

    where = {_LATE: 0}
    out_row = 1
    for name in _G_VECS[1:]:
        where[name] = add(pieces[name], "sum", 1)
    where["gm_ws"] = add(pieces["gm_ws"].reshape(GM_H * CH * CH // D, D), "copy", GM_H * CH * CH // D)
    where["gm_bs"] = add(pieces["gm_bs"].reshape(1, D), "copy", 1)
    cb = jnp.pad(pieces["ssd_conv_b"], ((0, 0), (0, 2 * D - CONV_CH)))
    where["ssd_conv_b"] = add(cb[:, :D], "sum", 1)
    add(cb[:, D:], "sum", 1)
    cw = jnp.pad(pieces["ssd_conv_w"][:CONV_K], ((0, 0), (0, 2 * D - CONV_CH)))
    where["ssd_conv_w"] = add(cw.reshape(2 * CONV_K, D), "copy", 2 * CONV_K)
    misc = jnp.concatenate([pieces["ssd_dt_bias"], pieces["ssd_a_log"], pieces["ssd_d"],
                            jnp.zeros((8, D - 3 * 128), _F32)], axis=1)
    where["misc"] = add(misc, "sum", 1)
    where["loss"] = add(pieces["loss"], "loss", 1)
    n_rows = -(-out_row // 8) * 8
    return jnp.concatenate(rows, axis=0), tuple(segments), n_rows, where


def _pack_small_params(vals, where, n_rows, my_block):
    rows, at = [], {}

    def add(name, arr):
        at[name] = sum(r.shape[0] for r in rows)
        rows.append(arr)

    for name in _G_VECS:
        add(name, vals[name].reshape(1, D))
    add("gm_ws", vals["gm_ws"].reshape(GM_H * CH * CH // D, D))
    add("gm_bs", vals["gm_bs"].reshape(1, D))
    cb = jnp.pad(vals["ssd_conv_b"].reshape(1, CONV_CH), ((0, 0), (0, 2 * D - CONV_CH)))
    add("ssd_conv_b", cb.reshape(2, D))
    cw = lax.dynamic_update_slice(jnp.zeros((CONV_K, 2 * D), _F32), vals["ssd_conv_w"].reshape(CONV_K, -1),
                                  (0, my_block * (CONV_CH // N_DEV)))
    add("ssd_conv_w", cw.reshape(2 * CONV_K, D))
    misc = jnp.concatenate([_pad_lanes(vals["ssd_dt_bias"].reshape(1, SSD_H)),
                            _pad_lanes(vals["ssd_a_log"].reshape(1, SSD_H)),
                            _pad_lanes(vals["ssd_d"].reshape(1, SSD_H)), jnp.zeros((1, D - 3 * 128), _F32)], axis=1)
    add("misc", misc)
    assert all(where[k] == r for k, r in at.items()), (where, at)
    rows.append(jnp.zeros((n_rows - sum(r.shape[0] for r in rows), D), _F32))
    return jnp.concatenate(rows, axis=0)


def _unpack_small(buf, where, my_block, shapes):
    out = {}
    for name in _G_VECS:
        out[name] = buf[where[name]].reshape(shapes[name])
    n_ws = GM_H * CH * CH // D
    out["gm_ws"] = buf[where["gm_ws"]:where["gm_ws"] + n_ws].reshape(shapes["gm_ws"])
    out["gm_bs"] = buf[where["gm_bs"]].reshape(shapes["gm_bs"])
    r = where["ssd_conv_b"]
    out["ssd_conv_b"] = buf[r:r + 2].reshape(1, 2 * D)[:, :CONV_CH].reshape(shapes["ssd_conv_b"])
    r = where["ssd_conv_w"]
    cw = buf[r:r + 2 * CONV_K].reshape(CONV_K, 2 * D)
    out["ssd_conv_w"] = lax.dynamic_slice(cw, (0, my_block * (CONV_CH // N_DEV)),
                                          (CONV_K, CONV_CH // N_DEV)).reshape(shapes["ssd_conv_w"])
    misc = buf[where["misc"]]
    for i, name in enumerate(("ssd_dt_bias", "ssd_a_log", "ssd_d")):
        out[name] = misc[i * 128:i * 128 + SSD_H].reshape(shapes[name])
    return out


_WEIGHTS = ("norm_mix_g", "w_in", "gm_v_norm_g", "gm_ws", "gm_bs", "gm_out_norm_g", "ssd_conv_w", "ssd_conv_b",
            "ssd_dt_bias", "ssd_a_log", "ssd_d", "ssd_norm_g", "w_out", "norm_mlp_g", "w_ff1", "w_ff2", "ple_norm_g",
            "w_ple_gate", "w_ple_proj", "final_norm_g")


def kernel(x, p, norm_mix_g, w_in, gm_v_norm_g, gm_ws, gm_bs, gm_out_norm_g, ssd_conv_w, ssd_conv_b, ssd_dt_bias, ssd_a_log, ssd_d, ssd_norm_g, w_out, norm_mlp_g, w_ff1, w_ff2, ple_norm_g, w_ple_gate, w_ple_proj, final_norm_g, loss_target, m_norm_mix_g, m_w_in, m_gm_v_norm_g, m_gm_ws, m_gm_bs, m_gm_out_norm_g, m_ssd_conv_w, m_ssd_conv_b, m_ssd_dt_bias, m_ssd_a_log, m_ssd_d, m_ssd_norm_g, m_w_out, m_norm_mlp_g, m_w_ff1, m_w_ff2, m_ple_norm_g, m_w_ple_gate, m_w_ple_proj, m_final_norm_g, v_norm_mix_g, v_w_in, v_gm_v_norm_g, v_gm_ws, v_gm_bs, v_gm_out_norm_g, v_ssd_conv_w, v_ssd_conv_b, v_ssd_dt_bias, v_ssd_a_log, v_ssd_d, v_ssd_norm_g, v_w_out, v_norm_mlp_g, v_w_ff1, v_w_ff2, v_ple_norm_g, v_w_ple_gate, v_w_ple_proj, v_final_norm_g):
    args = dict(locals())
    w = {n: args[n] for n in _WEIGHTS}
    m = {n: args["m_" + n] for n in _WEIGHTS}
    v = {n: args["v_" + n] for n in _WEIGHTS}
    shapes = {n: w[n].shape for n in _WEIGHTS}
    my_block = 4 * lax.axis_index("x") + 2 * lax.axis_index("y") + lax.axis_index("c")
    nb, seq_len, _ = x.shape

    local = lambda d, n: d[n][0].T if n == "w_in" else d[n][0]
    shard = {n: local(w, n).astype(_MXU) for n in _BIG_NAMES}
    small = {n: w[n] for n in _WEIGHTS if n not in _BIG_NAMES}
    gx, recv, small_blocks, late8, n_rows, where = _local_step(
        x.reshape(nb * seq_len, D), p.reshape(nb * seq_len, D_PLE), loss_target.reshape(nb * seq_len, D), shard,
        ssd_conv_w[0], small, seq_len)

    big_out = [{}, {}, {}, {}]
    for n in _BIG_NAMES:
        res = _sum_adam("sum_adam_" + n, recv[n], local(w, n), local(m, n), local(v, n))
        for k in range(4):
            big_out[k][n] = (res[k].T if n == "w_in" else res[k]).reshape(shapes[n])

    packs = [_pack_small_params(d, where, n_rows, my_block) for d in (w, m, v)]
    small_res = _small_final(small_blocks, late8, where[_LATE], *packs)
    loss = small_res[0][where["loss"], 0]
    small_out = [_unpack_small(a, where, my_block, shapes) for a in small_res]

    outs = [loss, gx.reshape(x.shape)]
    for k in range(4):
        outs += [big_out[k][n] if n in _BIG_NAMES else small_out[k][n] for n in _WEIGHTS]
    return tuple(outs)
```

```python
import functools
import math

import jax
import jax.numpy as jnp
import numpy as np
from jax import lax
from jax.experimental import pallas as pl
from jax.experimental.pallas import tpu as pltpu

_F32 = jnp.float32
_BF16 = jnp.bfloat16
_MXU = jnp.bfloat16

D = 1024
D_PLE = 256
GM_W = 1024
GM_H = 8
CH = 128
SSD_W = 1024
SSD_H = 16
SSD_P = 64
SSD_G = 2
SSD_N = 128
CONV_K = 4
CONV_CH = SSD_W + 2 * SSD_G * SSD_N
D_FF = 4096
D_IN = 2 * GM_W + SSD_W + CONV_CH + SSD_H
D_IN_PAD = 4736
DT_BLK = (D_IN_PAD - 128) // 128
EPS = 1e-6
N_DEV = 8
SHARD_IN = D_IN // N_DEV

LR, B1, B2, ADAM_EPS, WD, STEP = 0.001, 0.9, 0.999, 1e-08, 0.01, 10
_LOG2E = math.log2(math.e)
_LOG2_INV_SQRT_2PI = -0.5 * math.log2(2.0 * math.pi)

_V7X_VMEM_BYTES = 64 * 1024 * 1024
_VMEM_CAP = _V7X_VMEM_BYTES - 8 * 1024 * 1024
_MESH = pl.DeviceIdType.MESH


def _vmem_limit(nbytes):
    return int(min(_VMEM_CAP, max(32 * 1024 * 1024, nbytes * 5 // 4 + (4 << 20))))


def _nbytes(shape, dtype):
    return int(np.prod(shape)) * jnp.dtype(dtype).itemsize


def _mx(v):
    return v.astype(_MXU)


def _dot(a, b):
    return jnp.dot(a, b, preferred_element_type=_F32)


def _dot_nt(a, b):
    return lax.dot_general(a, b, (((1,), (1,)), ((), ())), preferred_element_type=_F32)


def _dot_tn(a, b):
    return lax.dot_general(a, b, (((0,), (0,)), ((), ())), preferred_element_type=_F32)


def _split3(a):
    hi = a.astype(_BF16)
    r = a - hi.astype(_F32)
    mid = r.astype(_BF16)
    lo = (r - mid.astype(_F32)).astype(_BF16)
    return hi, mid, lo


def _xdot(dotfn, a, b01):
    b = b01.astype(_BF16)
    hi, mid, lo = _split3(a)
    return (dotfn(hi, b) + dotfn(mid, b)) + dotfn(lo, b)


def _xdot_left(dotfn, a01, b):
    a = a01.astype(_BF16)
    hi, mid, lo = _split3(b)
    return (dotfn(a, hi) + dotfn(a, mid)) + dotfn(a, lo)


def _sum8(v):
    r, n = v.shape
    return v.reshape(r // 8, 8, n).sum(axis=0)


def _sigmoid(v):
    return 1.0 / (1.0 + jnp.exp(-v))


def _rms(xv, g):
    ms = jnp.mean(xv * xv, axis=-1, keepdims=True)
    return xv * lax.rsqrt(ms + EPS) * g


def _rms_bwd(xv, g, dn):
    r = lax.rsqrt(jnp.mean(xv * xv, axis=-1, keepdims=True) + EPS)
    nh = xv * r
    gy = dn * g
    dx = (gy - nh * jnp.mean(gy * nh, axis=-1, keepdims=True)) * r
    return dx, _sum8(dn * nh)


def _iota2(shape, axis):
    return lax.broadcasted_iota(jnp.int32, shape, axis)


def _norm_cast(name, x, g, tm=512, comm=()):
    t, n = x.shape
    tm = min(tm, t)
    steps = t // tm
    kinds = [kind for kind, _ in comm]
    c_in, c_in_specs, c_out_specs, c_out_shape, c_scratch = _comm_io(comm)

    def body(*refs):
        x_ref, g_ref = refs[0], refs[1]
        o_ref = refs[2 + len(comm)]
        comm_refs = (kinds, refs[2:2 + len(comm)], refs[3 + len(comm):3 + 2 * len(comm)], *refs[3 + 2 * len(comm):])
        if comm:
            pl.when(pl.program_id(0) == 0)(lambda: _comm_start(*comm_refs))

        o_ref[...] = _rms(x_ref[...], g_ref[...]).astype(o_ref.dtype)
        if comm:
            pl.when(pl.program_id(0) == steps - 1)(lambda: _comm_finish(*comm_refs))

    res = pl.pallas_call(
        body, name=name, grid=(steps,),
        in_specs=[pl.BlockSpec((tm, n), lambda i: (i, 0)), pl.BlockSpec((1, n), lambda i: (0, 0))] + c_in_specs,
        out_specs=[pl.BlockSpec((tm, n), lambda i: (i, 0))] + c_out_specs,
        out_shape=[jax.ShapeDtypeStruct((t, n), _MXU)] + c_out_shape, scratch_shapes=c_scratch,
        compiler_params=pltpu.CompilerParams(dimension_semantics=("arbitrary",)),
    )(x, g, *c_in)
    return res[0], res[1:]


def _matmul(name, a, b, mode, tm, tn, tk, epilogue, outs, extras=(), comm=(), b_to_epilogue=False):
    m, k = a.shape[::-1] if mode == "tn" else a.shape
    n = b.shape[0] if mode == "nt" else b.shape[1]
    tm, tn, tk = min(tm, m), min(tn, n), min(tk, k)
    assert m % tm == 0 and n % tn == 0 and k % tk == 0, (name, m, n, k, tm, tn, tk)
    if mode == "nn":
        a_spec = pl.BlockSpec((tm, tk), lambda i, j, kk: (i, kk))
        b_spec = pl.BlockSpec((tk, tn), lambda i, j, kk: (kk, j))
        dotfn = _dot
    elif mode == "nt":
        a_spec = pl.BlockSpec((tm, tk), lambda i, j, kk: (i, kk))
        b_spec = pl.BlockSpec((tn, tk), lambda i, j, kk: (j, kk))
        dotfn = _dot_nt
    else:
        a_spec = pl.BlockSpec((tk, tm), lambda i, j, kk: (kk, i))
        b_spec = pl.BlockSpec((tk, tn), lambda i, j, kk: (kk, j))
        dotfn = _dot_tn
    ni, nj, nk = m // tm, n // tn, k // tk
    n_ex, n_out, n_comm = len(extras), len(outs), len(comm)
    kinds = [kind for kind, _ in comm]
    c_in, c_in_specs, c_out_specs, c_out_shape, c_scratch = _comm_io(comm)

    in_specs, vmem = [a_spec, b_spec], 2 * (tm * tk * a.dtype.itemsize + tk * tn * b.dtype.itemsize)
    for arr, kind in extras:
        if kind == "tile":
            in_specs.append(pl.BlockSpec((tm, tn), lambda i, j, kk: (i, j)))
            vmem += 2 * _nbytes((tm, tn), arr.dtype)
        elif kind == "rows":
            in_specs.append(pl.BlockSpec((tm, arr.shape[1]), lambda i, j, kk: (i, 0)))
            vmem += 2 * _nbytes((tm, arr.shape[1]), arr.dtype)
        elif kind == "full":
            in_specs.append(pl.BlockSpec(arr.shape, lambda i, j, kk: (0,) * arr.ndim))
            vmem += 2 * _nbytes(arr.shape, arr.dtype)
        else:
            in_specs.append(pl.BlockSpec((1, tn), lambda i, j, kk: (0, j)))
    out_specs, out_shape = [], []
    for kind, dt in outs:
        if kind == "tile":
            out_specs.append(pl.BlockSpec((tm, tn), lambda i, j, kk: (i, j)))
            out_shape.append(jax.ShapeDtypeStruct((m, n), dt))
            vmem += 2 * _nbytes((tm, tn), dt)
        elif kind == "tile_t":
            out_specs.append(pl.BlockSpec((tn, tm), lambda i, j, kk: (j, i)))
            out_shape.append(jax.ShapeDtypeStruct((n, m), dt))
            vmem += 2 * _nbytes((tm, tn), dt)
        else:
            assert nj == 1, "the partial-sum rows are accumulated over consecutive row tiles"
            out_specs.append(pl.BlockSpec((8, tn), lambda i, j, kk: (0, 0)))
            out_shape.append(jax.ShapeDtypeStruct((8, n), dt))
    scratch = [pltpu.VMEM((tm, tn), _F32)] if nk > 1 else []
    vmem += _nbytes((tm, tn), _F32) * 2

    def body(*refs):
        a_ref, b_ref = refs[0], refs[1]
        ex_refs = refs[2:2 + n_ex]
        n_in = 2 + n_ex + n_comm
        out_refs = refs[n_in:n_in + n_out]
        i, j, kk = pl.program_id(0), pl.program_id(1), pl.program_id(2)
        comm_refs = (kinds, refs[2 + n_ex:n_in], refs[n_in + n_out:n_in + n_out + n_comm], *refs[len(refs) - 3:])
        if n_comm:
            pl.when((i == 0) & (j == 0) & (kk == 0))(lambda: _comm_start(*comm_refs))

        b_val = _mx(b_ref[...])
        part = dotfn(_mx(a_ref[...]), b_val)

        def finish(acc):
            vals = epilogue(acc, *([b_val] if b_to_epilogue else []), *[r[...] for r in ex_refs])
            for r, v, (kind, _) in zip(out_refs, vals, outs):
                if kind == "part8":
                    @pl.when(i == 0)
                    def _():
                        r[...] = v

                    @pl.when(i > 0)
                    def _():
                        r[...] += v
                else:
                    r[...] = v.astype(r.dtype)

        if nk == 1:
            finish(part)
        else:
            acc_ref = refs[n_in + n_out + n_comm]

            @pl.when(kk == 0)
            def _():
                acc_ref[...] = part

            @pl.when(kk > 0)
            def _():
                acc_ref[...] += part

            @pl.when(kk == nk - 1)
            def _():
                finish(acc_ref[...])

        if n_comm:
            pl.when((i == ni - 1) & (j == nj - 1) & (kk == nk - 1))(lambda: _comm_finish(*comm_refs))

    carried =n_comm or any(kind == "part8" for kind, _ in outs)
    sem = ("arbitrary",) * 3 if carried else ("parallel", "parallel", "arbitrary")
    res = pl.pallas_call(
        body, name=name, grid=(ni, nj, nk),
        in_specs=in_specs + c_in_specs, out_specs=out_specs + c_out_specs, out_shape=out_shape + c_out_shape,
        scratch_shapes=scratch + c_scratch,
        compiler_params=pltpu.CompilerParams(dimension_semantics=sem, vmem_limit_bytes=_vmem_limit(vmem)),
    )(a, b, *[arr for arr, _ in extras], *c_in)
    return res[:n_out], res[n_out:]


def _shift_down(v, halo8, j):
    if j == 0:
        return v
    r = pltpu.roll(v, j, axis=0)
    hr = pltpu.roll(halo8, j, axis=0)
    top = jnp.where(_iota2(hr.shape, 0) < j, hr, r[:8])
    return jnp.concatenate([top, r[8:]], axis=0)


def _shift_up(v, next8, j):
    if j == 0:
        return v
    rows = v.shape[0]
    r = pltpu.roll(v, rows - j, axis=0)
    nr = pltpu.roll(next8, 8 - j, axis=0)
    bot = jnp.where(_iota2(nr.shape, 0) >= 8 - j, nr, r[rows - 8:])
    return jnp.concatenate([r[:rows - 8], bot], axis=0)


def _silu_grad(sig, silu):
    return sig + silu * (1.0 - sig)


def _gmlp_fwd_vals(pu, pv, gv, ws_ref, bsb_ref, want_bwd):
    tril = _iota2((CH, CH), 0) >= _iota2((CH, CH), 1)
    cdf_u = 0.5 * (1.0 + lax.erf(pu * 0.7071067811865476))
    cdf_v = 0.5 * (1.0 + lax.erf(pv * 0.7071067811865476))
    u = pu * cdf_u
    v = pv * cdf_v
    ys, keep = [], [(cdf_u, cdf_v)] if want_bwd else []
    for h in range(GM_H):
        sl = slice(h * 128, (h + 1) * 128)
        vh = v[:, sl]
        r = lax.rsqrt(jnp.mean(vh * vh, axis=-1, keepdims=True) + EPS)
        vn = vh * r * gv[:, sl]
        wm = _mx(jnp.where(tril, ws_ref[h], 0.0))
        mixed = _dot(wm, _mx(vn)) + bsb_ref[h]
        ys.append(u[:, sl] * mixed)
        if want_bwd:
            keep.append((vh, r, vn, wm, mixed))
    return jnp.concatenate(ys, axis=1), u, keep


def _ssd_common(xbc, halo8, dtraw, cw_ref, cb, dtb, alog, e_ref, ltri):
    xs = [_shift_down(xbc, halo8, j) for j in range(CONV_K)]
    cpre = cb + sum(cw_ref[k:k + 1, :] * xs[CONV_K - 1 - k] for k in range(CONV_K))
    sig = _sigmoid(cpre)
    act = cpre * sig
    dtin = dtraw + dtb
    dt = jnp.maximum(dtin, 0.0) + jnp.log(1.0 + jnp.exp(-jnp.abs(dtin)))
    a_neg = -jnp.exp(alog)
    cs = _xdot_left(_dot, ltri, dt * a_neg)
    cs_last = cs[CH - 1:CH, :]
    ecs = jnp.exp(cs)
    dec = jnp.exp(cs_last - cs)
    cdec = jnp.exp(cs_last)
    e = e_ref[...]
    dt_x = _dot(dt.astype(_BF16), e)
    ecs_x = _dot(ecs.astype(_BF16), e)
    dec_x = _dot(dec.astype(_BF16), e)
    cdec_x = _xdot(_dot, jnp.broadcast_to(cdec, (8, 128)), e)[0:1, :]
    return dict(xs=xs, sig=sig, act=act, dtin=dtin, dt=dt, a_neg=a_neg, cs=cs, ecs=ecs, dec=dec, cdec=cdec,
                dt_x=dt_x, ecs_x=ecs_x, dec_x=dec_x, cdec_x=cdec_x)


def _head_lm(cs, cst_ref, h, tril):
    seg = jnp.broadcast_to(cs[:, h:h + 1], (CH, CH)) - cst_ref[h:h + 1, :]
    return jnp.exp(jnp.where(tril, seg, -jnp.inf))


def _mixer_fwd(proj, gv, ws, bsb, gout, cw8, cb, dtb, alog, d_x, ng, e_mat, ltri_mat, seq_chunks, comm=()):
    t = proj.shape[0]
    n_chunks = t // CH
    n_comm = len(comm)
    kinds = [kind for kind, _ in comm]
    c_in, c_in_specs, c_out_specs, c_out_shape, c_scratch = _comm_io(comm)

    def body(*refs):
        (pu_ref, pv_ref, z_ref, xbc_ref, dt_ref, halo_ref, gv_ref, ws_ref, bsb_ref, gout_ref, cw_ref, cb_ref,
         dtb_ref, alog_ref, dx_ref, ng_ref, e_ref, ltri_ref) = refs[:18]
        cat_ref, y_ref, st_ref = refs[18 + n_comm:21 + n_comm]
        s_ref, cst_ref = refs[21 + 2 * n_comm:23 + 2 * n_comm]
        comm_refs = (kinds, refs[18:18 + n_comm], refs[21 + n_comm:21 + 2 * n_comm], *refs[23 + 2 * n_comm:])
        c = pl.program_id(0)
        if n_comm:
            pl.when(c == 0)(lambda: _comm_start(*comm_refs))
            pl.when(c == n_chunks - 1)(lambda: _comm_finish(*comm_refs))

        first = (c % seq_chunks) == 0
        tril = _iota2((CH, CH), 0) >= _iota2((CH, CH), 1)
        lane = _iota2((CH, 128), 1)

        y_a, _, _ = _gmlp_fwd_vals(pu_ref[...], pv_ref[...], gv_ref[...], ws_ref, bsb_ref, False)
        cat_ref[:, 0:GM_W] = _rms(y_a, gout_ref[...]).astype(cat_ref.dtype)

        @pl.when(first)
        def _():
            s_ref[...] = jnp.zeros_like(s_ref)

        halo8 = jnp.where(first, 0.0, halo_ref[...])
        q = _ssd_common(xbc_ref[...], halo8, dt_ref[...], cw_ref, cb_ref[...], dtb_ref[...], alog_ref[...], e_ref,
                        ltri_ref[...])
        act = q["act"]
        xv = act[:, 0:SSD_W]
        xdt = xv * q["dt_x"]
        xdt_m = _mx(xdt)
        cs = q["cs"]
        cst_ref[...] = cs.T
        s_prev = s_ref[...]
        st_ref[...] = s_prev
        ys = []
        for g in range(SSD_G):
            bg = _mx(act[:, SSD_W + g * SSD_N:SSD_W + (g + 1) * SSD_N])
            cg = _mx(act[:, SSD_W + SSD_G * SSD_N + g * SSD_N:SSD_W + SSD_G * SSD_N + (g + 1) * SSD_N])
            cbm = _dot_nt(cg, bg)
            gs = slice(g * 512, (g + 1) * 512)
            for pr in range(4):
                ps = slice(g * 512 + pr * 128, g * 512 + (pr + 1) * 128)
                o = []
                for hh in range(2):
                    h = g * 8 + pr * 2 + hh
                    m_h = _mx(cbm * _head_lm(cs, cst_ref, h, tril))
                    o.append(_dot(m_h, xdt_m[:, ps]))
                ys.append(jnp.where(lane < SSD_P, o[0], o[1]))
            sg = s_prev[:, gs]
            yoff = _dot(cg, _mx(sg)) * q["ecs_x"][:, gs]
            ys[-4:] = [ys[-4 + i] + yoff[:, i * 128:(i + 1) * 128] for i in range(4)]
            st_new = _dot_tn(bg, _mx(q["dec_x"][:, gs] * xdt[:, gs]))
            s_ref[:, gs] = sg * q["cdec_x"][:, gs] + st_new
        y = jnp.concatenate(ys, axis=1) + dx_ref[...] * xv
        y_ref[...] = y
        zv = z_ref[...]
        yg = y * (zv * _sigmoid(zv))
        for g in range(SSD_G):
            gs = slice(g * 512, (g + 1) * 512)
            cat_ref[:, GM_W + g * 512:GM_W + (g + 1) * 512] = _rms(yg[:, gs], ng_ref[:, gs]).astype(cat_ref.dtype)

    blk = lambda w, j: pl.BlockSpec((CH, w), lambda c: (c, j))
    full = lambda arr: pl.BlockSpec(arr.shape, lambda c: (0,) * arr.ndim)
    consts = [gv, ws, bsb, gout, cw8, cb, dtb, alog, d_x, ng, e_mat, ltri_mat]
    res = pl.pallas_call(
        body, name="mixer_fwd", grid=(n_chunks,),
        in_specs=[blk(GM_W, 0), blk(GM_W, 1), blk(SSD_W, 2), blk(CONV_CH, 2), blk(128, DT_BLK),
                  pl.BlockSpec((8, CONV_CH), lambda c: (jnp.maximum(c * (CH // 8) - 1, 0), 2))]
        + [full(a) for a in consts] + c_in_specs,
        out_specs=[pl.BlockSpec((CH, 2 * D), lambda c: (c, 0)), pl.BlockSpec((CH, SSD_W), lambda c: (c, 0)),
                   pl.BlockSpec((CH, SSD_W), lambda c: (c, 0))] + c_out_specs,
        out_shape=[jax.ShapeDtypeStruct((t, 2 * D), _MXU), jax.ShapeDtypeStruct((t, SSD_W), _F32),
                   jax.ShapeDtypeStruct((t, SSD_W), _F32)] + c_out_shape,
        scratch_shapes=[pltpu.VMEM((SSD_N, SSD_W), _F32), pltpu.VMEM((128, CH), _F32)] + c_scratch,
        compiler_params=pltpu.CompilerParams(dimension_semantics=("arbitrary",), vmem_limit_bytes=48 << 20),
    )(proj, proj, proj, proj, proj, proj, *consts, *c_in)
    return res[:3], res[3:]


def _mixer_bwd(proj, dcat, yss, states, gv, ws, bsb, gout, cw8, cb, dtb, alog, d_x, ng, e_mat, et_mat, ltri_mat,
               seq_chunks):
    t = proj.shape[0]
    n_chunks = t // CH

    def body(pu_ref, pv_ref, z_ref, xbc_ref, dt_ref, halo_ref, dcat_ref, y_ref, st_ref,
             gv_ref, ws_ref, bsb_ref, gout_ref, cw_ref, cb_ref, dtb_ref, alog_ref, dx_ref, ng_ref,
             e_ref, et_ref, ltri_ref,
             dproj_ref, dws_ref, dbs_ref, dgv_ref, dgout_ref, dng_ref, dcw_ref, dcb_ref, ddtb_ref, dalog_ref, dd_ref,
             ds_ref, dnext_ref, dcst_ref, cst_ref, dbacc_ref, ddacc_ref):
        i = pl.program_id(0)
        c = n_chunks - 1 - i
        first = (c % seq_chunks) == 0
        last_in_seq = (c % seq_chunks) == seq_chunks - 1
        tril = _iota2((CH, CH), 0) >= _iota2((CH, CH), 1)
        lane = _iota2((CH, 128), 1)
        row = _iota2((CH, 128), 0)

        @pl.when(i == 0)
        def _():
            for r in (dws_ref, dbs_ref, dgv_ref, dgout_ref, dng_ref, dcw_ref, dcb_ref, ddtb_ref, dalog_ref, dd_ref,
                      dbacc_ref, ddacc_ref, dcst_ref):
                r[...] = jnp.zeros_like(r)

        @pl.when(last_in_seq)
        def _():
            ds_ref[...] = jnp.zeros_like(ds_ref)
            dnext_ref[...] = jnp.zeros_like(dnext_ref)

        dcat_v = dcat_ref[...].astype(_F32)

        pu, pv = pu_ref[...], pv_ref[...]
        gv_v = gv_ref[...]
        y_a, u, keep = _gmlp_fwd_vals(pu, pv, gv_v, ws_ref, bsb_ref, True)
        dy, dgout8 = _rms_bwd(y_a, gout_ref[...], dcat_v[:, 0:GM_W])
        dgout_ref[...] += dgout8
        dus, dvs, dgvs = [], [], []
        for h in range(GM_H):
            sl = slice(h * 128, (h + 1) * 128)
            vh, r, vn, wm, mixed = keep[h + 1]
            dyh = dy[:, sl]
            dus.append(dyh * mixed)
            dmix = dyh * u[:, sl]
            dmix_m = _mx(dmix)
            dws_ref[h] += jnp.where(tril, _dot_nt(dmix_m, _mx(vn)), 0.0)
            dbacc_ref[h] += dmix
            dvn = _dot_tn(wm, dmix_m)
            gy = dvn * gv_v[:, sl]
            nh = vh * r
            dvs.append((gy - nh * jnp.mean(gy * nh, axis=-1, keepdims=True)) * r)
            dgvs.append(_sum8(dvn * nh))
        dgv_ref[...] += jnp.concatenate(dgvs, axis=1)
        cdf_u, cdf_v = keep[0]
        gelu_grad = lambda pre, cdf: cdf + pre * jnp.exp2(pre * pre * (-0.5 * _LOG2E) + _LOG2_INV_SQRT_2PI)
        dproj_ref[:, 0:GM_W] = (jnp.concatenate(dus, axis=1) * gelu_grad(pu, cdf_u)).astype(dproj_ref.dtype)
        dproj_ref[:, GM_W:2 * GM_W] = (jnp.concatenate(dvs, axis=1) * gelu_grad(pv, cdf_v)).astype(dproj_ref.dtype)

        halo8 = jnp.where(first, 0.0, halo_ref[...])
        q = _ssd_common(xbc_ref[...], halo8, dt_ref[...], cw_ref, cb_ref[...], dtb_ref[...], alog_ref[...], e_ref,
                        ltri_ref[...])
        act = q["act"]
        xv = act[:, 0:SSD_W]
        dt_x, ecs_x, dec_x, cdec_x = q["dt_x"], q["ecs_x"], q["dec_x"], q["cdec_x"]
        xdt = xv * dt_x
        xdt_m = _mx(xdt)
        cs = q["cs"]
        cst_ref[...] = cs.T
        s_prev = st_ref[...]
        ds = ds_ref[...]
        yv = y_ref[...]
        zv = z_ref[...]
        sig_z = _sigmoid(zv)
        sz = zv * sig_z
        yg = yv * sz
        dygs, dng8 = [], []
        for g in range(SSD_G):
            gs = slice(g * 512, (g + 1) * 512)
            a_, b_ = _rms_bwd(yg[:, gs], ng_ref[:, gs], dcat_v[:, GM_W + g * 512:GM_W + (g + 1) * 512])
            dygs.append(a_)
            dng8.append(b_)
        dyg = jnp.concatenate(dygs, axis=1)
        dng_ref[...] += jnp.concatenate(dng8, axis=1)
        dyv = dyg * sz
        dproj_ref[:, 2 * GM_W:2 * GM_W + SSD_W] = (dyg * yv * _silu_grad(sig_z, sz)).astype(dproj_ref.dtype)
        ddacc_ref[...] += _sum8(dyv * xv)
        dyv_m = _mx(dyv)

        dxdt_parts, db_parts, dc_parts = [], [], []
        dcs = jnp.zeros((CH, 128), _F32)
        dcs_x_parts, ddec_x_parts, dcl_x_parts = [], [], []
        for g in range(SSD_G):
            gs = slice(g * 512, (g + 1) * 512)
            bg = _mx(act[:, SSD_W + g * SSD_N:SSD_W + (g + 1) * SSD_N])
            cg = _mx(act[:, SSD_W + SSD_G * SSD_N + g * SSD_N:SSD_W + SSD_G * SSD_N + (g + 1) * SSD_N])
            cbm = _dot_nt(cg, bg)
            sg = s_prev[:, gs]
            sg_m = _mx(sg)
            dsg = ds[:, gs]
            dsg_m = _mx(dsg)
            zoff = _dot(cg, sg_m)
            dz_off = dyv[:, gs] * ecs_x[:, gs]
            dz_off_m = _mx(dz_off)
            dcs_x_parts.append(dyv[:, gs] * zoff * ecs_x[:, gs])
            dcg = _dot_nt(dz_off_m, sg_m)
            dsprev = _dot_tn(cg, dz_off_m)
            w_st = dec_x[:, gs] * xdt[:, gs]
            dw_st = _dot(bg, dsg_m)
            dbg = _dot_nt(_mx(w_st), dsg_m)
            dxdt_g = dec_x[:, gs] * dw_st
            ddec_x_parts.append(dw_st * xdt[:, gs])
            dsprev = dsprev + cdec_x[:, gs] * dsg
            dcl_x_parts.append(jnp.sum(dsg * sg, axis=0, keepdims=True) * cdec_x[:, gs])
            ds_ref[:, gs] = dsprev
            dcb = jnp.zeros((CH, CH), _F32)
            dxdt_pairs = []
            for pr in range(4):
                ps = slice(g * 512 + pr * 128, g * 512 + (pr + 1) * 128)
                acc_pair = None
                for hh in range(2):
                    h = g * 8 + pr * 2 + hh
                    in_head = (lane < SSD_P) if hh == 0 else (lane >= SSD_P)
                    lm = _head_lm(cs, cst_ref, h, tril)
                    m_h = cbm * lm
                    m_hm = _mx(m_h)
                    dyh_m = _mx(jnp.where(in_head, dyv[:, ps], 0.0))
                    dm = _dot_nt(dyh_m, xdt_m[:, ps])
                    dcb = dcb + dm * lm
                    qm = dm * m_h
                    dcs = dcs + jnp.where(lane == h, jnp.sum(qm, axis=1, keepdims=True), 0.0)
                    dcst_ref[h:h + 1, :] = jnp.sum(qm, axis=0, keepdims=True)
                    contrib = jnp.where(in_head, _dot_tn(m_hm, dyv_m[:, ps]), 0.0)
                    acc_pair = contrib if acc_pair is None else acc_pair + contrib
                dxdt_pairs.append(acc_pair)
            dxdt_parts.append(dxdt_g + jnp.concatenate(dxdt_pairs, axis=1))
            dcb_m = _mx(dcb)
            dc_parts.append(dcg + _dot(dcb_m, bg))
            db_parts.append(dbg + _dot_tn(dcb_m, cg))
        dxdt = jnp.concatenate(dxdt_parts, axis=1)
        dxv = dx_ref[...] * dyv + dxdt * dt_x
        et = et_ref[...]
        head_sum = lambda v: _dot(v.astype(_BF16), et)
        ddt = head_sum(dxdt * xv)
        dcs = dcs - dcst_ref[...].T + head_sum(jnp.concatenate(dcs_x_parts, axis=1))
        ddec = head_sum(jnp.concatenate(ddec_x_parts, axis=1)) * q["dec"]
        dcs = dcs - ddec
        dcl = jnp.sum(ddec, axis=0, keepdims=True) + _xdot(
            _dot, jnp.broadcast_to(jnp.concatenate(dcl_x_parts, axis=1), (8, SSD_W)), et)[0:1, :]
        dcs = jnp.where(row == CH - 1, dcs + dcl, dcs)
        da = _xdot_left(_dot_tn, ltri_ref[...], dcs)
        ddt = ddt + da * q["a_neg"]
        dalog_ref[...] += _sum8(da * q["dt"] * q["a_neg"])
        ddtraw = jnp.where(lane < SSD_H, ddt * _sigmoid(q["dtin"]), 0.0)
        ddtb_ref[...] += _sum8(ddtraw)
        dproj_ref[:, D_IN_PAD - 128:D_IN_PAD] = ddtraw.astype(dproj_ref.dtype)
        dcpre = jnp.concatenate([dxv] + db_parts + dc_parts, axis=1) * _silu_grad(q["sig"], act)
        dcb_ref[...] += _sum8(dcpre)
        for k in range(CONV_K):
            dcw_ref[k:k + 1, :] += jnp.sum(dcpre * q["xs"][CONV_K - 1 - k], axis=0, keepdims=True)
        next8 = dnext_ref[...]
        dxbc = sum(cw_ref[k:k + 1, :] * _shift_up(dcpre, next8, CONV_K - 1 - k) for k in range(CONV_K))
        dproj_ref[:, 2 * GM_W + SSD_W:2 * GM_W + SSD_W + CONV_CH] = dxbc.astype(dproj_ref.dtype)
        dnext_ref[...] = dcpre[0:8, :]

        @pl.when(i == n_chunks - 1)
        def _():
            for h in range(GM_H):
                dbs_ref[h:h + 1, :] = _xdot_left(_dot_nt, jnp.ones((8, 128), _BF16), dbacc_ref[h])[0:1, :]
            dd_ref[...] = _xdot(_dot, ddacc_ref[...], et)

    rblk = lambda w, j: pl.BlockSpec((CH, w), lambda i: (n_chunks - 1 - i, j))
    full = lambda arr: pl.BlockSpec(arr.shape, lambda i: (0,) * arr.ndim)
    acc = lambda shape: pl.BlockSpec(shape, lambda i: (0,) * len(shape))
    consts = [gv, ws, bsb, gout, cw8, cb, dtb, alog, d_x, ng, e_mat, et_mat, ltri_mat]
    acc_shapes = [(GM_H, CH, CH), (8, 128), (8, GM_W), (8, GM_W), (8, SSD_W), (8, CONV_CH), (8, CONV_CH), (8, 128),
                  (8, 128), (8, 128)]
    return pl.pallas_call(
        body, name="mixer_bwd", grid=(n_chunks,),
        in_specs=[rblk(GM_W, 0), rblk(GM_W, 1), rblk(SSD_W, 2), rblk(CONV_CH, 2), rblk(128, DT_BLK),
                  pl.BlockSpec((8, CONV_CH), lambda i: (jnp.maximum((n_chunks - 1 - i) * (CH // 8) - 1, 0), 2)),
                  rblk(2 * D, 0), rblk(SSD_W, 0), rblk(SSD_W, 0)] + [full(a) for a in consts],
        out_specs=[rblk(D_IN_PAD, 0)] + [acc(s) for s in acc_shapes],
        out_shape=[jax.ShapeDtypeStruct((t, D_IN_PAD), _MXU)] + [jax.ShapeDtypeStruct(s, _F32) for s in acc_shapes],
        scratch_shapes=[pltpu.VMEM((SSD_N, SSD_W), _F32), pltpu.VMEM((8, CONV_CH), _F32),
                        pltpu.VMEM((128, CH), _F32), pltpu.VMEM((128, CH), _F32),
                        pltpu.VMEM((GM_H, CH, 128), _F32), pltpu.VMEM((8, SSD_W), _F32)],
        compiler_params=pltpu.CompilerParams(dimension_semantics=("arbitrary",), vmem_limit_bytes=48 << 20),
    )(proj, proj, proj, proj, proj, proj, dcat, yss, states, *consts)


def _peers():
    x, y, c = lax.axis_index("x"), lax.axis_index("y"), lax.axis_index("c")
    out = []
    for k in range(1, N_DEV):
        fx, fy, fc = (k >> 2) & 1, (k >> 1) & 1, k & 1
        px, py, pc = (x + fx) % 2, (y + fy) % 2, (c + fc) % 2
        out.append((k - 1, (px, py, pc), 4 * px + 2 * py + pc))
    return out, 4 * x + 2 * y + c


def _comm_io(comm):
    any_spec = pl.BlockSpec(memory_space=pl.ANY)
    n = len(comm)
    out_shape = []
    for (kind, axis), src in comm:
        shp = list(src.shape)
        if kind in ("gather", "gather2"):
            shp[axis] *= N_DEV
        else:
            shp[axis] //= N_DEV
            shp = [N_DEV] + shp
        out_shape.append(jax.ShapeDtypeStruct(tuple(shp), src.dtype))
    scratch = [pltpu.SemaphoreType.DMA((n * (N_DEV - 1),)), pltpu.SemaphoreType.DMA((n * (N_DEV - 1),)),
               pltpu.SemaphoreType.DMA((n,))] if n else []
    return [src for _, src in comm], [any_spec] * n, [any_spec] * n, out_shape, scratch


def _window(ref, axis, idx, size):
    start = pl.multiple_of(idx * size, size)
    return ref.at[tuple(pl.ds(start, size) if a == axis else slice(None) for a in range(len(ref.shape)))]


def _comm_plans(kinds, src_refs, dst_refs, send_sems, recv_sems, local_sems):
    x, y, c = lax.axis_index("x"), lax.axis_index("y"), lax.axis_index("c")
    peers, me = _peers()
    plans = []
    for s, ((kind, axis), src, dst) in enumerate(zip(kinds, src_refs, dst_refs)):
        sems = lambda k: dict(send_sem=send_sems.at[s * (N_DEV - 1) + k], recv_sem=recv_sems.at[s * (N_DEV - 1) + k])
        remote = lambda src_ref, dst_ref, k, pid: pltpu.make_async_remote_copy(
            src_ref=src_ref, dst_ref=dst_ref, device_id=pid, device_id_type=_MESH, **sems(k))
        if kind == "gather2":
            size = src.shape[axis]
            win = lambda idx: _window(dst, axis, idx, size)
            sib, sib_idx = (x, y, 1 - c), 4 * x + 2 * y + (1 - c)
            local = pltpu.make_async_copy(src, win(me), local_sems.at[s])
            to_sib = remote(src, win(me), 0, sib)
            starts, forwards = [local, to_sib], []
            waits = [(local, "local"), (to_sib, "send"), (remote(src, win(sib_idx), 0, sib), "recv")]
            for j, (fx, fy) in enumerate(((1, 0), (0, 1), (1, 1))):
                px, py = (x + fx) % 2, (y + fy) % 2
                same, other = 4 * px + 2 * py + c, 4 * px + 2 * py + (1 - c)
                out = remote(src, win(me), 1 + j, (px, py, c))
                starts.append(out)
                passed = remote(win(same), win(same), 4 + j, sib)
                forwards.append((remote(src, win(same), 1 + j, (px, py, c)), passed))
                waits += [(out, "send"), (passed, "send"), (remote(win(other), win(other), 4 + j, sib), "recv")]
            plans.append((starts, forwards, waits))
            continue
        if kind == "gather":
            size = src.shape[axis]
            src_for = lambda pidx: src
            dst_mine = _window(dst, axis, me, size)
        else:
            size = src.shape[axis] // N_DEV
            src_for = lambda pidx: _window(src, axis, pidx, size)
            dst_mine = dst.at[me]
        local = pltpu.make_async_copy(src_for(me), dst_mine, local_sems.at[s])
        remotes = [remote(src_for(pidx), dst_mine, k, pid) for k, pid, pidx in peers]
        plans.append(([local] + remotes, [], [(local, "local")] + [(cp, "both") for cp in remotes]))
    return plans


def _comm_start(*refs):
    for starts, _, _ in _comm_plans(*refs):
        for cp in starts:
            cp.start()


def _comm_finish(*refs):
    for _, forwards, waits in _comm_plans(*refs):
        for arrival, cp in forwards:
            arrival.wait_recv()
            cp.start()
        for cp, what in waits:
            if what == "send":
                cp.wait_send()
            elif what == "recv":
                cp.wait_recv()
            else:
                cp.wait()


def _adam_vals(w, g, m, v):
    m = B1 * m + (1.0 - B1) * g
    v = B2 * v + (1.0 - B2) * (g * g)
    m_hat = m / (1.0 - B1 ** STEP)
    v_hat = v / (1.0 - B2 ** STEP)
    delta = -LR * (m_hat / (jnp.sqrt(v_hat) + ADAM_EPS) + WD * w)
    return delta, m, v


def _sum_adam(name, recv, w, m, v, tile=256):
    _, r, wd = recv.shape
    if r % min(tile, r) == 0:
        tr, tc = min(tile, r), wd
    else:
        tr, tc = r, tile
        assert wd % tc == 0, (name, r, wd)

    def body(recv_ref, w_ref, m_ref, v_ref, g_out, d_out, m_out, v_out):
        g = recv_ref[0].astype(_F32)
        for s in range(1, N_DEV):
            g = g + recv_ref[s].astype(_F32)
        d_, m_, v_ = _adam_vals(w_ref[...], g, m_ref[...], v_ref[...])
        g_out[...] = g
        d_out[...] = d_
        m_out[...] = m_
        v_out[...] = v_

    spec = pl.BlockSpec((tr, tc), lambda i, j: (i, j))
    return pl.pallas_call(
        body, name=name, grid=(r // tr, wd // tc),
        in_specs=[pl.BlockSpec((N_DEV, tr, tc), lambda i, j: (0, i, j)), spec, spec, spec],
        out_specs=[spec] * 4, out_shape=[jax.ShapeDtypeStruct((r, wd), _F32)] * 4,
        compiler_params=pltpu.CompilerParams(dimension_semantics=("parallel", "parallel"),
                                             vmem_limit_bytes=48 << 20),
    )(recv, w, m, v)


def _small_local(parts, segments, n_rows):
    def body(parts_ref, loc_ref):
        loc_ref[...] = jnp.zeros_like(loc_ref)
        for out_row, n_out, in_row, n_in, kind in segments:
            if kind == "copy":
                loc_ref[out_row:out_row + n_out, :] = parts_ref[in_row:in_row + n_in, :]
            else:
                s = jnp.sum(parts_ref[in_row:in_row + n_in, :], axis=0, keepdims=True)
                if kind == "loss":
                    s = jnp.broadcast_to(jnp.sum(s, axis=1, keepdims=True) * (0.5 / D), (1, D))
                loc_ref[out_row:out_row + 1, :] = s

    vm = pl.BlockSpec(memory_space=pltpu.VMEM)
    return pl.pallas_call(body, name="small_local", in_specs=[vm], out_specs=vm,
                          out_shape=jax.ShapeDtypeStruct((n_rows, D), _F32))(parts)


def _small_final(blocks, late8, late_row, w, m, v):
    n_rows = w.shape[0]

    def body(blocks_ref, late_ref, w_ref, m_ref, v_ref, g_out, d_out, m_out, v_out, loc_ref, recv_ref, send_sems,
             recv_sems):
        peers, me = _peers()
        loc_ref[...] = jnp.broadcast_to(jnp.sum(late_ref[...], axis=0, keepdims=True), (8, D))
        recv_ref[me] = loc_ref[...]
        copies = [pltpu.make_async_remote_copy(src_ref=loc_ref, dst_ref=recv_ref.at[me], send_sem=send_sems.at[k],
                                               recv_sem=recv_sems.at[k], device_id=pid, device_id_type=_MESH)
                  for k, pid, _ in peers]
        for cp in copies:
            cp.start()
        g = blocks_ref[0:n_rows, :]
        for s in range(1, N_DEV):
            g = g + blocks_ref[s * n_rows:(s + 1) * n_rows, :]
        for cp in copies:
            cp.wait()
        late = recv_ref[0]
        for s in range(1, N_DEV):
            late = late + recv_ref[s]
        g = jnp.where(_iota2((n_rows, D), 0) == late_row, jnp.broadcast_to(late[0:1, :], (n_rows, D)), g)
        d_, m_, v_ = _adam_vals(w_ref[...], g, m_ref[...], v_ref[...])
        g_out[...] = g
        d_out[...] = d_
        m_out[...] = m_
        v_out[...] = v_

    vm = pl.BlockSpec(memory_space=pltpu.VMEM)
    return pl.pallas_call(
        body, name="small_final", in_specs=[vm] * 5, out_specs=[vm] * 4,
        out_shape=[jax.ShapeDtypeStruct((n_rows, D), _F32)] * 4,
        scratch_shapes=[pltpu.VMEM((8, D), _F32), pltpu.VMEM((N_DEV, 8, D), _F32),
                        pltpu.SemaphoreType.DMA((N_DEV - 1,)), pltpu.SemaphoreType.DMA((N_DEV - 1,))],
        compiler_params=pltpu.CompilerParams(vmem_limit_bytes=48 << 20),
    )(blocks, late8, w, m, v)


_BIG_NAMES =("w_in", "w_out", "w_ff1", "w_ff2", "w_ple_gate", "w_ple_proj")

_G_VECS = ("norm_mix_g", "gm_v_norm_g", "gm_out_norm_g", "ssd_norm_g", "norm_mlp_g", "ple_norm_g", "final_norm_g")
_LATE = _G_VECS[0]


def _const_mats():
    h = np.arange(128)[:, None]
    ch = np.arange(SSD_W)[None, :]
    e = (ch // SSD_P == h).astype(np.float32)
    ltri = (np.arange(CH)[:, None] >= np.arange(CH)[None, :]).astype(np.float32)
    return jnp.asarray(e, _BF16), jnp.asarray(e.T, _BF16), jnp.asarray(ltri, _BF16)


def _pad_lanes(v, n=128):
    return jnp.pad(v, ((0, 0), (0, n - v.shape[1])))


def _local_step(x, p, tgt, shard, conv_w_shard, small, seq_len):
    seq_chunks = seq_len // CH
    e_mat, et_mat, ltri_mat = _const_mats()
    g_mix, g_mlp, g_ple = small["norm_mix_g"], small["norm_mlp_g"], small["ple_norm_g"]
    g_fin = small["final_norm_g"].reshape(1, D)
    gv, gout, ng = small["gm_v_norm_g"], small["gm_out_norm_g"], small["ssd_norm_g"]
    ws = small["gm_ws"][0]
    bsb = jnp.broadcast_to(small["gm_bs"][0][:, :, None], (GM_H, CH, 128))
    cb = small["ssd_conv_b"]
    dtb, alog = _pad_lanes(small["ssd_dt_bias"]), _pad_lanes(small["ssd_a_log"])
    d_x = jnp.repeat(small["ssd_d"], SSD_P, axis=1)

    first = lambda acc: (acc,)
    rows, cols = ("gather2", 0), ("gather2", 1)
    n1, (g_win, g_cw) = _norm_cast("norm_mix", x, g_mix,
                                   comm=[(rows, shard["w_in"][None]), (("gather", 0), conv_w_shard[None])])
    w_in_t = jnp.pad(g_win.reshape(D_IN, D), ((0, D_IN_PAD - D_IN), (0, 0)))
    cw8 = jnp.pad(g_cw.transpose(1, 0, 2).reshape(CONV_K, CONV_CH), ((0, 8 - CONV_K), (0, 0)))
    mix_consts = (gv, ws, bsb, gout, cw8, cb, dtb, alog, d_x, ng)
    (proj,), (w_out, w1) = _matmul("proj_in", n1, w_in_t, "nt", 256, D_IN_PAD, D, first, [("tile", _F32)],
                                   comm=[(rows, shard["w_out"]), (cols, shard["w_ff1"])])
    (cat, yss, states), (w2, wg, wp) = _mixer_fwd(
        proj, *mix_consts, e_mat, ltri_mat, seq_chunks,
        comm=[(rows, shard["w_ff2"]), (rows, shard["w_ple_gate"]), (cols, shard["w_ple_proj"])])

    def epi_res_norm(acc, res, g):
        hv = acc + res
        return hv, _rms(hv, g)

    (h1, n2), _ = _matmul("proj_out", cat, w_out, "nn", 1024, D, 2 * D, epi_res_norm,
                          [("tile", _F32), ("tile", _MXU)], extras=[(x, "tile"), (g_mlp, "row")])

    def epi_relu2(acc):
        hid = jnp.maximum(acc, 0.0)
        return hid, hid * hid

    (hid, hid2), _ = _matmul("ff1", n2, w1, "nn", 1024, 1024, D, epi_relu2, [("tile", _MXU), ("tile", _MXU)])
    (h2, n3), _ = _matmul("ff2", hid2, w2, "nn", 512, D, D_FF, epi_res_norm, [("tile", _F32), ("tile", _MXU)],
                          extras=[(h1, "tile"), (g_ple, "row")])

    def epi_norm_bwd(acc, up, hv, g):
        dx, dg8 = _rms_bwd(hv, g, acc)
        dh = up + dx
        return dh, dh, dg8

    def epi_head(acc, wg_v, p_rows, wp_v, h2v, tg, gf, gp_):
        ppv = _dot(_mx(p_rows), wp_v)
        gate = _sigmoid(acc)
        gp = gate * ppv
        h3 = h2v + gp
        r = lax.rsqrt(jnp.mean(h3 * h3, axis=-1, keepdims=True) + EPS)
        nh = h3 * r
        err = nh * gf - tg
        gy = err * (gf * (1.0 / D))
        dh3 = (gy - nh * jnp.mean(gy * nh, axis=-1, keepdims=True)) * r
        dpp = dh3 * gate
        da3 = dpp * (ppv - gp)
        dh2, dh2_again, dgple8 = epi_norm_bwd(_dot_nt(_mx(da3), wg_v), dh3, h2v, gp_)
        return da3, dpp, dh2, dh2_again, _sum8(err * err), _sum8(err * nh) * (1.0 / D), dgple8

    (da3, dpp, dh2, dh2b, lossp, dgfin, dgple), _ = _matmul(
        "ple_gate_loss_bwd", n3, wg, "nn", 512, D, D, epi_head,
        [("tile", _MXU), ("tile", _MXU), ("tile", _F32), ("tile", _MXU), ("part8", _F32), ("part8", _F32),
         ("part8", _F32)],
        extras=[(p, "rows"), (wp, "full"), (h2, "tile"), (tgt, "tile"), (g_fin, "row"), (g_ple, "row")],
        b_to_epilogue=True)

    s_rows, s_cols = ("scatter", 0), ("scatter", 1)
    (dwp,), _ = _matmul("d_w_ple_proj", p, dpp, "tn", D_PLE, D, 2048, first, [("tile", _BF16)])
    (dwg,), _ = _matmul("d_w_ple_gate", n3, da3, "tn", D, D, 2048, first, [("tile", _BF16)])
    (dw2,), (r_wp, r_wg) = _matmul("d_w_ff2", hid2, dh2b, "tn", 1024, D, 2048, first, [("tile", _BF16)],
                                   comm=[(s_cols, dwp), (s_rows, dwg)])
    (da1,), (r_w2,) = _matmul("d_ff_hidden", dh2b, w2, "nt", 512, 2048, D,
                              lambda acc, hv: (acc * 2.0 * hv.astype(_F32),), [("tile", _MXU)],
                              extras=[(hid, "tile")], comm=[(s_rows, dw2)])
    (dw1,), _ = _matmul("d_w_ff1", n2, da1, "tn", 1024, 1024, 2048, first, [("tile", _BF16)])
    (dh1, dh1b, dgmlp), (r_w1,) = _matmul(
        "d_h1", da1, w1, "nt", 256, D, D_FF, epi_norm_bwd, [("tile", _F32), ("tile", _MXU), ("part8", _F32)],
        extras=[(dh2, "tile"), (h1, "tile"), (g_mlp, "row")], comm=[(s_cols, dw1)])
    (dwout,), _ = _matmul("d_w_out", cat, dh1b, "tn", 1024, D, 2048, first, [("tile", _BF16)])
    (dcat,), (r_wout,) = _matmul("d_cat", dh1b, w_out, "nt", 1024, 1024, D, first, [("tile", _MXU)],
                                 comm=[(s_rows, dwout)])
    (dproj, dws, dbs, dgv, dgout, dng, dcw, dcb, ddtb, dalog, dd) = _mixer_bwd(
        proj, dcat, yss, states, *mix_consts, e_mat, et_mat, ltri_mat, seq_chunks)
    pieces = dict(gm_v_norm_g=dgv, gm_out_norm_g=dgout, ssd_norm_g=dng, norm_mlp_g=dgmlp, ple_norm_g=dgple,
                  final_norm_g=dgfin, gm_ws=dws, gm_bs=dbs, ssd_conv_w=dcw, ssd_conv_b=dcb, ssd_dt_bias=ddtb,
                  ssd_a_log=dalog, ssd_d=dd, loss=lossp)
    parts, segments, n_rows, where = _small_layout(pieces)
    small_block = _small_local(parts, segments, n_rows)
    (dwin_t,), (small_blocks,) = _matmul("d_w_in", n1, dproj, "tn", 512, D_IN_PAD, 1024, lambda acc: (acc.T,),
                                         [("tile_t", _BF16)], comm=[(("gather", 0), small_block)])
    dwin_blocks = dwin_t[:D_IN].reshape(N_DEV, SHARD_IN, D)
    (gx, dgmix), (r_win,) = _matmul(
        "d_x", dproj, w_in_t, "nn", 256, D, D_IN_PAD, lambda *a: epi_norm_bwd(*a)[1:],
        [("tile", _F32), ("part8", _F32)], extras=[(dh1, "tile"), (x, "tile"), (g_mix, "row")],
        comm=[(s_rows, dwin_blocks)])
    r_win = r_win.reshape(N_DEV, SHARD_IN, D)

    big = dict(w_in=r_win, w_out=r_wout, w_ff1=r_w1, w_ff2=r_w2, w_ple_gate=r_wg, w_ple_proj=r_wp)
    return gx, big, small_blocks, dgmix, n_rows, where


def _small_layout(pieces):
    rows, segments = [], []
    in_row, out_row = 0, 0

    def add(arr, kind, n_out):
        nonlocal in_row, out_row
        rows.append(arr)
        segments.append((out_row, n_out, in_row, arr.shape[0], kind))
        start = out_row
        in_row += arr.shape[0]
        out_row += n_out
        return start

    where = {_LATE: 0}
    out_row = 1
    for name in _G_VECS[1:]:
        where[name] = add(pieces[name], "sum", 1)
    where["gm_ws"] = add(pieces["gm_ws"].reshape(GM_H * CH * CH // D, D), "copy", GM_H * CH * CH // D)
    where["gm_bs"] = add(pieces["gm_bs"].reshape(1, D), "copy", 1)
    cb = jnp.pad(pieces["ssd_conv_b"], ((0, 0), (0, 2 * D - CONV_CH)))
    where["ssd_conv_b"] = add(cb[:, :D], "sum", 1)
    add(cb[:, D:], "sum", 1)
    cw = jnp.pad(pieces["ssd_conv_w"][:CONV_K], ((0, 0), (0, 2 * D - CONV_CH)))
    where["ssd_conv_w"] = add(cw.reshape(2 * CONV_K, D), "copy", 2 * CONV_K)
    misc = jnp.concatenate([pieces["ssd_dt_bias"], pieces["ssd_a_log"], pieces["ssd_d"],
                            jnp.zeros((8, D - 3 * 128), _F32)], axis=1)
    where["misc"] = add(misc, "sum", 1)
    where["loss"] = add(pieces["loss"], "loss", 1)
    n_rows = -(-out_row // 8) * 8
    return jnp.concatenate(rows, axis=0), tuple(segments), n_rows, where


def _pack_small_params(vals, where, n_rows, my_block):
    rows, at = [], {}

    def add(name, arr):
        at[name] = sum(r.shape[0] for r in rows)
        rows.append(arr)

    for name in _G_VECS:
        add(name, vals[name].reshape(1, D))
    add("gm_ws", vals["gm_ws"].reshape(GM_H * CH * CH // D, D))
    add("gm_bs", vals["gm_bs"].reshape(1, D))
    cb = jnp.pad(vals["ssd_conv_b"].reshape(1, CONV_CH), ((0, 0), (0, 2 * D - CONV_CH)))
    add("ssd_conv_b", cb.reshape(2, D))
    cw = lax.dynamic_update_slice(jnp.zeros((CONV_K, 2 * D), _F32), vals["ssd_conv_w"].reshape(CONV_K, -1),
                                  (0, my_block * (CONV_CH // N_DEV)))
    add("ssd_conv_w", cw.reshape(2 * CONV_K, D))
    misc = jnp.concatenate([_pad_lanes(vals["ssd_dt_bias"].reshape(1, SSD_H)),
                            _pad_lanes(vals["ssd_a_log"].reshape(1, SSD_H)),
                            _pad_lanes(vals["ssd_d"].reshape(1, SSD_H)), jnp.zeros((1, D - 3 * 128), _F32)], axis=1)
    add("misc", misc)
    assert all(where[k] == r for k, r in at.items()), (where, at)
    rows.append(jnp.zeros((n_rows - sum(r.shape[0] for r in rows), D), _F32))
    return jnp.concatenate(rows, axis=0)


def _unpack_small(buf, where, my_block, shapes):
    out = {}
    for name in _G_VECS:
        out[name] = buf[where[name]].reshape(shapes[name])
    n_ws = GM_H * CH * CH // D
    out["gm_ws"] = buf[where["gm_ws"]:where["gm_ws"] + n_ws].reshape(shapes["gm_ws"])
    out["gm_bs"] = buf[where["gm_bs"]].reshape(shapes["gm_bs"])
    r = where["ssd_conv_b"]
    out["ssd_conv_b"] = buf[r:r + 2].reshape(1, 2 * D)[:, :CONV_CH].reshape(shapes["ssd_conv_b"])
    r = where["ssd_conv_w"]
    cw = buf[r:r + 2 * CONV_K].reshape(CONV_K, 2 * D)
    out["ssd_conv_w"] = lax.dynamic_slice(cw, (0, my_block * (CONV_CH // N_DEV)),
                                          (CONV_K, CONV_CH // N_DEV)).reshape(shapes["ssd_conv_w"])
    misc = buf[where["misc"]]
    for i, name in enumerate(("ssd_dt_bias", "ssd_a_log", "ssd_d")):
        out[name] = misc[i * 128:i * 128 + SSD_H].reshape(shapes[name])
    return out


_WEIGHTS = ("norm_mix_g", "w_in", "gm_v_norm_g", "gm_ws", "gm_bs", "gm_out_norm_g", "ssd_conv_w", "ssd_conv_b",
            "ssd_dt_bias", "ssd_a_log", "ssd_d", "ssd_norm_g", "w_out", "norm_mlp_g", "w_ff1", "w_ff2", "ple_norm_g",
            "w_ple_gate", "w_ple_proj", "final_norm_g")


def kernel(x, p, norm_mix_g, w_in, gm_v_norm_g, gm_ws, gm_bs, gm_out_norm_g, ssd_conv_w, ssd_conv_b, ssd_dt_bias, ssd_a_log, ssd_d, ssd_norm_g, w_out, norm_mlp_g, w_ff1, w_ff2, ple_norm_g, w_ple_gate, w_ple_proj, final_norm_g, loss_target, m_norm_mix_g, m_w_in, m_gm_v_norm_g, m_gm_ws, m_gm_bs, m_gm_out_norm_g, m_ssd_conv_w, m_ssd_conv_b, m_ssd_dt_bias, m_ssd_a_log, m_ssd_d, m_ssd_norm_g, m_w_out, m_norm_mlp_g, m_w_ff1, m_w_ff2, m_ple_norm_g, m_w_ple_gate, m_w_ple_proj, m_final_norm_g, v_norm_mix_g, v_w_in, v_gm_v_norm_g, v_gm_ws, v_gm_bs, v_gm_out_norm_g, v_ssd_conv_w, v_ssd_conv_b, v_ssd_dt_bias, v_ssd_a_log, v_ssd_d, v_ssd_norm_g, v_w_out, v_norm_mlp_g, v_w_ff1, v_w_ff2, v_ple_norm_g, v_w_ple_gate, v_w_ple_proj, v_final_norm_g):
    args = dict(locals())
    w = {n: args[n] for n in _WEIGHTS}
    m = {n: args["m_" + n] for n in _WEIGHTS}
    v = {n: args["v_" + n] for n in _WEIGHTS}
    shapes = {n: w[n].shape for n in _WEIGHTS}
    my_block = 4 * lax.axis_index("x") + 2 * lax.axis_index("y") + lax.axis_index("c")
    nb, seq_len, _ = x.shape

    local = lambda d, n: d[n][0].T if n == "w_in" else d[n][0]
    shard = {n: local(w, n).astype(_MXU) for n in _BIG_NAMES}
    small = {n: w[n] for n in _WEIGHTS if n not in _BIG_NAMES}
    gx, recv, small_blocks, late8, n_rows, where = _local_step(
        x.reshape(nb * seq_len, D), p.reshape(nb * seq_len, D_PLE), loss_target.reshape(nb * seq_len, D), shard,
        ssd_conv_w[0], small, seq_len)

    big_out = [{}, {}, {}, {}]
    for n in _BIG_NAMES:
        res = _sum_adam("sum_adam_" + n, recv[n], local(w, n), local(m, n), local(v, n))
        for k in range(4):
            big_out[k][n] = (res[k].T if n == "w_in" else res[k]).reshape(shapes[n])

    packs = [_pack_small_params(d, where, n_rows, my_block) for d in (w, m, v)]
    small_res = _small_final(small_blocks, late8, where[_LATE], *packs)
    loss = small_res[0][where["loss"], 0]
    small_out = [_unpack_small(a, where, my_block, shapes) for a in small_res]

    outs = [loss, gx.reshape(x.shape)]
    for k in range(4):
        outs += [big_out[k][n] if n in _BIG_NAMES else small_out[k][n] for n in _WEIGHTS]
    return tuple(outs)
```

```python
import functools
import math

import jax
import jax.numpy as jnp
import numpy as np
from jax import lax
from jax.experimental import pallas as pl
from jax.experimental.pallas import tpu as pltpu

_F32 = jnp.float32
_BF16 = jnp.bfloat16
_MXU = jnp.bfloat16

D = 1024
D_PLE = 256
GM_W = 1024
GM_H = 8
CH = 128
SSD_W = 1024
SSD_H = 16
SSD_P = 64
SSD_G = 2
SSD_N = 128
CONV_K = 4
CONV_CH = SSD_W + 2 * SSD_G * SSD_N
D_FF = 4096
D_IN = 2 * GM_W + SSD_W + CONV_CH + SSD_H
D_IN_PAD = 4736
EPS = 1e-6
DT_BLK = (D_IN_PAD - 128) // 128
N_DEV = 8
SHARD_IN = D_IN // N_DEV

LR, B1, B2, ADAM_EPS, WD, STEP = 0.001, 0.9, 0.999, 1e-08, 0.01, 10
_LOG2E = math.log2(math.e)
_LOG2_INV_SQRT_2PI = -0.5 * math.log2(2.0 * math.pi)

_SEQ_PER_STEP = 2
_V7X_VMEM_BYTES = 64 * 1024 * 1024
_VMEM_CAP = _V7X_VMEM_BYTES - 8 * 1024 * 1024
_MESH = pl.DeviceIdType.MESH


def _vmem_limit(nbytes):
    return int(min(_VMEM_CAP, max(32 * 1024 * 1024, nbytes * 5 // 4 + (4 << 20))))


def _nbytes(shape, dtype):
    return int(np.prod(shape)) * jnp.dtype(dtype).itemsize


def _mx(v):
    return v.astype(_MXU)


def _dot(a, b):
    return jnp.dot(a, b, preferred_element_type=_F32)


def _dot_nt(a, b):
    return lax.dot_general(a, b, (((1,), (1,)), ((), ())), preferred_element_type=_F32)


def _dot_tn(a, b):
    return lax.dot_general(a, b, (((0,), (0,)), ((), ())), preferred_element_type=_F32)


def _split3(a):
    hi = a.astype(_BF16)
    r = a - hi.astype(_F32)
    mid = r.astype(_BF16)
    lo = (r - mid.astype(_F32)).astype(_BF16)
    return hi, mid, lo


def _xdot(dotfn, a, b01):
    b = b01.astype(_BF16)
    hi, mid, lo = _split3(a)
    return (dotfn(hi, b) + dotfn(mid, b)) + dotfn(lo, b)


def _xdot_left(dotfn, a01, b):
    a = a01.astype(_BF16)
    hi, mid, lo = _split3(b)
    return (dotfn(a, hi) + dotfn(a, mid)) + dotfn(a, lo)


def _sum8(v):
    r, n = v.shape
    return v.reshape(r // 8, 8, n).sum(axis=0)


def _sigmoid(v):
    return 1.0 / (1.0 + jnp.exp(-v))


def _rms(xv, g):
    ms = jnp.mean(xv * xv, axis=-1, keepdims=True)
    return xv * lax.rsqrt(ms + EPS) * g


def _rms_bwd(xv, g, dn):
    r = lax.rsqrt(jnp.mean(xv * xv, axis=-1, keepdims=True) + EPS)
    nh = xv * r
    gy = dn * g
    dx = (gy - nh * jnp.mean(gy * nh, axis=-1, keepdims=True)) * r
    return dx, _sum8(dn * nh)


def _iota2(shape, axis):
    return lax.broadcasted_iota(jnp.int32, shape, axis)


def _norm_cast(name, x, g, tm=512, comm=()):
    t, n = x.shape
    tm = min(tm, t)
    steps = t // tm
    kinds = [kind for kind, _ in comm]
    c_in, c_in_specs, c_out_specs, c_out_shape, c_scratch = _comm_io(comm)

    def body(*refs):
        x_ref, g_ref = refs[0], refs[1]
        o_ref = refs[2 + len(comm)]
        comm_refs = (kinds, refs[2:2 + len(comm)], refs[3 + len(comm):3 + 2 * len(comm)], *refs[3 + 2 * len(comm):])
        if comm:
            pl.when(pl.program_id(0) == 0)(lambda: _comm_start(*comm_refs))

        o_ref[...] = _rms(x_ref[...], g_ref[...]).astype(o_ref.dtype)
        if comm:
            pl.when(pl.program_id(0) == steps - 1)(lambda: _comm_finish(*comm_refs))

    res = pl.pallas_call(
        body, name=name, grid=(steps,),
        in_specs=[pl.BlockSpec((tm, n), lambda i: (i, 0)), pl.BlockSpec((1, n), lambda i: (0, 0))] + c_in_specs,
        out_specs=[pl.BlockSpec((tm, n), lambda i: (i, 0))] + c_out_specs,
        out_shape=[jax.ShapeDtypeStruct((t, n), _MXU)] + c_out_shape, scratch_shapes=c_scratch,
        compiler_params=pltpu.CompilerParams(dimension_semantics=("arbitrary",)),
    )(x, g, *c_in)
    return res[0], res[1:]


def _matmul(name, a, b, mode, tm, tn, tk, epilogue, outs, extras=(), comm=(), b_to_epilogue=False):
    m, k = a.shape[::-1] if mode == "tn" else a.shape
    n = b.shape[0] if mode == "nt" else b.shape[1]
    tm, tn, tk = min(tm, m), min(tn, n), min(tk, k)
    assert m % tm == 0 and n % tn == 0 and k % tk == 0, (name, m, n, k, tm, tn, tk)
    if mode == "nn":
        a_spec = pl.BlockSpec((tm, tk), lambda i, j, kk: (i, kk))
        b_spec = pl.BlockSpec((tk, tn), lambda i, j, kk: (kk, j))
        dotfn = _dot
    elif mode == "nt":
        a_spec = pl.BlockSpec((tm, tk), lambda i, j, kk: (i, kk))
        b_spec = pl.BlockSpec((tn, tk), lambda i, j, kk: (j, kk))
        dotfn = _dot_nt
    else:
        a_spec = pl.BlockSpec((tk, tm), lambda i, j, kk: (kk, i))
        b_spec = pl.BlockSpec((tk, tn), lambda i, j, kk: (kk, j))
        dotfn = _dot_tn
    ni, nj, nk = m // tm, n // tn, k // tk
    n_ex, n_out, n_comm = len(extras), len(outs), len(comm)
    kinds = [kind for kind, _ in comm]
    c_in, c_in_specs, c_out_specs, c_out_shape, c_scratch = _comm_io(comm)

    in_specs, vmem = [a_spec, b_spec], 2 * (tm * tk * a.dtype.itemsize + tk * tn * b.dtype.itemsize)
    for arr, kind in extras:
        if kind == "tile":
            in_specs.append(pl.BlockSpec((tm, tn), lambda i, j, kk: (i, j)))
            vmem += 2 * _nbytes((tm, tn), arr.dtype)
        elif kind == "rows":
            in_specs.append(pl.BlockSpec((tm, arr.shape[1]), lambda i, j, kk: (i, 0)))
            vmem += 2 * _nbytes((tm, arr.shape[1]), arr.dtype)
        elif kind == "full":
            in_specs.append(pl.BlockSpec(arr.shape, lambda i, j, kk: (0,) * arr.ndim))
            vmem += 2 * _nbytes(arr.shape, arr.dtype)
        else:
            in_specs.append(pl.BlockSpec((1, tn), lambda i, j, kk: (0, j)))
    out_specs, out_shape = [], []
    for kind, dt in outs:
        if kind == "tile":
            out_specs.append(pl.BlockSpec((tm, tn), lambda i, j, kk: (i, j)))
            out_shape.append(jax.ShapeDtypeStruct((m, n), dt))
            vmem += 2 * _nbytes((tm, tn), dt)
        elif kind == "tile_t":
            out_specs.append(pl.BlockSpec((tn, tm), lambda i, j, kk: (j, i)))
            out_shape.append(jax.ShapeDtypeStruct((n, m), dt))
            vmem += 2 * _nbytes((tm, tn), dt)
        else:
            assert nj == 1, "the partial-sum rows are accumulated over consecutive row tiles"
            out_specs.append(pl.BlockSpec((8, tn), lambda i, j, kk: (0, 0)))
            out_shape.append(jax.ShapeDtypeStruct((8, n), dt))
    scratch = [pltpu.VMEM((tm, tn), _F32)] if nk > 1 else []
    vmem += _nbytes((tm, tn), _F32) * 2

    def body(*refs):
        a_ref, b_ref = refs[0], refs[1]
        ex_refs = refs[2:2 + n_ex]
        n_in = 2 + n_ex + n_comm
        out_refs = refs[n_in:n_in + n_out]
        i, j, kk = pl.program_id(0), pl.program_id(1), pl.program_id(2)
        comm_refs = (kinds, refs[2 + n_ex:n_in], refs[n_in + n_out:n_in + n_out + n_comm], *refs[len(refs) - 3:])
        if n_comm:
            pl.when((i == 0) & (j == 0) & (kk == 0))(lambda: _comm_start(*comm_refs))

        b_val = _mx(b_ref[...])
        part = dotfn(_mx(a_ref[...]), b_val)

        def finish(acc):
            vals = epilogue(acc, *([b_val] if b_to_epilogue else []), *[r[...] for r in ex_refs])
            for r, v, (kind, _) in zip(out_refs, vals, outs):
                if kind == "part8":
                    @pl.when(i == 0)
                    def _():
                        r[...] = v

                    @pl.when(i > 0)
                    def _():
                        r[...] += v
                else:
                    r[...] = v.astype(r.dtype)

        if nk == 1:
            finish(part)
        else:
            acc_ref = refs[n_in + n_out + n_comm]

            @pl.when(kk == 0)
            def _():
                acc_ref[...] = part

            @pl.when(kk > 0)
            def _():
                acc_ref[...] += part

            @pl.when(kk == nk - 1)
            def _():
                finish(acc_ref[...])

        if n_comm:
            pl.when((i == ni - 1) & (j == nj - 1) & (kk == nk - 1))(lambda: _comm_finish(*comm_refs))

    carried =n_comm or any(kind == "part8" for kind, _ in outs)
    sem = ("arbitrary",) * 3 if carried else ("parallel", "parallel", "arbitrary")
    res = pl.pallas_call(
        body, name=name, grid=(ni, nj, nk),
        in_specs=in_specs + c_in_specs, out_specs=out_specs + c_out_specs, out_shape=out_shape + c_out_shape,
        scratch_shapes=scratch + c_scratch,
        compiler_params=pltpu.CompilerParams(dimension_semantics=sem, vmem_limit_bytes=_vmem_limit(vmem)),
    )(a, b, *[arr for arr, _ in extras], *c_in)
    return res[:n_out], res[n_out:]


def _shift_down(v, halo8, j):
    if j == 0:
        return v
    r = pltpu.roll(v, j, axis=0)
    hr = pltpu.roll(halo8, j, axis=0)
    top = jnp.where(_iota2(hr.shape, 0) < j, hr, r[:8])
    return jnp.concatenate([top, r[8:]], axis=0)


def _shift_up(v, next8, j):
    if j == 0:
        return v
    rows = v.shape[0]
    r = pltpu.roll(v, rows - j, axis=0)
    nr = pltpu.roll(next8, 8 - j, axis=0)
    bot = jnp.where(_iota2(nr.shape, 0) >= 8 - j, nr, r[rows - 8:])
    return jnp.concatenate([r[:rows - 8], bot], axis=0)


def _silu_grad(sig, silu):
    return sig + silu * (1.0 - sig)


def _gmlp_fwd_vals(pu, pv, gv, ws_ref, bsb_ref, want_bwd):
    tril = _iota2((CH, CH), 0) >= _iota2((CH, CH), 1)
    cdf_u = 0.5 * (1.0 + lax.erf(pu * 0.7071067811865476))
    cdf_v = 0.5 * (1.0 + lax.erf(pv * 0.7071067811865476))
    u = pu * cdf_u
    v = pv * cdf_v
    ys, keep = [], [(cdf_u, cdf_v)] if want_bwd else []
    for h in range(GM_H):
        sl = slice(h * 128, (h + 1) * 128)
        vh = v[:, sl]
        r = lax.rsqrt(jnp.mean(vh * vh, axis=-1, keepdims=True) + EPS)
        vn = vh * r * gv[:, sl]
        wm = _mx(jnp.where(tril, ws_ref[h], 0.0))
        mixed = _dot(wm, _mx(vn)) + bsb_ref[h]
        ys.append(u[:, sl] * mixed)
        if want_bwd:
            keep.append((vh, r, vn, wm, mixed))
    return jnp.concatenate(ys, axis=1), u, keep


def _ssd_common(xbc, halo8, dtraw, cw_ref, cb, dtb, alog, e_ref, ltri):
    xs = [_shift_down(xbc, halo8, j) for j in range(CONV_K)]
    cpre = cb + sum(cw_ref[k:k + 1, :] * xs[CONV_K - 1 - k] for k in range(CONV_K))
    sig = _sigmoid(cpre)
    act = cpre * sig
    dtin = dtraw + dtb
    dt = jnp.maximum(dtin, 0.0) + jnp.log(1.0 + jnp.exp(-jnp.abs(dtin)))
    a_neg = -jnp.exp(alog)
    cs = _xdot_left(_dot, ltri, dt * a_neg)
    cs_last = cs[CH - 1:CH, :]
    ecs = jnp.exp(cs)
    dec = jnp.exp(cs_last - cs)
    cdec = jnp.exp(cs_last)
    e = e_ref[...]
    dt_x = _dot(dt.astype(_BF16), e)
    ecs_x = _dot(ecs.astype(_BF16), e)
    dec_x = _dot(dec.astype(_BF16), e)
    cdec_x = _xdot(_dot, jnp.broadcast_to(cdec, (8, 128)), e)[0:1, :]
    return dict(xs=xs, sig=sig, act=act, dtin=dtin, dt=dt, a_neg=a_neg, cs=cs, ecs=ecs, dec=dec, cdec=cdec,
                dt_x=dt_x, ecs_x=ecs_x, dec_x=dec_x, cdec_x=cdec_x)


def _head_lm(cs, cst_ref, h, tril):
    seg = jnp.broadcast_to(cs[:, h:h + 1], (CH, CH)) - cst_ref[h:h + 1, :]
    return jnp.exp(jnp.where(tril, seg, -jnp.inf))


def _mixer_fwd(proj, gv, ws, bsb, gout, cw8, cb, dtb, alog, d_x, ng, e_mat, ltri_mat, seq_chunks, comm=()):
    t = proj.shape[0]
    n_seq = t // (seq_chunks * CH)
    sb = math.gcd(_SEQ_PER_STEP, n_seq)
    n_groups = n_seq // sb
    proj = proj.reshape(n_seq, seq_chunks * CH, proj.shape[1])
    n_comm = len(comm)
    kinds = [kind for kind, _ in comm]
    c_in, c_in_specs, c_out_specs, c_out_shape, c_scratch = _comm_io(comm)

    def body(*refs):
        comm_refs = (kinds, refs[18:18 + n_comm], refs[21 + n_comm:21 + 2 * n_comm], *refs[23 + 2 * n_comm:])
        grp, c = pl.program_id(0), pl.program_id(1)
        if n_comm:
            pl.when((grp == 0) & (c == 0))(lambda: _comm_start(*comm_refs))
            pl.when((grp == n_groups - 1) & (c == seq_chunks - 1))(lambda: _comm_finish(*comm_refs))
        for s in range(sb):
            per_seq = lambda rs: [r.at[s] for r in rs]
            one_chunk(c == 0, *per_seq(refs[:6]), *refs[6:18], *per_seq(refs[18 + n_comm:21 + n_comm]),
                      *per_seq(refs[21 + 2 * n_comm:23 + 2 * n_comm]))

    def one_chunk(first, pu_ref, pv_ref, z_ref, xbc_ref, dt_ref, halo_ref, gv_ref, ws_ref, bsb_ref, gout_ref, cw_ref,
                  cb_ref, dtb_ref, alog_ref, dx_ref, ng_ref, e_ref, ltri_ref, cat_ref, y_ref, st_ref, s_ref, cst_ref):
        tril = _iota2((CH, CH), 0) >= _iota2((CH, CH), 1)
        lane = _iota2((CH, 128), 1)

        y_a, _, _ = _gmlp_fwd_vals(pu_ref[...], pv_ref[...], gv_ref[...], ws_ref, bsb_ref, False)
        cat_ref[:, 0:GM_W] = _rms(y_a, gout_ref[...]).astype(cat_ref.dtype)

        @pl.when(first)
        def _():
            s_ref[...] = jnp.zeros_like(s_ref)

        halo8 = jnp.where(first, 0.0, halo_ref[...])
        q = _ssd_common(xbc_ref[...], halo8, dt_ref[...], cw_ref, cb_ref[...], dtb_ref[...], alog_ref[...], e_ref,
                        ltri_ref[...])
        act = q["act"]
        xv = act[:, 0:SSD_W]
        xdt = xv * q["dt_x"]
        xdt_m = _mx(xdt)
        cs = q["cs"]
        cst_ref[...] = cs.T
        s_prev = s_ref[...]
        st_ref[...] = s_prev
        ys = []
        for g in range(SSD_G):
            bg = _mx(act[:, SSD_W + g * SSD_N:SSD_W + (g + 1) * SSD_N])
            cg = _mx(act[:, SSD_W + SSD_G * SSD_N + g * SSD_N:SSD_W + SSD_G * SSD_N + (g + 1) * SSD_N])
            cbm = _dot_nt(cg, bg)
            gs = slice(g * 512, (g + 1) * 512)
            for pr in range(4):
                ps = slice(g * 512 + pr * 128, g * 512 + (pr + 1) * 128)
                o = []
                for hh in range(2):
                    h = g * 8 + pr * 2 + hh
                    m_h = _mx(cbm * _head_lm(cs, cst_ref, h, tril))
                    o.append(_dot(m_h, xdt_m[:, ps]))
                ys.append(jnp.where(lane < SSD_P, o[0], o[1]))
            sg = s_prev[:, gs]
            yoff = _dot(cg, _mx(sg)) * q["ecs_x"][:, gs]
            ys[-4:] = [ys[-4 + i] + yoff[:, i * 128:(i + 1) * 128] for i in range(4)]
            st_new = _dot_tn(bg, _mx(q["dec_x"][:, gs] * xdt[:, gs]))
            s_ref[:, gs] = sg * q["cdec_x"][:, gs] + st_new
        y = jnp.concatenate(ys, axis=1) + dx_ref[...] * xv
        y_ref[...] = y
        zv = z_ref[...]
        yg = y * (zv * _sigmoid(zv))
        for g in range(SSD_G):
            gs = slice(g * 512, (g + 1) * 512)
            cat_ref[:, GM_W + g * 512:GM_W + (g + 1) * 512] = _rms(yg[:, gs], ng_ref[:, gs]).astype(cat_ref.dtype)

    blk = lambda w, j: pl.BlockSpec((sb, CH, w), lambda g, c: (g, c, j))
    full = lambda arr: pl.BlockSpec(arr.shape, lambda g, c: (0,) * arr.ndim)
    consts = [gv, ws, bsb, gout, cw8, cb, dtb, alog, d_x, ng, e_mat, ltri_mat]
    seq = seq_chunks * CH
    res = pl.pallas_call(
        body, name="mixer_fwd", grid=(n_groups, seq_chunks),
        in_specs=[blk(GM_W, 0), blk(GM_W, 1), blk(SSD_W, 2), blk(CONV_CH, 2), blk(128, DT_BLK),
                  pl.BlockSpec((sb, 8, CONV_CH), lambda g, c: (g, jnp.maximum(c * (CH // 8) - 1, 0), 2))]
        + [full(a) for a in consts] + c_in_specs,
        out_specs=[blk(2 * D, 0), blk(SSD_W, 0), blk(SSD_W, 0)] + c_out_specs,
        out_shape=[jax.ShapeDtypeStruct((n_seq, seq, 2 * D), _MXU), jax.ShapeDtypeStruct((n_seq, seq, SSD_W), _F32),
                   jax.ShapeDtypeStruct((n_seq, seq, SSD_W), _F32)] + c_out_shape,
        scratch_shapes=[pltpu.VMEM((sb, SSD_N, SSD_W), _F32), pltpu.VMEM((sb, 128, CH), _F32)] + c_scratch,
        compiler_params=pltpu.CompilerParams(dimension_semantics=("arbitrary", "arbitrary"),
                                             vmem_limit_bytes=48 << 20),
    )(proj, proj, proj, proj, proj, proj, *consts, *c_in)
    return [r.reshape(t, r.shape[-1]) for r in res[:3]], res[3:]


def _mixer_bwd(proj, dcat, yss, states, gv, ws, bsb, gout, cw8, cb, dtb, alog, d_x, ng, e_mat, et_mat, ltri_mat,
               seq_chunks):
    t = proj.shape[0]
    n_seq = t // (seq_chunks * CH)
    sb = math.gcd(_SEQ_PER_STEP, n_seq)
    n_groups = n_seq // sb
    seq = seq_chunks * CH
    proj, dcat, yss, states = [a.reshape(n_seq, seq, a.shape[1]) for a in (proj, dcat, yss, states)]

    def body(*refs):
        acc_refs, shared = refs[23:33], refs[37:39]
        grp, i = pl.program_id(0), pl.program_id(1)

        @pl.when((grp == 0) & (i == 0))
        def _():
            for r in (*acc_refs, *shared, refs[35]):
                r[...] = jnp.zeros_like(r)

        @pl.when(i == 0)
        def _():
            refs[33][...] = jnp.zeros_like(refs[33])
            refs[34][...] = jnp.zeros_like(refs[34])

        for s in range(sb):
            per_seq = lambda rs: [r.at[s] for r in rs]
            last_of_all = ((grp == n_groups - 1) & (i == seq_chunks - 1)) if s == sb - 1 else None
            one_chunk(seq_chunks - 1 - i, last_of_all, *per_seq(refs[:9]), *refs[9:22], refs[22].at[s], *acc_refs, *per_seq(refs[33:37]), *shared)

    def one_chunk(c, finalize, pu_ref, pv_ref, z_ref, xbc_ref, dt_ref, halo_ref, dcat_ref, y_ref, st_ref,
                  gv_ref, ws_ref, bsb_ref, gout_ref, cw_ref, cb_ref, dtb_ref, alog_ref, dx_ref, ng_ref,
                  e_ref, et_ref, ltri_ref,
                  dproj_ref, dws_ref, dbs_ref, dgv_ref, dgout_ref, dng_ref, dcw_ref, dcb_ref, ddtb_ref, dalog_ref,
                  dd_ref, ds_ref, dnext_ref, dcst_ref, cst_ref, dbacc_ref, ddacc_ref):
        first = c == 0
        tril = _iota2((CH, CH), 0) >= _iota2((CH, CH), 1)
        lane = _iota2((CH, 128), 1)
        row = _iota2((CH, 128), 0)
        dcat_v = dcat_ref[...].astype(_F32)

        pu, pv = pu_ref[...], pv_ref[...]
        gv_v = gv_ref[...]
        y_a, u, keep = _gmlp_fwd_vals(pu, pv, gv_v, ws_ref, bsb_ref, True)
        dy, dgout8 = _rms_bwd(y_a, gout_ref[...], dcat_v[:, 0:GM_W])
        dgout_ref[...] += dgout8
        dus, dvs, dgvs = [], [], []
        for h in range(GM_H):
            sl = slice(h * 128, (h + 1) * 128)
            vh, r, vn, wm, mixed = keep[h + 1]
            dyh = dy[:, sl]
            dus.append(dyh * mixed)
            dmix = dyh * u[:, sl]
            dmix_m = _mx(dmix)
            dws_ref[h] += jnp.where(tril, _dot_nt(dmix_m, _mx(vn)), 0.0)
            dbacc_ref[h] += dmix
            dvn = _dot_tn(wm, dmix_m)
            gy = dvn * gv_v[:, sl]
            nh = vh * r
            dvs.append((gy - nh * jnp.mean(gy * nh, axis=-1, keepdims=True)) * r)
            dgvs.append(_sum8(dvn * nh))
        dgv_ref[...] += jnp.concatenate(dgvs, axis=1)
        cdf_u, cdf_v = keep[0]
        gelu_grad = lambda pre, cdf: cdf + pre * jnp.exp2(pre * pre * (-0.5 * _LOG2E) + _LOG2_INV_SQRT_2PI)
        dproj_ref[:, 0:GM_W] = (jnp.concatenate(dus, axis=1) * gelu_grad(pu, cdf_u)).astype(dproj_ref.dtype)
        dproj_ref[:, GM_W:2 * GM_W] = (jnp.concatenate(dvs, axis=1) * gelu_grad(pv, cdf_v)).astype(dproj_ref.dtype)

        halo8 = jnp.where(first, 0.0, halo_ref[...])
        q = _ssd_common(xbc_ref[...], halo8, dt_ref[...], cw_ref, cb_ref[...], dtb_ref[...], alog_ref[...], e_ref,
                        ltri_ref[...])
        act = q["act"]
        xv = act[:, 0:SSD_W]
        dt_x, ecs_x, dec_x, cdec_x = q["dt_x"], q["ecs_x"], q["dec_x"], q["cdec_x"]
        xdt = xv * dt_x
        xdt_m = _mx(xdt)
        cs = q["cs"]
        cst_ref[...] = cs.T
        s_prev = st_ref[...]
        ds = ds_ref[...]
        yv = y_ref[...]
        zv = z_ref[...]
        sig_z = _sigmoid(zv)
        sz = zv * sig_z
        yg = yv * sz
        dygs, dng8 = [], []
        for g in range(SSD_G):
            gs = slice(g * 512, (g + 1) * 512)
            a_, b_ = _rms_bwd(yg[:, gs], ng_ref[:, gs], dcat_v[:, GM_W + g * 512:GM_W + (g + 1) * 512])
            dygs.append(a_)
            dng8.append(b_)
        dyg = jnp.concatenate(dygs, axis=1)
        dng_ref[...] += jnp.concatenate(dng8, axis=1)
        dyv = dyg * sz
        dproj_ref[:, 2 * GM_W:2 * GM_W + SSD_W] = (dyg * yv * _silu_grad(sig_z, sz)).astype(dproj_ref.dtype)
        ddacc_ref[...] += _sum8(dyv * xv)
        dyv_m = _mx(dyv)

        dxdt_parts, db_parts, dc_parts = [], [], []
        dcs = jnp.zeros((CH, 128), _F32)
        dcs_x_parts, ddec_x_parts, dcl_x_parts = [], [], []
        for g in range(SSD_G):
            gs = slice(g * 512, (g + 1) * 512)
            bg = _mx(act[:, SSD_W + g * SSD_N:SSD_W + (g + 1) * SSD_N])
            cg = _mx(act[:, SSD_W + SSD_G * SSD_N + g * SSD_N:SSD_W + SSD_G * SSD_N + (g + 1) * SSD_N])
            cbm = _dot_nt(cg, bg)
            sg = s_prev[:, gs]
            sg_m = _mx(sg)
            dsg = ds[:, gs]
            dsg_m = _mx(dsg)
            zoff = _dot(cg, sg_m)
            dz_off = dyv[:, gs] * ecs_x[:, gs]
            dz_off_m = _mx(dz_off)
            dcs_x_parts.append(dyv[:, gs] * zoff * ecs_x[:, gs])
            dcg = _dot_nt(dz_off_m, sg_m)
            dsprev = _dot_tn(cg, dz_off_m)
            w_st = dec_x[:, gs] * xdt[:, gs]
            dw_st = _dot(bg, dsg_m)
            dbg = _dot_nt(_mx(w_st), dsg_m)
            dxdt_g = dec_x[:, gs] * dw_st
            ddec_x_parts.append(dw_st * xdt[:, gs])
            dsprev = dsprev + cdec_x[:, gs] * dsg
            dcl_x_parts.append(jnp.sum(dsg * sg, axis=0, keepdims=True) * cdec_x[:, gs])
            ds_ref[:, gs] = dsprev
            dcb = jnp.zeros((CH, CH), _F32)
            dxdt_pairs = []
            for pr in range(4):
                ps = slice(g * 512 + pr * 128, g * 512 + (pr + 1) * 128)
                acc_pair = None
                for hh in range(2):
                    h = g * 8 + pr * 2 + hh
                    in_head = (lane < SSD_P) if hh == 0 else (lane >= SSD_P)
                    lm = _head_lm(cs, cst_ref, h, tril)
                    m_h = cbm * lm
                    m_hm = _mx(m_h)
                    dyh_m = _mx(jnp.where(in_head, dyv[:, ps], 0.0))
                    dm = _dot_nt(dyh_m, xdt_m[:, ps])
                    dcb = dcb + dm * lm
                    qm = dm * m_h
                    dcs = dcs + jnp.where(lane == h, jnp.sum(qm, axis=1, keepdims=True), 0.0)
                    dcst_ref[h:h + 1, :] = jnp.sum(qm, axis=0, keepdims=True)
                    contrib = jnp.where(in_head, _dot_tn(m_hm, dyv_m[:, ps]), 0.0)
                    acc_pair = contrib if acc_pair is None else acc_pair + contrib
                dxdt_pairs.append(acc_pair)
            dxdt_parts.append(dxdt_g + jnp.concatenate(dxdt_pairs, axis=1))
            dcb_m = _mx(dcb)
            dc_parts.append(dcg + _dot(dcb_m, bg))
            db_parts.append(dbg + _dot_tn(dcb_m, cg))
        dxdt = jnp.concatenate(dxdt_parts, axis=1)
        dxv = dx_ref[...] * dyv + dxdt * dt_x
        et = et_ref[...]
        head_sum = lambda v: _dot(v.astype(_BF16), et)
        ddt = head_sum(dxdt * xv)
        dcs = dcs - dcst_ref[...].T + head_sum(jnp.concatenate(dcs_x_parts, axis=1))
        ddec = head_sum(jnp.concatenate(ddec_x_parts, axis=1)) * q["dec"]
        dcs = dcs - ddec
        dcl = jnp.sum(ddec, axis=0, keepdims=True) + _xdot(
            _dot, jnp.broadcast_to(jnp.concatenate(dcl_x_parts, axis=1), (8, SSD_W)), et)[0:1, :]
        dcs = jnp.where(row == CH - 1, dcs + dcl, dcs)
        da = _xdot_left(_dot_tn, ltri_ref[...], dcs)
        ddt = ddt + da * q["a_neg"]
        dalog_ref[...] += _sum8(da * q["dt"] * q["a_neg"])
        ddtraw = jnp.where(lane < SSD_H, ddt * _sigmoid(q["dtin"]), 0.0)
        ddtb_ref[...] += _sum8(ddtraw)
        dproj_ref[:, D_IN_PAD - 128:D_IN_PAD] = ddtraw.astype(dproj_ref.dtype)
        dcpre = jnp.concatenate([dxv] + db_parts + dc_parts, axis=1) * _silu_grad(q["sig"], act)
        dcb_ref[...] += _sum8(dcpre)
        for k in range(CONV_K):
            dcw_ref[k:k + 1, :] += jnp.sum(dcpre * q["xs"][CONV_K - 1 - k], axis=0, keepdims=True)
        next8 = dnext_ref[...]
        dxbc = sum(cw_ref[k:k + 1, :] * _shift_up(dcpre, next8, CONV_K - 1 - k) for k in range(CONV_K))
        dproj_ref[:, 2 * GM_W + SSD_W:2 * GM_W + SSD_W + CONV_CH] = dxbc.astype(dproj_ref.dtype)
        dnext_ref[...] = dcpre[0:8, :]

        if finalize is not None:
            @pl.when(finalize)
            def _():
                for h in range(GM_H):
                    dbs_ref[h:h + 1, :] = _xdot_left(_dot_nt, jnp.ones((8, 128), _BF16), dbacc_ref[h])[0:1, :]
                dd_ref[...] = _xdot(_dot, ddacc_ref[...], et)

    rblk = lambda w, j: pl.BlockSpec((sb, CH, w), lambda g, i: (g, seq_chunks - 1 - i, j))
    full = lambda arr: pl.BlockSpec(arr.shape, lambda g, i: (0,) * arr.ndim)
    acc = lambda shape: pl.BlockSpec(shape, lambda g, i: (0,) * len(shape))
    consts = [gv, ws, bsb, gout, cw8, cb, dtb, alog, d_x, ng, e_mat, et_mat, ltri_mat]
    acc_shapes = [(GM_H, CH, CH), (8, 128), (8, GM_W), (8, GM_W), (8, SSD_W), (8, CONV_CH), (8, CONV_CH), (8, 128),
                  (8, 128), (8, 128)]
    res = pl.pallas_call(
        body, name="mixer_bwd", grid=(n_groups, seq_chunks),
        in_specs=[rblk(GM_W, 0), rblk(GM_W, 1), rblk(SSD_W, 2), rblk(CONV_CH, 2), rblk(128, DT_BLK),
                  pl.BlockSpec((sb, 8, CONV_CH),
                               lambda g, i: (g, jnp.maximum((seq_chunks - 1 - i) * (CH // 8) - 1, 0), 2)),
                  rblk(2 * D, 0), rblk(SSD_W, 0), rblk(SSD_W, 0)] + [full(a) for a in consts],
        out_specs=[rblk(D_IN_PAD, 0)] + [acc(s) for s in acc_shapes],
        out_shape=[jax.ShapeDtypeStruct((n_seq, seq, D_IN_PAD), _MXU)]
        + [jax.ShapeDtypeStruct(s, _F32) for s in acc_shapes],
        scratch_shapes=[pltpu.VMEM((sb, SSD_N, SSD_W), _F32), pltpu.VMEM((sb, 8, CONV_CH), _F32),
                        pltpu.VMEM((sb, 128, CH), _F32), pltpu.VMEM((sb, 128, CH), _F32),
                        pltpu.VMEM((GM_H, CH, 128), _F32), pltpu.VMEM((8, SSD_W), _F32)],
        compiler_params=pltpu.CompilerParams(dimension_semantics=("arbitrary", "arbitrary"),
                                             vmem_limit_bytes=48 << 20),
    )(proj, proj, proj, proj, proj, proj, dcat, yss, states, *consts)
    return [res[0].reshape(t, D_IN_PAD)] + list(res[1:])


def _peers():
    x, y, c = lax.axis_index("x"), lax.axis_index("y"), lax.axis_index("c")
    out = []
    for k in range(1, N_DEV):
        fx, fy, fc = (k >> 2) & 1, (k >> 1) & 1, k & 1
        px, py, pc = (x + fx) % 2, (y + fy) % 2, (c + fc) % 2
        out.append((k - 1, (px, py, pc), 4 * px + 2 * py + pc))
    return out, 4 * x + 2 * y + c


def _comm_io(comm):
    any_spec = pl.BlockSpec(memory_space=pl.ANY)
    n = len(comm)
    out_shape = []
    for (kind, axis), src in comm:
        shp = list(src.shape)
        if kind in ("gather", "gather2"):
            shp[axis] *= N_DEV
        else:
            shp[axis] //= N_DEV
            shp = [N_DEV] + shp
        out_shape.append(jax.ShapeDtypeStruct(tuple(shp), src.dtype))
    scratch = [pltpu.SemaphoreType.DMA((n * (N_DEV - 1),)), pltpu.SemaphoreType.DMA((n * (N_DEV - 1),)),
               pltpu.SemaphoreType.DMA((n,))] if n else []
    return [src for _, src in comm], [any_spec] * n, [any_spec] * n, out_shape, scratch


def _window(ref, axis, idx, size):
    start = pl.multiple_of(idx * size, size)
    return ref.at[tuple(pl.ds(start, size) if a == axis else slice(None) for a in range(len(ref.shape)))]


def _comm_plans(kinds, src_refs, dst_refs, send_sems, recv_sems, local_sems):
    x, y, c = lax.axis_index("x"), lax.axis_index("y"), lax.axis_index("c")
    peers, me = _peers()
    plans = []
    for s, ((kind, axis), src, dst) in enumerate(zip(kinds, src_refs, dst_refs)):
        sems = lambda k: dict(send_sem=send_sems.at[s * (N_DEV - 1) + k], recv_sem=recv_sems.at[s * (N_DEV - 1) + k])
        remote = lambda src_ref, dst_ref, k, pid: pltpu.make_async_remote_copy(
            src_ref=src_ref, dst_ref=dst_ref, device_id=pid, device_id_type=_MESH, **sems(k))
        if kind == "gather2":
            size = src.shape[axis]
            win = lambda idx: _window(dst, axis, idx, size)
            sib, sib_idx = (x, y, 1 - c), 4 * x + 2 * y + (1 - c)
            local = pltpu.make_async_copy(src, win(me), local_sems.at[s])
            to_sib = remote(src, win(me), 0, sib)
            starts, forwards = [local, to_sib], []
            waits = [(local, "local"), (to_sib, "send"), (remote(src, win(sib_idx), 0, sib), "recv")]
            for j, (fx, fy) in enumerate(((1, 0), (0, 1), (1, 1))):
                px, py = (x + fx) % 2, (y + fy) % 2
                same, other = 4 * px + 2 * py + c, 4 * px + 2 * py + (1 - c)
                out = remote(src, win(me), 1 + j, (px, py, c))
                starts.append(out)
                passed = remote(win(same), win(same), 4 + j, sib)
                forwards.append((remote(src, win(same), 1 + j, (px, py, c)), passed))
                waits += [(out, "send"), (passed, "send"), (remote(win(other), win(other), 4 + j, sib), "recv")]
            plans.append((starts, forwards, waits))
            continue
        if kind == "gather":
            size = src.shape[axis]
            src_for = lambda pidx: src
            dst_mine = _window(dst, axis, me, size)
        else:
            size = src.shape[axis] // N_DEV
            src_for = lambda pidx: _window(src, axis, pidx, size)
            dst_mine = dst.at[me]
        local = pltpu.make_async_copy(src_for(me), dst_mine, local_sems.at[s])
        remotes = [remote(src_for(pidx), dst_mine, k, pid) for k, pid, pidx in peers]
        plans.append(([local] + remotes, [], [(local, "local")] + [(cp, "both") for cp in remotes]))
    return plans


def _comm_start(*refs):
    for starts, _, _ in _comm_plans(*refs):
        for cp in starts:
            cp.start()


def _comm_finish(*refs):
    for _, forwards, waits in _comm_plans(*refs):
        for arrival, cp in forwards:
            arrival.wait_recv()
            cp.start()
        for cp, what in waits:
            if what == "send":
                cp.wait_send()
            elif what == "recv":
                cp.wait_recv()
            else:
                cp.wait()


def _adam_vals(w, g, m, v):
    m = B1 * m + (1.0 - B1) * g
    v = B2 * v + (1.0 - B2) * (g * g)
    m_hat = m / (1.0 - B1 ** STEP)
    v_hat = v / (1.0 - B2 ** STEP)
    delta = -LR * (m_hat / (jnp.sqrt(v_hat) + ADAM_EPS) + WD * w)
    return delta, m, v


def _sum_adam(name, recv, w, m, v, tile=256):
    _, r, wd = recv.shape
    if r % min(tile, r) == 0:
        tr, tc = min(tile, r), wd
    else:
        tr, tc = r, tile
        assert wd % tc == 0, (name, r, wd)

    def body(recv_ref, w_ref, m_ref, v_ref, g_out, d_out, m_out, v_out):
        g = recv_ref[0].astype(_F32)
        for s in range(1, N_DEV):
            g = g + recv_ref[s].astype(_F32)
        d_, m_, v_ = _adam_vals(w_ref[...], g, m_ref[...], v_ref[...])
        g_out[...] = g
        d_out[...] = d_
        m_out[...] = m_
        v_out[...] = v_

    spec = pl.BlockSpec((tr, tc), lambda i, j: (i, j))
    return pl.pallas_call(
        body, name=name, grid=(r // tr, wd // tc),
        in_specs=[pl.BlockSpec((N_DEV, tr, tc), lambda i, j: (0, i, j)), spec, spec, spec],
        out_specs=[spec] * 4, out_shape=[jax.ShapeDtypeStruct((r, wd), _F32)] * 4,
        compiler_params=pltpu.CompilerParams(dimension_semantics=("parallel", "parallel"),
                                             vmem_limit_bytes=48 << 20),
    )(recv, w, m, v)


def _small_local(parts, segments, n_rows):
    def body(parts_ref, loc_ref):
        loc_ref[...] = jnp.zeros_like(loc_ref)
        for out_row, n_out, in_row, n_in, kind in segments:
            if kind == "copy":
                loc_ref[out_row:out_row + n_out, :] = parts_ref[in_row:in_row + n_in, :]
            else:
                s = jnp.sum(parts_ref[in_row:in_row + n_in, :], axis=0, keepdims=True)
                if kind == "loss":
                    s = jnp.broadcast_to(jnp.sum(s, axis=1, keepdims=True) * (0.5 / D), (1, D))
                loc_ref[out_row:out_row + 1, :] = s

    vm = pl.BlockSpec(memory_space=pltpu.VMEM)
    return pl.pallas_call(body, name="small_local", in_specs=[vm], out_specs=vm,
                          out_shape=jax.ShapeDtypeStruct((n_rows, D), _F32))(parts)


def _small_final(blocks, late8, late_row, w, m, v):
    n_rows = w.shape[0]

    def body(blocks_ref, late_ref, w_ref, m_ref, v_ref, g_out, d_out, m_out, v_out, loc_ref, recv_ref, send_sems,
             recv_sems):
        peers, me = _peers()
        loc_ref[...] = jnp.broadcast_to(jnp.sum(late_ref[...], axis=0, keepdims=True), (8, D))
        recv_ref[me] = loc_ref[...]
        copies = [pltpu.make_async_remote_copy(src_ref=loc_ref, dst_ref=recv_ref.at[me], send_sem=send_sems.at[k],
                                               recv_sem=recv_sems.at[k], device_id=pid, device_id_type=_MESH)
                  for k, pid, _ in peers]
        for cp in copies:
            cp.start()
        g = blocks_ref[0:n_rows, :]
        for s in range(1, N_DEV):
            g = g + blocks_ref[s * n_rows:(s + 1) * n_rows, :]
        for cp in copies:
            cp.wait()
        late = recv_ref[0]
        for s in range(1, N_DEV):
            late = late + recv_ref[s]
        g = jnp.where(_iota2((n_rows, D), 0) == late_row, jnp.broadcast_to(late[0:1, :], (n_rows, D)), g)
        d_, m_, v_ = _adam_vals(w_ref[...], g, m_ref[...], v_ref[...])
        g_out[...] = g
        d_out[...] = d_
        m_out[...] = m_
        v_out[...] = v_

    vm = pl.BlockSpec(memory_space=pltpu.VMEM)
    return pl.pallas_call(
        body, name="small_final", in_specs=[vm] * 5, out_specs=[vm] * 4,
        out_shape=[jax.ShapeDtypeStruct((n_rows, D), _F32)] * 4,
        scratch_shapes=[pltpu.VMEM((8, D), _F32), pltpu.VMEM((N_DEV, 8, D), _F32),
                        pltpu.SemaphoreType.DMA((N_DEV - 1,)), pltpu.SemaphoreType.DMA((N_DEV - 1,))],
        compiler_params=pltpu.CompilerParams(vmem_limit_bytes=48 << 20),
    )(blocks, late8, w, m, v)


_BIG_NAMES =("w_in", "w_out", "w_ff1", "w_ff2", "w_ple_gate", "w_ple_proj")

_G_VECS = ("norm_mix_g", "gm_v_norm_g", "gm_out_norm_g", "ssd_norm_g", "norm_mlp_g", "ple_norm_g", "final_norm_g")
_LATE = _G_VECS[0]


def _const_mats():
    h = np.arange(128)[:, None]
    ch = np.arange(SSD_W)[None, :]
    e = (ch // SSD_P == h).astype(np.float32)
    ltri = (np.arange(CH)[:, None] >= np.arange(CH)[None, :]).astype(np.float32)
    return jnp.asarray(e, _BF16), jnp.asarray(e.T, _BF16), jnp.asarray(ltri, _BF16)


def _pad_lanes(v, n=128):
    return jnp.pad(v, ((0, 0), (0, n - v.shape[1])))


def _local_step(x, p, tgt, shard, conv_w_shard, small, seq_len):
    seq_chunks = seq_len // CH
    e_mat, et_mat, ltri_mat = _const_mats()
    g_mix, g_mlp, g_ple = small["norm_mix_g"], small["norm_mlp_g"], small["ple_norm_g"]
    g_fin = small["final_norm_g"].reshape(1, D)
    gv, gout, ng = small["gm_v_norm_g"], small["gm_out_norm_g"], small["ssd_norm_g"]
    ws = small["gm_ws"][0]
    bsb = jnp.broadcast_to(small["gm_bs"][0][:, :, None], (GM_H, CH, 128))
    cb = small["ssd_conv_b"]
    dtb, alog = _pad_lanes(small["ssd_dt_bias"]), _pad_lanes(small["ssd_a_log"])
    d_x = jnp.repeat(small["ssd_d"], SSD_P, axis=1)

    first = lambda acc: (acc,)
    rows, cols = ("gather2", 0), ("gather2", 1)
    n1, (g_win, g_cw) = _norm_cast("norm_mix", x, g_mix,
                                   comm=[(rows, shard["w_in"][None]), (("gather", 0), conv_w_shard[None])])
    w_in_t = jnp.pad(g_win.reshape(D_IN, D), ((0, D_IN_PAD - D_IN), (0, 0)))
    cw8 = jnp.pad(g_cw.transpose(1, 0, 2).reshape(CONV_K, CONV_CH), ((0, 8 - CONV_K), (0, 0)))
    mix_consts = (gv, ws, bsb, gout, cw8, cb, dtb, alog, d_x, ng)
    (proj,), (w_out, w1) = _matmul("proj_in", n1, w_in_t, "nt", 256, D_IN_PAD, D, first, [("tile", _F32)],
                                   comm=[(rows, shard["w_out"]), (cols, shard["w_ff1"])])
    (cat, yss, states), (w2, wg, wp) = _mixer_fwd(
        proj, *mix_consts, e_mat, ltri_mat, seq_chunks,
        comm=[(rows, shard["w_ff2"]), (rows, shard["w_ple_gate"]), (cols, shard["w_ple_proj"])])

    def epi_res_norm(acc, res, g):
        hv = acc + res
        return hv, _rms(hv, g)

    (h1, n2), _ = _matmul("proj_out", cat, w_out, "nn", 1024, D, 2 * D, epi_res_norm,
                          [("tile", _F32), ("tile", _MXU)], extras=[(x, "tile"), (g_mlp, "row")])

    def epi_relu2(acc):
        hid = jnp.maximum(acc, 0.0)
        return hid, hid * hid

    (hid, hid2), _ = _matmul("ff1", n2, w1, "nn", 1024, 1024, D, epi_relu2, [("tile", _MXU), ("tile", _MXU)])
    (h2, n3), _ = _matmul("ff2", hid2, w2, "nn", 512, D, D_FF, epi_res_norm, [("tile", _F32), ("tile", _MXU)],
                          extras=[(h1, "tile"), (g_ple, "row")])

    def epi_norm_bwd(acc, up, hv, g):
        dx, dg8 = _rms_bwd(hv, g, acc)
        dh = up + dx
        return dh, dh, dg8

    def epi_head(acc, wg_v, p_rows, wp_v, h2v, tg, gf, gp_):
        ppv = _dot(_mx(p_rows), wp_v)
        gate = _sigmoid(acc)
        gp = gate * ppv
        h3 = h2v + gp
        r = lax.rsqrt(jnp.mean(h3 * h3, axis=-1, keepdims=True) + EPS)
        nh = h3 * r
        err = nh * gf - tg
        gy = err * (gf * (1.0 / D))
        dh3 = (gy - nh * jnp.mean(gy * nh, axis=-1, keepdims=True)) * r
        dpp = dh3 * gate
        da3 = dpp * (ppv - gp)
        dh2, dh2_again, dgple8 = epi_norm_bwd(_dot_nt(_mx(da3), wg_v), dh3, h2v, gp_)
        return da3, dpp, dh2, dh2_again, _sum8(err * err), _sum8(err * nh) * (1.0 / D), dgple8

    (da3, dpp, dh2, dh2b, lossp, dgfin, dgple), _ = _matmul(
        "ple_gate_loss_bwd", n3, wg, "nn", 512, D, D, epi_head,
        [("tile", _MXU), ("tile", _MXU), ("tile", _F32), ("tile", _MXU), ("part8", _F32), ("part8", _F32),
         ("part8", _F32)],
        extras=[(p, "rows"), (wp, "full"), (h2, "tile"), (tgt, "tile"), (g_fin, "row"), (g_ple, "row")],
        b_to_epilogue=True)

    s_rows, s_cols = ("scatter", 0), ("scatter", 1)
    (dwp,), _ = _matmul("d_w_ple_proj", p, dpp, "tn", D_PLE, D, 2048, first, [("tile", _BF16)])
    (dwg,), _ = _matmul("d_w_ple_gate", n3, da3, "tn", D, D, 2048, first, [("tile", _BF16)])
    (dw2,), (r_wp, r_wg) = _matmul("d_w_ff2", hid2, dh2b, "tn", 1024, D, 2048, first, [("tile", _BF16)],
                                   comm=[(s_cols, dwp), (s_rows, dwg)])
    (da1,), (r_w2,) = _matmul("d_ff_hidden", dh2b, w2, "nt", 512, 2048, D,
                              lambda acc, hv: (acc * 2.0 * hv.astype(_F32),), [("tile", _MXU)],
                              extras=[(hid, "tile")], comm=[(s_rows, dw2)])
    (dw1,), _ = _matmul("d_w_ff1", n2, da1, "tn", 1024, 1024, 2048, first, [("tile", _BF16)])
    (dh1, dh1b, dgmlp), (r_w1,) = _matmul(
        "d_h1", da1, w1, "nt", 256, D, D_FF, epi_norm_bwd, [("tile", _F32), ("tile", _MXU), ("part8", _F32)],
        extras=[(dh2, "tile"), (h1, "tile"), (g_mlp, "row")], comm=[(s_cols, dw1)])
    (dwout,), _ = _matmul("d_w_out", cat, dh1b, "tn", 1024, D, 2048, first, [("tile", _BF16)])
    (dcat,), (r_wout,) = _matmul("d_cat", dh1b, w_out, "nt", 1024, 1024, D, first, [("tile", _MXU)],
                                 comm=[(s_rows, dwout)])
    (dproj, dws, dbs, dgv, dgout, dng, dcw, dcb, ddtb, dalog, dd) = _mixer_bwd(
        proj, dcat, yss, states, *mix_consts, e_mat, et_mat, ltri_mat, seq_chunks)
    pieces = dict(gm_v_norm_g=dgv, gm_out_norm_g=dgout, ssd_norm_g=dng, norm_mlp_g=dgmlp, ple_norm_g=dgple,
                  final_norm_g=dgfin, gm_ws=dws, gm_bs=dbs, ssd_conv_w=dcw, ssd_conv_b=dcb, ssd_dt_bias=ddtb,
                  ssd_a_log=dalog, ssd_d=dd, loss=lossp)
    parts, segments, n_rows, where = _small_layout(pieces)
    small_block = _small_local(parts, segments, n_rows)
    (dwin_t,), (small_blocks,) = _matmul("d_w_in", n1, dproj, "tn", 512, D_IN_PAD, 1024, lambda acc: (acc.T,),
                                         [("tile_t", _BF16)], comm=[(("gather", 0), small_block)])
    dwin_blocks = dwin_t[:D_IN].reshape(N_DEV, SHARD_IN, D)
    (gx, dgmix), (r_win,) = _matmul(
        "d_x", dproj, w_in_t, "nn", 256, D, D_IN_PAD, lambda *a: epi_norm_bwd(*a)[1:],
        [("tile", _F32), ("part8", _F32)], extras=[(dh1, "tile"), (x, "tile"), (g_mix, "row")],
        comm=[(s_rows, dwin_blocks)])
    r_win = r_win.reshape(N_DEV, SHARD_IN, D)

    big = dict(w_in=r_win, w_out=r_wout, w_ff1=r_w1, w_ff2=r_w2, w_ple_gate=r_wg, w_ple_proj=r_wp)
    return gx, big, small_blocks, dgmix, n_rows, where


def _small_layout(pieces):
    rows, segments = [], []
    in_row, out_row = 0, 0

    def add(arr, kind, n_out):
        nonlocal in_row, out_row
        rows.append(arr)
        segments.append((out_row, n_out, in_row, arr.shape[0], kind))
        start = out_row
        in_row += arr.shape[0]
        out_row += n_out
        return start

    where = {_LATE: 0}
    out_row = 1
    for name in _G_VECS[1:]:
        where[name] = add(pieces[name], "sum", 1)
    where["gm_ws"] = add(pieces["gm_ws"].reshape(GM_H * CH * CH // D, D), "copy", GM_H * CH * CH // D)
    where["gm_bs"] = add(pieces["gm_bs"].reshape(1, D), "copy", 1)
    cb = jnp.pad(pieces["ssd_conv_b"], ((0, 0), (0, 2 * D - CONV_CH)))
    where["ssd_conv_b"] = add(cb[:, :D], "sum", 1)
    add(cb[:, D:], "sum", 1)
    cw = jnp.pad(pieces["ssd_conv_w"][:CONV_K], ((0, 0), (0, 2 * D - CONV_CH)))
    where["ssd_conv_w"] = add(cw.reshape(2 * CONV_K, D), "copy", 2 * CONV_K)
    misc = jnp.concatenate([pieces["ssd_dt_bias"], pieces["ssd_a_log"], pieces["ssd_d"],
                            jnp.zeros((8, D - 3 * 128), _F32)], axis=1)
    where["misc"] = add(misc, "sum", 1)
    where["loss"] = add(pieces["loss"], "loss", 1)
    n_rows = -(-out_row // 8) * 8
    return jnp.concatenate(rows, axis=0), tuple(segments), n_rows, where


def _pack_small_params(vals, where, n_rows, my_block):
    rows, at = [], {}

    def add(name, arr):
        at[name] = sum(r.shape[0] for r in rows)
        rows.append(arr)

    for name in _G_VECS:
        add(name, vals[name].reshape(1, D))
    add("gm_ws", vals["gm_ws"].reshape(GM_H * CH * CH // D, D))
    add("gm_bs", vals["gm_bs"].reshape(1, D))
    cb = jnp.pad(vals["ssd_conv_b"].reshape(1, CONV_CH), ((0, 0), (0, 2 * D - CONV_CH)))
    add("ssd_conv_b", cb.reshape(2, D))
    cw = lax.dynamic_update_slice(jnp.zeros((CONV_K, 2 * D), _F32), vals["ssd_conv_w"].reshape(CONV_K, -1),
                                  (0, my_block * (CONV_CH // N_DEV)))
    add("ssd_conv_w", cw.reshape(2 * CONV_K, D))
    misc = jnp.concatenate([_pad_lanes(vals["ssd_dt_bias"].reshape(1, SSD_H)),
                            _pad_lanes(vals["ssd_a_log"].reshape(1, SSD_H)),
                            _pad_lanes(vals["ssd_d"].reshape(1, SSD_H)), jnp.zeros((1, D - 3 * 128), _F32)], axis=1)
    add("misc", misc)
    assert all(where[k] == r for k, r in at.items()), (where, at)
    rows.append(jnp.zeros((n_rows - sum(r.shape[0] for r in rows), D), _F32))
    return jnp.concatenate(rows, axis=0)


def _unpack_small(buf, where, my_block, shapes):
    out = {}
    for name in _G_VECS:
        out[name] = buf[where[name]].reshape(shapes[name])
    n_ws = GM_H * CH * CH // D
    out["gm_ws"] = buf[where["gm_ws"]:where["gm_ws"] + n_ws].reshape(shapes["gm_ws"])
    out["gm_bs"] = buf[where["gm_bs"]].reshape(shapes["gm_bs"])
    r = where["ssd_conv_b"]
    out["ssd_conv_b"] = buf[r:r + 2].reshape(1, 2 * D)[:, :CONV_CH].reshape(shapes["ssd_conv_b"])
    r = where["ssd_conv_w"]
    cw = buf[r:r + 2 * CONV_K].reshape(CONV_K, 2 * D)
    out["ssd_conv_w"] = lax.dynamic_slice(cw, (0, my_block * (CONV_CH // N_DEV)),
                                          (CONV_K, CONV_CH // N_DEV)).reshape(shapes["ssd_conv_w"])
    misc = buf[where["misc"]]
    for i, name in enumerate(("ssd_dt_bias", "ssd_a_log", "ssd_d")):
        out[name] = misc[i * 128:i * 128 + SSD_H].reshape(shapes[name])
    return out


_WEIGHTS = ("norm_mix_g", "w_in", "gm_v_norm_g", "gm_ws", "gm_bs", "gm_out_norm_g", "ssd_conv_w", "ssd_conv_b",
            "ssd_dt_bias", "ssd_a_log", "ssd_d", "ssd_norm_g", "w_out", "norm_mlp_g", "w_ff1", "w_ff2", "ple_norm_g",
            "w_ple_gate", "w_ple_proj", "final_norm_g")


def kernel(x, p, norm_mix_g, w_in, gm_v_norm_g, gm_ws, gm_bs, gm_out_norm_g, ssd_conv_w, ssd_conv_b, ssd_dt_bias, ssd_a_log, ssd_d, ssd_norm_g, w_out, norm_mlp_g, w_ff1, w_ff2, ple_norm_g, w_ple_gate, w_ple_proj, final_norm_g, loss_target, m_norm_mix_g, m_w_in, m_gm_v_norm_g, m_gm_ws, m_gm_bs, m_gm_out_norm_g, m_ssd_conv_w, m_ssd_conv_b, m_ssd_dt_bias, m_ssd_a_log, m_ssd_d, m_ssd_norm_g, m_w_out, m_norm_mlp_g, m_w_ff1, m_w_ff2, m_ple_norm_g, m_w_ple_gate, m_w_ple_proj, m_final_norm_g, v_norm_mix_g, v_w_in, v_gm_v_norm_g, v_gm_ws, v_gm_bs, v_gm_out_norm_g, v_ssd_conv_w, v_ssd_conv_b, v_ssd_dt_bias, v_ssd_a_log, v_ssd_d, v_ssd_norm_g, v_w_out, v_norm_mlp_g, v_w_ff1, v_w_ff2, v_ple_norm_g, v_w_ple_gate, v_w_ple_proj, v_final_norm_g):
    args = dict(locals())
    w = {n: args[n] for n in _WEIGHTS}
    m = {n: args["m_" + n] for n in _WEIGHTS}
    v = {n: args["v_" + n] for n in _WEIGHTS}
    shapes = {n: w[n].shape for n in _WEIGHTS}
    my_block = 4 * lax.axis_index("x") + 2 * lax.axis_index("y") + lax.axis_index("c")
    nb, seq_len, _ = x.shape

    local = lambda d, n: d[n][0].T if n == "w_in" else d[n][0]
    shard = {n: local(w, n).astype(_MXU) for n in _BIG_NAMES}
    small = {n: w[n] for n in _WEIGHTS if n not in _BIG_NAMES}
    gx, recv, small_blocks, late8, n_rows, where = _local_step(
        x.reshape(nb * seq_len, D), p.reshape(nb * seq_len, D_PLE), loss_target.reshape(nb * seq_len, D), shard,
        ssd_conv_w[0], small, seq_len)

    big_out = [{}, {}, {}, {}]
    for n in _BIG_NAMES:
        res = _sum_adam("sum_adam_" + n, recv[n], local(w, n), local(m, n), local(v, n))
        for k in range(4):
            big_out[k][n] = (res[k].T if n == "w_in" else res[k]).reshape(shapes[n])

    packs = [_pack_small_params(d, where, n_rows, my_block) for d in (w, m, v)]
    small_res = _small_final(small_blocks, late8, where[_LATE], *packs)
    loss = small_res[0][where["loss"], 0]
    small_out = [_unpack_small(a, where, my_block, shapes) for a in small_res]

    outs = [loss, gx.reshape(x.shape)]
    for k in range(4):
        outs += [big_out[k][n] if n in _BIG_NAMES else small_out[k][n] for n in _WEIGHTS]
    return tuple(outs)
```

```python
import functools
import math

import jax
import jax.numpy as jnp
import numpy as np
from jax import lax
from jax.experimental import pallas as pl
from jax.experimental.pallas import tpu as pltpu

_F32 = jnp.float32
_BF16 = jnp.bfloat16
_MXU = jnp.bfloat16

D = 1024
D_PLE = 256
GM_W = 1024
GM_H = 8
CH = 128
SSD_W = 1024
SSD_H = 16
SSD_P = 64
SSD_G = 2
SSD_N = 128
CONV_K = 4
CONV_CH = SSD_W + 2 * SSD_G * SSD_N
D_FF = 4096
D_IN = 2 * GM_W + SSD_W + CONV_CH + SSD_H
D_IN_PAD = 4736
EPS = 1e-6
DT_BLK = (D_IN_PAD - 128) // 128
N_DEV = 8
SHARD_IN = D_IN // N_DEV

LR, B1, B2, ADAM_EPS, WD, STEP = 0.001, 0.9, 0.999, 1e-08, 0.01, 10
_LOG2E = math.log2(math.e)
_LOG2_INV_SQRT_2PI = -0.5 * math.log2(2.0 * math.pi)

_SEQ_PER_STEP = 2
_V7X_VMEM_BYTES = 64 * 1024 * 1024
_VMEM_CAP = _V7X_VMEM_BYTES - 8 * 1024 * 1024
_MESH = pl.DeviceIdType.MESH


def _vmem_limit(nbytes):
    return int(min(_VMEM_CAP, max(32 * 1024 * 1024, nbytes * 5 // 4 + (4 << 20))))


def _nbytes(shape, dtype):
    return int(np.prod(shape)) * jnp.dtype(dtype).itemsize


def _mx(v):
    return v.astype(_MXU)


def _dot(a, b):
    return jnp.dot(a, b, preferred_element_type=_F32)


def _dot_nt(a, b):
    return lax.dot_general(a, b, (((1,), (1,)), ((), ())), preferred_element_type=_F32)


def _dot_tn(a, b):
    return lax.dot_general(a, b, (((0,), (0,)), ((), ())), preferred_element_type=_F32)


def _split3(a):
    hi = a.astype(_BF16)
    r = a - hi.astype(_F32)
    mid = r.astype(_BF16)
    lo = (r - mid.astype(_F32)).astype(_BF16)
    return hi, mid, lo


def _xdot(dotfn, a, b01):
    b = b01.astype(_BF16)
    hi, mid, lo = _split3(a)
    return (dotfn(hi, b) + dotfn(mid, b)) + dotfn(lo, b)


def _xdot_left(dotfn, a01, b):
    a = a01.astype(_BF16)
    hi, mid, lo = _split3(b)
    return (dotfn(a, hi) + dotfn(a, mid)) + dotfn(a, lo)


def _sum8(v):
    r, n = v.shape
    return v.reshape(r // 8, 8, n).sum(axis=0)


def _sigmoid(v):
    return 1.0 / (1.0 + jnp.exp(-v))


def _rms(xv, g):
    ms = jnp.mean(xv * xv, axis=-1, keepdims=True)
    return xv * lax.rsqrt(ms + EPS) * g


def _rms_bwd(xv, g, dn):
    r = lax.rsqrt(jnp.mean(xv * xv, axis=-1, keepdims=True) + EPS)
    nh = xv * r
    gy = dn * g
    dx = (gy - nh * jnp.mean(gy * nh, axis=-1, keepdims=True)) * r
    return dx, _sum8(dn * nh)


def _iota2(shape, axis):
    return lax.broadcasted_iota(jnp.int32, shape, axis)


def _norm_cast(name, x, g, tm=512, comm=()):
    t, n = x.shape
    tm = min(tm, t)
    steps = t // tm
    kinds = [kind for kind, _ in comm]
    c_in, c_in_specs, c_out_specs, c_out_shape, c_scratch = _comm_io(comm)

    def body(*refs):
        x_ref, g_ref = refs[0], refs[1]
        o_ref = refs[2 + len(comm)]
        comm_refs = (kinds, refs[2:2 + len(comm)], refs[3 + len(comm):3 + 2 * len(comm)], *refs[3 + 2 * len(comm):])
        if comm:
            pl.when(pl.program_id(0) == 0)(lambda: _comm_start(*comm_refs))

        o_ref[...] = _rms(x_ref[...], g_ref[...]).astype(o_ref.dtype)
        if comm:
            pl.when(pl.program_id(0) == steps - 1)(lambda: _comm_finish(*comm_refs))

    res = pl.pallas_call(
        body, name=name, grid=(steps,),
        in_specs=[pl.BlockSpec((tm, n), lambda i: (i, 0)), pl.BlockSpec((1, n), lambda i: (0, 0))] + c_in_specs,
        out_specs=[pl.BlockSpec((tm, n), lambda i: (i, 0))] + c_out_specs,
        out_shape=[jax.ShapeDtypeStruct((t, n), _MXU)] + c_out_shape, scratch_shapes=c_scratch,
        compiler_params=pltpu.CompilerParams(dimension_semantics=("arbitrary",)),
    )(x, g, *c_in)
    return res[0], res[1:]


def _matmul(name, a, b, mode, tm, tn, tk, epilogue, outs, extras=(), comm=(), b_to_epilogue=False):
    m, k = a.shape[::-1] if mode == "tn" else a.shape
    n = b.shape[0] if mode == "nt" else b.shape[1]
    tm, tn, tk = min(tm, m), min(tn, n), min(tk, k)
    assert m % tm == 0 and n % tn == 0 and k % tk == 0, (name, m, n, k, tm, tn, tk)
    if mode == "nn":
        a_spec = pl.BlockSpec((tm, tk), lambda i, j, kk: (i, kk))
        b_spec = pl.BlockSpec((tk, tn), lambda i, j, kk: (kk, j))
        dotfn = _dot
    elif mode == "nt":
        a_spec = pl.BlockSpec((tm, tk), lambda i, j, kk: (i, kk))
        b_spec = pl.BlockSpec((tn, tk), lambda i, j, kk: (j, kk))
        dotfn = _dot_nt
    else:
        a_spec = pl.BlockSpec((tk, tm), lambda i, j, kk: (kk, i))
        b_spec = pl.BlockSpec((tk, tn), lambda i, j, kk: (kk, j))
        dotfn = _dot_tn
    ni, nj, nk = m // tm, n // tn, k // tk
    n_ex, n_out, n_comm = len(extras), len(outs), len(comm)
    kinds = [kind for kind, _ in comm]
    c_in, c_in_specs, c_out_specs, c_out_shape, c_scratch = _comm_io(comm)

    in_specs, vmem = [a_spec, b_spec], 2 * (tm * tk * a.dtype.itemsize + tk * tn * b.dtype.itemsize)
    for arr, kind in extras:
        if kind == "tile":
            in_specs.append(pl.BlockSpec((tm, tn), lambda i, j, kk: (i, j)))
            vmem += 2 * _nbytes((tm, tn), arr.dtype)
        elif kind == "rows":
            in_specs.append(pl.BlockSpec((tm, arr.shape[1]), lambda i, j, kk: (i, 0)))
            vmem += 2 * _nbytes((tm, arr.shape[1]), arr.dtype)
        elif kind == "full":
            in_specs.append(pl.BlockSpec(arr.shape, lambda i, j, kk: (0,) * arr.ndim))
            vmem += 2 * _nbytes(arr.shape, arr.dtype)
        else:
            in_specs.append(pl.BlockSpec((1, tn), lambda i, j, kk: (0, j)))
    out_specs, out_shape = [], []
    for kind, dt in outs:
        if kind == "tile":
            out_specs.append(pl.BlockSpec((tm, tn), lambda i, j, kk: (i, j)))
            out_shape.append(jax.ShapeDtypeStruct((m, n), dt))
            vmem += 2 * _nbytes((tm, tn), dt)
        elif kind == "tile_t":
            out_specs.append(pl.BlockSpec((tn, tm), lambda i, j, kk: (j, i)))
            out_shape.append(jax.ShapeDtypeStruct((n, m), dt))
            vmem += 2 * _nbytes((tm, tn), dt)
        else:
            assert nj == 1, "the partial-sum rows are accumulated over consecutive row tiles"
            out_specs.append(pl.BlockSpec((8, tn), lambda i, j, kk: (0, 0)))
            out_shape.append(jax.ShapeDtypeStruct((8, n), dt))
    scratch = [pltpu.VMEM((tm, tn), _F32)] if nk > 1 else []
    vmem += _nbytes((tm, tn), _F32) * 2

    def body(*refs):
        a_ref, b_ref = refs[0], refs[1]
        ex_refs = refs[2:2 + n_ex]
        n_in = 2 + n_ex + n_comm
        out_refs = refs[n_in:n_in + n_out]
        i, j, kk = pl.program_id(0), pl.program_id(1), pl.program_id(2)
        comm_refs = (kinds, refs[2 + n_ex:n_in], refs[n_in + n_out:n_in + n_out + n_comm], *refs[len(refs) - 3:])
        if n_comm:
            pl.when((i == 0) & (j == 0) & (kk == 0))(lambda: _comm_start(*comm_refs))

        b_val = _mx(b_ref[...])
        part = dotfn(_mx(a_ref[...]), b_val)

        def finish(acc):
            vals = epilogue(acc, *([b_val] if b_to_epilogue else []), *[r[...] for r in ex_refs])
            for r, v, (kind, _) in zip(out_refs, vals, outs):
                if kind == "part8":
                    @pl.when(i == 0)
                    def _():
                        r[...] = v

                    @pl.when(i > 0)
                    def _():
                        r[...] += v
                else:
                    r[...] = v.astype(r.dtype)

        if nk == 1:
            finish(part)
        else:
            acc_ref = refs[n_in + n_out + n_comm]

            @pl.when(kk == 0)
            def _():
                acc_ref[...] = part

            @pl.when(kk > 0)
            def _():
                acc_ref[...] += part

            @pl.when(kk == nk - 1)
            def _():
                finish(acc_ref[...])

        if n_comm:
            pl.when((i == ni - 1) & (j == nj - 1) & (kk == nk - 1))(lambda: _comm_finish(*comm_refs))

    carried =n_comm or any(kind == "part8" for kind, _ in outs)
    sem = ("arbitrary",) * 3 if carried else ("parallel", "parallel", "arbitrary")
    res = pl.pallas_call(
        body, name=name, grid=(ni, nj, nk),
        in_specs=in_specs + c_in_specs, out_specs=out_specs + c_out_specs, out_shape=out_shape + c_out_shape,
        scratch_shapes=scratch + c_scratch,
        compiler_params=pltpu.CompilerParams(dimension_semantics=sem, vmem_limit_bytes=_vmem_limit(vmem)),
    )(a, b, *[arr for arr, _ in extras], *c_in)
    return res[:n_out], res[n_out:]


def _shift_down(v, halo8, j):
    if j == 0:
        return v
    r = pltpu.roll(v, j, axis=0)
    hr = pltpu.roll(halo8, j, axis=0)
    top = jnp.where(_iota2(hr.shape, 0) < j, hr, r[:8])
    return jnp.concatenate([top, r[8:]], axis=0)


def _shift_up(v, next8, j):
    if j == 0:
        return v
    rows = v.shape[0]
    r = pltpu.roll(v, rows - j, axis=0)
    nr = pltpu.roll(next8, 8 - j, axis=0)
    bot = jnp.where(_iota2(nr.shape, 0) >= 8 - j, nr, r[rows - 8:])
    return jnp.concatenate([r[:rows - 8], bot], axis=0)


def _silu_grad(sig, silu):
    return sig + silu * (1.0 - sig)


def _gmlp_fwd_vals(pu, pv, gv, ws_ref, bsb_ref, want_bwd):
    tril = _iota2((CH, CH), 0) >= _iota2((CH, CH), 1)
    cdf_u = 0.5 * (1.0 + lax.erf(pu * 0.7071067811865476))
    cdf_v = 0.5 * (1.0 + lax.erf(pv * 0.7071067811865476))
    u = pu * cdf_u
    v = pv * cdf_v
    ys, keep = [], [(cdf_u, cdf_v)] if want_bwd else []
    for h in range(GM_H):
        sl = slice(h * 128, (h + 1) * 128)
        vh = v[:, sl]
        r = lax.rsqrt(jnp.mean(vh * vh, axis=-1, keepdims=True) + EPS)
        vn = vh * r * gv[:, sl]
        wm = _mx(jnp.where(tril, ws_ref[h], 0.0))
        mixed = _dot(wm, _mx(vn)) + bsb_ref[h]
        ys.append(u[:, sl] * mixed)
        if want_bwd:
            keep.append((vh, r, vn, wm, mixed))
    return jnp.concatenate(ys, axis=1), u, keep


def _ssd_common(xbc, halo8, dtraw, cw_ref, cb, dtb, alog, e_ref, ltri):
    xs = [_shift_down(xbc, halo8, j) for j in range(CONV_K)]
    cpre = cb + sum(cw_ref[k:k + 1, :] * xs[CONV_K - 1 - k] for k in range(CONV_K))
    sig = _sigmoid(cpre)
    act = cpre * sig
    dtin = dtraw + dtb
    dt = jnp.maximum(dtin, 0.0) + jnp.log(1.0 + jnp.exp(-jnp.abs(dtin)))
    a_neg = -jnp.exp(alog)
    cs = _xdot_left(_dot, ltri, dt * a_neg)
    cs_last = cs[CH - 1:CH, :]
    ecs = jnp.exp(cs)
    dec = jnp.exp(cs_last - cs)
    cdec = jnp.exp(cs_last)
    e = e_ref[...]
    dt_x = _dot(dt.astype(_BF16), e)
    ecs_x = _dot(ecs.astype(_BF16), e)
    dec_x = _dot(dec.astype(_BF16), e)
    cdec_x = _xdot(_dot, jnp.broadcast_to(cdec, (8, 128)), e)[0:1, :]
    return dict(xs=xs, sig=sig, act=act, dtin=dtin, dt=dt, a_neg=a_neg, cs=cs, ecs=ecs, dec=dec, cdec=cdec,
                dt_x=dt_x, ecs_x=ecs_x, dec_x=dec_x, cdec_x=cdec_x)


def _head_lm(cs, cst_ref, h, tril):
    seg = jnp.broadcast_to(cs[:, h:h + 1], (CH, CH)) - cst_ref[h:h + 1, :]
    return jnp.exp(jnp.where(tril, seg, -jnp.inf))


def _mixer_fwd(proj, gv, ws, bsb, gout, cw8, cb, dtb, alog, d_x, ng, e_mat, ltri_mat, seq_chunks, comm=()):
    t = proj.shape[0]
    n_seq = t // (seq_chunks * CH)
    sb = math.gcd(_SEQ_PER_STEP, n_seq)
    n_groups = n_seq // sb
    proj = proj.reshape(n_seq, seq_chunks * CH, proj.shape[1])
    n_comm = len(comm)
    kinds = [kind for kind, _ in comm]
    c_in, c_in_specs, c_out_specs, c_out_shape, c_scratch = _comm_io(comm)

    def body(*refs):
        comm_refs = (kinds, refs[18:18 + n_comm], refs[21 + n_comm:21 + 2 * n_comm], *refs[23 + 2 * n_comm:])
        grp, c = pl.program_id(0), pl.program_id(1)
        if n_comm:
            pl.when((grp == 0) & (c == 0))(lambda: _comm_start(*comm_refs))
            pl.when((grp == n_groups - 1) & (c == seq_chunks - 1))(lambda: _comm_finish(*comm_refs))
        for s in range(sb):
            per_seq = lambda rs: [r.at[s] for r in rs]
            one_chunk(c == 0, *per_seq(refs[:6]), *refs[6:18], *per_seq(refs[18 + n_comm:21 + n_comm]),
                      *per_seq(refs[21 + 2 * n_comm:23 + 2 * n_comm]))

    def one_chunk(first, pu_ref, pv_ref, z_ref, xbc_ref, dt_ref, halo_ref, gv_ref, ws_ref, bsb_ref, gout_ref, cw_ref,
                  cb_ref, dtb_ref, alog_ref, dx_ref, ng_ref, e_ref, ltri_ref, cat_ref, y_ref, st_ref, s_ref, cst_ref):
        tril = _iota2((CH, CH), 0) >= _iota2((CH, CH), 1)
        lane = _iota2((CH, 128), 1)

        y_a, _, _ = _gmlp_fwd_vals(pu_ref[...], pv_ref[...], gv_ref[...], ws_ref, bsb_ref, False)
        cat_ref[:, 0:GM_W] = _rms(y_a, gout_ref[...]).astype(cat_ref.dtype)

        @pl.when(first)
        def _():
            s_ref[...] = jnp.zeros_like(s_ref)

        halo8 = jnp.where(first, 0.0, halo_ref[...])
        q = _ssd_common(xbc_ref[...], halo8, dt_ref[...], cw_ref, cb_ref[...], dtb_ref[...], alog_ref[...], e_ref,
                        ltri_ref[...])
        act = q["act"]
        xv = act[:, 0:SSD_W]
        xdt = xv * q["dt_x"]
        xdt_m = _mx(xdt)
        cs = q["cs"]
        cst_ref[...] = cs.T
        s_prev = s_ref[...]
        st_ref[...] = s_prev
        ys = []
        for g in range(SSD_G):
            bg = _mx(act[:, SSD_W + g * SSD_N:SSD_W + (g + 1) * SSD_N])
            cg = _mx(act[:, SSD_W + SSD_G * SSD_N + g * SSD_N:SSD_W + SSD_G * SSD_N + (g + 1) * SSD_N])
            cbm = _dot_nt(cg, bg)
            gs = slice(g * 512, (g + 1) * 512)
            for pr in range(4):
                ps = slice(g * 512 + pr * 128, g * 512 + (pr + 1) * 128)
                o = []
                for hh in range(2):
                    h = g * 8 + pr * 2 + hh
                    m_h = _mx(cbm * _head_lm(cs, cst_ref, h, tril))
                    o.append(_dot(m_h, xdt_m[:, ps]))
                ys.append(jnp.where(lane < SSD_P, o[0], o[1]))
            sg = s_prev[:, gs]
            yoff = _dot(cg, _mx(sg)) * q["ecs_x"][:, gs]
            ys[-4:] = [ys[-4 + i] + yoff[:, i * 128:(i + 1) * 128] for i in range(4)]
            st_new = _dot_tn(bg, _mx(q["dec_x"][:, gs] * xdt[:, gs]))
            s_ref[:, gs] = sg * q["cdec_x"][:, gs] + st_new
        y = jnp.concatenate(ys, axis=1) + dx_ref[...] * xv
        y_ref[...] = y
        zv = z_ref[...]
        yg = y * (zv * _sigmoid(zv))
        for g in range(SSD_G):
            gs = slice(g * 512, (g + 1) * 512)
            cat_ref[:, GM_W + g * 512:GM_W + (g + 1) * 512] = _rms(yg[:, gs], ng_ref[:, gs]).astype(cat_ref.dtype)

    blk = lambda w, j: pl.BlockSpec((sb, CH, w), lambda g, c: (g, c, j))
    full = lambda arr: pl.BlockSpec(arr.shape, lambda g, c: (0,) * arr.ndim)
    consts = [gv, ws, bsb, gout, cw8, cb, dtb, alog, d_x, ng, e_mat, ltri_mat]
    seq = seq_chunks * CH
    res = pl.pallas_call(
        body, name="mixer_fwd", grid=(n_groups, seq_chunks),
        in_specs=[blk(GM_W, 0), blk(GM_W, 1), blk(SSD_W, 2), blk(CONV_CH, 2), blk(128, DT_BLK),
                  pl.BlockSpec((sb, 8, CONV_CH), lambda g, c: (g, jnp.maximum(c * (CH // 8) - 1, 0), 2))]
        + [full(a) for a in consts] + c_in_specs,
        out_specs=[blk(2 * D, 0), blk(SSD_W, 0), blk(SSD_W, 0)] + c_out_specs,
        out_shape=[jax.ShapeDtypeStruct((n_seq, seq, 2 * D), _MXU), jax.ShapeDtypeStruct((n_seq, seq, SSD_W), _F32),
                   jax.ShapeDtypeStruct((n_seq, seq, SSD_W), _F32)] + c_out_shape,
        scratch_shapes=[pltpu.VMEM((sb, SSD_N, SSD_W), _F32), pltpu.VMEM((sb, 128, CH), _F32)] + c_scratch,
        compiler_params=pltpu.CompilerParams(dimension_semantics=("arbitrary", "arbitrary"),
                                             vmem_limit_bytes=48 << 20),
    )(proj, proj, proj, proj, proj, proj, *consts, *c_in)
    return [r.reshape(t, r.shape[-1]) for r in res[:3]], res[3:]


def _mixer_bwd(proj, dcat, yss, states, gv, ws, bsb, gout, cw8, cb, dtb, alog, d_x, ng, e_mat, et_mat, ltri_mat,
               seq_chunks):
    t = proj.shape[0]
    n_seq = t // (seq_chunks * CH)
    sb = math.gcd(_SEQ_PER_STEP, n_seq)
    n_groups = n_seq // sb
    seq = seq_chunks * CH
    proj, dcat, yss, states = [a.reshape(n_seq, seq, a.shape[1]) for a in (proj, dcat, yss, states)]

    def body(*refs):
        acc_refs, shared = refs[23:33], refs[37:39]
        grp, i = pl.program_id(0), pl.program_id(1)

        @pl.when((grp == 0) & (i == 0))
        def _():
            for r in (*acc_refs, *shared, refs[35]):
                r[...] = jnp.zeros_like(r)

        @pl.when(i == 0)
        def _():
            refs[33][...] = jnp.zeros_like(refs[33])
            refs[34][...] = jnp.zeros_like(refs[34])

        for s in range(sb):
            per_seq = lambda rs: [r.at[s] for r in rs]
            last_of_all = ((grp == n_groups - 1) & (i == seq_chunks - 1)) if s == sb - 1 else None
            one_chunk(seq_chunks - 1 - i, last_of_all, *per_seq(refs[:9]), *refs[9:22], refs[22].at[s], *acc_refs, *per_seq(refs[33:37]), *shared)

    def one_chunk(c, finalize, pu_ref, pv_ref, z_ref, xbc_ref, dt_ref, halo_ref, dcat_ref, y_ref, st_ref,
                  gv_ref, ws_ref, bsb_ref, gout_ref, cw_ref, cb_ref, dtb_ref, alog_ref, dx_ref, ng_ref,
                  e_ref, et_ref, ltri_ref,
                  dproj_ref, dws_ref, dbs_ref, dgv_ref, dgout_ref, dng_ref, dcw_ref, dcb_ref, ddtb_ref, dalog_ref,
                  dd_ref, ds_ref, dnext_ref, dcst_ref, cst_ref, dbacc_ref, ddacc_ref):
        first = c == 0
        tril = _iota2((CH, CH), 0) >= _iota2((CH, CH), 1)
        lane = _iota2((CH, 128), 1)
        row = _iota2((CH, 128), 0)
        dcat_v = dcat_ref[...].astype(_F32)

        pu, pv = pu_ref[...], pv_ref[...]
        gv_v = gv_ref[...]
        y_a, u, keep = _gmlp_fwd_vals(pu, pv, gv_v, ws_ref, bsb_ref, True)
        dy, dgout8 = _rms_bwd(y_a, gout_ref[...], dcat_v[:, 0:GM_W])
        dgout_ref[...] += dgout8
        dus, dvs, dgvs = [], [], []
        for h in range(GM_H):
            sl = slice(h * 128, (h + 1) * 128)
            vh, r, vn, wm, mixed = keep[h + 1]
            dyh = dy[:, sl]
            dus.append(dyh * mixed)
            dmix = dyh * u[:, sl]
            dmix_m = _mx(dmix)
            dws_ref[h] += jnp.where(tril, _dot_nt(dmix_m, _mx(vn)), 0.0)
            dbacc_ref[h] += dmix
            dvn = _dot_tn(wm, dmix_m)
            gy = dvn * gv_v[:, sl]
            nh = vh * r
            dvs.append((gy - nh * jnp.mean(gy * nh, axis=-1, keepdims=True)) * r)
            dgvs.append(_sum8(dvn * nh))
        dgv_ref[...] += jnp.concatenate(dgvs, axis=1)
        cdf_u, cdf_v = keep[0]
        gelu_grad = lambda pre, cdf: cdf + pre * jnp.exp2(pre * pre * (-0.5 * _LOG2E) + _LOG2_INV_SQRT_2PI)
        dproj_ref[:, 0:GM_W] = (jnp.concatenate(dus, axis=1) * gelu_grad(pu, cdf_u)).astype(dproj_ref.dtype)
        dproj_ref[:, GM_W:2 * GM_W] = (jnp.concatenate(dvs, axis=1) * gelu_grad(pv, cdf_v)).astype(dproj_ref.dtype)

        halo8 = jnp.where(first, 0.0, halo_ref[...])
        q = _ssd_common(xbc_ref[...], halo8, dt_ref[...], cw_ref, cb_ref[...], dtb_ref[...], alog_ref[...], e_ref,
                        ltri_ref[...])
        act = q["act"]
        xv = act[:, 0:SSD_W]
        dt_x, ecs_x, dec_x, cdec_x = q["dt_x"], q["ecs_x"], q["dec_x"], q["cdec_x"]
        xdt = xv * dt_x
        xdt_m = _mx(xdt)
        cs = q["cs"]
        cst_ref[...] = cs.T
        s_prev = st_ref[...]
        ds = ds_ref[...]
        yv = y_ref[...]
        zv = z_ref[...]
        sig_z = _sigmoid(zv)
        sz = zv * sig_z
        yg = yv * sz
        dygs, dng8 = [], []
        for g in range(SSD_G):
            gs = slice(g * 512, (g + 1) * 512)
            a_, b_ = _rms_bwd(yg[:, gs], ng_ref[:, gs], dcat_v[:, GM_W + g * 512:GM_W + (g + 1) * 512])
            dygs.append(a_)
            dng8.append(b_)
        dyg = jnp.concatenate(dygs, axis=1)
        dng_ref[...] += jnp.concatenate(dng8, axis=1)
        dyv = dyg * sz
        dproj_ref[:, 2 * GM_W:2 * GM_W + SSD_W] = (dyg * yv * _silu_grad(sig_z, sz)).astype(dproj_ref.dtype)
        ddacc_ref[...] += _sum8(dyv * xv)
        dyv_m = _mx(dyv)

        dxdt_parts, db_parts, dc_parts = [], [], []
        dcs = jnp.zeros((CH, 128), _F32)
        dcs_x_parts, ddec_x_parts, dcl_x_parts = [], [], []
        for g in range(SSD_G):
            gs = slice(g * 512, (g + 1) * 512)
            bg = _mx(act[:, SSD_W + g * SSD_N:SSD_W + (g + 1) * SSD_N])
            cg = _mx(act[:, SSD_W + SSD_G * SSD_N + g * SSD_N:SSD_W + SSD_G * SSD_N + (g + 1) * SSD_N])
            cbm = _dot_nt(cg, bg)
            sg = s_prev[:, gs]
            sg_m = _mx(sg)
            dsg = ds[:, gs]
            dsg_m = _mx(dsg)
            zoff = _dot(cg, sg_m)
            dz_off = dyv[:, gs] * ecs_x[:, gs]
            dz_off_m = _mx(dz_off)
            dcs_x_parts.append(dyv[:, gs] * zoff * ecs_x[:, gs])
            dcg = _dot_nt(dz_off_m, sg_m)
            dsprev = _dot_tn(cg, dz_off_m)
            w_st = dec_x[:, gs] * xdt[:, gs]
            dw_st = _dot(bg, dsg_m)
            dbg = _dot_nt(_mx(w_st), dsg_m)
            dxdt_g = dec_x[:, gs] * dw_st
            ddec_x_parts.append(dw_st * xdt[:, gs])
            dsprev = dsprev + cdec_x[:, gs] * dsg
            dcl_x_parts.append(jnp.sum(dsg * sg, axis=0, keepdims=True) * cdec_x[:, gs])
            ds_ref[:, gs] = dsprev
            dcb = jnp.zeros((CH, CH), _F32)
            dxdt_pairs = []
            for pr in range(4):
                ps = slice(g * 512 + pr * 128, g * 512 + (pr + 1) * 128)
                acc_pair = None
                for hh in range(2):
                    h = g * 8 + pr * 2 + hh
                    in_head = (lane < SSD_P) if hh == 0 else (lane >= SSD_P)
                    lm = _head_lm(cs, cst_ref, h, tril)
                    m_h = cbm * lm
                    m_hm = _mx(m_h)
                    dyh_m = _mx(jnp.where(in_head, dyv[:, ps], 0.0))
                    dm = _dot_nt(dyh_m, xdt_m[:, ps])
                    dcb = dcb + dm * lm
                    qm = dm * m_h
                    dcs = dcs + jnp.where(lane == h, jnp.sum(qm, axis=1, keepdims=True), 0.0)
                    dcst_ref[h:h + 1, :] = jnp.sum(qm, axis=0, keepdims=True)
                    contrib = jnp.where(in_head, _dot_tn(m_hm, dyv_m[:, ps]), 0.0)
                    acc_pair = contrib if acc_pair is None else acc_pair + contrib
                dxdt_pairs.append(acc_pair)
            dxdt_parts.append(dxdt_g + jnp.concatenate(dxdt_pairs, axis=1))
            dcb_m = _mx(dcb)
            dc_parts.append(dcg + _dot(dcb_m, bg))
            db_parts.append(dbg + _dot_tn(dcb_m, cg))
        dxdt = jnp.concatenate(dxdt_parts, axis=1)
        dxv = dx_ref[...] * dyv + dxdt * dt_x
        et = et_ref[...]
        head_sum = lambda v: _dot(v.astype(_BF16), et)
        ddt = head_sum(dxdt * xv)
        dcs = dcs - dcst_ref[...].T + head_sum(jnp.concatenate(dcs_x_parts, axis=1))
        ddec = head_sum(jnp.concatenate(ddec_x_parts, axis=1)) * q["dec"]
        dcs = dcs - ddec
        dcl = jnp.sum(ddec, axis=0, keepdims=True) + _xdot(
            _dot, jnp.broadcast_to(jnp.concatenate(dcl_x_parts, axis=1), (8, SSD_W)), et)[0:1, :]
        dcs = jnp.where(row == CH - 1, dcs + dcl, dcs)
        da = _xdot_left(_dot_tn, ltri_ref[...], dcs)
        ddt = ddt + da * q["a_neg"]
        dalog_ref[...] += _sum8(da * q["dt"] * q["a_neg"])
        ddtraw = jnp.where(lane < SSD_H, ddt * _sigmoid(q["dtin"]), 0.0)
        ddtb_ref[...] += _sum8(ddtraw)
        dproj_ref[:, D_IN_PAD - 128:D_IN_PAD] = ddtraw.astype(dproj_ref.dtype)
        dcpre = jnp.concatenate([dxv] + db_parts + dc_parts, axis=1) * _silu_grad(q["sig"], act)
        dcb_ref[...] += _sum8(dcpre)
        for k in range(CONV_K):
            dcw_ref[k:k + 1, :] += jnp.sum(dcpre * q["xs"][CONV_K - 1 - k], axis=0, keepdims=True)
        next8 = dnext_ref[...]
        dxbc = sum(cw_ref[k:k + 1, :] * _shift_up(dcpre, next8, CONV_K - 1 - k) for k in range(CONV_K))
        dproj_ref[:, 2 * GM_W + SSD_W:2 * GM_W + SSD_W + CONV_CH] = dxbc.astype(dproj_ref.dtype)
        dnext_ref[...] = dcpre[0:8, :]

        if finalize is not None:
            @pl.when(finalize)
            def _():
                for h in range(GM_H):
                    dbs_ref[h:h + 1, :] = _xdot_left(_dot_nt, jnp.ones((8, 128), _BF16), dbacc_ref[h])[0:1, :]
                dd_ref[...] = _xdot(_dot, ddacc_ref[...], et)

    rblk = lambda w, j: pl.BlockSpec((sb, CH, w), lambda g, i: (g, seq_chunks - 1 - i, j))
    full = lambda arr: pl.BlockSpec(arr.shape, lambda g, i: (0,) * arr.ndim)
    acc = lambda shape: pl.BlockSpec(shape, lambda g, i: (0,) * len(shape))
    consts = [gv, ws, bsb, gout, cw8, cb, dtb, alog, d_x, ng, e_mat, et_mat, ltri_mat]
    acc_shapes = [(GM_H, CH, CH), (8, 128), (8, GM_W), (8, GM_W), (8, SSD_W), (8, CONV_CH), (8, CONV_CH), (8, 128),
                  (8, 128), (8, 128)]
    res = pl.pallas_call(
        body, name="mixer_bwd", grid=(n_groups, seq_chunks),
        in_specs=[rblk(GM_W, 0), rblk(GM_W, 1), rblk(SSD_W, 2), rblk(CONV_CH, 2), rblk(128, DT_BLK),
                  pl.BlockSpec((sb, 8, CONV_CH),
                               lambda g, i: (g, jnp.maximum((seq_chunks - 1 - i) * (CH // 8) - 1, 0), 2)),
                  rblk(2 * D, 0), rblk(SSD_W, 0), rblk(SSD_W, 0)] + [full(a) for a in consts],
        out_specs=[rblk(D_IN_PAD, 0)] + [acc(s) for s in acc_shapes],
        out_shape=[jax.ShapeDtypeStruct((n_seq, seq, D_IN_PAD), _MXU)]
        + [jax.ShapeDtypeStruct(s, _F32) for s in acc_shapes],
        scratch_shapes=[pltpu.VMEM((sb, SSD_N, SSD_W), _F32), pltpu.VMEM((sb, 8, CONV_CH), _F32),
                        pltpu.VMEM((sb, 128, CH), _F32), pltpu.VMEM((sb, 128, CH), _F32),
                        pltpu.VMEM((GM_H, CH, 128), _F32), pltpu.VMEM((8, SSD_W), _F32)],
        compiler_params=pltpu.CompilerParams(dimension_semantics=("arbitrary", "arbitrary"),
                                             vmem_limit_bytes=48 << 20),
    )(proj, proj, proj, proj, proj, proj, dcat, yss, states, *consts)
    return [res[0].reshape(t, D_IN_PAD)] + list(res[1:])


def _peers():
    x, y, c = lax.axis_index("x"), lax.axis_index("y"), lax.axis_index("c")
    out = []
    for k in range(1, N_DEV):
        fx, fy, fc = (k >> 2) & 1, (k >> 1) & 1, k & 1
        px, py, pc = (x + fx) % 2, (y + fy) % 2, (c + fc) % 2
        out.append((k - 1, (px, py, pc), 4 * px + 2 * py + pc))
    return out, 4 * x + 2 * y + c


def _comm_io(comm):
    any_spec = pl.BlockSpec(memory_space=pl.ANY)
    n = len(comm)
    out_shape = []
    for (kind, axis), src in comm:
        shp = list(src.shape)
        if kind in ("gather", "gather2"):
            shp[axis] *= N_DEV
        else:
            shp[axis] //= N_DEV
            shp = [N_DEV] + shp
        out_shape.append(jax.ShapeDtypeStruct(tuple(shp), src.dtype))
    scratch = [pltpu.SemaphoreType.DMA((n * (N_DEV - 1),)), pltpu.SemaphoreType.DMA((n * (N_DEV - 1),)),
               pltpu.SemaphoreType.DMA((n,))] if n else []
    return [src for _, src in comm], [any_spec] * n, [any_spec] * n, out_shape, scratch


def _window(ref, axis, idx, size):
    start = pl.multiple_of(idx * size, size)
    return ref.at[tuple(pl.ds(start, size) if a == axis else slice(None) for a in range(len(ref.shape)))]


def _comm_plans(kinds, src_refs, dst_refs, send_sems, recv_sems, local_sems):
    x, y, c = lax.axis_index("x"), lax.axis_index("y"), lax.axis_index("c")
    peers, me = _peers()
    plans = []
    for s, ((kind, axis), src, dst) in enumerate(zip(kinds, src_refs, dst_refs)):
        sems = lambda k: dict(send_sem=send_sems.at[s * (N_DEV - 1) + k], recv_sem=recv_sems.at[s * (N_DEV - 1) + k])
        remote = lambda src_ref, dst_ref, k, pid: pltpu.make_async_remote_copy(
            src_ref=src_ref, dst_ref=dst_ref, device_id=pid, device_id_type=_MESH, **sems(k))
        if kind == "gather2":
            size = src.shape[axis]
            win = lambda idx: _window(dst, axis, idx, size)
            sib, sib_idx = (x, y, 1 - c), 4 * x + 2 * y + (1 - c)
            local = pltpu.make_async_copy(src, win(me), local_sems.at[s])
            to_sib = remote(src, win(me), 0, sib)
            starts, forwards = [local, to_sib], []
            waits = [(local, "local"), (to_sib, "send"), (remote(src, win(sib_idx), 0, sib), "recv")]
            for j, (fx, fy) in enumerate(((1, 0), (0, 1), (1, 1))):
                px, py = (x + fx) % 2, (y + fy) % 2
                same, other = 4 * px + 2 * py + c, 4 * px + 2 * py + (1 - c)
                out = remote(src, win(me), 1 + j, (px, py, c))
                starts.append(out)
                passed = remote(win(same), win(same), 4 + j, sib)
                forwards.append((remote(src, win(same), 1 + j, (px, py, c)), passed))
                waits += [(out, "send"), (passed, "send"), (remote(win(other), win(other), 4 + j, sib), "recv")]
            plans.append((starts, forwards, waits))
            continue
        if kind == "gather":
            size = src.shape[axis]
            src_for = lambda pidx: src
            dst_mine = _window(dst, axis, me, size)
        else:
            size = src.shape[axis] // N_DEV
            src_for = lambda pidx: _window(src, axis, pidx, size)
            dst_mine = dst.at[me]
        local = pltpu.make_async_copy(src_for(me), dst_mine, local_sems.at[s])
        remotes = [remote(src_for(pidx), dst_mine, k, pid) for k, pid, pidx in peers]
        plans.append(([local] + remotes, [], [(local, "local")] + [(cp, "both") for cp in remotes]))
    return plans


def _comm_start(*refs):
    for starts, _, _ in _comm_plans(*refs):
        for cp in starts:
            cp.start()


def _comm_finish(*refs):
    for _, forwards, waits in _comm_plans(*refs):
        for arrival, cp in forwards:
            arrival.wait_recv()
            cp.start()
        for cp, what in waits:
            if what == "send":
                cp.wait_send()
            elif what == "recv":
                cp.wait_recv()
            else:
                cp.wait()


def _adam_vals(w, g, m, v):
    m = B1 * m + (1.0 - B1) * g
    v = B2 * v + (1.0 - B2) * (g * g)
    m_hat = m / (1.0 - B1 ** STEP)
    v_hat = v / (1.0 - B2 ** STEP)
    delta = -LR * (m_hat / (jnp.sqrt(v_hat) + ADAM_EPS) + WD * w)
    return delta, m, v


def _sum_adam(name, recv, w, m, v, tile=256):
    _, r, wd = recv.shape
    if r % min(tile, r) == 0:
        tr, tc = min(tile, r), wd
    else:
        tr, tc = r, tile
        assert wd % tc == 0, (name, r, wd)

    def body(recv_ref, w_ref, m_ref, v_ref, g_out, d_out, m_out, v_out):
        g = recv_ref[0].astype(_F32)
        for s in range(1, N_DEV):
            g = g + recv_ref[s].astype(_F32)
        d_, m_, v_ = _adam_vals(w_ref[...], g, m_ref[...], v_ref[...])
        g_out[...] = g
        d_out[...] = d_
        m_out[...] = m_
        v_out[...] = v_

    spec = pl.BlockSpec((tr, tc), lambda i, j: (i, j))
    return pl.pallas_call(
        body, name=name, grid=(r // tr, wd // tc),
        in_specs=[pl.BlockSpec((N_DEV, tr, tc), lambda i, j: (0, i, j)), spec, spec, spec],
        out_specs=[spec] * 4, out_shape=[jax.ShapeDtypeStruct((r, wd), _F32)] * 4,
        compiler_params=pltpu.CompilerParams(dimension_semantics=("parallel", "parallel"),
                                             vmem_limit_bytes=48 << 20),
    )(recv, w, m, v)


def _small_local(parts, segments, n_rows):
    def body(parts_ref, loc_ref):
        loc_ref[...] = jnp.zeros_like(loc_ref)
        for out_row, n_out, in_row, n_in, kind in segments:
            if kind == "copy":
                loc_ref[out_row:out_row + n_out, :] = parts_ref[in_row:in_row + n_in, :]
            else:
                s = jnp.sum(parts_ref[in_row:in_row + n_in, :], axis=0, keepdims=True)
                if kind == "loss":
                    s = jnp.broadcast_to(jnp.sum(s, axis=1, keepdims=True) * (0.5 / D), (1, D))
                loc_ref[out_row:out_row + 1, :] = s

    vm = pl.BlockSpec(memory_space=pltpu.VMEM)
    return pl.pallas_call(body, name="small_local", in_specs=[vm], out_specs=vm,
                          out_shape=jax.ShapeDtypeStruct((n_rows, D), _F32))(parts)


def _small_final(blocks, late8, late_row, w, m, v):
    n_rows = w.shape[0]

    def body(blocks_ref, late_ref, w_ref, m_ref, v_ref, g_out, d_out, m_out, v_out, loc_ref, recv_ref, send_sems,
             recv_sems):
        peers, me = _peers()
        loc_ref[...] = jnp.broadcast_to(jnp.sum(late_ref[...], axis=0, keepdims=True), (8, D))
        recv_ref[me] = loc_ref[...]
        copies = [pltpu.make_async_remote_copy(src_ref=loc_ref, dst_ref=recv_ref.at[me], send_sem=send_sems.at[k],
                                               recv_sem=recv_sems.at[k], device_id=pid, device_id_type=_MESH)
                  for k, pid, _ in peers]
        for cp in copies:
            cp.start()
        g = blocks_ref[0:n_rows, :]
        for s in range(1, N_DEV):
            g = g + blocks_ref[s * n_rows:(s + 1) * n_rows, :]
        for cp in copies:
            cp.wait()
        late = recv_ref[0]
        for s in range(1, N_DEV):
            late = late + recv_ref[s]
        g = jnp.where(_iota2((n_rows, D), 0) == late_row, jnp.broadcast_to(late[0:1, :], (n_rows, D)), g)
        d_, m_, v_ = _adam_vals(w_ref[...], g, m_ref[...], v_ref[...])
        g_out[...] = g
        d_out[...] = d_
        m_out[...] = m_
        v_out[...] = v_

    vm = pl.BlockSpec(memory_space=pltpu.VMEM)
    return pl.pallas_call(
        body, name="small_final", in_specs=[vm] * 5, out_specs=[vm] * 4,
        out_shape=[jax.ShapeDtypeStruct((n_rows, D), _F32)] * 4,
        scratch_shapes=[pltpu.VMEM((8, D), _F32), pltpu.VMEM((N_DEV, 8, D), _F32),
                        pltpu.SemaphoreType.DMA((N_DEV - 1,)), pltpu.SemaphoreType.DMA((N_DEV - 1,))],
        compiler_params=pltpu.CompilerParams(vmem_limit_bytes=48 << 20),
    )(blocks, late8, w, m, v)


_BIG_NAMES =("w_in", "w_out", "w_ff1", "w_ff2", "w_ple_gate", "w_ple_proj")

_G_VECS = ("norm_mix_g", "gm_v_norm_g", "gm_out_norm_g", "ssd_norm_g", "norm_mlp_g", "ple_norm_g", "final_norm_g")
_LATE = _G_VECS[0]


def _const_mats():
    h = np.arange(128)[:, None]
    ch = np.arange(SSD_W)[None, :]
    e = (ch // SSD_P == h).astype(np.float32)
    ltri = (np.arange(CH)[:, None] >= np.arange(CH)[None, :]).astype(np.float32)
    return jnp.asarray(e, _BF16), jnp.asarray(e.T, _BF16), jnp.asarray(ltri, _BF16)


def _pad_lanes(v, n=128):
    return jnp.pad(v, ((0, 0), (0, n - v.shape[1])))


def _local_step(x, p, tgt, shard, conv_w_shard, small, seq_len):
    seq_chunks = seq_len // CH
    e_mat, et_mat, ltri_mat = _const_mats()
    g_mix, g_mlp, g_ple = small["norm_mix_g"], small["norm_mlp_g"], small["ple_norm_g"]
    g_fin = small["final_norm_g"].reshape(1, D)
    gv, gout, ng = small["gm_v_norm_g"], small["gm_out_norm_g"], small["ssd_norm_g"]
    ws = small["gm_ws"][0]
    bsb = jnp.broadcast_to(small["gm_bs"][0][:, :, None], (GM_H, CH, 128))
    cb = small["ssd_conv_b"]
    dtb, alog = _pad_lanes(small["ssd_dt_bias"]), _pad_lanes(small["ssd_a_log"])
    d_x = jnp.repeat(small["ssd_d"], SSD_P, axis=1)

    first = lambda acc: (acc,)
    rows, cols = ("gather2", 0), ("gather2", 1)
    n1, (g_win, g_cw) = _norm_cast("norm_mix", x, g_mix,
                                   comm=[(rows, shard["w_in"][None]), (("gather", 0), conv_w_shard[None])])
    w_in_t = jnp.pad(g_win.reshape(D_IN, D), ((0, D_IN_PAD - D_IN), (0, 0)))
    cw8 = jnp.pad(g_cw.transpose(1, 0, 2).reshape(CONV_K, CONV_CH), ((0, 8 - CONV_K), (0, 0)))
    mix_consts = (gv, ws, bsb, gout, cw8, cb, dtb, alog, d_x, ng)
    (proj,), (w_out, w1) = _matmul("proj_in", n1, w_in_t, "nt", 512, D_IN_PAD, D, first, [("tile", _F32)],
                                   comm=[(rows, shard["w_out"]), (cols, shard["w_ff1"])])
    (cat, yss, states), (w2, wg, wp) = _mixer_fwd(
        proj, *mix_consts, e_mat, ltri_mat, seq_chunks,
        comm=[(rows, shard["w_ff2"]), (rows, shard["w_ple_gate"]), (cols, shard["w_ple_proj"])])

    def epi_res_norm(acc, res, g):
        hv = acc + res
        return hv, _rms(hv, g)

    (h1, n2), _ = _matmul("proj_out", cat, w_out, "nn", 1024, D, 2 * D, epi_res_norm,
                          [("tile", _F32), ("tile", _MXU)], extras=[(x, "tile"), (g_mlp, "row")])

    def epi_relu2(acc):
        hid = jnp.maximum(acc, 0.0)
        return hid, hid * hid

    (hid, hid2), _ = _matmul("ff1", n2, w1, "nn", 1024, 1024, D, epi_relu2, [("tile", _MXU), ("tile", _MXU)])
    (h2, n3), _ = _matmul("ff2", hid2, w2, "nn", 512, D, D_FF, epi_res_norm, [("tile", _F32), ("tile", _MXU)],
                          extras=[(h1, "tile"), (g_ple, "row")])

    def epi_norm_bwd(acc, up, hv, g):
        dx, dg8 = _rms_bwd(hv, g, acc)
        dh = up + dx
        return dh, dh, dg8

    def epi_head(acc, wg_v, p_rows, wp_v, h2v, tg, gf, gp_):
        ppv = _dot(_mx(p_rows), wp_v)
        gate = _sigmoid(acc)
        gp = gate * ppv
        h3 = h2v + gp
        r = lax.rsqrt(jnp.mean(h3 * h3, axis=-1, keepdims=True) + EPS)
        nh = h3 * r
        err = nh * gf - tg
        gy = err * (gf * (1.0 / D))
        dh3 = (gy - nh * jnp.mean(gy * nh, axis=-1, keepdims=True)) * r
        dpp = dh3 * gate
        da3 = dpp * (ppv - gp)
        dh2, dh2_again, dgple8 = epi_norm_bwd(_dot_nt(_mx(da3), wg_v), dh3, h2v, gp_)
        return da3, dpp, dh2, dh2_again, _sum8(err * err), _sum8(err * nh) * (1.0 / D), dgple8

    (da3, dpp, dh2, dh2b, lossp, dgfin, dgple), _ = _matmul(
        "ple_gate_loss_bwd", n3, wg, "nn", 512, D, D, epi_head,
        [("tile", _MXU), ("tile", _MXU), ("tile", _F32), ("tile", _MXU), ("part8", _F32), ("part8", _F32),
         ("part8", _F32)],
        extras=[(p, "rows"), (wp, "full"), (h2, "tile"), (tgt, "tile"), (g_fin, "row"), (g_ple, "row")],
        b_to_epilogue=True)

    s_rows, s_cols = ("scatter", 0), ("scatter", 1)
    (dwp,), _ = _matmul("d_w_ple_proj", p, dpp, "tn", D_PLE, D, 2048, first, [("tile", _BF16)])
    (dwg,), _ = _matmul("d_w_ple_gate", n3, da3, "tn", D, D, 4096, first, [("tile", _BF16)])
    (dw2,), (r_wp, r_wg) = _matmul("d_w_ff2", hid2, dh2b, "tn", 1024, D, 4096, first, [("tile", _BF16)],
                                   comm=[(s_cols, dwp), (s_rows, dwg)])
    (da1,), (r_w2,) = _matmul("d_ff_hidden", dh2b, w2, "nt", 512, 2048, D,
                              lambda acc, hv: (acc * 2.0 * hv.astype(_F32),), [("tile", _MXU)],
                              extras=[(hid, "tile")], comm=[(s_rows, dw2)])
    (dw1,), _ = _matmul("d_w_ff1", n2, da1, "tn", 1024, 1024, 4096, first, [("tile", _BF16)])
    (dh1, dh1b, dgmlp), (r_w1,) = _matmul(
        "d_h1", da1, w1, "nt", 512, D, D_FF, epi_norm_bwd, [("tile", _F32), ("tile", _MXU), ("part8", _F32)],
        extras=[(dh2, "tile"), (h1, "tile"), (g_mlp, "row")], comm=[(s_cols, dw1)])
    (dwout,), _ = _matmul("d_w_out", cat, dh1b, "tn", 1024, D, 4096, first, [("tile", _BF16)])
    (dcat,), (r_wout,) = _matmul("d_cat", dh1b, w_out, "nt", 1024, 1024, D, first, [("tile", _MXU)],
                                 comm=[(s_rows, dwout)])
    (dproj, dws, dbs, dgv, dgout, dng, dcw, dcb, ddtb, dalog, dd) = _mixer_bwd(
        proj, dcat, yss, states, *mix_consts, e_mat, et_mat, ltri_mat, seq_chunks)
    pieces = dict(gm_v_norm_g=dgv, gm_out_norm_g=dgout, ssd_norm_g=dng, norm_mlp_g=dgmlp, ple_norm_g=dgple,
                  final_norm_g=dgfin, gm_ws=dws, gm_bs=dbs, ssd_conv_w=dcw, ssd_conv_b=dcb, ssd_dt_bias=ddtb,
                  ssd_a_log=dalog, ssd_d=dd, loss=lossp)
    parts, segments, n_rows, where = _small_layout(pieces)
    small_block = _small_local(parts, segments, n_rows)
    (dwin_t,), (small_blocks,) = _matmul("d_w_in", n1, dproj, "tn", 512, D_IN_PAD, 1024, lambda acc: (acc.T,),
                                         [("tile_t", _BF16)], comm=[(("gather", 0), small_block)])
    dwin_blocks = dwin_t[:D_IN].reshape(N_DEV, SHARD_IN, D)
    (gx, dgmix), (r_win,) = _matmul(
        "d_x", dproj, w_in_t, "nn", 512, D, D_IN_PAD, lambda *a: epi_norm_bwd(*a)[1:],
        [("tile", _F32), ("part8", _F32)], extras=[(dh1, "tile"), (x, "tile"), (g_mix, "row")],
        comm=[(s_rows, dwin_blocks)])
    r_win = r_win.reshape(N_DEV, SHARD_IN, D)

    big = dict(w_in=r_win, w_out=r_wout, w_ff1=r_w1, w_ff2=r_w2, w_ple_gate=r_wg, w_ple_proj=r_wp)
    return gx, big, small_blocks, dgmix, n_rows, where


def _small_layout(pieces):
    rows, segments = [], []
    in_row, out_row = 0, 0

    def add(arr, kind, n_out):
        nonlocal in_row, out_row
        rows.append(arr)
        segments.append((out_row, n_out, in_row, arr.shape[0], kind))
        start = out_row
        in_row += arr.shape[0]
        out_row += n_out
        return start

    where = {_LATE: 0}
    out_row = 1
    for name in _G_VECS[1:]:
        where[name] = add(pieces[name], "sum", 1)
    where["gm_ws"] = add(pieces["gm_ws"].reshape(GM_H * CH * CH // D, D), "copy", GM_H * CH * CH // D)
    where["gm_bs"] = add(pieces["gm_bs"].reshape(1, D), "copy", 1)
    cb = jnp.pad(pieces["ssd_conv_b"], ((0, 0), (0, 2 * D - CONV_CH)))
    where["ssd_conv_b"] = add(cb[:, :D], "sum", 1)
    add(cb[:, D:], "sum", 1)
    cw = jnp.pad(pieces["ssd_conv_w"][:CONV_K], ((0, 0), (0, 2 * D - CONV_CH)))
    where["ssd_conv_w"] = add(cw.reshape(2 * CONV_K, D), "copy", 2 * CONV_K)
    misc = jnp.concatenate([pieces["ssd_dt_bias"], pieces["ssd_a_log"], pieces["ssd_d"],
                            jnp.zeros((8, D - 3 * 128), _F32)], axis=1)
    where["misc"] = add(misc, "sum", 1)
    where["loss"] = add(pieces["loss"], "loss", 1)
    n_rows = -(-out_row // 8) * 8
    return jnp.concatenate(rows, axis=0), tuple(segments), n_rows, where


def _pack_small_params(vals, where, n_rows, my_block):
    rows, at = [], {}

    def add(name, arr):
        at[name] = sum(r.shape[0] for r in rows)
        rows.append(arr)

    for name in _G_VECS:
        add(name, vals[name].reshape(1, D))
    add("gm_ws", vals["gm_ws"].reshape(GM_H * CH * CH // D, D))
    add("gm_bs", vals["gm_bs"].reshape(1, D))
    cb = jnp.pad(vals["ssd_conv_b"].reshape(1, CONV_CH), ((0, 0), (0, 2 * D - CONV_CH)))
    add("ssd_conv_b", cb.reshape(2, D))
    cw = lax.dynamic_update_slice(jnp.zeros((CONV_K, 2 * D), _F32), vals["ssd_conv_w"].reshape(CONV_K, -1),
                                  (0, my_block * (CONV_CH // N_DEV)))
    add("ssd_conv_w", cw.reshape(2 * CONV_K, D))
    misc = jnp.concatenate([_pad_lanes(vals["ssd_dt_bias"].reshape(1, SSD_H)),
                            _pad_lanes(vals["ssd_a_log"].reshape(1, SSD_H)),
                            _pad_lanes(vals["ssd_d"].reshape(1, SSD_H)), jnp.zeros((1, D - 3 * 128), _F32)], axis=1)
    add("misc", misc)
    assert all(where[k] == r for k, r in at.items()), (where, at)
    rows.append(jnp.zeros((n_rows - sum(r.shape[0] for r in rows), D), _F32))
    return jnp.concatenate(rows, axis=0)


def _unpack_small(buf, where, my_block, shapes):
    out = {}
    for name in _G_VECS:
        out[name] = buf[where[name]].reshape(shapes[name])
    n_ws = GM_H * CH * CH // D
    out["gm_ws"] = buf[where["gm_ws"]:where["gm_ws"] + n_ws].reshape(shapes["gm_ws"])
    out["gm_bs"] = buf[where["gm_bs"]].reshape(shapes["gm_bs"])
    r = where["ssd_conv_b"]
    out["ssd_conv_b"] = buf[r:r + 2].reshape(1, 2 * D)[:, :CONV_CH].reshape(shapes["ssd_conv_b"])
    r = where["ssd_conv_w"]
    cw = buf[r:r + 2 * CONV_K].reshape(CONV_K, 2 * D)
    out["ssd_conv_w"] = lax.dynamic_slice(cw, (0, my_block * (CONV_CH // N_DEV)),
                                          (CONV_K, CONV_CH // N_DEV)).reshape(shapes["ssd_conv_w"])
    misc = buf[where["misc"]]
    for i, name in enumerate(("ssd_dt_bias", "ssd_a_log", "ssd_d")):
        out[name] = misc[i * 128:i * 128 + SSD_H].reshape(shapes[name])
    return out


_WEIGHTS = ("norm_mix_g", "w_in", "gm_v_norm_g", "gm_ws", "gm_bs", "gm_out_norm_g", "ssd_conv_w", "ssd_conv_b",
            "ssd_dt_bias", "ssd_a_log", "ssd_d", "ssd_norm_g", "w_out", "norm_mlp_g", "w_ff1", "w_ff2", "ple_norm_g",
            "w_ple_gate", "w_ple_proj", "final_norm_g")


def kernel(x, p, norm_mix_g, w_in, gm_v_norm_g, gm_ws, gm_bs, gm_out_norm_g, ssd_conv_w, ssd_conv_b, ssd_dt_bias, ssd_a_log, ssd_d, ssd_norm_g, w_out, norm_mlp_g, w_ff1, w_ff2, ple_norm_g, w_ple_gate, w_ple_proj, final_norm_g, loss_target, m_norm_mix_g, m_w_in, m_gm_v_norm_g, m_gm_ws, m_gm_bs, m_gm_out_norm_g, m_ssd_conv_w, m_ssd_conv_b, m_ssd_dt_bias, m_ssd_a_log, m_ssd_d, m_ssd_norm_g, m_w_out, m_norm_mlp_g, m_w_ff1, m_w_ff2, m_ple_norm_g, m_w_ple_gate, m_w_ple_proj, m_final_norm_g, v_norm_mix_g, v_w_in, v_gm_v_norm_g, v_gm_ws, v_gm_bs, v_gm_out_norm_g, v_ssd_conv_w, v_ssd_conv_b, v_ssd_dt_bias, v_ssd_a_log, v_ssd_d, v_ssd_norm_g, v_w_out, v_norm_mlp_g, v_w_ff1, v_w_ff2, v_ple_norm_g, v_w_ple_gate, v_w_ple_proj, v_final_norm_g):
    args = dict(locals())
    w = {n: args[n] for n in _WEIGHTS}
    m = {n: args["m_" + n] for n in _WEIGHTS}
    v = {n: args["v_" + n] for n in _WEIGHTS}
    shapes = {n: w[n].shape for n in _WEIGHTS}
    my_block = 4 * lax.axis_index("x") + 2 * lax.axis_index("y") + lax.axis_index("c")
    nb, seq_len, _ = x.shape

    local = lambda d, n: d[n][0].T if n == "w_in" else d[n][0]
    shard = {n: local(w, n).astype(_MXU) for n in _BIG_NAMES}
    small = {n: w[n] for n in _WEIGHTS if n not in _BIG_NAMES}
    gx, recv, small_blocks, late8, n_rows, where = _local_step(
        x.reshape(nb * seq_len, D), p.reshape(nb * seq_len, D_PLE), loss_target.reshape(nb * seq_len, D), shard,
        ssd_conv_w[0], small, seq_len)

    big_out = [{}, {}, {}, {}]
    for n in _BIG_NAMES:
        res = _sum_adam("sum_adam_" + n, recv[n], local(w, n), local(m, n), local(v, n))
        for k in range(4):
            big_out[k][n] = (res[k].T if n == "w_in" else res[k]).reshape(shapes[n])

    packs = [_pack_small_params(d, where, n_rows, my_block) for d in (w, m, v)]
    small_res = _small_final(small_blocks, late8, where[_LATE], *packs)
    loss = small_res[0][where["loss"], 0]
    small_out = [_unpack_small(a, where, my_block, shapes) for a in small_res]

    outs = [loss, gx.reshape(x.shape)]
    for k in range(4):
        outs += [big_out[k][n] if n in _BIG_NAMES else small_out[k][n] for n in _WEIGHTS]
    return tuple(outs)
```

```python
import functools
import math

import jax
import jax.numpy as jnp
import numpy as np
from jax import lax
from jax.experimental import pallas as pl
from jax.experimental.pallas import tpu as pltpu

_F32 = jnp.float32
_BF16 = jnp.bfloat16
_MXU = jnp.bfloat16

D = 1024
D_PLE = 256
GM_W = 1024
GM_H = 8
CH = 128
SSD_W = 1024
SSD_H = 16
SSD_P = 64
SSD_G = 2
SSD_N = 128
CONV_K = 4
CONV_CH = SSD_W + 2 * SSD_G * SSD_N
D_FF = 4096
D_IN = 2 * GM_W + SSD_W + CONV_CH + SSD_H
D_IN_PAD = 4736
EPS = 1e-6
DT_BLK = (D_IN_PAD - 128) // 128
N_DEV = 8
SHARD_IN = D_IN // N_DEV

LR, B1, B2, ADAM_EPS, WD, STEP = 0.001, 0.9, 0.999, 1e-08, 0.01, 10
_LOG2E = math.log2(math.e)
_LOG2_INV_SQRT_2PI = -0.5 * math.log2(2.0 * math.pi)

_SEQ_PER_STEP = 2
_V7X_VMEM_BYTES = 64 * 1024 * 1024
_VMEM_CAP = _V7X_VMEM_BYTES - 8 * 1024 * 1024
_MESH = pl.DeviceIdType.MESH


def _vmem_limit(nbytes):
    return int(min(_VMEM_CAP, max(32 * 1024 * 1024, nbytes * 5 // 4 + (4 << 20))))


def _nbytes(shape, dtype):
    return int(np.prod(shape)) * jnp.dtype(dtype).itemsize


def _mx(v):
    return v.astype(_MXU)


def _dot(a, b):
    return jnp.dot(a, b, preferred_element_type=_F32)


def _dot_nt(a, b):
    return lax.dot_general(a, b, (((1,), (1,)), ((), ())), preferred_element_type=_F32)


def _dot_tn(a, b):
    return lax.dot_general(a, b, (((0,), (0,)), ((), ())), preferred_element_type=_F32)


def _split3(a):
    hi = a.astype(_BF16)
    r = a - hi.astype(_F32)
    mid = r.astype(_BF16)
    lo = (r - mid.astype(_F32)).astype(_BF16)
    return hi, mid, lo


def _xdot(dotfn, a, b01):
    b = b01.astype(_BF16)
    hi, mid, lo = _split3(a)
    return (dotfn(hi, b) + dotfn(mid, b)) + dotfn(lo, b)


def _xdot_left(dotfn, a01, b):
    a = a01.astype(_BF16)
    hi, mid, lo = _split3(b)
    return (dotfn(a, hi) + dotfn(a, mid)) + dotfn(a, lo)


def _sum8(v):
    r, n = v.shape
    return v.reshape(r // 8, 8, n).sum(axis=0)


def _sigmoid(v):
    return 1.0 / (1.0 + jnp.exp(-v))


def _rms(xv, g):
    ms = jnp.mean(xv * xv, axis=-1, keepdims=True)
    return xv * lax.rsqrt(ms + EPS) * g


def _rms_bwd(xv, g, dn):
    r = lax.rsqrt(jnp.mean(xv * xv, axis=-1, keepdims=True) + EPS)
    nh = xv * r
    gy = dn * g
    dx = (gy - nh * jnp.mean(gy * nh, axis=-1, keepdims=True)) * r
    return dx, _sum8(dn * nh)


def _iota2(shape, axis):
    return lax.broadcasted_iota(jnp.int32, shape, axis)


def _norm_cast(name, x, g, tm=512, comm=()):
    t, n = x.shape
    tm = min(tm, t)
    steps = t // tm
    kinds = [kind for kind, _ in comm]
    c_in, c_in_specs, c_out_specs, c_out_shape, c_scratch = _comm_io(comm)

    def body(*refs):
        x_ref, g_ref = refs[0], refs[1]
        o_ref = refs[2 + len(comm)]
        comm_refs = (kinds, refs[2:2 + len(comm)], refs[3 + len(comm):3 + 2 * len(comm)], *refs[3 + 2 * len(comm):])
        if comm:
            pl.when(pl.program_id(0) == 0)(lambda: _comm_start(*comm_refs))

        o_ref[...] = _rms(x_ref[...], g_ref[...]).astype(o_ref.dtype)
        if comm:
            pl.when(pl.program_id(0) == steps - 1)(lambda: _comm_finish(*comm_refs))

    res = pl.pallas_call(
        body, name=name, grid=(steps,),
        in_specs=[pl.BlockSpec((tm, n), lambda i: (i, 0)), pl.BlockSpec((1, n), lambda i: (0, 0))] + c_in_specs,
        out_specs=[pl.BlockSpec((tm, n), lambda i: (i, 0))] + c_out_specs,
        out_shape=[jax.ShapeDtypeStruct((t, n), _MXU)] + c_out_shape, scratch_shapes=c_scratch,
        compiler_params=pltpu.CompilerParams(dimension_semantics=("arbitrary",)),
    )(x, g, *c_in)
    return res[0], res[1:]


def _matmul(name, a, b, mode, tm, tn, tk, epilogue, outs, extras=(), comm=(), b_to_epilogue=False):
    m, k = a.shape[::-1] if mode == "tn" else a.shape
    n = b.shape[0] if mode == "nt" else b.shape[1]
    tm, tn, tk = min(tm, m), min(tn, n), min(tk, k)
    assert m % tm == 0 and n % tn == 0 and k % tk == 0, (name, m, n, k, tm, tn, tk)
    if mode == "nn":
        a_spec = pl.BlockSpec((tm, tk), lambda i, j, kk: (i, kk))
        b_spec = pl.BlockSpec((tk, tn), lambda i, j, kk: (kk, j))
        dotfn = _dot
    elif mode == "nt":
        a_spec = pl.BlockSpec((tm, tk), lambda i, j, kk: (i, kk))
        b_spec = pl.BlockSpec((tn, tk), lambda i, j, kk: (j, kk))
        dotfn = _dot_nt
    else:
        a_spec = pl.BlockSpec((tk, tm), lambda i, j, kk: (kk, i))
        b_spec = pl.BlockSpec((tk, tn), lambda i, j, kk: (kk, j))
        dotfn = _dot_tn
    ni, nj, nk = m // tm, n // tn, k // tk
    n_ex, n_out, n_comm = len(extras), len(outs), len(comm)
    kinds = [kind for kind, _ in comm]
    c_in, c_in_specs, c_out_specs, c_out_shape, c_scratch = _comm_io(comm)

    in_specs, vmem = [a_spec, b_spec], 2 * (tm * tk * a.dtype.itemsize + tk * tn * b.dtype.itemsize)
    for arr, kind in extras:
        if kind == "tile":
            in_specs.append(pl.BlockSpec((tm, tn), lambda i, j, kk: (i, j)))
            vmem += 2 * _nbytes((tm, tn), arr.dtype)
        elif kind == "rows":
            in_specs.append(pl.BlockSpec((tm, arr.shape[1]), lambda i, j, kk: (i, 0)))
            vmem += 2 * _nbytes((tm, arr.shape[1]), arr.dtype)
        elif kind == "full":
            in_specs.append(pl.BlockSpec(arr.shape, lambda i, j, kk: (0,) * arr.ndim))
            vmem += 2 * _nbytes(arr.shape, arr.dtype)
        else:
            in_specs.append(pl.BlockSpec((1, tn), lambda i, j, kk: (0, j)))
    out_specs, out_shape = [], []
    for kind, dt in outs:
        if kind == "tile":
            out_specs.append(pl.BlockSpec((tm, tn), lambda i, j, kk: (i, j)))
            out_shape.append(jax.ShapeDtypeStruct((m, n), dt))
            vmem += 2 * _nbytes((tm, tn), dt)
        elif kind == "tile_t":
            out_specs.append(pl.BlockSpec((tn, tm), lambda i, j, kk: (j, i)))
            out_shape.append(jax.ShapeDtypeStruct((n, m), dt))
            vmem += 2 * _nbytes((tm, tn), dt)
        else:
            assert nj == 1, "the partial-sum rows are accumulated over consecutive row tiles"
            out_specs.append(pl.BlockSpec((8, tn), lambda i, j, kk: (0, 0)))
            out_shape.append(jax.ShapeDtypeStruct((8, n), dt))
    scratch = [pltpu.VMEM((tm, tn), _F32)] if nk > 1 else []
    vmem += _nbytes((tm, tn), _F32) * 2

    def body(*refs):
        a_ref, b_ref = refs[0], refs[1]
        ex_refs = refs[2:2 + n_ex]
        n_in = 2 + n_ex + n_comm
        out_refs = refs[n_in:n_in + n_out]
        i, j, kk = pl.program_id(0), pl.program_id(1), pl.program_id(2)
        comm_refs = (kinds, refs[2 + n_ex:n_in], refs[n_in + n_out:n_in + n_out + n_comm], *refs[len(refs) - 3:])
        if n_comm:
            pl.when((i == 0) & (j == 0) & (kk == 0))(lambda: _comm_start(*comm_refs))

        b_val = _mx(b_ref[...])
        part = dotfn(_mx(a_ref[...]), b_val)

        def finish(acc):
            vals = epilogue(acc, *([b_val] if b_to_epilogue else []), *[r[...] for r in ex_refs])
            for r, v, (kind, _) in zip(out_refs, vals, outs):
                if kind == "part8":
                    @pl.when(i == 0)
                    def _():
                        r[...] = v

                    @pl.when(i > 0)
                    def _():
                        r[...] += v
                else:
                    r[...] = v.astype(r.dtype)

        if nk == 1:
            finish(part)
        else:
            acc_ref = refs[n_in + n_out + n_comm]

            @pl.when(kk == 0)
            def _():
                acc_ref[...] = part

            @pl.when(kk > 0)
            def _():
                acc_ref[...] += part

            @pl.when(kk == nk - 1)
            def _():
                finish(acc_ref[...])

        if n_comm:
            pl.when((i == ni - 1) & (j == nj - 1) & (kk == nk - 1))(lambda: _comm_finish(*comm_refs))

    carried =n_comm or any(kind == "part8" for kind, _ in outs)
    sem = ("arbitrary",) * 3 if carried else ("parallel", "parallel", "arbitrary")
    res = pl.pallas_call(
        body, name=name, grid=(ni, nj, nk),
        in_specs=in_specs + c_in_specs, out_specs=out_specs + c_out_specs, out_shape=out_shape + c_out_shape,
        scratch_shapes=scratch + c_scratch,
        compiler_params=pltpu.CompilerParams(dimension_semantics=sem, vmem_limit_bytes=_vmem_limit(vmem)),
    )(a, b, *[arr for arr, _ in extras], *c_in)
    return res[:n_out], res[n_out:]


def _shift_down(v, halo8, j):
    if j == 0:
        return v
    r = pltpu.roll(v, j, axis=0)
    hr = pltpu.roll(halo8, j, axis=0)
    top = jnp.where(_iota2(hr.shape, 0) < j, hr, r[:8])
    return jnp.concatenate([top, r[8:]], axis=0)


def _shift_up(v, next8, j):
    if j == 0:
        return v
    rows = v.shape[0]
    r = pltpu.roll(v, rows - j, axis=0)
    nr = pltpu.roll(next8, 8 - j, axis=0)
    bot = jnp.where(_iota2(nr.shape, 0) >= 8 - j, nr, r[rows - 8:])
    return jnp.concatenate([r[:rows - 8], bot], axis=0)


def _silu_grad(sig, silu):
    return sig + silu * (1.0 - sig)


def _gmlp_fwd_vals(pu, pv, gv, ws_ref, bsb_ref, want_bwd):
    tril = _iota2((CH, CH), 0) >= _iota2((CH, CH), 1)
    cdf_u = 0.5 * (1.0 + lax.erf(pu * 0.7071067811865476))
    cdf_v = 0.5 * (1.0 + lax.erf(pv * 0.7071067811865476))
    u = pu * cdf_u
    v = pv * cdf_v
    ys, keep = [], [(cdf_u, cdf_v)] if want_bwd else []
    for h in range(GM_H):
        sl = slice(h * 128, (h + 1) * 128)
        vh = v[:, sl]
        r = lax.rsqrt(jnp.mean(vh * vh, axis=-1, keepdims=True) + EPS)
        vn = vh * r * gv[:, sl]
        wm = _mx(jnp.where(tril, ws_ref[h], 0.0))
        mixed = _dot(wm, _mx(vn)) + bsb_ref[h]
        ys.append(u[:, sl] * mixed)
        if want_bwd:
            keep.append((vh, r, vn, wm, mixed))
    return jnp.concatenate(ys, axis=1), u, keep


def _ssd_conv(xbc, halo8, cw_ref, cb):
    cpre = cb + sum(cw_ref[k:k + 1, :] * _shift_down(xbc, halo8, CONV_K - 1 - k) for k in range(CONV_K))
    sig = _sigmoid(cpre)
    return sig, cpre * sig


def _ssd_decay(dtraw, dtb, alog, e_ref, ltri):
    dtin = dtraw + dtb
    dt = jnp.maximum(dtin, 0.0) + jnp.log(1.0 + jnp.exp(-jnp.abs(dtin)))
    a_neg = -jnp.exp(alog)
    cs = _xdot_left(_dot, ltri, dt * a_neg)
    cs_last = cs[CH - 1:CH, :]
    ecs = jnp.exp(cs)
    dec = jnp.exp(cs_last - cs)
    cdec = jnp.exp(cs_last)
    e = e_ref[...]
    dt_x = _dot(dt.astype(_BF16), e)
    ecs_x = _dot(ecs.astype(_BF16), e)
    dec_x = _dot(dec.astype(_BF16), e)
    cdec_x = _xdot(_dot, jnp.broadcast_to(cdec, (8, 128)), e)[0:1, :]
    return dict(dtin=dtin, dt=dt, a_neg=a_neg, cs=cs, ecs=ecs, dec=dec, cdec=cdec,
                dt_x=dt_x, ecs_x=ecs_x, dec_x=dec_x, cdec_x=cdec_x)


def _head_lm(cs, cst_ref, h, tril):
    seg = jnp.broadcast_to(cs[:, h:h + 1], (CH, CH)) - cst_ref[h:h + 1, :]
    return jnp.exp(jnp.where(tril, seg, -jnp.inf))


def _mixer_fwd(proj, gv, ws, bsb, gout, cw8, cb, dtb, alog, d_x, ng, e_mat, ltri_mat, seq_chunks, comm=()):
    t = proj.shape[0]
    n_seq = t // (seq_chunks * CH)
    sb = math.gcd(_SEQ_PER_STEP, n_seq)
    n_groups = n_seq // sb
    proj = proj.reshape(n_seq, seq_chunks * CH, proj.shape[1])
    n_comm = len(comm)
    kinds = [kind for kind, _ in comm]
    c_in, c_in_specs, c_out_specs, c_out_shape, c_scratch = _comm_io(comm)

    def body(*refs):
        comm_refs = (kinds, refs[18:18 + n_comm], refs[23 + n_comm:23 + 2 * n_comm], *refs[25 + 2 * n_comm:])
        grp, c = pl.program_id(0), pl.program_id(1)
        if n_comm:
            pl.when((grp == 0) & (c == 0))(lambda: _comm_start(*comm_refs))
            pl.when((grp == n_groups - 1) & (c == seq_chunks - 1))(lambda: _comm_finish(*comm_refs))
        for s in range(sb):
            per_seq = lambda rs: [r.at[s] for r in rs]
            one_chunk(c == 0, *per_seq(refs[:6]), *refs[6:18], *per_seq(refs[18 + n_comm:23 + n_comm]),
                      *per_seq(refs[23 + 2 * n_comm:25 + 2 * n_comm]))

    def one_chunk(first, pu_ref, pv_ref, z_ref, xbc_ref, dt_ref, halo_ref, gv_ref, ws_ref, bsb_ref, gout_ref, cw_ref,
                  cb_ref, dtb_ref, alog_ref, dx_ref, ng_ref, e_ref, ltri_ref, cat_ref, y_ref, st_ref, act_ref, sig_ref,
                  s_ref, cst_ref):
        tril = _iota2((CH, CH), 0) >= _iota2((CH, CH), 1)
        lane = _iota2((CH, 128), 1)

        y_a, _, _ = _gmlp_fwd_vals(pu_ref[...], pv_ref[...], gv_ref[...], ws_ref, bsb_ref, False)
        cat_ref[:, 0:GM_W] = _rms(y_a, gout_ref[...]).astype(cat_ref.dtype)

        @pl.when(first)
        def _():
            s_ref[...] = jnp.zeros_like(s_ref)

        halo8 = jnp.where(first, 0.0, halo_ref[...])
        sig, act = _ssd_conv(xbc_ref[...], halo8, cw_ref, cb_ref[...])
        sig_ref[...] = sig
        act_ref[...] = act
        q = _ssd_decay(dt_ref[...], dtb_ref[...], alog_ref[...], e_ref, ltri_ref[...])
        xv = act[:, 0:SSD_W]
        xdt = xv * q["dt_x"]
        xdt_m = _mx(xdt)
        cs = q["cs"]
        cst_ref[...] = cs.T
        s_prev = s_ref[...]
        st_ref[...] = s_prev
        ys = []
        for g in range(SSD_G):
            bg = _mx(act[:, SSD_W + g * SSD_N:SSD_W + (g + 1) * SSD_N])
            cg = _mx(act[:, SSD_W + SSD_G * SSD_N + g * SSD_N:SSD_W + SSD_G * SSD_N + (g + 1) * SSD_N])
            cbm = _dot_nt(cg, bg)
            gs = slice(g * 512, (g + 1) * 512)
            for pr in range(4):
                ps = slice(g * 512 + pr * 128, g * 512 + (pr + 1) * 128)
                o = []
                for hh in range(2):
                    h = g * 8 + pr * 2 + hh
                    m_h = _mx(cbm * _head_lm(cs, cst_ref, h, tril))
                    o.append(_dot(m_h, xdt_m[:, ps]))
                ys.append(jnp.where(lane < SSD_P, o[0], o[1]))
            sg = s_prev[:, gs]
            yoff = _dot(cg, _mx(sg)) * q["ecs_x"][:, gs]
            ys[-4:] = [ys[-4 + i] + yoff[:, i * 128:(i + 1) * 128] for i in range(4)]
            st_new = _dot_tn(bg, _mx(q["dec_x"][:, gs] * xdt[:, gs]))
            s_ref[:, gs] = sg * q["cdec_x"][:, gs] + st_new
        y = jnp.concatenate(ys, axis=1) + dx_ref[...] * xv
        y_ref[...] = y
        zv = z_ref[...]
        yg = y * (zv * _sigmoid(zv))
        for g in range(SSD_G):
            gs = slice(g * 512, (g + 1) * 512)
            cat_ref[:, GM_W + g * 512:GM_W + (g + 1) * 512] = _rms(yg[:, gs], ng_ref[:, gs]).astype(cat_ref.dtype)

    blk = lambda w, j: pl.BlockSpec((sb, CH, w), lambda g, c: (g, c, j))
    full = lambda arr: pl.BlockSpec(arr.shape, lambda g, c: (0,) * arr.ndim)
    consts = [gv, ws, bsb, gout, cw8, cb, dtb, alog, d_x, ng, e_mat, ltri_mat]
    seq = seq_chunks * CH
    res = pl.pallas_call(
        body, name="mixer_fwd", grid=(n_groups, seq_chunks),
        in_specs=[blk(GM_W, 0), blk(GM_W, 1), blk(SSD_W, 2), blk(CONV_CH, 2), blk(128, DT_BLK),
                  pl.BlockSpec((sb, 8, CONV_CH), lambda g, c: (g, jnp.maximum(c * (CH // 8) - 1, 0), 2))]
        + [full(a) for a in consts] + c_in_specs,
        out_specs=[blk(2 * D, 0), blk(SSD_W, 0), blk(SSD_W, 0), blk(CONV_CH, 0), blk(CONV_CH, 0)] + c_out_specs,
        out_shape=[jax.ShapeDtypeStruct((n_seq, seq, 2 * D), _MXU), jax.ShapeDtypeStruct((n_seq, seq, SSD_W), _F32),
                   jax.ShapeDtypeStruct((n_seq, seq, SSD_W), _F32), jax.ShapeDtypeStruct((n_seq, seq, CONV_CH), _F32),
                   jax.ShapeDtypeStruct((n_seq, seq, CONV_CH), _F32)] + c_out_shape,
        scratch_shapes=[pltpu.VMEM((sb, SSD_N, SSD_W), _F32), pltpu.VMEM((sb, 128, CH), _F32)] + c_scratch,
        compiler_params=pltpu.CompilerParams(dimension_semantics=("arbitrary", "arbitrary"),
                                             vmem_limit_bytes=48 << 20),
    )(proj, proj, proj, proj, proj, proj, *consts, *c_in)
    return [r.reshape(t, r.shape[-1]) for r in res[:5]], res[5:]


def _mixer_bwd(proj, act, sig, dcat, yss, states, gv, ws, bsb, gout, cw8, cb, dtb, alog, d_x, ng, e_mat, et_mat,
               ltri_mat, seq_chunks):
    t = proj.shape[0]
    n_seq = t // (seq_chunks * CH)
    sb = math.gcd(_SEQ_PER_STEP, n_seq)
    n_groups = n_seq // sb
    seq = seq_chunks * CH
    proj, act, sig, dcat, yss, states = [a.reshape(n_seq, seq, a.shape[1])
                                         for a in (proj, act, sig, dcat, yss, states)]

    def body(*refs):
        acc_refs, shared = refs[24:34], refs[38:40]
        grp, i = pl.program_id(0), pl.program_id(1)

        @pl.when((grp == 0) & (i == 0))
        def _():
            for r in (*acc_refs, *shared, refs[36]):
                r[...] = jnp.zeros_like(r)

        @pl.when(i == 0)
        def _():
            refs[34][...] = jnp.zeros_like(refs[34])
            refs[35][...] = jnp.zeros_like(refs[35])

        for s in range(sb):
            per_seq = lambda rs: [r.at[s] for r in rs]
            last_of_all = ((grp == n_groups - 1) & (i == seq_chunks - 1)) if s == sb - 1 else None
            one_chunk(last_of_all, *per_seq(refs[:10]), *refs[10:23], refs[23].at[s], *acc_refs,
                      *per_seq(refs[34:38]), *shared)

    def one_chunk(finalize, pu_ref, pv_ref, z_ref, xbc_ref, dt_ref, act_ref, sig_ref, dcat_ref, y_ref, st_ref,
                  gv_ref, ws_ref, bsb_ref, gout_ref, cw_ref, cb_ref, dtb_ref, alog_ref, dx_ref, ng_ref,
                  e_ref, et_ref, ltri_ref,
                  dproj_ref, dws_ref, dbs_ref, dgv_ref, dgout_ref, dng_ref, dcw_ref, dcb_ref, ddtb_ref, dalog_ref,
                  dd_ref, ds_ref, dnext_ref, dcst_ref, cst_ref, dbacc_ref, ddacc_ref):
        tril = _iota2((CH, CH), 0) >= _iota2((CH, CH), 1)
        lane = _iota2((CH, 128), 1)
        row = _iota2((CH, 128), 0)
        dcat_v = dcat_ref[...].astype(_F32)

        pu, pv = pu_ref[...], pv_ref[...]
        gv_v = gv_ref[...]
        y_a, u, keep = _gmlp_fwd_vals(pu, pv, gv_v, ws_ref, bsb_ref, True)
        dy, dgout8 = _rms_bwd(y_a, gout_ref[...], dcat_v[:, 0:GM_W])
        dgout_ref[...] += dgout8
        dus, dvs, dgvs = [], [], []
        for h in range(GM_H):
            sl = slice(h * 128, (h + 1) * 128)
            vh, r, vn, wm, mixed = keep[h + 1]
            dyh = dy[:, sl]
            dus.append(dyh * mixed)
            dmix = dyh * u[:, sl]
            dmix_m = _mx(dmix)
            dws_ref[h] += jnp.where(tril, _dot_nt(dmix_m, _mx(vn)), 0.0)
            dbacc_ref[h] += dmix
            dvn = _dot_tn(wm, dmix_m)
            gy = dvn * gv_v[:, sl]
            nh = vh * r
            dvs.append((gy - nh * jnp.mean(gy * nh, axis=-1, keepdims=True)) * r)
            dgvs.append(_sum8(dvn * nh))
        dgv_ref[...] += jnp.concatenate(dgvs, axis=1)
        cdf_u, cdf_v = keep[0]
        gelu_grad = lambda pre, cdf: cdf + pre * jnp.exp2(pre * pre * (-0.5 * _LOG2E) + _LOG2_INV_SQRT_2PI)
        dproj_ref[:, 0:GM_W] = (jnp.concatenate(dus, axis=1) * gelu_grad(pu, cdf_u)).astype(dproj_ref.dtype)
        dproj_ref[:, GM_W:2 * GM_W] = (jnp.concatenate(dvs, axis=1) * gelu_grad(pv, cdf_v)).astype(dproj_ref.dtype)

        q = _ssd_decay(dt_ref[...], dtb_ref[...], alog_ref[...], e_ref, ltri_ref[...])
        act = act_ref[...]
        xv = act[:, 0:SSD_W]
        dt_x, ecs_x, dec_x, cdec_x = q["dt_x"], q["ecs_x"], q["dec_x"], q["cdec_x"]
        xdt = xv * dt_x
        xdt_m = _mx(xdt)
        cs = q["cs"]
        cst_ref[...] = cs.T
        s_prev = st_ref[...]
        ds = ds_ref[...]
        yv = y_ref[...]
        zv = z_ref[...]
        sig_z = _sigmoid(zv)
        sz = zv * sig_z
        yg = yv * sz
        dygs, dng8 = [], []
        for g in range(SSD_G):
            gs = slice(g * 512, (g + 1) * 512)
            a_, b_ = _rms_bwd(yg[:, gs], ng_ref[:, gs], dcat_v[:, GM_W + g * 512:GM_W + (g + 1) * 512])
            dygs.append(a_)
            dng8.append(b_)
        dyg = jnp.concatenate(dygs, axis=1)
        dng_ref[...] += jnp.concatenate(dng8, axis=1)
        dyv = dyg * sz
        dproj_ref[:, 2 * GM_W:2 * GM_W + SSD_W] = (dyg * yv * _silu_grad(sig_z, sz)).astype(dproj_ref.dtype)
        ddacc_ref[...] += _sum8(dyv * xv)
        dyv_m = _mx(dyv)

        dxdt_parts, db_parts, dc_parts = [], [], []
        dcs = jnp.zeros((CH, 128), _F32)
        dcs_x_parts, ddec_x_parts, dcl_x_parts = [], [], []
        for g in range(SSD_G):
            gs = slice(g * 512, (g + 1) * 512)
            bg = _mx(act[:, SSD_W + g * SSD_N:SSD_W + (g + 1) * SSD_N])
            cg = _mx(act[:, SSD_W + SSD_G * SSD_N + g * SSD_N:SSD_W + SSD_G * SSD_N + (g + 1) * SSD_N])
            cbm = _dot_nt(cg, bg)
            sg = s_prev[:, gs]
            sg_m = _mx(sg)
            dsg = ds[:, gs]
            dsg_m = _mx(dsg)
            zoff = _dot(cg, sg_m)
            dz_off = dyv[:, gs] * ecs_x[:, gs]
            dz_off_m = _mx(dz_off)
            dcs_x_parts.append(dyv[:, gs] * zoff * ecs_x[:, gs])
            dcg = _dot_nt(dz_off_m, sg_m)
            dsprev = _dot_tn(cg, dz_off_m)
            w_st = dec_x[:, gs] * xdt[:, gs]
            dw_st = _dot(bg, dsg_m)
            dbg = _dot_nt(_mx(w_st), dsg_m)
            dxdt_g = dec_x[:, gs] * dw_st
            ddec_x_parts.append(dw_st * xdt[:, gs])
            dsprev = dsprev + cdec_x[:, gs] * dsg
            dcl_x_parts.append(jnp.sum(dsg * sg, axis=0, keepdims=True) * cdec_x[:, gs])
            ds_ref[:, gs] = dsprev
            dcb = jnp.zeros((CH, CH), _F32)
            dxdt_pairs = []
            for pr in range(4):
                ps = slice(g * 512 + pr * 128, g * 512 + (pr + 1) * 128)
                acc_pair = None
                for hh in range(2):
                    h = g * 8 + pr * 2 + hh
                    in_head = (lane < SSD_P) if hh == 0 else (lane >= SSD_P)
                    lm = _head_lm(cs, cst_ref, h, tril)
                    m_h = cbm * lm
                    m_hm = _mx(m_h)
                    dyh_m = _mx(jnp.where(in_head, dyv[:, ps], 0.0))
                    dm = _dot_nt(dyh_m, xdt_m[:, ps])
                    dcb = dcb + dm * lm
                    qm = dm * m_h
                    dcs = dcs + jnp.where(lane == h, jnp.sum(qm, axis=1, keepdims=True), 0.0)
                    dcst_ref[h:h + 1, :] = jnp.sum(qm, axis=0, keepdims=True)
                    contrib = jnp.where(in_head, _dot_tn(m_hm, dyv_m[:, ps]), 0.0)
                    acc_pair = contrib if acc_pair is None else acc_pair + contrib
                dxdt_pairs.append(acc_pair)
            dxdt_parts.append(dxdt_g + jnp.concatenate(dxdt_pairs, axis=1))
            dcb_m = _mx(dcb)
            dc_parts.append(dcg + _dot(dcb_m, bg))
            db_parts.append(dbg + _dot_tn(dcb_m, cg))
        dxdt = jnp.concatenate(dxdt_parts, axis=1)
        dxv = dx_ref[...] * dyv + dxdt * dt_x
        et = et_ref[...]
        head_sum = lambda v: _dot(v.astype(_BF16), et)
        ddt = head_sum(dxdt * xv)
        dcs = dcs - dcst_ref[...].T + head_sum(jnp.concatenate(dcs_x_parts, axis=1))
        ddec = head_sum(jnp.concatenate(ddec_x_parts, axis=1)) * q["dec"]
        dcs = dcs - ddec
        dcl = jnp.sum(ddec, axis=0, keepdims=True) + _xdot(
            _dot, jnp.broadcast_to(jnp.concatenate(dcl_x_parts, axis=1), (8, SSD_W)), et)[0:1, :]
        dcs = jnp.where(row == CH - 1, dcs + dcl, dcs)
        da = _xdot_left(_dot_tn, ltri_ref[...], dcs)
        ddt = ddt + da * q["a_neg"]
        dalog_ref[...] += _sum8(da * q["dt"] * q["a_neg"])
        ddtraw = jnp.where(lane < SSD_H, ddt * _sigmoid(q["dtin"]), 0.0)
        ddtb_ref[...] += _sum8(ddtraw)
        dproj_ref[:, D_IN_PAD - 128:D_IN_PAD] = ddtraw.astype(dproj_ref.dtype)
        dcpre = jnp.concatenate([dxv] + db_parts + dc_parts, axis=1) * _silu_grad(sig_ref[...], act)
        dcb_ref[...] += _sum8(dcpre)
        next8 = dnext_ref[...]
        ups = [_shift_up(dcpre, next8, j) for j in range(CONV_K)]
        xbc = xbc_ref[...]
        for k in range(CONV_K):
            dcw_ref[k:k + 1, :] += jnp.sum(xbc * ups[CONV_K - 1 - k], axis=0, keepdims=True)
        dxbc = sum(cw_ref[k:k + 1, :] * ups[CONV_K - 1 - k] for k in range(CONV_K))
        dproj_ref[:, 2 * GM_W + SSD_W:2 * GM_W + SSD_W + CONV_CH] = dxbc.astype(dproj_ref.dtype)
        dnext_ref[...] = dcpre[0:8, :]

        if finalize is not None:
            @pl.when(finalize)
            def _():
                for h in range(GM_H):
                    dbs_ref[h:h + 1, :] = _xdot_left(_dot_nt, jnp.ones((8, 128), _BF16), dbacc_ref[h])[0:1, :]
                dd_ref[...] = _xdot(_dot, ddacc_ref[...], et)

    rblk = lambda w, j: pl.BlockSpec((sb, CH, w), lambda g, i: (g, seq_chunks - 1 - i, j))
    full = lambda arr: pl.BlockSpec(arr.shape, lambda g, i: (0,) * arr.ndim)
    acc = lambda shape: pl.BlockSpec(shape, lambda g, i: (0,) * len(shape))
    consts = [gv, ws, bsb, gout, cw8, cb, dtb, alog, d_x, ng, e_mat, et_mat, ltri_mat]
    acc_shapes = [(GM_H, CH, CH), (8, 128), (8, GM_W), (8, GM_W), (8, SSD_W), (8, CONV_CH), (8, CONV_CH), (8, 128),
                  (8, 128), (8, 128)]
    res = pl.pallas_call(
        body, name="mixer_bwd", grid=(n_groups, seq_chunks),
        in_specs=[rblk(GM_W, 0), rblk(GM_W, 1), rblk(SSD_W, 2), rblk(CONV_CH, 2), rblk(128, DT_BLK),
                  rblk(CONV_CH, 0), rblk(CONV_CH, 0),
                  rblk(2 * D, 0), rblk(SSD_W, 0), rblk(SSD_W, 0)] + [full(a) for a in consts],
        out_specs=[rblk(D_IN_PAD, 0)] + [acc(s) for s in acc_shapes],
        out_shape=[jax.ShapeDtypeStruct((n_seq, seq, D_IN_PAD), _MXU)]
        + [jax.ShapeDtypeStruct(s, _F32) for s in acc_shapes],
        scratch_shapes=[pltpu.VMEM((sb, SSD_N, SSD_W), _F32), pltpu.VMEM((sb, 8, CONV_CH), _F32),
                        pltpu.VMEM((sb, 128, CH), _F32), pltpu.VMEM((sb, 128, CH), _F32),
                        pltpu.VMEM((GM_H, CH, 128), _F32), pltpu.VMEM((8, SSD_W), _F32)],
        compiler_params=pltpu.CompilerParams(dimension_semantics=("arbitrary", "arbitrary"),
                                             vmem_limit_bytes=48 << 20),
    )(proj, proj, proj, proj, proj, act, sig, dcat, yss, states, *consts)
    return [res[0].reshape(t, D_IN_PAD)] + list(res[1:])


def _peers():
    x, y, c = lax.axis_index("x"), lax.axis_index("y"), lax.axis_index("c")
    out = []
    for k in range(1, N_DEV):
        fx, fy, fc = (k >> 2) & 1, (k >> 1) & 1, k & 1
        px, py, pc = (x + fx) % 2, (y + fy) % 2, (c + fc) % 2
        out.append((k - 1, (px, py, pc), 4 * px + 2 * py + pc))
    return out, 4 * x + 2 * y + c


def _comm_io(comm):
    any_spec = pl.BlockSpec(memory_space=pl.ANY)
    n = len(comm)
    out_shape = []
    for (kind, axis), src in comm:
        shp = list(src.shape)
        if kind in ("gather", "gather2"):
            shp[axis] *= N_DEV
        else:
            shp[axis] //= N_DEV
            shp = [N_DEV] + shp
        out_shape.append(jax.ShapeDtypeStruct(tuple(shp), src.dtype))
    scratch = [pltpu.SemaphoreType.DMA((n * (N_DEV - 1),)), pltpu.SemaphoreType.DMA((n * (N_DEV - 1),)),
               pltpu.SemaphoreType.DMA((n,))] if n else []
    return [src for _, src in comm], [any_spec] * n, [any_spec] * n, out_shape, scratch


def _window(ref, axis, idx, size):
    start = pl.multiple_of(idx * size, size)
    return ref.at[tuple(pl.ds(start, size) if a == axis else slice(None) for a in range(len(ref.shape)))]


def _comm_plans(kinds, src_refs, dst_refs, send_sems, recv_sems, local_sems):
    x, y, c = lax.axis_index("x"), lax.axis_index("y"), lax.axis_index("c")
    peers, me = _peers()
    plans = []
    for s, ((kind, axis), src, dst) in enumerate(zip(kinds, src_refs, dst_refs)):
        sems = lambda k: dict(send_sem=send_sems.at[s * (N_DEV - 1) + k], recv_sem=recv_sems.at[s * (N_DEV - 1) + k])
        remote = lambda src_ref, dst_ref, k, pid: pltpu.make_async_remote_copy(
            src_ref=src_ref, dst_ref=dst_ref, device_id=pid, device_id_type=_MESH, **sems(k))
        if kind == "gather2":
            size = src.shape[axis]
            win = lambda idx: _window(dst, axis, idx, size)
            sib, sib_idx = (x, y, 1 - c), 4 * x + 2 * y + (1 - c)
            local = pltpu.make_async_copy(src, win(me), local_sems.at[s])
            to_sib = remote(src, win(me), 0, sib)
            starts, forwards = [local, to_sib], []
            waits = [(local, "local"), (to_sib, "send"), (remote(src, win(sib_idx), 0, sib), "recv")]
            for j, (fx, fy) in enumerate(((1, 0), (0, 1), (1, 1))):
                px, py = (x + fx) % 2, (y + fy) % 2
                same, other = 4 * px + 2 * py + c, 4 * px + 2 * py + (1 - c)
                out = remote(src, win(me), 1 + j, (px, py, c))
                starts.append(out)
                passed = remote(win(same), win(same), 4 + j, sib)
                forwards.append((remote(src, win(same), 1 + j, (px, py, c)), passed))
                waits += [(out, "send"), (passed, "send"), (remote(win(other), win(other), 4 + j, sib), "recv")]
            plans.append((starts, forwards, waits))
            continue
        if kind == "gather":
            size = src.shape[axis]
            src_for = lambda pidx: src
            dst_mine = _window(dst, axis, me, size)
        else:
            size = src.shape[axis] // N_DEV
            src_for = lambda pidx: _window(src, axis, pidx, size)
            dst_mine = dst.at[me]
        local = pltpu.make_async_copy(src_for(me), dst_mine, local_sems.at[s])
        remotes = [remote(src_for(pidx), dst_mine, k, pid) for k, pid, pidx in peers]
        plans.append(([local] + remotes, [], [(local, "local")] + [(cp, "both") for cp in remotes]))
    return plans


def _comm_start(*refs):
    for starts, _, _ in _comm_plans(*refs):
        for cp in starts:
            cp.start()


def _comm_finish(*refs):
    for _, forwards, waits in _comm_plans(*refs):
        for arrival, cp in forwards:
            arrival.wait_recv()
            cp.start()
        for cp, what in waits:
            if what == "send":
                cp.wait_send()
            elif what == "recv":
                cp.wait_recv()
            else:
                cp.wait()


def _adam_vals(w, g, m, v):
    m = B1 * m + (1.0 - B1) * g
    v = B2 * v + (1.0 - B2) * (g * g)
    m_hat = m / (1.0 - B1 ** STEP)
    v_hat = v / (1.0 - B2 ** STEP)
    delta = -LR * (m_hat / (jnp.sqrt(v_hat) + ADAM_EPS) + WD * w)
    return delta, m, v


def _sum_adam(name, recv, w, m, v, tile=256):
    _, r, wd = recv.shape
    if r % min(tile, r) == 0:
        tr, tc = min(tile, r), wd
    else:
        tr, tc = r, tile
        assert wd % tc == 0, (name, r, wd)

    def body(recv_ref, w_ref, m_ref, v_ref, g_out, d_out, m_out, v_out):
        g = recv_ref[0].astype(_F32)
        for s in range(1, N_DEV):
            g = g + recv_ref[s].astype(_F32)
        d_, m_, v_ = _adam_vals(w_ref[...], g, m_ref[...], v_ref[...])
        g_out[...] = g
        d_out[...] = d_
        m_out[...] = m_
        v_out[...] = v_

    spec = pl.BlockSpec((tr, tc), lambda i, j: (i, j))
    return pl.pallas_call(
        body, name=name, grid=(r // tr, wd // tc),
        in_specs=[pl.BlockSpec((N_DEV, tr, tc), lambda i, j: (0, i, j)), spec, spec, spec],
        out_specs=[spec] * 4, out_shape=[jax.ShapeDtypeStruct((r, wd), _F32)] * 4,
        compiler_params=pltpu.CompilerParams(dimension_semantics=("parallel", "parallel"),
                                             vmem_limit_bytes=48 << 20),
    )(recv, w, m, v)


def _small_local(parts, segments, n_rows):
    def body(parts_ref, loc_ref):
        loc_ref[...] = jnp.zeros_like(loc_ref)
        for out_row, n_out, in_row, n_in, kind in segments:
            if kind == "copy":
                loc_ref[out_row:out_row + n_out, :] = parts_ref[in_row:in_row + n_in, :]
            else:
                s = jnp.sum(parts_ref[in_row:in_row + n_in, :], axis=0, keepdims=True)
                if kind == "loss":
                    s = jnp.broadcast_to(jnp.sum(s, axis=1, keepdims=True) * (0.5 / D), (1, D))
                loc_ref[out_row:out_row + 1, :] = s

    vm = pl.BlockSpec(memory_space=pltpu.VMEM)
    return pl.pallas_call(body, name="small_local", in_specs=[vm], out_specs=vm,
                          out_shape=jax.ShapeDtypeStruct((n_rows, D), _F32))(parts)


def _small_final(blocks, late8, late_row, w, m, v):
    n_rows = w.shape[0]

    def body(blocks_ref, late_ref, w_ref, m_ref, v_ref, g_out, d_out, m_out, v_out, loc_ref, recv_ref, send_sems,
             recv_sems):
        peers, me = _peers()
        loc_ref[...] = jnp.broadcast_to(jnp.sum(late_ref[...], axis=0, keepdims=True), (8, D))
        recv_ref[me] = loc_ref[...]
        copies = [pltpu.make_async_remote_copy(src_ref=loc_ref, dst_ref=recv_ref.at[me], send_sem=send_sems.at[k],
                                               recv_sem=recv_sems.at[k], device_id=pid, device_id_type=_MESH)
                  for k, pid, _ in peers]
        for cp in copies:
            cp.start()
        g = blocks_ref[0:n_rows, :]
        for s in range(1, N_DEV):
            g = g + blocks_ref[s * n_rows:(s + 1) * n_rows, :]
        for cp in copies:
            cp.wait()
        late = recv_ref[0]
        for s in range(1, N_DEV):
            late = late + recv_ref[s]
        g = jnp.where(_iota2((n_rows, D), 0) == late_row, jnp.broadcast_to(late[0:1, :], (n_rows, D)), g)
        d_, m_, v_ = _adam_vals(w_ref[...], g, m_ref[...], v_ref[...])
        g_out[...] = g
        d_out[...] = d_
        m_out[...] = m_
        v_out[...] = v_

    vm = pl.BlockSpec(memory_space=pltpu.VMEM)
    return pl.pallas_call(
        body, name="small_final", in_specs=[vm] * 5, out_specs=[vm] * 4,
        out_shape=[jax.ShapeDtypeStruct((n_rows, D), _F32)] * 4,
        scratch_shapes=[pltpu.VMEM((8, D), _F32), pltpu.VMEM((N_DEV, 8, D), _F32),
                        pltpu.SemaphoreType.DMA((N_DEV - 1,)), pltpu.SemaphoreType.DMA((N_DEV - 1,))],
        compiler_params=pltpu.CompilerParams(vmem_limit_bytes=48 << 20),
    )(blocks, late8, w, m, v)


_BIG_NAMES =("w_in", "w_out", "w_ff1", "w_ff2", "w_ple_gate", "w_ple_proj")

_G_VECS = ("norm_mix_g", "gm_v_norm_g", "gm_out_norm_g", "ssd_norm_g", "norm_mlp_g", "ple_norm_g", "final_norm_g")
_LATE = _G_VECS[0]


def _const_mats():
    h = np.arange(128)[:, None]
    ch = np.arange(SSD_W)[None, :]
    e = (ch // SSD_P == h).astype(np.float32)
    ltri = (np.arange(CH)[:, None] >= np.arange(CH)[None, :]).astype(np.float32)
    return jnp.asarray(e, _BF16), jnp.asarray(e.T, _BF16), jnp.asarray(ltri, _BF16)


def _pad_lanes(v, n=128):
    return jnp.pad(v, ((0, 0), (0, n - v.shape[1])))


def _local_step(x, p, tgt, shard, conv_w_shard, small, seq_len):
    seq_chunks = seq_len // CH
    e_mat, et_mat, ltri_mat = _const_mats()
    g_mix, g_mlp, g_ple = small["norm_mix_g"], small["norm_mlp_g"], small["ple_norm_g"]
    g_fin = small["final_norm_g"].reshape(1, D)
    gv, gout, ng = small["gm_v_norm_g"], small["gm_out_norm_g"], small["ssd_norm_g"]
    ws = small["gm_ws"][0]
    bsb = jnp.broadcast_to(small["gm_bs"][0][:, :, None], (GM_H, CH, 128))
    cb = small["ssd_conv_b"]
    dtb, alog = _pad_lanes(small["ssd_dt_bias"]), _pad_lanes(small["ssd_a_log"])
    d_x = jnp.repeat(small["ssd_d"], SSD_P, axis=1)

    first = lambda acc: (acc,)
    rows, cols = ("gather2", 0), ("gather2", 1)
    n1, (g_win, g_cw) = _norm_cast("norm_mix", x, g_mix,
                                   comm=[(rows, shard["w_in"][None]), (("gather", 0), conv_w_shard[None])])
    w_in_t = jnp.pad(g_win.reshape(D_IN, D), ((0, D_IN_PAD - D_IN), (0, 0)))
    cw8 = jnp.pad(g_cw.transpose(1, 0, 2).reshape(CONV_K, CONV_CH), ((0, 8 - CONV_K), (0, 0)))
    mix_consts = (gv, ws, bsb, gout, cw8, cb, dtb, alog, d_x, ng)
    (proj,), (w_out, w1) = _matmul("proj_in", n1, w_in_t, "nt", 512, D_IN_PAD, D, first, [("tile", _F32)],
                                   comm=[(rows, shard["w_out"]), (cols, shard["w_ff1"])])
    (cat, yss, states, conv_act, conv_sig), (w2, wg, wp) = _mixer_fwd(
        proj, *mix_consts, e_mat, ltri_mat, seq_chunks,
        comm=[(rows, shard["w_ff2"]), (rows, shard["w_ple_gate"]), (cols, shard["w_ple_proj"])])

    def epi_res_norm(acc, res, g):
        hv = acc + res
        return hv, _rms(hv, g)

    (h1, n2), _ = _matmul("proj_out", cat, w_out, "nn", 1024, D, 2 * D, epi_res_norm,
                          [("tile", _F32), ("tile", _MXU)], extras=[(x, "tile"), (g_mlp, "row")])

    def epi_relu2(acc):
        hid = jnp.maximum(acc, 0.0)
        return hid, hid * hid

    (hid, hid2), _ = _matmul("ff1", n2, w1, "nn", 1024, 1024, D, epi_relu2, [("tile", _MXU), ("tile", _MXU)])
    (h2, n3), _ = _matmul("ff2", hid2, w2, "nn", 512, D, D_FF, epi_res_norm, [("tile", _F32), ("tile", _MXU)],
                          extras=[(h1, "tile"), (g_ple, "row")])

    def epi_norm_bwd(acc, up, hv, g):
        dx, dg8 = _rms_bwd(hv, g, acc)
        dh = up + dx
        return dh, dh, dg8

    def epi_head(acc, wg_v, p_rows, wp_v, h2v, tg, gf, gp_):
        ppv = _dot(_mx(p_rows), wp_v)
        gate = _sigmoid(acc)
        gp = gate * ppv
        h3 = h2v + gp
        r = lax.rsqrt(jnp.mean(h3 * h3, axis=-1, keepdims=True) + EPS)
        nh = h3 * r
        err = nh * gf - tg
        gy = err * (gf * (1.0 / D))
        dh3 = (gy - nh * jnp.mean(gy * nh, axis=-1, keepdims=True)) * r
        dpp = dh3 * gate
        da3 = dpp * (ppv - gp)
        dh2, dh2_again, dgple8 = epi_norm_bwd(_dot_nt(_mx(da3), wg_v), dh3, h2v, gp_)
        return da3, dpp, dh2, dh2_again, _sum8(err * err), _sum8(err * nh) * (1.0 / D), dgple8

    (da3, dpp, dh2, dh2b, lossp, dgfin, dgple), _ = _matmul(
        "ple_gate_loss_bwd", n3, wg, "nn", 512, D, D, epi_head,
        [("tile", _MXU), ("tile", _MXU), ("tile", _F32), ("tile", _MXU), ("part8", _F32), ("part8", _F32),
         ("part8", _F32)],
        extras=[(p, "rows"), (wp, "full"), (h2, "tile"), (tgt, "tile"), (g_fin, "row"), (g_ple, "row")],
        b_to_epilogue=True)

    s_rows, s_cols = ("scatter", 0), ("scatter", 1)
    (dwp,), _ = _matmul("d_w_ple_proj", p, dpp, "tn", D_PLE, D, 2048, first, [("tile", _BF16)])
    (dwg,), _ = _matmul("d_w_ple_gate", n3, da3, "tn", D, D, 2048, first, [("tile", _BF16)])
    (dw2,), (r_wp, r_wg) = _matmul("d_w_ff2", hid2, dh2b, "tn", 1024, D, 4096, first, [("tile", _BF16)],
                                   comm=[(s_cols, dwp), (s_rows, dwg)])
    (da1,), (r_w2,) = _matmul("d_ff_hidden", dh2b, w2, "nt", 512, 2048, D,
                              lambda acc, hv: (acc * 2.0 * hv.astype(_F32),), [("tile", _MXU)],
                              extras=[(hid, "tile")], comm=[(s_rows, dw2)])
    (dw1,), _ = _matmul("d_w_ff1", n2, da1, "tn", 1024, 1024, 4096, first, [("tile", _BF16)])
    (dh1, dh1b, dgmlp), (r_w1,) = _matmul(
        "d_h1", da1, w1, "nt", 256, D, D_FF, epi_norm_bwd, [("tile", _F32), ("tile", _MXU), ("part8", _F32)],
        extras=[(dh2, "tile"), (h1, "tile"), (g_mlp, "row")], comm=[(s_cols, dw1)])
    (dwout,), _ = _matmul("d_w_out", cat, dh1b, "tn", 1024, D, 2048, first, [("tile", _BF16)])
    (dcat,), (r_wout,) = _matmul("d_cat", dh1b, w_out, "nt", 1024, 1024, D, first, [("tile", _MXU)],
                                 comm=[(s_rows, dwout)])
    (dproj, dws, dbs, dgv, dgout, dng, dcw, dcb, ddtb, dalog, dd) = _mixer_bwd(
        proj, conv_act, conv_sig, dcat, yss, states, *mix_consts, e_mat, et_mat, ltri_mat, seq_chunks)
    pieces = dict(gm_v_norm_g=dgv, gm_out_norm_g=dgout, ssd_norm_g=dng, norm_mlp_g=dgmlp, ple_norm_g=dgple,
                  final_norm_g=dgfin, gm_ws=dws, gm_bs=dbs, ssd_conv_w=dcw, ssd_conv_b=dcb, ssd_dt_bias=ddtb,
                  ssd_a_log=dalog, ssd_d=dd, loss=lossp)
    parts, segments, n_rows, where = _small_layout(pieces)
    small_block = _small_local(parts, segments, n_rows)
    (dwin_t,), (small_blocks,) = _matmul("d_w_in", n1, dproj, "tn", 512, D_IN_PAD, 1024, lambda acc: (acc.T,),
                                         [("tile_t", _BF16)], comm=[(("gather", 0), small_block)])
    dwin_blocks = dwin_t[:D_IN].reshape(N_DEV, SHARD_IN, D)
    (gx, dgmix), (r_win,) = _matmul(
        "d_x", dproj, w_in_t, "nn", 256, D, D_IN_PAD, lambda *a: epi_norm_bwd(*a)[1:],
        [("tile", _F32), ("part8", _F32)], extras=[(dh1, "tile"), (x, "tile"), (g_mix, "row")],
        comm=[(s_rows, dwin_blocks)])
    r_win = r_win.reshape(N_DEV, SHARD_IN, D)

    big = dict(w_in=r_win, w_out=r_wout, w_ff1=r_w1, w_ff2=r_w2, w_ple_gate=r_wg, w_ple_proj=r_wp)
    return gx, big, small_blocks, dgmix, n_rows, where


def _small_layout(pieces):
    rows, segments = [], []
    in_row, out_row = 0, 0

    def add(arr, kind, n_out):
        nonlocal in_row, out_row
        rows.append(arr)
        segments.append((out_row, n_out, in_row, arr.shape[0], kind))
        start = out_row
        in_row += arr.shape[0]
        out_row += n_out
        return start

    where = {_LATE: 0}
    out_row = 1
    for name in _G_VECS[1:]:
        where[name] = add(pieces[name], "sum", 1)
    where["gm_ws"] = add(pieces["gm_ws"].reshape(GM_H * CH * CH // D, D), "copy", GM_H * CH * CH // D)
    where["gm_bs"] = add(pieces["gm_bs"].reshape(1, D), "copy", 1)
    cb = jnp.pad(pieces["ssd_conv_b"], ((0, 0), (0, 2 * D - CONV_CH)))
    where["ssd_conv_b"] = add(cb[:, :D], "sum", 1)
    add(cb[:, D:], "sum", 1)
    cw = jnp.pad(pieces["ssd_conv_w"][:CONV_K], ((0, 0), (0, 2 * D - CONV_CH)))
    where["ssd_conv_w"] = add(cw.reshape(2 * CONV_K, D), "copy", 2 * CONV_K)
    misc = jnp.concatenate([pieces["ssd_dt_bias"], pieces["ssd_a_log"], pieces["ssd_d"],
                            jnp.zeros((8, D - 3 * 128), _F32)], axis=1)
    where["misc"] = add(misc, "sum", 1)
    where["loss"] = add(pieces["loss"], "loss", 1)
    n_rows = -(-out_row // 8) * 8
    return jnp.concatenate(rows, axis=0), tuple(segments), n_rows, where


def _pack_small_params(vals, where, n_rows, my_block):
    rows, at = [], {}

    def add(name, arr):
        at[name] = sum(r.shape[0] for r in rows)
        rows.append(arr)

    for name in _G_VECS:
        add(name, vals[name].reshape(1, D))
    add("gm_ws", vals["gm_ws"].reshape(GM_H * CH * CH // D, D))
    add("gm_bs", vals["gm_bs"].reshape(1, D))
    cb = jnp.pad(vals["ssd_conv_b"].reshape(1, CONV_CH), ((0, 0), (0, 2 * D - CONV_CH)))
    add("ssd_conv_b", cb.reshape(2, D))
    cw = lax.dynamic_update_slice(jnp.zeros((CONV_K, 2 * D), _F32), vals["ssd_conv_w"].reshape(CONV_K, -1),
                                  (0, my_block * (CONV_CH // N_DEV)))
    add("ssd_conv_w", cw.reshape(2 * CONV_K, D))
    misc = jnp.concatenate([_pad_lanes(vals["ssd_dt_bias"].reshape(1, SSD_H)),
                            _pad_lanes(vals["ssd_a_log"].reshape(1, SSD_H)),
                            _pad_lanes(vals["ssd_d"].reshape(1, SSD_H)), jnp.zeros((1, D - 3 * 128), _F32)], axis=1)
    add("misc", misc)
    assert all(where[k] == r for k, r in at.items()), (where, at)
    rows.append(jnp.zeros((n_rows - sum(r.shape[0] for r in rows), D), _F32))
    return jnp.concatenate(rows, axis=0)


def _unpack_small(buf, where, my_block, shapes):
    out = {}
    for name in _G_VECS:
        out[name] = buf[where[name]].reshape(shapes[name])
    n_ws = GM_H * CH * CH // D
    out["gm_ws"] = buf[where["gm_ws"]:where["gm_ws"] + n_ws].reshape(shapes["gm_ws"])
    out["gm_bs"] = buf[where["gm_bs"]].reshape(shapes["gm_bs"])
    r = where["ssd_conv_b"]
    out["ssd_conv_b"] = buf[r:r + 2].reshape(1, 2 * D)[:, :CONV_CH].reshape(shapes["ssd_conv_b"])
    r = where["ssd_conv_w"]
    cw = buf[r:r + 2 * CONV_K].reshape(CONV_K, 2 * D)
    out["ssd_conv_w"] = lax.dynamic_slice(cw, (0, my_block * (CONV_CH // N_DEV)),
                                          (CONV_K, CONV_CH // N_DEV)).reshape(shapes["ssd_conv_w"])
    misc = buf[where["misc"]]
    for i, name in enumerate(("ssd_dt_bias", "ssd_a_log", "ssd_d")):
        out[name] = misc[i * 128:i * 128 + SSD_H].reshape(shapes[name])
    return out


_WEIGHTS = ("norm_mix_g", "w_in", "gm_v_norm_g", "gm_ws", "gm_bs", "gm_out_norm_g", "ssd_conv_w", "ssd_conv_b",
            "ssd_dt_bias", "ssd_a_log", "ssd_d", "ssd_norm_g", "w_out", "norm_mlp_g", "w_ff1", "w_ff2", "ple_norm_g",
            "w_ple_gate", "w_ple_proj", "final_norm_g")


def kernel(x, p, norm_mix_g, w_in, gm_v_norm_g, gm_ws, gm_bs, gm_out_norm_g, ssd_conv_w, ssd_conv_b, ssd_dt_bias, ssd_a_log, ssd_d, ssd_norm_g, w_out, norm_mlp_g, w_ff1, w_ff2, ple_norm_g, w_ple_gate, w_ple_proj, final_norm_g, loss_target, m_norm_mix_g, m_w_in, m_gm_v_norm_g, m_gm_ws, m_gm_bs, m_gm_out_norm_g, m_ssd_conv_w, m_ssd_conv_b, m_ssd_dt_bias, m_ssd_a_log, m_ssd_d, m_ssd_norm_g, m_w_out, m_norm_mlp_g, m_w_ff1, m_w_ff2, m_ple_norm_g, m_w_ple_gate, m_w_ple_proj, m_final_norm_g, v_norm_mix_g, v_w_in, v_gm_v_norm_g, v_gm_ws, v_gm_bs, v_gm_out_norm_g, v_ssd_conv_w, v_ssd_conv_b, v_ssd_dt_bias, v_ssd_a_log, v_ssd_d, v_ssd_norm_g, v_w_out, v_norm_mlp_g, v_w_ff1, v_w_ff2, v_ple_norm_g, v_w_ple_gate, v_w_ple_proj, v_final_norm_g):
    args = dict(locals())
    w = {n: args[n] for n in _WEIGHTS}
    m = {n: args["m_" + n] for n in _WEIGHTS}
    v = {n: args["v_" + n] for n in _WEIGHTS}
    shapes = {n: w[n].shape for n in _WEIGHTS}
    my_block = 4 * lax.axis_index("x") + 2 * lax.axis_index("y") + lax.axis_index("c")
    nb, seq_len, _ = x.shape

    local = lambda d, n: d[n][0].T if n == "w_in" else d[n][0]
    shard = {n: local(w, n).astype(_MXU) for n in _BIG_NAMES}
    small = {n: w[n] for n in _WEIGHTS if n not in _BIG_NAMES}
    gx, recv, small_blocks, late8, n_rows, where = _local_step(
        x.reshape(nb * seq_len, D), p.reshape(nb * seq_len, D_PLE), loss_target.reshape(nb * seq_len, D), shard,
        ssd_conv_w[0], small, seq_len)

    big_out = [{}, {}, {}, {}]
    for n in _BIG_NAMES:
        res = _sum_adam("sum_adam_" + n, recv[n], local(w, n), local(m, n), local(v, n))
        for k in range(4):
            big_out[k][n] = (res[k].T if n == "w_in" else res[k]).reshape(shapes[n])

    packs = [_pack_small_params(d, where, n_rows, my_block) for d in (w, m, v)]
    small_res = _small_final(small_blocks, late8, where[_LATE], *packs)
    loss = small_res[0][where["loss"], 0]
    small_out = [_unpack_small(a, where, my_block, shapes) for a in small_res]

    outs = [loss, gx.reshape(x.shape)]
    for k in range(4):
        outs += [big_out[k][n] if n in _BIG_NAMES else small_out[k][n] for n in _WEIGHTS]
    return tuple(outs)
```

```python
import functools
import math

import jax
import jax.numpy as jnp
import numpy as np
from jax import lax
from jax.experimental import pallas as pl
from jax.experimental.pallas import tpu as pltpu

_F32 = jnp.float32
_BF16 = jnp.bfloat16
_MXU = jnp.bfloat16

D = 1024
D_PLE = 256
GM_W = 1024
GM_H = 8
CH = 128
SSD_W = 1024
SSD_H = 16
SSD_P = 64
SSD_G = 2
SSD_N = 128
CONV_K = 4
CONV_CH = SSD_W + 2 * SSD_G * SSD_N
D_FF = 4096
D_IN = 2 * GM_W + SSD_W + CONV_CH + SSD_H
D_IN_PAD = 4736
EPS = 1e-6
DT_BLK = (D_IN_PAD - 128) // 128
N_DEV = 8
SHARD_IN = D_IN // N_DEV

LR, B1, B2, ADAM_EPS, WD, STEP = 0.001, 0.9, 0.999, 1e-08, 0.01, 10
_LOG2E = math.log2(math.e)
_LOG2_INV_SQRT_2PI = -0.5 * math.log2(2.0 * math.pi)

_SEQ_PER_STEP = 2
_V7X_VMEM_BYTES = 64 * 1024 * 1024
_VMEM_CAP = _V7X_VMEM_BYTES - 8 * 1024 * 1024
_MESH = pl.DeviceIdType.MESH


def _vmem_limit(nbytes):
    return int(min(_VMEM_CAP, max(32 * 1024 * 1024, nbytes * 5 // 4 + (4 << 20))))


def _nbytes(shape, dtype):
    return int(np.prod(shape)) * jnp.dtype(dtype).itemsize


def _mx(v):
    return v.astype(_MXU)


def _dot(a, b):
    return jnp.dot(a, b, preferred_element_type=_F32)


def _dot_nt(a, b):
    return lax.dot_general(a, b, (((1,), (1,)), ((), ())), preferred_element_type=_F32)


def _dot_tn(a, b):
    return lax.dot_general(a, b, (((0,), (0,)), ((), ())), preferred_element_type=_F32)


def _split3(a):
    hi = a.astype(_BF16)
    r = a - hi.astype(_F32)
    mid = r.astype(_BF16)
    lo = (r - mid.astype(_F32)).astype(_BF16)
    return hi, mid, lo


def _xdot(dotfn, a, b01):
    b = b01.astype(_BF16)
    hi, mid, lo = _split3(a)
    return (dotfn(hi, b) + dotfn(mid, b)) + dotfn(lo, b)


def _xdot_left(dotfn, a01, b):
    a = a01.astype(_BF16)
    hi, mid, lo = _split3(b)
    return (dotfn(a, hi) + dotfn(a, mid)) + dotfn(a, lo)


def _sum8(v):
    r, n = v.shape
    return v.reshape(r // 8, 8, n).sum(axis=0)


def _sigmoid(v):
    return 1.0 / (1.0 + jnp.exp(-v))


def _rms(xv, g):
    ms = jnp.mean(xv * xv, axis=-1, keepdims=True)
    return xv * lax.rsqrt(ms + EPS) * g


def _rms_bwd(xv, g, dn):
    r = lax.rsqrt(jnp.mean(xv * xv, axis=-1, keepdims=True) + EPS)
    nh = xv * r
    gy = dn * g
    dx = (gy - nh * jnp.mean(gy * nh, axis=-1, keepdims=True)) * r
    return dx, _sum8(dn * nh)


def _iota2(shape, axis):
    return lax.broadcasted_iota(jnp.int32, shape, axis)


def _norm_cast(name, x, g, tm=512, comm=()):
    t, n = x.shape
    tm = min(tm, t)
    steps = t // tm
    kinds = [kind for kind, _ in comm]
    c_in, c_in_specs, c_out_specs, c_out_shape, c_scratch = _comm_io(comm)

    def body(*refs):
        x_ref, g_ref = refs[0], refs[1]
        o_ref = refs[2 + len(comm)]
        comm_refs = (kinds, refs[2:2 + len(comm)], refs[3 + len(comm):3 + 2 * len(comm)], *refs[3 + 2 * len(comm):])
        if comm:
            pl.when(pl.program_id(0) == 0)(lambda: _comm_start(*comm_refs))

        o_ref[...] = _rms(x_ref[...], g_ref[...]).astype(o_ref.dtype)
        if comm:
            pl.when(pl.program_id(0) == steps - 1)(lambda: _comm_finish(*comm_refs))

    res = pl.pallas_call(
        body, name=name, grid=(steps,),
        in_specs=[pl.BlockSpec((tm, n), lambda i: (i, 0)), pl.BlockSpec((1, n), lambda i: (0, 0))] + c_in_specs,
        out_specs=[pl.BlockSpec((tm, n), lambda i: (i, 0))] + c_out_specs,
        out_shape=[jax.ShapeDtypeStruct((t, n), _MXU)] + c_out_shape, scratch_shapes=c_scratch,
        compiler_params=pltpu.CompilerParams(dimension_semantics=("arbitrary",)),
    )(x, g, *c_in)
    return res[0], res[1:]


def _matmul(name, a, b, mode, tm, tn, tk, epilogue, outs, extras=(), comm=(), b_to_epilogue=False):
    m, k = a.shape[::-1] if mode == "tn" else a.shape
    n = b.shape[0] if mode == "nt" else b.shape[1]
    tm, tn, tk = min(tm, m), min(tn, n), min(tk, k)
    assert m % tm == 0 and n % tn == 0 and k % tk == 0, (name, m, n, k, tm, tn, tk)
    if mode == "nn":
        a_spec = pl.BlockSpec((tm, tk), lambda i, j, kk: (i, kk))
        b_spec = pl.BlockSpec((tk, tn), lambda i, j, kk: (kk, j))
        dotfn = _dot
    elif mode == "nt":
        a_spec = pl.BlockSpec((tm, tk), lambda i, j, kk: (i, kk))
        b_spec = pl.BlockSpec((tn, tk), lambda i, j, kk: (j, kk))
        dotfn = _dot_nt
    else:
        a_spec = pl.BlockSpec((tk, tm), lambda i, j, kk: (kk, i))
        b_spec = pl.BlockSpec((tk, tn), lambda i, j, kk: (kk, j))
        dotfn = _dot_tn
    ni, nj, nk = m // tm, n // tn, k // tk
    n_ex, n_out, n_comm = len(extras), len(outs), len(comm)
    kinds = [kind for kind, _ in comm]
    c_in, c_in_specs, c_out_specs, c_out_shape, c_scratch = _comm_io(comm)

    in_specs, vmem = [a_spec, b_spec], 2 * (tm * tk * a.dtype.itemsize + tk * tn * b.dtype.itemsize)
    for arr, kind in extras:
        if kind == "tile":
            in_specs.append(pl.BlockSpec((tm, tn), lambda i, j, kk: (i, j)))
            vmem += 2 * _nbytes((tm, tn), arr.dtype)
        elif kind == "rows":
            in_specs.append(pl.BlockSpec((tm, arr.shape[1]), lambda i, j, kk: (i, 0)))
            vmem += 2 * _nbytes((tm, arr.shape[1]), arr.dtype)
        elif kind == "full":
            in_specs.append(pl.BlockSpec(arr.shape, lambda i, j, kk: (0,) * arr.ndim))
            vmem += 2 * _nbytes(arr.shape, arr.dtype)
        else:
            in_specs.append(pl.BlockSpec((1, tn), lambda i, j, kk: (0, j)))
    out_specs, out_shape = [], []
    for kind, dt in outs:
        if kind == "tile":
            out_specs.append(pl.BlockSpec((tm, tn), lambda i, j, kk: (i, j)))
            out_shape.append(jax.ShapeDtypeStruct((m, n), dt))
            vmem += 2 * _nbytes((tm, tn), dt)
        elif kind == "tile_t":
            out_specs.append(pl.BlockSpec((tn, tm), lambda i, j, kk: (j, i)))
            out_shape.append(jax.ShapeDtypeStruct((n, m), dt))
            vmem += 2 * _nbytes((tm, tn), dt)
        else:
            assert nj == 1, "the partial-sum rows are accumulated over consecutive row tiles"
            out_specs.append(pl.BlockSpec((8, tn), lambda i, j, kk: (0, 0)))
            out_shape.append(jax.ShapeDtypeStruct((8, n), dt))
    scratch = [pltpu.VMEM((tm, tn), _F32)] if nk > 1 else []
    vmem += _nbytes((tm, tn), _F32) * 2

    def body(*refs):
        a_ref, b_ref = refs[0], refs[1]
        ex_refs = refs[2:2 + n_ex]
        n_in = 2 + n_ex + n_comm
        out_refs = refs[n_in:n_in + n_out]
        i, j, kk = pl.program_id(0), pl.program_id(1), pl.program_id(2)
        comm_refs = (kinds, refs[2 + n_ex:n_in], refs[n_in + n_out:n_in + n_out + n_comm], *refs[len(refs) - 3:])
        if n_comm:
            pl.when((i == 0) & (j == 0) & (kk == 0))(lambda: _comm_start(*comm_refs))

        b_val = _mx(b_ref[...])
        part = dotfn(_mx(a_ref[...]), b_val)

        def finish(acc):
            vals = epilogue(acc, *([b_val] if b_to_epilogue else []), *[r[...] for r in ex_refs])
            for r, v, (kind, _) in zip(out_refs, vals, outs):
                if kind == "part8":
                    @pl.when(i == 0)
                    def _():
                        r[...] = v

                    @pl.when(i > 0)
                    def _():
                        r[...] += v
                else:
                    r[...] = v.astype(r.dtype)

        if nk == 1:
            finish(part)
        else:
            acc_ref = refs[n_in + n_out + n_comm]

            @pl.when(kk == 0)
            def _():
                acc_ref[...] = part

            @pl.when(kk > 0)
            def _():
                acc_ref[...] += part

            @pl.when(kk == nk - 1)
            def _():
                finish(acc_ref[...])

        if n_comm:
            pl.when((i == ni - 1) & (j == nj - 1) & (kk == nk - 1))(lambda: _comm_finish(*comm_refs))

    carried =n_comm or any(kind == "part8" for kind, _ in outs)
    sem = ("arbitrary",) * 3 if carried else ("parallel", "parallel", "arbitrary")
    res = pl.pallas_call(
        body, name=name, grid=(ni, nj, nk),
        in_specs=in_specs + c_in_specs, out_specs=out_specs + c_out_specs, out_shape=out_shape + c_out_shape,
        scratch_shapes=scratch + c_scratch,
        compiler_params=pltpu.CompilerParams(dimension_semantics=sem, vmem_limit_bytes=_vmem_limit(vmem)),
    )(a, b, *[arr for arr, _ in extras], *c_in)
    return res[:n_out], res[n_out:]


def _shift_down(v, halo8, j):
    if j == 0:
        return v
    r = pltpu.roll(v, j, axis=0)
    hr = pltpu.roll(halo8, j, axis=0)
    top = jnp.where(_iota2(hr.shape, 0) < j, hr, r[:8])
    return jnp.concatenate([top, r[8:]], axis=0)


def _shift_up(v, next8, j):
    if j == 0:
        return v
    rows = v.shape[0]
    r = pltpu.roll(v, rows - j, axis=0)
    nr = pltpu.roll(next8, 8 - j, axis=0)
    bot = jnp.where(_iota2(nr.shape, 0) >= 8 - j, nr, r[rows - 8:])
    return jnp.concatenate([r[:rows - 8], bot], axis=0)


def _silu_grad(sig, silu):
    return sig + silu * (1.0 - sig)


def _gmlp_fwd_vals(pu, pv, gv, ws_ref, bsb_ref, want_bwd):
    tril = _iota2((CH, CH), 0) >= _iota2((CH, CH), 1)
    cdf_u = 0.5 * (1.0 + lax.erf(pu * 0.7071067811865476))
    cdf_v = 0.5 * (1.0 + lax.erf(pv * 0.7071067811865476))
    u = pu * cdf_u
    v = pv * cdf_v
    ys, keep = [], [(cdf_u, cdf_v)] if want_bwd else []
    for h in range(GM_H):
        sl = slice(h * 128, (h + 1) * 128)
        vh = v[:, sl]
        r = lax.rsqrt(jnp.mean(vh * vh, axis=-1, keepdims=True) + EPS)
        vn = vh * r * gv[:, sl]
        wm = _mx(jnp.where(tril, ws_ref[h], 0.0))
        mixed = _dot(wm, _mx(vn)) + bsb_ref[h]
        ys.append(u[:, sl] * mixed)
        if want_bwd:
            keep.append((vh, r, vn, wm, mixed))
    return jnp.concatenate(ys, axis=1), u, keep


def _ssd_conv(xbc, halo8, cw_ref, cb):
    cpre = cb + sum(cw_ref[k:k + 1, :] * _shift_down(xbc, halo8, CONV_K - 1 - k) for k in range(CONV_K))
    sig = _sigmoid(cpre)
    return sig, cpre * sig


def _ssd_decay(dtraw, dtb, alog, e_ref, ltri):
    dtin = dtraw + dtb
    dt = jnp.maximum(dtin, 0.0) + jnp.log(1.0 + jnp.exp(-jnp.abs(dtin)))
    a_neg = -jnp.exp(alog)
    cs = _xdot_left(_dot, ltri, dt * a_neg)
    cs_last = cs[CH - 1:CH, :]
    ecs = jnp.exp(cs)
    dec = jnp.exp(cs_last - cs)
    cdec = jnp.exp(cs_last)
    e = e_ref[...]
    dt_x = _dot(dt.astype(_BF16), e)
    ecs_x = _dot(ecs.astype(_BF16), e)
    dec_x = _dot(dec.astype(_BF16), e)
    cdec_x = _xdot(_dot, jnp.broadcast_to(cdec, (8, 128)), e)[0:1, :]
    return dict(dtin=dtin, dt=dt, a_neg=a_neg, cs=cs, ecs=ecs, dec=dec, cdec=cdec,
                dt_x=dt_x, ecs_x=ecs_x, dec_x=dec_x, cdec_x=cdec_x)


def _head_lm(cs, cst_ref, h, tril):
    seg = jnp.broadcast_to(cs[:, h:h + 1], (CH, CH)) - cst_ref[h:h + 1, :]
    return jnp.exp(jnp.where(tril, seg, -jnp.inf))


def _mixer_fwd(proj, gv, ws, bsb, gout, cw8, cb, dtb, alog, d_x, ng, e_mat, ltri_mat, seq_chunks, comm=()):
    t = proj.shape[0]
    n_seq = t // (seq_chunks * CH)
    sb = math.gcd(_SEQ_PER_STEP, n_seq)
    n_groups = n_seq // sb
    proj = proj.reshape(n_seq, seq_chunks * CH, proj.shape[1])
    n_comm = len(comm)
    kinds = [kind for kind, _ in comm]
    c_in, c_in_specs, c_out_specs, c_out_shape, c_scratch = _comm_io(comm)

    def body(*refs):
        comm_refs = (kinds, refs[18:18 + n_comm], refs[23 + n_comm:23 + 2 * n_comm], *refs[25 + 2 * n_comm:])
        grp, c = pl.program_id(0), pl.program_id(1)
        if n_comm:
            pl.when((grp == 0) & (c == 0))(lambda: _comm_start(*comm_refs))
            pl.when((grp == n_groups - 1) & (c == seq_chunks - 1))(lambda: _comm_finish(*comm_refs))
        for s in range(sb):
            per_seq = lambda rs: [r.at[s] for r in rs]
            one_chunk(c == 0, *per_seq(refs[:6]), *refs[6:18], *per_seq(refs[18 + n_comm:23 + n_comm]),
                      *per_seq(refs[23 + 2 * n_comm:25 + 2 * n_comm]))

    def one_chunk(first, pu_ref, pv_ref, z_ref, xbc_ref, dt_ref, halo_ref, gv_ref, ws_ref, bsb_ref, gout_ref, cw_ref,
                  cb_ref, dtb_ref, alog_ref, dx_ref, ng_ref, e_ref, ltri_ref, cat_ref, y_ref, st_ref, act_ref, sig_ref,
                  s_ref, cst_ref):
        tril = _iota2((CH, CH), 0) >= _iota2((CH, CH), 1)
        lane = _iota2((CH, 128), 1)

        y_a, _, _ = _gmlp_fwd_vals(pu_ref[...], pv_ref[...], gv_ref[...], ws_ref, bsb_ref, False)
        cat_ref[:, 0:GM_W] = _rms(y_a, gout_ref[...]).astype(cat_ref.dtype)

        @pl.when(first)
        def _():
            s_ref[...] = jnp.zeros_like(s_ref)

        halo8 = jnp.where(first, 0.0, halo_ref[...])
        sig, act = _ssd_conv(xbc_ref[...], halo8, cw_ref, cb_ref[...])
        sig_ref[...] = sig
        act_ref[...] = act
        q = _ssd_decay(dt_ref[...], dtb_ref[...], alog_ref[...], e_ref, ltri_ref[...])
        xv = act[:, 0:SSD_W]
        xdt = xv * q["dt_x"]
        xdt_m = _mx(xdt)
        cs = q["cs"]
        cst_ref[...] = cs.T
        s_prev = s_ref[...]
        st_ref[...] = s_prev
        ys = []
        for g in range(SSD_G):
            bg = _mx(act[:, SSD_W + g * SSD_N:SSD_W + (g + 1) * SSD_N])
            cg = _mx(act[:, SSD_W + SSD_G * SSD_N + g * SSD_N:SSD_W + SSD_G * SSD_N + (g + 1) * SSD_N])
            cbm = _dot_nt(cg, bg)
            gs = slice(g * 512, (g + 1) * 512)
            for pr in range(4):
                ps = slice(g * 512 + pr * 128, g * 512 + (pr + 1) * 128)
                o = []
                for hh in range(2):
                    h = g * 8 + pr * 2 + hh
                    m_h = _mx(cbm * _head_lm(cs, cst_ref, h, tril))
                    o.append(_dot(m_h, xdt_m[:, ps]))
                ys.append(jnp.where(lane < SSD_P, o[0], o[1]))
            sg = s_prev[:, gs]
            yoff = _dot(cg, _mx(sg)) * q["ecs_x"][:, gs]
            ys[-4:] = [ys[-4 + i] + yoff[:, i * 128:(i + 1) * 128] for i in range(4)]
            st_new = _dot_tn(bg, _mx(q["dec_x"][:, gs] * xdt[:, gs]))
            s_ref[:, gs] = sg * q["cdec_x"][:, gs] + st_new
        y = jnp.concatenate(ys, axis=1) + dx_ref[...] * xv
        y_ref[...] = y
        zv = z_ref[...]
        yg = y * (zv * _sigmoid(zv))
        for g in range(SSD_G):
            gs = slice(g * 512, (g + 1) * 512)
            cat_ref[:, GM_W + g * 512:GM_W + (g + 1) * 512] = _rms(yg[:, gs], ng_ref[:, gs]).astype(cat_ref.dtype)

    blk = lambda w, j: pl.BlockSpec((sb, CH, w), lambda g, c: (g, c, j))
    full = lambda arr: pl.BlockSpec(arr.shape, lambda g, c: (0,) * arr.ndim)
    consts = [gv, ws, bsb, gout, cw8, cb, dtb, alog, d_x, ng, e_mat, ltri_mat]
    seq = seq_chunks * CH
    res = pl.pallas_call(
        body, name="mixer_fwd", grid=(n_groups, seq_chunks),
        in_specs=[blk(GM_W, 0), blk(GM_W, 1), blk(SSD_W, 2), blk(CONV_CH, 2), blk(128, DT_BLK),
                  pl.BlockSpec((sb, 8, CONV_CH), lambda g, c: (g, jnp.maximum(c * (CH // 8) - 1, 0), 2))]
        + [full(a) for a in consts] + c_in_specs,
        out_specs=[blk(2 * D, 0), blk(SSD_W, 0), blk(SSD_W, 0), blk(CONV_CH, 0), blk(CONV_CH, 0)] + c_out_specs,
        out_shape=[jax.ShapeDtypeStruct((n_seq, seq, 2 * D), _MXU), jax.ShapeDtypeStruct((n_seq, seq, SSD_W), _F32),
                   jax.ShapeDtypeStruct((n_seq, seq, SSD_W), _F32), jax.ShapeDtypeStruct((n_seq, seq, CONV_CH), _F32),
                   jax.ShapeDtypeStruct((n_seq, seq, CONV_CH), _F32)] + c_out_shape,
        scratch_shapes=[pltpu.VMEM((sb, SSD_N, SSD_W), _F32), pltpu.VMEM((sb, 128, CH), _F32)] + c_scratch,
        compiler_params=pltpu.CompilerParams(dimension_semantics=("arbitrary", "arbitrary"),
                                             vmem_limit_bytes=48 << 20),
    )(proj, proj, proj, proj, proj, proj, *consts, *c_in)
    return [r.reshape(t, r.shape[-1]) for r in res[:5]], res[5:]


def _mixer_bwd(proj, act, sig, dcat, yss, states, gv, ws, bsb, gout, cw8, cb, dtb, alog, d_x, ng, e_mat, et_mat,
               ltri_mat, seq_chunks, comm=()):
    t = proj.shape[0]
    n_seq = t // (seq_chunks * CH)
    sb = math.gcd(_SEQ_PER_STEP, n_seq)
    n_groups = n_seq // sb
    seq = seq_chunks * CH
    proj, act, sig, dcat, yss, states = [a.reshape(n_seq, seq, a.shape[1])
                                         for a in (proj, act, sig, dcat, yss, states)]
    n_comm = len(comm)
    kinds = [kind for kind, _ in comm]
    c_in, c_in_specs, c_out_specs, c_out_shape, c_scratch = _comm_io(comm)

    def body(*refs):
        o0, s0 = 23 + n_comm, 34 + 2 * n_comm
        acc_refs, shared = refs[o0 + 1:o0 + 11], refs[s0 + 4:s0 + 6]
        comm_refs = (kinds, refs[23:23 + n_comm], refs[o0 + 11:o0 + 11 + n_comm], *refs[s0 + 6:])
        grp, i = pl.program_id(0), pl.program_id(1)
        if n_comm:
            pl.when((grp == 0) & (i == 0))(lambda: _comm_start(*comm_refs))

        @pl.when((grp == 0) & (i == 0))
        def _():
            for r in (*acc_refs, *shared, refs[s0 + 2]):
                r[...] = jnp.zeros_like(r)

        @pl.when(i == 0)
        def _():
            refs[s0][...] = jnp.zeros_like(refs[s0])
            refs[s0 + 1][...] = jnp.zeros_like(refs[s0 + 1])

        for s in range(sb):
            per_seq = lambda rs: [r.at[s] for r in rs]
            last_of_all = ((grp == n_groups - 1) & (i == seq_chunks - 1)) if s == sb - 1 else None
            one_chunk(last_of_all, *per_seq(refs[:10]), *refs[10:23], refs[o0].at[s], *acc_refs,
                      *per_seq(refs[s0:s0 + 4]), *shared)
        if n_comm:
            pl.when((grp == n_groups - 1) & (i == seq_chunks - 1))(lambda: _comm_finish(*comm_refs))

    def one_chunk(finalize, pu_ref, pv_ref, z_ref, xbc_ref, dt_ref, act_ref, sig_ref, dcat_ref, y_ref, st_ref,
                  gv_ref, ws_ref, bsb_ref, gout_ref, cw_ref, cb_ref, dtb_ref, alog_ref, dx_ref, ng_ref,
                  e_ref, et_ref, ltri_ref,
                  dproj_ref, dws_ref, dbs_ref, dgv_ref, dgout_ref, dng_ref, dcw_ref, dcb_ref, ddtb_ref, dalog_ref,
                  dd_ref, ds_ref, dnext_ref, dcst_ref, cst_ref, dbacc_ref, ddacc_ref):
        tril = _iota2((CH, CH), 0) >= _iota2((CH, CH), 1)
        lane = _iota2((CH, 128), 1)
        row = _iota2((CH, 128), 0)
        dcat_v = dcat_ref[...].astype(_F32)

        pu, pv = pu_ref[...], pv_ref[...]
        gv_v = gv_ref[...]
        y_a, u, keep = _gmlp_fwd_vals(pu, pv, gv_v, ws_ref, bsb_ref, True)
        dy, dgout8 = _rms_bwd(y_a, gout_ref[...], dcat_v[:, 0:GM_W])
        dgout_ref[...] += dgout8
        dus, dvs, dgvs = [], [], []
        for h in range(GM_H):
            sl = slice(h * 128, (h + 1) * 128)
            vh, r, vn, wm, mixed = keep[h + 1]
            dyh = dy[:, sl]
            dus.append(dyh * mixed)
            dmix = dyh * u[:, sl]
            dmix_m = _mx(dmix)
            dws_ref[h] += jnp.where(tril, _dot_nt(dmix_m, _mx(vn)), 0.0)
            dbacc_ref[h] += dmix
            dvn = _dot_tn(wm, dmix_m)
            gy = dvn * gv_v[:, sl]
            nh = vh * r
            dvs.append((gy - nh * jnp.mean(gy * nh, axis=-1, keepdims=True)) * r)
            dgvs.append(_sum8(dvn * nh))
        dgv_ref[...] += jnp.concatenate(dgvs, axis=1)
        cdf_u, cdf_v = keep[0]
        gelu_grad = lambda pre, cdf: cdf + pre * jnp.exp2(pre * pre * (-0.5 * _LOG2E) + _LOG2_INV_SQRT_2PI)
        dproj_ref[:, 0:GM_W] = (jnp.concatenate(dus, axis=1) * gelu_grad(pu, cdf_u)).astype(dproj_ref.dtype)
        dproj_ref[:, GM_W:2 * GM_W] = (jnp.concatenate(dvs, axis=1) * gelu_grad(pv, cdf_v)).astype(dproj_ref.dtype)

        q = _ssd_decay(dt_ref[...], dtb_ref[...], alog_ref[...], e_ref, ltri_ref[...])
        act = act_ref[...]
        xv = act[:, 0:SSD_W]
        dt_x, ecs_x, dec_x, cdec_x = q["dt_x"], q["ecs_x"], q["dec_x"], q["cdec_x"]
        xdt = xv * dt_x
        xdt_m = _mx(xdt)
        cs = q["cs"]
        cst_ref[...] = cs.T
        s_prev = st_ref[...]
        ds = ds_ref[...]
        yv = y_ref[...]
        zv = z_ref[...]
        sig_z = _sigmoid(zv)
        sz = zv * sig_z
        yg = yv * sz
        dygs, dng8 = [], []
        for g in range(SSD_G):
            gs = slice(g * 512, (g + 1) * 512)
            a_, b_ = _rms_bwd(yg[:, gs], ng_ref[:, gs], dcat_v[:, GM_W + g * 512:GM_W + (g + 1) * 512])
            dygs.append(a_)
            dng8.append(b_)
        dyg = jnp.concatenate(dygs, axis=1)
        dng_ref[...] += jnp.concatenate(dng8, axis=1)
        dyv = dyg * sz
        dproj_ref[:, 2 * GM_W:2 * GM_W + SSD_W] = (dyg * yv * _silu_grad(sig_z, sz)).astype(dproj_ref.dtype)
        ddacc_ref[...] += _sum8(dyv * xv)
        dyv_m = _mx(dyv)

        dxdt_parts, db_parts, dc_parts = [], [], []
        dcs = jnp.zeros((CH, 128), _F32)
        dcs_x_parts, ddec_x_parts, dcl_x_parts = [], [], []
        for g in range(SSD_G):
            gs = slice(g * 512, (g + 1) * 512)
            bg = _mx(act[:, SSD_W + g * SSD_N:SSD_W + (g + 1) * SSD_N])
            cg = _mx(act[:, SSD_W + SSD_G * SSD_N + g * SSD_N:SSD_W + SSD_G * SSD_N + (g + 1) * SSD_N])
            cbm = _dot_nt(cg, bg)
            sg = s_prev[:, gs]
            sg_m = _mx(sg)
            dsg = ds[:, gs]
            dsg_m = _mx(dsg)
            zoff = _dot(cg, sg_m)
            dz_off = dyv[:, gs] * ecs_x[:, gs]
            dz_off_m = _mx(dz_off)
            dcs_x_parts.append(dyv[:, gs] * zoff * ecs_x[:, gs])
            dcg = _dot_nt(dz_off_m, sg_m)
            dsprev = _dot_tn(cg, dz_off_m)
            w_st = dec_x[:, gs] * xdt[:, gs]
            dw_st = _dot(bg, dsg_m)
            dbg = _dot_nt(_mx(w_st), dsg_m)
            dxdt_g = dec_x[:, gs] * dw_st
            ddec_x_parts.append(dw_st * xdt[:, gs])
            dsprev = dsprev + cdec_x[:, gs] * dsg
            dcl_x_parts.append(jnp.sum(dsg * sg, axis=0, keepdims=True) * cdec_x[:, gs])
            ds_ref[:, gs] = dsprev
            dcb = jnp.zeros((CH, CH), _F32)
            dxdt_pairs = []
            for pr in range(4):
                ps = slice(g * 512 + pr * 128, g * 512 + (pr + 1) * 128)
                acc_pair = None
                for hh in range(2):
                    h = g * 8 + pr * 2 + hh
                    in_head = (lane < SSD_P) if hh == 0 else (lane >= SSD_P)
                    lm = _head_lm(cs, cst_ref, h, tril)
                    m_h = cbm * lm
                    m_hm = _mx(m_h)
                    dyh_m = _mx(jnp.where(in_head, dyv[:, ps], 0.0))
                    dm = _dot_nt(dyh_m, xdt_m[:, ps])
                    dcb = dcb + dm * lm
                    qm = dm * m_h
                    dcs = dcs + jnp.where(lane == h, jnp.sum(qm, axis=1, keepdims=True), 0.0)
                    dcst_ref[h:h + 1, :] = jnp.sum(qm, axis=0, keepdims=True)
                    contrib = jnp.where(in_head, _dot_tn(m_hm, dyv_m[:, ps]), 0.0)
                    acc_pair = contrib if acc_pair is None else acc_pair + contrib
                dxdt_pairs.append(acc_pair)
            dxdt_parts.append(dxdt_g + jnp.concatenate(dxdt_pairs, axis=1))
            dcb_m = _mx(dcb)
            dc_parts.append(dcg + _dot(dcb_m, bg))
            db_parts.append(dbg + _dot_tn(dcb_m, cg))
        dxdt = jnp.concatenate(dxdt_parts, axis=1)
        dxv = dx_ref[...] * dyv + dxdt * dt_x
        et = et_ref[...]
        head_sum = lambda v: _dot(v.astype(_BF16), et)
        ddt = head_sum(dxdt * xv)
        dcs = dcs - dcst_ref[...].T + head_sum(jnp.concatenate(dcs_x_parts, axis=1))
        ddec = head_sum(jnp.concatenate(ddec_x_parts, axis=1)) * q["dec"]
        dcs = dcs - ddec
        dcl = jnp.sum(ddec, axis=0, keepdims=True) + _xdot(
            _dot, jnp.broadcast_to(jnp.concatenate(dcl_x_parts, axis=1), (8, SSD_W)), et)[0:1, :]
        dcs = jnp.where(row == CH - 1, dcs + dcl, dcs)
        da = _xdot_left(_dot_tn, ltri_ref[...], dcs)
        ddt = ddt + da * q["a_neg"]
        dalog_ref[...] += _sum8(da * q["dt"] * q["a_neg"])
        ddtraw = jnp.where(lane < SSD_H, ddt * _sigmoid(q["dtin"]), 0.0)
        ddtb_ref[...] += _sum8(ddtraw)
        dproj_ref[:, D_IN_PAD - 128:D_IN_PAD] = ddtraw.astype(dproj_ref.dtype)
        dcpre = jnp.concatenate([dxv] + db_parts + dc_parts, axis=1) * _silu_grad(sig_ref[...], act)
        dcb_ref[...] += _sum8(dcpre)
        next8 = dnext_ref[...]
        ups = [_shift_up(dcpre, next8, j) for j in range(CONV_K)]
        xbc = xbc_ref[...]
        for k in range(CONV_K):
            dcw_ref[k:k + 1, :] += jnp.sum(xbc * ups[CONV_K - 1 - k], axis=0, keepdims=True)
        dxbc = sum(cw_ref[k:k + 1, :] * ups[CONV_K - 1 - k] for k in range(CONV_K))
        dproj_ref[:, 2 * GM_W + SSD_W:2 * GM_W + SSD_W + CONV_CH] = dxbc.astype(dproj_ref.dtype)
        dnext_ref[...] = dcpre[0:8, :]

        if finalize is not None:
            @pl.when(finalize)
            def _():
                for h in range(GM_H):
                    dbs_ref[h:h + 1, :] = _xdot_left(_dot_nt, jnp.ones((8, 128), _BF16), dbacc_ref[h])[0:1, :]
                dd_ref[...] = _xdot(_dot, ddacc_ref[...], et)

    rblk = lambda w, j: pl.BlockSpec((sb, CH, w), lambda g, i: (g, seq_chunks - 1 - i, j))
    full = lambda arr: pl.BlockSpec(arr.shape, lambda g, i: (0,) * arr.ndim)
    acc = lambda shape: pl.BlockSpec(shape, lambda g, i: (0,) * len(shape))
    consts = [gv, ws, bsb, gout, cw8, cb, dtb, alog, d_x, ng, e_mat, et_mat, ltri_mat]
    acc_shapes = [(GM_H, CH, CH), (8, 128), (8, GM_W), (8, GM_W), (8, SSD_W), (8, CONV_CH), (8, CONV_CH), (8, 128),
                  (8, 128), (8, 128)]
    res = pl.pallas_call(
        body, name="mixer_bwd", grid=(n_groups, seq_chunks),
        in_specs=[rblk(GM_W, 0), rblk(GM_W, 1), rblk(SSD_W, 2), rblk(CONV_CH, 2), rblk(128, DT_BLK),
                  rblk(CONV_CH, 0), rblk(CONV_CH, 0),
                  rblk(2 * D, 0), rblk(SSD_W, 0), rblk(SSD_W, 0)] + [full(a) for a in consts] + c_in_specs,
        out_specs=[rblk(D_IN_PAD, 0)] + [acc(s) for s in acc_shapes] + c_out_specs,
        out_shape=[jax.ShapeDtypeStruct((n_seq, seq, D_IN_PAD), _MXU)]
        + [jax.ShapeDtypeStruct(s, _F32) for s in acc_shapes] + c_out_shape,
        scratch_shapes=[pltpu.VMEM((sb, SSD_N, SSD_W), _F32), pltpu.VMEM((sb, 8, CONV_CH), _F32),
                        pltpu.VMEM((sb, 128, CH), _F32), pltpu.VMEM((sb, 128, CH), _F32),
                        pltpu.VMEM((GM_H, CH, 128), _F32), pltpu.VMEM((8, SSD_W), _F32)] + c_scratch,
        compiler_params=pltpu.CompilerParams(dimension_semantics=("arbitrary", "arbitrary"),
                                             vmem_limit_bytes=48 << 20),
    )(proj, proj, proj, proj, proj, act, sig, dcat, yss, states, *consts, *c_in)
    return [res[0].reshape(t, D_IN_PAD)] + list(res[1:11]), res[11:]


def _peers():
    x, y, c = lax.axis_index("x"), lax.axis_index("y"), lax.axis_index("c")
    out = []
    for k in range(1, N_DEV):
        fx, fy, fc = (k >> 2) & 1, (k >> 1) & 1, k & 1
        px, py, pc = (x + fx) % 2, (y + fy) % 2, (c + fc) % 2
        out.append((k - 1, (px, py, pc), 4 * px + 2 * py + pc))
    return out, 4 * x + 2 * y + c


def _comm_io(comm):
    any_spec = pl.BlockSpec(memory_space=pl.ANY)
    n = len(comm)
    out_shape = []
    for (kind, axis), src in comm:
        shp = list(src.shape)
        if kind in ("gather", "gather2"):
            shp[axis] *= N_DEV
        else:
            shp[axis] //= N_DEV
            shp = [N_DEV] + shp
        out_shape.append(jax.ShapeDtypeStruct(tuple(shp), src.dtype))
    scratch = [pltpu.SemaphoreType.DMA((n * (N_DEV - 1),)), pltpu.SemaphoreType.DMA((n * (N_DEV - 1),)),
               pltpu.SemaphoreType.DMA((n,))] if n else []
    return [src for _, src in comm], [any_spec] * n, [any_spec] * n, out_shape, scratch


def _window(ref, axis, idx, size):
    start = pl.multiple_of(idx * size, size)
    return ref.at[tuple(pl.ds(start, size) if a == axis else slice(None) for a in range(len(ref.shape)))]


def _comm_plans(kinds, src_refs, dst_refs, send_sems, recv_sems, local_sems):
    x, y, c = lax.axis_index("x"), lax.axis_index("y"), lax.axis_index("c")
    peers, me = _peers()
    plans = []
    for s, ((kind, axis), src, dst) in enumerate(zip(kinds, src_refs, dst_refs)):
        sems = lambda k: dict(send_sem=send_sems.at[s * (N_DEV - 1) + k], recv_sem=recv_sems.at[s * (N_DEV - 1) + k])
        remote = lambda src_ref, dst_ref, k, pid: pltpu.make_async_remote_copy(
            src_ref=src_ref, dst_ref=dst_ref, device_id=pid, device_id_type=_MESH, **sems(k))
        if kind == "gather2":
            size = src.shape[axis]
            win = lambda idx: _window(dst, axis, idx, size)
            sib, sib_idx = (x, y, 1 - c), 4 * x + 2 * y + (1 - c)
            local = pltpu.make_async_copy(src, win(me), local_sems.at[s])
            to_sib = remote(src, win(me), 0, sib)
            starts, forwards = [local, to_sib], []
            waits = [(local, "local"), (to_sib, "send"), (remote(src, win(sib_idx), 0, sib), "recv")]
            for j, (fx, fy) in enumerate(((1, 0), (0, 1), (1, 1))):
                px, py = (x + fx) % 2, (y + fy) % 2
                same, other = 4 * px + 2 * py + c, 4 * px + 2 * py + (1 - c)
                out = remote(src, win(me), 1 + j, (px, py, c))
                starts.append(out)
                passed = remote(win(same), win(same), 4 + j, sib)
                forwards.append((remote(src, win(same), 1 + j, (px, py, c)), passed))
                waits += [(out, "send"), (passed, "send"), (remote(win(other), win(other), 4 + j, sib), "recv")]
            plans.append((starts, forwards, waits))
            continue
        if kind == "gather":
            size = src.shape[axis]
            src_for = lambda pidx: src
            dst_mine = _window(dst, axis, me, size)
        else:
            size = src.shape[axis] // N_DEV
            src_for = lambda pidx: _window(src, axis, pidx, size)
            dst_mine = dst.at[me]
        local = pltpu.make_async_copy(src_for(me), dst_mine, local_sems.at[s])
        remotes = [remote(src_for(pidx), dst_mine, k, pid) for k, pid, pidx in peers]
        plans.append(([local] + remotes, [], [(local, "local")] + [(cp, "both") for cp in remotes]))
    return plans


def _comm_start(*refs):
    for starts, _, _ in _comm_plans(*refs):
        for cp in starts:
            cp.start()


def _comm_finish(*refs):
    for _, forwards, waits in _comm_plans(*refs):
        for arrival, cp in forwards:
            arrival.wait_recv()
            cp.start()
        for cp, what in waits:
            if what == "send":
                cp.wait_send()
            elif what == "recv":
                cp.wait_recv()
            else:
                cp.wait()


def _adam_vals(w, g, m, v):
    m = B1 * m + (1.0 - B1) * g
    v = B2 * v + (1.0 - B2) * (g * g)
    m_hat = m / (1.0 - B1 ** STEP)
    v_hat = v / (1.0 - B2 ** STEP)
    delta = -LR * (m_hat / (jnp.sqrt(v_hat) + ADAM_EPS) + WD * w)
    return delta, m, v


def _sum_adam(name, recv, w, m, v, tile=256):
    _, r, wd = recv.shape
    if r % min(tile, r) == 0:
        tr, tc = min(tile, r), wd
    else:
        tr, tc = r, tile
        assert wd % tc == 0, (name, r, wd)

    def body(recv_ref, w_ref, m_ref, v_ref, g_out, d_out, m_out, v_out):
        g = recv_ref[0].astype(_F32)
        for s in range(1, N_DEV):
            g = g + recv_ref[s].astype(_F32)
        d_, m_, v_ = _adam_vals(w_ref[...], g, m_ref[...], v_ref[...])
        g_out[...] = g
        d_out[...] = d_
        m_out[...] = m_
        v_out[...] = v_

    spec = pl.BlockSpec((tr, tc), lambda i, j: (i, j))
    return pl.pallas_call(
        body, name=name, grid=(r // tr, wd // tc),
        in_specs=[pl.BlockSpec((N_DEV, tr, tc), lambda i, j: (0, i, j)), spec, spec, spec],
        out_specs=[spec] * 4, out_shape=[jax.ShapeDtypeStruct((r, wd), _F32)] * 4,
        compiler_params=pltpu.CompilerParams(dimension_semantics=("parallel", "parallel"),
                                             vmem_limit_bytes=48 << 20),
    )(recv, w, m, v)


def _small_local(parts, segments, n_rows):
    def body(parts_ref, loc_ref):
        loc_ref[...] = jnp.zeros_like(loc_ref)
        for out_row, n_out, in_row, n_in, kind in segments:
            if kind == "copy":
                loc_ref[out_row:out_row + n_out, :] = parts_ref[in_row:in_row + n_in, :]
            else:
                s = jnp.sum(parts_ref[in_row:in_row + n_in, :], axis=0, keepdims=True)
                if kind == "loss":
                    s = jnp.broadcast_to(jnp.sum(s, axis=1, keepdims=True) * (0.5 / D), (1, D))
                loc_ref[out_row:out_row + 1, :] = s

    vm = pl.BlockSpec(memory_space=pltpu.VMEM)
    return pl.pallas_call(body, name="small_local", in_specs=[vm], out_specs=vm,
                          out_shape=jax.ShapeDtypeStruct((n_rows, D), _F32))(parts)


def _small_final(blocks, late8, late_row, w, m, v):
    n_rows = w.shape[0]

    def body(blocks_ref, late_ref, w_ref, m_ref, v_ref, g_out, d_out, m_out, v_out, loc_ref, recv_ref, send_sems,
             recv_sems):
        peers, me = _peers()
        loc_ref[...] = jnp.broadcast_to(jnp.sum(late_ref[...], axis=0, keepdims=True), (8, D))
        recv_ref[me] = loc_ref[...]
        copies = [pltpu.make_async_remote_copy(src_ref=loc_ref, dst_ref=recv_ref.at[me], send_sem=send_sems.at[k],
                                               recv_sem=recv_sems.at[k], device_id=pid, device_id_type=_MESH)
                  for k, pid, _ in peers]
        for cp in copies:
            cp.start()
        g = blocks_ref[0:n_rows, :]
        for s in range(1, N_DEV):
            g = g + blocks_ref[s * n_rows:(s + 1) * n_rows, :]
        for cp in copies:
            cp.wait()
        late = recv_ref[0]
        for s in range(1, N_DEV):
            late = late + recv_ref[s]
        g = jnp.where(_iota2((n_rows, D), 0) == late_row, jnp.broadcast_to(late[0:1, :], (n_rows, D)), g)
        d_, m_, v_ = _adam_vals(w_ref[...], g, m_ref[...], v_ref[...])
        g_out[...] = g
        d_out[...] = d_
        m_out[...] = m_
        v_out[...] = v_

    vm = pl.BlockSpec(memory_space=pltpu.VMEM)
    return pl.pallas_call(
        body, name="small_final", in_specs=[vm] * 5, out_specs=[vm] * 4,
        out_shape=[jax.ShapeDtypeStruct((n_rows, D), _F32)] * 4,
        scratch_shapes=[pltpu.VMEM((8, D), _F32), pltpu.VMEM((N_DEV, 8, D), _F32),
                        pltpu.SemaphoreType.DMA((N_DEV - 1,)), pltpu.SemaphoreType.DMA((N_DEV - 1,))],
        compiler_params=pltpu.CompilerParams(vmem_limit_bytes=48 << 20),
    )(blocks, late8, w, m, v)


_BIG_NAMES =("w_in", "w_out", "w_ff1", "w_ff2", "w_ple_gate", "w_ple_proj")

_G_VECS = ("norm_mix_g", "gm_v_norm_g", "gm_out_norm_g", "ssd_norm_g", "norm_mlp_g", "ple_norm_g", "final_norm_g")
_LATE = _G_VECS[0]


def _const_mats():
    h = np.arange(128)[:, None]
    ch = np.arange(SSD_W)[None, :]
    e = (ch // SSD_P == h).astype(np.float32)
    ltri = (np.arange(CH)[:, None] >= np.arange(CH)[None, :]).astype(np.float32)
    return jnp.asarray(e, _BF16), jnp.asarray(e.T, _BF16), jnp.asarray(ltri, _BF16)


def _pad_lanes(v, n=128):
    return jnp.pad(v, ((0, 0), (0, n - v.shape[1])))


def _local_step(x, p, tgt, shard, conv_w_shard, small, seq_len):
    seq_chunks = seq_len // CH
    e_mat, et_mat, ltri_mat = _const_mats()
    g_mix, g_mlp, g_ple = small["norm_mix_g"], small["norm_mlp_g"], small["ple_norm_g"]
    g_fin = small["final_norm_g"].reshape(1, D)
    gv, gout, ng = small["gm_v_norm_g"], small["gm_out_norm_g"], small["ssd_norm_g"]
    ws = small["gm_ws"][0]
    bsb = jnp.broadcast_to(small["gm_bs"][0][:, :, None], (GM_H, CH, 128))
    cb = small["ssd_conv_b"]
    dtb, alog = _pad_lanes(small["ssd_dt_bias"]), _pad_lanes(small["ssd_a_log"])
    d_x = jnp.repeat(small["ssd_d"], SSD_P, axis=1)

    first = lambda acc: (acc,)
    rows, cols = ("gather2", 0), ("gather2", 1)
    n1, (g_win, g_cw) = _norm_cast("norm_mix", x, g_mix,
                                   comm=[(rows, shard["w_in"][None]), (("gather", 0), conv_w_shard[None])])
    w_in_t = jnp.pad(g_win.reshape(D_IN, D), ((0, D_IN_PAD - D_IN), (0, 0)))
    cw8 = jnp.pad(g_cw.transpose(1, 0, 2).reshape(CONV_K, CONV_CH), ((0, 8 - CONV_K), (0, 0)))
    mix_consts = (gv, ws, bsb, gout, cw8, cb, dtb, alog, d_x, ng)
    (proj,), (w_out, w1) = _matmul("proj_in", n1, w_in_t, "nt", 512, D_IN_PAD, D, first, [("tile", _F32)],
                                   comm=[(rows, shard["w_out"]), (cols, shard["w_ff1"])])
    (cat, yss, states, conv_act, conv_sig), (w2, wg, wp) = _mixer_fwd(
        proj, *mix_consts, e_mat, ltri_mat, seq_chunks,
        comm=[(rows, shard["w_ff2"]), (rows, shard["w_ple_gate"]), (cols, shard["w_ple_proj"])])

    def epi_res_norm(acc, res, g):
        hv = acc + res
        return hv, _rms(hv, g)

    (h1, n2), _ = _matmul("proj_out", cat, w_out, "nn", 1024, D, 2 * D, epi_res_norm,
                          [("tile", _F32), ("tile", _MXU)], extras=[(x, "tile"), (g_mlp, "row")])

    def epi_relu2(acc):
        hid = jnp.maximum(acc, 0.0)
        return hid, hid * hid

    (hid, hid2), _ = _matmul("ff1", n2, w1, "nn", 1024, 1024, D, epi_relu2, [("tile", _MXU), ("tile", _MXU)])
    (h2, n3), _ = _matmul("ff2", hid2, w2, "nn", 512, D, D_FF, epi_res_norm, [("tile", _F32), ("tile", _MXU)],
                          extras=[(h1, "tile"), (g_ple, "row")])

    def epi_norm_bwd(acc, up, hv, g):
        dx, dg8 = _rms_bwd(hv, g, acc)
        dh = up + dx
        return dh, dh, dg8

    def epi_head(acc, wg_v, p_rows, wp_v, h2v, tg, gf, gp_):
        ppv = _dot(_mx(p_rows), wp_v)
        gate = _sigmoid(acc)
        gp = gate * ppv
        h3 = h2v + gp
        r = lax.rsqrt(jnp.mean(h3 * h3, axis=-1, keepdims=True) + EPS)
        nh = h3 * r
        err = nh * gf - tg
        gy = err * (gf * (1.0 / D))
        dh3 = (gy - nh * jnp.mean(gy * nh, axis=-1, keepdims=True)) * r
        dpp = dh3 * gate
        da3 = dpp * (ppv - gp)
        dh2, dh2_again, dgple8 = epi_norm_bwd(_dot_nt(_mx(da3), wg_v), dh3, h2v, gp_)
        return da3, dpp, dh2, dh2_again, _sum8(err * err), _sum8(err * nh) * (1.0 / D), dgple8

    (da3, dpp, dh2, dh2b, lossp, dgfin, dgple), _ = _matmul(
        "ple_gate_loss_bwd", n3, wg, "nn", 512, D, D, epi_head,
        [("tile", _MXU), ("tile", _MXU), ("tile", _F32), ("tile", _MXU), ("part8", _F32), ("part8", _F32),
         ("part8", _F32)],
        extras=[(p, "rows"), (wp, "full"), (h2, "tile"), (tgt, "tile"), (g_fin, "row"), (g_ple, "row")],
        b_to_epilogue=True)

    s_rows, s_cols = ("scatter", 0), ("scatter", 1)
    (dwp,), _ = _matmul("d_w_ple_proj", p, dpp, "tn", D_PLE, D, 2048, first, [("tile", _BF16)])
    (dwg,), _ = _matmul("d_w_ple_gate", n3, da3, "tn", D, D, 2048, first, [("tile", _BF16)])
    (dw2,), (r_wp, r_wg) = _matmul("d_w_ff2", hid2, dh2b, "tn", 1024, D, 4096, first, [("tile", _BF16)],
                                   comm=[(s_cols, dwp), (s_rows, dwg)])
    (da1,), _ = _matmul("d_ff_hidden", dh2b, w2, "nt", 512, 2048, D,
                        lambda acc, hv: (acc * 2.0 * hv.astype(_F32),), [("tile", _MXU)], extras=[(hid, "tile")])
    (dw1,), _ = _matmul("d_w_ff1", n2, da1, "tn", 1024, 1024, 4096, first, [("tile", _BF16)])
    (dh1, dh1b, dgmlp), _ = _matmul(
        "d_h1", da1, w1, "nt", 256, D, D_FF, epi_norm_bwd, [("tile", _F32), ("tile", _MXU), ("part8", _F32)],
        extras=[(dh2, "tile"), (h1, "tile"), (g_mlp, "row")])
    (dwout,), _ = _matmul("d_w_out", cat, dh1b, "tn", 1024, D, 2048, first, [("tile", _BF16)])
    (dcat,), _ = _matmul("d_cat", dh1b, w_out, "nt", 1024, 1024, D, first, [("tile", _MXU)])
    (dproj, dws, dbs, dgv, dgout, dng, dcw, dcb, ddtb, dalog, dd), (r_w2, r_w1, r_wout) = _mixer_bwd(
        proj, conv_act, conv_sig, dcat, yss, states, *mix_consts, e_mat, et_mat, ltri_mat, seq_chunks,
        comm=[(s_rows, dw2), (s_cols, dw1), (s_rows, dwout)])
    pieces = dict(gm_v_norm_g=dgv, gm_out_norm_g=dgout, ssd_norm_g=dng, norm_mlp_g=dgmlp, ple_norm_g=dgple,
                  final_norm_g=dgfin, gm_ws=dws, gm_bs=dbs, ssd_conv_w=dcw, ssd_conv_b=dcb, ssd_dt_bias=ddtb,
                  ssd_a_log=dalog, ssd_d=dd, loss=lossp)
    parts, segments, n_rows, where = _small_layout(pieces)
    small_block = _small_local(parts, segments, n_rows)
    (dwin_t,), (small_blocks,) = _matmul("d_w_in", n1, dproj, "tn", 512, D_IN_PAD, 1024, lambda acc: (acc.T,),
                                         [("tile_t", _BF16)], comm=[(("gather", 0), small_block)])
    dwin_blocks = dwin_t[:D_IN].reshape(N_DEV, SHARD_IN, D)
    (gx, dgmix), (r_win,) = _matmul(
        "d_x", dproj, w_in_t, "nn", 256, D, D_IN_PAD, lambda *a: epi_norm_bwd(*a)[1:],
        [("tile", _F32), ("part8", _F32)], extras=[(dh1, "tile"), (x, "tile"), (g_mix, "row")],
        comm=[(s_rows, dwin_blocks)])
    r_win = r_win.reshape(N_DEV, SHARD_IN, D)

    big = dict(w_in=r_win, w_out=r_wout, w_ff1=r_w1, w_ff2=r_w2, w_ple_gate=r_wg, w_ple_proj=r_wp)
    return gx, big, small_blocks, dgmix, n_rows, where


def _small_layout(pieces):
    rows, segments = [], []
    in_row, out_row = 0, 0

    def add(arr, kind, n_out):
        nonlocal in_row, out_row
        rows.append(arr)
        segments.append((out_row, n_out, in_row, arr.shape[0], kind))
        start = out_row
        in_row += arr.shape[0]
        out_row += n_out
        return start

    where = {_LATE: 0}
    out_row = 1
    for name in _G_VECS[1:]:
        where[name] = add(pieces[name], "sum", 1)
    where["gm_ws"] = add(pieces["gm_ws"].reshape(GM_H * CH * CH // D, D), "copy", GM_H * CH * CH // D)
    where["gm_bs"] = add(pieces["gm_bs"].reshape(1, D), "copy", 1)
    cb = jnp.pad(pieces["ssd_conv_b"], ((0, 0), (0, 2 * D - CONV_CH)))
    where["ssd_conv_b"] = add(cb[:, :D], "sum", 1)
    add(cb[:, D:], "sum", 1)
    cw = jnp.pad(pieces["ssd_conv_w"][:CONV_K], ((0, 0), (0, 2 * D - CONV_CH)))
    where["ssd_conv_w"] = add(cw.reshape(2 * CONV_K, D), "copy", 2 * CONV_K)
    misc = jnp.concatenate([pieces["ssd_dt_bias"], pieces["ssd_a_log"], pieces["ssd_d"],
                            jnp.zeros((8, D - 3 * 128), _F32)], axis=1)
    where["misc"] = add(misc, "sum", 1)
    where["loss"] = add(pieces["loss"], "loss", 1)
    n_rows = -(-out_row // 8) * 8
    return jnp.concatenate(rows, axis=0), tuple(segments), n_rows, where


def _pack_small_params(vals, where, n_rows, my_block):
    rows, at = [], {}

    def add(name, arr):
        at[name] = sum(r.shape[0] for r in rows)
        rows.append(arr)

    for name in _G_VECS:
        add(name, vals[name].reshape(1, D))
    add("gm_ws", vals["gm_ws"].reshape(GM_H * CH * CH // D, D))
    add("gm_bs", vals["gm_bs"].reshape(1, D))
    cb = jnp.pad(vals["ssd_conv_b"].reshape(1, CONV_CH), ((0, 0), (0, 2 * D - CONV_CH)))
    add("ssd_conv_b", cb.reshape(2, D))
    cw = lax.dynamic_update_slice(jnp.zeros((CONV_K, 2 * D), _F32), vals["ssd_conv_w"].reshape(CONV_K, -1),
                                  (0, my_block * (CONV_CH // N_DEV)))
    add("ssd_conv_w", cw.reshape(2 * CONV_K, D))
    misc = jnp.concatenate([_pad_lanes(vals["ssd_dt_bias"].reshape(1, SSD_H)),
                            _pad_lanes(vals["ssd_a_log"].reshape(1, SSD_H)),
                            _pad_lanes(vals["ssd_d"].reshape(1, SSD_H)), jnp.zeros((1, D - 3 * 128), _F32)], axis=1)
    add("misc", misc)
    assert all(where[k] == r for k, r in at.items()), (where, at)
    rows.append(jnp.zeros((n_rows - sum(r.shape[0] for r in rows), D), _F32))
    return jnp.concatenate(rows, axis=0)


def _unpack_small(buf, where, my_block, shapes):
    out = {}
    for name in _G_VECS:
        out[name] = buf[where[name]].reshape(shapes[name])
    n_ws = GM_H * CH * CH // D
    out["gm_ws"] = buf[where["gm_ws"]:where["gm_ws"] + n_ws].reshape(shapes["gm_ws"])
    out["gm_bs"] = buf[where["gm_bs"]].reshape(shapes["gm_bs"])
    r = where["ssd_conv_b"]
    out["ssd_conv_b"] = buf[r:r + 2].reshape(1, 2 * D)[:, :CONV_CH].reshape(shapes["ssd_conv_b"])
    r = where["ssd_conv_w"]
    cw = buf[r:r + 2 * CONV_K].reshape(CONV_K, 2 * D)
    out["ssd_conv_w"] = lax.dynamic_slice(cw, (0, my_block * (CONV_CH // N_DEV)),
                                          (CONV_K, CONV_CH // N_DEV)).reshape(shapes["ssd_conv_w"])
    misc = buf[where["misc"]]
    for i, name in enumerate(("ssd_dt_bias", "ssd_a_log", "ssd_d")):
        out[name] = misc[i * 128:i * 128 + SSD_H].reshape(shapes[name])
    return out


_WEIGHTS = ("norm_mix_g", "w_in", "gm_v_norm_g", "gm_ws", "gm_bs", "gm_out_norm_g", "ssd_conv_w", "ssd_conv_b",
            "ssd_dt_bias", "ssd_a_log", "ssd_d", "ssd_norm_g", "w_out", "norm_mlp_g", "w_ff1", "w_ff2", "ple_norm_g",
            "w_ple_gate", "w_ple_proj", "final_norm_g")


def kernel(x, p, norm_mix_g, w_in, gm_v_norm_g, gm_ws, gm_bs, gm_out_norm_g, ssd_conv_w, ssd_conv_b, ssd_dt_bias, ssd_a_log, ssd_d, ssd_norm_g, w_out, norm_mlp_g, w_ff1, w_ff2, ple_norm_g, w_ple_gate, w_ple_proj, final_norm_g, loss_target, m_norm_mix_g, m_w_in, m_gm_v_norm_g, m_gm_ws, m_gm_bs, m_gm_out_norm_g, m_ssd_conv_w, m_ssd_conv_b, m_ssd_dt_bias, m_ssd_a_log, m_ssd_d, m_ssd_norm_g, m_w_out, m_norm_mlp_g, m_w_ff1, m_w_ff2, m_ple_norm_g, m_w_ple_gate, m_w_ple_proj, m_final_norm_g, v_norm_mix_g, v_w_in, v_gm_v_norm_g, v_gm_ws, v_gm_bs, v_gm_out_norm_g, v_ssd_conv_w, v_ssd_conv_b, v_ssd_dt_bias, v_ssd_a_log, v_ssd_d, v_ssd_norm_g, v_w_out, v_norm_mlp_g, v_w_ff1, v_w_ff2, v_ple_norm_g, v_w_ple_gate, v_w_ple_proj, v_final_norm_g):
    args = dict(locals())
    w = {n: args[n] for n in _WEIGHTS}
    m = {n: args["m_" + n] for n in _WEIGHTS}
    v = {n: args["v_" + n] for n in _WEIGHTS}
    shapes = {n: w[n].shape for n in _WEIGHTS}
    my_block = 4 * lax.axis_index("x") + 2 * lax.axis_index("y") + lax.axis_index("c")
    nb, seq_len, _ = x.shape

    local = lambda d, n: d[n][0].T if n == "w_in" else d[n][0]
    shard = {n: local(w, n).astype(_MXU) for n in _BIG_NAMES}
    small = {n: w[n] for n in _WEIGHTS if n not in _BIG_NAMES}
    gx, recv, small_blocks, late8, n_rows, where = _local_step(
        x.reshape(nb * seq_len, D), p.reshape(nb * seq_len, D_PLE), loss_target.reshape(nb * seq_len, D), shard,
        ssd_conv_w[0], small, seq_len)

    big_out = [{}, {}, {}, {}]
    for n in _BIG_NAMES:
        res = _sum_adam("sum_adam_" + n, recv[n], local(w, n), local(m, n), local(v, n))
        for k in range(4):
            big_out[k][n] = (res[k].T if n == "w_in" else res[k]).reshape(shapes[n])

    packs = [_pack_small_params(d, where, n_rows, my_block) for d in (w, m, v)]
    small_res = _small_final(small_blocks, late8, where[_LATE], *packs)
    loss = small_res[0][where["loss"], 0]
    small_out = [_unpack_small(a, where, my_block, shapes) for a in small_res]

    outs = [loss, gx.reshape(x.shape)]
    for k in range(4):
        outs += [big_out[k][n] if n in _BIG_NAMES else small_out[k][n] for n in _WEIGHTS]
    return tuple(outs)
```

```python
import functools
import math

import jax
import jax.numpy as jnp
import numpy as np
from jax import lax
from jax.experimental import pallas as pl
from jax.experimental.pallas import tpu as pltpu

_F32 = jnp.float32
_BF16 = jnp.bfloat16
_MXU = jnp.bfloat16

D = 1024
D_PLE = 256
GM_W = 1024
GM_H = 8
CH = 128
SSD_W = 1024
SSD_H = 16
SSD_P = 64
SSD_G = 2
SSD_N = 128
CONV_K = 4
CONV_CH = SSD_W + 2 * SSD_G * SSD_N
D_FF = 4096
D_IN = 2 * GM_W + SSD_W + CONV_CH + SSD_H
D_IN_PAD = 4736
EPS = 1e-6
DT_BLK = (D_IN_PAD - 128) // 128
N_DEV = 8
SHARD_IN = D_IN // N_DEV

LR, B1, B2, ADAM_EPS, WD, STEP = 0.001, 0.9, 0.999, 1e-08, 0.01, 10
_LOG2E = math.log2(math.e)
_LOG2_INV_SQRT_2PI = -0.5 * math.log2(2.0 * math.pi)

_SEQ_PER_STEP = 2
_V7X_VMEM_BYTES = 64 * 1024 * 1024
_VMEM_CAP = _V7X_VMEM_BYTES - 8 * 1024 * 1024
_MESH = pl.DeviceIdType.MESH


def _vmem_limit(nbytes):
    return int(min(_VMEM_CAP, max(32 * 1024 * 1024, nbytes * 5 // 4 + (4 << 20))))


def _nbytes(shape, dtype):
    return int(np.prod(shape)) * jnp.dtype(dtype).itemsize


def _mx(v):
    return v.astype(_MXU)


def _dot(a, b):
    return jnp.dot(a, b, preferred_element_type=_F32)


def _dot_nt(a, b):
    return lax.dot_general(a, b, (((1,), (1,)), ((), ())), preferred_element_type=_F32)


def _dot_tn(a, b):
    return lax.dot_general(a, b, (((0,), (0,)), ((), ())), preferred_element_type=_F32)


def _split3(a):
    hi = a.astype(_BF16)
    r = a - hi.astype(_F32)
    mid = r.astype(_BF16)
    lo = (r - mid.astype(_F32)).astype(_BF16)
    return hi, mid, lo


def _xdot(dotfn, a, b01):
    b = b01.astype(_BF16)
    hi, mid, lo = _split3(a)
    return (dotfn(hi, b) + dotfn(mid, b)) + dotfn(lo, b)


def _xdot_left(dotfn, a01, b):
    a = a01.astype(_BF16)
    hi, mid, lo = _split3(b)
    return (dotfn(a, hi) + dotfn(a, mid)) + dotfn(a, lo)


def _sum8(v):
    r, n = v.shape
    return v.reshape(r // 8, 8, n).sum(axis=0)


def _sigmoid(v):
    return 1.0 / (1.0 + jnp.exp(-v))


def _rms(xv, g):
    ms = jnp.mean(xv * xv, axis=-1, keepdims=True)
    return xv * lax.rsqrt(ms + EPS) * g


def _rms_bwd(xv, g, dn):
    r = lax.rsqrt(jnp.mean(xv * xv, axis=-1, keepdims=True) + EPS)
    nh = xv * r
    gy = dn * g
    dx = (gy - nh * jnp.mean(gy * nh, axis=-1, keepdims=True)) * r
    return dx, _sum8(dn * nh)


def _iota2(shape, axis):
    return lax.broadcasted_iota(jnp.int32, shape, axis)


def _norm_cast(name, x, g, tm=512, comm=()):
    t, n = x.shape
    tm = min(tm, t)
    steps = t // tm
    kinds = [kind for kind, _ in comm]
    c_in, c_in_specs, c_out_specs, c_out_shape, c_scratch = _comm_io(comm)

    def body(*refs):
        x_ref, g_ref = refs[0], refs[1]
        o_ref = refs[2 + len(comm)]
        comm_refs = (kinds, refs[2:2 + len(comm)], refs[3 + len(comm):3 + 2 * len(comm)], *refs[3 + 2 * len(comm):])
        if comm:
            pl.when(pl.program_id(0) == 0)(lambda: _comm_start(*comm_refs))

        o_ref[...] = _rms(x_ref[...], g_ref[...]).astype(o_ref.dtype)
        if comm:
            pl.when(pl.program_id(0) == steps - 1)(lambda: _comm_finish(*comm_refs))

    res = pl.pallas_call(
        body, name=name, grid=(steps,),
        in_specs=[pl.BlockSpec((tm, n), lambda i: (i, 0)), pl.BlockSpec((1, n), lambda i: (0, 0))] + c_in_specs,
        out_specs=[pl.BlockSpec((tm, n), lambda i: (i, 0))] + c_out_specs,
        out_shape=[jax.ShapeDtypeStruct((t, n), _MXU)] + c_out_shape, scratch_shapes=c_scratch,
        compiler_params=pltpu.CompilerParams(dimension_semantics=("arbitrary",)),
    )(x, g, *c_in)
    return res[0], res[1:]


def _matmul(name, a, b, mode, tm, tn, tk, epilogue, outs, extras=(), comm=(), b_to_epilogue=False):
    m, k = a.shape[::-1] if mode == "tn" else a.shape
    n = b.shape[0] if mode == "nt" else b.shape[1]
    tm, tn, tk = min(tm, m), min(tn, n), min(tk, k)
    assert m % tm == 0 and n % tn == 0 and k % tk == 0, (name, m, n, k, tm, tn, tk)
    if mode == "nn":
        a_spec = pl.BlockSpec((tm, tk), lambda i, j, kk: (i, kk))
        b_spec = pl.BlockSpec((tk, tn), lambda i, j, kk: (kk, j))
        dotfn = _dot
    elif mode == "nt":
        a_spec = pl.BlockSpec((tm, tk), lambda i, j, kk: (i, kk))
        b_spec = pl.BlockSpec((tn, tk), lambda i, j, kk: (j, kk))
        dotfn = _dot_nt
    else:
        a_spec = pl.BlockSpec((tk, tm), lambda i, j, kk: (kk, i))
        b_spec = pl.BlockSpec((tk, tn), lambda i, j, kk: (kk, j))
        dotfn = _dot_tn
    ni, nj, nk = m // tm, n // tn, k // tk
    n_ex, n_out, n_comm = len(extras), len(outs), len(comm)
    kinds = [kind for kind, _ in comm]
    c_in, c_in_specs, c_out_specs, c_out_shape, c_scratch = _comm_io(comm)

    in_specs, vmem = [a_spec, b_spec], 2 * (tm * tk * a.dtype.itemsize + tk * tn * b.dtype.itemsize)
    for arr, kind in extras:
        if kind == "tile":
            in_specs.append(pl.BlockSpec((tm, tn), lambda i, j, kk: (i, j)))
            vmem += 2 * _nbytes((tm, tn), arr.dtype)
        elif kind == "rows":
            in_specs.append(pl.BlockSpec((tm, arr.shape[1]), lambda i, j, kk: (i, 0)))
            vmem += 2 * _nbytes((tm, arr.shape[1]), arr.dtype)
        elif kind == "full":
            in_specs.append(pl.BlockSpec(arr.shape, lambda i, j, kk: (0,) * arr.ndim))
            vmem += 2 * _nbytes(arr.shape, arr.dtype)
        else:
            in_specs.append(pl.BlockSpec((1, tn), lambda i, j, kk: (0, j)))
    out_specs, out_shape = [], []
    for kind, dt in outs:
        if kind == "tile":
            out_specs.append(pl.BlockSpec((tm, tn), lambda i, j, kk: (i, j)))
            out_shape.append(jax.ShapeDtypeStruct((m, n), dt))
            vmem += 2 * _nbytes((tm, tn), dt)
        elif kind == "tile_t":
            out_specs.append(pl.BlockSpec((tn, tm), lambda i, j, kk: (j, i)))
            out_shape.append(jax.ShapeDtypeStruct((n, m), dt))
            vmem += 2 * _nbytes((tm, tn), dt)
        else:
            assert nj == 1, "the partial-sum rows are accumulated over consecutive row tiles"
            out_specs.append(pl.BlockSpec((8, tn), lambda i, j, kk: (0, 0)))
            out_shape.append(jax.ShapeDtypeStruct((8, n), dt))
    scratch = [pltpu.VMEM((tm, tn), _F32)] if nk > 1 else []
    vmem += _nbytes((tm, tn), _F32) * 2

    def body(*refs):
        a_ref, b_ref = refs[0], refs[1]
        ex_refs = refs[2:2 + n_ex]
        n_in = 2 + n_ex + n_comm
        out_refs = refs[n_in:n_in + n_out]
        i, j, kk = pl.program_id(0), pl.program_id(1), pl.program_id(2)
        comm_refs = (kinds, refs[2 + n_ex:n_in], refs[n_in + n_out:n_in + n_out + n_comm], *refs[len(refs) - 3:])
        if n_comm:
            pl.when((i == 0) & (j == 0) & (kk == 0))(lambda: _comm_start(*comm_refs))

        b_val = _mx(b_ref[...])
        part = dotfn(_mx(a_ref[...]), b_val)

        def finish(acc):
            vals = epilogue(acc, *([b_val] if b_to_epilogue else []), *[r[...] for r in ex_refs])
            for r, v, (kind, _) in zip(out_refs, vals, outs):
                if kind == "part8":
                    @pl.when(i == 0)
                    def _():
                        r[...] = v

                    @pl.when(i > 0)
                    def _():
                        r[...] += v
                else:
                    r[...] = v.astype(r.dtype)

        if nk == 1:
            finish(part)
        else:
            acc_ref = refs[n_in + n_out + n_comm]

            @pl.when(kk == 0)
            def _():
                acc_ref[...] = part

            @pl.when(kk > 0)
            def _():
                acc_ref[...] += part

            @pl.when(kk == nk - 1)
            def _():
                finish(acc_ref[...])

        if n_comm:
            pl.when((i == ni - 1) & (j == nj - 1) & (kk == nk - 1))(lambda: _comm_finish(*comm_refs))

    carried =n_comm or any(kind == "part8" for kind, _ in outs)
    sem = ("arbitrary",) * 3 if carried else ("parallel", "parallel", "arbitrary")
    res = pl.pallas_call(
        body, name=name, grid=(ni, nj, nk),
        in_specs=in_specs + c_in_specs, out_specs=out_specs + c_out_specs, out_shape=out_shape + c_out_shape,
        scratch_shapes=scratch + c_scratch,
        compiler_params=pltpu.CompilerParams(dimension_semantics=sem, vmem_limit_bytes=_vmem_limit(vmem)),
    )(a, b, *[arr for arr, _ in extras], *c_in)
    return res[:n_out], res[n_out:]


def _shift_down(v, halo8, j):
    if j == 0:
        return v
    r = pltpu.roll(v, j, axis=0)
    hr = pltpu.roll(halo8, j, axis=0)
    top = jnp.where(_iota2(hr.shape, 0) < j, hr, r[:8])
    return jnp.concatenate([top, r[8:]], axis=0)


def _shift_up(v, next8, j):
    if j == 0:
        return v
    rows = v.shape[0]
    r = pltpu.roll(v, rows - j, axis=0)
    nr = pltpu.roll(next8, 8 - j, axis=0)
    bot = jnp.where(_iota2(nr.shape, 0) >= 8 - j, nr, r[rows - 8:])
    return jnp.concatenate([r[:rows - 8], bot], axis=0)


def _silu_grad(sig, silu):
    return sig + silu * (1.0 - sig)


def _gmlp_fwd_vals(pu, pv, gv, ws_ref, bsb_ref, want_bwd):
    tril = _iota2((CH, CH), 0) >= _iota2((CH, CH), 1)
    cdf_u = 0.5 * (1.0 + lax.erf(pu * 0.7071067811865476))
    cdf_v = 0.5 * (1.0 + lax.erf(pv * 0.7071067811865476))
    u = pu * cdf_u
    v = pv * cdf_v
    ys, keep = [], [(cdf_u, cdf_v)] if want_bwd else []
    for h in range(GM_H):
        sl = slice(h * 128, (h + 1) * 128)
        vh = v[:, sl]
        r = lax.rsqrt(jnp.mean(vh * vh, axis=-1, keepdims=True) + EPS)
        vn = vh * r * gv[:, sl]
        wm = _mx(jnp.where(tril, ws_ref[h], 0.0))
        mixed = _dot(wm, _mx(vn)) + bsb_ref[h]
        ys.append(u[:, sl] * mixed)
        if want_bwd:
            keep.append((vh, r, vn, wm, mixed))
    return jnp.concatenate(ys, axis=1), u, keep


def _ssd_conv(xbc, halo8, cw_ref, cb):
    cpre = cb + sum(cw_ref[k:k + 1, :] * _shift_down(xbc, halo8, CONV_K - 1 - k) for k in range(CONV_K))
    sig = _sigmoid(cpre)
    return sig, cpre * sig


def _ssd_decay(dtraw, dtb, alog, e_ref, ltri):
    dtin = dtraw + dtb
    dt = jnp.maximum(dtin, 0.0) + jnp.log(1.0 + jnp.exp(-jnp.abs(dtin)))
    a_neg = -jnp.exp(alog)
    cs = _xdot_left(_dot, ltri, dt * a_neg)
    cs_last = cs[CH - 1:CH, :]
    ecs = jnp.exp(cs)
    dec = jnp.exp(cs_last - cs)
    cdec = jnp.exp(cs_last)
    e = e_ref[...]
    dt_x = _dot(dt.astype(_BF16), e)
    ecs_x = _dot(ecs.astype(_BF16), e)
    dec_x = _dot(dec.astype(_BF16), e)
    cdec_x = _xdot(_dot, jnp.broadcast_to(cdec, (8, 128)), e)[0:1, :]
    return dict(dtin=dtin, dt=dt, a_neg=a_neg, cs=cs, ecs=ecs, dec=dec, cdec=cdec,
                dt_x=dt_x, ecs_x=ecs_x, dec_x=dec_x, cdec_x=cdec_x)


def _head_lm(cs, cst_ref, h, tril):
    seg = jnp.broadcast_to(cs[:, h:h + 1], (CH, CH)) - cst_ref[h:h + 1, :]
    return jnp.exp(jnp.where(tril, seg, -jnp.inf))


def _mixer_fwd(proj, gv, ws, bsb, gout, cw8, cb, dtb, alog, d_x, ng, e_mat, ltri_mat, seq_chunks, comm=()):
    t = proj.shape[0]
    n_seq = t // (seq_chunks * CH)
    sb = math.gcd(_SEQ_PER_STEP, n_seq)
    n_groups = n_seq // sb
    proj = proj.reshape(n_seq, seq_chunks * CH, proj.shape[1])
    n_comm = len(comm)
    kinds = [kind for kind, _ in comm]
    c_in, c_in_specs, c_out_specs, c_out_shape, c_scratch = _comm_io(comm)

    def body(*refs):
        comm_refs = (kinds, refs[18:18 + n_comm], refs[23 + n_comm:23 + 2 * n_comm], *refs[25 + 2 * n_comm:])
        grp, c = pl.program_id(0), pl.program_id(1)
        if n_comm:
            pl.when((grp == 0) & (c == 0))(lambda: _comm_start(*comm_refs))
            pl.when((grp == n_groups - 1) & (c == seq_chunks - 1))(lambda: _comm_finish(*comm_refs))
        for s in range(sb):
            per_seq = lambda rs: [r.at[s] for r in rs]
            one_chunk(c == 0, *per_seq(refs[:6]), *refs[6:18], *per_seq(refs[18 + n_comm:23 + n_comm]),
                      *per_seq(refs[23 + 2 * n_comm:25 + 2 * n_comm]))

    def one_chunk(first, pu_ref, pv_ref, z_ref, xbc_ref, dt_ref, halo_ref, gv_ref, ws_ref, bsb_ref, gout_ref, cw_ref,
                  cb_ref, dtb_ref, alog_ref, dx_ref, ng_ref, e_ref, ltri_ref, cat_ref, y_ref, st_ref, act_ref, sig_ref,
                  s_ref, cst_ref):
        tril = _iota2((CH, CH), 0) >= _iota2((CH, CH), 1)
        lane = _iota2((CH, 128), 1)

        y_a, _, _ = _gmlp_fwd_vals(pu_ref[...], pv_ref[...], gv_ref[...], ws_ref, bsb_ref, False)
        cat_ref[:, 0:GM_W] = _rms(y_a, gout_ref[...]).astype(cat_ref.dtype)

        @pl.when(first)
        def _():
            s_ref[...] = jnp.zeros_like(s_ref)

        halo8 = jnp.where(first, 0.0, halo_ref[...])
        sig, act = _ssd_conv(xbc_ref[...], halo8, cw_ref, cb_ref[...])
        sig_ref[...] = sig
        act_ref[...] = act
        q = _ssd_decay(dt_ref[...], dtb_ref[...], alog_ref[...], e_ref, ltri_ref[...])
        xv = act[:, 0:SSD_W]
        xdt = xv * q["dt_x"]
        xdt_m = _mx(xdt)
        cs = q["cs"]
        cst_ref[...] = cs.T
        s_prev = s_ref[...]
        st_ref[...] = s_prev
        ys = []
        for g in range(SSD_G):
            bg = _mx(act[:, SSD_W + g * SSD_N:SSD_W + (g + 1) * SSD_N])
            cg = _mx(act[:, SSD_W + SSD_G * SSD_N + g * SSD_N:SSD_W + SSD_G * SSD_N + (g + 1) * SSD_N])
            cbm = _dot_nt(cg, bg)
            gs = slice(g * 512, (g + 1) * 512)
            for pr in range(4):
                ps = slice(g * 512 + pr * 128, g * 512 + (pr + 1) * 128)
                o = []
                for hh in range(2):
                    h = g * 8 + pr * 2 + hh
                    m_h = _mx(cbm * _head_lm(cs, cst_ref, h, tril))
                    o.append(_dot(m_h, xdt_m[:, ps]))
                ys.append(jnp.where(lane < SSD_P, o[0], o[1]))
            sg = s_prev[:, gs]
            yoff = _dot(cg, _mx(sg)) * q["ecs_x"][:, gs]
            ys[-4:] = [ys[-4 + i] + yoff[:, i * 128:(i + 1) * 128] for i in range(4)]
            st_new = _dot_tn(bg, _mx(q["dec_x"][:, gs] * xdt[:, gs]))
            s_ref[:, gs] = sg * q["cdec_x"][:, gs] + st_new
        y = jnp.concatenate(ys, axis=1) + dx_ref[...] * xv
        y_ref[...] = y
        zv = z_ref[...]
        yg = y * (zv * _sigmoid(zv))
        for g in range(SSD_G):
            gs = slice(g * 512, (g + 1) * 512)
            cat_ref[:, GM_W + g * 512:GM_W + (g + 1) * 512] = _rms(yg[:, gs], ng_ref[:, gs]).astype(cat_ref.dtype)

    blk = lambda w, j: pl.BlockSpec((sb, CH, w), lambda g, c: (g, c, j))
    full = lambda arr: pl.BlockSpec(arr.shape, lambda g, c: (0,) * arr.ndim)
    consts = [gv, ws, bsb, gout, cw8, cb, dtb, alog, d_x, ng, e_mat, ltri_mat]
    seq = seq_chunks * CH
    res = pl.pallas_call(
        body, name="mixer_fwd", grid=(n_groups, seq_chunks),
        in_specs=[blk(GM_W, 0), blk(GM_W, 1), blk(SSD_W, 2), blk(CONV_CH, 2), blk(128, DT_BLK),
                  pl.BlockSpec((sb, 8, CONV_CH), lambda g, c: (g, jnp.maximum(c * (CH // 8) - 1, 0), 2))]
        + [full(a) for a in consts] + c_in_specs,
        out_specs=[blk(2 * D, 0), blk(SSD_W, 0), blk(SSD_W, 0), blk(CONV_CH, 0), blk(CONV_CH, 0)] + c_out_specs,
        out_shape=[jax.ShapeDtypeStruct((n_seq, seq, 2 * D), _MXU), jax.ShapeDtypeStruct((n_seq, seq, SSD_W), _F32),
                   jax.ShapeDtypeStruct((n_seq, seq, SSD_W), _F32), jax.ShapeDtypeStruct((n_seq, seq, CONV_CH), _F32),
                   jax.ShapeDtypeStruct((n_seq, seq, CONV_CH), _F32)] + c_out_shape,
        scratch_shapes=[pltpu.VMEM((sb, SSD_N, SSD_W), _F32), pltpu.VMEM((sb, 128, CH), _F32)] + c_scratch,
        compiler_params=pltpu.CompilerParams(dimension_semantics=("arbitrary", "arbitrary"),
                                             vmem_limit_bytes=48 << 20),
    )(proj, proj, proj, proj, proj, proj, *consts, *c_in)
    return [r.reshape(t, r.shape[-1]) for r in res[:5]], res[5:]


def _mixer_bwd(proj, act, sig, dcat, yss, states, gv, ws, bsb, gout, cw8, cb, dtb, alog, d_x, ng, e_mat, et_mat,
               ltri_mat, seq_chunks, comm=()):
    t = proj.shape[0]
    n_seq = t // (seq_chunks * CH)
    sb = math.gcd(_SEQ_PER_STEP, n_seq)
    n_groups = n_seq // sb
    seq = seq_chunks * CH
    proj, act, sig, dcat, yss, states = [a.reshape(n_seq, seq, a.shape[1])
                                         for a in (proj, act, sig, dcat, yss, states)]
    n_comm = len(comm)
    kinds = [kind for kind, _ in comm]
    c_in, c_in_specs, c_out_specs, c_out_shape, c_scratch = _comm_io(comm)

    def body(*refs):
        o0, s0 = 23 + n_comm, 34 + 2 * n_comm
        acc_refs, shared = refs[o0 + 1:o0 + 11], refs[s0 + 4:s0 + 6]
        comm_refs = (kinds, refs[23:23 + n_comm], refs[o0 + 11:o0 + 11 + n_comm], *refs[s0 + 6:])
        grp, i = pl.program_id(0), pl.program_id(1)
        if n_comm:
            pl.when((grp == 0) & (i == 0))(lambda: _comm_start(*comm_refs))

        @pl.when((grp == 0) & (i == 0))
        def _():
            for r in (*acc_refs, *shared, refs[s0 + 2]):
                r[...] = jnp.zeros_like(r)

        @pl.when(i == 0)
        def _():
            refs[s0][...] = jnp.zeros_like(refs[s0])
            refs[s0 + 1][...] = jnp.zeros_like(refs[s0 + 1])

        for s in range(sb):
            per_seq = lambda rs: [r.at[s] for r in rs]
            last_of_all = ((grp == n_groups - 1) & (i == seq_chunks - 1)) if s == sb - 1 else None
            one_chunk(last_of_all, *per_seq(refs[:10]), *refs[10:23], refs[o0].at[s], *acc_refs,
                      *per_seq(refs[s0:s0 + 4]), *shared)
        if n_comm:
            pl.when((grp == n_groups - 1) & (i == seq_chunks - 1))(lambda: _comm_finish(*comm_refs))

    def one_chunk(finalize, pu_ref, pv_ref, z_ref, xbc_ref, dt_ref, act_ref, sig_ref, dcat_ref, y_ref, st_ref,
                  gv_ref, ws_ref, bsb_ref, gout_ref, cw_ref, cb_ref, dtb_ref, alog_ref, dx_ref, ng_ref,
                  e_ref, et_ref, ltri_ref,
                  dproj_ref, dws_ref, dbs_ref, dgv_ref, dgout_ref, dng_ref, dcw_ref, dcb_ref, ddtb_ref, dalog_ref,
                  dd_ref, ds_ref, dnext_ref, dcst_ref, cst_ref, dbacc_ref, ddacc_ref):
        tril = _iota2((CH, CH), 0) >= _iota2((CH, CH), 1)
        lane = _iota2((CH, 128), 1)
        row = _iota2((CH, 128), 0)
        dcat_v = dcat_ref[...].astype(_F32)

        pu, pv = pu_ref[...], pv_ref[...]
        gv_v = gv_ref[...]
        y_a, u, keep = _gmlp_fwd_vals(pu, pv, gv_v, ws_ref, bsb_ref, True)
        dy, dgout8 = _rms_bwd(y_a, gout_ref[...], dcat_v[:, 0:GM_W])
        dgout_ref[...] += dgout8
        dus, dvs, dgvs = [], [], []
        for h in range(GM_H):
            sl = slice(h * 128, (h + 1) * 128)
            vh, r, vn, wm, mixed = keep[h + 1]
            dyh = dy[:, sl]
            dus.append(dyh * mixed)
            dmix = dyh * u[:, sl]
            dmix_m = _mx(dmix)
            dws_ref[h] += jnp.where(tril, _dot_nt(dmix_m, _mx(vn)), 0.0)
            dbacc_ref[h] += dmix
            dvn = _dot_tn(wm, dmix_m)
            gy = dvn * gv_v[:, sl]
            nh = vh * r
            dvs.append((gy - nh * jnp.mean(gy * nh, axis=-1, keepdims=True)) * r)
            dgvs.append(_sum8(dvn * nh))
        dgv_ref[...] += jnp.concatenate(dgvs, axis=1)
        cdf_u, cdf_v = keep[0]
        gelu_grad = lambda pre, cdf: cdf + pre * jnp.exp2(pre * pre * (-0.5 * _LOG2E) + _LOG2_INV_SQRT_2PI)
        dproj_ref[:, 0:GM_W] = (jnp.concatenate(dus, axis=1) * gelu_grad(pu, cdf_u)).astype(dproj_ref.dtype)
        dproj_ref[:, GM_W:2 * GM_W] = (jnp.concatenate(dvs, axis=1) * gelu_grad(pv, cdf_v)).astype(dproj_ref.dtype)

        q = _ssd_decay(dt_ref[...], dtb_ref[...], alog_ref[...], e_ref, ltri_ref[...])
        act = act_ref[...]
        xv = act[:, 0:SSD_W]
        dt_x, ecs_x, dec_x, cdec_x = q["dt_x"], q["ecs_x"], q["dec_x"], q["cdec_x"]
        xdt = xv * dt_x
        xdt_m = _mx(xdt)
        cs = q["cs"]
        cst_ref[...] = cs.T
        s_prev = st_ref[...]
        ds = ds_ref[...]
        yv = y_ref[...]
        zv = z_ref[...]
        sig_z = _sigmoid(zv)
        sz = zv * sig_z
        yg = yv * sz
        dygs, dng8 = [], []
        for g in range(SSD_G):
            gs = slice(g * 512, (g + 1) * 512)
            a_, b_ = _rms_bwd(yg[:, gs], ng_ref[:, gs], dcat_v[:, GM_W + g * 512:GM_W + (g + 1) * 512])
            dygs.append(a_)
            dng8.append(b_)
        dyg = jnp.concatenate(dygs, axis=1)
        dng_ref[...] += jnp.concatenate(dng8, axis=1)
        dyv = dyg * sz
        dproj_ref[:, 2 * GM_W:2 * GM_W + SSD_W] = (dyg * yv * _silu_grad(sig_z, sz)).astype(dproj_ref.dtype)
        ddacc_ref[...] += _sum8(dyv * xv)
        dyv_m = _mx(dyv)

        dxdt_parts, db_parts, dc_parts = [], [], []
        dcs = jnp.zeros((CH, 128), _F32)
        dcs_x_parts, ddec_x_parts, dcl_x_parts = [], [], []
        for g in range(SSD_G):
            gs = slice(g * 512, (g + 1) * 512)
            bg = _mx(act[:, SSD_W + g * SSD_N:SSD_W + (g + 1) * SSD_N])
            cg = _mx(act[:, SSD_W + SSD_G * SSD_N + g * SSD_N:SSD_W + SSD_G * SSD_N + (g + 1) * SSD_N])
            cbm = _dot_nt(cg, bg)
            sg = s_prev[:, gs]
            sg_m = _mx(sg)
            dsg = ds[:, gs]
            dsg_m = _mx(dsg)
            zoff = _dot(cg, sg_m)
            dz_off = dyv[:, gs] * ecs_x[:, gs]
            dz_off_m = _mx(dz_off)
            dcs_x_parts.append(dyv[:, gs] * zoff * ecs_x[:, gs])
            dcg = _dot_nt(dz_off_m, sg_m)
            dsprev = _dot_tn(cg, dz_off_m)
            w_st = dec_x[:, gs] * xdt[:, gs]
            dw_st = _dot(bg, dsg_m)
            dbg = _dot_nt(_mx(w_st), dsg_m)
            dxdt_g = dec_x[:, gs] * dw_st
            ddec_x_parts.append(dw_st * xdt[:, gs])
            dsprev = dsprev + cdec_x[:, gs] * dsg
            dcl_x_parts.append(jnp.sum(dsg * sg, axis=0, keepdims=True) * cdec_x[:, gs])
            ds_ref[:, gs] = dsprev
            dcb = jnp.zeros((CH, CH), _F32)
            dxdt_pairs = []
            for pr in range(4):
                ps = slice(g * 512 + pr * 128, g * 512 + (pr + 1) * 128)
                acc_pair = None
                for hh in range(2):
                    h = g * 8 + pr * 2 + hh
                    in_head = (lane < SSD_P) if hh == 0 else (lane >= SSD_P)
                    lm = _head_lm(cs, cst_ref, h, tril)
                    m_h = cbm * lm
                    m_hm = _mx(m_h)
                    dyh_m = _mx(jnp.where(in_head, dyv[:, ps], 0.0))
                    dm = _dot_nt(dyh_m, xdt_m[:, ps])
                    dcb = dcb + dm * lm
                    qm = dm * m_h
                    dcs = dcs + jnp.where(lane == h, jnp.sum(qm, axis=1, keepdims=True), 0.0)
                    dcst_ref[h:h + 1, :] = jnp.sum(qm, axis=0, keepdims=True)
                    contrib = jnp.where(in_head, _dot_tn(m_hm, dyv_m[:, ps]), 0.0)
                    acc_pair = contrib if acc_pair is None else acc_pair + contrib
                dxdt_pairs.append(acc_pair)
            dxdt_parts.append(dxdt_g + jnp.concatenate(dxdt_pairs, axis=1))
            dcb_m = _mx(dcb)
            dc_parts.append(dcg + _dot(dcb_m, bg))
            db_parts.append(dbg + _dot_tn(dcb_m, cg))
        dxdt = jnp.concatenate(dxdt_parts, axis=1)
        dxv = dx_ref[...] * dyv + dxdt * dt_x
        et = et_ref[...]
        head_sum = lambda v: _dot(v.astype(_BF16), et)
        ddt = head_sum(dxdt * xv)
        dcs = dcs - dcst_ref[...].T + head_sum(jnp.concatenate(dcs_x_parts, axis=1))
        ddec = head_sum(jnp.concatenate(ddec_x_parts, axis=1)) * q["dec"]
        dcs = dcs - ddec
        dcl = jnp.sum(ddec, axis=0, keepdims=True) + _xdot(
            _dot, jnp.broadcast_to(jnp.concatenate(dcl_x_parts, axis=1), (8, SSD_W)), et)[0:1, :]
        dcs = jnp.where(row == CH - 1, dcs + dcl, dcs)
        da = _xdot_left(_dot_tn, ltri_ref[...], dcs)
        ddt = ddt + da * q["a_neg"]
        dalog_ref[...] += _sum8(da * q["dt"] * q["a_neg"])
        ddtraw = jnp.where(lane < SSD_H, ddt * _sigmoid(q["dtin"]), 0.0)
        ddtb_ref[...] += _sum8(ddtraw)
        dproj_ref[:, D_IN_PAD - 128:D_IN_PAD] = ddtraw.astype(dproj_ref.dtype)
        dcpre = jnp.concatenate([dxv] + db_parts + dc_parts, axis=1) * _silu_grad(sig_ref[...], act)
        dcb_ref[...] += _sum8(dcpre)
        next8 = dnext_ref[...]
        ups = [_shift_up(dcpre, next8, j) for j in range(CONV_K)]
        xbc = xbc_ref[...]
        for k in range(CONV_K):
            dcw_ref[k:k + 1, :] += jnp.sum(xbc * ups[CONV_K - 1 - k], axis=0, keepdims=True)
        dxbc = sum(cw_ref[k:k + 1, :] * ups[CONV_K - 1 - k] for k in range(CONV_K))
        dproj_ref[:, 2 * GM_W + SSD_W:2 * GM_W + SSD_W + CONV_CH] = dxbc.astype(dproj_ref.dtype)
        dnext_ref[...] = dcpre[0:8, :]

        if finalize is not None:
            @pl.when(finalize)
            def _():
                for h in range(GM_H):
                    dbs_ref[h:h + 1, :] = _xdot_left(_dot_nt, jnp.ones((8, 128), _BF16), dbacc_ref[h])[0:1, :]
                dd_ref[...] = _xdot(_dot, ddacc_ref[...], et)

    rblk = lambda w, j: pl.BlockSpec((sb, CH, w), lambda g, i: (g, seq_chunks - 1 - i, j))
    full = lambda arr: pl.BlockSpec(arr.shape, lambda g, i: (0,) * arr.ndim)
    acc = lambda shape: pl.BlockSpec(shape, lambda g, i: (0,) * len(shape))
    consts = [gv, ws, bsb, gout, cw8, cb, dtb, alog, d_x, ng, e_mat, et_mat, ltri_mat]
    acc_shapes = [(GM_H, CH, CH), (8, 128), (8, GM_W), (8, GM_W), (8, SSD_W), (8, CONV_CH), (8, CONV_CH), (8, 128),
                  (8, 128), (8, 128)]
    res = pl.pallas_call(
        body, name="mixer_bwd", grid=(n_groups, seq_chunks),
        in_specs=[rblk(GM_W, 0), rblk(GM_W, 1), rblk(SSD_W, 2), rblk(CONV_CH, 2), rblk(128, DT_BLK),
                  rblk(CONV_CH, 0), rblk(CONV_CH, 0),
                  rblk(2 * D, 0), rblk(SSD_W, 0), rblk(SSD_W, 0)] + [full(a) for a in consts] + c_in_specs,
        out_specs=[rblk(D_IN_PAD, 0)] + [acc(s) for s in acc_shapes] + c_out_specs,
        out_shape=[jax.ShapeDtypeStruct((n_seq, seq, D_IN_PAD), _MXU)]
        + [jax.ShapeDtypeStruct(s, _F32) for s in acc_shapes] + c_out_shape,
        scratch_shapes=[pltpu.VMEM((sb, SSD_N, SSD_W), _F32), pltpu.VMEM((sb, 8, CONV_CH), _F32),
                        pltpu.VMEM((sb, 128, CH), _F32), pltpu.VMEM((sb, 128, CH), _F32),
                        pltpu.VMEM((GM_H, CH, 128), _F32), pltpu.VMEM((8, SSD_W), _F32)] + c_scratch,
        compiler_params=pltpu.CompilerParams(dimension_semantics=("arbitrary", "arbitrary"),
                                             vmem_limit_bytes=48 << 20),
    )(proj, proj, proj, proj, proj, act, sig, dcat, yss, states, *consts, *c_in)
    return [res[0].reshape(t, D_IN_PAD)] + list(res[1:11]), res[11:]


def _peers():
    x, y, c = lax.axis_index("x"), lax.axis_index("y"), lax.axis_index("c")
    out = []
    for k in range(1, N_DEV):
        fx, fy, fc = (k >> 2) & 1, (k >> 1) & 1, k & 1
        px, py, pc = (x + fx) % 2, (y + fy) % 2, (c + fc) % 2
        out.append((k - 1, (px, py, pc), 4 * px + 2 * py + pc))
    return out, 4 * x + 2 * y + c


def _comm_io(comm):
    any_spec = pl.BlockSpec(memory_space=pl.ANY)
    n = len(comm)
    out_shape = []
    for (kind, axis), src in comm:
        shp = list(src.shape)
        if kind in ("gather", "gather2"):
            shp[axis] *= N_DEV
        else:
            shp[axis] //= N_DEV
            shp = [N_DEV] + shp
        out_shape.append(jax.ShapeDtypeStruct(tuple(shp), src.dtype))
    scratch = [pltpu.SemaphoreType.DMA((n * (N_DEV - 1),)), pltpu.SemaphoreType.DMA((n * (N_DEV - 1),)),
               pltpu.SemaphoreType.DMA((n,))] if n else []
    return [src for _, src in comm], [any_spec] * n, [any_spec] * n, out_shape, scratch


def _window(ref, axis, idx, size):
    start = pl.multiple_of(idx * size, size)
    return ref.at[tuple(pl.ds(start, size) if a == axis else slice(None) for a in range(len(ref.shape)))]


def _comm_plans(kinds, src_refs, dst_refs, send_sems, recv_sems, local_sems):
    x, y, c = lax.axis_index("x"), lax.axis_index("y"), lax.axis_index("c")
    peers, me = _peers()
    plans = []
    for s, ((kind, axis), src, dst) in enumerate(zip(kinds, src_refs, dst_refs)):
        sems = lambda k: dict(send_sem=send_sems.at[s * (N_DEV - 1) + k], recv_sem=recv_sems.at[s * (N_DEV - 1) + k])
        remote = lambda src_ref, dst_ref, k, pid: pltpu.make_async_remote_copy(
            src_ref=src_ref, dst_ref=dst_ref, device_id=pid, device_id_type=_MESH, **sems(k))
        if kind == "gather2":
            size = src.shape[axis]
            win = lambda idx: _window(dst, axis, idx, size)
            sib, sib_idx = (x, y, 1 - c), 4 * x + 2 * y + (1 - c)
            local = pltpu.make_async_copy(src, win(me), local_sems.at[s])
            to_sib = remote(src, win(me), 0, sib)
            starts, forwards = [local, to_sib], []
            waits = [(local, "local"), (to_sib, "send"), (remote(src, win(sib_idx), 0, sib), "recv")]
            for j, (fx, fy) in enumerate(((1, 0), (0, 1), (1, 1))):
                px, py = (x + fx) % 2, (y + fy) % 2
                same, other = 4 * px + 2 * py + c, 4 * px + 2 * py + (1 - c)
                out = remote(src, win(me), 1 + j, (px, py, c))
                starts.append(out)
                passed = remote(win(same), win(same), 4 + j, sib)
                forwards.append((remote(src, win(same), 1 + j, (px, py, c)), passed))
                waits += [(out, "send"), (passed, "send"), (remote(win(other), win(other), 4 + j, sib), "recv")]
            plans.append((starts, forwards, waits))
            continue
        if kind == "gather":
            size = src.shape[axis]
            src_for = lambda pidx: src
            dst_mine = _window(dst, axis, me, size)
        else:
            size = src.shape[axis] // N_DEV
            src_for = lambda pidx: _window(src, axis, pidx, size)
            dst_mine = dst.at[me]
        local = pltpu.make_async_copy(src_for(me), dst_mine, local_sems.at[s])
        remotes = [remote(src_for(pidx), dst_mine, k, pid) for k, pid, pidx in peers]
        plans.append(([local] + remotes, [], [(local, "local")] + [(cp, "both") for cp in remotes]))
    return plans


def _comm_start(*refs):
    for starts, _, _ in _comm_plans(*refs):
        for cp in starts:
            cp.start()


def _comm_finish(*refs):
    for _, forwards, waits in _comm_plans(*refs):
        for arrival, cp in forwards:
            arrival.wait_recv()
            cp.start()
        for cp, what in waits:
            if what == "send":
                cp.wait_send()
            elif what == "recv":
                cp.wait_recv()
            else:
                cp.wait()


def _adam_vals(w, g, m, v):
    m = B1 * m + (1.0 - B1) * g
    v = B2 * v + (1.0 - B2) * (g * g)
    m_hat = m / (1.0 - B1 ** STEP)
    v_hat = v / (1.0 - B2 ** STEP)
    delta = -LR * (m_hat / (jnp.sqrt(v_hat) + ADAM_EPS) + WD * w)
    return delta, m, v


def _sum_adam(name, recv, w, m, v, tile=256):
    _, r, wd = recv.shape
    if r % min(tile, r) == 0:
        tr, tc = min(tile, r), wd
    else:
        tr, tc = r, tile
        assert wd % tc == 0, (name, r, wd)

    def body(recv_ref, w_ref, m_ref, v_ref, g_out, d_out, m_out, v_out):
        g = recv_ref[0].astype(_F32)
        for s in range(1, N_DEV):
            g = g + recv_ref[s].astype(_F32)
        d_, m_, v_ = _adam_vals(w_ref[...], g, m_ref[...], v_ref[...])
        g_out[...] = g
        d_out[...] = d_
        m_out[...] = m_
        v_out[...] = v_

    spec = pl.BlockSpec((tr, tc), lambda i, j: (i, j))
    return pl.pallas_call(
        body, name=name, grid=(r // tr, wd // tc),
        in_specs=[pl.BlockSpec((N_DEV, tr, tc), lambda i, j: (0, i, j)), spec, spec, spec],
        out_specs=[spec] * 4, out_shape=[jax.ShapeDtypeStruct((r, wd), _F32)] * 4,
        compiler_params=pltpu.CompilerParams(dimension_semantics=("parallel", "parallel"),
                                             vmem_limit_bytes=48 << 20),
    )(recv, w, m, v)


def _small_local(parts, segments, n_rows):
    def body(parts_ref, loc_ref):
        loc_ref[...] = jnp.zeros_like(loc_ref)
        for out_row, n_out, in_row, n_in, kind in segments:
            if kind == "copy":
                loc_ref[out_row:out_row + n_out, :] = parts_ref[in_row:in_row + n_in, :]
            else:
                s = jnp.sum(parts_ref[in_row:in_row + n_in, :], axis=0, keepdims=True)
                if kind == "loss":
                    s = jnp.broadcast_to(jnp.sum(s, axis=1, keepdims=True) * (0.5 / D), (1, D))
                loc_ref[out_row:out_row + 1, :] = s

    vm = pl.BlockSpec(memory_space=pltpu.VMEM)
    return pl.pallas_call(body, name="small_local", in_specs=[vm], out_specs=vm,
                          out_shape=jax.ShapeDtypeStruct((n_rows, D), _F32))(parts)


def _small_final(blocks, late8, late_row, w, m, v):
    n_rows = w.shape[0]

    def body(blocks_ref, late_ref, w_ref, m_ref, v_ref, g_out, d_out, m_out, v_out, loc_ref, recv_ref, send_sems,
             recv_sems):
        peers, me = _peers()
        loc_ref[...] = jnp.broadcast_to(jnp.sum(late_ref[...], axis=0, keepdims=True), (8, D))
        recv_ref[me] = loc_ref[...]
        copies = [pltpu.make_async_remote_copy(src_ref=loc_ref, dst_ref=recv_ref.at[me], send_sem=send_sems.at[k],
                                               recv_sem=recv_sems.at[k], device_id=pid, device_id_type=_MESH)
                  for k, pid, _ in peers]
        for cp in copies:
            cp.start()
        g = blocks_ref[0:n_rows, :]
        for s in range(1, N_DEV):
            g = g + blocks_ref[s * n_rows:(s + 1) * n_rows, :]
        for cp in copies:
            cp.wait()
        late = recv_ref[0]
        for s in range(1, N_DEV):
            late = late + recv_ref[s]
        g = jnp.where(_iota2((n_rows, D), 0) == late_row, jnp.broadcast_to(late[0:1, :], (n_rows, D)), g)
        d_, m_, v_ = _adam_vals(w_ref[...], g, m_ref[...], v_ref[...])
        g_out[...] = g
        d_out[...] = d_
        m_out[...] = m_
        v_out[...] = v_

    vm = pl.BlockSpec(memory_space=pltpu.VMEM)
    return pl.pallas_call(
        body, name="small_final", in_specs=[vm] * 5, out_specs=[vm] * 4,
        out_shape=[jax.ShapeDtypeStruct((n_rows, D), _F32)] * 4,
        scratch_shapes=[pltpu.VMEM((8, D), _F32), pltpu.VMEM((N_DEV, 8, D), _F32),
                        pltpu.SemaphoreType.DMA((N_DEV - 1,)), pltpu.SemaphoreType.DMA((N_DEV - 1,))],
        compiler_params=pltpu.CompilerParams(vmem_limit_bytes=48 << 20),
    )(blocks, late8, w, m, v)


_BIG_NAMES =("w_in", "w_out", "w_ff1", "w_ff2", "w_ple_gate", "w_ple_proj")

_G_VECS = ("norm_mix_g", "gm_v_norm_g", "gm_out_norm_g", "ssd_norm_g", "norm_mlp_g", "ple_norm_g", "final_norm_g")
_LATE = _G_VECS[0]


def _const_mats():
    h = np.arange(128)[:, None]
    ch = np.arange(SSD_W)[None, :]
    e = (ch // SSD_P == h).astype(np.float32)
    ltri = (np.arange(CH)[:, None] >= np.arange(CH)[None, :]).astype(np.float32)
    return jnp.asarray(e, _BF16), jnp.asarray(e.T, _BF16), jnp.asarray(ltri, _BF16)


def _pad_lanes(v, n=128):
    return jnp.pad(v, ((0, 0), (0, n - v.shape[1])))


def _local_step(x, p, tgt, shard, conv_w_shard, small, seq_len):
    seq_chunks = seq_len // CH
    e_mat, et_mat, ltri_mat = _const_mats()
    g_mix, g_mlp, g_ple = small["norm_mix_g"], small["norm_mlp_g"], small["ple_norm_g"]
    g_fin = small["final_norm_g"].reshape(1, D)
    gv, gout, ng = small["gm_v_norm_g"], small["gm_out_norm_g"], small["ssd_norm_g"]
    ws = small["gm_ws"][0]
    bsb = jnp.broadcast_to(small["gm_bs"][0][:, :, None], (GM_H, CH, 128))
    cb = small["ssd_conv_b"]
    dtb, alog = _pad_lanes(small["ssd_dt_bias"]), _pad_lanes(small["ssd_a_log"])
    d_x = jnp.repeat(small["ssd_d"], SSD_P, axis=1)

    first = lambda acc: (acc,)
    rows, cols = ("gather2", 0), ("gather2", 1)
    n1, (g_win, g_cw) = _norm_cast("norm_mix", x, g_mix,
                                   comm=[(rows, shard["w_in"][None]), (("gather", 0), conv_w_shard[None])])
    w_in_t = jnp.pad(g_win.reshape(D_IN, D), ((0, D_IN_PAD - D_IN), (0, 0)))
    cw8 = jnp.pad(g_cw.transpose(1, 0, 2).reshape(CONV_K, CONV_CH), ((0, 8 - CONV_K), (0, 0)))
    mix_consts = (gv, ws, bsb, gout, cw8, cb, dtb, alog, d_x, ng)
    (proj,), (w_out,) = _matmul("proj_in", n1, w_in_t, "nt", 512, D_IN_PAD, D, first, [("tile", _F32)],
                                comm=[(rows, shard["w_out"])])
    (cat, yss, states, conv_act, conv_sig), (w1, w2, wg, wp) = _mixer_fwd(
        proj, *mix_consts, e_mat, ltri_mat, seq_chunks,
        comm=[(cols, shard["w_ff1"]), (rows, shard["w_ff2"]), (rows, shard["w_ple_gate"]),
              (cols, shard["w_ple_proj"])])

    def epi_res_norm(acc, res, g):
        hv = acc + res
        return hv, _rms(hv, g)

    (h1, n2), _ = _matmul("proj_out", cat, w_out, "nn", 1024, D, 2 * D, epi_res_norm,
                          [("tile", _F32), ("tile", _MXU)], extras=[(x, "tile"), (g_mlp, "row")])

    def epi_relu2(acc):
        hid = jnp.maximum(acc, 0.0)
        return hid, hid * hid

    (hid, hid2), _ = _matmul("ff1", n2, w1, "nn", 1024, 1024, D, epi_relu2, [("tile", _MXU), ("tile", _MXU)])
    (h2, n3), _ = _matmul("ff2", hid2, w2, "nn", 512, D, D_FF, epi_res_norm, [("tile", _F32), ("tile", _MXU)],
                          extras=[(h1, "tile"), (g_ple, "row")])

    def epi_norm_bwd(acc, up, hv, g):
        dx, dg8 = _rms_bwd(hv, g, acc)
        dh = up + dx
        return dh, dh, dg8

    def epi_head(acc, wg_v, p_rows, wp_v, h2v, tg, gf, gp_):
        ppv = _dot(_mx(p_rows), wp_v)
        gate = _sigmoid(acc)
        gp = gate * ppv
        h3 = h2v + gp
        r = lax.rsqrt(jnp.mean(h3 * h3, axis=-1, keepdims=True) + EPS)
        nh = h3 * r
        err = nh * gf - tg
        gy = err * (gf * (1.0 / D))
        dh3 = (gy - nh * jnp.mean(gy * nh, axis=-1, keepdims=True)) * r
        dpp = dh3 * gate
        da3 = dpp * (ppv - gp)
        dh2, dh2_again, dgple8 = epi_norm_bwd(_dot_nt(_mx(da3), wg_v), dh3, h2v, gp_)
        return da3, dpp, dh2, dh2_again, _sum8(err * err), _sum8(err * nh) * (1.0 / D), dgple8

    (da3, dpp, dh2, dh2b, lossp, dgfin, dgple), _ = _matmul(
        "ple_gate_loss_bwd", n3, wg, "nn", 512, D, D, epi_head,
        [("tile", _MXU), ("tile", _MXU), ("tile", _F32), ("tile", _MXU), ("part8", _F32), ("part8", _F32),
         ("part8", _F32)],
        extras=[(p, "rows"), (wp, "full"), (h2, "tile"), (tgt, "tile"), (g_fin, "row"), (g_ple, "row")],
        b_to_epilogue=True)

    s_rows, s_cols = ("scatter", 0), ("scatter", 1)
    (dwp,), _ = _matmul("d_w_ple_proj", p, dpp, "tn", D_PLE, D, 2048, first, [("tile", _BF16)])
    (dwg,), _ = _matmul("d_w_ple_gate", n3, da3, "tn", D, D, 2048, first, [("tile", _BF16)])
    (dw2,), _ = _matmul("d_w_ff2", hid2, dh2b, "tn", 1024, D, 4096, first, [("tile", _BF16)])
    (da1,), _ = _matmul("d_ff_hidden", dh2b, w2, "nt", 512, 2048, D,
                        lambda acc, hv: (acc * 2.0 * hv.astype(_F32),), [("tile", _MXU)], extras=[(hid, "tile")])
    (dw1,), _ = _matmul("d_w_ff1", n2, da1, "tn", 1024, 1024, 4096, first, [("tile", _BF16)])
    (dh1, dh1b, dgmlp), _ = _matmul(
        "d_h1", da1, w1, "nt", 256, D, D_FF, epi_norm_bwd, [("tile", _F32), ("tile", _MXU), ("part8", _F32)],
        extras=[(dh2, "tile"), (h1, "tile"), (g_mlp, "row")])
    (dwout,), _ = _matmul("d_w_out", cat, dh1b, "tn", 1024, D, 2048, first, [("tile", _BF16)])
    (dcat,), _ = _matmul("d_cat", dh1b, w_out, "nt", 1024, 1024, D, first, [("tile", _MXU)])
    (dproj, dws, dbs, dgv, dgout, dng, dcw, dcb, ddtb, dalog, dd), (r_wp, r_wg, r_w2, r_w1, r_wout) = _mixer_bwd(
        proj, conv_act, conv_sig, dcat, yss, states, *mix_consts, e_mat, et_mat, ltri_mat, seq_chunks,
        comm=[(s_cols, dwp), (s_rows, dwg), (s_rows, dw2), (s_cols, dw1), (s_rows, dwout)])
    pieces = dict(gm_v_norm_g=dgv, gm_out_norm_g=dgout, ssd_norm_g=dng, norm_mlp_g=dgmlp, ple_norm_g=dgple,
                  final_norm_g=dgfin, gm_ws=dws, gm_bs=dbs, ssd_conv_w=dcw, ssd_conv_b=dcb, ssd_dt_bias=ddtb,
                  ssd_a_log=dalog, ssd_d=dd, loss=lossp)
    parts, segments, n_rows, where = _small_layout(pieces)
    small_block = _small_local(parts, segments, n_rows)
    (dwin_t,), (small_blocks,) = _matmul("d_w_in", n1, dproj, "tn", 512, D_IN_PAD, 1024, lambda acc: (acc.T,),
                                         [("tile_t", _BF16)], comm=[(("gather", 0), small_block)])
    dwin_blocks = dwin_t[:D_IN].reshape(N_DEV, SHARD_IN, D)
    (gx, dgmix), (r_win,) = _matmul(
        "d_x", dproj, w_in_t, "nn", 256, D, D_IN_PAD, lambda *a: epi_norm_bwd(*a)[1:],
        [("tile", _F32), ("part8", _F32)], extras=[(dh1, "tile"), (x, "tile"), (g_mix, "row")],
        comm=[(s_rows, dwin_blocks)])
    r_win = r_win.reshape(N_DEV, SHARD_IN, D)

    big = dict(w_in=r_win, w_out=r_wout, w_ff1=r_w1, w_ff2=r_w2, w_ple_gate=r_wg, w_ple_proj=r_wp)
    return gx, big, small_blocks, dgmix, n_rows, where


def _small_layout(pieces):
    rows, segments = [], []
    in_row, out_row = 0, 0

    def add(arr, kind, n_out):
        nonlocal in_row, out_row
        rows.append(arr)
        segments.append((out_row, n_out, in_row, arr.shape[0], kind))
        start = out_row
        in_row += arr.shape[0]
        out_row += n_out
        return start

    where = {_LATE: 0}
    out_row = 1
    for name in _G_VECS[1:]:
        where[name] = add(pieces[name], "sum", 1)
    where["gm_ws"] = add(pieces["gm_ws"].reshape(GM_H * CH * CH // D, D), "copy", GM_H * CH * CH // D)
    where["gm_bs"] = add(pieces["gm_bs"].reshape(1, D), "copy", 1)
    cb = jnp.pad(pieces["ssd_conv_b"], ((0, 0), (0, 2 * D - CONV_CH)))
    where["ssd_conv_b"] = add(cb[:, :D], "sum", 1)
    add(cb[:, D:], "sum", 1)
    cw = jnp.pad(pieces["ssd_conv_w"][:CONV_K], ((0, 0), (0, 2 * D - CONV_CH)))
    where["ssd_conv_w"] = add(cw.reshape(2 * CONV_K, D), "copy", 2 * CONV_K)
    misc = jnp.concatenate([pieces["ssd_dt_bias"], pieces["ssd_a_log"], pieces["ssd_d"],
                            jnp.zeros((8, D - 3 * 128), _F32)], axis=1)
    where["misc"] = add(misc, "sum", 1)
    where["loss"] = add(pieces["loss"], "loss", 1)
    n_rows = -(-out_row // 8) * 8
    return jnp.concatenate(rows, axis=0), tuple(segments), n_rows, where


def _pack_small_params(vals, where, n_rows, my_block):
    rows, at = [], {}

    def add(name, arr):
        at[name] = sum(r.shape[0] for r in rows)
        rows.append(arr)

    for name in _G_VECS:
        add(name, vals[name].reshape(1, D))
    add("gm_ws", vals["gm_ws"].reshape(GM_H * CH * CH // D, D))
    add("gm_bs", vals["gm_bs"].reshape(1, D))
    cb = jnp.pad(vals["ssd_conv_b"].reshape(1, CONV_CH), ((0, 0), (0, 2 * D - CONV_CH)))
    add("ssd_conv_b", cb.reshape(2, D))
    cw = lax.dynamic_update_slice(jnp.zeros((CONV_K, 2 * D), _F32), vals["ssd_conv_w"].reshape(CONV_K, -1),
                                  (0, my_block * (CONV_CH // N_DEV)))
    add("ssd_conv_w", cw.reshape(2 * CONV_K, D))
    misc = jnp.concatenate([_pad_lanes(vals["ssd_dt_bias"].reshape(1, SSD_H)),
                            _pad_lanes(vals["ssd_a_log"].reshape(1, SSD_H)),
                            _pad_lanes(vals["ssd_d"].reshape(1, SSD_H)), jnp.zeros((1, D - 3 * 128), _F32)], axis=1)
    add("misc", misc)
    assert all(where[k] == r for k, r in at.items()), (where, at)
    rows.append(jnp.zeros((n_rows - sum(r.shape[0] for r in rows), D), _F32))
    return jnp.concatenate(rows, axis=0)


def _unpack_small(buf, where, my_block, shapes):
    out = {}
    for name in _G_VECS:
        out[name] = buf[where[name]].reshape(shapes[name])
    n_ws = GM_H * CH * CH // D
    out["gm_ws"] = buf[where["gm_ws"]:where["gm_ws"] + n_ws].reshape(shapes["gm_ws"])
    out["gm_bs"] = buf[where["gm_bs"]].reshape(shapes["gm_bs"])
    r = where["ssd_conv_b"]
    out["ssd_conv_b"] = buf[r:r + 2].reshape(1, 2 * D)[:, :CONV_CH].reshape(shapes["ssd_conv_b"])
    r = where["ssd_conv_w"]
    cw = buf[r:r + 2 * CONV_K].reshape(CONV_K, 2 * D)
    out["ssd_conv_w"] = lax.dynamic_slice(cw, (0, my_block * (CONV_CH // N_DEV)),
                                          (CONV_K, CONV_CH // N_DEV)).reshape(shapes["ssd_conv_w"])
    misc = buf[where["misc"]]
    for i, name in enumerate(("ssd_dt_bias", "ssd_a_log", "ssd_d")):
        out[name] = misc[i * 128:i * 128 + SSD_H].reshape(shapes[name])
    return out


_WEIGHTS = ("norm_mix_g", "w_in", "gm_v_norm_g", "gm_ws", "gm_bs", "gm_out_norm_g", "ssd_conv_w", "ssd_conv_b",
            "ssd_dt_bias", "ssd_a_log", "ssd_d", "ssd_norm_g", "w_out", "norm_mlp_g", "w_ff1", "w_ff2", "ple_norm_g",
            "w_ple_gate", "w_ple_proj", "final_norm_g")


def kernel(x, p, norm_mix_g, w_in, gm_v_norm_g, gm_ws, gm_bs, gm_out_norm_g, ssd_conv_w, ssd_conv_b, ssd_dt_bias, ssd_a_log, ssd_d, ssd_norm_g, w_out, norm_mlp_g, w_ff1, w_ff2, ple_norm_g, w_ple_gate, w_ple_proj, final_norm_g, loss_target, m_norm_mix_g, m_w_in, m_gm_v_norm_g, m_gm_ws, m_gm_bs, m_gm_out_norm_g, m_ssd_conv_w, m_ssd_conv_b, m_ssd_dt_bias, m_ssd_a_log, m_ssd_d, m_ssd_norm_g, m_w_out, m_norm_mlp_g, m_w_ff1, m_w_ff2, m_ple_norm_g, m_w_ple_gate, m_w_ple_proj, m_final_norm_g, v_norm_mix_g, v_w_in, v_gm_v_norm_g, v_gm_ws, v_gm_bs, v_gm_out_norm_g, v_ssd_conv_w, v_ssd_conv_b, v_ssd_dt_bias, v_ssd_a_log, v_ssd_d, v_ssd_norm_g, v_w_out, v_norm_mlp_g, v_w_ff1, v_w_ff2, v_ple_norm_g, v_w_ple_gate, v_w_ple_proj, v_final_norm_g):
    args = dict(locals())
    w = {n: args[n] for n in _WEIGHTS}
    m = {n: args["m_" + n] for n in _WEIGHTS}
    v = {n: args["v_" + n] for n in _WEIGHTS}
    shapes = {n: w[n].shape for n in _WEIGHTS}
    my_block = 4 * lax.axis_index("x") + 2 * lax.axis_index("y") + lax.axis_index("c")
    nb, seq_len, _ = x.shape

    local = lambda d, n: d[n][0].T if n == "w_in" else d[n][0]
    shard = {n: local(w, n).astype(_MXU) for n in _BIG_NAMES}
    small = {n: w[n] for n in _WEIGHTS if n not in _BIG_NAMES}
    gx, recv, small_blocks, late8, n_rows, where = _local_step(
        x.reshape(nb * seq_len, D), p.reshape(nb * seq_len, D_PLE), loss_target.reshape(nb * seq_len, D), shard,
        ssd_conv_w[0], small, seq_len)

    big_out = [{}, {}, {}, {}]
    for n in _BIG_NAMES:
        res = _sum_adam("sum_adam_" + n, recv[n], local(w, n), local(m, n), local(v, n))
        for k in range(4):
            big_out[k][n] = (res[k].T if n == "w_in" else res[k]).reshape(shapes[n])

    packs = [_pack_small_params(d, where, n_rows, my_block) for d in (w, m, v)]
    small_res = _small_final(small_blocks, late8, where[_LATE], *packs)
    loss = small_res[0][where["loss"], 0]
    small_out = [_unpack_small(a, where, my_block, shapes) for a in small_res]

    outs = [loss, gx.reshape(x.shape)]
    for k in range(4):
        outs += [big_out[k][n] if n in _BIG_NAMES else small_out[k][n] for n in _WEIGHTS]
    return tuple(outs)
```

```python
import functools
import math

import jax
import jax.numpy as jnp
import numpy as np
from jax import lax
from jax.experimental import pallas as pl
from jax.experimental.pallas import tpu as pltpu

_F32 = jnp.float32
_BF16 = jnp.bfloat16
_MXU = jnp.bfloat16

D = 1024
D_PLE = 256
GM_W = 1024
GM_H = 8
CH = 128
SSD_W = 1024
SSD_H = 16
SSD_P = 64
SSD_G = 2
SSD_N = 128
CONV_K = 4
CONV_CH = SSD_W + 2 * SSD_G * SSD_N
D_FF = 4096
D_IN = 2 * GM_W + SSD_W + CONV_CH + SSD_H
D_IN_PAD = 4736
EPS = 1e-6
DT_BLK = (D_IN_PAD - 128) // 128
N_DEV = 8
SHARD_IN = D_IN // N_DEV

LR, B1, B2, ADAM_EPS, WD, STEP = 0.001, 0.9, 0.999, 1e-08, 0.01, 10
_LOG2E = math.log2(math.e)
_LOG2_INV_SQRT_2PI = -0.5 * math.log2(2.0 * math.pi)

_SEQ_PER_STEP = 2
_V7X_VMEM_BYTES = 64 * 1024 * 1024
_VMEM_CAP = _V7X_VMEM_BYTES - 8 * 1024 * 1024
_MESH = pl.DeviceIdType.MESH


def _vmem_limit(nbytes):
    return int(min(_VMEM_CAP, max(32 * 1024 * 1024, nbytes * 5 // 4 + (4 << 20))))


def _nbytes(shape, dtype):
    return int(np.prod(shape)) * jnp.dtype(dtype).itemsize


def _mx(v):
    return v.astype(_MXU)


def _dot(a, b):
    return jnp.dot(a, b, preferred_element_type=_F32)


def _dot_nt(a, b):
    return lax.dot_general(a, b, (((1,), (1,)), ((), ())), preferred_element_type=_F32)


def _dot_tn(a, b):
    return lax.dot_general(a, b, (((0,), (0,)), ((), ())), preferred_element_type=_F32)


def _split3(a):
    hi = a.astype(_BF16)
    r = a - hi.astype(_F32)
    mid = r.astype(_BF16)
    lo = (r - mid.astype(_F32)).astype(_BF16)
    return hi, mid, lo


def _xdot(dotfn, a, b01):
    b = b01.astype(_BF16)
    hi, mid, lo = _split3(a)
    return (dotfn(hi, b) + dotfn(mid, b)) + dotfn(lo, b)


def _xdot_left(dotfn, a01, b):
    a = a01.astype(_BF16)
    hi, mid, lo = _split3(b)
    return (dotfn(a, hi) + dotfn(a, mid)) + dotfn(a, lo)


def _sum8(v):
    r, n = v.shape
    return v.reshape(r // 8, 8, n).sum(axis=0)


def _sigmoid(v):
    return 1.0 / (1.0 + jnp.exp(-v))


def _rms(xv, g):
    ms = jnp.mean(xv * xv, axis=-1, keepdims=True)
    return xv * lax.rsqrt(ms + EPS) * g


def _rms_bwd(xv, g, dn):
    r = lax.rsqrt(jnp.mean(xv * xv, axis=-1, keepdims=True) + EPS)
    nh = xv * r
    gy = dn * g
    dx = (gy - nh * jnp.mean(gy * nh, axis=-1, keepdims=True)) * r
    return dx, _sum8(dn * nh)


def _iota2(shape, axis):
    return lax.broadcasted_iota(jnp.int32, shape, axis)


def _norm_cast(name, x, g, tm=512, comm=()):
    t, n = x.shape
    tm = min(tm, t)
    steps = t // tm
    kinds = [kind for kind, _ in comm]
    c_in, c_in_specs, c_out_specs, c_out_shape, c_scratch = _comm_io(comm)

    def body(*refs):
        x_ref, g_ref = refs[0], refs[1]
        o_ref = refs[2 + len(comm)]
        comm_refs = (kinds, refs[2:2 + len(comm)], refs[3 + len(comm):3 + 2 * len(comm)], *refs[3 + 2 * len(comm):])
        if comm:
            pl.when(pl.program_id(0) == 0)(lambda: _comm_start(*comm_refs))

        o_ref[...] = _rms(x_ref[...], g_ref[...]).astype(o_ref.dtype)
        if comm:
            pl.when(pl.program_id(0) == steps - 1)(lambda: _comm_finish(*comm_refs))

    res = pl.pallas_call(
        body, name=name, grid=(steps,),
        in_specs=[pl.BlockSpec((tm, n), lambda i: (i, 0)), pl.BlockSpec((1, n), lambda i: (0, 0))] + c_in_specs,
        out_specs=[pl.BlockSpec((tm, n), lambda i: (i, 0))] + c_out_specs,
        out_shape=[jax.ShapeDtypeStruct((t, n), _MXU)] + c_out_shape, scratch_shapes=c_scratch,
        compiler_params=pltpu.CompilerParams(dimension_semantics=("arbitrary",)),
    )(x, g, *c_in)
    return res[0], res[1:]


def _matmul(name, a, b, mode, tm, tn, tk, epilogue, outs, extras=(), comm=(), b_to_epilogue=False):
    m, k = a.shape[::-1] if mode == "tn" else a.shape
    n = b.shape[0] if mode == "nt" else b.shape[1]
    tm, tn, tk = min(tm, m), min(tn, n), min(tk, k)
    assert m % tm == 0 and n % tn == 0 and k % tk == 0, (name, m, n, k, tm, tn, tk)
    if mode == "nn":
        a_spec = pl.BlockSpec((tm, tk), lambda i, j, kk: (i, kk))
        b_spec = pl.BlockSpec((tk, tn), lambda i, j, kk: (kk, j))
        dotfn = _dot
    elif mode == "nt":
        a_spec = pl.BlockSpec((tm, tk), lambda i, j, kk: (i, kk))
        b_spec = pl.BlockSpec((tn, tk), lambda i, j, kk: (j, kk))
        dotfn = _dot_nt
    else:
        a_spec = pl.BlockSpec((tk, tm), lambda i, j, kk: (kk, i))
        b_spec = pl.BlockSpec((tk, tn), lambda i, j, kk: (kk, j))
        dotfn = _dot_tn
    ni, nj, nk = m // tm, n // tn, k // tk
    n_ex, n_out, n_comm = len(extras), len(outs), len(comm)
    kinds = [kind for kind, _ in comm]
    c_in, c_in_specs, c_out_specs, c_out_shape, c_scratch = _comm_io(comm)

    in_specs, vmem = [a_spec, b_spec], 2 * (tm * tk * a.dtype.itemsize + tk * tn * b.dtype.itemsize)
    for arr, kind in extras:
        if kind == "tile":
            in_specs.append(pl.BlockSpec((tm, tn), lambda i, j, kk: (i, j)))
            vmem += 2 * _nbytes((tm, tn), arr.dtype)
        elif kind == "rows":
            in_specs.append(pl.BlockSpec((tm, arr.shape[1]), lambda i, j, kk: (i, 0)))
            vmem += 2 * _nbytes((tm, arr.shape[1]), arr.dtype)
        elif kind == "full":
            in_specs.append(pl.BlockSpec(arr.shape, lambda i, j, kk: (0,) * arr.ndim))
            vmem += 2 * _nbytes(arr.shape, arr.dtype)
        else:
            in_specs.append(pl.BlockSpec((1, tn), lambda i, j, kk: (0, j)))
    out_specs, out_shape = [], []
    for kind, dt in outs:
        if kind == "tile":
            out_specs.append(pl.BlockSpec((tm, tn), lambda i, j, kk: (i, j)))
            out_shape.append(jax.ShapeDtypeStruct((m, n), dt))
            vmem += 2 * _nbytes((tm, tn), dt)
        elif kind == "tile_t":
            out_specs.append(pl.BlockSpec((tn, tm), lambda i, j, kk: (j, i)))
            out_shape.append(jax.ShapeDtypeStruct((n, m), dt))
            vmem += 2 * _nbytes((tm, tn), dt)
        else:
            assert nj == 1, "the partial-sum rows are accumulated over consecutive row tiles"
            out_specs.append(pl.BlockSpec((8, tn), lambda i, j, kk: (0, 0)))
            out_shape.append(jax.ShapeDtypeStruct((8, n), dt))
    scratch = [pltpu.VMEM((tm, tn), _F32)] if nk > 1 else []
    vmem += _nbytes((tm, tn), _F32) * 2

    def body(*refs):
        a_ref, b_ref = refs[0], refs[1]
        ex_refs = refs[2:2 + n_ex]
        n_in = 2 + n_ex + n_comm
        out_refs = refs[n_in:n_in + n_out]
        i, j, kk = pl.program_id(0), pl.program_id(1), pl.program_id(2)
        comm_refs = (kinds, refs[2 + n_ex:n_in], refs[n_in + n_out:n_in + n_out + n_comm], *refs[len(refs) - 3:])
        if n_comm:
            pl.when((i == 0) & (j == 0) & (kk == 0))(lambda: _comm_start(*comm_refs))

        b_val = _mx(b_ref[...])
        part = dotfn(_mx(a_ref[...]), b_val)

        def finish(acc):
            vals = epilogue(acc, *([b_val] if b_to_epilogue else []), *[r[...] for r in ex_refs])
            for r, v, (kind, _) in zip(out_refs, vals, outs):
                if kind == "part8":
                    @pl.when(i == 0)
                    def _():
                        r[...] = v

                    @pl.when(i > 0)
                    def _():
                        r[...] += v
                else:
                    r[...] = v.astype(r.dtype)

        if nk == 1:
            finish(part)
        else:
            acc_ref = refs[n_in + n_out + n_comm]

            @pl.when(kk == 0)
            def _():
                acc_ref[...] = part

            @pl.when(kk > 0)
            def _():
                acc_ref[...] += part

            @pl.when(kk == nk - 1)
            def _():
                finish(acc_ref[...])

        if n_comm:
            pl.when((i == ni - 1) & (j == nj - 1) & (kk == nk - 1))(lambda: _comm_finish(*comm_refs))

    carried =n_comm or any(kind == "part8" for kind, _ in outs)
    sem = ("arbitrary",) * 3 if carried else ("parallel", "parallel", "arbitrary")
    res = pl.pallas_call(
        body, name=name, grid=(ni, nj, nk),
        in_specs=in_specs + c_in_specs, out_specs=out_specs + c_out_specs, out_shape=out_shape + c_out_shape,
        scratch_shapes=scratch + c_scratch,
        compiler_params=pltpu.CompilerParams(dimension_semantics=sem, vmem_limit_bytes=_vmem_limit(vmem)),
    )(a, b, *[arr for arr, _ in extras], *c_in)
    return res[:n_out], res[n_out:]


def _shift_down(v, halo8, j):
    if j == 0:
        return v
    r = pltpu.roll(v, j, axis=0)
    hr = pltpu.roll(halo8, j, axis=0)
    top = jnp.where(_iota2(hr.shape, 0) < j, hr, r[:8])
    return jnp.concatenate([top, r[8:]], axis=0)


def _shift_up(v, next8, j):
    if j == 0:
        return v
    rows = v.shape[0]
    r = pltpu.roll(v, rows - j, axis=0)
    nr = pltpu.roll(next8, 8 - j, axis=0)
    bot = jnp.where(_iota2(nr.shape, 0) >= 8 - j, nr, r[rows - 8:])
    return jnp.concatenate([r[:rows - 8], bot], axis=0)


def _silu_grad(sig, silu):
    return sig + silu * (1.0 - sig)


def _gmlp_fwd_vals(pu, pv, gv, ws_ref, bsb_ref, want_bwd):
    tril = _iota2((CH, CH), 0) >= _iota2((CH, CH), 1)
    cdf_u = 0.5 * (1.0 + lax.erf(pu * 0.7071067811865476))
    cdf_v = 0.5 * (1.0 + lax.erf(pv * 0.7071067811865476))
    u = pu * cdf_u
    v = pv * cdf_v
    ys, keep = [], [(cdf_u, cdf_v)] if want_bwd else []
    for h in range(GM_H):
        sl = slice(h * 128, (h + 1) * 128)
        vh = v[:, sl]
        r = lax.rsqrt(jnp.mean(vh * vh, axis=-1, keepdims=True) + EPS)
        vn = vh * r * gv[:, sl]
        wm = _mx(jnp.where(tril, ws_ref[h], 0.0))
        mixed = _dot(wm, _mx(vn)) + bsb_ref[h]
        ys.append(u[:, sl] * mixed)
        if want_bwd:
            keep.append((vh, r, vn, wm, mixed))
    return jnp.concatenate(ys, axis=1), u, keep


def _ssd_conv(xbc, halo8, cw_ref, cb):
    cpre = cb + sum(cw_ref[k:k + 1, :] * _shift_down(xbc, halo8, CONV_K - 1 - k) for k in range(CONV_K))
    sig = _sigmoid(cpre)
    return sig, cpre * sig


def _ssd_decay(dtraw, dtb, alog, e_ref, ltri):
    dtin = dtraw + dtb
    dt = jnp.maximum(dtin, 0.0) + jnp.log(1.0 + jnp.exp(-jnp.abs(dtin)))
    a_neg = -jnp.exp(alog)
    cs = _xdot_left(_dot, ltri, dt * a_neg)
    cs_last = cs[CH - 1:CH, :]
    ecs = jnp.exp(cs)
    dec = jnp.exp(cs_last - cs)
    cdec = jnp.exp(cs_last)
    e = e_ref[...]
    dt_x = _dot(dt.astype(_BF16), e)
    ecs_x = _dot(ecs.astype(_BF16), e)
    dec_x = _dot(dec.astype(_BF16), e)
    cdec_x = _xdot(_dot, jnp.broadcast_to(cdec, (8, 128)), e)[0:1, :]
    return dict(dtin=dtin, dt=dt, a_neg=a_neg, cs=cs, ecs=ecs, dec=dec, cdec=cdec,
                dt_x=dt_x, ecs_x=ecs_x, dec_x=dec_x, cdec_x=cdec_x)


def _head_lm(cs, cst_ref, h, tril):
    seg = jnp.broadcast_to(cs[:, h:h + 1], (CH, CH)) - cst_ref[h:h + 1, :]
    return jnp.exp(jnp.where(tril, seg, -jnp.inf))


def _mixer_fwd(proj, gv, ws, bsb, gout, cw8, cb, dtb, alog, d_x, ng, e_mat, ltri_mat, seq_chunks, comm=()):
    t = proj.shape[0]
    n_seq = t // (seq_chunks * CH)
    sb = math.gcd(_SEQ_PER_STEP, n_seq)
    n_groups = n_seq // sb
    proj = proj.reshape(n_seq, seq_chunks * CH, proj.shape[1])
    n_comm = len(comm)
    kinds = [kind for kind, _ in comm]
    c_in, c_in_specs, c_out_specs, c_out_shape, c_scratch = _comm_io(comm)

    def body(*refs):
        comm_refs = (kinds, refs[18:18 + n_comm], refs[23 + n_comm:23 + 2 * n_comm], *refs[25 + 2 * n_comm:])
        grp, c = pl.program_id(0), pl.program_id(1)
        if n_comm:
            pl.when((grp == 0) & (c == 0))(lambda: _comm_start(*comm_refs))
            pl.when((grp == n_groups - 1) & (c == seq_chunks - 1))(lambda: _comm_finish(*comm_refs))
        for s in range(sb):
            per_seq = lambda rs: [r.at[s] for r in rs]
            one_chunk(c == 0, *per_seq(refs[:6]), *refs[6:18], *per_seq(refs[18 + n_comm:23 + n_comm]),
                      *per_seq(refs[23 + 2 * n_comm:25 + 2 * n_comm]))

    def one_chunk(first, pu_ref, pv_ref, z_ref, xbc_ref, dt_ref, halo_ref, gv_ref, ws_ref, bsb_ref, gout_ref, cw_ref,
                  cb_ref, dtb_ref, alog_ref, dx_ref, ng_ref, e_ref, ltri_ref, cat_ref, y_ref, st_ref, act_ref, sig_ref,
                  s_ref, cst_ref):
        tril = _iota2((CH, CH), 0) >= _iota2((CH, CH), 1)
        lane = _iota2((CH, 128), 1)

        y_a, _, _ = _gmlp_fwd_vals(pu_ref[...], pv_ref[...], gv_ref[...], ws_ref, bsb_ref, False)
        cat_ref[:, 0:GM_W] = _rms(y_a, gout_ref[...]).astype(cat_ref.dtype)

        @pl.when(first)
        def _():
            s_ref[...] = jnp.zeros_like(s_ref)

        halo8 = jnp.where(first, 0.0, halo_ref[...])
        sig, act = _ssd_conv(xbc_ref[...], halo8, cw_ref, cb_ref[...])
        sig_ref[...] = sig
        act_ref[...] = act
        q = _ssd_decay(dt_ref[...], dtb_ref[...], alog_ref[...], e_ref, ltri_ref[...])
        xv = act[:, 0:SSD_W]
        xdt = xv * q["dt_x"]
        xdt_m = _mx(xdt)
        cs = q["cs"]
        cst_ref[...] = cs.T
        s_prev = s_ref[...]
        st_ref[...] = s_prev
        ys = []
        for g in range(SSD_G):
            bg = _mx(act[:, SSD_W + g * SSD_N:SSD_W + (g + 1) * SSD_N])
            cg = _mx(act[:, SSD_W + SSD_G * SSD_N + g * SSD_N:SSD_W + SSD_G * SSD_N + (g + 1) * SSD_N])
            cbm = _dot_nt(cg, bg)
            gs = slice(g * 512, (g + 1) * 512)
            for pr in range(4):
                ps = slice(g * 512 + pr * 128, g * 512 + (pr + 1) * 128)
                o = []
                for hh in range(2):
                    h = g * 8 + pr * 2 + hh
                    m_h = _mx(cbm * _head_lm(cs, cst_ref, h, tril))
                    o.append(_dot(m_h, xdt_m[:, ps]))
                ys.append(jnp.where(lane < SSD_P, o[0], o[1]))
            sg = s_prev[:, gs]
            yoff = _dot(cg, _mx(sg)) * q["ecs_x"][:, gs]
            ys[-4:] = [ys[-4 + i] + yoff[:, i * 128:(i + 1) * 128] for i in range(4)]
            st_new = _dot_tn(bg, _mx(q["dec_x"][:, gs] * xdt[:, gs]))
            s_ref[:, gs] = sg * q["cdec_x"][:, gs] + st_new
        y = jnp.concatenate(ys, axis=1) + dx_ref[...] * xv
        y_ref[...] = y
        zv = z_ref[...]
        yg = y * (zv * _sigmoid(zv))
        for g in range(SSD_G):
            gs = slice(g * 512, (g + 1) * 512)
            cat_ref[:, GM_W + g * 512:GM_W + (g + 1) * 512] = _rms(yg[:, gs], ng_ref[:, gs]).astype(cat_ref.dtype)

    blk = lambda w, j: pl.BlockSpec((sb, CH, w), lambda g, c: (g, c, j))
    full = lambda arr: pl.BlockSpec(arr.shape, lambda g, c: (0,) * arr.ndim)
    consts = [gv, ws, bsb, gout, cw8, cb, dtb, alog, d_x, ng, e_mat, ltri_mat]
    seq = seq_chunks * CH
    res = pl.pallas_call(
        body, name="mixer_fwd", grid=(n_groups, seq_chunks),
        in_specs=[blk(GM_W, 0), blk(GM_W, 1), blk(SSD_W, 2), blk(CONV_CH, 2), blk(128, DT_BLK),
                  pl.BlockSpec((sb, 8, CONV_CH), lambda g, c: (g, jnp.maximum(c * (CH // 8) - 1, 0), 2))]
        + [full(a) for a in consts] + c_in_specs,
        out_specs=[blk(2 * D, 0), blk(SSD_W, 0), blk(SSD_W, 0), blk(CONV_CH, 0), blk(CONV_CH, 0)] + c_out_specs,
        out_shape=[jax.ShapeDtypeStruct((n_seq, seq, 2 * D), _MXU), jax.ShapeDtypeStruct((n_seq, seq, SSD_W), _F32),
                   jax.ShapeDtypeStruct((n_seq, seq, SSD_W), _F32), jax.ShapeDtypeStruct((n_seq, seq, CONV_CH), _F32),
                   jax.ShapeDtypeStruct((n_seq, seq, CONV_CH), _F32)] + c_out_shape,
        scratch_shapes=[pltpu.VMEM((sb, SSD_N, SSD_W), _F32), pltpu.VMEM((sb, 128, CH), _F32)] + c_scratch,
        compiler_params=pltpu.CompilerParams(dimension_semantics=("arbitrary", "arbitrary"),
                                             vmem_limit_bytes=48 << 20),
    )(proj, proj, proj, proj, proj, proj, *consts, *c_in)
    return [r.reshape(t, r.shape[-1]) for r in res[:5]], res[5:]


def _mixer_bwd(proj, act, sig, dcat, yss, states, gv, ws, bsb, gout, cw8, cb, dtb, alog, d_x, ng, e_mat, et_mat,
               ltri_mat, seq_chunks, comm=()):
    t = proj.shape[0]
    n_seq = t // (seq_chunks * CH)
    sb = math.gcd(_SEQ_PER_STEP, n_seq)
    n_groups = n_seq // sb
    seq = seq_chunks * CH
    proj, act, sig, dcat, yss, states = [a.reshape(n_seq, seq, a.shape[1])
                                         for a in (proj, act, sig, dcat, yss, states)]
    n_comm = len(comm)
    kinds = [kind for kind, _ in comm]
    c_in, c_in_specs, c_out_specs, c_out_shape, c_scratch = _comm_io(comm)

    def body(*refs):
        o0, s0 = 23 + n_comm, 34 + 2 * n_comm
        acc_refs, shared = refs[o0 + 1:o0 + 11], refs[s0 + 4:s0 + 6]
        comm_refs = (kinds, refs[23:23 + n_comm], refs[o0 + 11:o0 + 11 + n_comm], *refs[s0 + 6:])
        grp, i = pl.program_id(0), pl.program_id(1)
        if n_comm:
            pl.when((grp == 0) & (i == 0))(lambda: _comm_start(*comm_refs))

        @pl.when((grp == 0) & (i == 0))
        def _():
            for r in (*acc_refs, *shared, refs[s0 + 2]):
                r[...] = jnp.zeros_like(r)

        @pl.when(i == 0)
        def _():
            refs[s0][...] = jnp.zeros_like(refs[s0])
            refs[s0 + 1][...] = jnp.zeros_like(refs[s0 + 1])

        for s in range(sb):
            per_seq = lambda rs: [r.at[s] for r in rs]
            last_of_all = ((grp == n_groups - 1) & (i == seq_chunks - 1)) if s == sb - 1 else None
            one_chunk(last_of_all, *per_seq(refs[:10]), *refs[10:23], refs[o0].at[s], *acc_refs,
                      *per_seq(refs[s0:s0 + 4]), *shared)
        if n_comm:
            pl.when((grp == n_groups - 1) & (i == seq_chunks - 1))(lambda: _comm_finish(*comm_refs))

    def one_chunk(finalize, pu_ref, pv_ref, z_ref, xbc_ref, dt_ref, act_ref, sig_ref, dcat_ref, y_ref, st_ref,
                  gv_ref, ws_ref, bsb_ref, gout_ref, cw_ref, cb_ref, dtb_ref, alog_ref, dx_ref, ng_ref,
                  e_ref, et_ref, ltri_ref,
                  dproj_ref, dws_ref, dbs_ref, dgv_ref, dgout_ref, dng_ref, dcw_ref, dcb_ref, ddtb_ref, dalog_ref,
                  dd_ref, ds_ref, dnext_ref, dcst_ref, cst_ref, dbacc_ref, ddacc_ref):
        tril = _iota2((CH, CH), 0) >= _iota2((CH, CH), 1)
        lane = _iota2((CH, 128), 1)
        row = _iota2((CH, 128), 0)
        dcat_v = dcat_ref[...].astype(_F32)

        pu, pv = pu_ref[...], pv_ref[...]
        gv_v = gv_ref[...]
        y_a, u, keep = _gmlp_fwd_vals(pu, pv, gv_v, ws_ref, bsb_ref, True)
        dy, dgout8 = _rms_bwd(y_a, gout_ref[...], dcat_v[:, 0:GM_W])
        dgout_ref[...] += dgout8
        dus, dvs, dgvs = [], [], []
        for h in range(GM_H):
            sl = slice(h * 128, (h + 1) * 128)
            vh, r, vn, wm, mixed = keep[h + 1]
            dyh = dy[:, sl]
            dus.append(dyh * mixed)
            dmix = dyh * u[:, sl]
            dmix_m = _mx(dmix)
            dws_ref[h] += jnp.where(tril, _dot_nt(dmix_m, _mx(vn)), 0.0)
            dbacc_ref[h] += dmix
            dvn = _dot_tn(wm, dmix_m)
            gy = dvn * gv_v[:, sl]
            nh = vh * r
            dvs.append((gy - nh * jnp.mean(gy * nh, axis=-1, keepdims=True)) * r)
            dgvs.append(_sum8(dvn * nh))
        dgv_ref[...] += jnp.concatenate(dgvs, axis=1)
        cdf_u, cdf_v = keep[0]
        gelu_grad = lambda pre, cdf: cdf + pre * jnp.exp2(pre * pre * (-0.5 * _LOG2E) + _LOG2_INV_SQRT_2PI)
        dproj_ref[:, 0:GM_W] = (jnp.concatenate(dus, axis=1) * gelu_grad(pu, cdf_u)).astype(dproj_ref.dtype)
        dproj_ref[:, GM_W:2 * GM_W] = (jnp.concatenate(dvs, axis=1) * gelu_grad(pv, cdf_v)).astype(dproj_ref.dtype)

        q = _ssd_decay(dt_ref[...], dtb_ref[...], alog_ref[...], e_ref, ltri_ref[...])
        act = act_ref[...]
        xv = act[:, 0:SSD_W]
        dt_x, ecs_x, dec_x, cdec_x = q["dt_x"], q["ecs_x"], q["dec_x"], q["cdec_x"]
        xdt = xv * dt_x
        xdt_m = _mx(xdt)
        cs = q["cs"]
        cst_ref[...] = cs.T
        s_prev = st_ref[...]
        ds = ds_ref[...]
        yv = y_ref[...]
        zv = z_ref[...]
        sig_z = _sigmoid(zv)
        sz = zv * sig_z
        yg = yv * sz
        dygs, dng8 = [], []
        for g in range(SSD_G):
            gs = slice(g * 512, (g + 1) * 512)
            a_, b_ = _rms_bwd(yg[:, gs], ng_ref[:, gs], dcat_v[:, GM_W + g * 512:GM_W + (g + 1) * 512])
            dygs.append(a_)
            dng8.append(b_)
        dyg = jnp.concatenate(dygs, axis=1)
        dng_ref[...] += jnp.concatenate(dng8, axis=1)
        dyv = dyg * sz
        dproj_ref[:, 2 * GM_W:2 * GM_W + SSD_W] = (dyg * yv * _silu_grad(sig_z, sz)).astype(dproj_ref.dtype)
        ddacc_ref[...] += _sum8(dyv * xv)
        dyv_m = _mx(dyv)

        dxdt_parts, db_parts, dc_parts = [], [], []
        dcs = jnp.zeros((CH, 128), _F32)
        dcs_x_parts, ddec_x_parts, dcl_x_parts = [], [], []
        for g in range(SSD_G):
            gs = slice(g * 512, (g + 1) * 512)
            bg = _mx(act[:, SSD_W + g * SSD_N:SSD_W + (g + 1) * SSD_N])
            cg = _mx(act[:, SSD_W + SSD_G * SSD_N + g * SSD_N:SSD_W + SSD_G * SSD_N + (g + 1) * SSD_N])
            cbm = _dot_nt(cg, bg)
            sg = s_prev[:, gs]
            sg_m = _mx(sg)
            dsg = ds[:, gs]
            dsg_m = _mx(dsg)
            zoff = _dot(cg, sg_m)
            dz_off = dyv[:, gs] * ecs_x[:, gs]
            dz_off_m = _mx(dz_off)
            dcs_x_parts.append(dyv[:, gs] * zoff * ecs_x[:, gs])
            dcg = _dot_nt(dz_off_m, sg_m)
            dsprev = _dot_tn(cg, dz_off_m)
            w_st = dec_x[:, gs] * xdt[:, gs]
            dw_st = _dot(bg, dsg_m)
            dbg = _dot_nt(_mx(w_st), dsg_m)
            dxdt_g = dec_x[:, gs] * dw_st
            ddec_x_parts.append(dw_st * xdt[:, gs])
            dsprev = dsprev + cdec_x[:, gs] * dsg
            dcl_x_parts.append(jnp.sum(dsg * sg, axis=0, keepdims=True) * cdec_x[:, gs])
            ds_ref[:, gs] = dsprev
            dcb = jnp.zeros((CH, CH), _F32)
            dxdt_pairs = []
            for pr in range(4):
                ps = slice(g * 512 + pr * 128, g * 512 + (pr + 1) * 128)
                acc_pair = None
                for hh in range(2):
                    h = g * 8 + pr * 2 + hh
                    in_head = (lane < SSD_P) if hh == 0 else (lane >= SSD_P)
                    lm = _head_lm(cs, cst_ref, h, tril)
                    m_h = cbm * lm
                    m_hm = _mx(m_h)
                    dyh_m = _mx(jnp.where(in_head, dyv[:, ps], 0.0))
                    dm = _dot_nt(dyh_m, xdt_m[:, ps])
                    dcb = dcb + dm * lm
                    qm = dm * m_h
                    dcs = dcs + jnp.where(lane == h, jnp.sum(qm, axis=1, keepdims=True), 0.0)
                    dcst_ref[h:h + 1, :] = jnp.sum(qm, axis=0, keepdims=True)
                    contrib = jnp.where(in_head, _dot_tn(m_hm, dyv_m[:, ps]), 0.0)
                    acc_pair = contrib if acc_pair is None else acc_pair + contrib
                dxdt_pairs.append(acc_pair)
            dxdt_parts.append(dxdt_g + jnp.concatenate(dxdt_pairs, axis=1))
            dcb_m = _mx(dcb)
            dc_parts.append(dcg + _dot(dcb_m, bg))
            db_parts.append(dbg + _dot_tn(dcb_m, cg))
        dxdt = jnp.concatenate(dxdt_parts, axis=1)
        dxv = dx_ref[...] * dyv + dxdt * dt_x
        et = et_ref[...]
        head_sum = lambda v: _dot(v.astype(_BF16), et)
        ddt = head_sum(dxdt * xv)
        dcs = dcs - dcst_ref[...].T + head_sum(jnp.concatenate(dcs_x_parts, axis=1))
        ddec = head_sum(jnp.concatenate(ddec_x_parts, axis=1)) * q["dec"]
        dcs = dcs - ddec
        dcl = jnp.sum(ddec, axis=0, keepdims=True) + _xdot(
            _dot, jnp.broadcast_to(jnp.concatenate(dcl_x_parts, axis=1), (8, SSD_W)), et)[0:1, :]
        dcs = jnp.where(row == CH - 1, dcs + dcl, dcs)
        da = _xdot_left(_dot_tn, ltri_ref[...], dcs)
        ddt = ddt + da * q["a_neg"]
        dalog_ref[...] += _sum8(da * q["dt"] * q["a_neg"])
        ddtraw = jnp.where(lane < SSD_H, ddt * _sigmoid(q["dtin"]), 0.0)
        ddtb_ref[...] += _sum8(ddtraw)
        dproj_ref[:, D_IN_PAD - 128:D_IN_PAD] = ddtraw.astype(dproj_ref.dtype)
        dcpre = jnp.concatenate([dxv] + db_parts + dc_parts, axis=1) * _silu_grad(sig_ref[...], act)
        dcb_ref[...] += _sum8(dcpre)
        next8 = dnext_ref[...]
        ups = [_shift_up(dcpre, next8, j) for j in range(CONV_K)]
        xbc = xbc_ref[...]
        for k in range(CONV_K):
            dcw_ref[k:k + 1, :] += jnp.sum(xbc * ups[CONV_K - 1 - k], axis=0, keepdims=True)
        dxbc = sum(cw_ref[k:k + 1, :] * ups[CONV_K - 1 - k] for k in range(CONV_K))
        dproj_ref[:, 2 * GM_W + SSD_W:2 * GM_W + SSD_W + CONV_CH] = dxbc.astype(dproj_ref.dtype)
        dnext_ref[...] = dcpre[0:8, :]

        if finalize is not None:
            @pl.when(finalize)
            def _():
                for h in range(GM_H):
                    dbs_ref[h:h + 1, :] = _xdot_left(_dot_nt, jnp.ones((8, 128), _BF16), dbacc_ref[h])[0:1, :]
                dd_ref[...] = _xdot(_dot, ddacc_ref[...], et)

    rblk = lambda w, j: pl.BlockSpec((sb, CH, w), lambda g, i: (g, seq_chunks - 1 - i, j))
    full = lambda arr: pl.BlockSpec(arr.shape, lambda g, i: (0,) * arr.ndim)
    acc = lambda shape: pl.BlockSpec(shape, lambda g, i: (0,) * len(shape))
    consts = [gv, ws, bsb, gout, cw8, cb, dtb, alog, d_x, ng, e_mat, et_mat, ltri_mat]
    acc_shapes = [(GM_H, CH, CH), (8, 128), (8, GM_W), (8, GM_W), (8, SSD_W), (8, CONV_CH), (8, CONV_CH), (8, 128),
                  (8, 128), (8, 128)]
    res = pl.pallas_call(
        body, name="mixer_bwd", grid=(n_groups, seq_chunks),
        in_specs=[rblk(GM_W, 0), rblk(GM_W, 1), rblk(SSD_W, 2), rblk(CONV_CH, 2), rblk(128, DT_BLK),
                  rblk(CONV_CH, 0), rblk(CONV_CH, 0),
                  rblk(2 * D, 0), rblk(SSD_W, 0), rblk(SSD_W, 0)] + [full(a) for a in consts] + c_in_specs,
        out_specs=[rblk(D_IN_PAD, 0)] + [acc(s) for s in acc_shapes] + c_out_specs,
        out_shape=[jax.ShapeDtypeStruct((n_seq, seq, D_IN_PAD), _MXU)]
        + [jax.ShapeDtypeStruct(s, _F32) for s in acc_shapes] + c_out_shape,
        scratch_shapes=[pltpu.VMEM((sb, SSD_N, SSD_W), _F32), pltpu.VMEM((sb, 8, CONV_CH), _F32),
                        pltpu.VMEM((sb, 128, CH), _F32), pltpu.VMEM((sb, 128, CH), _F32),
                        pltpu.VMEM((GM_H, CH, 128), _F32), pltpu.VMEM((8, SSD_W), _F32)] + c_scratch,
        compiler_params=pltpu.CompilerParams(dimension_semantics=("arbitrary", "arbitrary"),
                                             vmem_limit_bytes=48 << 20),
    )(proj, proj, proj, proj, proj, act, sig, dcat, yss, states, *consts, *c_in)
    return [res[0].reshape(t, D_IN_PAD)] + list(res[1:11]), res[11:]


def _peers():
    x, y, c = lax.axis_index("x"), lax.axis_index("y"), lax.axis_index("c")
    out = []
    for k in range(1, N_DEV):
        fx, fy, fc = (k >> 2) & 1, (k >> 1) & 1, k & 1
        px, py, pc = (x + fx) % 2, (y + fy) % 2, (c + fc) % 2
        out.append((k - 1, (px, py, pc), 4 * px + 2 * py + pc))
    return out, 4 * x + 2 * y + c


def _comm_io(comm):
    any_spec = pl.BlockSpec(memory_space=pl.ANY)
    n = len(comm)
    out_shape = []
    for (kind, axis), src in comm:
        shp = list(src.shape)
        if kind in ("gather", "gather2"):
            shp[axis] *= N_DEV
        else:
            shp[axis] //= N_DEV
            shp = [N_DEV] + shp
        out_shape.append(jax.ShapeDtypeStruct(tuple(shp), src.dtype))
    scratch = [pltpu.SemaphoreType.DMA((n * (N_DEV - 1),)), pltpu.SemaphoreType.DMA((n * (N_DEV - 1),)),
               pltpu.SemaphoreType.DMA((n,))] if n else []
    return [src for _, src in comm], [any_spec] * n, [any_spec] * n, out_shape, scratch


def _window(ref, axis, idx, size):
    start = pl.multiple_of(idx * size, size)
    return ref.at[tuple(pl.ds(start, size) if a == axis else slice(None) for a in range(len(ref.shape)))]


def _comm_plans(kinds, src_refs, dst_refs, send_sems, recv_sems, local_sems):
    x, y, c = lax.axis_index("x"), lax.axis_index("y"), lax.axis_index("c")
    peers, me = _peers()
    plans = []
    for s, ((kind, axis), src, dst) in enumerate(zip(kinds, src_refs, dst_refs)):
        sems = lambda k: dict(send_sem=send_sems.at[s * (N_DEV - 1) + k], recv_sem=recv_sems.at[s * (N_DEV - 1) + k])
        remote = lambda src_ref, dst_ref, k, pid: pltpu.make_async_remote_copy(
            src_ref=src_ref, dst_ref=dst_ref, device_id=pid, device_id_type=_MESH, **sems(k))
        if kind == "gather2":
            size = src.shape[axis]
            win = lambda idx: _window(dst, axis, idx, size)
            sib, sib_idx = (x, y, 1 - c), 4 * x + 2 * y + (1 - c)
            local = pltpu.make_async_copy(src, win(me), local_sems.at[s])
            to_sib = remote(src, win(me), 0, sib)
            starts, forwards = [local, to_sib], []
            waits = [(local, "local"), (to_sib, "send"), (remote(src, win(sib_idx), 0, sib), "recv")]
            for j, (fx, fy) in enumerate(((1, 0), (0, 1), (1, 1))):
                px, py = (x + fx) % 2, (y + fy) % 2
                same, other = 4 * px + 2 * py + c, 4 * px + 2 * py + (1 - c)
                out = remote(src, win(me), 1 + j, (px, py, c))
                starts.append(out)
                passed = remote(win(same), win(same), 4 + j, sib)
                forwards.append((remote(src, win(same), 1 + j, (px, py, c)), passed))
                waits += [(out, "send"), (passed, "send"), (remote(win(other), win(other), 4 + j, sib), "recv")]
            plans.append((starts, forwards, waits))
            continue
        if kind == "gather":
            size = src.shape[axis]
            src_for = lambda pidx: src
            dst_mine = _window(dst, axis, me, size)
        else:
            size = src.shape[axis] // N_DEV
            src_for = lambda pidx: _window(src, axis, pidx, size)
            dst_mine = dst.at[me]
        local = pltpu.make_async_copy(src_for(me), dst_mine, local_sems.at[s])
        remotes = [remote(src_for(pidx), dst_mine, k, pid) for k, pid, pidx in peers]
        plans.append(([local] + remotes, [], [(local, "local")] + [(cp, "both") for cp in remotes]))
    return plans


def _comm_start(*refs):
    for starts, _, _ in _comm_plans(*refs):
        for cp in starts:
            cp.start()


def _comm_finish(*refs):
    for _, forwards, waits in _comm_plans(*refs):
        for arrival, cp in forwards:
            arrival.wait_recv()
            cp.start()
        for cp, what in waits:
            if what == "send":
                cp.wait_send()
            elif what == "recv":
                cp.wait_recv()
            else:
                cp.wait()


def _adam_vals(w, g, m, v):
    m = B1 * m + (1.0 - B1) * g
    v = B2 * v + (1.0 - B2) * (g * g)
    m_hat = m / (1.0 - B1 ** STEP)
    v_hat = v / (1.0 - B2 ** STEP)
    delta = -LR * (m_hat / (jnp.sqrt(v_hat) + ADAM_EPS) + WD * w)
    return delta, m, v


def _sum_adam(name, recv, w, m, v, tile=256):
    _, r, wd = recv.shape
    if r % min(tile, r) == 0:
        tr, tc = min(tile, r), wd
    else:
        tr, tc = r, tile
        assert wd % tc == 0, (name, r, wd)

    def body(recv_ref, w_ref, m_ref, v_ref, g_out, d_out, m_out, v_out):
        g = recv_ref[0].astype(_F32)
        for s in range(1, N_DEV):
            g = g + recv_ref[s].astype(_F32)
        d_, m_, v_ = _adam_vals(w_ref[...], g, m_ref[...], v_ref[...])
        g_out[...] = g
        d_out[...] = d_
        m_out[...] = m_
        v_out[...] = v_

    spec = pl.BlockSpec((tr, tc), lambda i, j: (i, j))
    return pl.pallas_call(
        body, name=name, grid=(r // tr, wd // tc),
        in_specs=[pl.BlockSpec((N_DEV, tr, tc), lambda i, j: (0, i, j)), spec, spec, spec],
        out_specs=[spec] * 4, out_shape=[jax.ShapeDtypeStruct((r, wd), _F32)] * 4,
        compiler_params=pltpu.CompilerParams(dimension_semantics=("parallel", "parallel"),
                                             vmem_limit_bytes=48 << 20),
    )(recv, w, m, v)


def _small_local(parts, segments, n_rows):
    def body(parts_ref, loc_ref):
        loc_ref[...] = jnp.zeros_like(loc_ref)
        for out_row, n_out, in_row, n_in, kind in segments:
            if kind == "copy":
                loc_ref[out_row:out_row + n_out, :] = parts_ref[in_row:in_row + n_in, :]
            else:
                s = jnp.sum(parts_ref[in_row:in_row + n_in, :], axis=0, keepdims=True)
                if kind == "loss":
                    s = jnp.broadcast_to(jnp.sum(s, axis=1, keepdims=True) * (0.5 / D), (1, D))
                loc_ref[out_row:out_row + 1, :] = s

    vm = pl.BlockSpec(memory_space=pltpu.VMEM)
    return pl.pallas_call(body, name="small_local", in_specs=[vm], out_specs=vm,
                          out_shape=jax.ShapeDtypeStruct((n_rows, D), _F32))(parts)


def _small_final(blocks, late8, late_row, w, m, v):
    n_rows = w.shape[0]

    def body(blocks_ref, late_ref, w_ref, m_ref, v_ref, g_out, d_out, m_out, v_out, loc_ref, recv_ref, send_sems,
             recv_sems):
        peers, me = _peers()
        loc_ref[...] = jnp.broadcast_to(jnp.sum(late_ref[...], axis=0, keepdims=True), (8, D))
        recv_ref[me] = loc_ref[...]
        copies = [pltpu.make_async_remote_copy(src_ref=loc_ref, dst_ref=recv_ref.at[me], send_sem=send_sems.at[k],
                                               recv_sem=recv_sems.at[k], device_id=pid, device_id_type=_MESH)
                  for k, pid, _ in peers]
        for cp in copies:
            cp.start()
        g = blocks_ref[0:n_rows, :]
        for s in range(1, N_DEV):
            g = g + blocks_ref[s * n_rows:(s + 1) * n_rows, :]
        for cp in copies:
            cp.wait()
        late = recv_ref[0]
        for s in range(1, N_DEV):
            late = late + recv_ref[s]
        g = jnp.where(_iota2((n_rows, D), 0) == late_row, jnp.broadcast_to(late[0:1, :], (n_rows, D)), g)
        d_, m_, v_ = _adam_vals(w_ref[...], g, m_ref[...], v_ref[...])
        g_out[...] = g
        d_out[...] = d_
        m_out[...] = m_
        v_out[...] = v_

    vm = pl.BlockSpec(memory_space=pltpu.VMEM)
    return pl.pallas_call(
        body, name="small_final", in_specs=[vm] * 5, out_specs=[vm] * 4,
        out_shape=[jax.ShapeDtypeStruct((n_rows, D), _F32)] * 4,
        scratch_shapes=[pltpu.VMEM((8, D), _F32), pltpu.VMEM((N_DEV, 8, D), _F32),
                        pltpu.SemaphoreType.DMA((N_DEV - 1,)), pltpu.SemaphoreType.DMA((N_DEV - 1,))],
        compiler_params=pltpu.CompilerParams(vmem_limit_bytes=48 << 20),
    )(blocks, late8, w, m, v)


_BIG_NAMES =("w_in", "w_out", "w_ff1", "w_ff2", "w_ple_gate", "w_ple_proj")

_G_VECS = ("norm_mix_g", "gm_v_norm_g", "gm_out_norm_g", "ssd_norm_g", "norm_mlp_g", "ple_norm_g", "final_norm_g")
_LATE = _G_VECS[0]


def _const_mats():
    h = np.arange(128)[:, None]
    ch = np.arange(SSD_W)[None, :]
    e = (ch // SSD_P == h).astype(np.float32)
    ltri = (np.arange(CH)[:, None] >= np.arange(CH)[None, :]).astype(np.float32)
    return jnp.asarray(e, _BF16), jnp.asarray(e.T, _BF16), jnp.asarray(ltri, _BF16)


def _pad_lanes(v, n=128):
    return jnp.pad(v, ((0, 0), (0, n - v.shape[1])))


def _local_step(x, p, tgt, shard, conv_w_shard, small, seq_len):
    seq_chunks = seq_len // CH
    e_mat, et_mat, ltri_mat = _const_mats()
    g_mix, g_mlp, g_ple = small["norm_mix_g"], small["norm_mlp_g"], small["ple_norm_g"]
    g_fin = small["final_norm_g"].reshape(1, D)
    gv, gout, ng = small["gm_v_norm_g"], small["gm_out_norm_g"], small["ssd_norm_g"]
    ws = small["gm_ws"][0]
    bsb = jnp.broadcast_to(small["gm_bs"][0][:, :, None], (GM_H, CH, 128))
    cb = small["ssd_conv_b"]
    dtb, alog = _pad_lanes(small["ssd_dt_bias"]), _pad_lanes(small["ssd_a_log"])
    d_x = jnp.repeat(small["ssd_d"], SSD_P, axis=1)

    first = lambda acc: (acc,)
    rows, cols = ("gather2", 0), ("gather2", 1)
    n1, (g_win, g_cw) = _norm_cast("norm_mix", x, g_mix,
                                   comm=[(rows, shard["w_in"][None]), (("gather", 0), conv_w_shard[None])])
    w_in_t = jnp.pad(g_win.reshape(D_IN, D), ((0, D_IN_PAD - D_IN), (0, 0)))
    cw8 = jnp.pad(g_cw.transpose(1, 0, 2).reshape(CONV_K, CONV_CH), ((0, 8 - CONV_K), (0, 0)))
    mix_consts = (gv, ws, bsb, gout, cw8, cb, dtb, alog, d_x, ng)
    (proj,), (w_out, w1) = _matmul("proj_in", n1, w_in_t, "nt", 512, D_IN_PAD, D, first, [("tile", _F32)],
                                   comm=[(rows, shard["w_out"]), (cols, shard["w_ff1"])])
    (cat, yss, states, conv_act, conv_sig), (w2, wg, wp) = _mixer_fwd(
        proj, *mix_consts, e_mat, ltri_mat, seq_chunks,
        comm=[(rows, shard["w_ff2"]), (rows, shard["w_ple_gate"]), (cols, shard["w_ple_proj"])])

    def epi_res_norm(acc, res, g):
        hv = acc + res
        return hv, _rms(hv, g)

    (h1, n2), _ = _matmul("proj_out", cat, w_out, "nn", 1024, D, 2 * D, epi_res_norm,
                          [("tile", _F32), ("tile", _MXU)], extras=[(x, "tile"), (g_mlp, "row")])

    def epi_relu2(acc):
        hid = jnp.maximum(acc, 0.0)
        return hid, hid * hid

    (hid, hid2), _ = _matmul("ff1", n2, w1, "nn", 1024, 1024, D, epi_relu2, [("tile", _MXU), ("tile", _MXU)])
    (h2, n3), _ = _matmul("ff2", hid2, w2, "nn", 512, D, D_FF, epi_res_norm, [("tile", _F32), ("tile", _MXU)],
                          extras=[(h1, "tile"), (g_ple, "row")])

    def epi_norm_bwd(acc, up, hv, g):
        dx, dg8 = _rms_bwd(hv, g, acc)
        dh = up + dx
        return dh, dh, dg8

    def epi_head(acc, wg_v, p_rows, wp_v, h2v, tg, gf, gp_):
        ppv = _dot(_mx(p_rows), wp_v)
        gate = _sigmoid(acc)
        gp = gate * ppv
        h3 = h2v + gp
        r = lax.rsqrt(jnp.mean(h3 * h3, axis=-1, keepdims=True) + EPS)
        nh = h3 * r
        err = nh * gf - tg
        gy = err * (gf * (1.0 / D))
        dh3 = (gy - nh * jnp.mean(gy * nh, axis=-1, keepdims=True)) * r
        dpp = dh3 * gate
        da3 = dpp * (ppv - gp)
        dh2, dh2_again, dgple8 = epi_norm_bwd(_dot_nt(_mx(da3), wg_v), dh3, h2v, gp_)
        return da3, dpp, dh2, dh2_again, _sum8(err * err), _sum8(err * nh) * (1.0 / D), dgple8

    (da3, dpp, dh2, dh2b, lossp, dgfin, dgple), _ = _matmul(
        "ple_gate_loss_bwd", n3, wg, "nn", 512, D, D, epi_head,
        [("tile", _MXU), ("tile", _MXU), ("tile", _F32), ("tile", _MXU), ("part8", _F32), ("part8", _F32),
         ("part8", _F32)],
        extras=[(p, "rows"), (wp, "full"), (h2, "tile"), (tgt, "tile"), (g_fin, "row"), (g_ple, "row")],
        b_to_epilogue=True)

    s_rows, s_cols = ("scatter", 0), ("scatter", 1)
    (dwp,), _ = _matmul("d_w_ple_proj", p, dpp, "tn", D_PLE, D, 2048, first, [("tile", _BF16)])
    (dwg,), _ = _matmul("d_w_ple_gate", n3, da3, "tn", D, D, 2048, first, [("tile", _BF16)])
    (dw2,), (r_wp, r_wg) = _matmul("d_w_ff2", hid2, dh2b, "tn", 1024, D, 4096, first, [("tile", _BF16)],
                                   comm=[(s_cols, dwp), (s_rows, dwg)])
    (da1,), _ = _matmul("d_ff_hidden", dh2b, w2, "nt", 512, 2048, D,
                        lambda acc, hv: (acc * 2.0 * hv.astype(_F32),), [("tile", _MXU)], extras=[(hid, "tile")])
    (dw1,), _ = _matmul("d_w_ff1", n2, da1, "tn", 1024, 1024, 4096, first, [("tile", _BF16)])
    (dh1, dh1b, dgmlp), _ = _matmul(
        "d_h1", da1, w1, "nt", 256, D, D_FF, epi_norm_bwd, [("tile", _F32), ("tile", _MXU), ("part8", _F32)],
        extras=[(dh2, "tile"), (h1, "tile"), (g_mlp, "row")])
    (dwout,), _ = _matmul("d_w_out", cat, dh1b, "tn", 1024, D, 2048, first, [("tile", _BF16)])
    (dcat,), _ = _matmul("d_cat", dh1b, w_out, "nt", 1024, 1024, D, first, [("tile", _MXU)])
    (dproj, dws, dbs, dgv, dgout, dng, dcw, dcb, ddtb, dalog, dd), (r_w2, r_w1, r_wout) = _mixer_bwd(
        proj, conv_act, conv_sig, dcat, yss, states, *mix_consts, e_mat, et_mat, ltri_mat, seq_chunks,
        comm=[(s_rows, dw2), (s_cols, dw1), (s_rows, dwout)])
    pieces = dict(gm_v_norm_g=dgv, gm_out_norm_g=dgout, ssd_norm_g=dng, norm_mlp_g=dgmlp, ple_norm_g=dgple,
                  final_norm_g=dgfin, gm_bs=dbs, ssd_conv_w=dcw, ssd_conv_b=dcb, ssd_dt_bias=ddtb,
                  ssd_a_log=dalog, ssd_d=dd, loss=lossp)
    parts, segments, n_rows, where = _small_layout(pieces)
    small_block = _small_local(parts, segments, n_rows)
    (dwin_t,), (small_blocks, r_ws) = _matmul(
        "d_w_in", n1, dproj, "tn", 512, D_IN_PAD, 1024, lambda acc: (acc.T,), [("tile_t", _BF16)],
        comm=[(("gather", 0), small_block), (("gather", 0), dws.reshape(GM_H * CH, CH))])
    dwin_blocks = dwin_t[:D_IN].reshape(N_DEV, SHARD_IN, D)
    (gx, dgmix), (r_win,) = _matmul(
        "d_x", dproj, w_in_t, "nn", 256, D, D_IN_PAD, lambda *a: epi_norm_bwd(*a)[1:],
        [("tile", _F32), ("part8", _F32)], extras=[(dh1, "tile"), (x, "tile"), (g_mix, "row")],
        comm=[(s_rows, dwin_blocks)])
    r_win = r_win.reshape(N_DEV, SHARD_IN, D)

    big = dict(w_in=r_win, w_out=r_wout, w_ff1=r_w1, w_ff2=r_w2, w_ple_gate=r_wg, w_ple_proj=r_wp,
               gm_ws=r_ws.reshape(N_DEV, GM_H * CH, CH))
    return gx, big, small_blocks, dgmix, n_rows, where


def _small_layout(pieces):
    rows, segments = [], []
    in_row, out_row = 0, 0

    def add(arr, kind, n_out):
        nonlocal in_row, out_row
        rows.append(arr)
        segments.append((out_row, n_out, in_row, arr.shape[0], kind))
        start = out_row
        in_row += arr.shape[0]
        out_row += n_out
        return start

    where = {_LATE: 0}
    out_row = 1
    for name in _G_VECS[1:]:
        where[name] = add(pieces[name], "sum", 1)
    where["gm_bs"] = add(pieces["gm_bs"].reshape(1, D), "copy", 1)
    cb = jnp.pad(pieces["ssd_conv_b"], ((0, 0), (0, 2 * D - CONV_CH)))
    where["ssd_conv_b"] = add(cb[:, :D], "sum", 1)
    add(cb[:, D:], "sum", 1)
    cw = jnp.pad(pieces["ssd_conv_w"][:CONV_K], ((0, 0), (0, 2 * D - CONV_CH)))
    where["ssd_conv_w"] = add(cw.reshape(2 * CONV_K, D), "copy", 2 * CONV_K)
    misc = jnp.concatenate([pieces["ssd_dt_bias"], pieces["ssd_a_log"], pieces["ssd_d"],
                            jnp.zeros((8, D - 3 * 128), _F32)], axis=1)
    where["misc"] = add(misc, "sum", 1)
    where["loss"] = add(pieces["loss"], "loss", 1)
    n_rows = -(-out_row // 8) * 8
    return jnp.concatenate(rows, axis=0), tuple(segments), n_rows, where


def _pack_small_params(vals, where, n_rows, my_block):
    rows, at = [], {}

    def add(name, arr):
        at[name] = sum(r.shape[0] for r in rows)
        rows.append(arr)

    for name in _G_VECS:
        add(name, vals[name].reshape(1, D))
    add("gm_bs", vals["gm_bs"].reshape(1, D))
    cb = jnp.pad(vals["ssd_conv_b"].reshape(1, CONV_CH), ((0, 0), (0, 2 * D - CONV_CH)))
    add("ssd_conv_b", cb.reshape(2, D))
    cw = lax.dynamic_update_slice(jnp.zeros((CONV_K, 2 * D), _F32), vals["ssd_conv_w"].reshape(CONV_K, -1),
                                  (0, my_block * (CONV_CH // N_DEV)))
    add("ssd_conv_w", cw.reshape(2 * CONV_K, D))
    misc = jnp.concatenate([_pad_lanes(vals["ssd_dt_bias"].reshape(1, SSD_H)),
                            _pad_lanes(vals["ssd_a_log"].reshape(1, SSD_H)),
                            _pad_lanes(vals["ssd_d"].reshape(1, SSD_H)), jnp.zeros((1, D - 3 * 128), _F32)], axis=1)
    add("misc", misc)
    assert all(where[k] == r for k, r in at.items()), (where, at)
    rows.append(jnp.zeros((n_rows - sum(r.shape[0] for r in rows), D), _F32))
    return jnp.concatenate(rows, axis=0)


def _unpack_small(buf, where, my_block, shapes):
    out = {}
    for name in _G_VECS:
        out[name] = buf[where[name]].reshape(shapes[name])
    out["gm_bs"] = buf[where["gm_bs"]].reshape(shapes["gm_bs"])
    r = where["ssd_conv_b"]
    out["ssd_conv_b"] = buf[r:r + 2].reshape(1, 2 * D)[:, :CONV_CH].reshape(shapes["ssd_conv_b"])
    r = where["ssd_conv_w"]
    cw = buf[r:r + 2 * CONV_K].reshape(CONV_K, 2 * D)
    out["ssd_conv_w"] = lax.dynamic_slice(cw, (0, my_block * (CONV_CH // N_DEV)),
                                          (CONV_K, CONV_CH // N_DEV)).reshape(shapes["ssd_conv_w"])
    misc = buf[where["misc"]]
    for i, name in enumerate(("ssd_dt_bias", "ssd_a_log", "ssd_d")):
        out[name] = misc[i * 128:i * 128 + SSD_H].reshape(shapes[name])
    return out


_WEIGHTS = ("norm_mix_g", "w_in", "gm_v_norm_g", "gm_ws", "gm_bs", "gm_out_norm_g", "ssd_conv_w", "ssd_conv_b",
            "ssd_dt_bias", "ssd_a_log", "ssd_d", "ssd_norm_g", "w_out", "norm_mlp_g", "w_ff1", "w_ff2", "ple_norm_g",
            "w_ple_gate", "w_ple_proj", "final_norm_g")


def kernel(x, p, norm_mix_g, w_in, gm_v_norm_g, gm_ws, gm_bs, gm_out_norm_g, ssd_conv_w, ssd_conv_b, ssd_dt_bias, ssd_a_log, ssd_d, ssd_norm_g, w_out, norm_mlp_g, w_ff1, w_ff2, ple_norm_g, w_ple_gate, w_ple_proj, final_norm_g, loss_target, m_norm_mix_g, m_w_in, m_gm_v_norm_g, m_gm_ws, m_gm_bs, m_gm_out_norm_g, m_ssd_conv_w, m_ssd_conv_b, m_ssd_dt_bias, m_ssd_a_log, m_ssd_d, m_ssd_norm_g, m_w_out, m_norm_mlp_g, m_w_ff1, m_w_ff2, m_ple_norm_g, m_w_ple_gate, m_w_ple_proj, m_final_norm_g, v_norm_mix_g, v_w_in, v_gm_v_norm_g, v_gm_ws, v_gm_bs, v_gm_out_norm_g, v_ssd_conv_w, v_ssd_conv_b, v_ssd_dt_bias, v_ssd_a_log, v_ssd_d, v_ssd_norm_g, v_w_out, v_norm_mlp_g, v_w_ff1, v_w_ff2, v_ple_norm_g, v_w_ple_gate, v_w_ple_proj, v_final_norm_g):
    args = dict(locals())
    w = {n: args[n] for n in _WEIGHTS}
    m = {n: args["m_" + n] for n in _WEIGHTS}
    v = {n: args["v_" + n] for n in _WEIGHTS}
    shapes = {n: w[n].shape for n in _WEIGHTS}
    my_block = 4 * lax.axis_index("x") + 2 * lax.axis_index("y") + lax.axis_index("c")
    nb, seq_len, _ = x.shape

    def local(d, n):
        if n == "w_in":
            return d[n][0].T
        return d[n][0].reshape(GM_H * CH, CH) if n == "gm_ws" else d[n][0]

    shard = {n: local(w, n).astype(_MXU) for n in _BIG_NAMES}
    small = {n: w[n] for n in _WEIGHTS if n not in _BIG_NAMES}
    gx, recv, small_blocks, late8, n_rows, where = _local_step(
        x.reshape(nb * seq_len, D), p.reshape(nb * seq_len, D_PLE), loss_target.reshape(nb * seq_len, D), shard,
        ssd_conv_w[0], small, seq_len)

    big_out = [{}, {}, {}, {}]
    for n in _BIG_NAMES + ("gm_ws",):
        res = _sum_adam("sum_adam_" + n, recv[n], local(w, n), local(m, n), local(v, n))
        for k in range(4):
            big_out[k][n] = (res[k].T if n == "w_in" else res[k]).reshape(shapes[n])

    packs = [_pack_small_params(d, where, n_rows, my_block) for d in (w, m, v)]
    small_res = _small_final(small_blocks, late8, where[_LATE], *packs)
    loss = small_res[0][where["loss"], 0]
    small_out = [_unpack_small(a, where, my_block, shapes) for a in small_res]

    outs = [loss, gx.reshape(x.shape)]
    for k in range(4):
        outs += [big_out[k][n] if n in big_out[k] else small_out[k][n] for n in _WEIGHTS]
    return tuple(outs)
```

```python
import math

import jax
import jax.numpy as jnp
import numpy as np
from jax import lax
from jax.experimental import pallas as pl
from jax.experimental.pallas import tpu as pltpu

_F32 = jnp.float32
_BF16 = jnp.bfloat16
_MXU = jnp.bfloat16

D = 1024
D_PLE = 256
GM_W = 1024
GM_H = 8
CH = 128
SSD_W = 1024
SSD_H = 16
SSD_P = 64
SSD_G = 2
SSD_N = 128
CONV_K = 4
CONV_CH = SSD_W + 2 * SSD_G * SSD_N
D_FF = 4096
D_IN = 2 * GM_W + SSD_W + CONV_CH + SSD_H
D_IN_PAD = 4736
EPS = 1e-6
DT_BLK = (D_IN_PAD - 128) // 128
N_DEV = 8
SHARD_IN = D_IN // N_DEV

LR, B1, B2, ADAM_EPS, WD, STEP = 0.001, 0.9, 0.999, 1e-08, 0.01, 10
_LOG2E = math.log2(math.e)
_LOG2_INV_SQRT_2PI = -0.5 * math.log2(2.0 * math.pi)

_SEQ_PER_STEP = 2
_V7X_VMEM_BYTES = 64 * 1024 * 1024
_VMEM_CAP = _V7X_VMEM_BYTES - 8 * 1024 * 1024
_MESH = pl.DeviceIdType.MESH


def _vmem_limit(nbytes):
    return int(min(_VMEM_CAP, max(32 * 1024 * 1024, nbytes * 5 // 4 + (4 << 20))))


def _nbytes(shape, dtype):
    return int(np.prod(shape)) * jnp.dtype(dtype).itemsize


def _mx(v):
    return v.astype(_MXU)


def _dot(a, b):
    return jnp.dot(a, b, preferred_element_type=_F32)


def _dot_nt(a, b):
    return lax.dot_general(a, b, (((1,), (1,)), ((), ())), preferred_element_type=_F32)


def _dot_tn(a, b):
    return lax.dot_general(a, b, (((0,), (0,)), ((), ())), preferred_element_type=_F32)


def _split3(a):
    hi = a.astype(_BF16)
    r = a - hi.astype(_F32)
    mid = r.astype(_BF16)
    lo = (r - mid.astype(_F32)).astype(_BF16)
    return hi, mid, lo


def _xdot(dotfn, a, b01):
    b = b01.astype(_BF16)
    hi, mid, lo = _split3(a)
    return (dotfn(hi, b) + dotfn(mid, b)) + dotfn(lo, b)


def _xdot_left(dotfn, a01, b):
    a = a01.astype(_BF16)
    hi, mid, lo = _split3(b)
    return (dotfn(a, hi) + dotfn(a, mid)) + dotfn(a, lo)


def _sum8(v):
    r, n = v.shape
    return v.reshape(r // 8, 8, n).sum(axis=0)


def _sigmoid(v):
    return 1.0 / (1.0 + jnp.exp(-v))


def _rms(xv, g):
    ms = jnp.mean(xv * xv, axis=-1, keepdims=True)
    return xv * lax.rsqrt(ms + EPS) * g


def _rms_bwd(xv, g, dn):
    r = lax.rsqrt(jnp.mean(xv * xv, axis=-1, keepdims=True) + EPS)
    nh = xv * r
    gy = dn * g
    dx = (gy - nh * jnp.mean(gy * nh, axis=-1, keepdims=True)) * r
    return dx, _sum8(dn * nh)


def _iota2(shape, axis):
    return lax.broadcasted_iota(jnp.int32, shape, axis)


def _norm_cast(name, x, g, tm=512, comm=()):
    t, n = x.shape
    tm = min(tm, t)
    steps = t // tm
    kinds = [kind for kind, _ in comm]
    c_in, c_in_specs, c_out_specs, c_out_shape, c_scratch = _comm_io(comm)

    def body(*refs):
        x_ref, g_ref = refs[0], refs[1]
        o_ref = refs[2 + len(comm)]
        comm_refs = (kinds, refs[2:2 + len(comm)], refs[3 + len(comm):3 + 2 * len(comm)], *refs[3 + 2 * len(comm):])
        if comm:
            pl.when(pl.program_id(0) == 0)(lambda: _comm_start(*comm_refs))

        o_ref[...] = _rms(x_ref[...], g_ref[...]).astype(o_ref.dtype)
        if comm:
            pl.when(pl.program_id(0) == steps - 1)(lambda: _comm_finish(*comm_refs))

    res = pl.pallas_call(
        body, name=name, grid=(steps,),
        in_specs=[pl.BlockSpec((tm, n), lambda i: (i, 0)), pl.BlockSpec((1, n), lambda i: (0, 0))] + c_in_specs,
        out_specs=[pl.BlockSpec((tm, n), lambda i: (i, 0))] + c_out_specs,
        out_shape=[jax.ShapeDtypeStruct((t, n), _MXU)] + c_out_shape, scratch_shapes=c_scratch,
        compiler_params=pltpu.CompilerParams(dimension_semantics=("arbitrary",)),
    )(x, g, *c_in)
    return res[0], res[1:]


def _matmul(name, a, b, mode, tm, tn, tk, epilogue, outs, extras=(), comm=(), b_to_epilogue=False):
    m, k = a.shape[::-1] if mode == "tn" else a.shape
    n = b.shape[0] if mode == "nt" else b.shape[1]
    tm, tn, tk = min(tm, m), min(tn, n), min(tk, k)
    assert m % tm == 0 and n % tn == 0 and k % tk == 0, (name, m, n, k, tm, tn, tk)
    if mode == "nn":
        a_spec = pl.BlockSpec((tm, tk), lambda i, j, kk: (i, kk))
        b_spec = pl.BlockSpec((tk, tn), lambda i, j, kk: (kk, j))
        dotfn = _dot
    elif mode == "nt":
        a_spec = pl.BlockSpec((tm, tk), lambda i, j, kk: (i, kk))
        b_spec = pl.BlockSpec((tn, tk), lambda i, j, kk: (j, kk))
        dotfn = _dot_nt
    else:
        a_spec = pl.BlockSpec((tk, tm), lambda i, j, kk: (kk, i))
        b_spec = pl.BlockSpec((tk, tn), lambda i, j, kk: (kk, j))
        dotfn = _dot_tn
    ni, nj, nk = m // tm, n // tn, k // tk
    n_ex, n_out, n_comm = len(extras), len(outs), len(comm)
    kinds = [kind for kind, _ in comm]
    c_in, c_in_specs, c_out_specs, c_out_shape, c_scratch = _comm_io(comm)

    in_specs, vmem = [a_spec, b_spec], 2 * (tm * tk * a.dtype.itemsize + tk * tn * b.dtype.itemsize)
    for arr, kind in extras:
        if kind == "tile":
            in_specs.append(pl.BlockSpec((tm, tn), lambda i, j, kk: (i, j)))
            vmem += 2 * _nbytes((tm, tn), arr.dtype)
        elif kind == "rows":
            in_specs.append(pl.BlockSpec((tm, arr.shape[1]), lambda i, j, kk: (i, 0)))
            vmem += 2 * _nbytes((tm, arr.shape[1]), arr.dtype)
        elif kind == "full":
            in_specs.append(pl.BlockSpec(arr.shape, lambda i, j, kk: (0,) * arr.ndim))
            vmem += 2 * _nbytes(arr.shape, arr.dtype)
        else:
            in_specs.append(pl.BlockSpec((1, tn), lambda i, j, kk: (0, j)))
    out_specs, out_shape = [], []
    for kind, dt in outs:
        if kind == "tile":
            out_specs.append(pl.BlockSpec((tm, tn), lambda i, j, kk: (i, j)))
            out_shape.append(jax.ShapeDtypeStruct((m, n), dt))
            vmem += 2 * _nbytes((tm, tn), dt)
        elif kind == "tile_t":
            out_specs.append(pl.BlockSpec((tn, tm), lambda i, j, kk: (j, i)))
            out_shape.append(jax.ShapeDtypeStruct((n, m), dt))
            vmem += 2 * _nbytes((tm, tn), dt)
        else:
            assert nj == 1, "the partial-sum rows are accumulated over consecutive row tiles"
            out_specs.append(pl.BlockSpec((8, tn), lambda i, j, kk: (0, 0)))
            out_shape.append(jax.ShapeDtypeStruct((8, n), dt))
    scratch = [pltpu.VMEM((tm, tn), _F32)] if nk > 1 else []
    vmem += _nbytes((tm, tn), _F32) * 2

    def body(*refs):
        a_ref, b_ref = refs[0], refs[1]
        ex_refs = refs[2:2 + n_ex]
        n_in = 2 + n_ex + n_comm
        out_refs = refs[n_in:n_in + n_out]
        i, j, kk = pl.program_id(0), pl.program_id(1), pl.program_id(2)
        comm_refs = (kinds, refs[2 + n_ex:n_in], refs[n_in + n_out:n_in + n_out + n_comm], *refs[len(refs) - 3:])
        if n_comm:
            pl.when((i == 0) & (j == 0) & (kk == 0))(lambda: _comm_start(*comm_refs))

        b_val = _mx(b_ref[...])
        part = dotfn(_mx(a_ref[...]), b_val)

        def finish(acc):
            vals = epilogue(acc, *([b_val] if b_to_epilogue else []), *[r[...] for r in ex_refs])
            for r, v, (kind, _) in zip(out_refs, vals, outs):
                if kind == "part8":
                    @pl.when(i == 0)
                    def _():
                        r[...] = v

                    @pl.when(i > 0)
                    def _():
                        r[...] += v
                else:
                    r[...] = v.astype(r.dtype)

        if nk == 1:
            finish(part)
        else:
            acc_ref = refs[n_in + n_out + n_comm]

            @pl.when(kk == 0)
            def _():
                acc_ref[...] = part

            @pl.when(kk > 0)
            def _():
                acc_ref[...] += part

            @pl.when(kk == nk - 1)
            def _():
                finish(acc_ref[...])

        if n_comm:
            pl.when((i == ni - 1) & (j == nj - 1) & (kk == nk - 1))(lambda: _comm_finish(*comm_refs))

    carried =n_comm or any(kind == "part8" for kind, _ in outs)
    sem = ("arbitrary",) * 3 if carried else ("parallel", "parallel", "arbitrary")
    res = pl.pallas_call(
        body, name=name, grid=(ni, nj, nk),
        in_specs=in_specs + c_in_specs, out_specs=out_specs + c_out_specs, out_shape=out_shape + c_out_shape,
        scratch_shapes=scratch + c_scratch,
        compiler_params=pltpu.CompilerParams(dimension_semantics=sem, vmem_limit_bytes=_vmem_limit(vmem)),
    )(a, b, *[arr for arr, _ in extras], *c_in)
    return res[:n_out], res[n_out:]


def _shift_down(v, halo8, j):
    if j == 0:
        return v
    r = pltpu.roll(v, j, axis=0)
    hr = pltpu.roll(halo8, j, axis=0)
    top = jnp.where(_iota2(hr.shape, 0) < j, hr, r[:8])
    return jnp.concatenate([top, r[8:]], axis=0)


def _shift_up(v, next8, j):
    if j == 0:
        return v
    rows = v.shape[0]
    r = pltpu.roll(v, rows - j, axis=0)
    nr = pltpu.roll(next8, 8 - j, axis=0)
    bot = jnp.where(_iota2(nr.shape, 0) >= 8 - j, nr, r[rows - 8:])
    return jnp.concatenate([r[:rows - 8], bot], axis=0)


def _silu_grad(sig, silu):
    return sig + silu * (1.0 - sig)


def _gmlp_fwd_vals(pu, pv, gv, ws_ref, bsb_ref, want_bwd):
    tril = _iota2((CH, CH), 0) >= _iota2((CH, CH), 1)
    cdf_u = 0.5 * (1.0 + lax.erf(pu * 0.7071067811865476))
    cdf_v = 0.5 * (1.0 + lax.erf(pv * 0.7071067811865476))
    u = pu * cdf_u
    v = pv * cdf_v
    ys, keep = [], [(cdf_u, cdf_v)] if want_bwd else []
    for h in range(GM_H):
        sl = slice(h * 128, (h + 1) * 128)
        vh = v[:, sl]
        r = lax.rsqrt(jnp.mean(vh * vh, axis=-1, keepdims=True) + EPS)
        vn = vh * r * gv[:, sl]
        wm = _mx(jnp.where(tril, ws_ref[h], 0.0))
        mixed = _dot(wm, _mx(vn)) + bsb_ref[h]
        ys.append(u[:, sl] * mixed)
        if want_bwd:
            keep.append((vh, r, vn, wm, mixed))
    return jnp.concatenate(ys, axis=1), u, keep


def _ssd_conv(xbc, halo8, cw_ref, cb):
    cpre = cb + sum(cw_ref[k:k + 1, :] * _shift_down(xbc, halo8, CONV_K - 1 - k) for k in range(CONV_K))
    sig = _sigmoid(cpre)
    return sig, cpre * sig


def _ssd_decay(dtraw, dtb, alog, e_ref, ltri):
    dtin = dtraw + dtb
    dt = jnp.maximum(dtin, 0.0) + jnp.log(1.0 + jnp.exp(-jnp.abs(dtin)))
    a_neg = -jnp.exp(alog)
    cs = _xdot_left(_dot, ltri, dt * a_neg)
    cs_last = cs[CH - 1:CH, :]
    ecs = jnp.exp(cs)
    dec = jnp.exp(cs_last - cs)
    cdec = jnp.exp(cs_last)
    e = e_ref[...]
    dt_x = _dot(dt.astype(_BF16), e)
    ecs_x = _dot(ecs.astype(_BF16), e)
    dec_x = _dot(dec.astype(_BF16), e)
    cdec_x = _xdot(_dot, jnp.broadcast_to(cdec, (8, 128)), e)[0:1, :]
    return dict(dtin=dtin, dt=dt, a_neg=a_neg, cs=cs, ecs=ecs, dec=dec, cdec=cdec,
                dt_x=dt_x, ecs_x=ecs_x, dec_x=dec_x, cdec_x=cdec_x)


def _head_lm(cs, cst_ref, h, tril):
    seg = jnp.broadcast_to(cs[:, h:h + 1], (CH, CH)) - cst_ref[h:h + 1, :]
    return jnp.exp(jnp.where(tril, seg, -jnp.inf))


def _mixer_fwd(proj, gv, ws, bsb, gout, cw8, cb, dtb, alog, d_x, ng, e_mat, ltri_mat, seq_chunks, comm=()):
    t = proj.shape[0]
    n_seq = t // (seq_chunks * CH)
    sb = math.gcd(_SEQ_PER_STEP, n_seq)
    n_groups = n_seq // sb
    proj = proj.reshape(n_seq, seq_chunks * CH, proj.shape[1])
    n_comm = len(comm)
    kinds = [kind for kind, _ in comm]
    c_in, c_in_specs, c_out_specs, c_out_shape, c_scratch = _comm_io(comm)

    def body(*refs):
        comm_refs = (kinds, refs[18:18 + n_comm], refs[23 + n_comm:23 + 2 * n_comm], *refs[25 + 2 * n_comm:])
        grp, c = pl.program_id(0), pl.program_id(1)
        if n_comm:
            pl.when((grp == 0) & (c == 0))(lambda: _comm_start(*comm_refs))
            pl.when((grp == n_groups - 1) & (c == seq_chunks - 1))(lambda: _comm_finish(*comm_refs))
        for s in range(sb):
            per_seq = lambda rs: [r.at[s] for r in rs]
            one_chunk(c == 0, *per_seq(refs[:6]), *refs[6:18], *per_seq(refs[18 + n_comm:23 + n_comm]),
                      *per_seq(refs[23 + 2 * n_comm:25 + 2 * n_comm]))

    def one_chunk(first, pu_ref, pv_ref, z_ref, xbc_ref, dt_ref, halo_ref, gv_ref, ws_ref, bsb_ref, gout_ref, cw_ref,
                  cb_ref, dtb_ref, alog_ref, dx_ref, ng_ref, e_ref, ltri_ref, cat_ref, y_ref, st_ref, act_ref, sig_ref,
                  s_ref, cst_ref):
        tril = _iota2((CH, CH), 0) >= _iota2((CH, CH), 1)
        lane = _iota2((CH, 128), 1)

        y_a, _, _ = _gmlp_fwd_vals(pu_ref[...], pv_ref[...], gv_ref[...], ws_ref, bsb_ref, False)
        cat_ref[:, 0:GM_W] = _rms(y_a, gout_ref[...]).astype(cat_ref.dtype)

        @pl.when(first)
        def _():
            s_ref[...] = jnp.zeros_like(s_ref)

        halo8 = jnp.where(first, 0.0, halo_ref[...])
        sig, act = _ssd_conv(xbc_ref[...], halo8, cw_ref, cb_ref[...])
        sig_ref[...] = sig
        act_ref[...] = act
        q = _ssd_decay(dt_ref[...], dtb_ref[...], alog_ref[...], e_ref, ltri_ref[...])
        xv = act[:, 0:SSD_W]
        xdt = xv * q["dt_x"]
        xdt_m = _mx(xdt)
        cs = q["cs"]
        cst_ref[...] = cs.T
        s_prev = s_ref[...]
        st_ref[...] = s_prev
        ys = []
        for g in range(SSD_G):
            bg = _mx(act[:, SSD_W + g * SSD_N:SSD_W + (g + 1) * SSD_N])
            cg = _mx(act[:, SSD_W + SSD_G * SSD_N + g * SSD_N:SSD_W + SSD_G * SSD_N + (g + 1) * SSD_N])
            cbm = _dot_nt(cg, bg)
            gs = slice(g * 512, (g + 1) * 512)
            for pr in range(4):
                ps = slice(g * 512 + pr * 128, g * 512 + (pr + 1) * 128)
                o = []
                for hh in range(2):
                    h = g * 8 + pr * 2 + hh
                    m_h = _mx(cbm * _head_lm(cs, cst_ref, h, tril))
                    o.append(_dot(m_h, xdt_m[:, ps]))
                ys.append(jnp.where(lane < SSD_P, o[0], o[1]))
            sg = s_prev[:, gs]
            yoff = _dot(cg, _mx(sg)) * q["ecs_x"][:, gs]
            ys[-4:] = [ys[-4 + i] + yoff[:, i * 128:(i + 1) * 128] for i in range(4)]
            st_new = _dot_tn(bg, _mx(q["dec_x"][:, gs] * xdt[:, gs]))
            s_ref[:, gs] = sg * q["cdec_x"][:, gs] + st_new
        y = jnp.concatenate(ys, axis=1) + dx_ref[...] * xv
        y_ref[...] = y
        zv = z_ref[...]
        yg = y * (zv * _sigmoid(zv))
        for g in range(SSD_G):
            gs = slice(g * 512, (g + 1) * 512)
            cat_ref[:, GM_W + g * 512:GM_W + (g + 1) * 512] = _rms(yg[:, gs], ng_ref[:, gs]).astype(cat_ref.dtype)

    blk = lambda w, j: pl.BlockSpec((sb, CH, w), lambda g, c: (g, c, j))
    full = lambda arr: pl.BlockSpec(arr.shape, lambda g, c: (0,) * arr.ndim)
    consts = [gv, ws, bsb, gout, cw8, cb, dtb, alog, d_x, ng, e_mat, ltri_mat]
    seq = seq_chunks * CH
    res = pl.pallas_call(
        body, name="mixer_fwd", grid=(n_groups, seq_chunks),
        in_specs=[blk(GM_W, 0), blk(GM_W, 1), blk(SSD_W, 2), blk(CONV_CH, 2), blk(128, DT_BLK),
                  pl.BlockSpec((sb, 8, CONV_CH), lambda g, c: (g, jnp.maximum(c * (CH // 8) - 1, 0), 2))]
        + [full(a) for a in consts] + c_in_specs,
        out_specs=[blk(2 * D, 0), blk(SSD_W, 0), blk(SSD_W, 0), blk(CONV_CH, 0), blk(CONV_CH, 0)] + c_out_specs,
        out_shape=[jax.ShapeDtypeStruct((n_seq, seq, 2 * D), _MXU), jax.ShapeDtypeStruct((n_seq, seq, SSD_W), _F32),
                   jax.ShapeDtypeStruct((n_seq, seq, SSD_W), _F32), jax.ShapeDtypeStruct((n_seq, seq, CONV_CH), _F32),
                   jax.ShapeDtypeStruct((n_seq, seq, CONV_CH), _F32)] + c_out_shape,
        scratch_shapes=[pltpu.VMEM((sb, SSD_N, SSD_W), _F32), pltpu.VMEM((sb, 128, CH), _F32)] + c_scratch,
        compiler_params=pltpu.CompilerParams(dimension_semantics=("arbitrary", "arbitrary"),
                                             vmem_limit_bytes=48 << 20),
    )(proj, proj, proj, proj, proj, proj, *consts, *c_in)
    return [r.reshape(t, r.shape[-1]) for r in res[:5]], res[5:]


def _mixer_bwd(proj, act, sig, dcat, yss, states, gv, ws, bsb, gout, cw8, cb, dtb, alog, d_x, ng, e_mat, et_mat,
               ltri_mat, seq_chunks, comm=()):
    t = proj.shape[0]
    n_seq = t // (seq_chunks * CH)
    sb = math.gcd(_SEQ_PER_STEP, n_seq)
    n_groups = n_seq // sb
    seq = seq_chunks * CH
    proj, act, sig, dcat, yss, states = [a.reshape(n_seq, seq, a.shape[1])
                                         for a in (proj, act, sig, dcat, yss, states)]
    n_comm = len(comm)
    kinds = [kind for kind, _ in comm]
    c_in, c_in_specs, c_out_specs, c_out_shape, c_scratch = _comm_io(comm)

    def body(*refs):
        o0, s0 = 23 + n_comm, 34 + 2 * n_comm
        acc_refs, shared = refs[o0 + 1:o0 + 11], refs[s0 + 4:s0 + 6]
        comm_refs = (kinds, refs[23:23 + n_comm], refs[o0 + 11:o0 + 11 + n_comm], *refs[s0 + 6:])
        grp, i = pl.program_id(0), pl.program_id(1)
        if n_comm:
            pl.when((grp == 0) & (i == 0))(lambda: _comm_start(*comm_refs))

        @pl.when((grp == 0) & (i == 0))
        def _():
            for r in (*acc_refs, *shared, refs[s0 + 2]):
                r[...] = jnp.zeros_like(r)

        @pl.when(i == 0)
        def _():
            refs[s0][...] = jnp.zeros_like(refs[s0])
            refs[s0 + 1][...] = jnp.zeros_like(refs[s0 + 1])

        for s in range(sb):
            per_seq = lambda rs: [r.at[s] for r in rs]
            last_of_all = ((grp == n_groups - 1) & (i == seq_chunks - 1)) if s == sb - 1 else None
            one_chunk(last_of_all, *per_seq(refs[:10]), *refs[10:23], refs[o0].at[s], *acc_refs,
                      *per_seq(refs[s0:s0 + 4]), *shared)
        if n_comm:
            pl.when((grp == n_groups - 1) & (i == seq_chunks - 1))(lambda: _comm_finish(*comm_refs))

    def one_chunk(finalize, pu_ref, pv_ref, z_ref, xbc_ref, dt_ref, act_ref, sig_ref, dcat_ref, y_ref, st_ref,
                  gv_ref, ws_ref, bsb_ref, gout_ref, cw_ref, cb_ref, dtb_ref, alog_ref, dx_ref, ng_ref,
                  e_ref, et_ref, ltri_ref,
                  dproj_ref, dws_ref, dbs_ref, dgv_ref, dgout_ref, dng_ref, dcw_ref, dcb_ref, ddtb_ref, dalog_ref,
                  dd_ref, ds_ref, dnext_ref, dcst_ref, cst_ref, dbacc_ref, ddacc_ref):
        tril = _iota2((CH, CH), 0) >= _iota2((CH, CH), 1)
        lane = _iota2((CH, 128), 1)
        row = _iota2((CH, 128), 0)
        dcat_v = dcat_ref[...].astype(_F32)

        pu, pv = pu_ref[...], pv_ref[...]
        gv_v = gv_ref[...]
        y_a, u, keep = _gmlp_fwd_vals(pu, pv, gv_v, ws_ref, bsb_ref, True)
        dy, dgout8 = _rms_bwd(y_a, gout_ref[...], dcat_v[:, 0:GM_W])
        dgout_ref[...] += dgout8
        dus, dvs, dgvs = [], [], []
        for h in range(GM_H):
            sl = slice(h * 128, (h + 1) * 128)
            vh, r, vn, wm, mixed = keep[h + 1]
            dyh = dy[:, sl]
            dus.append(dyh * mixed)
            dmix = dyh * u[:, sl]
            dmix_m = _mx(dmix)
            dws_ref[h] += jnp.where(tril, _dot_nt(dmix_m, _mx(vn)), 0.0)
            dbacc_ref[h] += dmix
            dvn = _dot_tn(wm, dmix_m)
            gy = dvn * gv_v[:, sl]
            nh = vh * r
            dvs.append((gy - nh * jnp.mean(gy * nh, axis=-1, keepdims=True)) * r)
            dgvs.append(_sum8(dvn * nh))
        dgv_ref[...] += jnp.concatenate(dgvs, axis=1)
        cdf_u, cdf_v = keep[0]
        gelu_grad = lambda pre, cdf: cdf + pre * jnp.exp2(pre * pre * (-0.5 * _LOG2E) + _LOG2_INV_SQRT_2PI)
        dproj_ref[:, 0:GM_W] = (jnp.concatenate(dus, axis=1) * gelu_grad(pu, cdf_u)).astype(dproj_ref.dtype)
        dproj_ref[:, GM_W:2 * GM_W] = (jnp.concatenate(dvs, axis=1) * gelu_grad(pv, cdf_v)).astype(dproj_ref.dtype)

        q = _ssd_decay(dt_ref[...], dtb_ref[...], alog_ref[...], e_ref, ltri_ref[...])
        act = act_ref[...]
        xv = act[:, 0:SSD_W]
        dt_x, ecs_x, dec_x, cdec_x = q["dt_x"], q["ecs_x"], q["dec_x"], q["cdec_x"]
        xdt = xv * dt_x
        xdt_m = _mx(xdt)
        cs = q["cs"]
        cst_ref[...] = cs.T
        s_prev = st_ref[...]
        ds = ds_ref[...]
        yv = y_ref[...]
        zv = z_ref[...]
        sig_z = _sigmoid(zv)
        sz = zv * sig_z
        yg = yv * sz
        dygs, dng8 = [], []
        for g in range(SSD_G):
            gs = slice(g * 512, (g + 1) * 512)
            a_, b_ = _rms_bwd(yg[:, gs], ng_ref[:, gs], dcat_v[:, GM_W + g * 512:GM_W + (g + 1) * 512])
            dygs.append(a_)
            dng8.append(b_)
        dyg = jnp.concatenate(dygs, axis=1)
        dng_ref[...] += jnp.concatenate(dng8, axis=1)
        dyv = dyg * sz
        dproj_ref[:, 2 * GM_W:2 * GM_W + SSD_W] = (dyg * yv * _silu_grad(sig_z, sz)).astype(dproj_ref.dtype)
        ddacc_ref[...] += _sum8(dyv * xv)
        dyv_m = _mx(dyv)

        dxdt_parts, db_parts, dc_parts = [], [], []
        dcs = jnp.zeros((CH, 128), _F32)
        dcs_x_parts, ddec_x_parts, dcl_x_parts = [], [], []
        for g in range(SSD_G):
            gs = slice(g * 512, (g + 1) * 512)
            bg = _mx(act[:, SSD_W + g * SSD_N:SSD_W + (g + 1) * SSD_N])
            cg = _mx(act[:, SSD_W + SSD_G * SSD_N + g * SSD_N:SSD_W + SSD_G * SSD_N + (g + 1) * SSD_N])
            cbm = _dot_nt(cg, bg)
            sg = s_prev[:, gs]
            sg_m = _mx(sg)
            dsg = ds[:, gs]
            dsg_m = _mx(dsg)
            zoff = _dot(cg, sg_m)
            dz_off = dyv[:, gs] * ecs_x[:, gs]
            dz_off_m = _mx(dz_off)
            dcs_x_parts.append(dyv[:, gs] * zoff * ecs_x[:, gs])
            dcg = _dot_nt(dz_off_m, sg_m)
            dsprev = _dot_tn(cg, dz_off_m)
            w_st = dec_x[:, gs] * xdt[:, gs]
            dw_st = _dot(bg, dsg_m)
            dbg = _dot_nt(_mx(w_st), dsg_m)
            dxdt_g = dec_x[:, gs] * dw_st
            ddec_x_parts.append(dw_st * xdt[:, gs])
            dsprev = dsprev + cdec_x[:, gs] * dsg
            dcl_x_parts.append(jnp.sum(dsg * sg, axis=0, keepdims=True) * cdec_x[:, gs])
            ds_ref[:, gs] = dsprev
            dcb = jnp.zeros((CH, CH), _F32)
            dxdt_pairs = []
            for pr in range(4):
                ps = slice(g * 512 + pr * 128, g * 512 + (pr + 1) * 128)
                acc_pair = None
                for hh in range(2):
                    h = g * 8 + pr * 2 + hh
                    in_head = (lane < SSD_P) if hh == 0 else (lane >= SSD_P)
                    lm = _head_lm(cs, cst_ref, h, tril)
                    m_h = cbm * lm
                    m_hm = _mx(m_h)
                    dyh_m = _mx(jnp.where(in_head, dyv[:, ps], 0.0))
                    dm = _dot_nt(dyh_m, xdt_m[:, ps])
                    dcb = dcb + dm * lm
                    qm = dm * m_h
                    dcs = dcs + jnp.where(lane == h, jnp.sum(qm, axis=1, keepdims=True), 0.0)
                    dcst_ref[h:h + 1, :] = jnp.sum(qm, axis=0, keepdims=True)
                    contrib = jnp.where(in_head, _dot_tn(m_hm, dyv_m[:, ps]), 0.0)
                    acc_pair = contrib if acc_pair is None else acc_pair + contrib
                dxdt_pairs.append(acc_pair)
            dxdt_parts.append(dxdt_g + jnp.concatenate(dxdt_pairs, axis=1))
            dcb_m = _mx(dcb)
            dc_parts.append(dcg + _dot(dcb_m, bg))
            db_parts.append(dbg + _dot_tn(dcb_m, cg))
        dxdt = jnp.concatenate(dxdt_parts, axis=1)
        dxv = dx_ref[...] * dyv + dxdt * dt_x
        et = et_ref[...]
        head_sum = lambda v: _dot(v.astype(_BF16), et)
        ddt = head_sum(dxdt * xv)
        dcs = dcs - dcst_ref[...].T + head_sum(jnp.concatenate(dcs_x_parts, axis=1))
        ddec = head_sum(jnp.concatenate(ddec_x_parts, axis=1)) * q["dec"]
        dcs = dcs - ddec
        dcl = jnp.sum(ddec, axis=0, keepdims=True) + _xdot(
            _dot, jnp.broadcast_to(jnp.concatenate(dcl_x_parts, axis=1), (8, SSD_W)), et)[0:1, :]
        dcs = jnp.where(row == CH - 1, dcs + dcl, dcs)
        da = _xdot_left(_dot_tn, ltri_ref[...], dcs)
        ddt = ddt + da * q["a_neg"]
        dalog_ref[...] += _sum8(da * q["dt"] * q["a_neg"])
        ddtraw = jnp.where(lane < SSD_H, ddt * _sigmoid(q["dtin"]), 0.0)
        ddtb_ref[...] += _sum8(ddtraw)
        dproj_ref[:, D_IN_PAD - 128:D_IN_PAD] = ddtraw.astype(dproj_ref.dtype)
        dcpre = jnp.concatenate([dxv] + db_parts + dc_parts, axis=1) * _silu_grad(sig_ref[...], act)
        dcb_ref[...] += _sum8(dcpre)
        next8 = dnext_ref[...]
        ups = [_shift_up(dcpre, next8, j) for j in range(CONV_K)]
        xbc = xbc_ref[...]
        for k in range(CONV_K):
            dcw_ref[k:k + 1, :] += jnp.sum(xbc * ups[CONV_K - 1 - k], axis=0, keepdims=True)
        dxbc = sum(cw_ref[k:k + 1, :] * ups[CONV_K - 1 - k] for k in range(CONV_K))
        dproj_ref[:, 2 * GM_W + SSD_W:2 * GM_W + SSD_W + CONV_CH] = dxbc.astype(dproj_ref.dtype)
        dnext_ref[...] = dcpre[0:8, :]

        if finalize is not None:
            @pl.when(finalize)
            def _():
                for h in range(GM_H):
                    dbs_ref[h:h + 1, :] = _xdot_left(_dot_nt, jnp.ones((8, 128), _BF16), dbacc_ref[h])[0:1, :]
                dd_ref[...] = _xdot(_dot, ddacc_ref[...], et)

    rblk = lambda w, j: pl.BlockSpec((sb, CH, w), lambda g, i: (g, seq_chunks - 1 - i, j))
    full = lambda arr: pl.BlockSpec(arr.shape, lambda g, i: (0,) * arr.ndim)
    acc = lambda shape: pl.BlockSpec(shape, lambda g, i: (0,) * len(shape))
    consts = [gv, ws, bsb, gout, cw8, cb, dtb, alog, d_x, ng, e_mat, et_mat, ltri_mat]
    acc_shapes = [(GM_H, CH, CH), (8, 128), (8, GM_W), (8, GM_W), (8, SSD_W), (8, CONV_CH), (8, CONV_CH), (8, 128),
                  (8, 128), (8, 128)]
    res = pl.pallas_call(
        body, name="mixer_bwd", grid=(n_groups, seq_chunks),
        in_specs=[rblk(GM_W, 0), rblk(GM_W, 1), rblk(SSD_W, 2), rblk(CONV_CH, 2), rblk(128, DT_BLK),
                  rblk(CONV_CH, 0), rblk(CONV_CH, 0),
                  rblk(2 * D, 0), rblk(SSD_W, 0), rblk(SSD_W, 0)] + [full(a) for a in consts] + c_in_specs,
        out_specs=[rblk(D_IN_PAD, 0)] + [acc(s) for s in acc_shapes] + c_out_specs,
        out_shape=[jax.ShapeDtypeStruct((n_seq, seq, D_IN_PAD), _MXU)]
        + [jax.ShapeDtypeStruct(s, _F32) for s in acc_shapes] + c_out_shape,
        scratch_shapes=[pltpu.VMEM((sb, SSD_N, SSD_W), _F32), pltpu.VMEM((sb, 8, CONV_CH), _F32),
                        pltpu.VMEM((sb, 128, CH), _F32), pltpu.VMEM((sb, 128, CH), _F32),
                        pltpu.VMEM((GM_H, CH, 128), _F32), pltpu.VMEM((8, SSD_W), _F32)] + c_scratch,
        compiler_params=pltpu.CompilerParams(dimension_semantics=("arbitrary", "arbitrary"),
                                             vmem_limit_bytes=48 << 20),
    )(proj, proj, proj, proj, proj, act, sig, dcat, yss, states, *consts, *c_in)
    return [res[0].reshape(t, D_IN_PAD)] + list(res[1:11]), res[11:]


def _peers():
    x, y, c = lax.axis_index("x"), lax.axis_index("y"), lax.axis_index("c")
    out = []
    for k in range(1, N_DEV):
        fx, fy, fc = (k >> 2) & 1, (k >> 1) & 1, k & 1
        px, py, pc = (x + fx) % 2, (y + fy) % 2, (c + fc) % 2
        out.append((k - 1, (px, py, pc), 4 * px + 2 * py + pc))
    return out, 4 * x + 2 * y + c


def _comm_io(comm):
    any_spec = pl.BlockSpec(memory_space=pl.ANY)
    n = len(comm)
    out_shape = []
    for (kind, axis), src in comm:
        shp = list(src.shape)
        if kind in ("gather", "gather2"):
            shp[axis] *= N_DEV
        else:
            shp[axis] //= N_DEV
            shp = [N_DEV] + shp
        out_shape.append(jax.ShapeDtypeStruct(tuple(shp), src.dtype))
    scratch = [pltpu.SemaphoreType.DMA((n * (N_DEV - 1),)), pltpu.SemaphoreType.DMA((n * (N_DEV - 1),)),
               pltpu.SemaphoreType.DMA((n,))] if n else []
    return [src for _, src in comm], [any_spec] * n, [any_spec] * n, out_shape, scratch


def _window(ref, axis, idx, size):
    start = pl.multiple_of(idx * size, size)
    return ref.at[tuple(pl.ds(start, size) if a == axis else slice(None) for a in range(len(ref.shape)))]


def _comm_plans(kinds, src_refs, dst_refs, send_sems, recv_sems, local_sems):
    x, y, c = lax.axis_index("x"), lax.axis_index("y"), lax.axis_index("c")
    peers, me = _peers()
    plans = []
    for s, ((kind, axis), src, dst) in enumerate(zip(kinds, src_refs, dst_refs)):
        sems = lambda k: dict(send_sem=send_sems.at[s * (N_DEV - 1) + k], recv_sem=recv_sems.at[s * (N_DEV - 1) + k])
        remote = lambda src_ref, dst_ref, k, pid: pltpu.make_async_remote_copy(
            src_ref=src_ref, dst_ref=dst_ref, device_id=pid, device_id_type=_MESH, **sems(k))
        if kind == "gather2":
            size = src.shape[axis]
            win = lambda idx: _window(dst, axis, idx, size)
            sib, sib_idx = (x, y, 1 - c), 4 * x + 2 * y + (1 - c)
            local = pltpu.make_async_copy(src, win(me), local_sems.at[s])
            to_sib = remote(src, win(me), 0, sib)
            starts, forwards = [local, to_sib], []
            waits = [(local, "local"), (to_sib, "send"), (remote(src, win(sib_idx), 0, sib), "recv")]
            for j, (fx, fy) in enumerate(((1, 0), (0, 1), (1, 1))):
                px, py = (x + fx) % 2, (y + fy) % 2
                same, other = 4 * px + 2 * py + c, 4 * px + 2 * py + (1 - c)
                out = remote(src, win(me), 1 + j, (px, py, c))
                starts.append(out)
                passed = remote(win(same), win(same), 4 + j, sib)
                forwards.append((remote(src, win(same), 1 + j, (px, py, c)), passed))
                waits += [(out, "send"), (passed, "send"), (remote(win(other), win(other), 4 + j, sib), "recv")]
            plans.append((starts, forwards, waits))
            continue
        if kind == "gather":
            size = src.shape[axis]
            src_for = lambda pidx: src
            dst_mine = _window(dst, axis, me, size)
        else:
            size = src.shape[axis] // N_DEV
            src_for = lambda pidx: _window(src, axis, pidx, size)
            dst_mine = dst.at[me]
        local = pltpu.make_async_copy(src_for(me), dst_mine, local_sems.at[s])
        remotes = [remote(src_for(pidx), dst_mine, k, pid) for k, pid, pidx in peers]
        plans.append(([local] + remotes, [], [(local, "local")] + [(cp, "both") for cp in remotes]))
    return plans


def _comm_start(*refs):
    for starts, _, _ in _comm_plans(*refs):
        for cp in starts:
            cp.start()


def _comm_finish(*refs):
    for _, forwards, waits in _comm_plans(*refs):
        for arrival, cp in forwards:
            arrival.wait_recv()
            cp.start()
        for cp, what in waits:
            if what == "send":
                cp.wait_send()
            elif what == "recv":
                cp.wait_recv()
            else:
                cp.wait()


def _adam_vals(w, g, m, v):
    m = B1 * m + (1.0 - B1) * g
    v = B2 * v + (1.0 - B2) * (g * g)
    m_hat = m / (1.0 - B1 ** STEP)
    v_hat = v / (1.0 - B2 ** STEP)
    delta = -LR * (m_hat / (jnp.sqrt(v_hat) + ADAM_EPS) + WD * w)
    return delta, m, v


def _sum_adam(name, recv, w, m, v, tile=256):
    _, r, wd = recv.shape
    if r % min(tile, r) == 0:
        tr, tc = min(tile, r), wd
    else:
        tr, tc = r, tile
        assert wd % tc == 0, (name, r, wd)

    def body(recv_ref, w_ref, m_ref, v_ref, g_out, d_out, m_out, v_out):
        g = recv_ref[0].astype(_F32)
        for s in range(1, N_DEV):
            g = g + recv_ref[s].astype(_F32)
        d_, m_, v_ = _adam_vals(w_ref[...], g, m_ref[...], v_ref[...])
        g_out[...] = g
        d_out[...] = d_
        m_out[...] = m_
        v_out[...] = v_

    spec = pl.BlockSpec((tr, tc), lambda i, j: (i, j))
    return pl.pallas_call(
        body, name=name, grid=(r // tr, wd // tc),
        in_specs=[pl.BlockSpec((N_DEV, tr, tc), lambda i, j: (0, i, j)), spec, spec, spec],
        out_specs=[spec] * 4, out_shape=[jax.ShapeDtypeStruct((r, wd), _F32)] * 4,
        compiler_params=pltpu.CompilerParams(dimension_semantics=("parallel", "parallel"),
                                             vmem_limit_bytes=48 << 20),
    )(recv, w, m, v)


def _small_local(parts, segments, n_rows):
    def body(parts_ref, loc_ref):
        loc_ref[...] = jnp.zeros_like(loc_ref)
        for out_row, n_out, in_row, n_in, kind in segments:
            if kind == "copy":
                loc_ref[out_row:out_row + n_out, :] = parts_ref[in_row:in_row + n_in, :]
            else:
                s = jnp.sum(parts_ref[in_row:in_row + n_in, :], axis=0, keepdims=True)
                if kind == "loss":
                    s = jnp.broadcast_to(jnp.sum(s, axis=1, keepdims=True) * (0.5 / D), (1, D))
                loc_ref[out_row:out_row + 1, :] = s

    vm = pl.BlockSpec(memory_space=pltpu.VMEM)
    return pl.pallas_call(body, name="small_local", in_specs=[vm], out_specs=vm,
                          out_shape=jax.ShapeDtypeStruct((n_rows, D), _F32))(parts)


def _small_final(blocks, late8, late_row, w, m, v):
    n_rows = w.shape[0]

    def body(blocks_ref, late_ref, w_ref, m_ref, v_ref, g_out, d_out, m_out, v_out, loc_ref, recv_ref, send_sems,
             recv_sems):
        peers, me = _peers()
        loc_ref[...] = jnp.broadcast_to(jnp.sum(late_ref[...], axis=0, keepdims=True), (8, D))
        recv_ref[me] = loc_ref[...]
        copies = [pltpu.make_async_remote_copy(src_ref=loc_ref, dst_ref=recv_ref.at[me], send_sem=send_sems.at[k],
                                               recv_sem=recv_sems.at[k], device_id=pid, device_id_type=_MESH)
                  for k, pid, _ in peers]
        for cp in copies:
            cp.start()
        g = blocks_ref[0:n_rows, :]
        for s in range(1, N_DEV):
            g = g + blocks_ref[s * n_rows:(s + 1) * n_rows, :]
        for cp in copies:
            cp.wait()
        late = recv_ref[0]
        for s in range(1, N_DEV):
            late = late + recv_ref[s]
        g = jnp.where(_iota2((n_rows, D), 0) == late_row, jnp.broadcast_to(late[0:1, :], (n_rows, D)), g)
        d_, m_, v_ = _adam_vals(w_ref[...], g, m_ref[...], v_ref[...])
        g_out[...] = g
        d_out[...] = d_
        m_out[...] = m_
        v_out[...] = v_

    vm = pl.BlockSpec(memory_space=pltpu.VMEM)
    return pl.pallas_call(
        body, name="small_final", in_specs=[vm] * 5, out_specs=[vm] * 4,
        out_shape=[jax.ShapeDtypeStruct((n_rows, D), _F32)] * 4,
        scratch_shapes=[pltpu.VMEM((8, D), _F32), pltpu.VMEM((N_DEV, 8, D), _F32),
                        pltpu.SemaphoreType.DMA((N_DEV - 1,)), pltpu.SemaphoreType.DMA((N_DEV - 1,))],
        compiler_params=pltpu.CompilerParams(vmem_limit_bytes=48 << 20),
    )(blocks, late8, w, m, v)


_BIG_NAMES =("w_in", "w_out", "w_ff1", "w_ff2", "w_ple_gate", "w_ple_proj")

_G_VECS = ("norm_mix_g", "gm_v_norm_g", "gm_out_norm_g", "ssd_norm_g", "norm_mlp_g", "ple_norm_g", "final_norm_g")
_LATE = _G_VECS[0]


def _const_mats():
    h = np.arange(128)[:, None]
    ch = np.arange(SSD_W)[None, :]
    e = (ch // SSD_P == h).astype(np.float32)
    ltri = (np.arange(CH)[:, None] >= np.arange(CH)[None, :]).astype(np.float32)
    return jnp.asarray(e, _BF16), jnp.asarray(e.T, _BF16), jnp.asarray(ltri, _BF16)


def _pad_lanes(v, n=128):
    return jnp.pad(v, ((0, 0), (0, n - v.shape[1])))


def _local_step(x, p, tgt, shard, conv_w_shard, small, seq_len):
    seq_chunks = seq_len // CH
    e_mat, et_mat, ltri_mat = _const_mats()
    g_mix, g_mlp, g_ple = small["norm_mix_g"], small["norm_mlp_g"], small["ple_norm_g"]
    g_fin = small["final_norm_g"].reshape(1, D)
    gv, gout, ng = small["gm_v_norm_g"], small["gm_out_norm_g"], small["ssd_norm_g"]
    ws = small["gm_ws"][0]
    bsb = jnp.broadcast_to(small["gm_bs"][0][:, :, None], (GM_H, CH, 128))
    cb = small["ssd_conv_b"]
    dtb, alog = _pad_lanes(small["ssd_dt_bias"]), _pad_lanes(small["ssd_a_log"])
    d_x = jnp.repeat(small["ssd_d"], SSD_P, axis=1)

    first = lambda acc: (acc,)
    rows, cols = ("gather2", 0), ("gather2", 1)
    n1, (g_win, g_cw) = _norm_cast("norm_mix", x, g_mix,
                                   comm=[(rows, shard["w_in"][None]), (("gather", 0), conv_w_shard[None])])
    w_in_t = jnp.pad(g_win.reshape(D_IN, D), ((0, D_IN_PAD - D_IN), (0, 0)))
    cw8 = jnp.pad(g_cw.transpose(1, 0, 2).reshape(CONV_K, CONV_CH), ((0, 8 - CONV_K), (0, 0)))
    mix_consts = (gv, ws, bsb, gout, cw8, cb, dtb, alog, d_x, ng)
    (proj,), (w_out, w1) = _matmul("proj_in", n1, w_in_t, "nt", 512, D_IN_PAD, D, first, [("tile", _F32)],
                                   comm=[(rows, shard["w_out"]), (cols, shard["w_ff1"])])
    (cat, yss, states, conv_act, conv_sig), (w2, wg, wp) = _mixer_fwd(
        proj, *mix_consts, e_mat, ltri_mat, seq_chunks,
        comm=[(rows, shard["w_ff2"]), (rows, shard["w_ple_gate"]), (cols, shard["w_ple_proj"])])

    def epi_res_norm(acc, res, g):
        hv = acc + res
        return hv, _rms(hv, g)

    (h1, n2), _ = _matmul("proj_out", cat, w_out, "nn", 1024, D, 2 * D, epi_res_norm,
                          [("tile", _F32), ("tile", _MXU)], extras=[(x, "tile"), (g_mlp, "row")])

    def epi_relu2(acc):
        hid = jnp.maximum(acc, 0.0)
        return hid, hid * hid

    (hid, hid2), _ = _matmul("ff1", n2, w1, "nn", 1024, 1024, D, epi_relu2, [("tile", _MXU), ("tile", _MXU)])
    (h2, n3), _ = _matmul("ff2", hid2, w2, "nn", 512, D, D_FF, epi_res_norm, [("tile", _F32), ("tile", _MXU)],
                          extras=[(h1, "tile"), (g_ple, "row")])

    def epi_norm_bwd(acc, up, hv, g):
        dx, dg8 = _rms_bwd(hv, g, acc)
        dh = up + dx
        return dh, dh, dg8

    def epi_head(acc, wg_v, p_rows, wp_v, h2v, tg, gf, gp_):
        ppv = _dot(_mx(p_rows), wp_v)
        gate = _sigmoid(acc)
        gp = gate * ppv
        h3 = h2v + gp
        r = lax.rsqrt(jnp.mean(h3 * h3, axis=-1, keepdims=True) + EPS)
        nh = h3 * r
        err = nh * gf - tg
        gy = err * (gf * (1.0 / D))
        dh3 = (gy - nh * jnp.mean(gy * nh, axis=-1, keepdims=True)) * r
        dpp = dh3 * gate
        da3 = dpp * (ppv - gp)
        dh2, dh2_again, dgple8 = epi_norm_bwd(_dot_nt(_mx(da3), wg_v), dh3, h2v, gp_)
        return da3, dpp, dh2, dh2_again, _sum8(err * err), _sum8(err * nh) * (1.0 / D), dgple8

    (da3, dpp, dh2, dh2b, lossp, dgfin, dgple), _ = _matmul(
        "ple_gate_loss_bwd", n3, wg, "nn", 512, D, D, epi_head,
        [("tile", _MXU), ("tile", _MXU), ("tile", _F32), ("tile", _MXU), ("part8", _F32), ("part8", _F32),
         ("part8", _F32)],
        extras=[(p, "rows"), (wp, "full"), (h2, "tile"), (tgt, "tile"), (g_fin, "row"), (g_ple, "row")],
        b_to_epilogue=True)

    s_rows, s_cols = ("scatter", 0), ("scatter", 1)
    (dwp,), _ = _matmul("d_w_ple_proj", p, dpp, "tn", D_PLE, D, 2048, first, [("tile", _BF16)])
    (dwg,), _ = _matmul("d_w_ple_gate", n3, da3, "tn", D, D, 2048, first, [("tile", _BF16)])
    (dw2,), (r_wp, r_wg) = _matmul("d_w_ff2", hid2, dh2b, "tn", 1024, D, 4096, first, [("tile", _BF16)],
                                   comm=[(s_cols, dwp), (s_rows, dwg)])
    (da1,), _ = _matmul("d_ff_hidden", dh2b, w2, "nt", 512, 2048, D,
                        lambda acc, hv: (acc * 2.0 * hv.astype(_F32),), [("tile", _MXU)], extras=[(hid, "tile")])
    (dw1,), _ = _matmul("d_w_ff1", n2, da1, "tn", 1024, 1024, 4096, first, [("tile", _BF16)])
    (dh1, dh1b, dgmlp), _ = _matmul(
        "d_h1", da1, w1, "nt", 256, D, D_FF, epi_norm_bwd, [("tile", _F32), ("tile", _MXU), ("part8", _F32)],
        extras=[(dh2, "tile"), (h1, "tile"), (g_mlp, "row")])
    (dwout,), _ = _matmul("d_w_out", cat, dh1b, "tn", 1024, D, 2048, first, [("tile", _BF16)])
    (dcat,), _ = _matmul("d_cat", dh1b, w_out, "nt", 1024, 1024, D, first, [("tile", _MXU)])
    (dproj, dws, dbs, dgv, dgout, dng, dcw, dcb, ddtb, dalog, dd), (r_w2, r_w1, r_wout) = _mixer_bwd(
        proj, conv_act, conv_sig, dcat, yss, states, *mix_consts, e_mat, et_mat, ltri_mat, seq_chunks,
        comm=[(s_rows, dw2), (s_cols, dw1), (s_rows, dwout)])
    pieces = dict(gm_v_norm_g=dgv, gm_out_norm_g=dgout, ssd_norm_g=dng, norm_mlp_g=dgmlp, ple_norm_g=dgple,
                  final_norm_g=dgfin, gm_ws=dws, gm_bs=dbs, ssd_conv_w=dcw, ssd_conv_b=dcb, ssd_dt_bias=ddtb,
                  ssd_a_log=dalog, ssd_d=dd, loss=lossp)
    parts, segments, n_rows, where = _small_layout(pieces)
    small_block = _small_local(parts, segments, n_rows)
    (dwin_t,), (small_blocks,) = _matmul("d_w_in", n1, dproj, "tn", 512, D_IN_PAD, 1024, lambda acc: (acc.T,),
                                         [("tile_t", _BF16)], comm=[(("gather", 0), small_block)])
    dwin_blocks = dwin_t[:D_IN].reshape(N_DEV, SHARD_IN, D)
    (gx, dgmix), (r_win,) = _matmul(
        "d_x", dproj, w_in_t, "nn", 256, D, D_IN_PAD, lambda *a: epi_norm_bwd(*a)[1:],
        [("tile", _F32), ("part8", _F32)], extras=[(dh1, "tile"), (x, "tile"), (g_mix, "row")],
        comm=[(s_rows, dwin_blocks)])
    r_win = r_win.reshape(N_DEV, SHARD_IN, D)

    big = dict(w_in=r_win, w_out=r_wout, w_ff1=r_w1, w_ff2=r_w2, w_ple_gate=r_wg, w_ple_proj=r_wp)
    return gx, big, small_blocks, dgmix, n_rows, where


def _small_layout(pieces):
    rows, segments = [], []
    in_row, out_row = 0, 0

    def add(arr, kind, n_out, new_group=True):
        nonlocal in_row, out_row
        if new_group:
            out_row = -(-out_row // 8) * 8
        rows.append(arr)
        segments.append((out_row, n_out, in_row, arr.shape[0], kind))
        start = out_row
        in_row += arr.shape[0]
        out_row += n_out
        return start

    where = {_LATE: 0}
    out_row = 1
    for name in _G_VECS[1:]:
        where[name] = add(pieces[name], "sum", 1)
    where["gm_ws"] = add(pieces["gm_ws"].reshape(GM_H * CH * CH // D, D), "copy", GM_H * CH * CH // D)
    where["gm_bs"] = add(jnp.pad(pieces["gm_bs"].reshape(1, D), ((0, 7), (0, 0))), "sum", 1)
    cb = jnp.pad(pieces["ssd_conv_b"], ((0, 0), (0, 2 * D - CONV_CH)))
    where["ssd_conv_b"] = add(cb[:, :D], "sum", 1)
    add(cb[:, D:], "sum", 1, new_group=False)
    cw = jnp.pad(pieces["ssd_conv_w"][:CONV_K], ((0, 0), (0, 2 * D - CONV_CH)))
    where["ssd_conv_w"] = add(cw.reshape(2 * CONV_K, D), "copy", 2 * CONV_K)
    misc = jnp.concatenate([pieces["ssd_dt_bias"], pieces["ssd_a_log"], pieces["ssd_d"],
                            jnp.zeros((8, D - 3 * 128), _F32)], axis=1)
    where["misc"] = add(misc, "sum", 1)
    where["loss"] = add(pieces["loss"], "loss", 1)
    n_rows = -(-out_row // 8) * 8
    return jnp.concatenate(rows, axis=0), tuple(segments), n_rows, where


def _pack_small_params(vals, where, n_rows, my_block):
    rows, at = [], {}

    def add(name, arr):
        at[name] = sum(r.shape[0] for r in rows)
        rows.append(jnp.pad(arr, ((0, -arr.shape[0] % 8), (0, 0))))

    for name in _G_VECS:
        add(name, vals[name].reshape(1, D))
    add("gm_ws", vals["gm_ws"].reshape(GM_H * CH * CH // D, D))
    add("gm_bs", vals["gm_bs"].reshape(1, D))
    cb = jnp.pad(vals["ssd_conv_b"].reshape(1, CONV_CH), ((0, 0), (0, 2 * D - CONV_CH)))
    add("ssd_conv_b", cb.reshape(2, D))
    cw = lax.dynamic_update_slice(jnp.zeros((CONV_K, 2 * D), _F32), vals["ssd_conv_w"].reshape(CONV_K, -1),
                                  (0, my_block * (CONV_CH // N_DEV)))
    add("ssd_conv_w", cw.reshape(2 * CONV_K, D))
    misc = jnp.concatenate([_pad_lanes(vals["ssd_dt_bias"].reshape(1, SSD_H)),
                            _pad_lanes(vals["ssd_a_log"].reshape(1, SSD_H)),
                            _pad_lanes(vals["ssd_d"].reshape(1, SSD_H)), jnp.zeros((1, D - 3 * 128), _F32)], axis=1)
    add("misc", misc)
    assert all(where[k] == r for k, r in at.items()), (where, at)
    rows.append(jnp.zeros((n_rows - sum(r.shape[0] for r in rows), D), _F32))
    return jnp.concatenate(rows, axis=0)


def _unpack_small(buf, where, my_block, shapes):
    out = {}
    for name in _G_VECS:
        out[name] = buf[where[name]].reshape(shapes[name])
    n_ws = GM_H * CH * CH // D
    out["gm_ws"] = buf[where["gm_ws"]:where["gm_ws"] + n_ws].reshape(shapes["gm_ws"])
    out["gm_bs"] = buf[where["gm_bs"]].reshape(shapes["gm_bs"])
    r = where["ssd_conv_b"]
    out["ssd_conv_b"] = buf[r:r + 2].reshape(1, 2 * D)[:, :CONV_CH].reshape(shapes["ssd_conv_b"])
    r = where["ssd_conv_w"]
    cw = buf[r:r + 2 * CONV_K].reshape(CONV_K, 2 * D)
    out["ssd_conv_w"] = lax.dynamic_slice(cw, (0, my_block * (CONV_CH // N_DEV)),
                                          (CONV_K, CONV_CH // N_DEV)).reshape(shapes["ssd_conv_w"])
    misc = buf[where["misc"]]
    for i, name in enumerate(("ssd_dt_bias", "ssd_a_log", "ssd_d")):
        out[name] = misc[i * 128:i * 128 + SSD_H].reshape(shapes[name])
    return out


_WEIGHTS = ("norm_mix_g", "w_in", "gm_v_norm_g", "gm_ws", "gm_bs", "gm_out_norm_g", "ssd_conv_w", "ssd_conv_b",
            "ssd_dt_bias", "ssd_a_log", "ssd_d", "ssd_norm_g", "w_out", "norm_mlp_g", "w_ff1", "w_ff2", "ple_norm_g",
            "w_ple_gate", "w_ple_proj", "final_norm_g")


def kernel(x, p, norm_mix_g, w_in, gm_v_norm_g, gm_ws, gm_bs, gm_out_norm_g, ssd_conv_w, ssd_conv_b, ssd_dt_bias, ssd_a_log, ssd_d, ssd_norm_g, w_out, norm_mlp_g, w_ff1, w_ff2, ple_norm_g, w_ple_gate, w_ple_proj, final_norm_g, loss_target, m_norm_mix_g, m_w_in, m_gm_v_norm_g, m_gm_ws, m_gm_bs, m_gm_out_norm_g, m_ssd_conv_w, m_ssd_conv_b, m_ssd_dt_bias, m_ssd_a_log, m_ssd_d, m_ssd_norm_g, m_w_out, m_norm_mlp_g, m_w_ff1, m_w_ff2, m_ple_norm_g, m_w_ple_gate, m_w_ple_proj, m_final_norm_g, v_norm_mix_g, v_w_in, v_gm_v_norm_g, v_gm_ws, v_gm_bs, v_gm_out_norm_g, v_ssd_conv_w, v_ssd_conv_b, v_ssd_dt_bias, v_ssd_a_log, v_ssd_d, v_ssd_norm_g, v_w_out, v_norm_mlp_g, v_w_ff1, v_w_ff2, v_ple_norm_g, v_w_ple_gate, v_w_ple_proj, v_final_norm_g):
    args = dict(locals())
    w = {n: args[n] for n in _WEIGHTS}
    m = {n: args["m_" + n] for n in _WEIGHTS}
    v = {n: args["v_" + n] for n in _WEIGHTS}
    shapes = {n: w[n].shape for n in _WEIGHTS}
    my_block = 4 * lax.axis_index("x") + 2 * lax.axis_index("y") + lax.axis_index("c")
    nb, seq_len, _ = x.shape

    local = lambda d, n: d[n][0].T if n == "w_in" else d[n][0]
    shard = {n: local(w, n).astype(_MXU) for n in _BIG_NAMES}
    small = {n: w[n] for n in _WEIGHTS if n not in _BIG_NAMES}
    gx, recv, small_blocks, late8, n_rows, where = _local_step(
        x.reshape(nb * seq_len, D), p.reshape(nb * seq_len, D_PLE), loss_target.reshape(nb * seq_len, D), shard,
        ssd_conv_w[0], small, seq_len)

    big_out = [{}, {}, {}, {}]
    for n in _BIG_NAMES:
        res = _sum_adam("sum_adam_" + n, recv[n], local(w, n), local(m, n), local(v, n))
        for k in range(4):
            big_out[k][n] = (res[k].T if n == "w_in" else res[k]).reshape(shapes[n])

    packs = [_pack_small_params(d, where, n_rows, my_block) for d in (w, m, v)]
    small_res = _small_final(small_blocks, late8, where[_LATE], *packs)
    loss = small_res[0][where["loss"], 0]
    small_out = [_unpack_small(a, where, my_block, shapes) for a in small_res]

    outs = [loss, gx.reshape(x.shape)]
    for k in range(4):
        outs += [big_out[k][n] if n in _BIG_NAMES else small_out[k][n] for n in _WEIGHTS]
    return tuple(outs)
```

```python
import math

import jax
import jax.numpy as jnp
import numpy as np
from jax import lax
from jax.experimental import pallas as pl
from jax.experimental.pallas import tpu as pltpu

_F32 = jnp.float32
_BF16 = jnp.bfloat16
_MXU = jnp.bfloat16

D = 1024
D_PLE = 256
GM_W = 1024
GM_H = 8
CH = 128
SSD_W = 1024
SSD_H = 16
SSD_P = 64
SSD_G = 2
SSD_N = 128
CONV_K = 4
CONV_CH = SSD_W + 2 * SSD_G * SSD_N
D_FF = 4096
D_IN = 2 * GM_W + SSD_W + CONV_CH + SSD_H
D_IN_PAD = 4736
EPS = 1e-6
DT_BLK = (D_IN_PAD - 128) // 128
N_DEV = 8
SHARD_IN = D_IN // N_DEV

LR, B1, B2, ADAM_EPS, WD, STEP = 0.001, 0.9, 0.999, 1e-08, 0.01, 10
_LOG2E = math.log2(math.e)
_LOG2_INV_SQRT_2PI = -0.5 * math.log2(2.0 * math.pi)

_SEQ_PER_STEP = 2
_V7X_VMEM_BYTES = 64 * 1024 * 1024
_VMEM_CAP = _V7X_VMEM_BYTES - 8 * 1024 * 1024
_MESH = pl.DeviceIdType.MESH


def _vmem_limit(nbytes):
    return int(min(_VMEM_CAP, max(32 * 1024 * 1024, nbytes * 5 // 4 + (4 << 20))))


def _nbytes(shape, dtype):
    return int(np.prod(shape)) * jnp.dtype(dtype).itemsize


def _mx(v):
    return v.astype(_MXU)


def _dot(a, b):
    return jnp.dot(a, b, preferred_element_type=_F32)


def _dot_nt(a, b):
    return lax.dot_general(a, b, (((1,), (1,)), ((), ())), preferred_element_type=_F32)


def _dot_tn(a, b):
    return lax.dot_general(a, b, (((0,), (0,)), ((), ())), preferred_element_type=_F32)


def _split3(a):
    hi = a.astype(_BF16)
    r = a - hi.astype(_F32)
    mid = r.astype(_BF16)
    lo = (r - mid.astype(_F32)).astype(_BF16)
    return hi, mid, lo


def _xdot(dotfn, a, b01):
    b = b01.astype(_BF16)
    hi, mid, lo = _split3(a)
    return (dotfn(hi, b) + dotfn(mid, b)) + dotfn(lo, b)


def _xdot_left(dotfn, a01, b):
    a = a01.astype(_BF16)
    hi, mid, lo = _split3(b)
    return (dotfn(a, hi) + dotfn(a, mid)) + dotfn(a, lo)


def _sum8(v):
    r, n = v.shape
    return v.reshape(r // 8, 8, n).sum(axis=0)


def _sigmoid(v):
    return 1.0 / (1.0 + jnp.exp(-v))


def _rms(xv, g):
    ms = jnp.mean(xv * xv, axis=-1, keepdims=True)
    return xv * lax.rsqrt(ms + EPS) * g


def _rms_bwd(xv, g, dn):
    r = lax.rsqrt(jnp.mean(xv * xv, axis=-1, keepdims=True) + EPS)
    nh = xv * r
    gy = dn * g
    dx = (gy - nh * jnp.mean(gy * nh, axis=-1, keepdims=True)) * r
    return dx, _sum8(dn * nh)


def _iota2(shape, axis):
    return lax.broadcasted_iota(jnp.int32, shape, axis)


def _norm_cast(name, x, g, tm=512, comm=()):
    t, n = x.shape
    tm = min(tm, t)
    steps = t // tm
    kinds = [kind for kind, _ in comm]
    c_in, c_in_specs, c_out_specs, c_out_shape, c_scratch = _comm_io(comm)

    def body(*refs):
        x_ref, g_ref = refs[0], refs[1]
        o_ref = refs[2 + len(comm)]
        comm_refs = (kinds, refs[2:2 + len(comm)], refs[3 + len(comm):3 + 2 * len(comm)], *refs[3 + 2 * len(comm):])
        if comm:
            pl.when(pl.program_id(0) == 0)(lambda: _comm_start(*comm_refs))

        o_ref[...] = _rms(x_ref[...], g_ref[...]).astype(o_ref.dtype)
        if comm:
            pl.when(pl.program_id(0) == steps - 1)(lambda: _comm_finish(*comm_refs))

    res = pl.pallas_call(
        body, name=name, grid=(steps,),
        in_specs=[pl.BlockSpec((tm, n), lambda i: (i, 0)), pl.BlockSpec((1, n), lambda i: (0, 0))] + c_in_specs,
        out_specs=[pl.BlockSpec((tm, n), lambda i: (i, 0))] + c_out_specs,
        out_shape=[jax.ShapeDtypeStruct((t, n), _MXU)] + c_out_shape, scratch_shapes=c_scratch,
        compiler_params=pltpu.CompilerParams(dimension_semantics=("arbitrary",)),
    )(x, g, *c_in)
    return res[0], res[1:]


def _matmul(name, a, b, mode, tm, tn, tk, epilogue, outs, extras=(), comm=(), b_to_epilogue=False):
    m, k = a.shape[::-1] if mode == "tn" else a.shape
    n = b.shape[0] if mode == "nt" else b.shape[1]
    tm, tn, tk = min(tm, m), min(tn, n), min(tk, k)
    assert m % tm == 0 and n % tn == 0 and k % tk == 0, (name, m, n, k, tm, tn, tk)
    if mode == "nn":
        a_spec = pl.BlockSpec((tm, tk), lambda i, j, kk: (i, kk))
        b_spec = pl.BlockSpec((tk, tn), lambda i, j, kk: (kk, j))
        dotfn = _dot
    elif mode == "nt":
        a_spec = pl.BlockSpec((tm, tk), lambda i, j, kk: (i, kk))
        b_spec = pl.BlockSpec((tn, tk), lambda i, j, kk: (j, kk))
        dotfn = _dot_nt
    else:
        a_spec = pl.BlockSpec((tk, tm), lambda i, j, kk: (kk, i))
        b_spec = pl.BlockSpec((tk, tn), lambda i, j, kk: (kk, j))
        dotfn = _dot_tn
    ni, nj, nk = m // tm, n // tn, k // tk
    n_ex, n_out, n_comm = len(extras), len(outs), len(comm)
    kinds = [kind for kind, _ in comm]
    c_in, c_in_specs, c_out_specs, c_out_shape, c_scratch = _comm_io(comm)

    in_specs, vmem = [a_spec, b_spec], 2 * (tm * tk * a.dtype.itemsize + tk * tn * b.dtype.itemsize)
    for arr, kind in extras:
        if kind == "tile":
            in_specs.append(pl.BlockSpec((tm, tn), lambda i, j, kk: (i, j)))
            vmem += 2 * _nbytes((tm, tn), arr.dtype)
        elif kind == "rows":
            in_specs.append(pl.BlockSpec((tm, arr.shape[1]), lambda i, j, kk: (i, 0)))
            vmem += 2 * _nbytes((tm, arr.shape[1]), arr.dtype)
        elif kind == "full":
            in_specs.append(pl.BlockSpec(arr.shape, lambda i, j, kk: (0,) * arr.ndim))
            vmem += 2 * _nbytes(arr.shape, arr.dtype)
        else:
            in_specs.append(pl.BlockSpec((1, tn), lambda i, j, kk: (0, j)))
    out_specs, out_shape = [], []
    for kind, dt in outs:
        if kind == "tile":
            out_specs.append(pl.BlockSpec((tm, tn), lambda i, j, kk: (i, j)))
            out_shape.append(jax.ShapeDtypeStruct((m, n), dt))
            vmem += 2 * _nbytes((tm, tn), dt)
        elif kind == "tile_t":
            out_specs.append(pl.BlockSpec((tn, tm), lambda i, j, kk: (j, i)))
            out_shape.append(jax.ShapeDtypeStruct((n, m), dt))
            vmem += 2 * _nbytes((tm, tn), dt)
        else:
            assert nj == 1, "the partial-sum rows are accumulated over consecutive row tiles"
            out_specs.append(pl.BlockSpec((8, tn), lambda i, j, kk: (0, 0)))
            out_shape.append(jax.ShapeDtypeStruct((8, n), dt))
    scratch = [pltpu.VMEM((tm, tn), _F32)] if nk > 1 else []
    vmem += _nbytes((tm, tn), _F32) * 2

    def body(*refs):
        a_ref, b_ref = refs[0], refs[1]
        ex_refs = refs[2:2 + n_ex]
        n_in = 2 + n_ex + n_comm
        out_refs = refs[n_in:n_in + n_out]
        i, j, kk = pl.program_id(0), pl.program_id(1), pl.program_id(2)
        comm_refs = (kinds, refs[2 + n_ex:n_in], refs[n_in + n_out:n_in + n_out + n_comm], *refs[len(refs) - 3:])
        if n_comm:
            pl.when((i == 0) & (j == 0) & (kk == 0))(lambda: _comm_start(*comm_refs))

        b_val = _mx(b_ref[...])
        part = dotfn(_mx(a_ref[...]), b_val)

        def finish(acc):
            vals = epilogue(acc, *([b_val] if b_to_epilogue else []), *[r[...] for r in ex_refs])
            for r, v, (kind, _) in zip(out_refs, vals, outs):
                if kind == "part8":
                    @pl.when(i == 0)
                    def _():
                        r[...] = v

                    @pl.when(i > 0)
                    def _():
                        r[...] += v
                else:
                    r[...] = v.astype(r.dtype)

        if nk == 1:
            finish(part)
        else:
            acc_ref = refs[n_in + n_out + n_comm]

            @pl.when(kk == 0)
            def _():
                acc_ref[...] = part

            @pl.when(kk > 0)
            def _():
                acc_ref[...] += part

            @pl.when(kk == nk - 1)
            def _():
                finish(acc_ref[...])

        if n_comm:
            pl.when((i == ni - 1) & (j == nj - 1) & (kk == nk - 1))(lambda: _comm_finish(*comm_refs))

    carried =n_comm or any(kind == "part8" for kind, _ in outs)
    sem = ("arbitrary",) * 3 if carried else ("parallel", "parallel", "arbitrary")
    res = pl.pallas_call(
        body, name=name, grid=(ni, nj, nk),
        in_specs=in_specs + c_in_specs, out_specs=out_specs + c_out_specs, out_shape=out_shape + c_out_shape,
        scratch_shapes=scratch + c_scratch,
        compiler_params=pltpu.CompilerParams(dimension_semantics=sem, vmem_limit_bytes=_vmem_limit(vmem)),
    )(a, b, *[arr for arr, _ in extras], *c_in)
    return res[:n_out], res[n_out:]


def _shift_down(v, halo8, j):
    if j == 0:
        return v
    r = pltpu.roll(v, j, axis=0)
    hr = pltpu.roll(halo8, j, axis=0)
    top = jnp.where(_iota2(hr.shape, 0) < j, hr, r[:8])
    return jnp.concatenate([top, r[8:]], axis=0)


def _shift_up(v, next8, j):
    if j == 0:
        return v
    rows = v.shape[0]
    r = pltpu.roll(v, rows - j, axis=0)
    nr = pltpu.roll(next8, 8 - j, axis=0)
    bot = jnp.where(_iota2(nr.shape, 0) >= 8 - j, nr, r[rows - 8:])
    return jnp.concatenate([r[:rows - 8], bot], axis=0)


def _silu_grad(sig, silu):
    return sig + silu * (1.0 - sig)


def _gmlp_fwd_vals(pu, pv, gv, ws_ref, bsb_ref, want_bwd):
    tril = _iota2((CH, CH), 0) >= _iota2((CH, CH), 1)
    cdf_u = 0.5 * (1.0 + lax.erf(pu * 0.7071067811865476))
    cdf_v = 0.5 * (1.0 + lax.erf(pv * 0.7071067811865476))
    u = pu * cdf_u
    v = pv * cdf_v
    ys, keep = [], [(cdf_u, cdf_v)] if want_bwd else []
    for h in range(GM_H):
        sl = slice(h * 128, (h + 1) * 128)
        vh = v[:, sl]
        r = lax.rsqrt(jnp.mean(vh * vh, axis=-1, keepdims=True) + EPS)
        vn = vh * r * gv[:, sl]
        wm = _mx(jnp.where(tril, ws_ref[h], 0.0))
        mixed = _dot(wm, _mx(vn)) + bsb_ref[h]
        ys.append(u[:, sl] * mixed)
        if want_bwd:
            keep.append((vh, r, vn, wm, mixed))
    return jnp.concatenate(ys, axis=1), u, keep


def _ssd_conv(xbc, halo8, cw_ref, cb):
    cpre = cb + sum(cw_ref[k:k + 1, :] * _shift_down(xbc, halo8, CONV_K - 1 - k) for k in range(CONV_K))
    sig = _sigmoid(cpre)
    return sig, cpre * sig


def _ssd_decay(dtraw, dtb, alog, e_ref, ltri):
    dtin = dtraw + dtb
    dt = jnp.maximum(dtin, 0.0) + jnp.log(1.0 + jnp.exp(-jnp.abs(dtin)))
    a_neg = -jnp.exp(alog)
    cs = _xdot_left(_dot, ltri, dt * a_neg)
    cs_last = cs[CH - 1:CH, :]
    ecs = jnp.exp(cs)
    dec = jnp.exp(cs_last - cs)
    cdec = jnp.exp(cs_last)
    e = e_ref[...]
    dt_x = _dot(dt.astype(_BF16), e)
    ecs_x = _dot(ecs.astype(_BF16), e)
    dec_x = _dot(dec.astype(_BF16), e)
    cdec_x = _xdot(_dot, jnp.broadcast_to(cdec, (8, 128)), e)[0:1, :]
    return dict(dtin=dtin, dt=dt, a_neg=a_neg, cs=cs, ecs=ecs, dec=dec, cdec=cdec,
                dt_x=dt_x, ecs_x=ecs_x, dec_x=dec_x, cdec_x=cdec_x)


def _head_lm(cs, cst_ref, h, tril):
    seg = jnp.broadcast_to(cs[:, h:h + 1], (CH, CH)) - cst_ref[h:h + 1, :]
    return jnp.exp(jnp.where(tril, seg, -jnp.inf))


def _mixer_fwd(proj, gv, ws, bsb, gout, cw8, cb, dtb, alog, d_x, ng, e_mat, ltri_mat, seq_chunks, comm=()):
    t = proj.shape[0]
    n_seq = t // (seq_chunks * CH)
    sb = math.gcd(_SEQ_PER_STEP, n_seq)
    n_groups = n_seq // sb
    proj = proj.reshape(n_seq, seq_chunks * CH, proj.shape[1])
    n_comm = len(comm)
    kinds = [kind for kind, _ in comm]
    c_in, c_in_specs, c_out_specs, c_out_shape, c_scratch = _comm_io(comm)

    def body(*refs):
        comm_refs = (kinds, refs[18:18 + n_comm], refs[23 + n_comm:23 + 2 * n_comm], *refs[25 + 2 * n_comm:])
        grp, c = pl.program_id(0), pl.program_id(1)
        if n_comm:
            pl.when((grp == 0) & (c == 0))(lambda: _comm_start(*comm_refs))
            pl.when((grp == n_groups - 1) & (c == seq_chunks - 1))(lambda: _comm_finish(*comm_refs))
        for s in range(sb):
            per_seq = lambda rs: [r.at[s] for r in rs]
            one_chunk(c == 0, *per_seq(refs[:6]), *refs[6:18], *per_seq(refs[18 + n_comm:23 + n_comm]),
                      *per_seq(refs[23 + 2 * n_comm:25 + 2 * n_comm]))

    def one_chunk(first, pu_ref, pv_ref, z_ref, xbc_ref, dt_ref, halo_ref, gv_ref, ws_ref, bsb_ref, gout_ref, cw_ref,
                  cb_ref, dtb_ref, alog_ref, dx_ref, ng_ref, e_ref, ltri_ref, cat_ref, y_ref, st_ref, act_ref, sig_ref,
                  s_ref, cst_ref):
        tril = _iota2((CH, CH), 0) >= _iota2((CH, CH), 1)
        lane = _iota2((CH, 128), 1)

        y_a, _, _ = _gmlp_fwd_vals(pu_ref[...], pv_ref[...], gv_ref[...], ws_ref, bsb_ref, False)
        cat_ref[:, 0:GM_W] = _rms(y_a, gout_ref[...]).astype(cat_ref.dtype)

        @pl.when(first)
        def _():
            s_ref[...] = jnp.zeros_like(s_ref)

        halo8 = jnp.where(first, 0.0, halo_ref[...])
        sig, act = _ssd_conv(xbc_ref[...], halo8, cw_ref, cb_ref[...])
        sig_ref[...] = sig
        act_ref[...] = act
        q = _ssd_decay(dt_ref[...], dtb_ref[...], alog_ref[...], e_ref, ltri_ref[...])
        xv = act[:, 0:SSD_W]
        xdt = xv * q["dt_x"]
        xdt_m = _mx(xdt)
        cs = q["cs"]
        cst_ref[...] = cs.T
        s_prev = s_ref[...]
        st_ref[...] = s_prev
        ys = []
        for g in range(SSD_G):
            bg = _mx(act[:, SSD_W + g * SSD_N:SSD_W + (g + 1) * SSD_N])
            cg = _mx(act[:, SSD_W + SSD_G * SSD_N + g * SSD_N:SSD_W + SSD_G * SSD_N + (g + 1) * SSD_N])
            cbm = _dot_nt(cg, bg)
            gs = slice(g * 512, (g + 1) * 512)
            for pr in range(4):
                ps = slice(g * 512 + pr * 128, g * 512 + (pr + 1) * 128)
                o = []
                for hh in range(2):
                    h = g * 8 + pr * 2 + hh
                    m_h = _mx(cbm * _head_lm(cs, cst_ref, h, tril))
                    o.append(_dot(m_h, xdt_m[:, ps]))
                ys.append(jnp.where(lane < SSD_P, o[0], o[1]))
            sg = s_prev[:, gs]
            yoff = _dot(cg, _mx(sg)) * q["ecs_x"][:, gs]
            ys[-4:] = [ys[-4 + i] + yoff[:, i * 128:(i + 1) * 128] for i in range(4)]
            st_new = _dot_tn(bg, _mx(q["dec_x"][:, gs] * xdt[:, gs]))
            s_ref[:, gs] = sg * q["cdec_x"][:, gs] + st_new
        y = jnp.concatenate(ys, axis=1) + dx_ref[...] * xv
        y_ref[...] = y
        zv = z_ref[...]
        yg = y * (zv * _sigmoid(zv))
        for g in range(SSD_G):
            gs = slice(g * 512, (g + 1) * 512)
            cat_ref[:, GM_W + g * 512:GM_W + (g + 1) * 512] = _rms(yg[:, gs], ng_ref[:, gs]).astype(cat_ref.dtype)

    blk = lambda w, j: pl.BlockSpec((sb, CH, w), lambda g, c: (g, c, j))
    full = lambda arr: pl.BlockSpec(arr.shape, lambda g, c: (0,) * arr.ndim)
    consts = [gv, ws, bsb, gout, cw8, cb, dtb, alog, d_x, ng, e_mat, ltri_mat]
    seq = seq_chunks * CH
    res = pl.pallas_call(
        body, name="mixer_fwd", grid=(n_groups, seq_chunks),
        in_specs=[blk(GM_W, 0), blk(GM_W, 1), blk(SSD_W, 2), blk(CONV_CH, 2), blk(128, DT_BLK),
                  pl.BlockSpec((sb, 8, CONV_CH), lambda g, c: (g, jnp.maximum(c * (CH // 8) - 1, 0), 2))]
        + [full(a) for a in consts] + c_in_specs,
        out_specs=[blk(2 * D, 0), blk(SSD_W, 0), blk(SSD_W, 0), blk(CONV_CH, 0), blk(CONV_CH, 0)] + c_out_specs,
        out_shape=[jax.ShapeDtypeStruct((n_seq, seq, 2 * D), _MXU), jax.ShapeDtypeStruct((n_seq, seq, SSD_W), _F32),
                   jax.ShapeDtypeStruct((n_seq, seq, SSD_W), _F32), jax.ShapeDtypeStruct((n_seq, seq, CONV_CH), _F32),
                   jax.ShapeDtypeStruct((n_seq, seq, CONV_CH), _F32)] + c_out_shape,
        scratch_shapes=[pltpu.VMEM((sb, SSD_N, SSD_W), _F32), pltpu.VMEM((sb, 128, CH), _F32)] + c_scratch,
        compiler_params=pltpu.CompilerParams(dimension_semantics=("arbitrary", "arbitrary"),
                                             vmem_limit_bytes=48 << 20),
    )(proj, proj, proj, proj, proj, proj, *consts, *c_in)
    return [r.reshape(t, r.shape[-1]) for r in res[:5]], res[5:]


def _mixer_bwd(proj, act, sig, dcat, yss, states, gv, ws, bsb, gout, cw8, cb, dtb, alog, d_x, ng, e_mat, et_mat,
               ltri_mat, seq_chunks, comm=()):
    t = proj.shape[0]
    n_seq = t // (seq_chunks * CH)
    sb = math.gcd(_SEQ_PER_STEP, n_seq)
    n_groups = n_seq // sb
    seq = seq_chunks * CH
    proj, act, sig, dcat, yss, states = [a.reshape(n_seq, seq, a.shape[1])
                                         for a in (proj, act, sig, dcat, yss, states)]
    n_comm = len(comm)
    kinds = [kind for kind, _ in comm]
    c_in, c_in_specs, c_out_specs, c_out_shape, c_scratch = _comm_io(comm)

    def body(*refs):
        o0, s0 = 23 + n_comm, 34 + 2 * n_comm
        acc_refs, shared = refs[o0 + 1:o0 + 11], refs[s0 + 4:s0 + 6]
        comm_refs = (kinds, refs[23:23 + n_comm], refs[o0 + 11:o0 + 11 + n_comm], *refs[s0 + 6:])
        grp, i = pl.program_id(0), pl.program_id(1)
        if n_comm:
            pl.when((grp == 0) & (i == 0))(lambda: _comm_start(*comm_refs))

        @pl.when((grp == 0) & (i == 0))
        def _():
            for r in (*acc_refs, *shared, refs[s0 + 2]):
                r[...] = jnp.zeros_like(r)

        @pl.when(i == 0)
        def _():
            refs[s0][...] = jnp.zeros_like(refs[s0])
            refs[s0 + 1][...] = jnp.zeros_like(refs[s0 + 1])

        for s in range(sb):
            per_seq = lambda rs: [r.at[s] for r in rs]
            last_of_all = ((grp == n_groups - 1) & (i == seq_chunks - 1)) if s == sb - 1 else None
            one_chunk(last_of_all, *per_seq(refs[:10]), *refs[10:23], refs[o0].at[s], *acc_refs,
                      *per_seq(refs[s0:s0 + 4]), *shared)
        if n_comm:
            pl.when((grp == n_groups - 1) & (i == seq_chunks - 1))(lambda: _comm_finish(*comm_refs))

    def one_chunk(finalize, pu_ref, pv_ref, z_ref, xbc_ref, dt_ref, act_ref, sig_ref, dcat_ref, y_ref, st_ref,
                  gv_ref, ws_ref, bsb_ref, gout_ref, cw_ref, cb_ref, dtb_ref, alog_ref, dx_ref, ng_ref,
                  e_ref, et_ref, ltri_ref,
                  dproj_ref, dws_ref, dbs_ref, dgv_ref, dgout_ref, dng_ref, dcw_ref, dcb_ref, ddtb_ref, dalog_ref,
                  dd_ref, ds_ref, dnext_ref, dcst_ref, cst_ref, dbacc_ref, ddacc_ref):
        tril = _iota2((CH, CH), 0) >= _iota2((CH, CH), 1)
        lane = _iota2((CH, 128), 1)
        row = _iota2((CH, 128), 0)
        dcat_v = dcat_ref[...].astype(_F32)

        pu, pv = pu_ref[...], pv_ref[...]
        gv_v = gv_ref[...]
        y_a, u, keep = _gmlp_fwd_vals(pu, pv, gv_v, ws_ref, bsb_ref, True)
        dy, dgout8 = _rms_bwd(y_a, gout_ref[...], dcat_v[:, 0:GM_W])
        dgout_ref[...] += dgout8
        dus, dvs, dgvs = [], [], []
        for h in range(GM_H):
            sl = slice(h * 128, (h + 1) * 128)
            vh, r, vn, wm, mixed = keep[h + 1]
            dyh = dy[:, sl]
            dus.append(dyh * mixed)
            dmix = dyh * u[:, sl]
            dmix_m = _mx(dmix)
            dws_ref[h] += jnp.where(tril, _dot_nt(dmix_m, _mx(vn)), 0.0)
            dbacc_ref[h] += dmix
            dvn = _dot_tn(wm, dmix_m)
            gy = dvn * gv_v[:, sl]
            nh = vh * r
            dvs.append((gy - nh * jnp.mean(gy * nh, axis=-1, keepdims=True)) * r)
            dgvs.append(_sum8(dvn * nh))
        dgv_ref[...] += jnp.concatenate(dgvs, axis=1)
        cdf_u, cdf_v = keep[0]
        gelu_grad = lambda pre, cdf: cdf + pre * jnp.exp2(pre * pre * (-0.5 * _LOG2E) + _LOG2_INV_SQRT_2PI)
        dproj_ref[:, 0:GM_W] = (jnp.concatenate(dus, axis=1) * gelu_grad(pu, cdf_u)).astype(dproj_ref.dtype)
        dproj_ref[:, GM_W:2 * GM_W] = (jnp.concatenate(dvs, axis=1) * gelu_grad(pv, cdf_v)).astype(dproj_ref.dtype)

        q = _ssd_decay(dt_ref[...], dtb_ref[...], alog_ref[...], e_ref, ltri_ref[...])
        act = act_ref[...]
        xv = act[:, 0:SSD_W]
        dt_x, ecs_x, dec_x, cdec_x = q["dt_x"], q["ecs_x"], q["dec_x"], q["cdec_x"]
        xdt = xv * dt_x
        xdt_m = _mx(xdt)
        cs = q["cs"]
        cst_ref[...] = cs.T
        s_prev = st_ref[...]
        ds = ds_ref[...]
        yv = y_ref[...]
        zv = z_ref[...]
        sig_z = _sigmoid(zv)
        sz = zv * sig_z
        yg = yv * sz
        dygs, dng8 = [], []
        for g in range(SSD_G):
            gs = slice(g * 512, (g + 1) * 512)
            a_, b_ = _rms_bwd(yg[:, gs], ng_ref[:, gs], dcat_v[:, GM_W + g * 512:GM_W + (g + 1) * 512])
            dygs.append(a_)
            dng8.append(b_)
        dyg = jnp.concatenate(dygs, axis=1)
        dng_ref[...] += jnp.concatenate(dng8, axis=1)
        dyv = dyg * sz
        dproj_ref[:, 2 * GM_W:2 * GM_W + SSD_W] = (dyg * yv * _silu_grad(sig_z, sz)).astype(dproj_ref.dtype)
        ddacc_ref[...] += _sum8(dyv * xv)
        dyv_m = _mx(dyv)

        dxdt_parts, db_parts, dc_parts = [], [], []
        dcs = jnp.zeros((CH, 128), _F32)
        dcs_x_parts, ddec_x_parts, dcl_x_parts = [], [], []
        for g in range(SSD_G):
            gs = slice(g * 512, (g + 1) * 512)
            bg = _mx(act[:, SSD_W + g * SSD_N:SSD_W + (g + 1) * SSD_N])
            cg = _mx(act[:, SSD_W + SSD_G * SSD_N + g * SSD_N:SSD_W + SSD_G * SSD_N + (g + 1) * SSD_N])
            cbm = _dot_nt(cg, bg)
            sg = s_prev[:, gs]
            sg_m = _mx(sg)
            dsg = ds[:, gs]
            dsg_m = _mx(dsg)
            zoff = _dot(cg, sg_m)
            dz_off = dyv[:, gs] * ecs_x[:, gs]
            dz_off_m = _mx(dz_off)
            dcs_x_parts.append(dyv[:, gs] * zoff * ecs_x[:, gs])
            dcg = _dot_nt(dz_off_m, sg_m)
            dsprev = _dot_tn(cg, dz_off_m)
            w_st = dec_x[:, gs] * xdt[:, gs]
            dw_st = _dot(bg, dsg_m)
            dbg = _dot_nt(_mx(w_st), dsg_m)
            dxdt_g = dec_x[:, gs] * dw_st
            ddec_x_parts.append(dw_st * xdt[:, gs])
            dsprev = dsprev + cdec_x[:, gs] * dsg
            dcl_x_parts.append(jnp.sum(dsg * sg, axis=0, keepdims=True) * cdec_x[:, gs])
            ds_ref[:, gs] = dsprev
            dcb = jnp.zeros((CH, CH), _F32)
            dxdt_pairs = []
            for pr in range(4):
                ps = slice(g * 512 + pr * 128, g * 512 + (pr + 1) * 128)
                acc_pair = None
                for hh in range(2):
                    h = g * 8 + pr * 2 + hh
                    in_head = (lane < SSD_P) if hh == 0 else (lane >= SSD_P)
                    lm = _head_lm(cs, cst_ref, h, tril)
                    m_h = cbm * lm
                    m_hm = _mx(m_h)
                    dyh_m = _mx(jnp.where(in_head, dyv[:, ps], 0.0))
                    dm = _dot_nt(dyh_m, xdt_m[:, ps])
                    dcb = dcb + dm * lm
                    qm = dm * m_h
                    dcs = dcs + jnp.where(lane == h, jnp.sum(qm, axis=1, keepdims=True), 0.0)
                    dcst_ref[h:h + 1, :] = jnp.sum(qm, axis=0, keepdims=True)
                    contrib = jnp.where(in_head, _dot_tn(m_hm, dyv_m[:, ps]), 0.0)
                    acc_pair = contrib if acc_pair is None else acc_pair + contrib
                dxdt_pairs.append(acc_pair)
            dxdt_parts.append(dxdt_g + jnp.concatenate(dxdt_pairs, axis=1))
            dcb_m = _mx(dcb)
            dc_parts.append(dcg + _dot(dcb_m, bg))
            db_parts.append(dbg + _dot_tn(dcb_m, cg))
        dxdt = jnp.concatenate(dxdt_parts, axis=1)
        dxv = dx_ref[...] * dyv + dxdt * dt_x
        et = et_ref[...]
        head_sum = lambda v: _dot(v.astype(_BF16), et)
        ddt = head_sum(dxdt * xv)
        dcs = dcs - dcst_ref[...].T + head_sum(jnp.concatenate(dcs_x_parts, axis=1))
        ddec = head_sum(jnp.concatenate(ddec_x_parts, axis=1)) * q["dec"]
        dcs = dcs - ddec
        dcl = jnp.sum(ddec, axis=0, keepdims=True) + _xdot(
            _dot, jnp.broadcast_to(jnp.concatenate(dcl_x_parts, axis=1), (8, SSD_W)), et)[0:1, :]
        dcs = jnp.where(row == CH - 1, dcs + dcl, dcs)
        da = _xdot_left(_dot_tn, ltri_ref[...], dcs)
        ddt = ddt + da * q["a_neg"]
        dalog_ref[...] += _sum8(da * q["dt"] * q["a_neg"])
        ddtraw = jnp.where(lane < SSD_H, ddt * _sigmoid(q["dtin"]), 0.0)
        ddtb_ref[...] += _sum8(ddtraw)
        dproj_ref[:, D_IN_PAD - 128:D_IN_PAD] = ddtraw.astype(dproj_ref.dtype)
        dcpre = jnp.concatenate([dxv] + db_parts + dc_parts, axis=1) * _silu_grad(sig_ref[...], act)
        dcb_ref[...] += _sum8(dcpre)
        next8 = dnext_ref[...]
        ups = [_shift_up(dcpre, next8, j) for j in range(CONV_K)]
        xbc = xbc_ref[...]
        for k in range(CONV_K):
            dcw_ref[k:k + 1, :] += jnp.sum(xbc * ups[CONV_K - 1 - k], axis=0, keepdims=True)
        dxbc = sum(cw_ref[k:k + 1, :] * ups[CONV_K - 1 - k] for k in range(CONV_K))
        dproj_ref[:, 2 * GM_W + SSD_W:2 * GM_W + SSD_W + CONV_CH] = dxbc.astype(dproj_ref.dtype)
        dnext_ref[...] = dcpre[0:8, :]

        if finalize is not None:
            @pl.when(finalize)
            def _():
                for h in range(GM_H):
                    dbs_ref[h:h + 1, :] = _xdot_left(_dot_nt, jnp.ones((8, 128), _BF16), dbacc_ref[h])[0:1, :]
                dd_ref[...] = _xdot(_dot, ddacc_ref[...], et)

    rblk = lambda w, j: pl.BlockSpec((sb, CH, w), lambda g, i: (g, seq_chunks - 1 - i, j))
    full = lambda arr: pl.BlockSpec(arr.shape, lambda g, i: (0,) * arr.ndim)
    acc = lambda shape: pl.BlockSpec(shape, lambda g, i: (0,) * len(shape))
    consts = [gv, ws, bsb, gout, cw8, cb, dtb, alog, d_x, ng, e_mat, et_mat, ltri_mat]
    acc_shapes = [(GM_H, CH, CH), (8, 128), (8, GM_W), (8, GM_W), (8, SSD_W), (8, CONV_CH), (8, CONV_CH), (8, 128),
                  (8, 128), (8, 128)]
    res = pl.pallas_call(
        body, name="mixer_bwd", grid=(n_groups, seq_chunks),
        in_specs=[rblk(GM_W, 0), rblk(GM_W, 1), rblk(SSD_W, 2), rblk(CONV_CH, 2), rblk(128, DT_BLK),
                  rblk(CONV_CH, 0), rblk(CONV_CH, 0),
                  rblk(2 * D, 0), rblk(SSD_W, 0), rblk(SSD_W, 0)] + [full(a) for a in consts] + c_in_specs,
        out_specs=[rblk(D_IN_PAD, 0)] + [acc(s) for s in acc_shapes] + c_out_specs,
        out_shape=[jax.ShapeDtypeStruct((n_seq, seq, D_IN_PAD), _MXU)]
        + [jax.ShapeDtypeStruct(s, _F32) for s in acc_shapes] + c_out_shape,
        scratch_shapes=[pltpu.VMEM((sb, SSD_N, SSD_W), _F32), pltpu.VMEM((sb, 8, CONV_CH), _F32),
                        pltpu.VMEM((sb, 128, CH), _F32), pltpu.VMEM((sb, 128, CH), _F32),
                        pltpu.VMEM((GM_H, CH, 128), _F32), pltpu.VMEM((8, SSD_W), _F32)] + c_scratch,
        compiler_params=pltpu.CompilerParams(dimension_semantics=("arbitrary", "arbitrary"),
                                             vmem_limit_bytes=48 << 20),
    )(proj, proj, proj, proj, proj, act, sig, dcat, yss, states, *consts, *c_in)
    return [res[0].reshape(t, D_IN_PAD)] + list(res[1:11]), res[11:]


def _peers():
    x, y, c = lax.axis_index("x"), lax.axis_index("y"), lax.axis_index("c")
    out = []
    for k in range(1, N_DEV):
        fx, fy, fc = (k >> 2) & 1, (k >> 1) & 1, k & 1
        px, py, pc = (x + fx) % 2, (y + fy) % 2, (c + fc) % 2
        out.append((k - 1, (px, py, pc), 4 * px + 2 * py + pc))
    return out, 4 * x + 2 * y + c


def _comm_io(comm):
    any_spec = pl.BlockSpec(memory_space=pl.ANY)
    n = len(comm)
    out_shape = []
    for (kind, axis), src in comm:
        shp = list(src.shape)
        if kind in ("gather", "gather2"):
            shp[axis] *= N_DEV
        else:
            shp[axis] //= N_DEV
            shp = [N_DEV] + shp
        out_shape.append(jax.ShapeDtypeStruct(tuple(shp), src.dtype))
    scratch = [pltpu.SemaphoreType.DMA((n * (N_DEV - 1),)), pltpu.SemaphoreType.DMA((n * (N_DEV - 1),)),
               pltpu.SemaphoreType.DMA((n,))] if n else []
    return [src for _, src in comm], [any_spec] * n, [any_spec] * n, out_shape, scratch


def _window(ref, axis, idx, size):
    start = pl.multiple_of(idx * size, size)
    return ref.at[tuple(pl.ds(start, size) if a == axis else slice(None) for a in range(len(ref.shape)))]


def _comm_plans(kinds, src_refs, dst_refs, send_sems, recv_sems, local_sems):
    x, y, c = lax.axis_index("x"), lax.axis_index("y"), lax.axis_index("c")
    peers, me = _peers()
    plans = []
    for s, ((kind, axis), src, dst) in enumerate(zip(kinds, src_refs, dst_refs)):
        sems = lambda k: dict(send_sem=send_sems.at[s * (N_DEV - 1) + k], recv_sem=recv_sems.at[s * (N_DEV - 1) + k])
        remote = lambda src_ref, dst_ref, k, pid: pltpu.make_async_remote_copy(
            src_ref=src_ref, dst_ref=dst_ref, device_id=pid, device_id_type=_MESH, **sems(k))
        if kind == "gather2":
            size = src.shape[axis]
            win = lambda idx: _window(dst, axis, idx, size)
            sib, sib_idx = (x, y, 1 - c), 4 * x + 2 * y + (1 - c)
            local = pltpu.make_async_copy(src, win(me), local_sems.at[s])
            to_sib = remote(src, win(me), 0, sib)
            starts, forwards = [local, to_sib], []
            waits = [(local, "local"), (to_sib, "send"), (remote(src, win(sib_idx), 0, sib), "recv")]
            for j, (fx, fy) in enumerate(((1, 0), (0, 1), (1, 1))):
                px, py = (x + fx) % 2, (y + fy) % 2
                same, other = 4 * px + 2 * py + c, 4 * px + 2 * py + (1 - c)
                out = remote(src, win(me), 1 + j, (px, py, c))
                starts.append(out)
                passed = remote(win(same), win(same), 4 + j, sib)
                forwards.append((remote(src, win(same), 1 + j, (px, py, c)), passed))
                waits += [(out, "send"), (passed, "send"), (remote(win(other), win(other), 4 + j, sib), "recv")]
            plans.append((starts, forwards, waits))
            continue
        if kind == "gather":
            size = src.shape[axis]
            src_for = lambda pidx: src
            dst_mine = _window(dst, axis, me, size)
        else:
            size = src.shape[axis] // N_DEV
            src_for = lambda pidx: _window(src, axis, pidx, size)
            dst_mine = dst.at[me]
        local = pltpu.make_async_copy(src_for(me), dst_mine, local_sems.at[s])
        remotes = [remote(src_for(pidx), dst_mine, k, pid) for k, pid, pidx in peers]
        plans.append(([local] + remotes, [], [(local, "local")] + [(cp, "both") for cp in remotes]))
    return plans


def _comm_start(*refs):
    for starts, _, _ in _comm_plans(*refs):
        for cp in starts:
            cp.start()


def _comm_finish(*refs):
    for _, forwards, waits in _comm_plans(*refs):
        for arrival, cp in forwards:
            arrival.wait_recv()
            cp.start()
        for cp, what in waits:
            if what == "send":
                cp.wait_send()
            elif what == "recv":
                cp.wait_recv()
            else:
                cp.wait()


def _adam_vals(w, g, m, v):
    m = B1 * m + (1.0 - B1) * g
    v = B2 * v + (1.0 - B2) * (g * g)
    m_hat = m / (1.0 - B1 ** STEP)
    v_hat = v / (1.0 - B2 ** STEP)
    delta = -LR * (m_hat / (jnp.sqrt(v_hat) + ADAM_EPS) + WD * w)
    return delta, m, v


def _sum_adam(name, recv, w, m, v, tile=256):
    _, r, wd = recv.shape
    if r % min(tile, r) == 0:
        tr, tc = min(tile, r), wd
    else:
        tr, tc = r, tile
        assert wd % tc == 0, (name, r, wd)

    def body(recv_ref, w_ref, m_ref, v_ref, g_out, d_out, m_out, v_out):
        g = recv_ref[0].astype(_F32)
        for s in range(1, N_DEV):
            g = g + recv_ref[s].astype(_F32)
        d_, m_, v_ = _adam_vals(w_ref[...], g, m_ref[...], v_ref[...])
        g_out[...] = g
        d_out[...] = d_
        m_out[...] = m_
        v_out[...] = v_

    spec = pl.BlockSpec((tr, tc), lambda i, j: (i, j))
    return pl.pallas_call(
        body, name=name, grid=(r // tr, wd // tc),
        in_specs=[pl.BlockSpec((N_DEV, tr, tc), lambda i, j: (0, i, j)), spec, spec, spec],
        out_specs=[spec] * 4, out_shape=[jax.ShapeDtypeStruct((r, wd), _F32)] * 4,
        compiler_params=pltpu.CompilerParams(dimension_semantics=("parallel", "parallel"),
                                             vmem_limit_bytes=48 << 20),
    )(recv, w, m, v)


def _small_local(parts, segments, n_rows):
    def body(parts_ref, loc_ref):
        loc_ref[...] = jnp.zeros_like(loc_ref)
        for out_row, n_out, in_row, n_in, kind in segments:
            if kind == "copy":
                loc_ref[out_row:out_row + n_out, :] = parts_ref[in_row:in_row + n_in, :]
            else:
                s = jnp.sum(parts_ref[in_row:in_row + n_in, :], axis=0, keepdims=True)
                if kind == "loss":
                    s = jnp.broadcast_to(jnp.sum(s, axis=1, keepdims=True) * (0.5 / D), (1, D))
                loc_ref[out_row:out_row + 1, :] = s

    vm = pl.BlockSpec(memory_space=pltpu.VMEM)
    return pl.pallas_call(body, name="small_local", in_specs=[vm], out_specs=vm,
                          out_shape=jax.ShapeDtypeStruct((n_rows, D), _F32))(parts)


def _small_final(blocks, late8, late_row, w, m, v):
    n_rows = w.shape[0]

    def body(blocks_ref, late_ref, w_ref, m_ref, v_ref, g_out, d_out, m_out, v_out, loc_ref, recv_ref, send_sems,
             recv_sems):
        peers, me = _peers()
        loc_ref[...] = jnp.broadcast_to(jnp.sum(late_ref[...], axis=0, keepdims=True), (8, D))
        recv_ref[me] = loc_ref[...]
        copies = [pltpu.make_async_remote_copy(src_ref=loc_ref, dst_ref=recv_ref.at[me], send_sem=send_sems.at[k],
                                               recv_sem=recv_sems.at[k], device_id=pid, device_id_type=_MESH)
                  for k, pid, _ in peers]
        for cp in copies:
            cp.start()
        g = blocks_ref[0:n_rows, :]
        for s in range(1, N_DEV):
            g = g + blocks_ref[s * n_rows:(s + 1) * n_rows, :]
        for cp in copies:
            cp.wait()
        late = recv_ref[0]
        for s in range(1, N_DEV):
            late = late + recv_ref[s]
        g = jnp.where(_iota2((n_rows, D), 0) == late_row, jnp.broadcast_to(late[0:1, :], (n_rows, D)), g)
        d_, m_, v_ = _adam_vals(w_ref[...], g, m_ref[...], v_ref[...])
        g_out[...] = g
        d_out[...] = d_
        m_out[...] = m_
        v_out[...] = v_

    vm = pl.BlockSpec(memory_space=pltpu.VMEM)
    return pl.pallas_call(
        body, name="small_final", in_specs=[vm] * 5, out_specs=[vm] * 4,
        out_shape=[jax.ShapeDtypeStruct((n_rows, D), _F32)] * 4,
        scratch_shapes=[pltpu.VMEM((8, D), _F32), pltpu.VMEM((N_DEV, 8, D), _F32),
                        pltpu.SemaphoreType.DMA((N_DEV - 1,)), pltpu.SemaphoreType.DMA((N_DEV - 1,))],
        compiler_params=pltpu.CompilerParams(vmem_limit_bytes=48 << 20),
    )(blocks, late8, w, m, v)


_BIG_NAMES =("w_in", "w_out", "w_ff1", "w_ff2", "w_ple_gate", "w_ple_proj")

_G_VECS = ("norm_mix_g", "gm_v_norm_g", "gm_out_norm_g", "ssd_norm_g", "norm_mlp_g", "ple_norm_g", "final_norm_g")
_LATE = _G_VECS[0]


def _const_mats():
    h = np.arange(128)[:, None]
    ch = np.arange(SSD_W)[None, :]
    e = (ch // SSD_P == h).astype(np.float32)
    ltri = (np.arange(CH)[:, None] >= np.arange(CH)[None, :]).astype(np.float32)
    return jnp.asarray(e, _BF16), jnp.asarray(e.T, _BF16), jnp.asarray(ltri, _BF16)


def _pad_lanes(v, n=128):
    return jnp.pad(v, ((0, 0), (0, n - v.shape[1])))


def _local_step(x, p, tgt, shard, conv_w_shard, small, seq_len):
    seq_chunks = seq_len // CH
    e_mat, et_mat, ltri_mat = _const_mats()
    g_mix, g_mlp, g_ple = small["norm_mix_g"], small["norm_mlp_g"], small["ple_norm_g"]
    g_fin = small["final_norm_g"].reshape(1, D)
    gv, gout, ng = small["gm_v_norm_g"], small["gm_out_norm_g"], small["ssd_norm_g"]
    ws = small["gm_ws"][0]
    bsb = jnp.broadcast_to(small["gm_bs"][0][:, :, None], (GM_H, CH, 128))
    cb = small["ssd_conv_b"]
    dtb, alog = _pad_lanes(small["ssd_dt_bias"]), _pad_lanes(small["ssd_a_log"])
    d_x = jnp.repeat(small["ssd_d"], SSD_P, axis=1)

    first = lambda acc: (acc,)
    rows, cols = ("gather2", 0), ("gather2", 1)
    n1, (g_win, g_cw) = _norm_cast("norm_mix", x, g_mix,
                                   comm=[(rows, shard["w_in"][None]), (("gather", 0), conv_w_shard[None])])
    w_in_t = jnp.pad(g_win.reshape(D_IN, D), ((0, D_IN_PAD - D_IN), (0, 0)))
    cw8 = jnp.pad(g_cw.transpose(1, 0, 2).reshape(CONV_K, CONV_CH), ((0, 8 - CONV_K), (0, 0)))
    mix_consts = (gv, ws, bsb, gout, cw8, cb, dtb, alog, d_x, ng)
    (proj,), (w_out, w1) = _matmul("proj_in", n1, w_in_t, "nt", 512, D_IN_PAD, D, first, [("tile", _F32)],
                                   comm=[(rows, shard["w_out"]), (cols, shard["w_ff1"])])
    (cat, yss, states, conv_act, conv_sig), (w2, wg, wp) = _mixer_fwd(
        proj, *mix_consts, e_mat, ltri_mat, seq_chunks,
        comm=[(rows, shard["w_ff2"]), (rows, shard["w_ple_gate"]), (cols, shard["w_ple_proj"])])

    def epi_res_norm(acc, res, g):
        hv = acc + res
        return hv, _rms(hv, g)

    (h1, n2), _ = _matmul("proj_out", cat, w_out, "nn", 1024, D, 2 * D, epi_res_norm,
                          [("tile", _F32), ("tile", _MXU)], extras=[(x, "tile"), (g_mlp, "row")])

    def epi_relu2(acc):
        hid = jnp.maximum(acc, 0.0)
        return hid, hid * hid

    (hid, hid2), _ = _matmul("ff1", n2, w1, "nn", 1024, 1024, D, epi_relu2, [("tile", _MXU), ("tile", _MXU)])
    (h2, n3), _ = _matmul("ff2", hid2, w2, "nn", 512, D, D_FF, epi_res_norm, [("tile", _F32), ("tile", _MXU)],
                          extras=[(h1, "tile"), (g_ple, "row")])

    def epi_norm_bwd(acc, up, hv, g):
        dx, dg8 = _rms_bwd(hv, g, acc)
        dh = up + dx
        return dh, dh, dg8

    def epi_head(acc, wg_v, p_rows, wp_v, h2v, tg, gf, gp_):
        ppv = _dot(_mx(p_rows), wp_v)
        gate = _sigmoid(acc)
        gp = gate * ppv
        h3 = h2v + gp
        r = lax.rsqrt(jnp.mean(h3 * h3, axis=-1, keepdims=True) + EPS)
        nh = h3 * r
        err = nh * gf - tg
        gy = err * (gf * (1.0 / D))
        dh3 = (gy - nh * jnp.mean(gy * nh, axis=-1, keepdims=True)) * r
        dpp = dh3 * gate
        da3 = dpp * (ppv - gp)
        dh2, dh2_again, dgple8 = epi_norm_bwd(_dot_nt(_mx(da3), wg_v), dh3, h2v, gp_)
        return da3, dpp, dh2, dh2_again, _sum8(err * err), _sum8(err * nh) * (1.0 / D), dgple8

    (da3, dpp, dh2, dh2b, lossp, dgfin, dgple), _ = _matmul(
        "ple_gate_loss_bwd", n3, wg, "nn", 512, D, D, epi_head,
        [("tile", _MXU), ("tile", _MXU), ("tile", _F32), ("tile", _MXU), ("part8", _F32), ("part8", _F32),
         ("part8", _F32)],
        extras=[(p, "rows"), (wp, "full"), (h2, "tile"), (tgt, "tile"), (g_fin, "row"), (g_ple, "row")],
        b_to_epilogue=True)

    s_rows, s_cols = ("scatter", 0), ("scatter", 1)
    (dwp,), _ = _matmul("d_w_ple_proj", p, dpp, "tn", D_PLE, D, 2048, first, [("tile", _BF16)])
    (dwg,), _ = _matmul("d_w_ple_gate", n3, da3, "tn", D, D, 2048, first, [("tile", _BF16)])
    (dw2,), _ = _matmul("d_w_ff2", hid2, dh2b, "tn", 1024, D, 4096, first, [("tile", _BF16)])
    (da1,), _ = _matmul("d_ff_hidden", dh2b, w2, "nt", 512, 2048, D,
                        lambda acc, hv: (acc * 2.0 * hv.astype(_F32),), [("tile", _MXU)], extras=[(hid, "tile")])
    (dw1,), _ = _matmul("d_w_ff1", n2, da1, "tn", 1024, 1024, 4096, first, [("tile", _BF16)])
    (dh1, dh1b, dgmlp), _ = _matmul(
        "d_h1", da1, w1, "nt", 256, D, D_FF, epi_norm_bwd, [("tile", _F32), ("tile", _MXU), ("part8", _F32)],
        extras=[(dh2, "tile"), (h1, "tile"), (g_mlp, "row")])
    (dwout,), _ = _matmul("d_w_out", cat, dh1b, "tn", 1024, D, 2048, first, [("tile", _BF16)])
    (dcat,), _ = _matmul("d_cat", dh1b, w_out, "nt", 1024, 1024, D, first, [("tile", _MXU)])
    (dproj, dws, dbs, dgv, dgout, dng, dcw, dcb, ddtb, dalog, dd), (r_wp, r_wg, r_w2, r_w1, r_wout) = _mixer_bwd(
        proj, conv_act, conv_sig, dcat, yss, states, *mix_consts, e_mat, et_mat, ltri_mat, seq_chunks,
        comm=[(s_cols, dwp), (s_rows, dwg), (s_rows, dw2), (s_cols, dw1), (s_rows, dwout)])
    pieces = dict(gm_v_norm_g=dgv, gm_out_norm_g=dgout, ssd_norm_g=dng, norm_mlp_g=dgmlp, ple_norm_g=dgple,
                  final_norm_g=dgfin, gm_ws=dws, gm_bs=dbs, ssd_conv_w=dcw, ssd_conv_b=dcb, ssd_dt_bias=ddtb,
                  ssd_a_log=dalog, ssd_d=dd, loss=lossp)
    parts, segments, n_rows, where = _small_layout(pieces)
    small_block = _small_local(parts, segments, n_rows)
    (dwin_t,), (small_blocks,) = _matmul("d_w_in", n1, dproj, "tn", 512, D_IN_PAD, 1024, lambda acc: (acc.T,),
                                         [("tile_t", _BF16)], comm=[(("gather", 0), small_block)])
    dwin_blocks = dwin_t[:D_IN].reshape(N_DEV, SHARD_IN, D)
    (gx, dgmix), (r_win,) = _matmul(
        "d_x", dproj, w_in_t, "nn", 256, D, D_IN_PAD, lambda *a: epi_norm_bwd(*a)[1:],
        [("tile", _F32), ("part8", _F32)], extras=[(dh1, "tile"), (x, "tile"), (g_mix, "row")],
        comm=[(s_rows, dwin_blocks)])
    r_win = r_win.reshape(N_DEV, SHARD_IN, D)

    big = dict(w_in=r_win, w_out=r_wout, w_ff1=r_w1, w_ff2=r_w2, w_ple_gate=r_wg, w_ple_proj=r_wp)
    return gx, big, small_blocks, dgmix, n_rows, where


def _small_layout(pieces):
    rows, segments = [], []
    in_row, out_row = 0, 0

    def add(arr, kind, n_out, new_group=True):
        nonlocal in_row, out_row
        if new_group:
            out_row = -(-out_row // 8) * 8
        rows.append(arr)
        segments.append((out_row, n_out, in_row, arr.shape[0], kind))
        start = out_row
        in_row += arr.shape[0]
        out_row += n_out
        return start

    where = {_LATE: 0}
    out_row = 1
    for name in _G_VECS[1:]:
        where[name] = add(pieces[name], "sum", 1)
    where["gm_ws"] = add(pieces["gm_ws"].reshape(GM_H * CH * CH // D, D), "copy", GM_H * CH * CH // D)
    where["gm_bs"] = add(jnp.pad(pieces["gm_bs"].reshape(1, D), ((0, 7), (0, 0))), "sum", 1)
    cb = jnp.pad(pieces["ssd_conv_b"], ((0, 0), (0, 2 * D - CONV_CH)))
    where["ssd_conv_b"] = add(cb[:, :D], "sum", 1)
    add(cb[:, D:], "sum", 1, new_group=False)
    cw = jnp.pad(pieces["ssd_conv_w"][:CONV_K], ((0, 0), (0, 2 * D - CONV_CH)))
    where["ssd_conv_w"] = add(cw.reshape(2 * CONV_K, D), "copy", 2 * CONV_K)
    misc = jnp.concatenate([pieces["ssd_dt_bias"], pieces["ssd_a_log"], pieces["ssd_d"],
                            jnp.zeros((8, D - 3 * 128), _F32)], axis=1)
    where["misc"] = add(misc, "sum", 1)
    where["loss"] = add(pieces["loss"], "loss", 1)
    n_rows = -(-out_row // 8) * 8
    return jnp.concatenate(rows, axis=0), tuple(segments), n_rows, where


def _pack_small_params(vals, where, n_rows, my_block):
    rows, at = [], {}

    def add(name, arr):
        at[name] = sum(r.shape[0] for r in rows)
        rows.append(jnp.pad(arr, ((0, -arr.shape[0] % 8), (0, 0))))

    for name in _G_VECS:
        add(name, vals[name].reshape(1, D))
    add("gm_ws", vals["gm_ws"].reshape(GM_H * CH * CH // D, D))
    add("gm_bs", vals["gm_bs"].reshape(1, D))
    cb = jnp.pad(vals["ssd_conv_b"].reshape(1, CONV_CH), ((0, 0), (0, 2 * D - CONV_CH)))
    add("ssd_conv_b", cb.reshape(2, D))
    cw = lax.dynamic_update_slice(jnp.zeros((CONV_K, 2 * D), _F32), vals["ssd_conv_w"].reshape(CONV_K, -1),
                                  (0, my_block * (CONV_CH // N_DEV)))
    add("ssd_conv_w", cw.reshape(2 * CONV_K, D))
    misc = jnp.concatenate([_pad_lanes(vals["ssd_dt_bias"].reshape(1, SSD_H)),
                            _pad_lanes(vals["ssd_a_log"].reshape(1, SSD_H)),
                            _pad_lanes(vals["ssd_d"].reshape(1, SSD_H)), jnp.zeros((1, D - 3 * 128), _F32)], axis=1)
    add("misc", misc)
    assert all(where[k] == r for k, r in at.items()), (where, at)
    rows.append(jnp.zeros((n_rows - sum(r.shape[0] for r in rows), D), _F32))
    return jnp.concatenate(rows, axis=0)


def _unpack_small(buf, where, my_block, shapes):
    out = {}
    for name in _G_VECS:
        out[name] = buf[where[name]].reshape(shapes[name])
    n_ws = GM_H * CH * CH // D
    out["gm_ws"] = buf[where["gm_ws"]:where["gm_ws"] + n_ws].reshape(shapes["gm_ws"])
    out["gm_bs"] = buf[where["gm_bs"]].reshape(shapes["gm_bs"])
    r = where["ssd_conv_b"]
    out["ssd_conv_b"] = buf[r:r + 2].reshape(1, 2 * D)[:, :CONV_CH].reshape(shapes["ssd_conv_b"])
    r = where["ssd_conv_w"]
    cw = buf[r:r + 2 * CONV_K].reshape(CONV_K, 2 * D)
    out["ssd_conv_w"] = lax.dynamic_slice(cw, (0, my_block * (CONV_CH // N_DEV)),
                                          (CONV_K, CONV_CH // N_DEV)).reshape(shapes["ssd_conv_w"])
    misc = buf[where["misc"]]
    for i, name in enumerate(("ssd_dt_bias", "ssd_a_log", "ssd_d")):
        out[name] = misc[i * 128:i * 128 + SSD_H].reshape(shapes[name])
    return out


_WEIGHTS = ("norm_mix_g", "w_in", "gm_v_norm_g", "gm_ws", "gm_bs", "gm_out_norm_g", "ssd_conv_w", "ssd_conv_b",
            "ssd_dt_bias", "ssd_a_log", "ssd_d", "ssd_norm_g", "w_out", "norm_mlp_g", "w_ff1", "w_ff2", "ple_norm_g",
            "w_ple_gate", "w_ple_proj", "final_norm_g")


def kernel(x, p, norm_mix_g, w_in, gm_v_norm_g, gm_ws, gm_bs, gm_out_norm_g, ssd_conv_w, ssd_conv_b, ssd_dt_bias, ssd_a_log, ssd_d, ssd_norm_g, w_out, norm_mlp_g, w_ff1, w_ff2, ple_norm_g, w_ple_gate, w_ple_proj, final_norm_g, loss_target, m_norm_mix_g, m_w_in, m_gm_v_norm_g, m_gm_ws, m_gm_bs, m_gm_out_norm_g, m_ssd_conv_w, m_ssd_conv_b, m_ssd_dt_bias, m_ssd_a_log, m_ssd_d, m_ssd_norm_g, m_w_out, m_norm_mlp_g, m_w_ff1, m_w_ff2, m_ple_norm_g, m_w_ple_gate, m_w_ple_proj, m_final_norm_g, v_norm_mix_g, v_w_in, v_gm_v_norm_g, v_gm_ws, v_gm_bs, v_gm_out_norm_g, v_ssd_conv_w, v_ssd_conv_b, v_ssd_dt_bias, v_ssd_a_log, v_ssd_d, v_ssd_norm_g, v_w_out, v_norm_mlp_g, v_w_ff1, v_w_ff2, v_ple_norm_g, v_w_ple_gate, v_w_ple_proj, v_final_norm_g):
    args = dict(locals())
    w = {n: args[n] for n in _WEIGHTS}
    m = {n: args["m_" + n] for n in _WEIGHTS}
    v = {n: args["v_" + n] for n in _WEIGHTS}
    shapes = {n: w[n].shape for n in _WEIGHTS}
    my_block = 4 * lax.axis_index("x") + 2 * lax.axis_index("y") + lax.axis_index("c")
    nb, seq_len, _ = x.shape

    local = lambda d, n: d[n][0].T if n == "w_in" else d[n][0]
    shard = {n: local(w, n).astype(_MXU) for n in _BIG_NAMES}
    small = {n: w[n] for n in _WEIGHTS if n not in _BIG_NAMES}
    gx, recv, small_blocks, late8, n_rows, where = _local_step(
        x.reshape(nb * seq_len, D), p.reshape(nb * seq_len, D_PLE), loss_target.reshape(nb * seq_len, D), shard,
        ssd_conv_w[0], small, seq_len)

    big_out = [{}, {}, {}, {}]
    for n in _BIG_NAMES:
        res = _sum_adam("sum_adam_" + n, recv[n], local(w, n), local(m, n), local(v, n))
        for k in range(4):
            big_out[k][n] = (res[k].T if n == "w_in" else res[k]).reshape(shapes[n])

    packs = [_pack_small_params(d, where, n_rows, my_block) for d in (w, m, v)]
    small_res = _small_final(small_blocks, late8, where[_LATE], *packs)
    loss = small_res[0][where["loss"], 0]
    small_out = [_unpack_small(a, where, my_block, shapes) for a in small_res]

    outs = [loss, gx.reshape(x.shape)]
    for k in range(4):
        outs += [big_out[k][n] if n in _BIG_NAMES else small_out[k][n] for n in _WEIGHTS]
    return tuple(outs)
```

```python
import math

import jax
import jax.numpy as jnp
import numpy as np
from jax import lax
from jax.experimental import pallas as pl
from jax.experimental.pallas import tpu as pltpu

_F32 = jnp.float32
_BF16 = jnp.bfloat16
_MXU = jnp.bfloat16

D = 1024
D_PLE = 256
GM_W = 1024
GM_H = 8
CH = 128
SSD_W = 1024
SSD_H = 16
SSD_P = 64
SSD_G = 2
SSD_N = 128
CONV_K = 4
CONV_CH = SSD_W + 2 * SSD_G * SSD_N
D_FF = 4096
D_IN = 2 * GM_W + SSD_W + CONV_CH + SSD_H
D_IN_PAD = 4736
EPS = 1e-6
DT_BLK = (D_IN_PAD - 128) // 128
N_DEV = 8
SHARD_IN = D_IN // N_DEV

LR, B1, B2, ADAM_EPS, WD, STEP = 0.001, 0.9, 0.999, 1e-08, 0.01, 10
_LOG2E = math.log2(math.e)
_LOG2_INV_SQRT_2PI = -0.5 * math.log2(2.0 * math.pi)

_SEQ_PER_STEP = 2
_V7X_VMEM_BYTES = 64 * 1024 * 1024
_VMEM_CAP = _V7X_VMEM_BYTES - 8 * 1024 * 1024
_MESH = pl.DeviceIdType.MESH


def _vmem_limit(nbytes):
    return int(min(_VMEM_CAP, max(32 * 1024 * 1024, nbytes * 5 // 4 + (4 << 20))))


def _nbytes(shape, dtype):
    return int(np.prod(shape)) * jnp.dtype(dtype).itemsize


def _mx(v):
    return v.astype(_MXU)


def _dot(a, b):
    return jnp.dot(a, b, preferred_element_type=_F32)


def _dot_nt(a, b):
    return lax.dot_general(a, b, (((1,), (1,)), ((), ())), preferred_element_type=_F32)


def _dot_tn(a, b):
    return lax.dot_general(a, b, (((0,), (0,)), ((), ())), preferred_element_type=_F32)


def _split3(a):
    hi = a.astype(_BF16)
    r = a - hi.astype(_F32)
    mid = r.astype(_BF16)
    lo = (r - mid.astype(_F32)).astype(_BF16)
    return hi, mid, lo


def _xdot(dotfn, a, b01):
    b = b01.astype(_BF16)
    hi, mid, lo = _split3(a)
    return (dotfn(hi, b) + dotfn(mid, b)) + dotfn(lo, b)


def _xdot_left(dotfn, a01, b):
    a = a01.astype(_BF16)
    hi, mid, lo = _split3(b)
    return (dotfn(a, hi) + dotfn(a, mid)) + dotfn(a, lo)


def _sum8(v):
    r, n = v.shape
    return v.reshape(r // 8, 8, n).sum(axis=0)


def _sigmoid(v):
    return 1.0 / (1.0 + jnp.exp(-v))


def _rms(xv, g):
    ms = jnp.mean(xv * xv, axis=-1, keepdims=True)
    return xv * lax.rsqrt(ms + EPS) * g


def _rms_bwd(xv, g, dn):
    r = lax.rsqrt(jnp.mean(xv * xv, axis=-1, keepdims=True) + EPS)
    nh = xv * r
    gy = dn * g
    dx = (gy - nh * jnp.mean(gy * nh, axis=-1, keepdims=True)) * r
    return dx, _sum8(dn * nh)


def _iota2(shape, axis):
    return lax.broadcasted_iota(jnp.int32, shape, axis)


def _norm_cast(name, x, g, tm=512, comm=()):
    t, n = x.shape
    tm = min(tm, t)
    steps = t // tm
    kinds = [kind for kind, _ in comm]
    c_in, c_in_specs, c_out_specs, c_out_shape, c_scratch = _comm_io(comm)

    def body(*refs):
        x_ref, g_ref = refs[0], refs[1]
        o_ref = refs[2 + len(comm)]
        comm_refs = (kinds, refs[2:2 + len(comm)], refs[3 + len(comm):3 + 2 * len(comm)], *refs[3 + 2 * len(comm):])
        if comm:
            pl.when(pl.program_id(0) == 0)(lambda: _comm_start(*comm_refs))

        o_ref[...] = _rms(x_ref[...], g_ref[...]).astype(o_ref.dtype)
        if comm:
            pl.when(pl.program_id(0) == steps - 1)(lambda: _comm_finish(*comm_refs))

    res = pl.pallas_call(
        body, name=name, grid=(steps,),
        in_specs=[pl.BlockSpec((tm, n), lambda i: (i, 0)), pl.BlockSpec((1, n), lambda i: (0, 0))] + c_in_specs,
        out_specs=[pl.BlockSpec((tm, n), lambda i: (i, 0))] + c_out_specs,
        out_shape=[jax.ShapeDtypeStruct((t, n), _MXU)] + c_out_shape, scratch_shapes=c_scratch,
        compiler_params=pltpu.CompilerParams(dimension_semantics=("arbitrary",)),
    )(x, g, *c_in)
    return res[0], res[1:]


def _matmul(name, a, b, mode, tm, tn, tk, epilogue, outs, extras=(), comm=(), b_to_epilogue=False):
    m, k = a.shape[::-1] if mode == "tn" else a.shape
    n = b.shape[0] if mode == "nt" else b.shape[1]
    tm, tn, tk = min(tm, m), min(tn, n), min(tk, k)
    assert m % tm == 0 and n % tn == 0 and k % tk == 0, (name, m, n, k, tm, tn, tk)
    if mode == "nn":
        a_spec = pl.BlockSpec((tm, tk), lambda i, j, kk: (i, kk))
        b_spec = pl.BlockSpec((tk, tn), lambda i, j, kk: (kk, j))
        dotfn = _dot
    elif mode == "nt":
        a_spec = pl.BlockSpec((tm, tk), lambda i, j, kk: (i, kk))
        b_spec = pl.BlockSpec((tn, tk), lambda i, j, kk: (j, kk))
        dotfn = _dot_nt
    else:
        a_spec = pl.BlockSpec((tk, tm), lambda i, j, kk: (kk, i))
        b_spec = pl.BlockSpec((tk, tn), lambda i, j, kk: (kk, j))
        dotfn = _dot_tn
    ni, nj, nk = m // tm, n // tn, k // tk
    if mode != "tn" and nj == 1 and nk == 1 and ni > 2:
        b_spec = pl.BlockSpec(b_spec.block_shape, b_spec.index_map, pipeline_mode=pl.Buffered(1))
    n_ex, n_out, n_comm = len(extras), len(outs), len(comm)
    kinds = [kind for kind, _ in comm]
    c_in, c_in_specs, c_out_specs, c_out_shape, c_scratch = _comm_io(comm)

    in_specs, vmem = [a_spec, b_spec], 2 * (tm * tk * a.dtype.itemsize + tk * tn * b.dtype.itemsize)
    for arr, kind in extras:
        if kind == "tile":
            in_specs.append(pl.BlockSpec((tm, tn), lambda i, j, kk: (i, j)))
            vmem += 2 * _nbytes((tm, tn), arr.dtype)
        elif kind == "rows":
            in_specs.append(pl.BlockSpec((tm, arr.shape[1]), lambda i, j, kk: (i, 0)))
            vmem += 2 * _nbytes((tm, arr.shape[1]), arr.dtype)
        elif kind == "full":
            in_specs.append(pl.BlockSpec(arr.shape, lambda i, j, kk: (0,) * arr.ndim))
            vmem += 2 * _nbytes(arr.shape, arr.dtype)
        else:
            in_specs.append(pl.BlockSpec((1, tn), lambda i, j, kk: (0, j)))
    out_specs, out_shape = [], []
    for kind, dt in outs:
        if kind == "tile":
            out_specs.append(pl.BlockSpec((tm, tn), lambda i, j, kk: (i, j)))
            out_shape.append(jax.ShapeDtypeStruct((m, n), dt))
            vmem += 2 * _nbytes((tm, tn), dt)
        elif kind == "tile_t":
            out_specs.append(pl.BlockSpec((tn, tm), lambda i, j, kk: (j, i)))
            out_shape.append(jax.ShapeDtypeStruct((n, m), dt))
            vmem += 2 * _nbytes((tm, tn), dt)
        else:
            assert nj == 1, "the partial-sum rows are accumulated over consecutive row tiles"
            out_specs.append(pl.BlockSpec((8, tn), lambda i, j, kk: (0, 0)))
            out_shape.append(jax.ShapeDtypeStruct((8, n), dt))
    scratch = [pltpu.VMEM((tm, tn), _F32)] if nk > 1 else []
    vmem += _nbytes((tm, tn), _F32) * 2

    def body(*refs):
        a_ref, b_ref = refs[0], refs[1]
        ex_refs = refs[2:2 + n_ex]
        n_in = 2 + n_ex + n_comm
        out_refs = refs[n_in:n_in + n_out]
        i, j, kk = pl.program_id(0), pl.program_id(1), pl.program_id(2)
        comm_refs = (kinds, refs[2 + n_ex:n_in], refs[n_in + n_out:n_in + n_out + n_comm], *refs[len(refs) - 3:])
        if n_comm:
            pl.when((i == 0) & (j == 0) & (kk == 0))(lambda: _comm_start(*comm_refs))

        b_val = _mx(b_ref[...])
        part = dotfn(_mx(a_ref[...]), b_val)

        def finish(acc):
            vals = epilogue(acc, *([b_val] if b_to_epilogue else []), *[r[...] for r in ex_refs])
            for r, v, (kind, _) in zip(out_refs, vals, outs):
                if kind == "part8":
                    @pl.when(i == 0)
                    def _():
                        r[...] = v

                    @pl.when(i > 0)
                    def _():
                        r[...] += v
                else:
                    r[...] = v.astype(r.dtype)

        if nk == 1:
            finish(part)
        else:
            acc_ref = refs[n_in + n_out + n_comm]

            @pl.when(kk == 0)
            def _():
                acc_ref[...] = part

            @pl.when(kk > 0)
            def _():
                acc_ref[...] += part

            @pl.when(kk == nk - 1)
            def _():
                finish(acc_ref[...])

        if n_comm:
            pl.when((i == ni - 1) & (j == nj - 1) & (kk == nk - 1))(lambda: _comm_finish(*comm_refs))

    carried =n_comm or any(kind == "part8" for kind, _ in outs)
    sem = ("arbitrary",) * 3 if carried else ("parallel", "parallel", "arbitrary")
    res = pl.pallas_call(
        body, name=name, grid=(ni, nj, nk),
        in_specs=in_specs + c_in_specs, out_specs=out_specs + c_out_specs, out_shape=out_shape + c_out_shape,
        scratch_shapes=scratch + c_scratch,
        compiler_params=pltpu.CompilerParams(dimension_semantics=sem, vmem_limit_bytes=_vmem_limit(vmem)),
    )(a, b, *[arr for arr, _ in extras], *c_in)
    return res[:n_out], res[n_out:]


def _shift_down(v, halo8, j):
    if j == 0:
        return v
    r = pltpu.roll(v, j, axis=0)
    hr = pltpu.roll(halo8, j, axis=0)
    top = jnp.where(_iota2(hr.shape, 0) < j, hr, r[:8])
    return jnp.concatenate([top, r[8:]], axis=0)


def _shift_up(v, next8, j):
    if j == 0:
        return v
    rows = v.shape[0]
    r = pltpu.roll(v, rows - j, axis=0)
    nr = pltpu.roll(next8, 8 - j, axis=0)
    bot = jnp.where(_iota2(nr.shape, 0) >= 8 - j, nr, r[rows - 8:])
    return jnp.concatenate([r[:rows - 8], bot], axis=0)


def _silu_grad(sig, silu):
    return sig + silu * (1.0 - sig)


def _gmlp_fwd_vals(pu, pv, gv, ws_ref, bsb_ref, want_bwd):
    tril = _iota2((CH, CH), 0) >= _iota2((CH, CH), 1)
    cdf_u = 0.5 * (1.0 + lax.erf(pu * 0.7071067811865476))
    cdf_v = 0.5 * (1.0 + lax.erf(pv * 0.7071067811865476))
    u = pu * cdf_u
    v = pv * cdf_v
    ys, keep = [], [(cdf_u, cdf_v)] if want_bwd else []
    for h in range(GM_H):
        sl = slice(h * 128, (h + 1) * 128)
        vh = v[:, sl]
        r = lax.rsqrt(jnp.mean(vh * vh, axis=-1, keepdims=True) + EPS)
        vn = vh * r * gv[:, sl]
        wm = _mx(jnp.where(tril, ws_ref[h], 0.0))
        mixed = _dot(wm, _mx(vn)) + bsb_ref[h]
        ys.append(u[:, sl] * mixed)
        if want_bwd:
            keep.append((vh, r, vn, wm, mixed))
    return jnp.concatenate(ys, axis=1), u, keep


def _ssd_conv(xbc, halo8, cw_ref, cb):
    cpre = cb + sum(cw_ref[k:k + 1, :] * _shift_down(xbc, halo8, CONV_K - 1 - k) for k in range(CONV_K))
    sig = _sigmoid(cpre)
    return sig, cpre * sig


def _ssd_decay(dtraw, dtb, alog, e_ref, ltri):
    dtin = dtraw + dtb
    dt = jnp.maximum(dtin, 0.0) + jnp.log(1.0 + jnp.exp(-jnp.abs(dtin)))
    a_neg = -jnp.exp(alog)
    cs = _xdot_left(_dot, ltri, dt * a_neg)
    cs_last = cs[CH - 1:CH, :]
    ecs = jnp.exp(cs)
    dec = jnp.exp(cs_last - cs)
    cdec = jnp.exp(cs_last)
    e = e_ref[...]
    dt_x = _dot(dt.astype(_BF16), e)
    ecs_x = _dot(ecs.astype(_BF16), e)
    dec_x = _dot(dec.astype(_BF16), e)
    cdec_x = _xdot(_dot, jnp.broadcast_to(cdec, (8, 128)), e)[0:1, :]
    return dict(dtin=dtin, dt=dt, a_neg=a_neg, cs=cs, ecs=ecs, dec=dec, cdec=cdec,
                dt_x=dt_x, ecs_x=ecs_x, dec_x=dec_x, cdec_x=cdec_x)


def _head_lm(cs, cst_ref, h, tril):
    seg = jnp.broadcast_to(cs[:, h:h + 1], (CH, CH)) - cst_ref[h:h + 1, :]
    return jnp.exp(jnp.where(tril, seg, -jnp.inf))


def _mixer_fwd(proj, gv, ws, bsb, gout, cw8, cb, dtb, alog, d_x, ng, e_mat, ltri_mat, seq_chunks, comm=()):
    t = proj.shape[0]
    n_seq = t // (seq_chunks * CH)
    sb = math.gcd(_SEQ_PER_STEP, n_seq)
    n_groups = n_seq // sb
    proj = proj.reshape(n_seq, seq_chunks * CH, proj.shape[1])
    n_comm = len(comm)
    kinds = [kind for kind, _ in comm]
    c_in, c_in_specs, c_out_specs, c_out_shape, c_scratch = _comm_io(comm)

    def body(*refs):
        comm_refs = (kinds, refs[18:18 + n_comm], refs[23 + n_comm:23 + 2 * n_comm], *refs[25 + 2 * n_comm:])
        grp, c = pl.program_id(0), pl.program_id(1)
        if n_comm:
            pl.when((grp == 0) & (c == 0))(lambda: _comm_start(*comm_refs))
            pl.when((grp == n_groups - 1) & (c == seq_chunks - 1))(lambda: _comm_finish(*comm_refs))
        for s in range(sb):
            per_seq = lambda rs: [r.at[s] for r in rs]
            one_chunk(c == 0, *per_seq(refs[:6]), *refs[6:18], *per_seq(refs[18 + n_comm:23 + n_comm]),
                      *per_seq(refs[23 + 2 * n_comm:25 + 2 * n_comm]))

    def one_chunk(first, pu_ref, pv_ref, z_ref, xbc_ref, dt_ref, halo_ref, gv_ref, ws_ref, bsb_ref, gout_ref, cw_ref,
                  cb_ref, dtb_ref, alog_ref, dx_ref, ng_ref, e_ref, ltri_ref, cat_ref, y_ref, st_ref, act_ref, sig_ref,
                  s_ref, cst_ref):
        tril = _iota2((CH, CH), 0) >= _iota2((CH, CH), 1)
        lane = _iota2((CH, 128), 1)

        y_a, _, _ = _gmlp_fwd_vals(pu_ref[...], pv_ref[...], gv_ref[...], ws_ref, bsb_ref, False)
        cat_ref[:, 0:GM_W] = _rms(y_a, gout_ref[...]).astype(cat_ref.dtype)

        @pl.when(first)
        def _():
            s_ref[...] = jnp.zeros_like(s_ref)

        halo8 = jnp.where(first, 0.0, halo_ref[...])
        sig, act = _ssd_conv(xbc_ref[...], halo8, cw_ref, cb_ref[...])
        sig_ref[...] = sig
        act_ref[...] = act
        q = _ssd_decay(dt_ref[...], dtb_ref[...], alog_ref[...], e_ref, ltri_ref[...])
        xv = act[:, 0:SSD_W]
        xdt = xv * q["dt_x"]
        xdt_m = _mx(xdt)
        cs = q["cs"]
        cst_ref[...] = cs.T
        s_prev = s_ref[...]
        st_ref[...] = s_prev
        ys = []
        for g in range(SSD_G):
            bg = _mx(act[:, SSD_W + g * SSD_N:SSD_W + (g + 1) * SSD_N])
            cg = _mx(act[:, SSD_W + SSD_G * SSD_N + g * SSD_N:SSD_W + SSD_G * SSD_N + (g + 1) * SSD_N])
            cbm = _dot_nt(cg, bg)
            gs = slice(g * 512, (g + 1) * 512)
            for pr in range(4):
                ps = slice(g * 512 + pr * 128, g * 512 + (pr + 1) * 128)
                o = []
                for hh in range(2):
                    h = g * 8 + pr * 2 + hh
                    m_h = _mx(cbm * _head_lm(cs, cst_ref, h, tril))
                    o.append(_dot(m_h, xdt_m[:, ps]))
                ys.append(jnp.where(lane < SSD_P, o[0], o[1]))
            sg = s_prev[:, gs]
            yoff = _dot(cg, _mx(sg)) * q["ecs_x"][:, gs]
            ys[-4:] = [ys[-4 + i] + yoff[:, i * 128:(i + 1) * 128] for i in range(4)]
            st_new = _dot_tn(bg, _mx(q["dec_x"][:, gs] * xdt[:, gs]))
            s_ref[:, gs] = sg * q["cdec_x"][:, gs] + st_new
        y = jnp.concatenate(ys, axis=1) + dx_ref[...] * xv
        y_ref[...] = y
        zv = z_ref[...]
        yg = y * (zv * _sigmoid(zv))
        for g in range(SSD_G):
            gs = slice(g * 512, (g + 1) * 512)
            cat_ref[:, GM_W + g * 512:GM_W + (g + 1) * 512] = _rms(yg[:, gs], ng_ref[:, gs]).astype(cat_ref.dtype)

    blk = lambda w, j: pl.BlockSpec((sb, CH, w), lambda g, c: (g, c, j))
    full = lambda arr: pl.BlockSpec(arr.shape, lambda g, c: (0,) * arr.ndim)
    consts = [gv, ws, bsb, gout, cw8, cb, dtb, alog, d_x, ng, e_mat, ltri_mat]
    seq = seq_chunks * CH
    res = pl.pallas_call(
        body, name="mixer_fwd", grid=(n_groups, seq_chunks),
        in_specs=[blk(GM_W, 0), blk(GM_W, 1), blk(SSD_W, 2), blk(CONV_CH, 2), blk(128, DT_BLK),
                  pl.BlockSpec((sb, 8, CONV_CH), lambda g, c: (g, jnp.maximum(c * (CH // 8) - 1, 0), 2))]
        + [full(a) for a in consts] + c_in_specs,
        out_specs=[blk(2 * D, 0), blk(SSD_W, 0), blk(SSD_W, 0), blk(CONV_CH, 0), blk(CONV_CH, 0)] + c_out_specs,
        out_shape=[jax.ShapeDtypeStruct((n_seq, seq, 2 * D), _MXU), jax.ShapeDtypeStruct((n_seq, seq, SSD_W), _F32),
                   jax.ShapeDtypeStruct((n_seq, seq, SSD_W), _F32), jax.ShapeDtypeStruct((n_seq, seq, CONV_CH), _F32),
                   jax.ShapeDtypeStruct((n_seq, seq, CONV_CH), _F32)] + c_out_shape,
        scratch_shapes=[pltpu.VMEM((sb, SSD_N, SSD_W), _F32), pltpu.VMEM((sb, 128, CH), _F32)] + c_scratch,
        compiler_params=pltpu.CompilerParams(dimension_semantics=("arbitrary", "arbitrary"),
                                             vmem_limit_bytes=48 << 20),
    )(proj, proj, proj, proj, proj, proj, *consts, *c_in)
    return [r.reshape(t, r.shape[-1]) for r in res[:5]], res[5:]


def _mixer_bwd(proj, act, sig, dcat, yss, states, gv, ws, bsb, gout, cw8, cb, dtb, alog, d_x, ng, e_mat, et_mat,
               ltri_mat, seq_chunks, comm=()):
    t = proj.shape[0]
    n_seq = t // (seq_chunks * CH)
    sb = math.gcd(_SEQ_PER_STEP, n_seq)
    n_groups = n_seq // sb
    seq = seq_chunks * CH
    proj, act, sig, dcat, yss, states = [a.reshape(n_seq, seq, a.shape[1])
                                         for a in (proj, act, sig, dcat, yss, states)]
    n_comm = len(comm)
    kinds = [kind for kind, _ in comm]
    c_in, c_in_specs, c_out_specs, c_out_shape, c_scratch = _comm_io(comm)

    def body(*refs):
        o0, s0 = 23 + n_comm, 34 + 2 * n_comm
        acc_refs, shared = refs[o0 + 1:o0 + 11], refs[s0 + 4:s0 + 6]
        comm_refs = (kinds, refs[23:23 + n_comm], refs[o0 + 11:o0 + 11 + n_comm], *refs[s0 + 6:])
        grp, i = pl.program_id(0), pl.program_id(1)
        if n_comm:
            pl.when((grp == 0) & (i == 0))(lambda: _comm_start(*comm_refs))

        @pl.when((grp == 0) & (i == 0))
        def _():
            for r in (*acc_refs, *shared, refs[s0 + 2]):
                r[...] = jnp.zeros_like(r)

        @pl.when(i == 0)
        def _():
            refs[s0][...] = jnp.zeros_like(refs[s0])
            refs[s0 + 1][...] = jnp.zeros_like(refs[s0 + 1])

        for s in range(sb):
            per_seq = lambda rs: [r.at[s] for r in rs]
            last_of_all = ((grp == n_groups - 1) & (i == seq_chunks - 1)) if s == sb - 1 else None
            one_chunk(last_of_all, *per_seq(refs[:10]), *refs[10:23], refs[o0].at[s], *acc_refs,
                      *per_seq(refs[s0:s0 + 4]), *shared)
        if n_comm:
            pl.when((grp == n_groups - 1) & (i == seq_chunks - 1))(lambda: _comm_finish(*comm_refs))

    def one_chunk(finalize, pu_ref, pv_ref, z_ref, xbc_ref, dt_ref, act_ref, sig_ref, dcat_ref, y_ref, st_ref,
                  gv_ref, ws_ref, bsb_ref, gout_ref, cw_ref, cb_ref, dtb_ref, alog_ref, dx_ref, ng_ref,
                  e_ref, et_ref, ltri_ref,
                  dproj_ref, dws_ref, dbs_ref, dgv_ref, dgout_ref, dng_ref, dcw_ref, dcb_ref, ddtb_ref, dalog_ref,
                  dd_ref, ds_ref, dnext_ref, dcst_ref, cst_ref, dbacc_ref, ddacc_ref):
        tril = _iota2((CH, CH), 0) >= _iota2((CH, CH), 1)
        lane = _iota2((CH, 128), 1)
        row = _iota2((CH, 128), 0)
        dcat_v = dcat_ref[...].astype(_F32)

        pu, pv = pu_ref[...], pv_ref[...]
        gv_v = gv_ref[...]
        y_a, u, keep = _gmlp_fwd_vals(pu, pv, gv_v, ws_ref, bsb_ref, True)
        dy, dgout8 = _rms_bwd(y_a, gout_ref[...], dcat_v[:, 0:GM_W])
        dgout_ref[...] += dgout8
        dus, dvs, dgvs = [], [], []
        for h in range(GM_H):
            sl = slice(h * 128, (h + 1) * 128)
            vh, r, vn, wm, mixed = keep[h + 1]
            dyh = dy[:, sl]
            dus.append(dyh * mixed)
            dmix = dyh * u[:, sl]
            dmix_m = _mx(dmix)
            dws_ref[h] += jnp.where(tril, _dot_nt(dmix_m, _mx(vn)), 0.0)
            dbacc_ref[h] += dmix
            dvn = _dot_tn(wm, dmix_m)
            gy = dvn * gv_v[:, sl]
            nh = vh * r
            dvs.append((gy - nh * jnp.mean(gy * nh, axis=-1, keepdims=True)) * r)
            dgvs.append(_sum8(dvn * nh))
        dgv_ref[...] += jnp.concatenate(dgvs, axis=1)
        cdf_u, cdf_v = keep[0]
        gelu_grad = lambda pre, cdf: cdf + pre * jnp.exp2(pre * pre * (-0.5 * _LOG2E) + _LOG2_INV_SQRT_2PI)
        dproj_ref[:, 0:GM_W] = (jnp.concatenate(dus, axis=1) * gelu_grad(pu, cdf_u)).astype(dproj_ref.dtype)
        dproj_ref[:, GM_W:2 * GM_W] = (jnp.concatenate(dvs, axis=1) * gelu_grad(pv, cdf_v)).astype(dproj_ref.dtype)

        q = _ssd_decay(dt_ref[...], dtb_ref[...], alog_ref[...], e_ref, ltri_ref[...])
        act = act_ref[...]
        xv = act[:, 0:SSD_W]
        dt_x, ecs_x, dec_x, cdec_x = q["dt_x"], q["ecs_x"], q["dec_x"], q["cdec_x"]
        xdt = xv * dt_x
        xdt_m = _mx(xdt)
        cs = q["cs"]
        cst_ref[...] = cs.T
        s_prev = st_ref[...]
        ds = ds_ref[...]
        yv = y_ref[...]
        zv = z_ref[...]
        sig_z = _sigmoid(zv)
        sz = zv * sig_z
        yg = yv * sz
        dygs, dng8 = [], []
        for g in range(SSD_G):
            gs = slice(g * 512, (g + 1) * 512)
            a_, b_ = _rms_bwd(yg[:, gs], ng_ref[:, gs], dcat_v[:, GM_W + g * 512:GM_W + (g + 1) * 512])
            dygs.append(a_)
            dng8.append(b_)
        dyg = jnp.concatenate(dygs, axis=1)
        dng_ref[...] += jnp.concatenate(dng8, axis=1)
        dyv = dyg * sz
        dproj_ref[:, 2 * GM_W:2 * GM_W + SSD_W] = (dyg * yv * _silu_grad(sig_z, sz)).astype(dproj_ref.dtype)
        ddacc_ref[...] += _sum8(dyv * xv)
        dyv_m = _mx(dyv)

        dxdt_parts, db_parts, dc_parts = [], [], []
        dcs = jnp.zeros((CH, 128), _F32)
        dcs_x_parts, ddec_x_parts, dcl_x_parts = [], [], []
        for g in range(SSD_G):
            gs = slice(g * 512, (g + 1) * 512)
            bg = _mx(act[:, SSD_W + g * SSD_N:SSD_W + (g + 1) * SSD_N])
            cg = _mx(act[:, SSD_W + SSD_G * SSD_N + g * SSD_N:SSD_W + SSD_G * SSD_N + (g + 1) * SSD_N])
            cbm = _dot_nt(cg, bg)
            sg = s_prev[:, gs]
            sg_m = _mx(sg)
            dsg = ds[:, gs]
            dsg_m = _mx(dsg)
            zoff = _dot(cg, sg_m)
            dz_off = dyv[:, gs] * ecs_x[:, gs]
            dz_off_m = _mx(dz_off)
            dcs_x_parts.append(dyv[:, gs] * zoff * ecs_x[:, gs])
            dcg = _dot_nt(dz_off_m, sg_m)
            dsprev = _dot_tn(cg, dz_off_m)
            w_st = dec_x[:, gs] * xdt[:, gs]
            dw_st = _dot(bg, dsg_m)
            dbg = _dot_nt(_mx(w_st), dsg_m)
            dxdt_g = dec_x[:, gs] * dw_st
            ddec_x_parts.append(dw_st * xdt[:, gs])
            dsprev = dsprev + cdec_x[:, gs] * dsg
            dcl_x_parts.append(jnp.sum(dsg * sg, axis=0, keepdims=True) * cdec_x[:, gs])
            ds_ref[:, gs] = dsprev
            dcb = jnp.zeros((CH, CH), _F32)
            dxdt_pairs = []
            for pr in range(4):
                ps = slice(g * 512 + pr * 128, g * 512 + (pr + 1) * 128)
                acc_pair = None
                for hh in range(2):
                    h = g * 8 + pr * 2 + hh
                    in_head = (lane < SSD_P) if hh == 0 else (lane >= SSD_P)
                    lm = _head_lm(cs, cst_ref, h, tril)
                    m_h = cbm * lm
                    m_hm = _mx(m_h)
                    dyh_m = _mx(jnp.where(in_head, dyv[:, ps], 0.0))
                    dm = _dot_nt(dyh_m, xdt_m[:, ps])
                    dcb = dcb + dm * lm
                    qm = dm * m_h
                    dcs = dcs + jnp.where(lane == h, jnp.sum(qm, axis=1, keepdims=True), 0.0)
                    dcst_ref[h:h + 1, :] = jnp.sum(qm, axis=0, keepdims=True)
                    contrib = jnp.where(in_head, _dot_tn(m_hm, dyv_m[:, ps]), 0.0)
                    acc_pair = contrib if acc_pair is None else acc_pair + contrib
                dxdt_pairs.append(acc_pair)
            dxdt_parts.append(dxdt_g + jnp.concatenate(dxdt_pairs, axis=1))
            dcb_m = _mx(dcb)
            dc_parts.append(dcg + _dot(dcb_m, bg))
            db_parts.append(dbg + _dot_tn(dcb_m, cg))
        dxdt = jnp.concatenate(dxdt_parts, axis=1)
        dxv = dx_ref[...] * dyv + dxdt * dt_x
        et = et_ref[...]
        head_sum = lambda v: _dot(v.astype(_BF16), et)
        ddt = head_sum(dxdt * xv)
        dcs = dcs - dcst_ref[...].T + head_sum(jnp.concatenate(dcs_x_parts, axis=1))
        ddec = head_sum(jnp.concatenate(ddec_x_parts, axis=1)) * q["dec"]
        dcs = dcs - ddec
        dcl = jnp.sum(ddec, axis=0, keepdims=True) + _xdot(
            _dot, jnp.broadcast_to(jnp.concatenate(dcl_x_parts, axis=1), (8, SSD_W)), et)[0:1, :]
        dcs = jnp.where(row == CH - 1, dcs + dcl, dcs)
        da = _xdot_left(_dot_tn, ltri_ref[...], dcs)
        ddt = ddt + da * q["a_neg"]
        dalog_ref[...] += _sum8(da * q["dt"] * q["a_neg"])
        ddtraw = jnp.where(lane < SSD_H, ddt * _sigmoid(q["dtin"]), 0.0)
        ddtb_ref[...] += _sum8(ddtraw)
        dproj_ref[:, D_IN_PAD - 128:D_IN_PAD] = ddtraw.astype(dproj_ref.dtype)
        dcpre = jnp.concatenate([dxv] + db_parts + dc_parts, axis=1) * _silu_grad(sig_ref[...], act)
        dcb_ref[...] += _sum8(dcpre)
        next8 = dnext_ref[...]
        ups = [_shift_up(dcpre, next8, j) for j in range(CONV_K)]
        xbc = xbc_ref[...]
        for k in range(CONV_K):
            dcw_ref[k:k + 1, :] += jnp.sum(xbc * ups[CONV_K - 1 - k], axis=0, keepdims=True)
        dxbc = sum(cw_ref[k:k + 1, :] * ups[CONV_K - 1 - k] for k in range(CONV_K))
        dproj_ref[:, 2 * GM_W + SSD_W:2 * GM_W + SSD_W + CONV_CH] = dxbc.astype(dproj_ref.dtype)
        dnext_ref[...] = dcpre[0:8, :]

        if finalize is not None:
            @pl.when(finalize)
            def _():
                for h in range(GM_H):
                    dbs_ref[h:h + 1, :] = _xdot_left(_dot_nt, jnp.ones((8, 128), _BF16), dbacc_ref[h])[0:1, :]
                dd_ref[...] = _xdot(_dot, ddacc_ref[...], et)

    rblk = lambda w, j: pl.BlockSpec((sb, CH, w), lambda g, i: (g, seq_chunks - 1 - i, j))
    full = lambda arr: pl.BlockSpec(arr.shape, lambda g, i: (0,) * arr.ndim)
    acc = lambda shape: pl.BlockSpec(shape, lambda g, i: (0,) * len(shape))
    consts = [gv, ws, bsb, gout, cw8, cb, dtb, alog, d_x, ng, e_mat, et_mat, ltri_mat]
    acc_shapes = [(GM_H, CH, CH), (8, 128), (8, GM_W), (8, GM_W), (8, SSD_W), (8, CONV_CH), (8, CONV_CH), (8, 128),
                  (8, 128), (8, 128)]
    res = pl.pallas_call(
        body, name="mixer_bwd", grid=(n_groups, seq_chunks),
        in_specs=[rblk(GM_W, 0), rblk(GM_W, 1), rblk(SSD_W, 2), rblk(CONV_CH, 2), rblk(128, DT_BLK),
                  rblk(CONV_CH, 0), rblk(CONV_CH, 0),
                  rblk(2 * D, 0), rblk(SSD_W, 0), rblk(SSD_W, 0)] + [full(a) for a in consts] + c_in_specs,
        out_specs=[rblk(D_IN_PAD, 0)] + [acc(s) for s in acc_shapes] + c_out_specs,
        out_shape=[jax.ShapeDtypeStruct((n_seq, seq, D_IN_PAD), _MXU)]
        + [jax.ShapeDtypeStruct(s, _F32) for s in acc_shapes] + c_out_shape,
        scratch_shapes=[pltpu.VMEM((sb, SSD_N, SSD_W), _F32), pltpu.VMEM((sb, 8, CONV_CH), _F32),
                        pltpu.VMEM((sb, 128, CH), _F32), pltpu.VMEM((sb, 128, CH), _F32),
                        pltpu.VMEM((GM_H, CH, 128), _F32), pltpu.VMEM((8, SSD_W), _F32)] + c_scratch,
        compiler_params=pltpu.CompilerParams(dimension_semantics=("arbitrary", "arbitrary"),
                                             vmem_limit_bytes=48 << 20),
    )(proj, proj, proj, proj, proj, act, sig, dcat, yss, states, *consts, *c_in)
    return [res[0].reshape(t, D_IN_PAD)] + list(res[1:11]), res[11:]


def _peers():
    x, y, c = lax.axis_index("x"), lax.axis_index("y"), lax.axis_index("c")
    out = []
    for k in range(1, N_DEV):
        fx, fy, fc = (k >> 2) & 1, (k >> 1) & 1, k & 1
        px, py, pc = (x + fx) % 2, (y + fy) % 2, (c + fc) % 2
        out.append((k - 1, (px, py, pc), 4 * px + 2 * py + pc))
    return out, 4 * x + 2 * y + c


def _comm_io(comm):
    any_spec = pl.BlockSpec(memory_space=pl.ANY)
    n = len(comm)
    out_shape = []
    for (kind, axis), src in comm:
        shp = list(src.shape)
        if kind in ("gather", "gather2"):
            shp[axis] *= N_DEV
        else:
            shp[axis] //= N_DEV
            shp = [N_DEV] + shp
        out_shape.append(jax.ShapeDtypeStruct(tuple(shp), src.dtype))
    scratch = [pltpu.SemaphoreType.DMA((n * (N_DEV - 1),)), pltpu.SemaphoreType.DMA((n * (N_DEV - 1),)),
               pltpu.SemaphoreType.DMA((n,))] if n else []
    return [src for _, src in comm], [any_spec] * n, [any_spec] * n, out_shape, scratch


def _window(ref, axis, idx, size):
    start = pl.multiple_of(idx * size, size)
    return ref.at[tuple(pl.ds(start, size) if a == axis else slice(None) for a in range(len(ref.shape)))]


def _comm_plans(kinds, src_refs, dst_refs, send_sems, recv_sems, local_sems):
    x, y, c = lax.axis_index("x"), lax.axis_index("y"), lax.axis_index("c")
    peers, me = _peers()
    plans = []
    for s, ((kind, axis), src, dst) in enumerate(zip(kinds, src_refs, dst_refs)):
        sems = lambda k: dict(send_sem=send_sems.at[s * (N_DEV - 1) + k], recv_sem=recv_sems.at[s * (N_DEV - 1) + k])
        remote = lambda src_ref, dst_ref, k, pid: pltpu.make_async_remote_copy(
            src_ref=src_ref, dst_ref=dst_ref, device_id=pid, device_id_type=_MESH, **sems(k))
        if kind == "gather2":
            size = src.shape[axis]
            win = lambda idx: _window(dst, axis, idx, size)
            sib, sib_idx = (x, y, 1 - c), 4 * x + 2 * y + (1 - c)
            local = pltpu.make_async_copy(src, win(me), local_sems.at[s])
            to_sib = remote(src, win(me), 0, sib)
            starts, forwards = [local, to_sib], []
            waits = [(local, "local"), (to_sib, "send"), (remote(src, win(sib_idx), 0, sib), "recv")]
            for j, (fx, fy) in enumerate(((1, 0), (0, 1), (1, 1))):
                px, py = (x + fx) % 2, (y + fy) % 2
                same, other = 4 * px + 2 * py + c, 4 * px + 2 * py + (1 - c)
                out = remote(src, win(me), 1 + j, (px, py, c))
                starts.append(out)
                passed = remote(win(same), win(same), 4 + j, sib)
                forwards.append((remote(src, win(same), 1 + j, (px, py, c)), passed))
                waits += [(out, "send"), (passed, "send"), (remote(win(other), win(other), 4 + j, sib), "recv")]
            plans.append((starts, forwards, waits))
            continue
        if kind == "gather":
            size = src.shape[axis]
            src_for = lambda pidx: src
            dst_mine = _window(dst, axis, me, size)
        else:
            size = src.shape[axis] // N_DEV
            src_for = lambda pidx: _window(src, axis, pidx, size)
            dst_mine = dst.at[me]
        local = pltpu.make_async_copy(src_for(me), dst_mine, local_sems.at[s])
        remotes = [remote(src_for(pidx), dst_mine, k, pid) for k, pid, pidx in peers]
        plans.append(([local] + remotes, [], [(local, "local")] + [(cp, "both") for cp in remotes]))
    return plans


def _comm_start(*refs):
    for starts, _, _ in _comm_plans(*refs):
        for cp in starts:
            cp.start()


def _comm_finish(*refs):
    for _, forwards, waits in _comm_plans(*refs):
        for arrival, cp in forwards:
            arrival.wait_recv()
            cp.start()
        for cp, what in waits:
            if what == "send":
                cp.wait_send()
            elif what == "recv":
                cp.wait_recv()
            else:
                cp.wait()


def _adam_vals(w, g, m, v):
    m = B1 * m + (1.0 - B1) * g
    v = B2 * v + (1.0 - B2) * (g * g)
    m_hat = m / (1.0 - B1 ** STEP)
    v_hat = v / (1.0 - B2 ** STEP)
    delta = -LR * (m_hat / (jnp.sqrt(v_hat) + ADAM_EPS) + WD * w)
    return delta, m, v


def _sum_adam(name, recv, w, m, v, tile=256):
    _, r, wd = recv.shape
    if r % min(tile, r) == 0:
        tr, tc = min(tile, r), wd
    else:
        tr, tc = r, tile
        assert wd % tc == 0, (name, r, wd)

    def body(recv_ref, w_ref, m_ref, v_ref, g_out, d_out, m_out, v_out):
        g = recv_ref[0].astype(_F32)
        for s in range(1, N_DEV):
            g = g + recv_ref[s].astype(_F32)
        d_, m_, v_ = _adam_vals(w_ref[...], g, m_ref[...], v_ref[...])
        g_out[...] = g
        d_out[...] = d_
        m_out[...] = m_
        v_out[...] = v_

    spec = pl.BlockSpec((tr, tc), lambda i, j: (i, j))
    return pl.pallas_call(
        body, name=name, grid=(r // tr, wd // tc),
        in_specs=[pl.BlockSpec((N_DEV, tr, tc), lambda i, j: (0, i, j)), spec, spec, spec],
        out_specs=[spec] * 4, out_shape=[jax.ShapeDtypeStruct((r, wd), _F32)] * 4,
        compiler_params=pltpu.CompilerParams(dimension_semantics=("parallel", "parallel"),
                                             vmem_limit_bytes=48 << 20),
    )(recv, w, m, v)


def _small_local(parts, segments, n_rows):
    def body(parts_ref, loc_ref):
        loc_ref[...] = jnp.zeros_like(loc_ref)
        for out_row, n_out, in_row, n_in, kind in segments:
            if kind == "copy":
                loc_ref[out_row:out_row + n_out, :] = parts_ref[in_row:in_row + n_in, :]
            else:
                s = jnp.sum(parts_ref[in_row:in_row + n_in, :], axis=0, keepdims=True)
                if kind == "loss":
                    s = jnp.broadcast_to(jnp.sum(s, axis=1, keepdims=True) * (0.5 / D), (1, D))
                loc_ref[out_row:out_row + 1, :] = s

    vm = pl.BlockSpec(memory_space=pltpu.VMEM)
    return pl.pallas_call(body, name="small_local", in_specs=[vm], out_specs=vm,
                          out_shape=jax.ShapeDtypeStruct((n_rows, D), _F32))(parts)


def _small_final(blocks, late8, late_row, w, m, v):
    n_rows = w.shape[0]

    def body(blocks_ref, late_ref, w_ref, m_ref, v_ref, g_out, d_out, m_out, v_out, loc_ref, recv_ref, send_sems,
             recv_sems):
        peers, me = _peers()
        loc_ref[...] = jnp.broadcast_to(jnp.sum(late_ref[...], axis=0, keepdims=True), (8, D))
        recv_ref[me] = loc_ref[...]
        copies = [pltpu.make_async_remote_copy(src_ref=loc_ref, dst_ref=recv_ref.at[me], send_sem=send_sems.at[k],
                                               recv_sem=recv_sems.at[k], device_id=pid, device_id_type=_MESH)
                  for k, pid, _ in peers]
        for cp in copies:
            cp.start()
        g = blocks_ref[0:n_rows, :]
        for s in range(1, N_DEV):
            g = g + blocks_ref[s * n_rows:(s + 1) * n_rows, :]
        for cp in copies:
            cp.wait()
        late = recv_ref[0]
        for s in range(1, N_DEV):
            late = late + recv_ref[s]
        g = jnp.where(_iota2((n_rows, D), 0) == late_row, jnp.broadcast_to(late[0:1, :], (n_rows, D)), g)
        d_, m_, v_ = _adam_vals(w_ref[...], g, m_ref[...], v_ref[...])
        g_out[...] = g
        d_out[...] = d_
        m_out[...] = m_
        v_out[...] = v_

    vm = pl.BlockSpec(memory_space=pltpu.VMEM)
    return pl.pallas_call(
        body, name="small_final", in_specs=[vm] * 5, out_specs=[vm] * 4,
        out_shape=[jax.ShapeDtypeStruct((n_rows, D), _F32)] * 4,
        scratch_shapes=[pltpu.VMEM((8, D), _F32), pltpu.VMEM((N_DEV, 8, D), _F32),
                        pltpu.SemaphoreType.DMA((N_DEV - 1,)), pltpu.SemaphoreType.DMA((N_DEV - 1,))],
        compiler_params=pltpu.CompilerParams(vmem_limit_bytes=48 << 20),
    )(blocks, late8, w, m, v)


_BIG_NAMES =("w_in", "w_out", "w_ff1", "w_ff2", "w_ple_gate", "w_ple_proj")

_G_VECS = ("norm_mix_g", "gm_v_norm_g", "gm_out_norm_g", "ssd_norm_g", "norm_mlp_g", "ple_norm_g", "final_norm_g")
_LATE = _G_VECS[0]


def _const_mats():
    h = np.arange(128)[:, None]
    ch = np.arange(SSD_W)[None, :]
    e = (ch // SSD_P == h).astype(np.float32)
    ltri = (np.arange(CH)[:, None] >= np.arange(CH)[None, :]).astype(np.float32)
    return jnp.asarray(e, _BF16), jnp.asarray(e.T, _BF16), jnp.asarray(ltri, _BF16)


def _pad_lanes(v, n=128):
    return jnp.pad(v, ((0, 0), (0, n - v.shape[1])))


def _local_step(x, p, tgt, shard, conv_w_shard, small, seq_len):
    seq_chunks = seq_len // CH
    e_mat, et_mat, ltri_mat = _const_mats()
    g_mix, g_mlp, g_ple = small["norm_mix_g"], small["norm_mlp_g"], small["ple_norm_g"]
    g_fin = small["final_norm_g"].reshape(1, D)
    gv, gout, ng = small["gm_v_norm_g"], small["gm_out_norm_g"], small["ssd_norm_g"]
    ws = small["gm_ws"][0]
    bsb = jnp.broadcast_to(small["gm_bs"][0][:, :, None], (GM_H, CH, 128))
    cb = small["ssd_conv_b"]
    dtb, alog = _pad_lanes(small["ssd_dt_bias"]), _pad_lanes(small["ssd_a_log"])
    d_x = jnp.repeat(small["ssd_d"], SSD_P, axis=1)

    first = lambda acc: (acc,)
    rows, cols = ("gather2", 0), ("gather2", 1)
    n1, (g_win, g_cw) = _norm_cast("norm_mix", x, g_mix,
                                   comm=[(rows, shard["w_in"][None]), (("gather", 0), conv_w_shard[None])])
    w_in_t = jnp.pad(g_win.reshape(D_IN, D), ((0, D_IN_PAD - D_IN), (0, 0)))
    cw8 = jnp.pad(g_cw.transpose(1, 0, 2).reshape(CONV_K, CONV_CH), ((0, 8 - CONV_K), (0, 0)))
    mix_consts = (gv, ws, bsb, gout, cw8, cb, dtb, alog, d_x, ng)
    (proj,), (w_out, w1) = _matmul("proj_in", n1, w_in_t, "nt", 512, D_IN_PAD, D, first, [("tile", _F32)],
                                   comm=[(rows, shard["w_out"]), (cols, shard["w_ff1"])])
    (cat, yss, states, conv_act, conv_sig), (w2, wg, wp) = _mixer_fwd(
        proj, *mix_consts, e_mat, ltri_mat, seq_chunks,
        comm=[(rows, shard["w_ff2"]), (rows, shard["w_ple_gate"]), (cols, shard["w_ple_proj"])])

    def epi_res_norm(acc, res, g):
        hv = acc + res
        return hv, _rms(hv, g)

    (h1, n2), _ = _matmul("proj_out", cat, w_out, "nn", 1024, D, 2 * D, epi_res_norm,
                          [("tile", _F32), ("tile", _MXU)], extras=[(x, "tile"), (g_mlp, "row")])

    def epi_relu2(acc):
        hid = jnp.maximum(acc, 0.0)
        return hid, hid * hid

    (hid, hid2), _ = _matmul("ff1", n2, w1, "nn", 1024, 1024, D, epi_relu2, [("tile", _MXU), ("tile", _MXU)])
    (h2, n3), _ = _matmul("ff2", hid2, w2, "nn", 512, D, D_FF, epi_res_norm, [("tile", _F32), ("tile", _MXU)],
                          extras=[(h1, "tile"), (g_ple, "row")])

    def epi_norm_bwd(acc, up, hv, g):
        dx, dg8 = _rms_bwd(hv, g, acc)
        dh = up + dx
        return dh, dh, dg8

    def epi_head(acc, wg_v, p_rows, wp_v, h2v, tg, gf, gp_):
        ppv = _dot(_mx(p_rows), wp_v)
        gate = _sigmoid(acc)
        gp = gate * ppv
        h3 = h2v + gp
        r = lax.rsqrt(jnp.mean(h3 * h3, axis=-1, keepdims=True) + EPS)
        nh = h3 * r
        err = nh * gf - tg
        gy = err * (gf * (1.0 / D))
        dh3 = (gy - nh * jnp.mean(gy * nh, axis=-1, keepdims=True)) * r
        dpp = dh3 * gate
        da3 = dpp * (ppv - gp)
        dh2, dh2_again, dgple8 = epi_norm_bwd(_dot_nt(_mx(da3), wg_v), dh3, h2v, gp_)
        return da3, dpp, dh2, dh2_again, _sum8(err * err), _sum8(err * nh) * (1.0 / D), dgple8

    (da3, dpp, dh2, dh2b, lossp, dgfin, dgple), _ = _matmul(
        "ple_gate_loss_bwd", n3, wg, "nn", 512, D, D, epi_head,
        [("tile", _MXU), ("tile", _MXU), ("tile", _F32), ("tile", _MXU), ("part8", _F32), ("part8", _F32),
         ("part8", _F32)],
        extras=[(p, "rows"), (wp, "full"), (h2, "tile"), (tgt, "tile"), (g_fin, "row"), (g_ple, "row")],
        b_to_epilogue=True)

    s_rows, s_cols = ("scatter", 0), ("scatter", 1)
    (dwp,), _ = _matmul("d_w_ple_proj", p, dpp, "tn", D_PLE, D, 2048, first, [("tile", _BF16)])
    (dwg,), _ = _matmul("d_w_ple_gate", n3, da3, "tn", D, D, 2048, first, [("tile", _BF16)])
    (dw2,), (r_wp, r_wg) = _matmul("d_w_ff2", hid2, dh2b, "tn", 1024, D, 4096, first, [("tile", _BF16)],
                                   comm=[(s_cols, dwp), (s_rows, dwg)])
    (da1,), _ = _matmul("d_ff_hidden", dh2b, w2, "nt", 512, 2048, D,
                        lambda acc, hv: (acc * 2.0 * hv.astype(_F32),), [("tile", _MXU)], extras=[(hid, "tile")])
    (dw1,), _ = _matmul("d_w_ff1", n2, da1, "tn", 1024, 1024, 4096, first, [("tile", _BF16)])
    (dh1, dh1b, dgmlp), _ = _matmul(
        "d_h1", da1, w1, "nt", 256, D, D_FF, epi_norm_bwd, [("tile", _F32), ("tile", _MXU), ("part8", _F32)],
        extras=[(dh2, "tile"), (h1, "tile"), (g_mlp, "row")])
    (dwout,), _ = _matmul("d_w_out", cat, dh1b, "tn", 1024, D, 2048, first, [("tile", _BF16)])
    (dcat,), _ = _matmul("d_cat", dh1b, w_out, "nt", 1024, 1024, D, first, [("tile", _MXU)])
    (dproj, dws, dbs, dgv, dgout, dng, dcw, dcb, ddtb, dalog, dd), (r_w2, r_w1, r_wout) = _mixer_bwd(
        proj, conv_act, conv_sig, dcat, yss, states, *mix_consts, e_mat, et_mat, ltri_mat, seq_chunks,
        comm=[(s_rows, dw2), (s_cols, dw1), (s_rows, dwout)])
    pieces = dict(gm_v_norm_g=dgv, gm_out_norm_g=dgout, ssd_norm_g=dng, norm_mlp_g=dgmlp, ple_norm_g=dgple,
                  final_norm_g=dgfin, gm_ws=dws, gm_bs=dbs, ssd_conv_w=dcw, ssd_conv_b=dcb, ssd_dt_bias=ddtb,
                  ssd_a_log=dalog, ssd_d=dd, loss=lossp)
    parts, segments, n_rows, where = _small_layout(pieces)
    small_block = _small_local(parts, segments, n_rows)
    (dwin_t,), (small_blocks,) = _matmul("d_w_in", n1, dproj, "tn", 512, D_IN_PAD, 1024, lambda acc: (acc.T,),
                                         [("tile_t", _BF16)], comm=[(("gather", 0), small_block)])
    dwin_blocks = dwin_t[:D_IN].reshape(N_DEV, SHARD_IN, D)
    (gx, dgmix), (r_win,) = _matmul(
        "d_x", dproj, w_in_t, "nn", 256, D, D_IN_PAD, lambda *a: epi_norm_bwd(*a)[1:],
        [("tile", _F32), ("part8", _F32)], extras=[(dh1, "tile"), (x, "tile"), (g_mix, "row")],
        comm=[(s_rows, dwin_blocks)])
    r_win = r_win.reshape(N_DEV, SHARD_IN, D)

    big = dict(w_in=r_win, w_out=r_wout, w_ff1=r_w1, w_ff2=r_w2, w_ple_gate=r_wg, w_ple_proj=r_wp)
    return gx, big, small_blocks, dgmix, n_rows, where


def _small_layout(pieces):
    rows, segments = [], []
    in_row, out_row = 0, 0

    def add(arr, kind, n_out, new_group=True):
        nonlocal in_row, out_row
        if new_group:
            out_row = -(-out_row // 8) * 8
        rows.append(arr)
        segments.append((out_row, n_out, in_row, arr.shape[0], kind))
        start = out_row
        in_row += arr.shape[0]
        out_row += n_out
        return start

    where = {_LATE: 0}
    out_row = 1
    for name in _G_VECS[1:]:
        where[name] = add(pieces[name], "sum", 1)
    where["gm_ws"] = add(pieces["gm_ws"].reshape(GM_H * CH * CH // D, D), "copy", GM_H * CH * CH // D)
    where["gm_bs"] = add(jnp.pad(pieces["gm_bs"].reshape(1, D), ((0, 7), (0, 0))), "sum", 1)
    cb = jnp.pad(pieces["ssd_conv_b"], ((0, 0), (0, 2 * D - CONV_CH)))
    where["ssd_conv_b"] = add(cb[:, :D], "sum", 1)
    add(cb[:, D:], "sum", 1, new_group=False)
    cw = jnp.pad(pieces["ssd_conv_w"][:CONV_K], ((0, 0), (0, 2 * D - CONV_CH)))
    where["ssd_conv_w"] = add(cw.reshape(2 * CONV_K, D), "copy", 2 * CONV_K)
    misc = jnp.concatenate([pieces["ssd_dt_bias"], pieces["ssd_a_log"], pieces["ssd_d"],
                            jnp.zeros((8, D - 3 * 128), _F32)], axis=1)
    where["misc"] = add(misc, "sum", 1)
    where["loss"] = add(pieces["loss"], "loss", 1)
    n_rows = -(-out_row // 8) * 8
    return jnp.concatenate(rows, axis=0), tuple(segments), n_rows, where


def _pack_small_params(vals, where, n_rows, my_block):
    rows, at = [], {}

    def add(name, arr):
        at[name] = sum(r.shape[0] for r in rows)
        rows.append(jnp.pad(arr, ((0, -arr.shape[0] % 8), (0, 0))))

    for name in _G_VECS:
        add(name, vals[name].reshape(1, D))
    add("gm_ws", vals["gm_ws"].reshape(GM_H * CH * CH // D, D))
    add("gm_bs", vals["gm_bs"].reshape(1, D))
    cb = jnp.pad(vals["ssd_conv_b"].reshape(1, CONV_CH), ((0, 0), (0, 2 * D - CONV_CH)))
    add("ssd_conv_b", cb.reshape(2, D))
    cw = lax.dynamic_update_slice(jnp.zeros((CONV_K, 2 * D), _F32), vals["ssd_conv_w"].reshape(CONV_K, -1),
                                  (0, my_block * (CONV_CH // N_DEV)))
    add("ssd_conv_w", cw.reshape(2 * CONV_K, D))
    misc = jnp.concatenate([_pad_lanes(vals["ssd_dt_bias"].reshape(1, SSD_H)),
                            _pad_lanes(vals["ssd_a_log"].reshape(1, SSD_H)),
                            _pad_lanes(vals["ssd_d"].reshape(1, SSD_H)), jnp.zeros((1, D - 3 * 128), _F32)], axis=1)
    add("misc", misc)
    assert all(where[k] == r for k, r in at.items()), (where, at)
    rows.append(jnp.zeros((n_rows - sum(r.shape[0] for r in rows), D), _F32))
    return jnp.concatenate(rows, axis=0)


def _unpack_small(buf, where, my_block, shapes):
    out = {}
    for name in _G_VECS:
        out[name] = buf[where[name]].reshape(shapes[name])
    n_ws = GM_H * CH * CH // D
    out["gm_ws"] = buf[where["gm_ws"]:where["gm_ws"] + n_ws].reshape(shapes["gm_ws"])
    out["gm_bs"] = buf[where["gm_bs"]].reshape(shapes["gm_bs"])
    r = where["ssd_conv_b"]
    out["ssd_conv_b"] = buf[r:r + 2].reshape(1, 2 * D)[:, :CONV_CH].reshape(shapes["ssd_conv_b"])
    r = where["ssd_conv_w"]
    cw = buf[r:r + 2 * CONV_K].reshape(CONV_K, 2 * D)
    out["ssd_conv_w"] = lax.dynamic_slice(cw, (0, my_block * (CONV_CH // N_DEV)),
                                          (CONV_K, CONV_CH // N_DEV)).reshape(shapes["ssd_conv_w"])
    misc = buf[where["misc"]]
    for i, name in enumerate(("ssd_dt_bias", "ssd_a_log", "ssd_d")):
        out[name] = misc[i * 128:i * 128 + SSD_H].reshape(shapes[name])
    return out


_WEIGHTS = ("norm_mix_g", "w_in", "gm_v_norm_g", "gm_ws", "gm_bs", "gm_out_norm_g", "ssd_conv_w", "ssd_conv_b",
            "ssd_dt_bias", "ssd_a_log", "ssd_d", "ssd_norm_g", "w_out", "norm_mlp_g", "w_ff1", "w_ff2", "ple_norm_g",
            "w_ple_gate", "w_ple_proj", "final_norm_g")


def kernel(x, p, norm_mix_g, w_in, gm_v_norm_g, gm_ws, gm_bs, gm_out_norm_g, ssd_conv_w, ssd_conv_b, ssd_dt_bias, ssd_a_log, ssd_d, ssd_norm_g, w_out, norm_mlp_g, w_ff1, w_ff2, ple_norm_g, w_ple_gate, w_ple_proj, final_norm_g, loss_target, m_norm_mix_g, m_w_in, m_gm_v_norm_g, m_gm_ws, m_gm_bs, m_gm_out_norm_g, m_ssd_conv_w, m_ssd_conv_b, m_ssd_dt_bias, m_ssd_a_log, m_ssd_d, m_ssd_norm_g, m_w_out, m_norm_mlp_g, m_w_ff1, m_w_ff2, m_ple_norm_g, m_w_ple_gate, m_w_ple_proj, m_final_norm_g, v_norm_mix_g, v_w_in, v_gm_v_norm_g, v_gm_ws, v_gm_bs, v_gm_out_norm_g, v_ssd_conv_w, v_ssd_conv_b, v_ssd_dt_bias, v_ssd_a_log, v_ssd_d, v_ssd_norm_g, v_w_out, v_norm_mlp_g, v_w_ff1, v_w_ff2, v_ple_norm_g, v_w_ple_gate, v_w_ple_proj, v_final_norm_g):
    args = dict(locals())
    w = {n: args[n] for n in _WEIGHTS}
    m = {n: args["m_" + n] for n in _WEIGHTS}
    v = {n: args["v_" + n] for n in _WEIGHTS}
    shapes = {n: w[n].shape for n in _WEIGHTS}
    my_block = 4 * lax.axis_index("x") + 2 * lax.axis_index("y") + lax.axis_index("c")
    nb, seq_len, _ = x.shape

    local = lambda d, n: d[n][0].T if n == "w_in" else d[n][0]
    shard = {n: local(w, n).astype(_MXU) for n in _BIG_NAMES}
    small = {n: w[n] for n in _WEIGHTS if n not in _BIG_NAMES}
    gx, recv, small_blocks, late8, n_rows, where = _local_step(
        x.reshape(nb * seq_len, D), p.reshape(nb * seq_len, D_PLE), loss_target.reshape(nb * seq_len, D), shard,
        ssd_conv_w[0], small, seq_len)

    big_out = [{}, {}, {}, {}]
    for n in _BIG_NAMES:
        res = _sum_adam("sum_adam_" + n, recv[n], local(w, n), local(m, n), local(v, n))
        for k in range(4):
            big_out[k][n] = (res[k].T if n == "w_in" else res[k]).reshape(shapes[n])

    packs = [_pack_small_params(d, where, n_rows, my_block) for d in (w, m, v)]
    small_res = _small_final(small_blocks, late8, where[_LATE], *packs)
    loss = small_res[0][where["loss"], 0]
    small_out = [_unpack_small(a, where, my_block, shapes) for a in small_res]

    outs = [loss, gx.reshape(x.shape)]
    for k in range(4):
        outs += [big_out[k][n] if n in _BIG_NAMES else small_out[k][n] for n in _WEIGHTS]
    return tuple(outs)
```

```python
import math

import jax
import jax.numpy as jnp
import numpy as np
from jax import lax
from jax.experimental import pallas as pl
from jax.experimental.pallas import tpu as pltpu

_F32 = jnp.float32
_BF16 = jnp.bfloat16
_MXU = jnp.bfloat16

D = 1024
D_PLE = 256
GM_W = 1024
GM_H = 8
CH = 128
SSD_W = 1024
SSD_H = 16
SSD_P = 64
SSD_G = 2
SSD_N = 128
CONV_K = 4
CONV_CH = SSD_W + 2 * SSD_G * SSD_N
D_FF = 4096
D_IN = 2 * GM_W + SSD_W + CONV_CH + SSD_H
D_IN_PAD = 4736
EPS = 1e-6
DT_BLK = (D_IN_PAD - 128) // 128
N_DEV = 8
SHARD_IN = D_IN // N_DEV

LR, B1, B2, ADAM_EPS, WD, STEP = 0.001, 0.9, 0.999, 1e-08, 0.01, 10
_LOG2E = math.log2(math.e)
_LOG2_INV_SQRT_2PI = -0.5 * math.log2(2.0 * math.pi)

_SEQ_PER_STEP = 2
_SEQ_PER_STEP_FWD = 4
_V7X_VMEM_BYTES = 64 * 1024 * 1024
_VMEM_CAP = _V7X_VMEM_BYTES - 8 * 1024 * 1024
_MESH = pl.DeviceIdType.MESH


def _vmem_limit(nbytes):
    return int(min(_VMEM_CAP, max(32 * 1024 * 1024, nbytes * 5 // 4 + (4 << 20))))


def _nbytes(shape, dtype):
    return int(np.prod(shape)) * jnp.dtype(dtype).itemsize


def _mx(v):
    return v.astype(_MXU)


def _dot(a, b):
    return jnp.dot(a, b, preferred_element_type=_F32)


def _dot_nt(a, b):
    return lax.dot_general(a, b, (((1,), (1,)), ((), ())), preferred_element_type=_F32)


def _dot_tn(a, b):
    return lax.dot_general(a, b, (((0,), (0,)), ((), ())), preferred_element_type=_F32)


def _split3(a):
    hi = a.astype(_BF16)
    r = a - hi.astype(_F32)
    mid = r.astype(_BF16)
    lo = (r - mid.astype(_F32)).astype(_BF16)
    return hi, mid, lo


def _xdot(dotfn, a, b01):
    b = b01.astype(_BF16)
    hi, mid, lo = _split3(a)
    return (dotfn(hi, b) + dotfn(mid, b)) + dotfn(lo, b)


def _xdot_left(dotfn, a01, b):
    a = a01.astype(_BF16)
    hi, mid, lo = _split3(b)
    return (dotfn(a, hi) + dotfn(a, mid)) + dotfn(a, lo)


def _sum8(v):
    r, n = v.shape
    return v.reshape(r // 8, 8, n).sum(axis=0)


def _sigmoid(v):
    return 1.0 / (1.0 + jnp.exp(-v))


def _rms(xv, g):
    ms = jnp.mean(xv * xv, axis=-1, keepdims=True)
    return xv * lax.rsqrt(ms + EPS) * g


def _rms_bwd(xv, g, dn):
    r = lax.rsqrt(jnp.mean(xv * xv, axis=-1, keepdims=True) + EPS)
    nh = xv * r
    gy = dn * g
    dx = (gy - nh * jnp.mean(gy * nh, axis=-1, keepdims=True)) * r
    return dx, _sum8(dn * nh)


def _iota2(shape, axis):
    return lax.broadcasted_iota(jnp.int32, shape, axis)


def _norm_cast(name, x, g, tm=512, comm=()):
    t, n = x.shape
    tm = min(tm, t)
    steps = t // tm
    kinds = [kind for kind, _ in comm]
    c_in, c_in_specs, c_out_specs, c_out_shape, c_scratch = _comm_io(comm)

    def body(*refs):
        x_ref, g_ref = refs[0], refs[1]
        o_ref = refs[2 + len(comm)]
        comm_refs = (kinds, refs[2:2 + len(comm)], refs[3 + len(comm):3 + 2 * len(comm)], *refs[3 + 2 * len(comm):])
        if comm:
            pl.when(pl.program_id(0) == 0)(lambda: _comm_start(*comm_refs))

        o_ref[...] = _rms(x_ref[...], g_ref[...]).astype(o_ref.dtype)
        if comm:
            pl.when(pl.program_id(0) == steps - 1)(lambda: _comm_finish(*comm_refs))

    res = pl.pallas_call(
        body, name=name, grid=(steps,),
        in_specs=[pl.BlockSpec((tm, n), lambda i: (i, 0)), pl.BlockSpec((1, n), lambda i: (0, 0))] + c_in_specs,
        out_specs=[pl.BlockSpec((tm, n), lambda i: (i, 0))] + c_out_specs,
        out_shape=[jax.ShapeDtypeStruct((t, n), _MXU)] + c_out_shape, scratch_shapes=c_scratch,
        compiler_params=pltpu.CompilerParams(dimension_semantics=("arbitrary",)),
    )(x, g, *c_in)
    return res[0], res[1:]


def _matmul(name, a, b, mode, tm, tn, tk, epilogue, outs, extras=(), comm=(), b_to_epilogue=False):
    m, k = a.shape[::-1] if mode == "tn" else a.shape
    n = b.shape[0] if mode == "nt" else b.shape[1]
    tm, tn, tk = min(tm, m), min(tn, n), min(tk, k)
    assert m % tm == 0 and n % tn == 0 and k % tk == 0, (name, m, n, k, tm, tn, tk)
    if mode == "nn":
        a_spec = pl.BlockSpec((tm, tk), lambda i, j, kk: (i, kk))
        b_spec = pl.BlockSpec((tk, tn), lambda i, j, kk: (kk, j))
        dotfn = _dot
    elif mode == "nt":
        a_spec = pl.BlockSpec((tm, tk), lambda i, j, kk: (i, kk))
        b_spec = pl.BlockSpec((tn, tk), lambda i, j, kk: (j, kk))
        dotfn = _dot_nt
    else:
        a_spec = pl.BlockSpec((tk, tm), lambda i, j, kk: (kk, i))
        b_spec = pl.BlockSpec((tk, tn), lambda i, j, kk: (kk, j))
        dotfn = _dot_tn
    ni, nj, nk = m // tm, n // tn, k // tk
    if mode != "tn" and nj == 1 and nk == 1 and ni > 2:
        b_spec = pl.BlockSpec(b_spec.block_shape, b_spec.index_map, pipeline_mode=pl.Buffered(1))
    n_ex, n_out, n_comm = len(extras), len(outs), len(comm)
    kinds = [kind for kind, _ in comm]
    c_in, c_in_specs, c_out_specs, c_out_shape, c_scratch = _comm_io(comm)

    in_specs, vmem = [a_spec, b_spec], 2 * (tm * tk * a.dtype.itemsize + tk * tn * b.dtype.itemsize)
    for arr, kind in extras:
        if kind == "tile":
            in_specs.append(pl.BlockSpec((tm, tn), lambda i, j, kk: (i, j)))
            vmem += 2 * _nbytes((tm, tn), arr.dtype)
        elif kind == "rows":
            in_specs.append(pl.BlockSpec((tm, arr.shape[1]), lambda i, j, kk: (i, 0)))
            vmem += 2 * _nbytes((tm, arr.shape[1]), arr.dtype)
        elif kind == "full":
            in_specs.append(pl.BlockSpec(arr.shape, lambda i, j, kk: (0,) * arr.ndim))
            vmem += 2 * _nbytes(arr.shape, arr.dtype)
        else:
            in_specs.append(pl.BlockSpec((1, tn), lambda i, j, kk: (0, j)))
    out_specs, out_shape = [], []
    for kind, dt in outs:
        if kind == "tile":
            out_specs.append(pl.BlockSpec((tm, tn), lambda i, j, kk: (i, j)))
            out_shape.append(jax.ShapeDtypeStruct((m, n), dt))
            vmem += 2 * _nbytes((tm, tn), dt)
        elif kind == "tile_t":
            out_specs.append(pl.BlockSpec((tn, tm), lambda i, j, kk: (j, i)))
            out_shape.append(jax.ShapeDtypeStruct((n, m), dt))
            vmem += 2 * _nbytes((tm, tn), dt)
        else:
            assert nj == 1, "the partial-sum rows are accumulated over consecutive row tiles"
            out_specs.append(pl.BlockSpec((8, tn), lambda i, j, kk: (0, 0)))
            out_shape.append(jax.ShapeDtypeStruct((8, n), dt))
    scratch = [pltpu.VMEM((tm, tn), _F32)] if nk > 1 else []
    vmem += _nbytes((tm, tn), _F32) * 2

    def body(*refs):
        a_ref, b_ref = refs[0], refs[1]
        ex_refs = refs[2:2 + n_ex]
        n_in = 2 + n_ex + n_comm
        out_refs = refs[n_in:n_in + n_out]
        i, j, kk = pl.program_id(0), pl.program_id(1), pl.program_id(2)
        comm_refs = (kinds, refs[2 + n_ex:n_in], refs[n_in + n_out:n_in + n_out + n_comm], *refs[len(refs) - 3:])
        if n_comm:
            pl.when((i == 0) & (j == 0) & (kk == 0))(lambda: _comm_start(*comm_refs))

        b_val = _mx(b_ref[...])
        part = dotfn(_mx(a_ref[...]), b_val)

        def finish(acc):
            vals = epilogue(acc, *([b_val] if b_to_epilogue else []), *[r[...] for r in ex_refs])
            for r, v, (kind, _) in zip(out_refs, vals, outs):
                if kind == "part8":
                    @pl.when(i == 0)
                    def _():
                        r[...] = v

                    @pl.when(i > 0)
                    def _():
                        r[...] += v
                else:
                    r[...] = v.astype(r.dtype)

        if nk == 1:
            finish(part)
        else:
            acc_ref = refs[n_in + n_out + n_comm]

            @pl.when(kk == 0)
            def _():
                acc_ref[...] = part

            @pl.when(kk > 0)
            def _():
                acc_ref[...] += part

            @pl.when(kk == nk - 1)
            def _():
                finish(acc_ref[...])

        if n_comm:
            pl.when((i == ni - 1) & (j == nj - 1) & (kk == nk - 1))(lambda: _comm_finish(*comm_refs))

    carried =n_comm or any(kind == "part8" for kind, _ in outs)
    sem = ("arbitrary",) * 3 if carried else ("parallel", "parallel", "arbitrary")
    res = pl.pallas_call(
        body, name=name, grid=(ni, nj, nk),
        in_specs=in_specs + c_in_specs, out_specs=out_specs + c_out_specs, out_shape=out_shape + c_out_shape,
        scratch_shapes=scratch + c_scratch,
        compiler_params=pltpu.CompilerParams(dimension_semantics=sem, vmem_limit_bytes=_vmem_limit(vmem)),
    )(a, b, *[arr for arr, _ in extras], *c_in)
    return res[:n_out], res[n_out:]


def _shift_down(v, halo8, j):
    if j == 0:
        return v
    r = pltpu.roll(v, j, axis=0)
    hr = pltpu.roll(halo8, j, axis=0)
    top = jnp.where(_iota2(hr.shape, 0) < j, hr, r[:8])
    return jnp.concatenate([top, r[8:]], axis=0)


def _shift_up(v, next8, j):
    if j == 0:
        return v
    rows = v.shape[0]
    r = pltpu.roll(v, rows - j, axis=0)
    nr = pltpu.roll(next8, 8 - j, axis=0)
    bot = jnp.where(_iota2(nr.shape, 0) >= 8 - j, nr, r[rows - 8:])
    return jnp.concatenate([r[:rows - 8], bot], axis=0)


def _silu_grad(sig, silu):
    return sig + silu * (1.0 - sig)


def _gmlp_fwd_vals(pu, pv, gv, ws_ref, bsb_ref, want_bwd):
    tril = _iota2((CH, CH), 0) >= _iota2((CH, CH), 1)
    cdf_u = 0.5 * (1.0 + lax.erf(pu * 0.7071067811865476))
    cdf_v = 0.5 * (1.0 + lax.erf(pv * 0.7071067811865476))
    u = pu * cdf_u
    v = pv * cdf_v
    ys, keep = [], [(cdf_u, cdf_v)] if want_bwd else []
    for h in range(GM_H):
        sl = slice(h * 128, (h + 1) * 128)
        vh = v[:, sl]
        r = lax.rsqrt(jnp.mean(vh * vh, axis=-1, keepdims=True) + EPS)
        vn = vh * r * gv[:, sl]
        wm = _mx(jnp.where(tril, ws_ref[h], 0.0))
        mixed = _dot(wm, _mx(vn)) + bsb_ref[h]
        ys.append(u[:, sl] * mixed)
        if want_bwd:
            keep.append((vh, r, vn, wm, mixed))
    return jnp.concatenate(ys, axis=1), u, keep


def _ssd_conv(xbc, halo8, cw_ref, cb):
    cpre = cb + sum(cw_ref[k:k + 1, :] * _shift_down(xbc, halo8, CONV_K - 1 - k) for k in range(CONV_K))
    sig = _sigmoid(cpre)
    return sig, cpre * sig


def _ssd_decay(dtraw, dtb, alog, e_ref, ltri):
    dtin = dtraw + dtb
    dt = jnp.maximum(dtin, 0.0) + jnp.log(1.0 + jnp.exp(-jnp.abs(dtin)))
    a_neg = -jnp.exp(alog)
    cs = _xdot_left(_dot, ltri, dt * a_neg)
    cs_last = cs[CH - 1:CH, :]
    ecs = jnp.exp(cs)
    dec = jnp.exp(cs_last - cs)
    cdec = jnp.exp(cs_last)
    e = e_ref[...]
    dt_x = _dot(dt.astype(_BF16), e)
    ecs_x = _dot(ecs.astype(_BF16), e)
    dec_x = _dot(dec.astype(_BF16), e)
    cdec_x = _xdot(_dot, jnp.broadcast_to(cdec, (8, 128)), e)[0:1, :]
    return dict(dtin=dtin, dt=dt, a_neg=a_neg, cs=cs, ecs=ecs, dec=dec, cdec=cdec,
                dt_x=dt_x, ecs_x=ecs_x, dec_x=dec_x, cdec_x=cdec_x)


def _head_lm(cs, cst_ref, h, tril):
    seg = jnp.broadcast_to(cs[:, h:h + 1], (CH, CH)) - cst_ref[h:h + 1, :]
    return jnp.exp(jnp.where(tril, seg, -jnp.inf))


def _mixer_fwd(proj, gv, ws, bsb, gout, cw8, cb, dtb, alog, d_x, ng, e_mat, ltri_mat, seq_chunks, comm=()):
    t = proj.shape[0]
    n_seq = t // (seq_chunks * CH)
    sb = math.gcd(_SEQ_PER_STEP_FWD, n_seq)
    n_groups = n_seq // sb
    proj = proj.reshape(n_seq, seq_chunks * CH, proj.shape[1])
    n_comm = len(comm)
    kinds = [kind for kind, _ in comm]
    c_in, c_in_specs, c_out_specs, c_out_shape, c_scratch = _comm_io(comm)

    def body(*refs):
        comm_refs = (kinds, refs[18:18 + n_comm], refs[23 + n_comm:23 + 2 * n_comm], *refs[25 + 2 * n_comm:])
        grp, c = pl.program_id(0), pl.program_id(1)
        if n_comm:
            pl.when((grp == 0) & (c == 0))(lambda: _comm_start(*comm_refs))
            pl.when((grp == n_groups - 1) & (c == seq_chunks - 1))(lambda: _comm_finish(*comm_refs))
        for s in range(sb):
            per_seq = lambda rs: [r.at[s] for r in rs]
            one_chunk(c == 0, *per_seq(refs[:6]), *refs[6:18], *per_seq(refs[18 + n_comm:23 + n_comm]),
                      *per_seq(refs[23 + 2 * n_comm:25 + 2 * n_comm]))

    def one_chunk(first, pu_ref, pv_ref, z_ref, xbc_ref, dt_ref, halo_ref, gv_ref, ws_ref, bsb_ref, gout_ref, cw_ref,
                  cb_ref, dtb_ref, alog_ref, dx_ref, ng_ref, e_ref, ltri_ref, cat_ref, y_ref, st_ref, act_ref, sig_ref,
                  s_ref, cst_ref):
        tril = _iota2((CH, CH), 0) >= _iota2((CH, CH), 1)
        lane = _iota2((CH, 128), 1)

        y_a, _, _ = _gmlp_fwd_vals(pu_ref[...], pv_ref[...], gv_ref[...], ws_ref, bsb_ref, False)
        cat_ref[:, 0:GM_W] = _rms(y_a, gout_ref[...]).astype(cat_ref.dtype)

        @pl.when(first)
        def _():
            s_ref[...] = jnp.zeros_like(s_ref)

        halo8 = jnp.where(first, 0.0, halo_ref[...])
        sig, act = _ssd_conv(xbc_ref[...], halo8, cw_ref, cb_ref[...])
        sig_ref[...] = sig
        act_ref[...] = act
        q = _ssd_decay(dt_ref[...], dtb_ref[...], alog_ref[...], e_ref, ltri_ref[...])
        xv = act[:, 0:SSD_W]
        xdt = xv * q["dt_x"]
        xdt_m = _mx(xdt)
        cs = q["cs"]
        cst_ref[...] = cs.T
        s_prev = s_ref[...]
        st_ref[...] = s_prev
        ys = []
        for g in range(SSD_G):
            bg = _mx(act[:, SSD_W + g * SSD_N:SSD_W + (g + 1) * SSD_N])
            cg = _mx(act[:, SSD_W + SSD_G * SSD_N + g * SSD_N:SSD_W + SSD_G * SSD_N + (g + 1) * SSD_N])
            cbm = _dot_nt(cg, bg)
            gs = slice(g * 512, (g + 1) * 512)
            for pr in range(4):
                ps = slice(g * 512 + pr * 128, g * 512 + (pr + 1) * 128)
                o = []
                for hh in range(2):
                    h = g * 8 + pr * 2 + hh
                    m_h = _mx(cbm * _head_lm(cs, cst_ref, h, tril))
                    o.append(_dot(m_h, xdt_m[:, ps]))
                ys.append(jnp.where(lane < SSD_P, o[0], o[1]))
            sg = s_prev[:, gs]
            yoff = _dot(cg, _mx(sg)) * q["ecs_x"][:, gs]
            ys[-4:] = [ys[-4 + i] + yoff[:, i * 128:(i + 1) * 128] for i in range(4)]
            st_new = _dot_tn(bg, _mx(q["dec_x"][:, gs] * xdt[:, gs]))
            s_ref[:, gs] = sg * q["cdec_x"][:, gs] + st_new
        y = jnp.concatenate(ys, axis=1) + dx_ref[...] * xv
        y_ref[...] = y
        zv = z_ref[...]
        yg = y * (zv * _sigmoid(zv))
        for g in range(SSD_G):
            gs = slice(g * 512, (g + 1) * 512)
            cat_ref[:, GM_W + g * 512:GM_W + (g + 1) * 512] = _rms(yg[:, gs], ng_ref[:, gs]).astype(cat_ref.dtype)

    blk = lambda w, j: pl.BlockSpec((sb, CH, w), lambda g, c: (g, c, j))
    full = lambda arr: pl.BlockSpec(arr.shape, lambda g, c: (0,) * arr.ndim, pipeline_mode=pl.Buffered(1))
    consts = [gv, ws, bsb, gout, cw8, cb, dtb, alog, d_x, ng, e_mat, ltri_mat]
    seq = seq_chunks * CH
    res = pl.pallas_call(
        body, name="mixer_fwd", grid=(n_groups, seq_chunks),
        in_specs=[blk(GM_W, 0), blk(GM_W, 1), blk(SSD_W, 2), blk(CONV_CH, 2), blk(128, DT_BLK),
                  pl.BlockSpec((sb, 8, CONV_CH), lambda g, c: (g, jnp.maximum(c * (CH // 8) - 1, 0), 2))]
        + [full(a) for a in consts] + c_in_specs,
        out_specs=[blk(2 * D, 0), blk(SSD_W, 0), blk(SSD_W, 0), blk(CONV_CH, 0), blk(CONV_CH, 0)] + c_out_specs,
        out_shape=[jax.ShapeDtypeStruct((n_seq, seq, 2 * D), _MXU), jax.ShapeDtypeStruct((n_seq, seq, SSD_W), _F32),
                   jax.ShapeDtypeStruct((n_seq, seq, SSD_W), _F32), jax.ShapeDtypeStruct((n_seq, seq, CONV_CH), _F32),
                   jax.ShapeDtypeStruct((n_seq, seq, CONV_CH), _F32)] + c_out_shape,
        scratch_shapes=[pltpu.VMEM((sb, SSD_N, SSD_W), _F32), pltpu.VMEM((sb, 128, CH), _F32)] + c_scratch,
        compiler_params=pltpu.CompilerParams(dimension_semantics=("arbitrary", "arbitrary"),
                                             vmem_limit_bytes=48 << 20),
    )(proj, proj, proj, proj, proj, proj, *consts, *c_in)
    return [r.reshape(t, r.shape[-1]) for r in res[:5]], res[5:]


def _mixer_bwd(proj, act, sig, dcat, yss, states, gv, ws, bsb, gout, cw8, cb, dtb, alog, d_x, ng, e_mat, et_mat,
               ltri_mat, seq_chunks, comm=()):
    t = proj.shape[0]
    n_seq = t // (seq_chunks * CH)
    sb = math.gcd(_SEQ_PER_STEP, n_seq)
    n_groups = n_seq // sb
    seq = seq_chunks * CH
    proj, act, sig, dcat, yss, states = [a.reshape(n_seq, seq, a.shape[1])
                                         for a in (proj, act, sig, dcat, yss, states)]
    n_comm = len(comm)
    kinds = [kind for kind, _ in comm]
    c_in, c_in_specs, c_out_specs, c_out_shape, c_scratch = _comm_io(comm)

    def body(*refs):
        o0, s0 = 23 + n_comm, 34 + 2 * n_comm
        acc_refs, shared = refs[o0 + 1:o0 + 11], refs[s0 + 4:s0 + 6]
        comm_refs = (kinds, refs[23:23 + n_comm], refs[o0 + 11:o0 + 11 + n_comm], *refs[s0 + 6:])
        grp, i = pl.program_id(0), pl.program_id(1)
        if n_comm:
            pl.when((grp == 0) & (i == 0))(lambda: _comm_start(*comm_refs))

        @pl.when((grp == 0) & (i == 0))
        def _():
            for r in (*acc_refs, *shared, refs[s0 + 2]):
                r[...] = jnp.zeros_like(r)

        @pl.when(i == 0)
        def _():
            refs[s0][...] = jnp.zeros_like(refs[s0])
            refs[s0 + 1][...] = jnp.zeros_like(refs[s0 + 1])

        for s in range(sb):
            per_seq = lambda rs: [r.at[s] for r in rs]
            last_of_all = ((grp == n_groups - 1) & (i == seq_chunks - 1)) if s == sb - 1 else None
            one_chunk(last_of_all, *per_seq(refs[:10]), *refs[10:23], refs[o0].at[s], *acc_refs,
                      *per_seq(refs[s0:s0 + 4]), *shared)
        if n_comm:
            pl.when((grp == n_groups - 1) & (i == seq_chunks - 1))(lambda: _comm_finish(*comm_refs))

    def one_chunk(finalize, pu_ref, pv_ref, z_ref, xbc_ref, dt_ref, act_ref, sig_ref, dcat_ref, y_ref, st_ref,
                  gv_ref, ws_ref, bsb_ref, gout_ref, cw_ref, cb_ref, dtb_ref, alog_ref, dx_ref, ng_ref,
                  e_ref, et_ref, ltri_ref,
                  dproj_ref, dws_ref, dbs_ref, dgv_ref, dgout_ref, dng_ref, dcw_ref, dcb_ref, ddtb_ref, dalog_ref,
                  dd_ref, ds_ref, dnext_ref, dcst_ref, cst_ref, dbacc_ref, ddacc_ref):
        tril = _iota2((CH, CH), 0) >= _iota2((CH, CH), 1)
        lane = _iota2((CH, 128), 1)
        row = _iota2((CH, 128), 0)
        dcat_v = dcat_ref[...].astype(_F32)

        pu, pv = pu_ref[...], pv_ref[...]
        gv_v = gv_ref[...]
        y_a, u, keep = _gmlp_fwd_vals(pu, pv, gv_v, ws_ref, bsb_ref, True)
        dy, dgout8 = _rms_bwd(y_a, gout_ref[...], dcat_v[:, 0:GM_W])
        dgout_ref[...] += dgout8
        dus, dvs, dgvs = [], [], []
        for h in range(GM_H):
            sl = slice(h * 128, (h + 1) * 128)
            vh, r, vn, wm, mixed = keep[h + 1]
            dyh = dy[:, sl]
            dus.append(dyh * mixed)
            dmix = dyh * u[:, sl]
            dmix_m = _mx(dmix)
            dws_ref[h] += jnp.where(tril, _dot_nt(dmix_m, _mx(vn)), 0.0)
            dbacc_ref[h] += dmix
            dvn = _dot_tn(wm, dmix_m)
            gy = dvn * gv_v[:, sl]
            nh = vh * r
            dvs.append((gy - nh * jnp.mean(gy * nh, axis=-1, keepdims=True)) * r)
            dgvs.append(_sum8(dvn * nh))
        dgv_ref[...] += jnp.concatenate(dgvs, axis=1)
        cdf_u, cdf_v = keep[0]
        gelu_grad = lambda pre, cdf: cdf + pre * jnp.exp2(pre * pre * (-0.5 * _LOG2E) + _LOG2_INV_SQRT_2PI)
        dproj_ref[:, 0:GM_W] = (jnp.concatenate(dus, axis=1) * gelu_grad(pu, cdf_u)).astype(dproj_ref.dtype)
        dproj_ref[:, GM_W:2 * GM_W] = (jnp.concatenate(dvs, axis=1) * gelu_grad(pv, cdf_v)).astype(dproj_ref.dtype)

        q = _ssd_decay(dt_ref[...], dtb_ref[...], alog_ref[...], e_ref, ltri_ref[...])
        act = act_ref[...]
        xv = act[:, 0:SSD_W]
        dt_x, ecs_x, dec_x, cdec_x = q["dt_x"], q["ecs_x"], q["dec_x"], q["cdec_x"]
        xdt = xv * dt_x
        xdt_m = _mx(xdt)
        cs = q["cs"]
        cst_ref[...] = cs.T
        s_prev = st_ref[...]
        ds = ds_ref[...]
        yv = y_ref[...]
        zv = z_ref[...]
        sig_z = _sigmoid(zv)
        sz = zv * sig_z
        yg = yv * sz
        dygs, dng8 = [], []
        for g in range(SSD_G):
            gs = slice(g * 512, (g + 1) * 512)
            a_, b_ = _rms_bwd(yg[:, gs], ng_ref[:, gs], dcat_v[:, GM_W + g * 512:GM_W + (g + 1) * 512])
            dygs.append(a_)
            dng8.append(b_)
        dyg = jnp.concatenate(dygs, axis=1)
        dng_ref[...] += jnp.concatenate(dng8, axis=1)
        dyv = dyg * sz
        dproj_ref[:, 2 * GM_W:2 * GM_W + SSD_W] = (dyg * yv * _silu_grad(sig_z, sz)).astype(dproj_ref.dtype)
        ddacc_ref[...] += _sum8(dyv * xv)
        dyv_m = _mx(dyv)

        dxdt_parts, db_parts, dc_parts = [], [], []
        dcs = jnp.zeros((CH, 128), _F32)
        dcs_x_parts, ddec_x_parts, dcl_x_parts = [], [], []
        for g in range(SSD_G):
            gs = slice(g * 512, (g + 1) * 512)
            bg = _mx(act[:, SSD_W + g * SSD_N:SSD_W + (g + 1) * SSD_N])
            cg = _mx(act[:, SSD_W + SSD_G * SSD_N + g * SSD_N:SSD_W + SSD_G * SSD_N + (g + 1) * SSD_N])
            cbm = _dot_nt(cg, bg)
            sg = s_prev[:, gs]
            sg_m = _mx(sg)
            dsg = ds[:, gs]
            dsg_m = _mx(dsg)
            zoff = _dot(cg, sg_m)
            dz_off = dyv[:, gs] * ecs_x[:, gs]
            dz_off_m = _mx(dz_off)
            dcs_x_parts.append(dyv[:, gs] * zoff * ecs_x[:, gs])
            dcg = _dot_nt(dz_off_m, sg_m)
            dsprev = _dot_tn(cg, dz_off_m)
            w_st = dec_x[:, gs] * xdt[:, gs]
            dw_st = _dot(bg, dsg_m)
            dbg = _dot_nt(_mx(w_st), dsg_m)
            dxdt_g = dec_x[:, gs] * dw_st
            ddec_x_parts.append(dw_st * xdt[:, gs])
            dsprev = dsprev + cdec_x[:, gs] * dsg
            dcl_x_parts.append(jnp.sum(dsg * sg, axis=0, keepdims=True) * cdec_x[:, gs])
            ds_ref[:, gs] = dsprev
            dcb = jnp.zeros((CH, CH), _F32)
            dxdt_pairs = []
            for pr in range(4):
                ps = slice(g * 512 + pr * 128, g * 512 + (pr + 1) * 128)
                acc_pair = None
                for hh in range(2):
                    h = g * 8 + pr * 2 + hh
                    in_head = (lane < SSD_P) if hh == 0 else (lane >= SSD_P)
                    lm = _head_lm(cs, cst_ref, h, tril)
                    m_h = cbm * lm
                    m_hm = _mx(m_h)
                    dyh_m = _mx(jnp.where(in_head, dyv[:, ps], 0.0))
                    dm = _dot_nt(dyh_m, xdt_m[:, ps])
                    dcb = dcb + dm * lm
                    qm = dm * m_h
                    dcs = dcs + jnp.where(lane == h, jnp.sum(qm, axis=1, keepdims=True), 0.0)
                    dcst_ref[h:h + 1, :] = jnp.sum(qm, axis=0, keepdims=True)
                    contrib = jnp.where(in_head, _dot_tn(m_hm, dyv_m[:, ps]), 0.0)
                    acc_pair = contrib if acc_pair is None else acc_pair + contrib
                dxdt_pairs.append(acc_pair)
            dxdt_parts.append(dxdt_g + jnp.concatenate(dxdt_pairs, axis=1))
            dcb_m = _mx(dcb)
            dc_parts.append(dcg + _dot(dcb_m, bg))
            db_parts.append(dbg + _dot_tn(dcb_m, cg))
        dxdt = jnp.concatenate(dxdt_parts, axis=1)
        dxv = dx_ref[...] * dyv + dxdt * dt_x
        et = et_ref[...]
        head_sum = lambda v: _dot(v.astype(_BF16), et)
        ddt = head_sum(dxdt * xv)
        dcs = dcs - dcst_ref[...].T + head_sum(jnp.concatenate(dcs_x_parts, axis=1))
        ddec = head_sum(jnp.concatenate(ddec_x_parts, axis=1)) * q["dec"]
        dcs = dcs - ddec
        dcl = jnp.sum(ddec, axis=0, keepdims=True) + _xdot(
            _dot, jnp.broadcast_to(jnp.concatenate(dcl_x_parts, axis=1), (8, SSD_W)), et)[0:1, :]
        dcs = jnp.where(row == CH - 1, dcs + dcl, dcs)
        da = _xdot_left(_dot_tn, ltri_ref[...], dcs)
        ddt = ddt + da * q["a_neg"]
        dalog_ref[...] += _sum8(da * q["dt"] * q["a_neg"])
        ddtraw = jnp.where(lane < SSD_H, ddt * _sigmoid(q["dtin"]), 0.0)
        ddtb_ref[...] += _sum8(ddtraw)
        dproj_ref[:, D_IN_PAD - 128:D_IN_PAD] = ddtraw.astype(dproj_ref.dtype)
        dcpre = jnp.concatenate([dxv] + db_parts + dc_parts, axis=1) * _silu_grad(sig_ref[...], act)
        dcb_ref[...] += _sum8(dcpre)
        next8 = dnext_ref[...]
        ups = [_shift_up(dcpre, next8, j) for j in range(CONV_K)]
        xbc = xbc_ref[...]
        for k in range(CONV_K):
            dcw_ref[k:k + 1, :] += jnp.sum(xbc * ups[CONV_K - 1 - k], axis=0, keepdims=True)
        dxbc = sum(cw_ref[k:k + 1, :] * ups[CONV_K - 1 - k] for k in range(CONV_K))
        dproj_ref[:, 2 * GM_W + SSD_W:2 * GM_W + SSD_W + CONV_CH] = dxbc.astype(dproj_ref.dtype)
        dnext_ref[...] = dcpre[0:8, :]

        if finalize is not None:
            @pl.when(finalize)
            def _():
                for h in range(GM_H):
                    dbs_ref[h:h + 1, :] = _xdot_left(_dot_nt, jnp.ones((8, 128), _BF16), dbacc_ref[h])[0:1, :]
                dd_ref[...] = _xdot(_dot, ddacc_ref[...], et)

    rblk = lambda w, j: pl.BlockSpec((sb, CH, w), lambda g, i: (g, seq_chunks - 1 - i, j))
    full = lambda arr: pl.BlockSpec(arr.shape, lambda g, i: (0,) * arr.ndim)
    acc = lambda shape: pl.BlockSpec(shape, lambda g, i: (0,) * len(shape))
    consts = [gv, ws, bsb, gout, cw8, cb, dtb, alog, d_x, ng, e_mat, et_mat, ltri_mat]
    acc_shapes = [(GM_H, CH, CH), (8, 128), (8, GM_W), (8, GM_W), (8, SSD_W), (8, CONV_CH), (8, CONV_CH), (8, 128),
                  (8, 128), (8, 128)]
    res = pl.pallas_call(
        body, name="mixer_bwd", grid=(n_groups, seq_chunks),
        in_specs=[rblk(GM_W, 0), rblk(GM_W, 1), rblk(SSD_W, 2), rblk(CONV_CH, 2), rblk(128, DT_BLK),
                  rblk(CONV_CH, 0), rblk(CONV_CH, 0),
                  rblk(2 * D, 0), rblk(SSD_W, 0), rblk(SSD_W, 0)] + [full(a) for a in consts] + c_in_specs,
        out_specs=[rblk(D_IN_PAD, 0)] + [acc(s) for s in acc_shapes] + c_out_specs,
        out_shape=[jax.ShapeDtypeStruct((n_seq, seq, D_IN_PAD), _MXU)]
        + [jax.ShapeDtypeStruct(s, _F32) for s in acc_shapes] + c_out_shape,
        scratch_shapes=[pltpu.VMEM((sb, SSD_N, SSD_W), _F32), pltpu.VMEM((sb, 8, CONV_CH), _F32),
                        pltpu.VMEM((sb, 128, CH), _F32), pltpu.VMEM((sb, 128, CH), _F32),
                        pltpu.VMEM((GM_H, CH, 128), _F32), pltpu.VMEM((8, SSD_W), _F32)] + c_scratch,
        compiler_params=pltpu.CompilerParams(dimension_semantics=("arbitrary", "arbitrary"),
                                             vmem_limit_bytes=48 << 20),
    )(proj, proj, proj, proj, proj, act, sig, dcat, yss, states, *consts, *c_in)
    return [res[0].reshape(t, D_IN_PAD)] + list(res[1:11]), res[11:]


def _peers():
    x, y, c = lax.axis_index("x"), lax.axis_index("y"), lax.axis_index("c")
    out = []
    for k in range(1, N_DEV):
        fx, fy, fc = (k >> 2) & 1, (k >> 1) & 1, k & 1
        px, py, pc = (x + fx) % 2, (y + fy) % 2, (c + fc) % 2
        out.append((k - 1, (px, py, pc), 4 * px + 2 * py + pc))
    return out, 4 * x + 2 * y + c


def _comm_io(comm):
    any_spec = pl.BlockSpec(memory_space=pl.ANY)
    n = len(comm)
    out_shape = []
    for (kind, axis), src in comm:
        shp = list(src.shape)
        if kind in ("gather", "gather2"):
            shp[axis] *= N_DEV
        else:
            shp[axis] //= N_DEV
            shp = [N_DEV] + shp
        out_shape.append(jax.ShapeDtypeStruct(tuple(shp), src.dtype))
    scratch = [pltpu.SemaphoreType.DMA((n * (N_DEV - 1),)), pltpu.SemaphoreType.DMA((n * (N_DEV - 1),)),
               pltpu.SemaphoreType.DMA((n,))] if n else []
    return [src for _, src in comm], [any_spec] * n, [any_spec] * n, out_shape, scratch


def _window(ref, axis, idx, size):
    start = pl.multiple_of(idx * size, size)
    return ref.at[tuple(pl.ds(start, size) if a == axis else slice(None) for a in range(len(ref.shape)))]


def _comm_plans(kinds, src_refs, dst_refs, send_sems, recv_sems, local_sems):
    x, y, c = lax.axis_index("x"), lax.axis_index("y"), lax.axis_index("c")
    peers, me = _peers()
    plans = []
    for s, ((kind, axis), src, dst) in enumerate(zip(kinds, src_refs, dst_refs)):
        sems = lambda k: dict(send_sem=send_sems.at[s * (N_DEV - 1) + k], recv_sem=recv_sems.at[s * (N_DEV - 1) + k])
        remote = lambda src_ref, dst_ref, k, pid: pltpu.make_async_remote_copy(
            src_ref=src_ref, dst_ref=dst_ref, device_id=pid, device_id_type=_MESH, **sems(k))
        if kind == "gather2":
            size = src.shape[axis]
            win = lambda idx: _window(dst, axis, idx, size)
            sib, sib_idx = (x, y, 1 - c), 4 * x + 2 * y + (1 - c)
            local = pltpu.make_async_copy(src, win(me), local_sems.at[s])
            to_sib = remote(src, win(me), 0, sib)
            starts, forwards = [local, to_sib], []
            waits = [(local, "local"), (to_sib, "send"), (remote(src, win(sib_idx), 0, sib), "recv")]
            for j, (fx, fy) in enumerate(((1, 0), (0, 1), (1, 1))):
                px, py = (x + fx) % 2, (y + fy) % 2
                same, other = 4 * px + 2 * py + c, 4 * px + 2 * py + (1 - c)
                out = remote(src, win(me), 1 + j, (px, py, c))
                starts.append(out)
                passed = remote(win(same), win(same), 4 + j, sib)
                forwards.append((remote(src, win(same), 1 + j, (px, py, c)), passed))
                waits += [(out, "send"), (passed, "send"), (remote(win(other), win(other), 4 + j, sib), "recv")]
            plans.append((starts, forwards, waits))
            continue
        if kind == "gather":
            size = src.shape[axis]
            src_for = lambda pidx: src
            dst_mine = _window(dst, axis, me, size)
        else:
            size = src.shape[axis] // N_DEV
            src_for = lambda pidx: _window(src, axis, pidx, size)
            dst_mine = dst.at[me]
        local = pltpu.make_async_copy(src_for(me), dst_mine, local_sems.at[s])
        remotes = [remote(src_for(pidx), dst_mine, k, pid) for k, pid, pidx in peers]
        plans.append(([local] + remotes, [], [(local, "local")] + [(cp, "both") for cp in remotes]))
    return plans


def _comm_start(*refs):
    for starts, _, _ in _comm_plans(*refs):
        for cp in starts:
            cp.start()


def _comm_finish(*refs):
    for _, forwards, waits in _comm_plans(*refs):
        for arrival, cp in forwards:
            arrival.wait_recv()
            cp.start()
        for cp, what in waits:
            if what == "send":
                cp.wait_send()
            elif what == "recv":
                cp.wait_recv()
            else:
                cp.wait()


def _adam_vals(w, g, m, v):
    m = B1 * m + (1.0 - B1) * g
    v = B2 * v + (1.0 - B2) * (g * g)
    m_hat = m / (1.0 - B1 ** STEP)
    v_hat = v / (1.0 - B2 ** STEP)
    delta = -LR * (m_hat / (jnp.sqrt(v_hat) + ADAM_EPS) + WD * w)
    return delta, m, v


def _sum_adam(name, recv, w, m, v, tile=256):
    _, r, wd = recv.shape
    if r % min(tile, r) == 0:
        tr, tc = min(tile, r), wd
    else:
        tr, tc = r, tile
        assert wd % tc == 0, (name, r, wd)

    def body(recv_ref, w_ref, m_ref, v_ref, g_out, d_out, m_out, v_out):
        g = recv_ref[0].astype(_F32)
        for s in range(1, N_DEV):
            g = g + recv_ref[s].astype(_F32)
        d_, m_, v_ = _adam_vals(w_ref[...], g, m_ref[...], v_ref[...])
        g_out[...] = g
        d_out[...] = d_
        m_out[...] = m_
        v_out[...] = v_

    spec = pl.BlockSpec((tr, tc), lambda i, j: (i, j))
    return pl.pallas_call(
        body, name=name, grid=(r // tr, wd // tc),
        in_specs=[pl.BlockSpec((N_DEV, tr, tc), lambda i, j: (0, i, j)), spec, spec, spec],
        out_specs=[spec] * 4, out_shape=[jax.ShapeDtypeStruct((r, wd), _F32)] * 4,
        compiler_params=pltpu.CompilerParams(dimension_semantics=("parallel", "parallel"),
                                             vmem_limit_bytes=48 << 20),
    )(recv, w, m, v)


def _small_local(parts, segments, n_rows):
    def body(parts_ref, loc_ref):
        loc_ref[...] = jnp.zeros_like(loc_ref)
        for out_row, n_out, in_row, n_in, kind in segments:
            if kind == "copy":
                loc_ref[out_row:out_row + n_out, :] = parts_ref[in_row:in_row + n_in, :]
            else:
                s = jnp.sum(parts_ref[in_row:in_row + n_in, :], axis=0, keepdims=True)
                if kind == "loss":
                    s = jnp.broadcast_to(jnp.sum(s, axis=1, keepdims=True) * (0.5 / D), (1, D))
                loc_ref[out_row:out_row + 1, :] = s

    vm = pl.BlockSpec(memory_space=pltpu.VMEM)
    return pl.pallas_call(body, name="small_local", in_specs=[vm], out_specs=vm,
                          out_shape=jax.ShapeDtypeStruct((n_rows, D), _F32))(parts)


def _small_final(blocks, late8, late_row, w, m, v):
    n_rows = w.shape[0]

    def body(blocks_ref, late_ref, w_ref, m_ref, v_ref, g_out, d_out, m_out, v_out, loc_ref, recv_ref, send_sems,
             recv_sems):
        peers, me = _peers()
        loc_ref[...] = jnp.broadcast_to(jnp.sum(late_ref[...], axis=0, keepdims=True), (8, D))
        recv_ref[me] = loc_ref[...]
        copies = [pltpu.make_async_remote_copy(src_ref=loc_ref, dst_ref=recv_ref.at[me], send_sem=send_sems.at[k],
                                               recv_sem=recv_sems.at[k], device_id=pid, device_id_type=_MESH)
                  for k, pid, _ in peers]
        for cp in copies:
            cp.start()
        g = blocks_ref[0:n_rows, :]
        for s in range(1, N_DEV):
            g = g + blocks_ref[s * n_rows:(s + 1) * n_rows, :]
        for cp in copies:
            cp.wait()
        late = recv_ref[0]
        for s in range(1, N_DEV):
            late = late + recv_ref[s]
        g = jnp.where(_iota2((n_rows, D), 0) == late_row, jnp.broadcast_to(late[0:1, :], (n_rows, D)), g)
        d_, m_, v_ = _adam_vals(w_ref[...], g, m_ref[...], v_ref[...])
        g_out[...] = g
        d_out[...] = d_
        m_out[...] = m_
        v_out[...] = v_

    vm = pl.BlockSpec(memory_space=pltpu.VMEM)
    return pl.pallas_call(
        body, name="small_final", in_specs=[vm] * 5, out_specs=[vm] * 4,
        out_shape=[jax.ShapeDtypeStruct((n_rows, D), _F32)] * 4,
        scratch_shapes=[pltpu.VMEM((8, D), _F32), pltpu.VMEM((N_DEV, 8, D), _F32),
                        pltpu.SemaphoreType.DMA((N_DEV - 1,)), pltpu.SemaphoreType.DMA((N_DEV - 1,))],
        compiler_params=pltpu.CompilerParams(vmem_limit_bytes=48 << 20),
    )(blocks, late8, w, m, v)


_BIG_NAMES =("w_in", "w_out", "w_ff1", "w_ff2", "w_ple_gate", "w_ple_proj")

_G_VECS = ("norm_mix_g", "gm_v_norm_g", "gm_out_norm_g", "ssd_norm_g", "norm_mlp_g", "ple_norm_g", "final_norm_g")
_LATE = _G_VECS[0]


def _const_mats():
    h = np.arange(128)[:, None]
    ch = np.arange(SSD_W)[None, :]
    e = (ch // SSD_P == h).astype(np.float32)
    ltri = (np.arange(CH)[:, None] >= np.arange(CH)[None, :]).astype(np.float32)
    return jnp.asarray(e, _BF16), jnp.asarray(e.T, _BF16), jnp.asarray(ltri, _BF16)


def _pad_lanes(v, n=128):
    return jnp.pad(v, ((0, 0), (0, n - v.shape[1])))


def _local_step(x, p, tgt, shard, conv_w_shard, small, seq_len):
    seq_chunks = seq_len // CH
    e_mat, et_mat, ltri_mat = _const_mats()
    g_mix, g_mlp, g_ple = small["norm_mix_g"], small["norm_mlp_g"], small["ple_norm_g"]
    g_fin = small["final_norm_g"].reshape(1, D)
    gv, gout, ng = small["gm_v_norm_g"], small["gm_out_norm_g"], small["ssd_norm_g"]
    ws = small["gm_ws"][0]
    bsb = jnp.broadcast_to(small["gm_bs"][0][:, :, None], (GM_H, CH, 128))
    cb = small["ssd_conv_b"]
    dtb, alog = _pad_lanes(small["ssd_dt_bias"]), _pad_lanes(small["ssd_a_log"])
    d_x = jnp.repeat(small["ssd_d"], SSD_P, axis=1)

    first = lambda acc: (acc,)
    rows, cols = ("gather2", 0), ("gather2", 1)
    n1, (g_win, g_cw) = _norm_cast("norm_mix", x, g_mix,
                                   comm=[(rows, shard["w_in"][None]), (("gather", 0), conv_w_shard[None])])
    w_in_t = jnp.pad(g_win.reshape(D_IN, D), ((0, D_IN_PAD - D_IN), (0, 0)))
    cw8 = jnp.pad(g_cw.transpose(1, 0, 2).reshape(CONV_K, CONV_CH), ((0, 8 - CONV_K), (0, 0)))
    mix_consts = (gv, ws, bsb, gout, cw8, cb, dtb, alog, d_x, ng)
    (proj,), (w_out, w1) = _matmul("proj_in", n1, w_in_t, "nt", 512, D_IN_PAD, D, first, [("tile", _F32)],
                                   comm=[(rows, shard["w_out"]), (cols, shard["w_ff1"])])
    (cat, yss, states, conv_act, conv_sig), (w2, wg, wp) = _mixer_fwd(
        proj, *mix_consts, e_mat, ltri_mat, seq_chunks,
        comm=[(rows, shard["w_ff2"]), (rows, shard["w_ple_gate"]), (cols, shard["w_ple_proj"])])

    def epi_res_norm(acc, res, g):
        hv = acc + res
        return hv, _rms(hv, g)

    (h1, n2), _ = _matmul("proj_out", cat, w_out, "nn", 1024, D, 2 * D, epi_res_norm,
                          [("tile", _F32), ("tile", _MXU)], extras=[(x, "tile"), (g_mlp, "row")])

    def epi_relu2(acc):
        hid = jnp.maximum(acc, 0.0)
        return hid, hid * hid

    (hid, hid2), _ = _matmul("ff1", n2, w1, "nn", 1024, 1024, D, epi_relu2, [("tile", _MXU), ("tile", _MXU)])
    (h2, n3), _ = _matmul("ff2", hid2, w2, "nn", 512, D, D_FF, epi_res_norm, [("tile", _F32), ("tile", _MXU)],
                          extras=[(h1, "tile"), (g_ple, "row")])

    def epi_norm_bwd(acc, up, hv, g):
        dx, dg8 = _rms_bwd(hv, g, acc)
        dh = up + dx
        return dh, dh, dg8

    def epi_head(acc, wg_v, p_rows, wp_v, h2v, tg, gf, gp_):
        ppv = _dot(_mx(p_rows), wp_v)
        gate = _sigmoid(acc)
        gp = gate * ppv
        h3 = h2v + gp
        r = lax.rsqrt(jnp.mean(h3 * h3, axis=-1, keepdims=True) + EPS)
        nh = h3 * r
        err = nh * gf - tg
        gy = err * (gf * (1.0 / D))
        dh3 = (gy - nh * jnp.mean(gy * nh, axis=-1, keepdims=True)) * r
        dpp = dh3 * gate
        da3 = dpp * (ppv - gp)
        dh2, dh2_again, dgple8 = epi_norm_bwd(_dot_nt(_mx(da3), wg_v), dh3, h2v, gp_)
        return da3, dpp, dh2, dh2_again, _sum8(err * err), _sum8(err * nh) * (1.0 / D), dgple8

    (da3, dpp, dh2, dh2b, lossp, dgfin, dgple), _ = _matmul(
        "ple_gate_loss_bwd", n3, wg, "nn", 512, D, D, epi_head,
        [("tile", _MXU), ("tile", _MXU), ("tile", _F32), ("tile", _MXU), ("part8", _F32), ("part8", _F32),
         ("part8", _F32)],
        extras=[(p, "rows"), (wp, "full"), (h2, "tile"), (tgt, "tile"), (g_fin, "row"), (g_ple, "row")],
        b_to_epilogue=True)

    s_rows, s_cols = ("scatter", 0), ("scatter", 1)
    (dwp,), _ = _matmul("d_w_ple_proj", p, dpp, "tn", D_PLE, D, 2048, first, [("tile", _BF16)])
    (dwg,), _ = _matmul("d_w_ple_gate", n3, da3, "tn", D, D, 2048, first, [("tile", _BF16)])
    (dw2,), (r_wp, r_wg) = _matmul("d_w_ff2", hid2, dh2b, "tn", 1024, D, 4096, first, [("tile", _BF16)],
                                   comm=[(s_cols, dwp), (s_rows, dwg)])
    (da1,), _ = _matmul("d_ff_hidden", dh2b, w2, "nt", 512, 2048, D,
                        lambda acc, hv: (acc * 2.0 * hv.astype(_F32),), [("tile", _MXU)], extras=[(hid, "tile")])
    (dw1,), _ = _matmul("d_w_ff1", n2, da1, "tn", 1024, 1024, 4096, first, [("tile", _BF16)])
    (dh1, dh1b, dgmlp), _ = _matmul(
        "d_h1", da1, w1, "nt", 256, D, D_FF, epi_norm_bwd, [("tile", _F32), ("tile", _MXU), ("part8", _F32)],
        extras=[(dh2, "tile"), (h1, "tile"), (g_mlp, "row")])
    (dwout,), _ = _matmul("d_w_out", cat, dh1b, "tn", 1024, D, 2048, first, [("tile", _BF16)])
    (dcat,), _ = _matmul("d_cat", dh1b, w_out, "nt", 1024, 1024, D, first, [("tile", _MXU)])
    (dproj, dws, dbs, dgv, dgout, dng, dcw, dcb, ddtb, dalog, dd), (r_w2, r_w1, r_wout) = _mixer_bwd(
        proj, conv_act, conv_sig, dcat, yss, states, *mix_consts, e_mat, et_mat, ltri_mat, seq_chunks,
        comm=[(s_rows, dw2), (s_cols, dw1), (s_rows, dwout)])
    pieces = dict(gm_v_norm_g=dgv, gm_out_norm_g=dgout, ssd_norm_g=dng, norm_mlp_g=dgmlp, ple_norm_g=dgple,
                  final_norm_g=dgfin, gm_ws=dws, gm_bs=dbs, ssd_conv_w=dcw, ssd_conv_b=dcb, ssd_dt_bias=ddtb,
                  ssd_a_log=dalog, ssd_d=dd, loss=lossp)
    parts, segments, n_rows, where = _small_layout(pieces)
    small_block = _small_local(parts, segments, n_rows)
    (dwin_t,), (small_blocks,) = _matmul("d_w_in", n1, dproj, "tn", 512, D_IN_PAD, 1024, lambda acc: (acc.T,),
                                         [("tile_t", _BF16)], comm=[(("gather", 0), small_block)])
    dwin_blocks = dwin_t[:D_IN].reshape(N_DEV, SHARD_IN, D)
    (gx, dgmix), (r_win,) = _matmul(
        "d_x", dproj, w_in_t, "nn", 256, D, D_IN_PAD, lambda *a: epi_norm_bwd(*a)[1:],
        [("tile", _F32), ("part8", _F32)], extras=[(dh1, "tile"), (x, "tile"), (g_mix, "row")],
        comm=[(s_rows, dwin_blocks)])
    r_win = r_win.reshape(N_DEV, SHARD_IN, D)

    big = dict(w_in=r_win, w_out=r_wout, w_ff1=r_w1, w_ff2=r_w2, w_ple_gate=r_wg, w_ple_proj=r_wp)
    return gx, big, small_blocks, dgmix, n_rows, where


def _small_layout(pieces):
    rows, segments = [], []
    in_row, out_row = 0, 0

    def add(arr, kind, n_out, new_group=True):
        nonlocal in_row, out_row
        if new_group:
            out_row = -(-out_row // 8) * 8
        rows.append(arr)
        segments.append((out_row, n_out, in_row, arr.shape[0], kind))
        start = out_row
        in_row += arr.shape[0]
        out_row += n_out
        return start

    where = {_LATE: 0}
    out_row = 1
    for name in _G_VECS[1:]:
        where[name] = add(pieces[name], "sum", 1)
    where["gm_ws"] = add(pieces["gm_ws"].reshape(GM_H * CH * CH // D, D), "copy", GM_H * CH * CH // D)
    where["gm_bs"] = add(jnp.pad(pieces["gm_bs"].reshape(1, D), ((0, 7), (0, 0))), "sum", 1)
    cb = jnp.pad(pieces["ssd_conv_b"], ((0, 0), (0, 2 * D - CONV_CH)))
    where["ssd_conv_b"] = add(cb[:, :D], "sum", 1)
    add(cb[:, D:], "sum", 1, new_group=False)
    cw = jnp.pad(pieces["ssd_conv_w"][:CONV_K], ((0, 0), (0, 2 * D - CONV_CH)))
    where["ssd_conv_w"] = add(cw.reshape(2 * CONV_K, D), "copy", 2 * CONV_K)
    misc = jnp.concatenate([pieces["ssd_dt_bias"], pieces["ssd_a_log"], pieces["ssd_d"],
                            jnp.zeros((8, D - 3 * 128), _F32)], axis=1)
    where["misc"] = add(misc, "sum", 1)
    where["loss"] = add(pieces["loss"], "loss", 1)
    n_rows = -(-out_row // 8) * 8
    return jnp.concatenate(rows, axis=0), tuple(segments), n_rows, where


def _pack_small_params(vals, where, n_rows, my_block):
    rows, at = [], {}

    def add(name, arr):
        at[name] = sum(r.shape[0] for r in rows)
        rows.append(jnp.pad(arr, ((0, -arr.shape[0] % 8), (0, 0))))

    for name in _G_VECS:
        add(name, vals[name].reshape(1, D))
    add("gm_ws", vals["gm_ws"].reshape(GM_H * CH * CH // D, D))
    add("gm_bs", vals["gm_bs"].reshape(1, D))
    cb = jnp.pad(vals["ssd_conv_b"].reshape(1, CONV_CH), ((0, 0), (0, 2 * D - CONV_CH)))
    add("ssd_conv_b", cb.reshape(2, D))
    cw = lax.dynamic_update_slice(jnp.zeros((CONV_K, 2 * D), _F32), vals["ssd_conv_w"].reshape(CONV_K, -1),
                                  (0, my_block * (CONV_CH // N_DEV)))
    add("ssd_conv_w", cw.reshape(2 * CONV_K, D))
    misc = jnp.concatenate([_pad_lanes(vals["ssd_dt_bias"].reshape(1, SSD_H)),
                            _pad_lanes(vals["ssd_a_log"].reshape(1, SSD_H)),
                            _pad_lanes(vals["ssd_d"].reshape(1, SSD_H)), jnp.zeros((1, D - 3 * 128), _F32)], axis=1)
    add("misc", misc)
    assert all(where[k] == r for k, r in at.items()), (where, at)
    rows.append(jnp.zeros((n_rows - sum(r.shape[0] for r in rows), D), _F32))
    return jnp.concatenate(rows, axis=0)


def _unpack_small(buf, where, my_block, shapes):
    out = {}
    for name in _G_VECS:
        out[name] = buf[where[name]].reshape(shapes[name])
    n_ws = GM_H * CH * CH // D
    out["gm_ws"] = buf[where["gm_ws"]:where["gm_ws"] + n_ws].reshape(shapes["gm_ws"])
    out["gm_bs"] = buf[where["gm_bs"]].reshape(shapes["gm_bs"])
    r = where["ssd_conv_b"]
    out["ssd_conv_b"] = buf[r:r + 2].reshape(1, 2 * D)[:, :CONV_CH].reshape(shapes["ssd_conv_b"])
    r = where["ssd_conv_w"]
    cw = buf[r:r + 2 * CONV_K].reshape(CONV_K, 2 * D)
    out["ssd_conv_w"] = lax.dynamic_slice(cw, (0, my_block * (CONV_CH // N_DEV)),
                                          (CONV_K, CONV_CH // N_DEV)).reshape(shapes["ssd_conv_w"])
    misc = buf[where["misc"]]
    for i, name in enumerate(("ssd_dt_bias", "ssd_a_log", "ssd_d")):
        out[name] = misc[i * 128:i * 128 + SSD_H].reshape(shapes[name])
    return out


_WEIGHTS = ("norm_mix_g", "w_in", "gm_v_norm_g", "gm_ws", "gm_bs", "gm_out_norm_g", "ssd_conv_w", "ssd_conv_b",
            "ssd_dt_bias", "ssd_a_log", "ssd_d", "ssd_norm_g", "w_out", "norm_mlp_g", "w_ff1", "w_ff2", "ple_norm_g",
            "w_ple_gate", "w_ple_proj", "final_norm_g")


def kernel(x, p, norm_mix_g, w_in, gm_v_norm_g, gm_ws, gm_bs, gm_out_norm_g, ssd_conv_w, ssd_conv_b, ssd_dt_bias, ssd_a_log, ssd_d, ssd_norm_g, w_out, norm_mlp_g, w_ff1, w_ff2, ple_norm_g, w_ple_gate, w_ple_proj, final_norm_g, loss_target, m_norm_mix_g, m_w_in, m_gm_v_norm_g, m_gm_ws, m_gm_bs, m_gm_out_norm_g, m_ssd_conv_w, m_ssd_conv_b, m_ssd_dt_bias, m_ssd_a_log, m_ssd_d, m_ssd_norm_g, m_w_out, m_norm_mlp_g, m_w_ff1, m_w_ff2, m_ple_norm_g, m_w_ple_gate, m_w_ple_proj, m_final_norm_g, v_norm_mix_g, v_w_in, v_gm_v_norm_g, v_gm_ws, v_gm_bs, v_gm_out_norm_g, v_ssd_conv_w, v_ssd_conv_b, v_ssd_dt_bias, v_ssd_a_log, v_ssd_d, v_ssd_norm_g, v_w_out, v_norm_mlp_g, v_w_ff1, v_w_ff2, v_ple_norm_g, v_w_ple_gate, v_w_ple_proj, v_final_norm_g):
    args = dict(locals())
    w = {n: args[n] for n in _WEIGHTS}
    m = {n: args["m_" + n] for n in _WEIGHTS}
    v = {n: args["v_" + n] for n in _WEIGHTS}
    shapes = {n: w[n].shape for n in _WEIGHTS}
    my_block = 4 * lax.axis_index("x") + 2 * lax.axis_index("y") + lax.axis_index("c")
    nb, seq_len, _ = x.shape

    local = lambda d, n: d[n][0].T if n == "w_in" else d[n][0]
    shard = {n: local(w, n).astype(_MXU) for n in _BIG_NAMES}
    small = {n: w[n] for n in _WEIGHTS if n not in _BIG_NAMES}
    gx, recv, small_blocks, late8, n_rows, where = _local_step(
        x.reshape(nb * seq_len, D), p.reshape(nb * seq_len, D_PLE), loss_target.reshape(nb * seq_len, D), shard,
        ssd_conv_w[0], small, seq_len)

    big_out = [{}, {}, {}, {}]
    for n in _BIG_NAMES:
        res = _sum_adam("sum_adam_" + n, recv[n], local(w, n), local(m, n), local(v, n))
        for k in range(4):
            big_out[k][n] = (res[k].T if n == "w_in" else res[k]).reshape(shapes[n])

    packs = [_pack_small_params(d, where, n_rows, my_block) for d in (w, m, v)]
    small_res = _small_final(small_blocks, late8, where[_LATE], *packs)
    loss = small_res[0][where["loss"], 0]
    small_out = [_unpack_small(a, where, my_block, shapes) for a in small_res]

    outs = [loss, gx.reshape(x.shape)]
    for k in range(4):
        outs += [big_out[k][n] if n in _BIG_NAMES else small_out[k][n] for n in _WEIGHTS]
    return tuple(outs)
```
